```python
import math
import jax, jax.numpy as jnp
from jax import lax
import numpy as np

D_MODEL = 1024
BATCH = 2
SEQ = 8192
DEPTH = 1

M_HEADS = 4
M_HEAD_DIM = 256
M_WIDTH = M_HEADS * M_HEAD_DIM
M_CHUNK = 64
CONV_WIDTH = 4
N_HEADS = 8
N_KV_GROUPS = 2
N_HPG = N_HEADS // N_KV_GROUPS
N_HEAD_DIM = 64
N_WIDTH = N_HEADS * N_HEAD_DIM
N_KV_WIDTH = N_KV_GROUPS * N_HEAD_DIM
CMP_BLOCK = 32
CMP_STRIDE = 16
CMP_HIDDEN = 128
SEL_BLOCK = 64
SEL_TOPK = 16
WINDOW = 512
Q_BLOCK = 128
ROPE_THETA = 500000.0
ROPE_DIM = N_HEAD_DIM // 4
NORM_EPS = 1e-6
NEG = -1e30
FORCE = 1e9

IN_SPLITS = (
    ("m_q", M_WIDTH), ("m_k", M_WIDTH), ("m_v", M_WIDTH), ("m_o", M_WIDTH), ("m_z", M_WIDTH),
    ("m_i", M_HEADS), ("m_f", M_HEADS),
    ("n_q", N_WIDTH), ("n_kc", N_KV_WIDTH), ("n_vc", N_KV_WIDTH),
    ("n_ks", N_KV_WIDTH), ("n_vs", N_KV_WIDTH), ("n_kw", N_KV_WIDTH), ("n_vw", N_KV_WIDTH),
    ("n_g", 3 * N_HEADS), ("n_z", N_WIDTH),
    ("g_a", D_MODEL), ("g_b", D_MODEL),
)
IN_WIDTH = sum(w for _, w in IN_SPLITS)

kernel_name = "hybrid_mlstm_nsa_gated_block"


def _rmsnorm(x, g):
    xf = x.astype(jnp.float32)
    r = lax.rsqrt(jnp.mean(xf * xf, axis=-1, keepdims=True) + NORM_EPS)
    return (xf * r * g).astype(x.dtype)


def _split_in(u):
    names = [n for n, _ in IN_SPLITS]
    cuts = np.cumsum([w for _, w in IN_SPLITS])[:-1].tolist()
    return dict(zip(names, jnp.split(u, cuts, axis=-1)))


def _causal_dwconv(u, w, b):
    K = w.shape[0]
    T = u.shape[1]
    up = jnp.pad(u, ((0, 0), (K - 1, 0), (0, 0)))
    return b + sum(up[:, j:j + T] * w[j] for j in range(K))


def _partial_rope(u, pos):
    half = ROPE_DIM // 2
    inv = jnp.power(jnp.float32(ROPE_THETA), -jnp.arange(half, dtype=jnp.float32) * (2.0 / ROPE_DIM))
    ang = pos.astype(jnp.float32)[..., None] * inv
    cos = jnp.cos(ang)[:, :, None, :]
    sin = jnp.sin(ang)[:, :, None, :]
    u1 = u[..., :half]
    u2 = u[..., half:ROPE_DIM]
    r1 = u1 * cos - u2 * sin
    r2 = u2 * cos + u1 * sin
    return jnp.concatenate([r1.astype(u.dtype), r2.astype(u.dtype), u[..., ROPE_DIM:]], axis=-1)


def _mlstm(q, k, v, i_pre, f_pre):
    B, T, H, d = q.shape
    L = M_CHUNK
    NC = T // L
    k = k * (d ** -0.5)

    def to_chunks(a):
        a = a.reshape((B, NC, L, H) + a.shape[3:])
        return jnp.moveaxis(a, (1, 3), (0, 2))

    log_f = jax.nn.log_sigmoid(f_pre)
    log_i = i_pre
    causal = jnp.tril(jnp.ones((L, L), dtype=bool))

    def step(carry, xs):
        C, n, m = carry
        qc, kc, vc, li, lf = xs
        a = jnp.cumsum(lf, axis=-1)
        A = a[..., -1]
        logD = a[..., :, None] - a[..., None, :] + li[..., None, :]
        logD = jnp.where(causal, logD, -jnp.inf)
        inter = a + m[..., None]
        m_t = jnp.maximum(inter, jnp.max(logD, axis=-1))
        w_inter = jnp.exp(inter - m_t)
        s = jnp.einsum('bhld,bhsd->bhls', qc, kc) * jnp.exp(logD - m_t[..., None])
        num = w_inter[..., None] * jnp.einsum('bhld,bhde->bhle', qc, C) + jnp.einsum('bhls,bhse->bhle', s, vc)
        den = w_inter * jnp.einsum('bhld,bhd->bhl', qc, n) + jnp.sum(s, axis=-1)
        h = num / jnp.maximum(jnp.abs(den), jnp.exp(-m_t))[..., None]
        logw = A[..., None] - a + li
        m_new = jnp.maximum(A + m, jnp.max(logw, axis=-1))
        wk = jnp.exp(logw - m_new[..., None])
        decay = jnp.exp(A + m - m_new)
        C_new = decay[..., None, None] * C + jnp.einsum('bhs,bhsd,bhse->bhde', wk, kc, vc)
        n_new = decay[..., None] * n + jnp.einsum('bhs,bhsd->bhd', wk, kc)
        return (C_new, n_new, m_new), h

    init = (jnp.zeros((B, H, d, d), jnp.float32), jnp.zeros((B, H, d), jnp.float32),
            jnp.zeros((B, H), jnp.float32))
    xs = (to_chunks(q), to_chunks(k), to_chunks(v), to_chunks(log_i), to_chunks(log_f))
    _, hs = lax.scan(step, init, xs)
    hs = jnp.moveaxis(hs, (0, 2), (1, 3))
    return hs.reshape(B, T, H, d)


def _head_layernorm(h, g):
    B, T = h.shape[0], h.shape[1]
    mu = jnp.mean(h, axis=-1, keepdims=True)
    var = jnp.mean(jnp.square(h - mu), axis=-1, keepdims=True)
    hn = (h - mu) * lax.rsqrt(var + NORM_EPS)
    return hn.reshape(B, T, -1) * g


def _compress(kv, pos_emb, w1, w2):
    B, T, G, dk = kv.shape
    n_cmp = (T - CMP_BLOCK) // CMP_STRIDE + 1
    idx = jnp.arange(n_cmp)[:, None] * CMP_STRIDE + jnp.arange(CMP_BLOCK)[None, :]
    blocks = kv[:, idx] + pos_emb[None, None, :, None, :]
    blocks = jnp.moveaxis(blocks, 3, 1).reshape(B, G, n_cmp, CMP_BLOCK * dk)
    return jax.nn.gelu(blocks @ w1) @ w2


def _nsa(q_raw, q_rope, kc, vc, ks, vs, kw, vw, gates):
    B, G, h, T, dk = q_raw.shape
    n_cmp = kc.shape[2]
    n_sel = T // SEL_BLOCK
    n_top = min(SEL_TOPK, n_sel)
    scale = dk ** -0.5
    cmp_end = jnp.arange(n_cmp) * CMP_STRIDE + (CMP_BLOCK - 1)
    ci = jnp.arange(n_cmp)[:, None] * CMP_STRIDE
    sj = jnp.arange(n_sel)[None, :] * SEL_BLOCK
    overlap = ((ci < sj + SEL_BLOCK) & (ci + CMP_BLOCK > sj)).astype(jnp.float32)
    ks_blocks = ks.reshape(B, G, n_sel, SEL_BLOCK, dk)
    vs_blocks = vs.reshape(B, G, n_sel, SEL_BLOCK, dk)
    kw_pad = jnp.pad(kw, ((0, 0), (0, 0), (WINDOW, 0), (0, 0)))
    vw_pad = jnp.pad(vw, ((0, 0), (0, 0), (WINDOW, 0), (0, 0)))
    bi = jnp.arange(B)[:, None, None, None]
    gi = jnp.arange(G)[None, :, None, None]
    blk = jnp.arange(n_sel)

    def block(i):
        t0 = i * Q_BLOCK
        tq = t0 + jnp.arange(Q_BLOCK)
        qr = lax.dynamic_slice_in_dim(q_raw, t0, Q_BLOCK, axis=3)
        qp = lax.dynamic_slice_in_dim(q_rope, t0, Q_BLOCK, axis=3)
        g = lax.dynamic_slice_in_dim(gates, t0, Q_BLOCK, axis=3)
        s_c = jnp.einsum('bghqd,bgnd->bghqn', qr, kc).astype(jnp.float32) * scale
        m_c = cmp_end[None, :] <= tq[:, None]
        p_c = jax.nn.softmax(jnp.where(m_c, s_c, NEG), axis=-1) * m_c
        o_c = jnp.einsum('bghqn,bgnd->bghqd', p_c.astype(vc.dtype), vc)
        imp = jnp.einsum('bghqn,ns->bgqs', p_c, overlap)
        cur = tq // SEL_BLOCK
        forced = (blk[None, :] == 0) | (blk[None, :] == cur[:, None]) | (blk[None, :] == cur[:, None] - 1)
        valid = blk[None, :] * SEL_BLOCK <= tq[:, None]
        score = jnp.where(forced, FORCE, jnp.where(valid, imp, -1.0))
        _, idx = lax.top_k(score, n_top)
        kg = ks_blocks[bi, gi, idx]
        vg = vs_blocks[bi, gi, idx]
        kpos = idx[..., None] * SEL_BLOCK + jnp.arange(SEL_BLOCK)
        m_s = kpos <= tq[:, None, None]
        s_s = jnp.einsum('bghqd,bgqnsd->bghqns', qp, kg).astype(jnp.float32) * scale
        s_s = jnp.where(m_s[:, :, None], s_s, NEG).reshape(B, G, h, Q_BLOCK, n_top * SEL_BLOCK)
        p_s = jax.nn.softmax(s_s, axis=-1).reshape(B, G, h, Q_BLOCK, n_top, SEL_BLOCK)
        o_s = jnp.einsum('bghqns,bgqnsd->bghqd', p_s.astype(vg.dtype), vg)
        kwb = lax.dynamic_slice_in_dim(kw_pad, t0, Q_BLOCK + WINDOW, axis=2)
        vwb = lax.dynamic_slice_in_dim(vw_pad, t0, Q_BLOCK + WINDOW, axis=2)
        wpos = t0 - WINDOW + jnp.arange(Q_BLOCK + WINDOW)
        diff = tq[:, None] - wpos[None, :]
        m_w = (diff >= 0) & (diff < WINDOW) & (wpos[None, :] >= 0)
        s_w = jnp.einsum('bghqd,bgkd->bghqk', qp, kwb).astype(jnp.float32) * scale
        p_w = jax.nn.softmax(jnp.where(m_w, s_w, NEG), axis=-1)
        o_w = jnp.einsum('bghqk,bgkd->bghqd', p_w.astype(vwb.dtype), vwb)
        return g[..., 0:1] * o_c + g[..., 1:2] * o_s + g[..., 2:3] * o_w

    out = lax.map(block, jnp.arange(T // Q_BLOCK))
    out = jnp.moveaxis(out, 0, 3).reshape(B, G, h, T, dk)
    return out.transpose(0, 3, 1, 2, 4).reshape(B, T, G * h * dk)


def setup_inputs(seed: int = 0) -> dict:
    key = jax.random.key(seed)
    ks = jax.random.split(key, 24)
    f32 = jnp.float32

    def nrm(k, shape, scale):
        return jax.random.normal(k, shape, f32) * scale

    x = nrm(ks[0], (BATCH, SEQ, D_MODEL), 1.0)
    positions = jnp.tile(jnp.arange(SEQ, dtype=jnp.int32)[None, :], (BATCH, 1))
    norm_g = 1.0 + nrm(ks[1], (DEPTH, D_MODEL), 0.05)
    w_in = nrm(ks[2], (DEPTH, D_MODEL, IN_WIDTH), D_MODEL ** -0.5)
    conv_q_w = nrm(ks[3], (DEPTH, CONV_WIDTH, M_WIDTH), CONV_WIDTH ** -0.5)
    conv_q_b = nrm(ks[4], (DEPTH, M_WIDTH), 0.02)
    conv_k_w = nrm(ks[5], (DEPTH, CONV_WIDTH, M_WIDTH), CONV_WIDTH ** -0.5)
    conv_k_b = nrm(ks[6], (DEPTH, M_WIDTH), 0.02)
    b_igate = nrm(ks[7], (DEPTH, M_HEADS), 0.1)
    b_fgate = jnp.linspace(3.0, 6.0, M_HEADS, dtype=f32)[None, :] + nrm(ks[8], (DEPTH, M_HEADS), 0.1)
    mh_norm_g = 1.0 + nrm(ks[9], (DEPTH, M_WIDTH), 0.05)
    cmp_pos_k = nrm(ks[10], (DEPTH, CMP_BLOCK, N_HEAD_DIM), 0.1)
    cmp_w1_k = nrm(ks[11], (DEPTH, CMP_BLOCK * N_HEAD_DIM, CMP_HIDDEN), (CMP_BLOCK * N_HEAD_DIM) ** -0.5)
    cmp_w2_k = nrm(ks[12], (DEPTH, CMP_HIDDEN, N_HEAD_DIM), CMP_HIDDEN ** -0.5)
    cmp_pos_v = nrm(ks[13], (DEPTH, CMP_BLOCK, N_HEAD_DIM), 0.1)
    cmp_w1_v = nrm(ks[14], (DEPTH, CMP_BLOCK * N_HEAD_DIM, CMP_HIDDEN), (CMP_BLOCK * N_HEAD_DIM) ** -0.5)
    cmp_w2_v = nrm(ks[15], (DEPTH, CMP_HIDDEN, N_HEAD_DIM), CMP_HIDDEN ** -0.5)
    b_nsa_gate = nrm(ks[16], (DEPTH, 3 * N_HEADS), 0.1)
    w_branch_a = nrm(ks[17], (DEPTH, M_WIDTH, D_MODEL), M_WIDTH ** -0.5)
    w_branch_b = nrm(ks[18], (DEPTH, N_WIDTH, D_MODEL), N_WIDTH ** -0.5)
    w_out = nrm(ks[19], (DEPTH, D_MODEL, D_MODEL), D_MODEL ** -0.5)
    final_norm_g = 1.0 + nrm(ks[20], (D_MODEL,), 0.05)
    return {"x": x, "positions": positions, "norm_g": norm_g, "w_in": w_in,
            "conv_q_w": conv_q_w, "conv_q_b": conv_q_b, "conv_k_w": conv_k_w, "conv_k_b": conv_k_b,
            "b_igate": b_igate, "b_fgate": b_fgate, "mh_norm_g": mh_norm_g,
            "cmp_pos_k": cmp_pos_k, "cmp_w1_k": cmp_w1_k, "cmp_w2_k": cmp_w2_k,
            "cmp_pos_v": cmp_pos_v, "cmp_w1_v": cmp_w1_v, "cmp_w2_v": cmp_w2_v,
            "b_nsa_gate": b_nsa_gate, "w_branch_a": w_branch_a, "w_branch_b": w_branch_b,
            "w_out": w_out, "final_norm_g": final_norm_g}


def reference(x, positions, norm_g, w_in, conv_q_w, conv_q_b, conv_k_w, conv_k_b, b_igate, b_fgate,
              mh_norm_g, cmp_pos_k, cmp_w1_k, cmp_w2_k, cmp_pos_v, cmp_w1_v, cmp_w2_v, b_nsa_gate,
              w_branch_a, w_branch_b, w_out, final_norm_g):
    B, T, _ = x.shape
    f32 = jnp.float32
    for l in range(DEPTH):
        hN = _rmsnorm(x, norm_g[l])
        p = _split_in(hN @ w_in[l])
        mq = jax.nn.silu(_causal_dwconv(p["m_q"], conv_q_w[l], conv_q_b[l]))
        mk = jax.nn.silu(_causal_dwconv(p["m_k"], conv_k_w[l], conv_k_b[l]))
        mh = _mlstm(mq.reshape(B, T, M_HEADS, M_HEAD_DIM).astype(f32),
                    mk.reshape(B, T, M_HEADS, M_HEAD_DIM).astype(f32),
                    p["m_v"].reshape(B, T, M_HEADS, M_HEAD_DIM).astype(f32),
                    (p["m_i"] + b_igate[l]).astype(f32),
                    (p["m_f"] + b_fgate[l]).astype(f32))
        mh = _head_layernorm(mh, mh_norm_g[l]).astype(x.dtype)
        ya = jax.nn.sigmoid(p["m_o"]) * mh * jax.nn.silu(p["m_z"])
        nq = p["n_q"].reshape(B, T, N_HEADS, N_HEAD_DIM)
        q_raw = nq.reshape(B, T, N_KV_GROUPS, N_HPG, N_HEAD_DIM).transpose(0, 2, 3, 1, 4)
        q_rope = _partial_rope(nq, positions).reshape(B, T, N_KV_GROUPS, N_HPG, N_HEAD_DIM).transpose(0, 2, 3, 1, 4)

        def kvh(a):
            return a.reshape(B, T, N_KV_GROUPS, N_HEAD_DIM)

        kc = _compress(kvh(p["n_kc"]), cmp_pos_k[l], cmp_w1_k[l], cmp_w2_k[l])
        vc = _compress(kvh(p["n_vc"]), cmp_pos_v[l], cmp_w1_v[l], cmp_w2_v[l])
        ks = _partial_rope(kvh(p["n_ks"]), positions).transpose(0, 2, 1, 3)
        vs = kvh(p["n_vs"]).transpose(0, 2, 1, 3)
        kw = _partial_rope(kvh(p["n_kw"]), positions).transpose(0, 2, 1, 3)
        vw = kvh(p["n_vw"]).transpose(0, 2, 1, 3)
        gates = jax.nn.sigmoid(p["n_g"] + b_nsa_gate[l]).reshape(B, T, N_KV_GROUPS, N_HPG, 3).transpose(0, 2, 3, 1, 4)
        yn = _nsa(q_raw, q_rope, kc, vc, ks, vs, kw, vw, gates)
        yb = yn * jax.nn.silu(p["n_z"])
        merged = (jax.nn.sigmoid(p["g_a"]) * (ya @ w_branch_a[l])
                  + jax.nn.sigmoid(p["g_b"]) * (yb @ w_branch_b[l]))
        x = x + merged @ w_out[l]
    return _rmsnorm(x, final_norm_g)
```

```python
import functools
import math

import jax
import jax.numpy as jnp
from jax import lax
from jax.experimental import pallas as pl
from jax.experimental.pallas import tpu as pltpu

F32 = jnp.float32
BF16 = jnp.bfloat16

D_MODEL = 1024
M_HEADS = 4
M_HEAD_DIM = 256
M_WIDTH = M_HEADS * M_HEAD_DIM
CONV_WIDTH = 4
N_HEADS = 8
N_KV_GROUPS = 2
N_HPG = N_HEADS // N_KV_GROUPS
N_HEAD_DIM = 64
N_WIDTH = N_HEADS * N_HEAD_DIM
CMP_BLOCK = 32
CMP_STRIDE = 16
CMP_HIDDEN = 128
SEL_BLOCK = 64
SEL_TOPK = 16
WINDOW = 512
Q_BLOCK = 128
ROPE_THETA = 500000.0
ROPE_DIM = N_HEAD_DIM // 4
NORM_EPS = 1e-6
NEG = -1e30
FORCE = 1e9

LANES = 128
MLSTM_CHUNK = 256
SEL_CHUNK = 512
VMEM_LIMIT = 56 * 1024 * 1024

C_MQ, C_MK, C_MV, C_MO, C_MZ, C_GA, C_GB = 0, 1024, 2048, 3072, 4096, 5120, 6144
C_NQ, C_NZ = 7168, 7680
C_KV = 8192
C_CMP = 8704
C_SM = 8960
P_W = 9216
_O_MI, _O_NQ, _O_KC, _O_KS, _O_NG, _O_NZ, _O_GA, _O_END = 5120, 5128, 5640, 5896, 6408, 6432, 6944, 8992


def _cparams(sem):
    return pltpu.CompilerParams(dimension_semantics=sem, vmem_limit_bytes=VMEM_LIMIT)


def _sigmoid(x):
    return 1.0 / (1.0 + jnp.exp(-x))


def _silu(x):
    return x * _sigmoid(x)


def _log_sigmoid(x):
    return jnp.minimum(x, 0.0) - jnp.log1p(jnp.exp(-jnp.abs(x)))


def _inproj_kernel(x_ref, g_ref, w_ref, o_ref, hn_ref):
    @pl.when(pl.program_id(1) == 0)
    def _():
        x = x_ref[...]
        r = lax.rsqrt(jnp.mean(x * x, axis=-1, keepdims=True) + NORM_EPS)
        hn_ref[...] = (x * r * g_ref[...]).astype(BF16)

    o_ref[...] = jnp.dot(hn_ref[...], w_ref[...], preferred_element_type=F32)


def _inproj(x2, norm_g, w_perm):
    m = x2.shape[0]
    tm, tn = 1024, 1024
    return pl.pallas_call(
        _inproj_kernel,
        out_shape=jax.ShapeDtypeStruct((m, P_W), F32),
        grid=(m // tm, P_W // tn),
        in_specs=[
            pl.BlockSpec((tm, D_MODEL), lambda i, j: (i, 0)),
            pl.BlockSpec((1, D_MODEL), lambda i, j: (0, 0)),
            pl.BlockSpec((D_MODEL, tn), lambda i, j: (0, j)),
        ],
        out_specs=pl.BlockSpec((tm, tn), lambda i, j: (i, j)),
        scratch_shapes=[pltpu.VMEM((tm, D_MODEL), BF16)],
        compiler_params=_cparams(("parallel", "arbitrary")),
        name="inproj",
    )(x2, norm_g, w_perm)


def _mlstm_kernel(q_ref, k_ref, v_ref, og_ref, z_ref, sm_ref, cqw_ref, cqb_ref, ckw_ref, ckb_ref,
                  sb_ref, ng_ref, ya_ref, qbuf, kbuf, c_state, n_state, m_state):
    L = q_ref.shape[0]
    d = M_HEAD_DIM

    @pl.when(pl.program_id(1) == 0)
    def _():
        qbuf[0:8, :] = jnp.zeros((8, M_WIDTH), F32)
        kbuf[0:8, :] = jnp.zeros((8, M_WIDTH), F32)
        c_state[...] = jnp.zeros_like(c_state)
        n_state[...] = jnp.zeros_like(n_state)
        m_state[...] = jnp.zeros_like(m_state)

    qbuf[8:8 + L, :] = q_ref[...]
    kbuf[8:8 + L, :] = k_ref[...]

    def conv(buf, w_ref, b_ref):
        acc = b_ref[...] + buf[5:5 + L, :] * w_ref[0:1, :]
        for j in range(1, CONV_WIDTH):
            acc = acc + buf[5 + j:5 + j + L, :] * w_ref[j:j + 1, :]
        return _silu(acc)

    mq = conv(qbuf, cqw_ref, cqb_ref)
    mk = conv(kbuf, ckw_ref, ckb_ref) * (d ** -0.5)
    qbuf[0:8, :] = qbuf[L:L + 8, :]
    kbuf[0:8, :] = kbuf[L:L + 8, :]

    sm = sm_ref[...] + sb_ref[...]
    lf = _log_sigmoid(sm)
    row = lax.broadcasted_iota(jnp.int32, (L, L), 0)
    col = lax.broadcasted_iota(jnp.int32, (L, L), 1)
    causal = row >= col
    tril = causal.astype(F32)
    triu = (row <= col).astype(F32)
    a_col = jnp.dot(tril, lf, precision=lax.Precision.HIGHEST, preferred_element_type=F32)
    sm_t = sm.T[0:8, :]
    a_row = jnp.dot(_log_sigmoid(sm_t), triu, precision=lax.Precision.HIGHEST,
                    preferred_element_type=F32)

    v = v_ref[...]
    for h in range(M_HEADS):
        hs = slice(h * d, (h + 1) * d)
        a_c = a_col[:, 4 + h:5 + h]
        li_c = sm[:, h:h + 1]
        a_r = a_row[4 + h:5 + h, :]
        li_r = sm_t[h:h + 1, :]
        m_prev = m_state[h:h + 1, 0:1]
        a_end = a_c[L - 1:L, :]

        log_d = jnp.where(causal, a_c - a_r + li_r, -jnp.inf)
        inter = a_c + m_prev
        m_t = jnp.maximum(inter, jnp.max(log_d, axis=1, keepdims=True))
        w_inter = jnp.exp(inter - m_t)
        dm = jnp.exp(log_d - m_t)

        qh = mq[:, hs]
        kh = mk[:, hs]
        qb = qh.astype(BF16)
        kb = kh.astype(BF16)
        vb = v[:, hs].astype(BF16)
        s = lax.dot_general(qb, kb, (((1,), (1,)), ((), ())), preferred_element_type=F32) * dm
        c_old = c_state[h]
        n_old = n_state[h:h + 1, :]
        num = (w_inter * jnp.dot(qb, c_old.astype(BF16), preferred_element_type=F32)
               + jnp.dot(s.astype(BF16), vb, preferred_element_type=F32))
        den = (w_inter * jnp.sum(qh * n_old, axis=1, keepdims=True)
               + jnp.sum(s, axis=1, keepdims=True))
        hh = num / jnp.maximum(jnp.abs(den), jnp.exp(-m_t))

        logw = a_end - a_c + li_c
        m_new = jnp.maximum(a_end + m_prev, jnp.max(logw, axis=0, keepdims=True))
        wk = jnp.exp(logw - m_new)
        decay = jnp.exp(a_end + m_prev - m_new)
        kw = kh * wk
        c_state[h] = decay * c_old + lax.dot_general(kw.astype(BF16), vb, (((0,), (0,)), ((), ())),
                                                     preferred_element_type=F32)
        n_state[h:h + 1, :] = decay * n_old + jnp.sum(kw, axis=0, keepdims=True)
        m_state[h:h + 1, :] = jnp.broadcast_to(m_new, (1, LANES))

        mu = jnp.mean(hh, axis=1, keepdims=True)
        dc = hh - mu
        var = jnp.mean(dc * dc, axis=1, keepdims=True)
        hn = dc * lax.rsqrt(var + NORM_EPS) * ng_ref[:, hs]
        ya = _sigmoid(og_ref[:, hs]) * hn * _silu(z_ref[:, hs])
        ya_ref[:, hs] = ya.astype(BF16)


def _mlstm(p, b, t, conv_q_w, conv_q_b, conv_k_w, conv_k_b, sm_bias, mh_norm_g):
    L = MLSTM_CHUNK
    nc = t // L

    def colspec(cstart):
        return pl.BlockSpec((L, M_WIDTH), lambda bi, c: (bi * nc + c, cstart // M_WIDTH))

    def full(shape):
        return pl.BlockSpec(shape, lambda bi, c: (0,) * len(shape))

    return pl.pallas_call(
        _mlstm_kernel,
        out_shape=jax.ShapeDtypeStruct((b * t, M_WIDTH), BF16),
        grid=(b, nc),
        in_specs=[
            colspec(C_MQ), colspec(C_MK), colspec(C_MV), colspec(C_MO), colspec(C_MZ),
            pl.BlockSpec((L, LANES), lambda bi, c: (bi * nc + c, C_SM // LANES)),
            full((CONV_WIDTH, M_WIDTH)), full((1, M_WIDTH)), full((CONV_WIDTH, M_WIDTH)), full((1, M_WIDTH)),
            full((1, LANES)), full((1, M_WIDTH)),
        ],
        out_specs=pl.BlockSpec((L, M_WIDTH), lambda bi, c: (bi * nc + c, 0)),
        scratch_shapes=[
            pltpu.VMEM((L + 8, M_WIDTH), F32), pltpu.VMEM((L + 8, M_WIDTH), F32),
            pltpu.VMEM((M_HEADS, M_HEAD_DIM, M_HEAD_DIM), F32),
            pltpu.VMEM((8, M_HEAD_DIM), F32), pltpu.VMEM((8, LANES), F32),
        ],
        compiler_params=_cparams(("parallel", "arbitrary")),
        name="mlstm",
    )(p, p, p, p, p, p, conv_q_w, conv_q_b, conv_k_w, conv_k_b, sm_bias, mh_norm_g)


def _compress_kernel(r_ref, wt_ref, wb_ref, posk_ref, posv_ref, w1k_ref, w1v_ref, w2k_ref, w2v_ref,
                     kct_ref, vc_ref):
    r = r_ref[...].astype(BF16)
    nr = r.shape[0]
    u = jnp.dot(r, wt_ref[...], preferred_element_type=F32)
    v = jnp.dot(r, wb_ref[...], preferred_element_type=F32)
    v = pltpu.roll(v, nr - 1, axis=0)
    rowi = lax.broadcasted_iota(jnp.int32, v.shape, 0)
    v = jnp.where(rowi < nr - 1, v, 0.0)

    def pos_term(pos_ref, w_ref):
        pb = jnp.broadcast_to(pos_ref[...].astype(BF16), (8, CMP_BLOCK * N_HEAD_DIM))
        return jnp.dot(pb, w_ref[...], preferred_element_type=F32)[0:1, :]

    pk = pos_term(posk_ref, w1k_ref)
    pv = pos_term(posv_ref, w1v_ref)
    hid = u + v + jnp.concatenate([pk, pk, pv, pv], axis=1)
    act = 0.5 * hid * (1.0 + jnp.tanh(math.sqrt(2.0 / math.pi) * (hid + 0.044715 * (hid * hid * hid))))
    for c in range(2 * N_KV_GROUPS):
        w2 = w2k_ref[...] if c < N_KV_GROUPS else w2v_ref[...]
        out = jnp.dot(act[:, c * CMP_HIDDEN:(c + 1) * CMP_HIDDEN].astype(BF16), w2,
                      preferred_element_type=F32)
        if c < N_KV_GROUPS:
            kct_ref[c] = out.T.astype(BF16)
        else:
            vc_ref[c - N_KV_GROUPS] = out.astype(BF16)


def _compress(r, wt, wb, posk, posv, w1k, w1v, w2k, w2v):
    b, nr, rw = r.shape
    g = N_KV_GROUPS

    def full(a):
        return pl.BlockSpec(a.shape, lambda bi: (0,) * a.ndim)

    return pl.pallas_call(
        _compress_kernel,
        out_shape=(jax.ShapeDtypeStruct((b, g, LANES, nr), BF16), jax.ShapeDtypeStruct((b, g, nr, LANES), BF16)),
        grid=(b,),
        in_specs=[pl.BlockSpec((None, nr, rw), lambda bi: (bi, 0, 0)),
                  full(wt), full(wb), full(posk), full(posv), full(w1k), full(w1v), full(w2k), full(w2v)],
        out_specs=(pl.BlockSpec((None, g, LANES, nr), lambda bi: (bi, 0, 0, 0)),
                   pl.BlockSpec((None, g, nr, LANES), lambda bi: (bi, 0, 0, 0))),
        compiler_params=_cparams(("parallel",)),
        name="compress",
    )(r, wt, wb, posk, posv, w1k, w1v, w2k, w2v)


def _kvprep_kernel(q_ref, kv_ref, sm_ref, pos_ref, inv_ref, gb_ref,
                   qraw_ref, qrope_ref, kst_ref, vs_ref, kwt_ref, vw_ref, gates_ref):
    tk = q_ref.shape[0]
    lane = lax.broadcasted_iota(jnp.int32, (tk, LANES), 1)
    lih = lane % N_HEAD_DIM
    ang = pos_ref[...] * inv_ref[...]
    cos = jnp.cos(ang)
    sin = jnp.sin(ang)
    half = ROPE_DIM // 2
    s_up = jnp.where(lih < half, -sin, 0.0)
    s_dn = jnp.where((lih >= half) & (lih < ROPE_DIM), sin, 0.0)

    def rope(u):
        return u * cos + pltpu.roll(u, LANES - half, axis=1) * s_up + pltpu.roll(u, half, axis=1) * s_dn

    def head_pad(u, odd):
        if odd:
            u = pltpu.roll(u, N_HEAD_DIM, axis=1)
        return jnp.where(lane < N_HEAD_DIM, u, 0.0)

    scale = N_HEAD_DIM ** -0.5
    for g in range(N_KV_GROUPS):
        for hp in range(N_HPG // 2):
            c0 = g * (N_HPG * N_HEAD_DIM) + hp * LANES
            u = q_ref[:, c0:c0 + LANES] * scale
            ur = rope(u)
            for odd in range(2):
                h = 2 * hp + odd
                qraw_ref[g, :, h * LANES:(h + 1) * LANES] = head_pad(u, odd).astype(BF16)
                qrope_ref[g, :, h * LANES:(h + 1) * LANES] = head_pad(ur, odd).astype(BF16)

    ks_t = rope(kv_ref[:, 0:LANES]).T
    kw_t = rope(kv_ref[:, 2 * LANES:3 * LANES]).T
    vs = kv_ref[:, LANES:2 * LANES]
    vw = kv_ref[:, 3 * LANES:4 * LANES]
    t_glob = pl.program_id(1) * tk + lax.broadcasted_iota(jnp.int32, (LANES, tk), 1)
    blk_row = lax.broadcasted_iota(jnp.int32, (LANES, tk), 0)
    onehot = jnp.where(blk_row == t_glob // SEL_BLOCK, 1.0, 0.0).astype(BF16)
    zeros_t = jnp.zeros((N_HEAD_DIM, tk), BF16)
    ones_lane = jnp.where(lane == N_HEAD_DIM, 1.0, 0.0)
    gates = _sigmoid(sm_ref[...] + gb_ref[...])
    for g in range(N_KV_GROUPS):
        rs = slice(g * N_HEAD_DIM, (g + 1) * N_HEAD_DIM)
        kst_ref[g, 0:LANES, :] = onehot
        kst_ref[g, LANES:LANES + N_HEAD_DIM, :] = ks_t[rs, :].astype(BF16)
        kst_ref[g, LANES + N_HEAD_DIM:2 * LANES, :] = zeros_t
        kwt_ref[g, 0:N_HEAD_DIM, :] = kw_t[rs, :].astype(BF16)
        kwt_ref[g, N_HEAD_DIM:LANES, :] = zeros_t
        vs_ref[g] = (head_pad(vs, g) + ones_lane).astype(BF16)
        vw_ref[g] = (head_pad(vw, g) + ones_lane).astype(BF16)
        goff = 8 + g * (3 * N_HPG)
        gates_ref[g] = pltpu.roll(gates, LANES - goff, axis=1)


def _kvprep(p, b, t, pos_col, inv_row, gate_bias):
    tk = 512
    nt = t // tk
    g = N_KV_GROUPS
    qw = N_HPG * LANES

    def tok(shape):
        return pl.BlockSpec((None, g) + shape, lambda bi, i: (bi, 0, i, 0))

    def tok_t(rows):
        return pl.BlockSpec((None, g, rows, tk), lambda bi, i: (bi, 0, 0, i))

    return pl.pallas_call(
        _kvprep_kernel,
        out_shape=(
            jax.ShapeDtypeStruct((b, g, t, qw), BF16), jax.ShapeDtypeStruct((b, g, t, qw), BF16),
            jax.ShapeDtypeStruct((b, g, 2 * LANES, t), BF16), jax.ShapeDtypeStruct((b, g, t, LANES), BF16),
            jax.ShapeDtypeStruct((b, g, LANES, t), BF16), jax.ShapeDtypeStruct((b, g, t, LANES), BF16),
            jax.ShapeDtypeStruct((b, g, t, LANES), F32),
        ),
        grid=(b, nt),
        in_specs=[
            pl.BlockSpec((tk, N_WIDTH), lambda bi, i: (bi * nt + i, C_NQ // N_WIDTH)),
            pl.BlockSpec((tk, 4 * LANES), lambda bi, i: (bi * nt + i, C_KV // (4 * LANES))),
            pl.BlockSpec((tk, LANES), lambda bi, i: (bi * nt + i, C_SM // LANES)),
            pl.BlockSpec((tk, 1), lambda bi, i: (bi * nt + i, 0)),
            pl.BlockSpec((1, LANES), lambda bi, i: (0, 0)),
            pl.BlockSpec((1, LANES), lambda bi, i: (0, 0)),
        ],
        out_specs=(tok((tk, qw)), tok((tk, qw)), tok_t(2 * LANES), tok((tk, LANES)),
                   tok_t(LANES), tok((tk, LANES)), tok((tk, LANES))),
        compiler_params=_cparams(("parallel", "arbitrary")),
        name="kvprep",
    )(p, p, p, pos_col, inv_row, gate_bias)


def _nsa_kernel(qraw_ref, qrope_ref, kct_ref, vc_ref, kst_ref, vs_ref, kwt_ref, vw_ref, gates_ref, nz_ref,
                ov_ref, yb_ref, acc_ref, m_ref):
    i = pl.program_id(2)
    t0 = i * Q_BLOCK
    Q = Q_BLOCK
    nh = N_HPG
    t_len = kst_ref.shape[1]
    n_cmp_pad = kct_ref.shape[1]

    qrow = lax.broadcasted_iota(jnp.int32, (Q, LANES), 0)
    lane = lax.broadcasted_iota(jnp.int32, (Q, LANES), 1)
    tq = t0 + qrow

    ncol = lax.broadcasted_iota(jnp.int32, (Q, n_cmp_pad), 1)
    tq_c = t0 + lax.broadcasted_iota(jnp.int32, (Q, n_cmp_pad), 0)
    m_c = (ncol * CMP_STRIDE + (CMP_BLOCK - 1)) <= tq_c
    kct = kct_ref[...]
    vc = vc_ref[...]
    psum = jnp.zeros((Q, n_cmp_pad), F32)
    o_c = []
    for h in range(nh):
        qh = qraw_ref[:, h * LANES:(h + 1) * LANES]
        s = jnp.where(m_c, jnp.dot(qh, kct, preferred_element_type=F32), NEG)
        e = jnp.where(m_c, jnp.exp(s - jnp.max(s, axis=1, keepdims=True)), 0.0)
        den = jnp.sum(e, axis=1, keepdims=True)
        pc = e / jnp.where(den == 0.0, 1.0, den)
        psum = psum + pc
        o_c.append(jnp.dot(pc.astype(BF16), vc, preferred_element_type=F32))
    p_hi = psum.astype(BF16)
    p_lo = (psum - p_hi.astype(F32)).astype(BF16)
    ov = ov_ref[...]
    imp = jnp.dot(p_hi, ov, preferred_element_type=F32) + jnp.dot(p_lo, ov, preferred_element_type=F32)

    cur = tq // SEL_BLOCK
    forced = (lane == 0) | (lane == cur) | (lane == cur - 1)
    valid = lane * SEL_BLOCK <= tq
    n_sel = t_len // SEL_BLOCK
    score = jnp.where(forced, FORCE, jnp.where(valid, imp, -1.0))
    score = jnp.where(lane < n_sel, score, -jnp.inf)
    sel = jnp.zeros((Q, LANES), F32)
    for _ in range(min(SEL_TOPK, n_sel)):
        mx = jnp.max(score, axis=1, keepdims=True)
        idx = jnp.min(jnp.where(score == mx, lane, LANES), axis=1, keepdims=True)
        pick = lane == idx
        sel = jnp.where(pick, 1.0, sel)
        score = jnp.where(pick, -jnp.inf, score)
    bias = jnp.where((sel > 0.0) & (lane < 2 * i), 0.0, NEG).astype(BF16)

    q_rope = jnp.concatenate([qrope_ref[:, h * LANES:(h + 1) * LANES] for h in range(nh)], axis=0)
    q_aug = jnp.concatenate([jnp.concatenate([bias] * nh, axis=0), q_rope], axis=1)
    rq = lax.broadcasted_iota(jnp.int32, (nh * Q, Q), 0) % Q
    ck = lax.broadcasted_iota(jnp.int32, (nh * Q, Q), 1)
    t0a = pl.multiple_of(t0, Q)
    s_d = jnp.dot(q_rope, kst_ref[LANES:2 * LANES, pl.ds(t0a, Q)], preferred_element_type=F32)
    s_d = jnp.where(ck <= rq, s_d, NEG)
    m0 = jnp.max(s_d, axis=1, keepdims=True)
    p_d = jnp.exp(s_d - m0)
    m_ref[...] = m0
    acc_ref[...] = jnp.dot(p_d.astype(BF16), vs_ref[pl.ds(t0a, Q), :], preferred_element_type=F32)

    def sel_step(c, carry):
        k0 = pl.multiple_of(c * SEL_CHUNK, SEL_CHUNK)
        s = jnp.dot(q_aug, kst_ref[:, pl.ds(k0, SEL_CHUNK)], preferred_element_type=F32)
        m_old = m_ref[...]
        m_new = jnp.maximum(m_old, jnp.max(s, axis=1, keepdims=True))
        p = jnp.exp(s - m_new)
        acc_ref[...] = (jnp.exp(m_old - m_new) * acc_ref[...]
                        + jnp.dot(p.astype(BF16), vs_ref[pl.ds(k0, SEL_CHUNK), :], preferred_element_type=F32))
        m_ref[...] = m_new
        return carry

    n_chunks = (t0 + SEL_CHUNK - 1) // SEL_CHUNK
    lax.fori_loop(0, n_chunks, sel_step, 0)
    acc = acc_ref[...]
    o_s = acc / acc[:, N_HEAD_DIM:N_HEAD_DIM + 1]

    wlen = WINDOW + Q
    ws = pl.multiple_of(jnp.maximum(t0 - WINDOW, 0), Q)
    s_w = jnp.dot(q_rope, kwt_ref[:, pl.ds(ws, wlen)], preferred_element_type=F32)
    tq_w = t0 + lax.broadcasted_iota(jnp.int32, (nh * Q, wlen), 0) % Q
    diff = tq_w - (ws + lax.broadcasted_iota(jnp.int32, (nh * Q, wlen), 1))
    s_w = jnp.where((diff >= 0) & (diff < WINDOW), s_w, NEG)
    p_w = jnp.exp(s_w - jnp.max(s_w, axis=1, keepdims=True))
    acc_w = jnp.dot(p_w.astype(BF16), vw_ref[pl.ds(ws, wlen), :], preferred_element_type=F32)
    o_w = acc_w / acc_w[:, N_HEAD_DIM:N_HEAD_DIM + 1]

    gates = gates_ref[...]
    outs = []
    for h in range(nh):
        rs = slice(h * Q, (h + 1) * Q)
        outs.append(gates[:, 3 * h:3 * h + 1] * o_c[h]
                    + gates[:, 3 * h + 1:3 * h + 2] * o_s[rs, :]
                    + gates[:, 3 * h + 2:3 * h + 3] * o_w[rs, :])
    for hp in range(nh // 2):
        yn = jnp.where(lane < N_HEAD_DIM, outs[2 * hp], pltpu.roll(outs[2 * hp + 1], N_HEAD_DIM, axis=1))
        ls = slice(hp * LANES, (hp + 1) * LANES)
        yb_ref[:, ls] = (yn * _silu(nz_ref[:, ls])).astype(BF16)


def _nsa(p, b, t, qraw, qrope, kct, vc, kst, vs, kwt, vw, gates, overlap):
    g = N_KV_GROUPS
    nqb = t // Q_BLOCK
    qw = N_HPG * LANES
    gw = N_HPG * N_HEAD_DIM
    nr = kct.shape[-1]

    def qblk(w):
        return pl.BlockSpec((None, None, Q_BLOCK, w), lambda bi, gi, i: (bi, gi, i, 0))

    def whole(rows, cols):
        return pl.BlockSpec((None, None, rows, cols), lambda bi, gi, i: (bi, gi, 0, 0))

    return pl.pallas_call(
        _nsa_kernel,
        out_shape=jax.ShapeDtypeStruct((b * t, N_WIDTH), BF16),
        grid=(b, g, nqb),
        in_specs=[
            qblk(qw), qblk(qw),
            whole(LANES, nr), whole(nr, LANES),
            whole(2 * LANES, t), whole(t, LANES), whole(LANES, t), whole(t, LANES),
            qblk(LANES),
            pl.BlockSpec((Q_BLOCK, gw), lambda bi, gi, i: (bi * nqb + i, C_NZ // gw + gi)),
            pl.BlockSpec(overlap.shape, lambda bi, gi, i: (0, 0)),
        ],
        out_specs=pl.BlockSpec((Q_BLOCK, gw), lambda bi, gi, i: (bi * nqb + i, gi)),
        scratch_shapes=[pltpu.VMEM((N_HPG * Q_BLOCK, LANES), F32), pltpu.VMEM((N_HPG * Q_BLOCK, 1), F32)],
        compiler_params=_cparams(("parallel", "parallel", "arbitrary")),
        name="nsa",
    )(qraw, qrope, kct, vc, kst, vs, kwt, vw, gates, p, overlap)


def _outproj_kernel(final, ya_ref, yb_ref, ga_ref, gb_ref, x_ref, wa_ref, wb_ref, wo_ref, fg_ref, o_ref):
    a = jnp.dot(ya_ref[...], wa_ref[...], preferred_element_type=F32)
    bq = jnp.dot(yb_ref[...], wb_ref[...], preferred_element_type=F32)
    merged = _sigmoid(ga_ref[...]) * a + _sigmoid(gb_ref[...]) * bq
    y = x_ref[...] + jnp.dot(merged.astype(BF16), wo_ref[...], preferred_element_type=F32)
    if final:
        y = y * lax.rsqrt(jnp.mean(y * y, axis=-1, keepdims=True) + NORM_EPS) * fg_ref[...]
    o_ref[...] = y


def _outproj(ya, yb, p, x2, wa, wb, wo, fg, final):
    m = x2.shape[0]
    tm = 512

    def full(a):
        return pl.BlockSpec(a.shape, lambda i: (0, 0))

    return pl.pallas_call(
        functools.partial(_outproj_kernel, final),
        out_shape=jax.ShapeDtypeStruct((m, D_MODEL), F32),
        grid=(m // tm,),
        in_specs=[
            pl.BlockSpec((tm, M_WIDTH), lambda i: (i, 0)),
            pl.BlockSpec((tm, N_WIDTH), lambda i: (i, 0)),
            pl.BlockSpec((tm, D_MODEL), lambda i: (i, C_GA // D_MODEL)),
            pl.BlockSpec((tm, D_MODEL), lambda i: (i, C_GB // D_MODEL)),
            pl.BlockSpec((tm, D_MODEL), lambda i: (i, 0)),
            full(wa), full(wb), full(wo), full(fg),
        ],
        out_specs=pl.BlockSpec((tm, D_MODEL), lambda i: (i, 0)),
        compiler_params=_cparams(("parallel",)),
        name="outproj",
    )(ya, yb, p, p, x2, wa, wb, wo, fg)


def _permute_w_in(w):
    pad = jnp.zeros((w.shape[0], P_W - _O_END), w.dtype)
    return jnp.concatenate([
        w[:, 0:_O_MI],
        w[:, _O_GA:_O_END],
        w[:, _O_NQ:_O_KC],
        w[:, _O_NZ:_O_GA],
        w[:, _O_KS:_O_NG],
        w[:, _O_KC:_O_KS],
        w[:, _O_MI:_O_NQ],
        w[:, _O_NG:_O_NZ],
        pad], axis=1)


def _expand_cmp_w1(w1k, w1v):
    half = CMP_BLOCK // 2
    k3 = w1k.reshape(CMP_BLOCK, N_HEAD_DIM, CMP_HIDDEN)
    v3 = w1v.reshape(CMP_BLOCK, N_HEAD_DIM, CMP_HIDDEN)
    outs = []
    for sl in (slice(0, half), slice(half, CMP_BLOCK)):
        z = jnp.zeros((half, 4, N_HEAD_DIM, 4, CMP_HIDDEN), F32)
        z = z.at[:, 0, :, 0, :].set(k3[sl]).at[:, 1, :, 1, :].set(k3[sl])
        z = z.at[:, 2, :, 2, :].set(v3[sl]).at[:, 3, :, 3, :].set(v3[sl])
        outs.append(z.reshape(half * 4 * N_HEAD_DIM, 4 * CMP_HIDDEN).astype(BF16))
    return outs


def kernel(x, positions, norm_g, w_in, conv_q_w, conv_q_b, conv_k_w, conv_k_b, b_igate, b_fgate, mh_norm_g,
           cmp_pos_k, cmp_w1_k, cmp_w2_k, cmp_pos_v, cmp_w1_v, cmp_w2_v, b_nsa_gate, w_branch_a, w_branch_b,
           w_out, final_norm_g):
    b, t, _ = x.shape
    assert t % SEL_CHUNK == 0 and t >= WINDOW + Q_BLOCK and t // SEL_BLOCK <= LANES and t % MLSTM_CHUNK == 0
    x2 = x.reshape(b * t, D_MODEL)
    for l in range(norm_g.shape[0]):
        p = _inproj(x2, norm_g[l][None, :], _permute_w_in(w_in[l]).astype(BF16))

        sm_bias = jnp.zeros((1, LANES), F32).at[0, 0:M_HEADS].set(b_igate[l]).at[0, M_HEADS:2 * M_HEADS].set(b_fgate[l])
        ya = _mlstm(p, b, t, conv_q_w[l], conv_q_b[l][None, :], conv_k_w[l], conv_k_b[l][None, :], sm_bias,
                    mh_norm_g[l][None, :])

        nr = t // CMP_STRIDE
        r = p.reshape(b, t, P_W)[:, :, C_CMP:C_CMP + 2 * LANES].reshape(b, nr, CMP_STRIDE * 2 * LANES)
        wt, wb = _expand_cmp_w1(cmp_w1_k[l], cmp_w1_v[l])
        w2pad = lambda w: jnp.pad(w, ((0, 0), (0, LANES - N_HEAD_DIM))).astype(BF16)
        kct, vc = _compress(r, wt, wb, cmp_pos_k[l].reshape(1, -1), cmp_pos_v[l].reshape(1, -1),
                            cmp_w1_k[l].astype(BF16), cmp_w1_v[l].astype(BF16), w2pad(cmp_w2_k[l]), w2pad(cmp_w2_v[l]))

        half = ROPE_DIM // 2
        inv = jnp.power(jnp.float32(ROPE_THETA), -jnp.arange(half, dtype=F32) * (2.0 / ROPE_DIM))
        lih = jnp.arange(LANES) % N_HEAD_DIM
        inv_row = jnp.where(lih < ROPE_DIM, inv[lih % half], 0.0)[None, :].astype(F32)
        pos_col = positions.astype(F32).reshape(b * t, 1)
        gate_bias = jnp.zeros((1, LANES), F32).at[0, 8:8 + 3 * N_HEADS].set(b_nsa_gate[l])
        qraw, qrope, kst, vs, kwt, vw, gates = _kvprep(p, b, t, pos_col, inv_row, gate_bias)

        ci = jnp.arange(nr)[:, None] * CMP_STRIDE
        sj = jnp.arange(LANES)[None, :] * SEL_BLOCK
        overlap = ((ci < sj + SEL_BLOCK) & (ci + CMP_BLOCK > sj) & (jnp.arange(nr)[:, None] < nr - 1)
                   & (jnp.arange(LANES)[None, :] < t // SEL_BLOCK)).astype(BF16)
        yb = _nsa(p, b, t, qraw, qrope, kct, vc, kst, vs, kwt, vw, gates, overlap)

        x2 = _outproj(ya, yb, p, x2, w_branch_a[l].astype(BF16), w_branch_b[l].astype(BF16),
                      w_out[l].astype(BF16), final_norm_g[None, :], l == norm_g.shape[0] - 1)
    return x2.reshape(b, t, D_MODEL)
```

```python
import functools
import math

import jax
import jax.numpy as jnp
from jax import lax
from jax.experimental import pallas as pl
from jax.experimental.pallas import tpu as pltpu

F32 = jnp.float32
BF16 = jnp.bfloat16

D_MODEL = 1024
M_HEADS = 4
M_HEAD_DIM = 256
M_WIDTH = M_HEADS * M_HEAD_DIM
CONV_WIDTH = 4
N_HEADS = 8
N_KV_GROUPS = 2
N_HPG = N_HEADS // N_KV_GROUPS
N_HEAD_DIM = 64
N_WIDTH = N_HEADS * N_HEAD_DIM
CMP_BLOCK = 32
CMP_STRIDE = 16
CMP_HIDDEN = 128
SEL_BLOCK = 64
SEL_TOPK = 16
WINDOW = 512
Q_BLOCK = 128
ROPE_THETA = 500000.0
ROPE_DIM = N_HEAD_DIM // 4
NORM_EPS = 1e-6
NEG = -1e30
FORCE = 1e9

LANES = 128
MLSTM_CHUNK = 256
SEL_CHUNK = 512
VMEM_LIMIT = 56 * 1024 * 1024

C_MQ, C_MK, C_MV, C_MO, C_MZ, C_GA, C_GB = 0, 1024, 2048, 3072, 4096, 5120, 6144
C_NQ, C_NZ = 7168, 7680
C_KV = 8192
C_CMP = 8704
C_SM = 8960
P_W = 9216
_O_MI, _O_NQ, _O_KC, _O_KS, _O_NG, _O_NZ, _O_GA, _O_END = 5120, 5128, 5640, 5896, 6408, 6432, 6944, 8992


def _cparams(sem):
    return pltpu.CompilerParams(dimension_semantics=sem, vmem_limit_bytes=VMEM_LIMIT)


def _sigmoid(x):
    return 1.0 / (1.0 + jnp.exp(-x))


def _silu(x):
    return x * _sigmoid(x)


def _log_sigmoid(x):
    return jnp.minimum(x, 0.0) - jnp.log1p(jnp.exp(-jnp.abs(x)))


def _inproj_kernel(x_ref, g_ref, w_ref, o_ref, hn_ref):
    @pl.when(pl.program_id(1) == 0)
    def _():
        x = x_ref[...]
        r = lax.rsqrt(jnp.mean(x * x, axis=-1, keepdims=True) + NORM_EPS)
        hn_ref[...] = (x * r * g_ref[...]).astype(BF16)

    o_ref[...] = jnp.dot(hn_ref[...], w_ref[...], preferred_element_type=F32)


def _inproj(x2, norm_g, w_perm):
    m = x2.shape[0]
    tm, tn = 1024, 1024
    return pl.pallas_call(
        _inproj_kernel,
        out_shape=jax.ShapeDtypeStruct((m, P_W), F32),
        grid=(m // tm, P_W // tn),
        in_specs=[
            pl.BlockSpec((tm, D_MODEL), lambda i, j: (i, 0)),
            pl.BlockSpec((1, D_MODEL), lambda i, j: (0, 0)),
            pl.BlockSpec((D_MODEL, tn), lambda i, j: (0, j)),
        ],
        out_specs=pl.BlockSpec((tm, tn), lambda i, j: (i, j)),
        scratch_shapes=[pltpu.VMEM((tm, D_MODEL), BF16)],
        compiler_params=_cparams(("parallel", "arbitrary")),
        name="inproj",
    )(x2, norm_g, w_perm)


def _mlstm_kernel(q_ref, k_ref, v_ref, og_ref, z_ref, sm_ref, cqw_ref, cqb_ref, ckw_ref, ckb_ref,
                  sb_ref, ng_ref, ya_ref, qbuf, kbuf, c_state, n_state, m_state):
    L = q_ref.shape[0]
    d = M_HEAD_DIM

    @pl.when(pl.program_id(1) == 0)
    def _():
        qbuf[0:8, :] = jnp.zeros((8, M_WIDTH), F32)
        kbuf[0:8, :] = jnp.zeros((8, M_WIDTH), F32)
        c_state[...] = jnp.zeros_like(c_state)
        n_state[...] = jnp.zeros_like(n_state)
        m_state[...] = jnp.zeros_like(m_state)

    qbuf[8:8 + L, :] = q_ref[...]
    kbuf[8:8 + L, :] = k_ref[...]

    def conv(buf, w_ref, b_ref):
        acc = b_ref[...] + buf[5:5 + L, :] * w_ref[0:1, :]
        for j in range(1, CONV_WIDTH):
            acc = acc + buf[5 + j:5 + j + L, :] * w_ref[j:j + 1, :]
        return _silu(acc)

    mq = conv(qbuf, cqw_ref, cqb_ref)
    mk = conv(kbuf, ckw_ref, ckb_ref) * (d ** -0.5)
    qbuf[0:8, :] = qbuf[L:L + 8, :]
    kbuf[0:8, :] = kbuf[L:L + 8, :]

    sm = sm_ref[...] + sb_ref[...]
    lf = _log_sigmoid(sm)
    row = lax.broadcasted_iota(jnp.int32, (L, L), 0)
    col = lax.broadcasted_iota(jnp.int32, (L, L), 1)
    causal = row >= col
    tril = causal.astype(F32)
    triu = (row <= col).astype(F32)
    a_col = jnp.dot(tril, lf, precision=lax.Precision.HIGHEST, preferred_element_type=F32)
    sm_t = sm.T[0:8, :]
    a_row = jnp.dot(_log_sigmoid(sm_t), triu, precision=lax.Precision.HIGHEST,
                    preferred_element_type=F32)

    v = v_ref[...]
    for h in range(M_HEADS):
        hs = slice(h * d, (h + 1) * d)
        a_c = a_col[:, 4 + h:5 + h]
        li_c = sm[:, h:h + 1]
        a_r = a_row[4 + h:5 + h, :]
        li_r = sm_t[h:h + 1, :]
        m_prev = m_state[h:h + 1, 0:1]
        a_end = a_c[L - 1:L, :]

        log_d = jnp.where(causal, a_c - a_r + li_r, -jnp.inf)
        inter = a_c + m_prev
        m_t = jnp.maximum(inter, jnp.max(log_d, axis=1, keepdims=True))
        w_inter = jnp.exp(inter - m_t)
        dm = jnp.exp(log_d - m_t)

        qh = mq[:, hs]
        kh = mk[:, hs]
        qb = qh.astype(BF16)
        kb = kh.astype(BF16)
        vb = v[:, hs].astype(BF16)
        s = lax.dot_general(qb, kb, (((1,), (1,)), ((), ())), preferred_element_type=F32) * dm
        c_old = c_state[h]
        n_old = n_state[h:h + 1, :]
        num = (w_inter * jnp.dot(qb, c_old.astype(BF16), preferred_element_type=F32)
               + jnp.dot(s.astype(BF16), vb, preferred_element_type=F32))
        den = (w_inter * jnp.sum(qh * n_old, axis=1, keepdims=True)
               + jnp.sum(s, axis=1, keepdims=True))
        hh = num / jnp.maximum(jnp.abs(den), jnp.exp(-m_t))

        logw = a_end - a_c + li_c
        m_new = jnp.maximum(a_end + m_prev, jnp.max(logw, axis=0, keepdims=True))
        wk = jnp.exp(logw - m_new)
        decay = jnp.exp(a_end + m_prev - m_new)
        kw = kh * wk
        c_state[h] = decay * c_old + lax.dot_general(kw.astype(BF16), vb, (((0,), (0,)), ((), ())),
                                                     preferred_element_type=F32)
        n_state[h:h + 1, :] = decay * n_old + jnp.sum(kw, axis=0, keepdims=True)
        m_state[h:h + 1, :] = jnp.broadcast_to(m_new, (1, LANES))

        mu = jnp.mean(hh, axis=1, keepdims=True)
        dc = hh - mu
        var = jnp.mean(dc * dc, axis=1, keepdims=True)
        hn = dc * lax.rsqrt(var + NORM_EPS) * ng_ref[:, hs]
        ya = _sigmoid(og_ref[:, hs]) * hn * _silu(z_ref[:, hs])
        ya_ref[:, hs] = ya.astype(BF16)


def _mlstm(p, b, t, conv_q_w, conv_q_b, conv_k_w, conv_k_b, sm_bias, mh_norm_g):
    L = MLSTM_CHUNK
    nc = t // L

    def colspec(cstart):
        return pl.BlockSpec((L, M_WIDTH), lambda bi, c: (bi * nc + c, cstart // M_WIDTH))

    def full(shape):
        return pl.BlockSpec(shape, lambda bi, c: (0,) * len(shape))

    return pl.pallas_call(
        _mlstm_kernel,
        out_shape=jax.ShapeDtypeStruct((b * t, M_WIDTH), BF16),
        grid=(b, nc),
        in_specs=[
            colspec(C_MQ), colspec(C_MK), colspec(C_MV), colspec(C_MO), colspec(C_MZ),
            pl.BlockSpec((L, LANES), lambda bi, c: (bi * nc + c, C_SM // LANES)),
            full((CONV_WIDTH, M_WIDTH)), full((1, M_WIDTH)), full((CONV_WIDTH, M_WIDTH)), full((1, M_WIDTH)),
            full((1, LANES)), full((1, M_WIDTH)),
        ],
        out_specs=pl.BlockSpec((L, M_WIDTH), lambda bi, c: (bi * nc + c, 0)),
        scratch_shapes=[
            pltpu.VMEM((L + 8, M_WIDTH), F32), pltpu.VMEM((L + 8, M_WIDTH), F32),
            pltpu.VMEM((M_HEADS, M_HEAD_DIM, M_HEAD_DIM), F32),
            pltpu.VMEM((8, M_HEAD_DIM), F32), pltpu.VMEM((8, LANES), F32),
        ],
        compiler_params=_cparams(("parallel", "arbitrary")),
        name="mlstm",
    )(p, p, p, p, p, p, conv_q_w, conv_q_b, conv_k_w, conv_k_b, sm_bias, mh_norm_g)


def _compress_kernel(r_ref, wt_ref, wb_ref, posk_ref, posv_ref, w1k_ref, w1v_ref, w2k_ref, w2v_ref,
                     kc_ref, vct_ref):
    r = r_ref[...].astype(BF16)
    nr = r.shape[0]
    u = jnp.dot(r, wt_ref[...], preferred_element_type=F32)
    v = jnp.dot(r, wb_ref[...], preferred_element_type=F32)
    v = pltpu.roll(v, nr - 1, axis=0)
    rowi = lax.broadcasted_iota(jnp.int32, v.shape, 0)
    v = jnp.where(rowi < nr - 1, v, 0.0)

    def pos_term(pos_ref, w_ref):
        pb = jnp.broadcast_to(pos_ref[...].astype(BF16), (8, CMP_BLOCK * N_HEAD_DIM))
        return jnp.dot(pb, w_ref[...], preferred_element_type=F32)[0:1, :]

    pk = pos_term(posk_ref, w1k_ref)
    pv = pos_term(posv_ref, w1v_ref)
    hid = u + v + jnp.concatenate([pk, pk, pv, pv], axis=1)
    act = 0.5 * hid * (1.0 + jnp.tanh(math.sqrt(2.0 / math.pi) * (hid + 0.044715 * (hid * hid * hid))))
    for c in range(2 * N_KV_GROUPS):
        w2 = w2k_ref[...] if c < N_KV_GROUPS else w2v_ref[...]
        out = jnp.dot(act[:, c * CMP_HIDDEN:(c + 1) * CMP_HIDDEN].astype(BF16), w2,
                      preferred_element_type=F32)
        if c < N_KV_GROUPS:
            kc_ref[c] = out.astype(BF16)
        else:
            vct_ref[c - N_KV_GROUPS] = out.T.astype(BF16)


def _compress(r, wt, wb, posk, posv, w1k, w1v, w2k, w2v):
    b, nr, rw = r.shape
    g = N_KV_GROUPS

    def full(a):
        return pl.BlockSpec(a.shape, lambda bi: (0,) * a.ndim)

    return pl.pallas_call(
        _compress_kernel,
        out_shape=(jax.ShapeDtypeStruct((b, g, nr, LANES), BF16), jax.ShapeDtypeStruct((b, g, LANES, nr), BF16)),
        grid=(b,),
        in_specs=[pl.BlockSpec((None, nr, rw), lambda bi: (bi, 0, 0)),
                  full(wt), full(wb), full(posk), full(posv), full(w1k), full(w1v), full(w2k), full(w2v)],
        out_specs=(pl.BlockSpec((None, g, nr, LANES), lambda bi: (bi, 0, 0, 0)),
                   pl.BlockSpec((None, g, LANES, nr), lambda bi: (bi, 0, 0, 0))),
        compiler_params=_cparams(("parallel",)),
        name="compress",
    )(r, wt, wb, posk, posv, w1k, w1v, w2k, w2v)


def _kvprep_kernel(q_ref, kv_ref, sm_ref, pos_ref, inv_ref, gb_ref,
                   qrawt_ref, qrope_ref, kst_ref, vs_ref, kwt_ref, vw_ref, gates_ref):
    tk = q_ref.shape[0]
    lane = lax.broadcasted_iota(jnp.int32, (tk, LANES), 1)
    lih = lane % N_HEAD_DIM
    ang = pos_ref[...] * inv_ref[...]
    cos = jnp.cos(ang)
    sin = jnp.sin(ang)
    half = ROPE_DIM // 2
    s_up = jnp.where(lih < half, -sin, 0.0)
    s_dn = jnp.where((lih >= half) & (lih < ROPE_DIM), sin, 0.0)

    def rope(u):
        return u * cos + pltpu.roll(u, LANES - half, axis=1) * s_up + pltpu.roll(u, half, axis=1) * s_dn

    def head_pad(u, odd):
        if odd:
            u = pltpu.roll(u, N_HEAD_DIM, axis=1)
        return jnp.where(lane < N_HEAD_DIM, u, 0.0)

    scale = N_HEAD_DIM ** -0.5
    zeros_t = jnp.zeros((N_HEAD_DIM, tk), BF16)
    for g in range(N_KV_GROUPS):
        for hp in range(N_HPG // 2):
            c0 = g * (N_HPG * N_HEAD_DIM) + hp * LANES
            u = q_ref[:, c0:c0 + LANES] * scale
            ur = rope(u)
            u_t = u.T.astype(BF16)
            for odd in range(2):
                h = 2 * hp + odd
                qrawt_ref[g, h * LANES:h * LANES + N_HEAD_DIM, :] = u_t[odd * N_HEAD_DIM:(odd + 1) * N_HEAD_DIM, :]
                qrawt_ref[g, h * LANES + N_HEAD_DIM:(h + 1) * LANES, :] = zeros_t
                qrope_ref[g, :, h * LANES:(h + 1) * LANES] = head_pad(ur, odd).astype(BF16)

    ks_t = rope(kv_ref[:, 0:LANES]).T
    kw_t = rope(kv_ref[:, 2 * LANES:3 * LANES]).T
    vs = kv_ref[:, LANES:2 * LANES]
    vw = kv_ref[:, 3 * LANES:4 * LANES]
    t_glob = pl.program_id(1) * tk + lax.broadcasted_iota(jnp.int32, (LANES, tk), 1)
    blk_row = lax.broadcasted_iota(jnp.int32, (LANES, tk), 0)
    onehot = jnp.where(blk_row == t_glob // SEL_BLOCK, 1.0, 0.0).astype(BF16)
    ones_lane = jnp.where(lane == N_HEAD_DIM, 1.0, 0.0)
    gates = _sigmoid(sm_ref[...] + gb_ref[...])
    for g in range(N_KV_GROUPS):
        rs = slice(g * N_HEAD_DIM, (g + 1) * N_HEAD_DIM)
        kst_ref[g, 0:LANES, :] = onehot
        kst_ref[g, LANES:LANES + N_HEAD_DIM, :] = ks_t[rs, :].astype(BF16)
        kst_ref[g, LANES + N_HEAD_DIM:2 * LANES, :] = zeros_t
        kwt_ref[g, 0:N_HEAD_DIM, :] = kw_t[rs, :].astype(BF16)
        kwt_ref[g, N_HEAD_DIM:LANES, :] = zeros_t
        vs_ref[g] = (head_pad(vs, g) + ones_lane).astype(BF16)
        vw_ref[g] = (head_pad(vw, g) + ones_lane).astype(BF16)
        goff = 8 + g * (3 * N_HPG)
        gates_ref[g] = pltpu.roll(gates, LANES - goff, axis=1)


def _kvprep(p, b, t, pos_col, inv_row, gate_bias):
    tk = 512
    nt = t // tk
    g = N_KV_GROUPS
    qw = N_HPG * LANES

    def tok(shape):
        return pl.BlockSpec((None, g) + shape, lambda bi, i: (bi, 0, i, 0))

    def tok_t(rows):
        return pl.BlockSpec((None, g, rows, tk), lambda bi, i: (bi, 0, 0, i))

    return pl.pallas_call(
        _kvprep_kernel,
        out_shape=(
            jax.ShapeDtypeStruct((b, g, qw, t), BF16), jax.ShapeDtypeStruct((b, g, t, qw), BF16),
            jax.ShapeDtypeStruct((b, g, 2 * LANES, t), BF16), jax.ShapeDtypeStruct((b, g, t, LANES), BF16),
            jax.ShapeDtypeStruct((b, g, LANES, t), BF16), jax.ShapeDtypeStruct((b, g, t, LANES), BF16),
            jax.ShapeDtypeStruct((b, g, t, LANES), F32),
        ),
        grid=(b, nt),
        in_specs=[
            pl.BlockSpec((tk, N_WIDTH), lambda bi, i: (bi * nt + i, C_NQ // N_WIDTH)),
            pl.BlockSpec((tk, 4 * LANES), lambda bi, i: (bi * nt + i, C_KV // (4 * LANES))),
            pl.BlockSpec((tk, LANES), lambda bi, i: (bi * nt + i, C_SM // LANES)),
            pl.BlockSpec((tk, 1), lambda bi, i: (bi * nt + i, 0)),
            pl.BlockSpec((1, LANES), lambda bi, i: (0, 0)),
            pl.BlockSpec((1, LANES), lambda bi, i: (0, 0)),
        ],
        out_specs=(tok_t(qw), tok((tk, qw)), tok_t(2 * LANES), tok((tk, LANES)),
                   tok_t(LANES), tok((tk, LANES)), tok((tk, LANES))),
        compiler_params=_cparams(("parallel", "arbitrary")),
        name="kvprep",
    )(p, p, p, pos_col, inv_row, gate_bias)


def _select_kernel(qt_ref, kc_ref, vct_ref, ovt_ref, oc_ref, bias_ref):
    qs = qt_ref.shape[1]
    nr = kc_ref.shape[0]
    t0 = pl.program_id(2) * qs
    m_c = (lax.broadcasted_iota(jnp.int32, (nr, qs), 0) * CMP_STRIDE + (CMP_BLOCK - 1)
           <= t0 + lax.broadcasted_iota(jnp.int32, (nr, qs), 1))
    kc = kc_ref[...]
    vct = vct_ref[...]
    psum = jnp.zeros((nr, qs), F32)
    for h in range(N_HPG):
        s = jnp.dot(kc, qt_ref[h * LANES:(h + 1) * LANES, :], preferred_element_type=F32)
        s = jnp.where(m_c, s, NEG)
        e = jnp.where(m_c, jnp.exp(s - jnp.max(s, axis=0, keepdims=True)), 0.0)
        den = jnp.sum(e, axis=0, keepdims=True)
        pc = e / jnp.where(den == 0.0, 1.0, den)
        psum = psum + pc
        oc_t = jnp.dot(vct, pc.astype(BF16), preferred_element_type=F32)
        oc_ref[:, h * LANES:(h + 1) * LANES] = oc_t.T
    p_hi = psum.astype(BF16)
    p_lo = (psum - p_hi.astype(F32)).astype(BF16)
    ovt = ovt_ref[...]
    imp = jnp.dot(ovt, p_hi, preferred_element_type=F32) + jnp.dot(ovt, p_lo, preferred_element_type=F32)

    blk = lax.broadcasted_iota(jnp.int32, (LANES, qs), 0)
    tq = t0 + lax.broadcasted_iota(jnp.int32, (LANES, qs), 1)
    cur = tq // SEL_BLOCK
    forced = (blk == 0) | (blk == cur) | (blk == cur - 1)
    n_sel = (nr * CMP_STRIDE) // SEL_BLOCK
    score = jnp.where(forced, FORCE, jnp.where(blk * SEL_BLOCK <= tq, imp, -1.0))
    score = jnp.where(blk < n_sel, score, -jnp.inf)
    sel = jnp.zeros((LANES, qs), F32)
    for _ in range(min(SEL_TOPK, n_sel)):
        mx = jnp.max(score, axis=0, keepdims=True)
        idx = jnp.min(jnp.where(score == mx, blk, LANES), axis=0, keepdims=True)
        pick = blk == idx
        sel = jnp.where(pick, 1.0, sel)
        score = jnp.where(pick, -jnp.inf, score)
    cutoff = (tq // Q_BLOCK) * (Q_BLOCK // SEL_BLOCK)
    bias_t = jnp.where((sel > 0.0) & (blk < cutoff), 0.0, NEG)
    bias_ref[...] = bias_t.T.astype(BF16)


def _select(qrawt, kc, vct, ovt):
    b, g, qw, t = qrawt.shape
    nr = kc.shape[2]
    qs = 512

    def whole(rows, cols):
        return pl.BlockSpec((None, None, rows, cols), lambda bi, gi, i: (bi, gi, 0, 0))

    return pl.pallas_call(
        _select_kernel,
        out_shape=(jax.ShapeDtypeStruct((b, g, t, qw), F32), jax.ShapeDtypeStruct((b, g, t, LANES), BF16)),
        grid=(b, g, t // qs),
        in_specs=[
            pl.BlockSpec((None, None, qw, qs), lambda bi, gi, i: (bi, gi, 0, i)),
            whole(nr, LANES), whole(LANES, nr),
            pl.BlockSpec(ovt.shape, lambda bi, gi, i: (0, 0)),
        ],
        out_specs=(pl.BlockSpec((None, None, qs, qw), lambda bi, gi, i: (bi, gi, i, 0)),
                   pl.BlockSpec((None, None, qs, LANES), lambda bi, gi, i: (bi, gi, i, 0))),
        compiler_params=_cparams(("parallel", "parallel", "parallel")),
        name="select",
    )(qrawt, kc, vct, ovt)


def _nsa_kernel(qrope_ref, bias_ref, oc_ref, kst_ref, vs_ref, kwt_ref, vw_ref, gates_ref, nz_ref,
                yb_ref, qaug_ref, acc_ref, m_ref):
    i = pl.program_id(1)
    t0 = i * Q_BLOCK
    Q = Q_BLOCK
    nh = N_HPG
    ng = N_KV_GROUPS
    rows = nh * Q
    t0a = pl.multiple_of(t0, Q)
    rq = lax.broadcasted_iota(jnp.int32, (rows, Q), 0) % Q
    ck = lax.broadcasted_iota(jnp.int32, (rows, Q), 1)
    lane = lax.broadcasted_iota(jnp.int32, (Q, LANES), 1)

    def q_rope(g):
        return jnp.concatenate([qrope_ref[g, :, h * LANES:(h + 1) * LANES] for h in range(nh)], axis=0)

    for g in range(ng):
        qr = q_rope(g)
        qaug_ref[g, :, 0:LANES] = jnp.concatenate([bias_ref[g]] * nh, axis=0)
        qaug_ref[g, :, LANES:2 * LANES] = qr
        s_d = jnp.dot(qr, kst_ref[g, LANES:2 * LANES, pl.ds(t0a, Q)], preferred_element_type=F32)
        s_d = jnp.where(ck <= rq, s_d, NEG)
        m0 = jnp.max(s_d, axis=1, keepdims=True)
        p_d = jnp.exp(s_d - m0)
        m_ref[g] = jnp.broadcast_to(m0, (rows, LANES))
        acc_ref[g] = jnp.dot(p_d.astype(BF16), vs_ref[g, pl.ds(t0a, Q), :], preferred_element_type=F32)

    ncb = SEL_CHUNK // LANES

    def sel_step(c, carry):
        k0 = pl.multiple_of(c * SEL_CHUNK, SEL_CHUNK)
        for g in range(ng):
            s = jnp.dot(qaug_ref[g], kst_ref[g, :, pl.ds(k0, SEL_CHUNK)], preferred_element_type=F32)
            m_old = m_ref[g]
            m_new = jnp.maximum(m_old, jnp.max(s, axis=1, keepdims=True))
            p = jnp.concatenate([jnp.exp(s[:, cb * LANES:(cb + 1) * LANES] - m_new) for cb in range(ncb)], axis=1)
            acc_ref[g] = (jnp.exp(m_old - m_new) * acc_ref[g]
                          + jnp.dot(p.astype(BF16), vs_ref[g, pl.ds(k0, SEL_CHUNK), :],
                                    preferred_element_type=F32))
            m_ref[g] = m_new
        return carry

    lax.fori_loop(0, (t0 + SEL_CHUNK - 1) // SEL_CHUNK, sel_step, 0)

    wlen = WINDOW + Q
    ws = pl.multiple_of(jnp.maximum(t0 - WINDOW, 0), Q)
    tq_w = t0 + lax.broadcasted_iota(jnp.int32, (rows, wlen), 0) % Q
    diff = tq_w - (ws + lax.broadcasted_iota(jnp.int32, (rows, wlen), 1))
    m_w = (diff >= 0) & (diff < WINDOW)
    for g in range(ng):
        acc = acc_ref[g]
        o_s = acc / acc[:, N_HEAD_DIM:N_HEAD_DIM + 1]
        s_w = jnp.dot(q_rope(g), kwt_ref[g, :, pl.ds(ws, wlen)], preferred_element_type=F32)
        s_w = jnp.where(m_w, s_w, NEG)
        p_w = jnp.exp(s_w - jnp.max(s_w, axis=1, keepdims=True))
        acc_w = jnp.dot(p_w.astype(BF16), vw_ref[g, pl.ds(ws, wlen), :], preferred_element_type=F32)
        o_w = acc_w / acc_w[:, N_HEAD_DIM:N_HEAD_DIM + 1]
        gates = gates_ref[g]
        outs = []
        for h in range(nh):
            rs = slice(h * Q, (h + 1) * Q)
            outs.append(gates[:, 3 * h:3 * h + 1] * oc_ref[g, :, h * LANES:(h + 1) * LANES]
                        + gates[:, 3 * h + 1:3 * h + 2] * o_s[rs, :]
                        + gates[:, 3 * h + 2:3 * h + 3] * o_w[rs, :])
        for hp in range(nh // 2):
            yn = jnp.where(lane < N_HEAD_DIM, outs[2 * hp], pltpu.roll(outs[2 * hp + 1], N_HEAD_DIM, axis=1))
            ls = slice((g * (nh // 2) + hp) * LANES, (g * (nh // 2) + hp + 1) * LANES)
            yb_ref[:, ls] = (yn * _silu(nz_ref[:, ls])).astype(BF16)


def _nsa(p, b, t, qrope, bias, oc, kst, vs, kwt, vw, gates):
    g = N_KV_GROUPS
    nqb = t // Q_BLOCK
    qw = N_HPG * LANES
    rows = N_HPG * Q_BLOCK

    def qblk(w):
        return pl.BlockSpec((None, g, Q_BLOCK, w), lambda bi, i: (bi, 0, i, 0))

    def whole(r, c):
        return pl.BlockSpec((None, g, r, c), lambda bi, i: (bi, 0, 0, 0), pipeline_mode=pl.Buffered(1))

    return pl.pallas_call(
        _nsa_kernel,
        out_shape=jax.ShapeDtypeStruct((b * t, N_WIDTH), BF16),
        grid=(b, nqb),
        in_specs=[
            qblk(qw), qblk(LANES), qblk(qw),
            whole(2 * LANES, t), whole(t, LANES), whole(LANES, t), whole(t, LANES),
            qblk(LANES),
            pl.BlockSpec((Q_BLOCK, N_WIDTH), lambda bi, i: (bi * nqb + i, C_NZ // N_WIDTH)),
        ],
        out_specs=pl.BlockSpec((Q_BLOCK, N_WIDTH), lambda bi, i: (bi * nqb + i, 0)),
        scratch_shapes=[pltpu.VMEM((g, rows, 2 * LANES), BF16), pltpu.VMEM((g, rows, LANES), F32),
                        pltpu.VMEM((g, rows, LANES), F32)],
        compiler_params=_cparams(("parallel", "arbitrary")),
        name="nsa",
    )(qrope, bias, oc, kst, vs, kwt, vw, gates, p)


def _outproj_kernel(final, ya_ref, yb_ref, ga_ref, gb_ref, x_ref, wa_ref, wb_ref, wo_ref, fg_ref, o_ref):
    a = jnp.dot(ya_ref[...], wa_ref[...], preferred_element_type=F32)
    bq = jnp.dot(yb_ref[...], wb_ref[...], preferred_element_type=F32)
    merged = _sigmoid(ga_ref[...]) * a + _sigmoid(gb_ref[...]) * bq
    y = x_ref[...] + jnp.dot(merged.astype(BF16), wo_ref[...], preferred_element_type=F32)
    if final:
        y = y * lax.rsqrt(jnp.mean(y * y, axis=-1, keepdims=True) + NORM_EPS) * fg_ref[...]
    o_ref[...] = y


def _outproj(ya, yb, p, x2, wa, wb, wo, fg, final):
    m = x2.shape[0]
    tm = 512

    def full(a):
        return pl.BlockSpec(a.shape, lambda i: (0, 0))

    return pl.pallas_call(
        functools.partial(_outproj_kernel, final),
        out_shape=jax.ShapeDtypeStruct((m, D_MODEL), F32),
        grid=(m // tm,),
        in_specs=[
            pl.BlockSpec((tm, M_WIDTH), lambda i: (i, 0)),
            pl.BlockSpec((tm, N_WIDTH), lambda i: (i, 0)),
            pl.BlockSpec((tm, D_MODEL), lambda i: (i, C_GA // D_MODEL)),
            pl.BlockSpec((tm, D_MODEL), lambda i: (i, C_GB // D_MODEL)),
            pl.BlockSpec((tm, D_MODEL), lambda i: (i, 0)),
            full(wa), full(wb), full(wo), full(fg),
        ],
        out_specs=pl.BlockSpec((tm, D_MODEL), lambda i: (i, 0)),
        compiler_params=_cparams(("parallel",)),
        name="outproj",
    )(ya, yb, p, p, x2, wa, wb, wo, fg)


def _permute_w_in(w):
    pad = jnp.zeros((w.shape[0], P_W - _O_END), w.dtype)
    return jnp.concatenate([
        w[:, 0:_O_MI],
        w[:, _O_GA:_O_END],
        w[:, _O_NQ:_O_KC],
        w[:, _O_NZ:_O_GA],
        w[:, _O_KS:_O_NG],
        w[:, _O_KC:_O_KS],
        w[:, _O_MI:_O_NQ],
        w[:, _O_NG:_O_NZ],
        pad], axis=1)


def _expand_cmp_w1(w1k, w1v):
    half = CMP_BLOCK // 2
    k3 = w1k.reshape(CMP_BLOCK, N_HEAD_DIM, CMP_HIDDEN)
    v3 = w1v.reshape(CMP_BLOCK, N_HEAD_DIM, CMP_HIDDEN)
    outs = []
    for sl in (slice(0, half), slice(half, CMP_BLOCK)):
        z = jnp.zeros((half, 4, N_HEAD_DIM, 4, CMP_HIDDEN), F32)
        z = z.at[:, 0, :, 0, :].set(k3[sl]).at[:, 1, :, 1, :].set(k3[sl])
        z = z.at[:, 2, :, 2, :].set(v3[sl]).at[:, 3, :, 3, :].set(v3[sl])
        outs.append(z.reshape(half * 4 * N_HEAD_DIM, 4 * CMP_HIDDEN).astype(BF16))
    return outs


def kernel(x, positions, norm_g, w_in, conv_q_w, conv_q_b, conv_k_w, conv_k_b, b_igate, b_fgate, mh_norm_g,
           cmp_pos_k, cmp_w1_k, cmp_w2_k, cmp_pos_v, cmp_w1_v, cmp_w2_v, b_nsa_gate, w_branch_a, w_branch_b,
           w_out, final_norm_g):
    b, t, _ = x.shape
    assert t % SEL_CHUNK == 0 and t >= WINDOW + Q_BLOCK and t // SEL_BLOCK <= LANES and t % MLSTM_CHUNK == 0
    x2 = x.reshape(b * t, D_MODEL)
    for l in range(norm_g.shape[0]):
        p = _inproj(x2, norm_g[l][None, :], _permute_w_in(w_in[l]).astype(BF16))

        sm_bias = jnp.zeros((1, LANES), F32).at[0, 0:M_HEADS].set(b_igate[l]).at[0, M_HEADS:2 * M_HEADS].set(b_fgate[l])
        ya = _mlstm(p, b, t, conv_q_w[l], conv_q_b[l][None, :], conv_k_w[l], conv_k_b[l][None, :], sm_bias,
                    mh_norm_g[l][None, :])

        nr = t // CMP_STRIDE
        r = p.reshape(b, t, P_W)[:, :, C_CMP:C_CMP + 2 * LANES].reshape(b, nr, CMP_STRIDE * 2 * LANES)
        wt, wb = _expand_cmp_w1(cmp_w1_k[l], cmp_w1_v[l])
        w2pad = lambda w: jnp.pad(w, ((0, 0), (0, LANES - N_HEAD_DIM))).astype(BF16)
        kc, vct = _compress(r, wt, wb, cmp_pos_k[l].reshape(1, -1), cmp_pos_v[l].reshape(1, -1),
                            cmp_w1_k[l].astype(BF16), cmp_w1_v[l].astype(BF16), w2pad(cmp_w2_k[l]), w2pad(cmp_w2_v[l]))

        half = ROPE_DIM // 2
        inv = jnp.power(jnp.float32(ROPE_THETA), -jnp.arange(half, dtype=F32) * (2.0 / ROPE_DIM))
        lih = jnp.arange(LANES) % N_HEAD_DIM
        inv_row = jnp.where(lih < ROPE_DIM, inv[lih % half], 0.0)[None, :].astype(F32)
        pos_col = positions.astype(F32).reshape(b * t, 1)
        gate_bias = jnp.zeros((1, LANES), F32).at[0, 8:8 + 3 * N_HEADS].set(b_nsa_gate[l])
        qrawt, qrope, kst, vs, kwt, vw, gates = _kvprep(p, b, t, pos_col, inv_row, gate_bias)

        ci = jnp.arange(nr)[None, :] * CMP_STRIDE
        sj = jnp.arange(LANES)[:, None] * SEL_BLOCK
        overlap_t = ((ci < sj + SEL_BLOCK) & (ci + CMP_BLOCK > sj) & (jnp.arange(nr)[None, :] < nr - 1)
                     & (jnp.arange(LANES)[:, None] < t // SEL_BLOCK)).astype(BF16)
        oc, bias = _select(qrawt, kc, vct, overlap_t)
        yb = _nsa(p, b, t, qrope, bias, oc, kst, vs, kwt, vw, gates)

        x2 = _outproj(ya, yb, p, x2, w_branch_a[l].astype(BF16), w_branch_b[l].astype(BF16),
                      w_out[l].astype(BF16), final_norm_g[None, :], l == norm_g.shape[0] - 1)
    return x2.reshape(b, t, D_MODEL)
```

```python
import functools
import math

import jax
import jax.numpy as jnp
from jax import lax
from jax.experimental import pallas as pl
from jax.experimental.pallas import tpu as pltpu

F32 = jnp.float32
BF16 = jnp.bfloat16

D_MODEL = 1024
M_HEADS = 4
M_HEAD_DIM = 256
M_WIDTH = M_HEADS * M_HEAD_DIM
CONV_WIDTH = 4
N_HEADS = 8
N_KV_GROUPS = 2
N_HPG = N_HEADS // N_KV_GROUPS
N_HEAD_DIM = 64
N_WIDTH = N_HEADS * N_HEAD_DIM
CMP_BLOCK = 32
CMP_STRIDE = 16
CMP_HIDDEN = 128
SEL_BLOCK = 64
SEL_TOPK = 16
WINDOW = 512
Q_BLOCK = 128
ROPE_THETA = 500000.0
ROPE_DIM = N_HEAD_DIM // 4
NORM_EPS = 1e-6
NEG = -1e30
FORCE = 1e9

LANES = 128
MLSTM_CHUNK = 256
SEL_CHUNK = 512
VMEM_LIMIT = 56 * 1024 * 1024

C_MQ, C_MK, C_MV, C_MO, C_MZ, C_GA, C_GB = 0, 1024, 2048, 3072, 4096, 5120, 6144
C_NQ, C_NZ = 7168, 7680
C_KV = 8192
C_CMP = 8704
C_SM = 8960
P_W = 9216
_O_MI, _O_NQ, _O_KC, _O_KS, _O_NG, _O_NZ, _O_GA, _O_END = 5120, 5128, 5640, 5896, 6408, 6432, 6944, 8992


def _cparams(sem):
    return pltpu.CompilerParams(dimension_semantics=sem, vmem_limit_bytes=VMEM_LIMIT)


def _sigmoid(x):
    return 0.5 * jnp.tanh(0.5 * x) + 0.5


def _silu(x):
    return x * _sigmoid(x)


def _log_sigmoid(x):
    return jnp.minimum(x, 0.0) - jnp.log1p(jnp.exp(-jnp.abs(x)))


def _inproj_kernel(x_ref, g_ref, w_ref, o_ref, hn_ref):
    @pl.when(pl.program_id(1) == 0)
    def _():
        x = x_ref[...]
        r = lax.rsqrt(jnp.mean(x * x, axis=-1, keepdims=True) + NORM_EPS)
        hn_ref[...] = (x * r * g_ref[...]).astype(BF16)

    o_ref[...] = jnp.dot(hn_ref[...], w_ref[...], preferred_element_type=F32)


def _inproj(x2, norm_g, w_perm):
    m = x2.shape[0]
    tm, tn = 1024, 1024
    return pl.pallas_call(
        _inproj_kernel,
        out_shape=jax.ShapeDtypeStruct((m, P_W), F32),
        grid=(m // tm, P_W // tn),
        in_specs=[
            pl.BlockSpec((tm, D_MODEL), lambda i, j: (i, 0)),
            pl.BlockSpec((1, D_MODEL), lambda i, j: (0, 0)),
            pl.BlockSpec((D_MODEL, tn), lambda i, j: (0, j)),
        ],
        out_specs=pl.BlockSpec((tm, tn), lambda i, j: (i, j)),
        scratch_shapes=[pltpu.VMEM((tm, D_MODEL), BF16)],
        compiler_params=_cparams(("parallel", "arbitrary")),
        name="inproj",
    )(x2, norm_g, w_perm)


def _mlstm_kernel(q_ref, k_ref, v_ref, og_ref, z_ref, sm_ref, cqw_ref, cqb_ref, ckw_ref, ckb_ref,
                  sb_ref, ng_ref, ya_ref, qbuf, kbuf, c_state, n_state, m_state):
    L = q_ref.shape[0]
    d = M_HEAD_DIM

    @pl.when(pl.program_id(1) == 0)
    def _():
        qbuf[0:8, :] = jnp.zeros((8, M_WIDTH), F32)
        kbuf[0:8, :] = jnp.zeros((8, M_WIDTH), F32)
        c_state[...] = jnp.zeros_like(c_state)
        n_state[...] = jnp.zeros_like(n_state)
        m_state[...] = jnp.zeros_like(m_state)

    qbuf[8:8 + L, :] = q_ref[...]
    kbuf[8:8 + L, :] = k_ref[...]

    def conv(buf, w_ref, b_ref):
        acc = b_ref[...] + buf[5:5 + L, :] * w_ref[0:1, :]
        for j in range(1, CONV_WIDTH):
            acc = acc + buf[5 + j:5 + j + L, :] * w_ref[j:j + 1, :]
        return _silu(acc)

    mq = conv(qbuf, cqw_ref, cqb_ref)
    mk = conv(kbuf, ckw_ref, ckb_ref) * (d ** -0.5)
    qbuf[0:8, :] = qbuf[L:L + 8, :]
    kbuf[0:8, :] = kbuf[L:L + 8, :]

    sm = sm_ref[...] + sb_ref[...]
    lf = _log_sigmoid(sm)
    row = lax.broadcasted_iota(jnp.int32, (L, L), 0)
    col = lax.broadcasted_iota(jnp.int32, (L, L), 1)
    causal = row >= col
    tril = causal.astype(F32)
    triu = (row <= col).astype(F32)
    a_col = jnp.dot(tril, lf, precision=lax.Precision.HIGHEST, preferred_element_type=F32)
    sm_t = sm.T[0:8, :]
    a_row = jnp.dot(_log_sigmoid(sm_t), triu, precision=lax.Precision.HIGHEST,
                    preferred_element_type=F32)

    v = v_ref[...]
    for h in range(M_HEADS):
        hs = slice(h * d, (h + 1) * d)
        a_c = a_col[:, 4 + h:5 + h]
        li_c = sm[:, h:h + 1]
        a_r = a_row[4 + h:5 + h, :]
        li_r = sm_t[h:h + 1, :]
        m_prev = m_state[h:h + 1, 0:1]
        a_end = a_c[L - 1:L, :]

        log_d = jnp.where(causal, a_c - a_r + li_r, -jnp.inf)
        inter = a_c + m_prev
        m_t = jnp.maximum(inter, jnp.max(log_d, axis=1, keepdims=True))
        w_inter = jnp.exp(inter - m_t)
        dm = jnp.exp(log_d - m_t)

        qh = mq[:, hs]
        kh = mk[:, hs]
        qb = qh.astype(BF16)
        kb = kh.astype(BF16)
        vb = v[:, hs].astype(BF16)
        s = lax.dot_general(qb, kb, (((1,), (1,)), ((), ())), preferred_element_type=F32) * dm
        c_old = c_state[h]
        n_old = n_state[h:h + 1, :]
        num = (w_inter * jnp.dot(qb, c_old.astype(BF16), preferred_element_type=F32)
               + jnp.dot(s.astype(BF16), vb, preferred_element_type=F32))
        den = (w_inter * jnp.sum(qh * n_old, axis=1, keepdims=True)
               + jnp.sum(s, axis=1, keepdims=True))
        hh = num * (1.0 / jnp.maximum(jnp.abs(den), jnp.exp(-m_t)))

        logw = a_end - a_c + li_c
        m_new = jnp.maximum(a_end + m_prev, jnp.max(logw, axis=0, keepdims=True))
        wk = jnp.exp(logw - m_new)
        decay = jnp.exp(a_end + m_prev - m_new)
        kw = kh * wk
        c_state[h] = decay * c_old + lax.dot_general(kw.astype(BF16), vb, (((0,), (0,)), ((), ())),
                                                     preferred_element_type=F32)
        n_state[h:h + 1, :] = decay * n_old + jnp.sum(kw, axis=0, keepdims=True)
        m_state[h:h + 1, :] = jnp.broadcast_to(m_new, (1, LANES))

        mu = jnp.mean(hh, axis=1, keepdims=True)
        dc = hh - mu
        var = jnp.mean(dc * dc, axis=1, keepdims=True)
        hn = dc * lax.rsqrt(var + NORM_EPS) * ng_ref[:, hs]
        ya = _sigmoid(og_ref[:, hs]) * hn * _silu(z_ref[:, hs])
        ya_ref[:, hs] = ya.astype(BF16)


def _mlstm(p, b, t, conv_q_w, conv_q_b, conv_k_w, conv_k_b, sm_bias, mh_norm_g):
    L = MLSTM_CHUNK
    nc = t // L

    def colspec(cstart):
        return pl.BlockSpec((L, M_WIDTH), lambda bi, c: (bi * nc + c, cstart // M_WIDTH))

    def full(shape):
        return pl.BlockSpec(shape, lambda bi, c: (0,) * len(shape))

    return pl.pallas_call(
        _mlstm_kernel,
        out_shape=jax.ShapeDtypeStruct((b * t, M_WIDTH), BF16),
        grid=(b, nc),
        in_specs=[
            colspec(C_MQ), colspec(C_MK), colspec(C_MV), colspec(C_MO), colspec(C_MZ),
            pl.BlockSpec((L, LANES), lambda bi, c: (bi * nc + c, C_SM // LANES)),
            full((CONV_WIDTH, M_WIDTH)), full((1, M_WIDTH)), full((CONV_WIDTH, M_WIDTH)), full((1, M_WIDTH)),
            full((1, LANES)), full((1, M_WIDTH)),
        ],
        out_specs=pl.BlockSpec((L, M_WIDTH), lambda bi, c: (bi * nc + c, 0)),
        scratch_shapes=[
            pltpu.VMEM((L + 8, M_WIDTH), F32), pltpu.VMEM((L + 8, M_WIDTH), F32),
            pltpu.VMEM((M_HEADS, M_HEAD_DIM, M_HEAD_DIM), F32),
            pltpu.VMEM((8, M_HEAD_DIM), F32), pltpu.VMEM((8, LANES), F32),
        ],
        compiler_params=_cparams(("parallel", "arbitrary")),
        name="mlstm",
    )(p, p, p, p, p, p, conv_q_w, conv_q_b, conv_k_w, conv_k_b, sm_bias, mh_norm_g)


def _compress_kernel(r_ref, wt_ref, wb_ref, posk_ref, posv_ref, w1k_ref, w1v_ref, w2k_ref, w2v_ref,
                     kc_ref, vct_ref):
    r = r_ref[...].astype(BF16)
    nr = r.shape[0]
    u = jnp.dot(r, wt_ref[...], preferred_element_type=F32)
    v = jnp.dot(r, wb_ref[...], preferred_element_type=F32)
    v = pltpu.roll(v, nr - 1, axis=0)
    rowi = lax.broadcasted_iota(jnp.int32, v.shape, 0)
    v = jnp.where(rowi < nr - 1, v, 0.0)

    def pos_term(pos_ref, w_ref):
        pb = jnp.broadcast_to(pos_ref[...].astype(BF16), (8, CMP_BLOCK * N_HEAD_DIM))
        return jnp.dot(pb, w_ref[...], preferred_element_type=F32)[0:1, :]

    pk = pos_term(posk_ref, w1k_ref)
    pv = pos_term(posv_ref, w1v_ref)
    hid = u + v + jnp.concatenate([pk, pk, pv, pv], axis=1)
    act = 0.5 * hid * (1.0 + jnp.tanh(math.sqrt(2.0 / math.pi) * (hid + 0.044715 * (hid * hid * hid))))
    for c in range(2 * N_KV_GROUPS):
        w2 = w2k_ref[...] if c < N_KV_GROUPS else w2v_ref[...]
        out = jnp.dot(act[:, c * CMP_HIDDEN:(c + 1) * CMP_HIDDEN].astype(BF16), w2,
                      preferred_element_type=F32)
        if c < N_KV_GROUPS:
            kc_ref[c] = out.astype(BF16)
        else:
            vct_ref[c - N_KV_GROUPS] = out.T.astype(BF16)


def _compress(r, wt, wb, posk, posv, w1k, w1v, w2k, w2v):
    b, nr, rw = r.shape
    g = N_KV_GROUPS

    def full(a):
        return pl.BlockSpec(a.shape, lambda bi: (0,) * a.ndim)

    return pl.pallas_call(
        _compress_kernel,
        out_shape=(jax.ShapeDtypeStruct((b, g, nr, LANES), BF16), jax.ShapeDtypeStruct((b, g, LANES, nr), BF16)),
        grid=(b,),
        in_specs=[pl.BlockSpec((None, nr, rw), lambda bi: (bi, 0, 0)),
                  full(wt), full(wb), full(posk), full(posv), full(w1k), full(w1v), full(w2k), full(w2v)],
        out_specs=(pl.BlockSpec((None, g, nr, LANES), lambda bi: (bi, 0, 0, 0)),
                   pl.BlockSpec((None, g, LANES, nr), lambda bi: (bi, 0, 0, 0))),
        compiler_params=_cparams(("parallel",)),
        name="compress",
    )(r, wt, wb, posk, posv, w1k, w1v, w2k, w2v)


def _kvprep_kernel(q_ref, kv_ref, sm_ref, pos_ref, inv_ref, gb_ref,
                   qrawt_ref, qrope_ref, kst_ref, vs_ref, kwt_ref, vw_ref, gates_ref):
    tk = q_ref.shape[0]
    lane = lax.broadcasted_iota(jnp.int32, (tk, LANES), 1)
    lih = lane % N_HEAD_DIM
    ang = pos_ref[...] * inv_ref[...]
    cos = jnp.cos(ang)
    sin = jnp.sin(ang)
    half = ROPE_DIM // 2
    s_up = jnp.where(lih < half, -sin, 0.0)
    s_dn = jnp.where((lih >= half) & (lih < ROPE_DIM), sin, 0.0)

    def rope(u):
        return u * cos + pltpu.roll(u, LANES - half, axis=1) * s_up + pltpu.roll(u, half, axis=1) * s_dn

    def head_pad(u, odd):
        if odd:
            u = pltpu.roll(u, N_HEAD_DIM, axis=1)
        return jnp.where(lane < N_HEAD_DIM, u, 0.0)

    scale = N_HEAD_DIM ** -0.5 * math.log2(math.e)
    zeros_t = jnp.zeros((N_HEAD_DIM, tk), BF16)
    for g in range(N_KV_GROUPS):
        for hp in range(N_HPG // 2):
            c0 = g * (N_HPG * N_HEAD_DIM) + hp * LANES
            u = q_ref[:, c0:c0 + LANES] * scale
            ur = rope(u)
            u_t = u.T.astype(BF16)
            for odd in range(2):
                h = 2 * hp + odd
                qrawt_ref[g, h * LANES:h * LANES + N_HEAD_DIM, :] = u_t[odd * N_HEAD_DIM:(odd + 1) * N_HEAD_DIM, :]
                qrawt_ref[g, h * LANES + N_HEAD_DIM:(h + 1) * LANES, :] = zeros_t
                qrope_ref[g, :, h * LANES:(h + 1) * LANES] = head_pad(ur, odd).astype(BF16)

    ks_t = rope(kv_ref[:, 0:LANES]).T
    kw_t = rope(kv_ref[:, 2 * LANES:3 * LANES]).T
    vs = kv_ref[:, LANES:2 * LANES]
    vw = kv_ref[:, 3 * LANES:4 * LANES]
    t_glob = pl.program_id(1) * tk + lax.broadcasted_iota(jnp.int32, (LANES, tk), 1)
    blk_row = lax.broadcasted_iota(jnp.int32, (LANES, tk), 0)
    onehot = jnp.where(blk_row == t_glob // SEL_BLOCK, 1.0, 0.0).astype(BF16)
    ones_lane = jnp.where(lane == N_HEAD_DIM, 1.0, 0.0)
    gates = _sigmoid(sm_ref[...] + gb_ref[...])
    for g in range(N_KV_GROUPS):
        rs = slice(g * N_HEAD_DIM, (g + 1) * N_HEAD_DIM)
        kst_ref[g, 0:LANES, :] = onehot
        kst_ref[g, LANES:LANES + N_HEAD_DIM, :] = ks_t[rs, :].astype(BF16)
        kst_ref[g, LANES + N_HEAD_DIM:2 * LANES, :] = zeros_t
        kwt_ref[g, 0:N_HEAD_DIM, :] = kw_t[rs, :].astype(BF16)
        kwt_ref[g, N_HEAD_DIM:LANES, :] = zeros_t
        vs_ref[g] = (head_pad(vs, g) + ones_lane).astype(BF16)
        vw_ref[g] = (head_pad(vw, g) + ones_lane).astype(BF16)
        goff = 8 + g * (3 * N_HPG)
        gates_ref[g] = pltpu.roll(gates, LANES - goff, axis=1)


def _kvprep(p, b, t, pos_col, inv_row, gate_bias):
    tk = 512
    nt = t // tk
    g = N_KV_GROUPS
    qw = N_HPG * LANES

    def tok(shape):
        return pl.BlockSpec((None, g) + shape, lambda bi, i: (bi, 0, i, 0))

    def tok_t(rows):
        return pl.BlockSpec((None, g, rows, tk), lambda bi, i: (bi, 0, 0, i))

    return pl.pallas_call(
        _kvprep_kernel,
        out_shape=(
            jax.ShapeDtypeStruct((b, g, qw, t), BF16), jax.ShapeDtypeStruct((b, g, t, qw), BF16),
            jax.ShapeDtypeStruct((b, g, 2 * LANES, t), BF16), jax.ShapeDtypeStruct((b, g, t, LANES), BF16),
            jax.ShapeDtypeStruct((b, g, LANES, t), BF16), jax.ShapeDtypeStruct((b, g, t, LANES), BF16),
            jax.ShapeDtypeStruct((b, g, t, LANES), F32),
        ),
        grid=(b, nt),
        in_specs=[
            pl.BlockSpec((tk, N_WIDTH), lambda bi, i: (bi * nt + i, C_NQ // N_WIDTH)),
            pl.BlockSpec((tk, 4 * LANES), lambda bi, i: (bi * nt + i, C_KV // (4 * LANES))),
            pl.BlockSpec((tk, LANES), lambda bi, i: (bi * nt + i, C_SM // LANES)),
            pl.BlockSpec((tk, 1), lambda bi, i: (bi * nt + i, 0)),
            pl.BlockSpec((1, LANES), lambda bi, i: (0, 0)),
            pl.BlockSpec((1, LANES), lambda bi, i: (0, 0)),
        ],
        out_specs=(tok_t(qw), tok((tk, qw)), tok_t(2 * LANES), tok((tk, LANES)),
                   tok_t(LANES), tok((tk, LANES)), tok((tk, LANES))),
        compiler_params=_cparams(("parallel", "arbitrary")),
        name="kvprep",
    )(p, p, p, pos_col, inv_row, gate_bias)


def _select_kernel(qt_ref, kc_ref, vct_ref, ovt_ref, oc_ref, bias_ref):
    qs = qt_ref.shape[1]
    nr = kc_ref.shape[0]
    t0 = pl.program_id(2) * qs
    m_c = (lax.broadcasted_iota(jnp.int32, (nr, qs), 0) * CMP_STRIDE + (CMP_BLOCK - 1)
           <= t0 + lax.broadcasted_iota(jnp.int32, (nr, qs), 1))
    kc = kc_ref[...]
    vct = vct_ref[...]
    psum = jnp.zeros((nr, qs), F32)
    for h in range(N_HPG):
        s = jnp.dot(kc, qt_ref[h * LANES:(h + 1) * LANES, :], preferred_element_type=F32)
        s = jnp.where(m_c, s, NEG)
        e = jnp.where(m_c, jnp.exp2(s - jnp.max(s, axis=0, keepdims=True)), 0.0)
        den = jnp.sum(e, axis=0, keepdims=True)
        pc = e * (1.0 / jnp.where(den == 0.0, 1.0, den))
        psum = psum + pc
        oc_t = jnp.dot(vct, pc.astype(BF16), preferred_element_type=F32)
        oc_ref[:, h * LANES:(h + 1) * LANES] = oc_t.T
    p_hi = psum.astype(BF16)
    p_lo = (psum - p_hi.astype(F32)).astype(BF16)
    ovt = ovt_ref[...]
    imp = jnp.dot(ovt, p_hi, preferred_element_type=F32) + jnp.dot(ovt, p_lo, preferred_element_type=F32)

    blk = lax.broadcasted_iota(jnp.int32, (LANES, qs), 0)
    tq = t0 + lax.broadcasted_iota(jnp.int32, (LANES, qs), 1)
    cur = tq // SEL_BLOCK
    forced = (blk == 0) | (blk == cur) | (blk == cur - 1)
    n_sel = (nr * CMP_STRIDE) // SEL_BLOCK
    score = jnp.where(forced, FORCE, jnp.where(blk * SEL_BLOCK <= tq, imp, -1.0))
    score = jnp.where(blk < n_sel, score, -jnp.inf)
    for _ in range(min(SEL_TOPK, n_sel)):
        mx = jnp.max(score, axis=0, keepdims=True)
        idx = jnp.min(jnp.where(score == mx, blk, LANES), axis=0, keepdims=True)
        score = jnp.where(blk == idx, -jnp.inf, score)
    cutoff = jnp.minimum((tq // Q_BLOCK) * (Q_BLOCK // SEL_BLOCK), n_sel)
    bias_t = jnp.where((score == -jnp.inf) & (blk < cutoff), 0.0, NEG)
    bias_ref[...] = bias_t.T.astype(BF16)


def _select(qrawt, kc, vct, ovt):
    b, g, qw, t = qrawt.shape
    nr = kc.shape[2]
    qs = 512

    def whole(rows, cols):
        return pl.BlockSpec((None, None, rows, cols), lambda bi, gi, i: (bi, gi, 0, 0))

    return pl.pallas_call(
        _select_kernel,
        out_shape=(jax.ShapeDtypeStruct((b, g, t, qw), F32), jax.ShapeDtypeStruct((b, g, t, LANES), BF16)),
        grid=(b, g, t // qs),
        in_specs=[
            pl.BlockSpec((None, None, qw, qs), lambda bi, gi, i: (bi, gi, 0, i)),
            whole(nr, LANES), whole(LANES, nr),
            pl.BlockSpec(ovt.shape, lambda bi, gi, i: (0, 0)),
        ],
        out_specs=(pl.BlockSpec((None, None, qs, qw), lambda bi, gi, i: (bi, gi, i, 0)),
                   pl.BlockSpec((None, None, qs, LANES), lambda bi, gi, i: (bi, gi, i, 0))),
        compiler_params=_cparams(("parallel", "parallel", "parallel")),
        name="select",
    )(qrawt, kc, vct, ovt)


def _nsa_kernel(qrope_ref, bias_ref, oc_ref, kst_ref, vs_ref, kwt_ref, vw_ref, gates_ref, nz_ref,
                yb_ref, qaug_ref, acc_ref, m_ref):
    i = pl.program_id(1)
    t0 = i * Q_BLOCK
    Q = Q_BLOCK
    nh = N_HPG
    ng = N_KV_GROUPS
    rows = nh * Q
    t0a = pl.multiple_of(t0, Q)
    rq = lax.broadcasted_iota(jnp.int32, (rows, Q), 0) % Q
    ck = lax.broadcasted_iota(jnp.int32, (rows, Q), 1)
    lane = lax.broadcasted_iota(jnp.int32, (Q, LANES), 1)

    def q_rope(g):
        return jnp.concatenate([qrope_ref[g, :, h * LANES:(h + 1) * LANES] for h in range(nh)], axis=0)

    for g in range(ng):
        qr = q_rope(g)
        qaug_ref[g, :, 0:LANES] = jnp.concatenate([bias_ref[g]] * nh, axis=0)
        qaug_ref[g, :, LANES:2 * LANES] = qr
        s_d = jnp.dot(qr, kst_ref[g, LANES:2 * LANES, pl.ds(t0a, Q)], preferred_element_type=F32)
        s_d = jnp.where(ck <= rq, s_d, NEG)
        m0 = jnp.max(s_d, axis=1, keepdims=True)
        p_d = jnp.exp2(s_d - m0)
        m_ref[g] = jnp.broadcast_to(m0, (rows, LANES))
        acc_ref[g] = jnp.dot(p_d.astype(BF16), vs_ref[g, pl.ds(t0a, Q), :], preferred_element_type=F32)

    ncb = SEL_CHUNK // LANES

    def sel_chunk(c):
        k0 = pl.multiple_of(c * SEL_CHUNK, SEL_CHUNK)
        for g in range(ng):
            s = jnp.dot(qaug_ref[g], kst_ref[g, :, pl.ds(k0, SEL_CHUNK)], preferred_element_type=F32)
            m_old = m_ref[g]
            m_new = jnp.maximum(m_old, jnp.max(s, axis=1, keepdims=True))
            p = jnp.concatenate([jnp.exp2(s[:, cb * LANES:(cb + 1) * LANES] - m_new) for cb in range(ncb)], axis=1)
            acc_ref[g] = (jnp.exp2(m_old - m_new) * acc_ref[g]
                          + jnp.dot(p.astype(BF16), vs_ref[g, pl.ds(k0, SEL_CHUNK), :],
                                    preferred_element_type=F32))
            m_ref[g] = m_new

    def sel_pair(c2, carry):
        sel_chunk(2 * c2)
        sel_chunk(2 * c2 + 1)
        return carry

    n_chunks = (t0 + SEL_CHUNK - 1) // SEL_CHUNK
    lax.fori_loop(0, n_chunks // 2, sel_pair, 0)

    @pl.when(n_chunks % 2 == 1)
    def _():
        sel_chunk(n_chunks - 1)

    wlen = WINDOW + Q
    ws = pl.multiple_of(jnp.maximum(t0 - WINDOW, 0), Q)
    tq_w = t0 + lax.broadcasted_iota(jnp.int32, (rows, wlen), 0) % Q
    diff = tq_w - (ws + lax.broadcasted_iota(jnp.int32, (rows, wlen), 1))
    m_w = (diff >= 0) & (diff < WINDOW)
    for g in range(ng):
        acc = acc_ref[g]
        o_s = acc * (1.0 / acc[:, N_HEAD_DIM:N_HEAD_DIM + 1])
        s_w = jnp.dot(q_rope(g), kwt_ref[g, :, pl.ds(ws, wlen)], preferred_element_type=F32)
        s_w = jnp.where(m_w, s_w, NEG)
        p_w = jnp.exp2(s_w - jnp.max(s_w, axis=1, keepdims=True))
        acc_w = jnp.dot(p_w.astype(BF16), vw_ref[g, pl.ds(ws, wlen), :], preferred_element_type=F32)
        o_w = acc_w * (1.0 / acc_w[:, N_HEAD_DIM:N_HEAD_DIM + 1])
        gates = gates_ref[g]
        outs = []
        for h in range(nh):
            rs = slice(h * Q, (h + 1) * Q)
            outs.append(gates[:, 3 * h:3 * h + 1] * oc_ref[g, :, h * LANES:(h + 1) * LANES]
                        + gates[:, 3 * h + 1:3 * h + 2] * o_s[rs, :]
                        + gates[:, 3 * h + 2:3 * h + 3] * o_w[rs, :])
        for hp in range(nh // 2):
            yn = jnp.where(lane < N_HEAD_DIM, outs[2 * hp], pltpu.roll(outs[2 * hp + 1], N_HEAD_DIM, axis=1))
            ls = slice((g * (nh // 2) + hp) * LANES, (g * (nh // 2) + hp + 1) * LANES)
            yb_ref[:, ls] = (yn * _silu(nz_ref[:, ls])).astype(BF16)


def _nsa(p, b, t, qrope, bias, oc, kst, vs, kwt, vw, gates):
    g = N_KV_GROUPS
    nqb = t // Q_BLOCK
    qw = N_HPG * LANES
    rows = N_HPG * Q_BLOCK

    def qblk(w):
        return pl.BlockSpec((None, g, Q_BLOCK, w), lambda bi, i: (bi, 0, i, 0))

    def whole(r, c):
        return pl.BlockSpec((None, g, r, c), lambda bi, i: (bi, 0, 0, 0), pipeline_mode=pl.Buffered(1))

    return pl.pallas_call(
        _nsa_kernel,
        out_shape=jax.ShapeDtypeStruct((b * t, N_WIDTH), BF16),
        grid=(b, nqb),
        in_specs=[
            qblk(qw), qblk(LANES), qblk(qw),
            whole(2 * LANES, t), whole(t, LANES), whole(LANES, t), whole(t, LANES),
            qblk(LANES),
            pl.BlockSpec((Q_BLOCK, N_WIDTH), lambda bi, i: (bi * nqb + i, C_NZ // N_WIDTH)),
        ],
        out_specs=pl.BlockSpec((Q_BLOCK, N_WIDTH), lambda bi, i: (bi * nqb + i, 0)),
        scratch_shapes=[pltpu.VMEM((g, rows, 2 * LANES), BF16), pltpu.VMEM((g, rows, LANES), F32),
                        pltpu.VMEM((g, rows, LANES), F32)],
        compiler_params=_cparams(("parallel", "arbitrary")),
        name="nsa",
    )(qrope, bias, oc, kst, vs, kwt, vw, gates, p)


def _outproj_kernel(final, ya_ref, yb_ref, ga_ref, gb_ref, x_ref, wa_ref, wb_ref, wo_ref, fg_ref, o_ref):
    a = jnp.dot(ya_ref[...], wa_ref[...], preferred_element_type=F32)
    bq = jnp.dot(yb_ref[...], wb_ref[...], preferred_element_type=F32)
    merged = _sigmoid(ga_ref[...]) * a + _sigmoid(gb_ref[...]) * bq
    y = x_ref[...] + jnp.dot(merged.astype(BF16), wo_ref[...], preferred_element_type=F32)
    if final:
        y = y * lax.rsqrt(jnp.mean(y * y, axis=-1, keepdims=True) + NORM_EPS) * fg_ref[...]
    o_ref[...] = y


def _outproj(ya, yb, p, x2, wa, wb, wo, fg, final):
    m = x2.shape[0]
    tm = 512

    def full(a):
        return pl.BlockSpec(a.shape, lambda i: (0, 0))

    return pl.pallas_call(
        functools.partial(_outproj_kernel, final),
        out_shape=jax.ShapeDtypeStruct((m, D_MODEL), F32),
        grid=(m // tm,),
        in_specs=[
            pl.BlockSpec((tm, M_WIDTH), lambda i: (i, 0)),
            pl.BlockSpec((tm, N_WIDTH), lambda i: (i, 0)),
            pl.BlockSpec((tm, D_MODEL), lambda i: (i, C_GA // D_MODEL)),
            pl.BlockSpec((tm, D_MODEL), lambda i: (i, C_GB // D_MODEL)),
            pl.BlockSpec((tm, D_MODEL), lambda i: (i, 0)),
            full(wa), full(wb), full(wo), full(fg),
        ],
        out_specs=pl.BlockSpec((tm, D_MODEL), lambda i: (i, 0)),
        compiler_params=_cparams(("parallel",)),
        name="outproj",
    )(ya, yb, p, p, x2, wa, wb, wo, fg)


def _permute_w_in(w):
    pad = jnp.zeros((w.shape[0], P_W - _O_END), w.dtype)
    return jnp.concatenate([
        w[:, 0:_O_MI],
        w[:, _O_GA:_O_END],
        w[:, _O_NQ:_O_KC],
        w[:, _O_NZ:_O_GA],
        w[:, _O_KS:_O_NG],
        w[:, _O_KC:_O_KS],
        w[:, _O_MI:_O_NQ],
        w[:, _O_NG:_O_NZ],
        pad], axis=1)


def _expand_cmp_w1(w1k, w1v):
    half = CMP_BLOCK // 2
    k3 = w1k.astype(BF16).reshape(CMP_BLOCK, 1, N_HEAD_DIM, 1, CMP_HIDDEN)
    v3 = w1v.astype(BF16).reshape(CMP_BLOCK, 1, N_HEAD_DIM, 1, CMP_HIDDEN)
    nc = 2 * N_KV_GROUPS
    eye = jnp.eye(nc, dtype=BF16)
    is_k = (jnp.arange(nc) < N_KV_GROUPS).astype(BF16)
    mk = (eye * is_k[:, None]).reshape(1, nc, 1, nc, 1)
    mv = (eye * (1 - is_k)[:, None]).reshape(1, nc, 1, nc, 1)
    z = k3 * mk + v3 * mv
    z = z.reshape(CMP_BLOCK * nc * N_HEAD_DIM, nc * CMP_HIDDEN)
    return z[:half * nc * N_HEAD_DIM], z[half * nc * N_HEAD_DIM:]


def kernel(x, positions, norm_g, w_in, conv_q_w, conv_q_b, conv_k_w, conv_k_b, b_igate, b_fgate, mh_norm_g,
           cmp_pos_k, cmp_w1_k, cmp_w2_k, cmp_pos_v, cmp_w1_v, cmp_w2_v, b_nsa_gate, w_branch_a, w_branch_b,
           w_out, final_norm_g):
    b, t, _ = x.shape
    assert t % SEL_CHUNK == 0 and t >= WINDOW + Q_BLOCK and t // SEL_BLOCK <= LANES and t % MLSTM_CHUNK == 0
    x2 = x.reshape(b * t, D_MODEL)
    for l in range(norm_g.shape[0]):
        p = _inproj(x2, norm_g[l][None, :], _permute_w_in(w_in[l].astype(BF16)))

        sm_bias = jnp.zeros((1, LANES), F32).at[0, 0:M_HEADS].set(b_igate[l]).at[0, M_HEADS:2 * M_HEADS].set(b_fgate[l])
        ya = _mlstm(p, b, t, conv_q_w[l], conv_q_b[l][None, :], conv_k_w[l], conv_k_b[l][None, :], sm_bias,
                    mh_norm_g[l][None, :])

        nr = t // CMP_STRIDE
        r = p.reshape(b, t, P_W)[:, :, C_CMP:C_CMP + 2 * LANES].reshape(b, nr, CMP_STRIDE * 2 * LANES)
        wt, wb = _expand_cmp_w1(cmp_w1_k[l], cmp_w1_v[l])
        w2pad = lambda w: jnp.pad(w, ((0, 0), (0, LANES - N_HEAD_DIM))).astype(BF16)
        kc, vct = _compress(r, wt, wb, cmp_pos_k[l].reshape(1, -1), cmp_pos_v[l].reshape(1, -1),
                            cmp_w1_k[l].astype(BF16), cmp_w1_v[l].astype(BF16), w2pad(cmp_w2_k[l]), w2pad(cmp_w2_v[l]))

        half = ROPE_DIM // 2
        inv = jnp.power(jnp.float32(ROPE_THETA), -jnp.arange(half, dtype=F32) * (2.0 / ROPE_DIM))
        lih = jnp.arange(LANES) % N_HEAD_DIM
        inv_row = jnp.where(lih < ROPE_DIM, inv[lih % half], 0.0)[None, :].astype(F32)
        pos_col = positions.astype(F32).reshape(b * t, 1)
        gate_bias = jnp.zeros((1, LANES), F32).at[0, 8:8 + 3 * N_HEADS].set(b_nsa_gate[l])
        qrawt, qrope, kst, vs, kwt, vw, gates = _kvprep(p, b, t, pos_col, inv_row, gate_bias)

        ci = jnp.arange(nr)[None, :] * CMP_STRIDE
        sj = jnp.arange(LANES)[:, None] * SEL_BLOCK
        overlap_t = ((ci < sj + SEL_BLOCK) & (ci + CMP_BLOCK > sj) & (jnp.arange(nr)[None, :] < nr - 1)
                     & (jnp.arange(LANES)[:, None] < t // SEL_BLOCK)).astype(BF16)
        oc, bias = _select(qrawt, kc, vct, overlap_t)
        yb = _nsa(p, b, t, qrope, bias, oc, kst, vs, kwt, vw, gates)

        x2 = _outproj(ya, yb, p, x2, w_branch_a[l].astype(BF16), w_branch_b[l].astype(BF16),
                      w_out[l].astype(BF16), final_norm_g[None, :], l == norm_g.shape[0] - 1)
    return x2.reshape(b, t, D_MODEL)
```

```python
import functools
import math

import jax
import jax.numpy as jnp
from jax import lax
from jax.experimental import pallas as pl
from jax.experimental.pallas import tpu as pltpu

F32 = jnp.float32
BF16 = jnp.bfloat16

D_MODEL = 1024
M_HEADS = 4
M_HEAD_DIM = 256
M_WIDTH = M_HEADS * M_HEAD_DIM
CONV_WIDTH = 4
N_HEADS = 8
N_KV_GROUPS = 2
N_HPG = N_HEADS // N_KV_GROUPS
N_HEAD_DIM = 64
N_WIDTH = N_HEADS * N_HEAD_DIM
CMP_BLOCK = 32
CMP_STRIDE = 16
CMP_HIDDEN = 128
SEL_BLOCK = 64
SEL_TOPK = 16
WINDOW = 512
Q_BLOCK = 128
ROPE_THETA = 500000.0
ROPE_DIM = N_HEAD_DIM // 4
NORM_EPS = 1e-6
NEG = -1e30
FORCE = 1e9

LANES = 128
MLSTM_CHUNK = 256
SEL_CHUNK = 512
VMEM_LIMIT = 56 * 1024 * 1024

A_MQ, A_MK, A_MV, A_NQ = 0, 1024, 2048, 3072
A_KV = 3584
A_CMP = 4096
PA_W = 4608
B_MO, B_MZ, B_GA, B_GB, B_NZ = 0, 1024, 2048, 3072, 4096
B_SM = 4608
PB_W = 5120
PROJ_TN = 512
_O_MO, _O_MI, _O_NQ, _O_KC, _O_KS, _O_NG, _O_NZ, _O_GA, _O_END = 3072, 5120, 5128, 5640, 5896, 6408, 6432, 6944, 8992


def _cparams(sem):
    return pltpu.CompilerParams(dimension_semantics=sem, vmem_limit_bytes=VMEM_LIMIT)


def _sigmoid(x):
    return 0.5 * jnp.tanh(0.5 * x) + 0.5


def _silu(x):
    h = 0.5 * x
    return h + h * jnp.tanh(h)


def _log_sigmoid(x):
    return jnp.minimum(x, 0.0) - jnp.log1p(jnp.exp(-jnp.abs(x)))


def _rmsnorm_to(hn_ref, x_ref, g_ref):
    x = x_ref[...]
    r = lax.rsqrt(jnp.mean(x * x, axis=-1, keepdims=True) + NORM_EPS)
    hn_ref[...] = (x * r * g_ref[...]).astype(BF16)


N_CONV_TILES = (A_MV - A_MQ) // PROJ_TN


def _inproj_a_kernel(t_len, x_ref, g_ref, w_ref, cw_ref, cb_ref, o_ref, hn_ref, halo_ref):
    i = pl.program_id(0)
    j = pl.program_id(1)
    tm = x_ref.shape[0]

    @pl.when(j == 0)
    def _():
        _rmsnorm_to(hn_ref, x_ref, g_ref)

    def proj():
        return jnp.dot(hn_ref[...], w_ref[...], preferred_element_type=F32)

    @pl.when(j < N_CONV_TILES)
    def _():
        acc = proj()
        tn = acc.shape[1]
        first = (i * tm) % t_len == 0
        prev = jnp.where(first, 0.0, halo_ref[j])
        halo_ref[j] = acc[tm - 8:tm, :]
        a3 = acc.reshape(tm // 8, 8, tn)
        sub = lax.broadcasted_iota(jnp.int32, (tm // 8, 8, tn), 1)
        y = cb_ref[...] + acc * cw_ref[CONV_WIDTH - 1:CONV_WIDTH, :]
        for d in range(1, CONV_WIDTH):
            r = pltpu.roll(a3, d, axis=1)
            rp = pltpu.roll(prev.reshape(1, 8, tn), d, axis=1)
            above = jnp.concatenate([rp, r[:-1]], axis=0)
            sh = jnp.where(sub < d, above, r).reshape(tm, tn)
            y = y + sh * cw_ref[CONV_WIDTH - 1 - d:CONV_WIDTH - d, :]
        half_y = 0.5 * y
        hs = y * jnp.where(j >= N_CONV_TILES // 2, 0.5 * M_HEAD_DIM ** -0.5, 0.5)
        o_ref[...] = (hs + hs * jnp.tanh(half_y)).astype(BF16)

    @pl.when(j >= N_CONV_TILES)
    def _():
        o_ref[...] = proj().astype(BF16)


def _inproj_a(x2, t_len, norm_g, wa, conv_w, conv_b):
    m = x2.shape[0]
    tm, tn = 1024, PROJ_TN
    last = N_CONV_TILES - 1
    return pl.pallas_call(
        functools.partial(_inproj_a_kernel, t_len),
        out_shape=jax.ShapeDtypeStruct((m, PA_W), BF16),
        grid=(m // tm, PA_W // tn),
        in_specs=[
            pl.BlockSpec((tm, D_MODEL), lambda i, j: (i, 0)),
            pl.BlockSpec((1, D_MODEL), lambda i, j: (0, 0)),
            pl.BlockSpec((D_MODEL, tn), lambda i, j: (0, j)),
            pl.BlockSpec((CONV_WIDTH, tn), lambda i, j: (0, jnp.minimum(j, last))),
            pl.BlockSpec((1, tn), lambda i, j: (0, jnp.minimum(j, last))),
        ],
        out_specs=pl.BlockSpec((tm, tn), lambda i, j: (i, j)),
        scratch_shapes=[pltpu.VMEM((tm, D_MODEL), BF16), pltpu.VMEM((N_CONV_TILES, 8, tn), F32)],
        compiler_params=_cparams(("arbitrary", "arbitrary")),
        name="inproj_a",
    )(x2, norm_g, wa, conv_w, conv_b)


_B_SIGMOID_TILES = tuple(c // PROJ_TN for c in (B_MO, B_MO + PROJ_TN, B_GA, B_GA + PROJ_TN, B_GB, B_GB + PROJ_TN))
_B_SILU_TILES = tuple(c // PROJ_TN for c in (B_MZ, B_MZ + PROJ_TN, B_NZ))


def _inproj_b_kernel(x_ref, g_ref, w_ref, o_ref, hn_ref):
    j = pl.program_id(1)

    @pl.when(j == 0)
    def _():
        _rmsnorm_to(hn_ref, x_ref, g_ref)

    def proj():
        return jnp.dot(hn_ref[...], w_ref[...], preferred_element_type=F32)

    def any_of(tiles):
        return functools.reduce(jnp.logical_or, [j == t for t in tiles])

    is_sig = any_of(_B_SIGMOID_TILES)
    is_silu = any_of(_B_SILU_TILES)

    @pl.when(is_sig)
    def _():
        o_ref[...] = _sigmoid(proj())

    @pl.when(is_silu)
    def _():
        o_ref[...] = _silu(proj())

    @pl.when(jnp.logical_not(is_sig | is_silu))
    def _():
        o_ref[...] = proj()


def _inproj_b(x2, norm_g, wb):
    m = x2.shape[0]
    tm, tn = 2048, PROJ_TN
    return pl.pallas_call(
        _inproj_b_kernel,
        out_shape=jax.ShapeDtypeStruct((m, PB_W), F32),
        grid=(m // tm, PB_W // tn),
        in_specs=[
            pl.BlockSpec((tm, D_MODEL), lambda i, j: (i, 0)),
            pl.BlockSpec((1, D_MODEL), lambda i, j: (0, 0)),
            pl.BlockSpec((D_MODEL, tn), lambda i, j: (0, j)),
        ],
        out_specs=pl.BlockSpec((tm, tn), lambda i, j: (i, j)),
        scratch_shapes=[pltpu.VMEM((tm, D_MODEL), BF16)],
        compiler_params=_cparams(("parallel", "arbitrary")),
        name="inproj_b",
    )(x2, norm_g, wb)


def _mlstm_kernel(q_ref, k_ref, v_ref, og_ref, z_ref, sm_ref, sb_ref, ng_ref, ya_ref,
                  c_state, n_state, m_state):
    L = q_ref.shape[0]
    d = M_HEAD_DIM

    @pl.when(pl.program_id(1) == 0)
    def _():
        c_state[...] = jnp.zeros_like(c_state)
        n_state[...] = jnp.zeros_like(n_state)
        m_state[...] = jnp.zeros_like(m_state)

    sm = sm_ref[...] + sb_ref[...]
    lf = _log_sigmoid(sm)
    row = lax.broadcasted_iota(jnp.int32, (L, L), 0)
    col = lax.broadcasted_iota(jnp.int32, (L, L), 1)
    causal = row >= col
    tril = causal.astype(F32)
    triu = (row <= col).astype(F32)
    a_col = jnp.dot(tril, lf, precision=lax.Precision.HIGHEST, preferred_element_type=F32)
    sm_t = sm.T[0:8, :]
    a_row = jnp.dot(_log_sigmoid(sm_t), triu, precision=lax.Precision.HIGHEST,
                    preferred_element_type=F32)

    for h in range(M_HEADS):
        hs = slice(h * d, (h + 1) * d)
        a_c = a_col[:, 4 + h:5 + h]
        li_c = sm[:, h:h + 1]
        a_r = a_row[4 + h:5 + h, :]
        li_r = sm_t[h:h + 1, :]
        m_prev = m_state[h:h + 1, 0:1]
        a_end = a_c[L - 1:L, :]

        log_d = jnp.where(causal, a_c - a_r + li_r, -jnp.inf)
        inter = a_c + m_prev
        m_t = jnp.maximum(inter, jnp.max(log_d, axis=1, keepdims=True))
        w_inter = jnp.exp(inter - m_t)
        dm = jnp.exp(log_d - m_t)

        qb = q_ref[:, hs]
        kb = k_ref[:, hs]
        vb = v_ref[:, hs]
        qh = qb.astype(F32)
        kh = kb.astype(F32)
        s = lax.dot_general(qb, kb, (((1,), (1,)), ((), ())), preferred_element_type=F32) * dm
        c_old = c_state[h]
        n_old = n_state[h:h + 1, :]
        num = (w_inter * jnp.dot(qb, c_old.astype(BF16), preferred_element_type=F32)
               + jnp.dot(s.astype(BF16), vb, preferred_element_type=F32))
        den = (w_inter * jnp.sum(qh * n_old, axis=1, keepdims=True)
               + jnp.sum(s, axis=1, keepdims=True))
        hh = num * (1.0 / jnp.maximum(jnp.abs(den), jnp.exp(-m_t)))

        logw = a_end - a_c + li_c
        m_new = jnp.maximum(a_end + m_prev, jnp.max(logw, axis=0, keepdims=True))
        wk = jnp.exp(logw - m_new)
        decay = jnp.exp(a_end + m_prev - m_new)
        kw = kh * wk
        c_state[h] = decay * c_old + lax.dot_general(kw.astype(BF16), vb, (((0,), (0,)), ((), ())),
                                                     preferred_element_type=F32)
        n_state[h:h + 1, :] = decay * n_old + jnp.sum(kw, axis=0, keepdims=True)
        m_state[h:h + 1, :] = jnp.broadcast_to(m_new, (1, LANES))

        mu = jnp.mean(hh, axis=1, keepdims=True)
        dc = hh - mu
        var = jnp.mean(dc * dc, axis=1, keepdims=True)
        hn = dc * lax.rsqrt(var + NORM_EPS) * ng_ref[:, hs]
        ya = og_ref[:, hs] * hn * z_ref[:, hs]
        ya_ref[:, hs] = ya.astype(BF16)


def _mlstm(pa, pb, b, t, sm_bias, mh_norm_g):
    L = MLSTM_CHUNK
    nc = t // L

    def colspec(cstart):
        return pl.BlockSpec((L, M_WIDTH), lambda bi, c: (bi * nc + c, cstart // M_WIDTH))

    def full(shape):
        return pl.BlockSpec(shape, lambda bi, c: (0,) * len(shape))

    return pl.pallas_call(
        _mlstm_kernel,
        out_shape=jax.ShapeDtypeStruct((b * t, M_WIDTH), BF16),
        grid=(b, nc),
        in_specs=[
            colspec(A_MQ), colspec(A_MK), colspec(A_MV), colspec(B_MO), colspec(B_MZ),
            pl.BlockSpec((L, LANES), lambda bi, c: (bi * nc + c, B_SM // LANES)),
            full((1, LANES)), full((1, M_WIDTH)),
        ],
        out_specs=pl.BlockSpec((L, M_WIDTH), lambda bi, c: (bi * nc + c, 0)),
        scratch_shapes=[
            pltpu.VMEM((M_HEADS, M_HEAD_DIM, M_HEAD_DIM), F32),
            pltpu.VMEM((8, M_HEAD_DIM), F32), pltpu.VMEM((8, LANES), F32),
        ],
        compiler_params=_cparams(("parallel", "arbitrary")),
        name="mlstm",
    )(pa, pa, pa, pb, pb, pb, sm_bias, mh_norm_g)


def _compress_kernel(r_ref, wt_ref, wb_ref, posk_ref, posv_ref, w1k_ref, w1v_ref, w2k_ref, w2v_ref,
                     kc_ref, vct_ref):
    r = r_ref[...].astype(BF16)
    nr = r.shape[0]
    u = jnp.dot(r, wt_ref[...], preferred_element_type=F32)
    v = jnp.dot(r, wb_ref[...], preferred_element_type=F32)
    v = pltpu.roll(v, nr - 1, axis=0)
    rowi = lax.broadcasted_iota(jnp.int32, v.shape, 0)
    v = jnp.where(rowi < nr - 1, v, 0.0)

    def pos_term(pos_ref, w_ref):
        pb = jnp.broadcast_to(pos_ref[...].astype(BF16), (8, CMP_BLOCK * N_HEAD_DIM))
        return jnp.dot(pb, w_ref[...], preferred_element_type=F32)[0:1, :]

    pk = pos_term(posk_ref, w1k_ref)
    pv = pos_term(posv_ref, w1v_ref)
    hid = u + v + jnp.concatenate([pk, pk, pv, pv], axis=1)
    act = 0.5 * hid * (1.0 + jnp.tanh(math.sqrt(2.0 / math.pi) * (hid + 0.044715 * (hid * hid * hid))))
    for c in range(2 * N_KV_GROUPS):
        w2 = w2k_ref[...] if c < N_KV_GROUPS else w2v_ref[...]
        out = jnp.dot(act[:, c * CMP_HIDDEN:(c + 1) * CMP_HIDDEN].astype(BF16), w2,
                      preferred_element_type=F32)
        if c < N_KV_GROUPS:
            kc_ref[c] = out.astype(BF16)
        else:
            vct_ref[c - N_KV_GROUPS] = out.T.astype(BF16)


def _compress(r, wt, wb, posk, posv, w1k, w1v, w2k, w2v):
    b, nr, rw = r.shape
    g = N_KV_GROUPS

    def full(a):
        return pl.BlockSpec(a.shape, lambda bi: (0,) * a.ndim)

    return pl.pallas_call(
        _compress_kernel,
        out_shape=(jax.ShapeDtypeStruct((b, g, nr, LANES), BF16), jax.ShapeDtypeStruct((b, g, LANES, nr), BF16)),
        grid=(b,),
        in_specs=[pl.BlockSpec((None, nr, rw), lambda bi: (bi, 0, 0)),
                  full(wt), full(wb), full(posk), full(posv), full(w1k), full(w1v), full(w2k), full(w2v)],
        out_specs=(pl.BlockSpec((None, g, nr, LANES), lambda bi: (bi, 0, 0, 0)),
                   pl.BlockSpec((None, g, LANES, nr), lambda bi: (bi, 0, 0, 0))),
        compiler_params=_cparams(("parallel",)),
        name="compress",
    )(r, wt, wb, posk, posv, w1k, w1v, w2k, w2v)


def _kvprep_kernel(q_ref, kv_ref, sm_ref, pos_ref, inv_ref, gb_ref,
                   qrawt_ref, qrope_ref, kst_ref, vs_ref, kwt_ref, vw_ref, gates_ref):
    tk = q_ref.shape[0]
    lane = lax.broadcasted_iota(jnp.int32, (tk, LANES), 1)
    lih = lane % N_HEAD_DIM
    ang = pos_ref[...] * inv_ref[...]
    cos = jnp.cos(ang)
    sin = jnp.sin(ang)
    half = ROPE_DIM // 2
    s_up = jnp.where(lih < half, -sin, 0.0)
    s_dn = jnp.where((lih >= half) & (lih < ROPE_DIM), sin, 0.0)

    def rope(u):
        return u * cos + pltpu.roll(u, LANES - half, axis=1) * s_up + pltpu.roll(u, half, axis=1) * s_dn

    def head_pad(u, odd):
        if odd:
            u = pltpu.roll(u, N_HEAD_DIM, axis=1)
        return jnp.where(lane < N_HEAD_DIM, u, 0.0)

    scale = N_HEAD_DIM ** -0.5 * math.log2(math.e)
    zeros_t = jnp.zeros((N_HEAD_DIM, tk), BF16)
    for g in range(N_KV_GROUPS):
        for hp in range(N_HPG // 2):
            c0 = g * (N_HPG * N_HEAD_DIM) + hp * LANES
            u = q_ref[:, c0:c0 + LANES].astype(F32) * scale
            ur = rope(u)
            u_t = u.T.astype(BF16)
            for odd in range(2):
                h = 2 * hp + odd
                qrawt_ref[g, h * LANES:h * LANES + N_HEAD_DIM, :] = u_t[odd * N_HEAD_DIM:(odd + 1) * N_HEAD_DIM, :]
                qrawt_ref[g, h * LANES + N_HEAD_DIM:(h + 1) * LANES, :] = zeros_t
                qrope_ref[g, :, h * LANES:(h + 1) * LANES] = head_pad(ur, odd).astype(BF16)

    ks_t = rope(kv_ref[:, 0:LANES].astype(F32)).T
    kw_t = rope(kv_ref[:, 2 * LANES:3 * LANES].astype(F32)).T
    vs = kv_ref[:, LANES:2 * LANES].astype(F32)
    vw = kv_ref[:, 3 * LANES:4 * LANES].astype(F32)
    t_glob = pl.program_id(1) * tk + lax.broadcasted_iota(jnp.int32, (LANES, tk), 1)
    blk_row = lax.broadcasted_iota(jnp.int32, (LANES, tk), 0)
    onehot = jnp.where(blk_row == t_glob // SEL_BLOCK, 1.0, 0.0).astype(BF16)
    ones_lane = jnp.where(lane == N_HEAD_DIM, 1.0, 0.0)
    gates = _sigmoid(sm_ref[...] + gb_ref[...])
    for g in range(N_KV_GROUPS):
        rs = slice(g * N_HEAD_DIM, (g + 1) * N_HEAD_DIM)
        kst_ref[g, 0:LANES, :] = onehot
        kst_ref[g, LANES:LANES + N_HEAD_DIM, :] = ks_t[rs, :].astype(BF16)
        kst_ref[g, LANES + N_HEAD_DIM:2 * LANES, :] = zeros_t
        kwt_ref[g, 0:N_HEAD_DIM, :] = kw_t[rs, :].astype(BF16)
        kwt_ref[g, N_HEAD_DIM:LANES, :] = zeros_t
        vs_ref[g] = (head_pad(vs, g) + ones_lane).astype(BF16)
        vw_ref[g] = (head_pad(vw, g) + ones_lane).astype(BF16)
        goff = 8 + g * (3 * N_HPG)
        gates_ref[g] = pltpu.roll(gates, LANES - goff, axis=1)


def _kvprep(pa, pb, b, t, pos_col, inv_row, gate_bias):
    tk = 512
    nt = t // tk
    g = N_KV_GROUPS
    qw = N_HPG * LANES

    def tok(shape):
        return pl.BlockSpec((None, g) + shape, lambda bi, i: (bi, 0, i, 0))

    def tok_t(rows):
        return pl.BlockSpec((None, g, rows, tk), lambda bi, i: (bi, 0, 0, i))

    return pl.pallas_call(
        _kvprep_kernel,
        out_shape=(
            jax.ShapeDtypeStruct((b, g, qw, t), BF16), jax.ShapeDtypeStruct((b, g, t, qw), BF16),
            jax.ShapeDtypeStruct((b, g, 2 * LANES, t), BF16), jax.ShapeDtypeStruct((b, g, t, LANES), BF16),
            jax.ShapeDtypeStruct((b, g, LANES, t), BF16), jax.ShapeDtypeStruct((b, g, t, LANES), BF16),
            jax.ShapeDtypeStruct((b, g, t, LANES), F32),
        ),
        grid=(b, nt),
        in_specs=[
            pl.BlockSpec((tk, N_WIDTH), lambda bi, i: (bi * nt + i, A_NQ // N_WIDTH)),
            pl.BlockSpec((tk, 4 * LANES), lambda bi, i: (bi * nt + i, A_KV // (4 * LANES))),
            pl.BlockSpec((tk, LANES), lambda bi, i: (bi * nt + i, B_SM // LANES)),
            pl.BlockSpec((tk, 1), lambda bi, i: (bi * nt + i, 0)),
            pl.BlockSpec((1, LANES), lambda bi, i: (0, 0)),
            pl.BlockSpec((1, LANES), lambda bi, i: (0, 0)),
        ],
        out_specs=(tok_t(qw), tok((tk, qw)), tok_t(2 * LANES), tok((tk, LANES)),
                   tok_t(LANES), tok((tk, LANES)), tok((tk, LANES))),
        compiler_params=_cparams(("parallel", "arbitrary")),
        name="kvprep",
    )(pa, pa, pb, pos_col, inv_row, gate_bias)


def _select_kernel(qt_ref, kc_ref, vct_ref, ovt_ref, oc_ref, bias_ref):
    qs = qt_ref.shape[1]
    nr = kc_ref.shape[0]
    t0 = pl.program_id(2) * qs
    m_c = (lax.broadcasted_iota(jnp.int32, (nr, qs), 0) * CMP_STRIDE + (CMP_BLOCK - 1)
           <= t0 + lax.broadcasted_iota(jnp.int32, (nr, qs), 1))
    kc = kc_ref[...]
    vct = vct_ref[...]
    psum = jnp.zeros((nr, qs), F32)
    for h in range(N_HPG):
        s = jnp.dot(kc, qt_ref[h * LANES:(h + 1) * LANES, :], preferred_element_type=F32)
        s = jnp.where(m_c, s, NEG)
        e = jnp.where(m_c, jnp.exp2(s - jnp.max(s, axis=0, keepdims=True)), 0.0)
        den = jnp.sum(e, axis=0, keepdims=True)
        pc = e * (1.0 / jnp.where(den == 0.0, 1.0, den))
        psum = psum + pc
        oc_t = jnp.dot(vct, pc.astype(BF16), preferred_element_type=F32)
        oc_ref[:, h * LANES:(h + 1) * LANES] = oc_t.T
    p_hi = psum.astype(BF16)
    p_lo = (psum - p_hi.astype(F32)).astype(BF16)
    ovt = ovt_ref[...]
    imp = jnp.dot(ovt, p_hi, preferred_element_type=F32) + jnp.dot(ovt, p_lo, preferred_element_type=F32)

    blk = lax.broadcasted_iota(jnp.int32, (LANES, qs), 0)
    tq = t0 + lax.broadcasted_iota(jnp.int32, (LANES, qs), 1)
    cur = tq // SEL_BLOCK
    forced = (blk == 0) | (blk == cur) | (blk == cur - 1)
    n_sel = (nr * CMP_STRIDE) // SEL_BLOCK
    score = jnp.where(forced, FORCE, jnp.where(blk * SEL_BLOCK <= tq, imp, -1.0))
    score = jnp.where(blk < n_sel, score, -jnp.inf)
    for _ in range(min(SEL_TOPK, n_sel)):
        mx = jnp.max(score, axis=0, keepdims=True)
        idx = jnp.min(jnp.where(score == mx, blk, LANES), axis=0, keepdims=True)
        score = jnp.where(blk == idx, -jnp.inf, score)
    cutoff = jnp.minimum((tq // Q_BLOCK) * (Q_BLOCK // SEL_BLOCK), n_sel)
    bias_t = jnp.where((score == -jnp.inf) & (blk < cutoff), 0.0, NEG)
    bias_ref[...] = bias_t.T.astype(BF16)


def _select(qrawt, kc, vct, ovt):
    b, g, qw, t = qrawt.shape
    nr = kc.shape[2]
    qs = 512

    def whole(rows, cols):
        return pl.BlockSpec((None, None, rows, cols), lambda bi, gi, i: (bi, gi, 0, 0))

    return pl.pallas_call(
        _select_kernel,
        out_shape=(jax.ShapeDtypeStruct((b, g, t, qw), F32), jax.ShapeDtypeStruct((b, g, t, LANES), BF16)),
        grid=(b, g, t // qs),
        in_specs=[
            pl.BlockSpec((None, None, qw, qs), lambda bi, gi, i: (bi, gi, 0, i)),
            whole(nr, LANES), whole(LANES, nr),
            pl.BlockSpec(ovt.shape, lambda bi, gi, i: (0, 0)),
        ],
        out_specs=(pl.BlockSpec((None, None, qs, qw), lambda bi, gi, i: (bi, gi, i, 0)),
                   pl.BlockSpec((None, None, qs, LANES), lambda bi, gi, i: (bi, gi, i, 0))),
        compiler_params=_cparams(("parallel", "parallel", "parallel")),
        name="select",
    )(qrawt, kc, vct, ovt)


def _nsa_kernel(qrope_ref, bias_ref, oc_ref, kst_ref, vs_ref, kwt_ref, vw_ref, gates_ref, nz_ref,
                yb_ref, qaug_ref, acc_ref, m_ref):
    i = pl.program_id(1)
    t0 = i * Q_BLOCK
    Q = Q_BLOCK
    nh = N_HPG
    ng = N_KV_GROUPS
    rows = nh * Q
    t0a = pl.multiple_of(t0, Q)
    rq = lax.broadcasted_iota(jnp.int32, (rows, Q), 0) % Q
    ck = lax.broadcasted_iota(jnp.int32, (rows, Q), 1)
    lane = lax.broadcasted_iota(jnp.int32, (Q, LANES), 1)

    def q_rope(g):
        return jnp.concatenate([qrope_ref[g, :, h * LANES:(h + 1) * LANES] for h in range(nh)], axis=0)

    for g in range(ng):
        qr = q_rope(g)
        qaug_ref[g, :, 0:LANES] = jnp.concatenate([bias_ref[g]] * nh, axis=0)
        qaug_ref[g, :, LANES:2 * LANES] = qr
        s_d = jnp.dot(qr, kst_ref[g, LANES:2 * LANES, pl.ds(t0a, Q)], preferred_element_type=F32)
        s_d = jnp.where(ck <= rq, s_d, NEG)
        m0 = jnp.max(s_d, axis=1, keepdims=True)
        p_d = jnp.exp2(s_d - m0)
        m_ref[g] = jnp.broadcast_to(m0, (rows, LANES))
        acc_ref[g] = jnp.dot(p_d.astype(BF16), vs_ref[g, pl.ds(t0a, Q), :], preferred_element_type=F32)

    ncb = SEL_CHUNK // LANES

    def sel_chunk(c):
        k0 = pl.multiple_of(c * SEL_CHUNK, SEL_CHUNK)
        for g in range(ng):
            s = jnp.dot(qaug_ref[g], kst_ref[g, :, pl.ds(k0, SEL_CHUNK)], preferred_element_type=F32)
            m_old = m_ref[g]
            m_new = jnp.maximum(m_old, jnp.max(s, axis=1, keepdims=True))
            p = jnp.concatenate([jnp.exp2(s[:, cb * LANES:(cb + 1) * LANES] - m_new) for cb in range(ncb)], axis=1)
            acc_ref[g] = (jnp.exp2(m_old - m_new) * acc_ref[g]
                          + jnp.dot(p.astype(BF16), vs_ref[g, pl.ds(k0, SEL_CHUNK), :],
                                    preferred_element_type=F32))
            m_ref[g] = m_new

    def sel_pair(c2, carry):
        sel_chunk(2 * c2)
        sel_chunk(2 * c2 + 1)
        return carry

    n_chunks = (t0 + SEL_CHUNK - 1) // SEL_CHUNK
    lax.fori_loop(0, n_chunks // 2, sel_pair, 0)

    @pl.when(n_chunks % 2 == 1)
    def _():
        sel_chunk(n_chunks - 1)

    wlen = WINDOW + Q
    ws = pl.multiple_of(jnp.maximum(t0 - WINDOW, 0), Q)
    tq_w = t0 + lax.broadcasted_iota(jnp.int32, (rows, wlen), 0) % Q
    diff = tq_w - (ws + lax.broadcasted_iota(jnp.int32, (rows, wlen), 1))
    m_w = (diff >= 0) & (diff < WINDOW)
    for g in range(ng):
        acc = acc_ref[g]
        o_s = acc * (1.0 / acc[:, N_HEAD_DIM:N_HEAD_DIM + 1])
        s_w = jnp.dot(q_rope(g), kwt_ref[g, :, pl.ds(ws, wlen)], preferred_element_type=F32)
        s_w = jnp.where(m_w, s_w, NEG)
        p_w = jnp.exp2(s_w - jnp.max(s_w, axis=1, keepdims=True))
        acc_w = jnp.dot(p_w.astype(BF16), vw_ref[g, pl.ds(ws, wlen), :], preferred_element_type=F32)
        o_w = acc_w * (1.0 / acc_w[:, N_HEAD_DIM:N_HEAD_DIM + 1])
        gates = gates_ref[g]
        outs = []
        for h in range(nh):
            rs = slice(h * Q, (h + 1) * Q)
            outs.append(gates[:, 3 * h:3 * h + 1] * oc_ref[g, :, h * LANES:(h + 1) * LANES]
                        + gates[:, 3 * h + 1:3 * h + 2] * o_s[rs, :]
                        + gates[:, 3 * h + 2:3 * h + 3] * o_w[rs, :])
        for hp in range(nh // 2):
            yn = jnp.where(lane < N_HEAD_DIM, outs[2 * hp], pltpu.roll(outs[2 * hp + 1], N_HEAD_DIM, axis=1))
            ls = slice((g * (nh // 2) + hp) * LANES, (g * (nh // 2) + hp + 1) * LANES)
            yb_ref[:, ls] = (yn * nz_ref[:, ls]).astype(BF16)


def _nsa(pb, b, t, qrope, bias, oc, kst, vs, kwt, vw, gates):
    g = N_KV_GROUPS
    nqb = t // Q_BLOCK
    qw = N_HPG * LANES
    rows = N_HPG * Q_BLOCK

    def qblk(w):
        return pl.BlockSpec((None, g, Q_BLOCK, w), lambda bi, i: (bi, 0, i, 0))

    def whole(r, c):
        return pl.BlockSpec((None, g, r, c), lambda bi, i: (bi, 0, 0, 0), pipeline_mode=pl.Buffered(1))

    return pl.pallas_call(
        _nsa_kernel,
        out_shape=jax.ShapeDtypeStruct((b * t, N_WIDTH), BF16),
        grid=(b, nqb),
        in_specs=[
            qblk(qw), qblk(LANES), qblk(qw),
            whole(2 * LANES, t), whole(t, LANES), whole(LANES, t), whole(t, LANES),
            qblk(LANES),
            pl.BlockSpec((Q_BLOCK, N_WIDTH), lambda bi, i: (bi * nqb + i, B_NZ // N_WIDTH)),
        ],
        out_specs=pl.BlockSpec((Q_BLOCK, N_WIDTH), lambda bi, i: (bi * nqb + i, 0)),
        scratch_shapes=[pltpu.VMEM((g, rows, 2 * LANES), BF16), pltpu.VMEM((g, rows, LANES), F32),
                        pltpu.VMEM((g, rows, LANES), F32)],
        compiler_params=_cparams(("parallel", "arbitrary")),
        name="nsa",
    )(qrope, bias, oc, kst, vs, kwt, vw, gates, pb)


def _outproj_kernel(final, ya_ref, yb_ref, ga_ref, gb_ref, x_ref, wa_ref, wb_ref, wo_ref, fg_ref, o_ref):
    a = jnp.dot(ya_ref[...], wa_ref[...], preferred_element_type=F32)
    bq = jnp.dot(yb_ref[...], wb_ref[...], preferred_element_type=F32)
    merged = ga_ref[...] * a + gb_ref[...] * bq
    y = x_ref[...] + jnp.dot(merged.astype(BF16), wo_ref[...], preferred_element_type=F32)
    if final:
        y = y * lax.rsqrt(jnp.mean(y * y, axis=-1, keepdims=True) + NORM_EPS) * fg_ref[...]
    o_ref[...] = y


def _outproj(ya, yb, pb, x2, wa, wb, wo, fg, final):
    m = x2.shape[0]
    tm = 512

    def full(a):
        return pl.BlockSpec(a.shape, lambda i: (0, 0))

    return pl.pallas_call(
        functools.partial(_outproj_kernel, final),
        out_shape=jax.ShapeDtypeStruct((m, D_MODEL), F32),
        grid=(m // tm,),
        in_specs=[
            pl.BlockSpec((tm, M_WIDTH), lambda i: (i, 0)),
            pl.BlockSpec((tm, N_WIDTH), lambda i: (i, 0)),
            pl.BlockSpec((tm, D_MODEL), lambda i: (i, B_GA // D_MODEL)),
            pl.BlockSpec((tm, D_MODEL), lambda i: (i, B_GB // D_MODEL)),
            pl.BlockSpec((tm, D_MODEL), lambda i: (i, 0)),
            full(wa), full(wb), full(wo), full(fg),
        ],
        out_specs=pl.BlockSpec((tm, D_MODEL), lambda i: (i, 0)),
        compiler_params=_cparams(("parallel",)),
        name="outproj",
    )(ya, yb, pb, pb, x2, wa, wb, wo, fg)


def _permute_w_in(w):
    def zeros(n):
        return jnp.zeros((w.shape[0], n), w.dtype)
    wa = jnp.concatenate([
        w[:, 0:_O_MO],
        w[:, _O_NQ:_O_KC],
        w[:, _O_KS:_O_NG],
        w[:, _O_KC:_O_KS],
        zeros(PA_W - (A_CMP + 2 * LANES))], axis=1)
    wb = jnp.concatenate([
        w[:, _O_MO:_O_MI],
        w[:, _O_GA:_O_END],
        w[:, _O_NZ:_O_GA],
        w[:, _O_MI:_O_NQ],
        w[:, _O_NG:_O_NZ],
        zeros(PB_W - (B_SM + 2 * M_HEADS + 3 * N_HEADS))], axis=1)
    return wa, wb


def _expand_cmp_w1(w1k, w1v):
    half = CMP_BLOCK // 2
    k3 = w1k.astype(BF16).reshape(CMP_BLOCK, 1, N_HEAD_DIM, 1, CMP_HIDDEN)
    v3 = w1v.astype(BF16).reshape(CMP_BLOCK, 1, N_HEAD_DIM, 1, CMP_HIDDEN)
    nc = 2 * N_KV_GROUPS
    eye = jnp.eye(nc, dtype=BF16)
    is_k = (jnp.arange(nc) < N_KV_GROUPS).astype(BF16)
    mk = (eye * is_k[:, None]).reshape(1, nc, 1, nc, 1)
    mv = (eye * (1 - is_k)[:, None]).reshape(1, nc, 1, nc, 1)
    z = k3 * mk + v3 * mv
    z = z.reshape(CMP_BLOCK * nc * N_HEAD_DIM, nc * CMP_HIDDEN)
    return z[:half * nc * N_HEAD_DIM], z[half * nc * N_HEAD_DIM:]


def kernel(x, positions, norm_g, w_in, conv_q_w, conv_q_b, conv_k_w, conv_k_b, b_igate, b_fgate, mh_norm_g,
           cmp_pos_k, cmp_w1_k, cmp_w2_k, cmp_pos_v, cmp_w1_v, cmp_w2_v, b_nsa_gate, w_branch_a, w_branch_b,
           w_out, final_norm_g):
    b, t, _ = x.shape
    assert t % SEL_CHUNK == 0 and t >= WINDOW + Q_BLOCK and t // SEL_BLOCK <= LANES and t % MLSTM_CHUNK == 0
    x2 = x.reshape(b * t, D_MODEL)
    for l in range(norm_g.shape[0]):
        wa, wb_in = _permute_w_in(w_in[l].astype(BF16))
        conv_w = jnp.concatenate([conv_q_w[l], conv_k_w[l]], axis=1)
        conv_b = jnp.concatenate([conv_q_b[l], conv_k_b[l]])[None, :]
        pa = _inproj_a(x2, t, norm_g[l][None, :], wa, conv_w, conv_b)
        pb = _inproj_b(x2, norm_g[l][None, :], wb_in)

        sm_bias = jnp.zeros((1, LANES), F32).at[0, 0:M_HEADS].set(b_igate[l]).at[0, M_HEADS:2 * M_HEADS].set(b_fgate[l])
        ya = _mlstm(pa, pb, b, t, sm_bias, mh_norm_g[l][None, :])

        nr = t // CMP_STRIDE
        r = pa.reshape(b, t, PA_W)[:, :, A_CMP:A_CMP + 2 * LANES].reshape(b, nr, CMP_STRIDE * 2 * LANES)
        wt, wb = _expand_cmp_w1(cmp_w1_k[l], cmp_w1_v[l])
        w2pad = lambda w: jnp.pad(w, ((0, 0), (0, LANES - N_HEAD_DIM))).astype(BF16)
        kc, vct = _compress(r, wt, wb, cmp_pos_k[l].reshape(1, -1), cmp_pos_v[l].reshape(1, -1),
                            cmp_w1_k[l].astype(BF16), cmp_w1_v[l].astype(BF16), w2pad(cmp_w2_k[l]), w2pad(cmp_w2_v[l]))

        half = ROPE_DIM // 2
        inv = jnp.power(jnp.float32(ROPE_THETA), -jnp.arange(half, dtype=F32) * (2.0 / ROPE_DIM))
        lih = jnp.arange(LANES) % N_HEAD_DIM
        inv_row = jnp.where(lih < ROPE_DIM, inv[lih % half], 0.0)[None, :].astype(F32)
        pos_col = positions.astype(F32).reshape(b * t, 1)
        gate_bias = jnp.zeros((1, LANES), F32).at[0, 8:8 + 3 * N_HEADS].set(b_nsa_gate[l])
        qrawt, qrope, kst, vs, kwt, vw, gates = _kvprep(pa, pb, b, t, pos_col, inv_row, gate_bias)

        ci = jnp.arange(nr)[None, :] * CMP_STRIDE
        sj = jnp.arange(LANES)[:, None] * SEL_BLOCK
        overlap_t = ((ci < sj + SEL_BLOCK) & (ci + CMP_BLOCK > sj) & (jnp.arange(nr)[None, :] < nr - 1)
                     & (jnp.arange(LANES)[:, None] < t // SEL_BLOCK)).astype(BF16)
        oc, bias = _select(qrawt, kc, vct, overlap_t)
        yb = _nsa(pb, b, t, qrope, bias, oc, kst, vs, kwt, vw, gates)

        x2 = _outproj(ya, yb, pb, x2, w_branch_a[l].astype(BF16), w_branch_b[l].astype(BF16),
                      w_out[l].astype(BF16), final_norm_g[None, :], l == norm_g.shape[0] - 1)
    return x2.reshape(b, t, D_MODEL)
```

```python
import functools
import math

import jax
import jax.numpy as jnp
from jax import lax
from jax.experimental import pallas as pl
from jax.experimental.pallas import tpu as pltpu

F32 = jnp.float32
BF16 = jnp.bfloat16

D_MODEL = 1024
M_HEADS = 4
M_HEAD_DIM = 256
M_WIDTH = M_HEADS * M_HEAD_DIM
CONV_WIDTH = 4
N_HEADS = 8
N_KV_GROUPS = 2
N_HPG = N_HEADS // N_KV_GROUPS
N_HEAD_DIM = 64
N_WIDTH = N_HEADS * N_HEAD_DIM
CMP_BLOCK = 32
CMP_STRIDE = 16
CMP_HIDDEN = 128
SEL_BLOCK = 64
SEL_TOPK = 16
WINDOW = 512
Q_BLOCK = 128
ROPE_THETA = 500000.0
ROPE_DIM = N_HEAD_DIM // 4
NORM_EPS = 1e-6
NEG = -1e30
FORCE = 1e9

LANES = 128
SUBLANES = 8
MLSTM_CHUNK = 256
SEL_CHUNK = 512
VMEM_LIMIT = 56 * 1024 * 1024

C_MQ, C_MK, C_MV, C_MO, C_MZ, C_GA, C_GB = 0, 1024, 2048, 3072, 4096, 5120, 6144
C_NQ, C_NZ = 7168, 7680
C_KV = 8192
C_CMP = 8704
C_SM = 8960
P_W = 9216
_O_MI, _O_NQ, _O_KC, _O_KS, _O_NG, _O_NZ, _O_GA, _O_END = 5120, 5128, 5640, 5896, 6408, 6432, 6944, 8992


def _cparams(sem):
    return pltpu.CompilerParams(dimension_semantics=sem, vmem_limit_bytes=VMEM_LIMIT)


def _sigmoid(x):
    return 0.5 * jnp.tanh(0.5 * x) + 0.5


def _silu(x):
    h = 0.5 * x
    return h + h * jnp.tanh(h)


def _log_sigmoid(x):
    return jnp.minimum(x, 0.0) - jnp.log1p(jnp.exp(-jnp.abs(x)))


def _inproj_kernel(x_ref, g_ref, w_ref, o_ref, hn_ref):
    @pl.when(pl.program_id(1) == 0)
    def _():
        x = x_ref[...]
        r = lax.rsqrt(jnp.mean(x * x, axis=-1, keepdims=True) + NORM_EPS)
        hn_ref[...] = (x * r * g_ref[...]).astype(BF16)

    o_ref[...] = jnp.dot(hn_ref[...], w_ref[...], preferred_element_type=F32)


def _inproj(x2, norm_g, w_perm):
    m = x2.shape[0]
    tm, tn = 1024, 1024
    return pl.pallas_call(
        _inproj_kernel,
        out_shape=jax.ShapeDtypeStruct((m, P_W), F32),
        grid=(m // tm, P_W // tn),
        in_specs=[
            pl.BlockSpec((tm, D_MODEL), lambda i, j: (i, 0)),
            pl.BlockSpec((1, D_MODEL), lambda i, j: (0, 0)),
            pl.BlockSpec((D_MODEL, tn), lambda i, j: (0, j)),
        ],
        out_specs=pl.BlockSpec((tm, tn), lambda i, j: (i, j)),
        scratch_shapes=[pltpu.VMEM((tm, D_MODEL), BF16)],
        compiler_params=_cparams(("parallel", "arbitrary")),
        name="inproj",
    )(x2, norm_g, w_perm)


def _causal_conv_silu(u, prev_ref, w_ref, b_ref):
    n, w = u.shape
    u3 = u.reshape(n // SUBLANES, SUBLANES, w)
    sub = lax.broadcasted_iota(jnp.int32, u3.shape, 1)
    prev3 = prev_ref[...].reshape(1, SUBLANES, w)
    y = b_ref[...] + u * w_ref[CONV_WIDTH - 1:CONV_WIDTH, :]
    for d in range(1, CONV_WIDTH):
        r = pltpu.roll(u3, d, axis=1)
        above = jnp.concatenate([pltpu.roll(prev3, d, axis=1), r[:-1]], axis=0)
        y = y + jnp.where(sub < d, above, r).reshape(n, w) * w_ref[CONV_WIDTH - 1 - d:CONV_WIDTH - d, :]
    prev_ref[...] = u[n - SUBLANES:n, :]
    return _silu(y)


def _mlstm_kernel(q_ref, k_ref, v_ref, og_ref, z_ref, sm_ref, cqw_ref, cqb_ref, ckw_ref, ckb_ref,
                  sb_ref, ng_ref, ya_ref, qprev, kprev, c_state, n_state, m_state):
    L = q_ref.shape[0]
    d = M_HEAD_DIM

    @pl.when(pl.program_id(1) == 0)
    def _():
        qprev[...] = jnp.zeros_like(qprev)
        kprev[...] = jnp.zeros_like(kprev)
        c_state[...] = jnp.zeros_like(c_state)
        n_state[...] = jnp.zeros_like(n_state)
        m_state[...] = jnp.zeros_like(m_state)

    mq = _causal_conv_silu(q_ref[...], qprev, cqw_ref, cqb_ref)
    mk = _causal_conv_silu(k_ref[...], kprev, ckw_ref, ckb_ref) * (d ** -0.5)

    sm = sm_ref[...] + sb_ref[...]
    lf = _log_sigmoid(sm)
    row = lax.broadcasted_iota(jnp.int32, (L, L), 0)
    col = lax.broadcasted_iota(jnp.int32, (L, L), 1)
    causal = row >= col
    tril = causal.astype(F32)
    triu = (row <= col).astype(F32)
    a_col = jnp.dot(tril, lf, precision=lax.Precision.HIGHEST, preferred_element_type=F32)
    sm_t = sm.T[0:8, :]
    a_row = jnp.dot(_log_sigmoid(sm_t), triu, precision=lax.Precision.HIGHEST,
                    preferred_element_type=F32)

    v = v_ref[...]
    for h in range(M_HEADS):
        hs = slice(h * d, (h + 1) * d)
        a_c = a_col[:, 4 + h:5 + h]
        li_c = sm[:, h:h + 1]
        a_r = a_row[4 + h:5 + h, :]
        li_r = sm_t[h:h + 1, :]
        m_prev = m_state[h:h + 1, 0:1]
        a_end = a_c[L - 1:L, :]

        log_d = jnp.where(causal, a_c - a_r + li_r, -jnp.inf)
        inter = a_c + m_prev
        m_t = jnp.maximum(inter, jnp.max(log_d, axis=1, keepdims=True))
        w_inter = jnp.exp(inter - m_t)
        dm = jnp.exp(log_d - m_t)

        qh = mq[:, hs]
        kh = mk[:, hs]
        qb = qh.astype(BF16)
        kb = kh.astype(BF16)
        vb = v[:, hs].astype(BF16)
        s = lax.dot_general(qb, kb, (((1,), (1,)), ((), ())), preferred_element_type=F32) * dm
        c_old = c_state[h]
        n_old = n_state[h:h + 1, :]
        num = (w_inter * jnp.dot(qb, c_old.astype(BF16), preferred_element_type=F32)
               + jnp.dot(s.astype(BF16), vb, preferred_element_type=F32))
        den = (w_inter * jnp.sum(qh * n_old, axis=1, keepdims=True)
               + jnp.sum(s, axis=1, keepdims=True))
        hh = num * (1.0 / jnp.maximum(jnp.abs(den), jnp.exp(-m_t)))

        logw = a_end - a_c + li_c
        m_new = jnp.maximum(a_end + m_prev, jnp.max(logw, axis=0, keepdims=True))
        wk = jnp.exp(logw - m_new)
        decay = jnp.exp(a_end + m_prev - m_new)
        kw = kh * wk
        c_state[h] = decay * c_old + lax.dot_general(kw.astype(BF16), vb, (((0,), (0,)), ((), ())),
                                                     preferred_element_type=F32)
        n_state[h:h + 1, :] = decay * n_old + jnp.sum(kw, axis=0, keepdims=True)
        m_state[h:h + 1, :] = jnp.broadcast_to(m_new, (1, LANES))

        mu = jnp.mean(hh, axis=1, keepdims=True)
        dc = hh - mu
        var = jnp.mean(dc * dc, axis=1, keepdims=True)
        hn = dc * lax.rsqrt(var + NORM_EPS) * ng_ref[:, hs]
        ya = _sigmoid(og_ref[:, hs]) * hn * _silu(z_ref[:, hs])
        ya_ref[:, hs] = ya.astype(BF16)


def _mlstm(p, b, t, conv_q_w, conv_q_b, conv_k_w, conv_k_b, sm_bias, mh_norm_g):
    L = MLSTM_CHUNK
    nc = t // L

    def colspec(cstart):
        return pl.BlockSpec((L, M_WIDTH), lambda bi, c: (bi * nc + c, cstart // M_WIDTH))

    def full(shape):
        return pl.BlockSpec(shape, lambda bi, c: (0,) * len(shape))

    return pl.pallas_call(
        _mlstm_kernel,
        out_shape=jax.ShapeDtypeStruct((b * t, M_WIDTH), BF16),
        grid=(b, nc),
        in_specs=[
            colspec(C_MQ), colspec(C_MK), colspec(C_MV), colspec(C_MO), colspec(C_MZ),
            pl.BlockSpec((L, LANES), lambda bi, c: (bi * nc + c, C_SM // LANES)),
            full((CONV_WIDTH, M_WIDTH)), full((1, M_WIDTH)), full((CONV_WIDTH, M_WIDTH)), full((1, M_WIDTH)),
            full((1, LANES)), full((1, M_WIDTH)),
        ],
        out_specs=pl.BlockSpec((L, M_WIDTH), lambda bi, c: (bi * nc + c, 0)),
        scratch_shapes=[
            pltpu.VMEM((SUBLANES, M_WIDTH), F32), pltpu.VMEM((SUBLANES, M_WIDTH), F32),
            pltpu.VMEM((M_HEADS, M_HEAD_DIM, M_HEAD_DIM), F32),
            pltpu.VMEM((8, M_HEAD_DIM), F32), pltpu.VMEM((8, LANES), F32),
        ],
        compiler_params=_cparams(("parallel", "arbitrary")),
        name="mlstm",
    )(p, p, p, p, p, p, conv_q_w, conv_q_b, conv_k_w, conv_k_b, sm_bias, mh_norm_g)


def _compress_kernel(r_ref, wt_ref, wb_ref, posk_ref, posv_ref, w1k_ref, w1v_ref, w2k_ref, w2v_ref,
                     kc_ref, vct_ref):
    r = r_ref[...].astype(BF16)
    nr = r.shape[0]
    u = jnp.dot(r, wt_ref[...], preferred_element_type=F32)
    v = jnp.dot(r, wb_ref[...], preferred_element_type=F32)
    v = pltpu.roll(v, nr - 1, axis=0)
    rowi = lax.broadcasted_iota(jnp.int32, v.shape, 0)
    v = jnp.where(rowi < nr - 1, v, 0.0)

    def pos_term(pos_ref, w_ref):
        pb = jnp.broadcast_to(pos_ref[...].astype(BF16), (8, CMP_BLOCK * N_HEAD_DIM))
        return jnp.dot(pb, w_ref[...], preferred_element_type=F32)[0:1, :]

    pk = pos_term(posk_ref, w1k_ref)
    pv = pos_term(posv_ref, w1v_ref)
    hid = u + v + jnp.concatenate([pk, pk, pv, pv], axis=1)
    act = 0.5 * hid * (1.0 + jnp.tanh(math.sqrt(2.0 / math.pi) * (hid + 0.044715 * (hid * hid * hid))))
    for c in range(2 * N_KV_GROUPS):
        w2 = w2k_ref[...] if c < N_KV_GROUPS else w2v_ref[...]
        out = jnp.dot(act[:, c * CMP_HIDDEN:(c + 1) * CMP_HIDDEN].astype(BF16), w2,
                      preferred_element_type=F32)
        if c < N_KV_GROUPS:
            kc_ref[c] = out.astype(BF16)
        else:
            vct_ref[c - N_KV_GROUPS] = out.T.astype(BF16)


def _compress(r, wt, wb, posk, posv, w1k, w1v, w2k, w2v):
    b, nr, rw = r.shape
    g = N_KV_GROUPS

    def full(a):
        return pl.BlockSpec(a.shape, lambda bi: (0,) * a.ndim)

    return pl.pallas_call(
        _compress_kernel,
        out_shape=(jax.ShapeDtypeStruct((b, g, nr, LANES), BF16), jax.ShapeDtypeStruct((b, g, LANES, nr), BF16)),
        grid=(b,),
        in_specs=[pl.BlockSpec((None, nr, rw), lambda bi: (bi, 0, 0)),
                  full(wt), full(wb), full(posk), full(posv), full(w1k), full(w1v), full(w2k), full(w2v)],
        out_specs=(pl.BlockSpec((None, g, nr, LANES), lambda bi: (bi, 0, 0, 0)),
                   pl.BlockSpec((None, g, LANES, nr), lambda bi: (bi, 0, 0, 0))),
        compiler_params=_cparams(("parallel",)),
        name="compress",
    )(r, wt, wb, posk, posv, w1k, w1v, w2k, w2v)


def _kvprep_kernel(q_ref, kv_ref, sm_ref, pos_ref, inv_ref, gb_ref,
                   qrawt_ref, qrope_ref, kst_ref, vs_ref, kwt_ref, vw_ref, gates_ref):
    tk = q_ref.shape[0]
    lane = lax.broadcasted_iota(jnp.int32, (tk, LANES), 1)
    lih = lane % N_HEAD_DIM
    ang = pos_ref[...] * inv_ref[...]
    cos = jnp.cos(ang)
    sin = jnp.sin(ang)
    half = ROPE_DIM // 2
    s_up = jnp.where(lih < half, -sin, 0.0)
    s_dn = jnp.where((lih >= half) & (lih < ROPE_DIM), sin, 0.0)

    def rope(u):
        return u * cos + pltpu.roll(u, LANES - half, axis=1) * s_up + pltpu.roll(u, half, axis=1) * s_dn

    def head_pad(u, odd):
        if odd:
            u = pltpu.roll(u, N_HEAD_DIM, axis=1)
        return jnp.where(lane < N_HEAD_DIM, u, 0.0)

    scale = N_HEAD_DIM ** -0.5 * math.log2(math.e)
    zeros_t = jnp.zeros((N_HEAD_DIM, tk), BF16)
    for g in range(N_KV_GROUPS):
        for hp in range(N_HPG // 2):
            c0 = g * (N_HPG * N_HEAD_DIM) + hp * LANES
            u = q_ref[:, c0:c0 + LANES] * scale
            ur = rope(u)
            u_t = u.T.astype(BF16)
            for odd in range(2):
                h = 2 * hp + odd
                qrawt_ref[g, h * LANES:h * LANES + N_HEAD_DIM, :] = u_t[odd * N_HEAD_DIM:(odd + 1) * N_HEAD_DIM, :]
                qrawt_ref[g, h * LANES + N_HEAD_DIM:(h + 1) * LANES, :] = zeros_t
                qrope_ref[g, :, h * LANES:(h + 1) * LANES] = head_pad(ur, odd).astype(BF16)

    ks_t = rope(kv_ref[:, 0:LANES]).T
    kw_t = rope(kv_ref[:, 2 * LANES:3 * LANES]).T
    vs = kv_ref[:, LANES:2 * LANES]
    vw = kv_ref[:, 3 * LANES:4 * LANES]
    t_glob = pl.program_id(1) * tk + lax.broadcasted_iota(jnp.int32, (LANES, tk), 1)
    blk_row = lax.broadcasted_iota(jnp.int32, (LANES, tk), 0)
    onehot = jnp.where(blk_row == t_glob // SEL_BLOCK, 1.0, 0.0).astype(BF16)
    ones_lane = jnp.where(lane == N_HEAD_DIM, 1.0, 0.0)
    gates = _sigmoid(sm_ref[...] + gb_ref[...])
    for g in range(N_KV_GROUPS):
        rs = slice(g * N_HEAD_DIM, (g + 1) * N_HEAD_DIM)
        kst_ref[g, 0:LANES, :] = onehot
        kst_ref[g, LANES:LANES + N_HEAD_DIM, :] = ks_t[rs, :].astype(BF16)
        kst_ref[g, LANES + N_HEAD_DIM:2 * LANES, :] = zeros_t
        kwt_ref[g, 0:N_HEAD_DIM, :] = kw_t[rs, :].astype(BF16)
        kwt_ref[g, N_HEAD_DIM:LANES, :] = zeros_t
        vs_ref[g] = (head_pad(vs, g) + ones_lane).astype(BF16)
        vw_ref[g] = (head_pad(vw, g) + ones_lane).astype(BF16)
        goff = 8 + g * (3 * N_HPG)
        gates_ref[g] = pltpu.roll(gates, LANES - goff, axis=1)


def _kvprep(p, b, t, pos_col, inv_row, gate_bias):
    tk = 512
    nt = t // tk
    g = N_KV_GROUPS
    qw = N_HPG * LANES

    def tok(shape):
        return pl.BlockSpec((None, g) + shape, lambda bi, i: (bi, 0, i, 0))

    def tok_t(rows):
        return pl.BlockSpec((None, g, rows, tk), lambda bi, i: (bi, 0, 0, i))

    return pl.pallas_call(
        _kvprep_kernel,
        out_shape=(
            jax.ShapeDtypeStruct((b, g, qw, t), BF16), jax.ShapeDtypeStruct((b, g, t, qw), BF16),
            jax.ShapeDtypeStruct((b, g, 2 * LANES, t), BF16), jax.ShapeDtypeStruct((b, g, t, LANES), BF16),
            jax.ShapeDtypeStruct((b, g, LANES, t), BF16), jax.ShapeDtypeStruct((b, g, t, LANES), BF16),
            jax.ShapeDtypeStruct((b, g, t, LANES), F32),
        ),
        grid=(b, nt),
        in_specs=[
            pl.BlockSpec((tk, N_WIDTH), lambda bi, i: (bi * nt + i, C_NQ // N_WIDTH)),
            pl.BlockSpec((tk, 4 * LANES), lambda bi, i: (bi * nt + i, C_KV // (4 * LANES))),
            pl.BlockSpec((tk, LANES), lambda bi, i: (bi * nt + i, C_SM // LANES)),
            pl.BlockSpec((tk, 1), lambda bi, i: (bi * nt + i, 0)),
            pl.BlockSpec((1, LANES), lambda bi, i: (0, 0)),
            pl.BlockSpec((1, LANES), lambda bi, i: (0, 0)),
        ],
        out_specs=(tok_t(qw), tok((tk, qw)), tok_t(2 * LANES), tok((tk, LANES)),
                   tok_t(LANES), tok((tk, LANES)), tok((tk, LANES))),
        compiler_params=_cparams(("parallel", "arbitrary")),
        name="kvprep",
    )(p, p, p, pos_col, inv_row, gate_bias)


def _select_kernel(qt_ref, kc_ref, vct_ref, ovt_ref, oc_ref, bias_ref):
    qs = qt_ref.shape[1]
    nr = kc_ref.shape[0]
    t0 = pl.program_id(2) * qs
    m_c = (lax.broadcasted_iota(jnp.int32, (nr, qs), 0) * CMP_STRIDE + (CMP_BLOCK - 1)
           <= t0 + lax.broadcasted_iota(jnp.int32, (nr, qs), 1))
    kc = kc_ref[...]
    vct = vct_ref[...]
    psum = jnp.zeros((nr, qs), F32)
    for h in range(N_HPG):
        s = jnp.dot(kc, qt_ref[h * LANES:(h + 1) * LANES, :], preferred_element_type=F32)
        s = jnp.where(m_c, s, NEG)
        e = jnp.where(m_c, jnp.exp2(s - jnp.max(s, axis=0, keepdims=True)), 0.0)
        den = jnp.sum(e, axis=0, keepdims=True)
        pc = e * (1.0 / jnp.where(den == 0.0, 1.0, den))
        psum = psum + pc
        oc_t = jnp.dot(vct, pc.astype(BF16), preferred_element_type=F32)
        oc_ref[:, h * LANES:(h + 1) * LANES] = oc_t.T
    p_hi = psum.astype(BF16)
    p_lo = (psum - p_hi.astype(F32)).astype(BF16)
    ovt = ovt_ref[...]
    imp = jnp.dot(ovt, p_hi, preferred_element_type=F32) + jnp.dot(ovt, p_lo, preferred_element_type=F32)

    blk = lax.broadcasted_iota(jnp.int32, (LANES, qs), 0)
    tq = t0 + lax.broadcasted_iota(jnp.int32, (LANES, qs), 1)
    cur = tq // SEL_BLOCK
    forced = (blk == 0) | (blk == cur) | (blk == cur - 1)
    n_sel = (nr * CMP_STRIDE) // SEL_BLOCK
    score = jnp.where(forced, FORCE, jnp.where(blk * SEL_BLOCK <= tq, imp, -1.0))
    score = jnp.where(blk < n_sel, score, -jnp.inf)
    for _ in range(min(SEL_TOPK, n_sel)):
        mx = jnp.max(score, axis=0, keepdims=True)
        idx = jnp.min(jnp.where(score == mx, blk, LANES), axis=0, keepdims=True)
        score = jnp.where(blk == idx, -jnp.inf, score)
    cutoff = jnp.minimum((tq // Q_BLOCK) * (Q_BLOCK // SEL_BLOCK), n_sel)
    bias_t = jnp.where((score == -jnp.inf) & (blk < cutoff), 0.0, NEG)
    bias_ref[...] = bias_t.T.astype(BF16)


def _select(qrawt, kc, vct, ovt):
    b, g, qw, t = qrawt.shape
    nr = kc.shape[2]
    qs = 512

    def whole(rows, cols):
        return pl.BlockSpec((None, None, rows, cols), lambda bi, gi, i: (bi, gi, 0, 0))

    return pl.pallas_call(
        _select_kernel,
        out_shape=(jax.ShapeDtypeStruct((b, g, t, qw), F32), jax.ShapeDtypeStruct((b, g, t, LANES), BF16)),
        grid=(b, g, t // qs),
        in_specs=[
            pl.BlockSpec((None, None, qw, qs), lambda bi, gi, i: (bi, gi, 0, i)),
            whole(nr, LANES), whole(LANES, nr),
            pl.BlockSpec(ovt.shape, lambda bi, gi, i: (0, 0)),
        ],
        out_specs=(pl.BlockSpec((None, None, qs, qw), lambda bi, gi, i: (bi, gi, i, 0)),
                   pl.BlockSpec((None, None, qs, LANES), lambda bi, gi, i: (bi, gi, i, 0))),
        compiler_params=_cparams(("parallel", "parallel", "parallel")),
        name="select",
    )(qrawt, kc, vct, ovt)


def _nsa_kernel(qrope_ref, bias_ref, oc_ref, kst_ref, vs_ref, kwt_ref, vw_ref, gates_ref, nz_ref,
                yb_ref, qaug_ref, acc_ref, m_ref, s_ref):
    i = pl.program_id(1)
    t0 = i * Q_BLOCK
    Q = Q_BLOCK
    nh = N_HPG
    ng = N_KV_GROUPS
    rows = nh * Q
    t0a = pl.multiple_of(t0, Q)
    rq = lax.broadcasted_iota(jnp.int32, (rows, Q), 0) % Q
    ck = lax.broadcasted_iota(jnp.int32, (rows, Q), 1)
    lane = lax.broadcasted_iota(jnp.int32, (Q, LANES), 1)

    def q_rope(g):
        return jnp.concatenate([qrope_ref[g, :, h * LANES:(h + 1) * LANES] for h in range(nh)], axis=0)

    for g in range(ng):
        qr = q_rope(g)
        qaug_ref[g, :, 0:LANES] = jnp.concatenate([bias_ref[g]] * nh, axis=0)
        qaug_ref[g, :, LANES:2 * LANES] = qr
        s_d = jnp.dot(qr, kst_ref[g, LANES:2 * LANES, pl.ds(t0a, Q)], preferred_element_type=F32)
        s_d = jnp.where(ck <= rq, s_d, NEG)
        m0 = jnp.max(s_d, axis=1, keepdims=True)
        p_d = jnp.exp2(s_d - m0)
        m_ref[g] = jnp.broadcast_to(m0, (rows, LANES))
        acc_ref[g] = jnp.dot(p_d.astype(BF16), vs_ref[g, pl.ds(t0a, Q), :], preferred_element_type=F32)

    ncb = SEL_CHUNK // LANES
    n_chunks = (t0 + SEL_CHUNK - 1) // SEL_CHUNK
    last = jnp.maximum(n_chunks - 1, 0)

    def scores(g, c, slot):
        k0 = pl.multiple_of(c * SEL_CHUNK, SEL_CHUNK)
        s_ref[slot, g] = jnp.dot(qaug_ref[g], kst_ref[g, :, pl.ds(k0, SEL_CHUNK)], preferred_element_type=F32)

    def consume(g, c, slot):
        k0 = pl.multiple_of(c * SEL_CHUNK, SEL_CHUNK)
        s = s_ref[slot, g]
        m_old = m_ref[g]
        m_new = jnp.maximum(m_old, jnp.max(s, axis=1, keepdims=True))
        p = jnp.concatenate([jnp.exp2(s[:, cb * LANES:(cb + 1) * LANES] - m_new) for cb in range(ncb)], axis=1)
        acc_ref[g] = (jnp.exp2(m_old - m_new) * acc_ref[g]
                      + jnp.dot(p.astype(BF16), vs_ref[g, pl.ds(k0, SEL_CHUNK), :], preferred_element_type=F32))
        m_ref[g] = m_new

    @pl.when(n_chunks > 0)
    def _():
        for g in range(ng):
            scores(g, 0, 0)

    def sel_pair(c2, carry):
        c = 2 * c2
        for g in range(ng):
            scores(g, c + 1, 1)
            consume(g, c, 0)
        for g in range(ng):
            scores(g, jnp.minimum(c + 2, last), 0)
            consume(g, c + 1, 1)
        return carry

    lax.fori_loop(0, n_chunks // 2, sel_pair, 0)

    @pl.when(n_chunks % 2 == 1)
    def _():
        for g in range(ng):
            consume(g, n_chunks - 1, 0)

    wlen = WINDOW + Q
    ws = pl.multiple_of(jnp.maximum(t0 - WINDOW, 0), Q)
    tq_w = t0 + lax.broadcasted_iota(jnp.int32, (rows, wlen), 0) % Q
    diff = tq_w - (ws + lax.broadcasted_iota(jnp.int32, (rows, wlen), 1))
    m_w = (diff >= 0) & (diff < WINDOW)
    for g in range(ng):
        acc = acc_ref[g]
        o_s = acc * (1.0 / acc[:, N_HEAD_DIM:N_HEAD_DIM + 1])
        s_w = jnp.dot(q_rope(g), kwt_ref[g, :, pl.ds(ws, wlen)], preferred_element_type=F32)
        s_w = jnp.where(m_w, s_w, NEG)
        p_w = jnp.exp2(s_w - jnp.max(s_w, axis=1, keepdims=True))
        acc_w = jnp.dot(p_w.astype(BF16), vw_ref[g, pl.ds(ws, wlen), :], preferred_element_type=F32)
        o_w = acc_w * (1.0 / acc_w[:, N_HEAD_DIM:N_HEAD_DIM + 1])
        gates = gates_ref[g]
        outs = []
        for h in range(nh):
            rs = slice(h * Q, (h + 1) * Q)
            outs.append(gates[:, 3 * h:3 * h + 1] * oc_ref[g, :, h * LANES:(h + 1) * LANES]
                        + gates[:, 3 * h + 1:3 * h + 2] * o_s[rs, :]
                        + gates[:, 3 * h + 2:3 * h + 3] * o_w[rs, :])
        for hp in range(nh // 2):
            yn = jnp.where(lane < N_HEAD_DIM, outs[2 * hp], pltpu.roll(outs[2 * hp + 1], N_HEAD_DIM, axis=1))
            ls = slice((g * (nh // 2) + hp) * LANES, (g * (nh // 2) + hp + 1) * LANES)
            yb_ref[:, ls] = (yn * _silu(nz_ref[:, ls])).astype(BF16)


def _nsa(p, b, t, qrope, bias, oc, kst, vs, kwt, vw, gates):
    g = N_KV_GROUPS
    nqb = t // Q_BLOCK
    qw = N_HPG * LANES
    rows = N_HPG * Q_BLOCK

    def qblk(w):
        return pl.BlockSpec((None, g, Q_BLOCK, w), lambda bi, i: (bi, 0, i, 0))

    def whole(r, c):
        return pl.BlockSpec((None, g, r, c), lambda bi, i: (bi, 0, 0, 0), pipeline_mode=pl.Buffered(1))

    return pl.pallas_call(
        _nsa_kernel,
        out_shape=jax.ShapeDtypeStruct((b * t, N_WIDTH), BF16),
        grid=(b, nqb),
        in_specs=[
            qblk(qw), qblk(LANES), qblk(qw),
            whole(2 * LANES, t), whole(t, LANES), whole(LANES, t), whole(t, LANES),
            qblk(LANES),
            pl.BlockSpec((Q_BLOCK, N_WIDTH), lambda bi, i: (bi * nqb + i, C_NZ // N_WIDTH)),
        ],
        out_specs=pl.BlockSpec((Q_BLOCK, N_WIDTH), lambda bi, i: (bi * nqb + i, 0)),
        scratch_shapes=[pltpu.VMEM((g, rows, 2 * LANES), BF16), pltpu.VMEM((g, rows, LANES), F32),
                        pltpu.VMEM((g, rows, LANES), F32), pltpu.VMEM((2, g, rows, SEL_CHUNK), F32)],
        compiler_params=_cparams(("parallel", "arbitrary")),
        name="nsa",
    )(qrope, bias, oc, kst, vs, kwt, vw, gates, p)


def _outproj_kernel(final, ya_ref, yb_ref, ga_ref, gb_ref, x_ref, wa_ref, wb_ref, wo_ref, fg_ref, o_ref):
    a = jnp.dot(ya_ref[...], wa_ref[...], preferred_element_type=F32)
    bq = jnp.dot(yb_ref[...], wb_ref[...], preferred_element_type=F32)
    merged = _sigmoid(ga_ref[...]) * a + _sigmoid(gb_ref[...]) * bq
    y = x_ref[...] + jnp.dot(merged.astype(BF16), wo_ref[...], preferred_element_type=F32)
    if final:
        y = y * lax.rsqrt(jnp.mean(y * y, axis=-1, keepdims=True) + NORM_EPS) * fg_ref[...]
    o_ref[...] = y


def _outproj(ya, yb, p, x2, wa, wb, wo, fg, final):
    m = x2.shape[0]
    tm = 512

    def full(a):
        return pl.BlockSpec(a.shape, lambda i: (0, 0))

    return pl.pallas_call(
        functools.partial(_outproj_kernel, final),
        out_shape=jax.ShapeDtypeStruct((m, D_MODEL), F32),
        grid=(m // tm,),
        in_specs=[
            pl.BlockSpec((tm, M_WIDTH), lambda i: (i, 0)),
            pl.BlockSpec((tm, N_WIDTH), lambda i: (i, 0)),
            pl.BlockSpec((tm, D_MODEL), lambda i: (i, C_GA // D_MODEL)),
            pl.BlockSpec((tm, D_MODEL), lambda i: (i, C_GB // D_MODEL)),
            pl.BlockSpec((tm, D_MODEL), lambda i: (i, 0)),
            full(wa), full(wb), full(wo), full(fg),
        ],
        out_specs=pl.BlockSpec((tm, D_MODEL), lambda i: (i, 0)),
        compiler_params=_cparams(("parallel",)),
        name="outproj",
    )(ya, yb, p, p, x2, wa, wb, wo, fg)


def _permute_w_in(w):
    pad = jnp.zeros((w.shape[0], P_W - _O_END), w.dtype)
    return jnp.concatenate([
        w[:, 0:_O_MI],
        w[:, _O_GA:_O_END],
        w[:, _O_NQ:_O_KC],
        w[:, _O_NZ:_O_GA],
        w[:, _O_KS:_O_NG],
        w[:, _O_KC:_O_KS],
        w[:, _O_MI:_O_NQ],
        w[:, _O_NG:_O_NZ],
        pad], axis=1)


def _expand_cmp_w1(w1k, w1v):
    half = CMP_BLOCK // 2
    k3 = w1k.astype(BF16).reshape(CMP_BLOCK, 1, N_HEAD_DIM, 1, CMP_HIDDEN)
    v3 = w1v.astype(BF16).reshape(CMP_BLOCK, 1, N_HEAD_DIM, 1, CMP_HIDDEN)
    nc = 2 * N_KV_GROUPS
    eye = jnp.eye(nc, dtype=BF16)
    is_k = (jnp.arange(nc) < N_KV_GROUPS).astype(BF16)
    mk = (eye * is_k[:, None]).reshape(1, nc, 1, nc, 1)
    mv = (eye * (1 - is_k)[:, None]).reshape(1, nc, 1, nc, 1)
    z = k3 * mk + v3 * mv
    z = z.reshape(CMP_BLOCK * nc * N_HEAD_DIM, nc * CMP_HIDDEN)
    return z[:half * nc * N_HEAD_DIM], z[half * nc * N_HEAD_DIM:]


def kernel(x, positions, norm_g, w_in, conv_q_w, conv_q_b, conv_k_w, conv_k_b, b_igate, b_fgate, mh_norm_g,
           cmp_pos_k, cmp_w1_k, cmp_w2_k, cmp_pos_v, cmp_w1_v, cmp_w2_v, b_nsa_gate, w_branch_a, w_branch_b,
           w_out, final_norm_g):
    b, t, _ = x.shape
    assert t % (2 * SEL_CHUNK) == 0 and t >= WINDOW + Q_BLOCK and t // SEL_BLOCK <= LANES and t % MLSTM_CHUNK == 0
    x2 = x.reshape(b * t, D_MODEL)
    for l in range(norm_g.shape[0]):
        p = _inproj(x2, norm_g[l][None, :], _permute_w_in(w_in[l].astype(BF16)))

        sm_bias = jnp.zeros((1, LANES), F32).at[0, 0:M_HEADS].set(b_igate[l]).at[0, M_HEADS:2 * M_HEADS].set(b_fgate[l])
        ya = _mlstm(p, b, t, conv_q_w[l], conv_q_b[l][None, :], conv_k_w[l], conv_k_b[l][None, :], sm_bias,
                    mh_norm_g[l][None, :])

        nr = t // CMP_STRIDE
        r = p.reshape(b, t, P_W)[:, :, C_CMP:C_CMP + 2 * LANES].reshape(b, nr, CMP_STRIDE * 2 * LANES)
        wt, wb = _expand_cmp_w1(cmp_w1_k[l], cmp_w1_v[l])
        w2pad = lambda w: jnp.pad(w, ((0, 0), (0, LANES - N_HEAD_DIM))).astype(BF16)
        kc, vct = _compress(r, wt, wb, cmp_pos_k[l].reshape(1, -1), cmp_pos_v[l].reshape(1, -1),
                            cmp_w1_k[l].astype(BF16), cmp_w1_v[l].astype(BF16), w2pad(cmp_w2_k[l]), w2pad(cmp_w2_v[l]))

        half = ROPE_DIM // 2
        inv = jnp.power(jnp.float32(ROPE_THETA), -jnp.arange(half, dtype=F32) * (2.0 / ROPE_DIM))
        lih = jnp.arange(LANES) % N_HEAD_DIM
        inv_row = jnp.where(lih < ROPE_DIM, inv[lih % half], 0.0)[None, :].astype(F32)
        pos_col = positions.astype(F32).reshape(b * t, 1)
        gate_bias = jnp.zeros((1, LANES), F32).at[0, 8:8 + 3 * N_HEADS].set(b_nsa_gate[l])
        qrawt, qrope, kst, vs, kwt, vw, gates = _kvprep(p, b, t, pos_col, inv_row, gate_bias)

        ci = jnp.arange(nr)[None, :] * CMP_STRIDE
        sj = jnp.arange(LANES)[:, None] * SEL_BLOCK
        overlap_t = ((ci < sj + SEL_BLOCK) & (ci + CMP_BLOCK > sj) & (jnp.arange(nr)[None, :] < nr - 1)
                     & (jnp.arange(LANES)[:, None] < t // SEL_BLOCK)).astype(BF16)
        oc, bias = _select(qrawt, kc, vct, overlap_t)
        yb = _nsa(p, b, t, qrope, bias, oc, kst, vs, kwt, vw, gates)

        x2 = _outproj(ya, yb, p, x2, w_branch_a[l].astype(BF16), w_branch_b[l].astype(BF16),
                      w_out[l].astype(BF16), final_norm_g[None, :], l == norm_g.shape[0] - 1)
    return x2.reshape(b, t, D_MODEL)
```

```python
import functools
import math

import jax
import jax.numpy as jnp
from jax import lax
from jax.experimental import pallas as pl
from jax.experimental.pallas import tpu as pltpu

F32 = jnp.float32
BF16 = jnp.bfloat16

D_MODEL = 1024
M_HEADS = 4
M_HEAD_DIM = 256
M_WIDTH = M_HEADS * M_HEAD_DIM
CONV_WIDTH = 4
N_HEADS = 8
N_KV_GROUPS = 2
N_HPG = N_HEADS // N_KV_GROUPS
N_HEAD_DIM = 64
N_WIDTH = N_HEADS * N_HEAD_DIM
CMP_BLOCK = 32
CMP_STRIDE = 16
CMP_HIDDEN = 128
SEL_BLOCK = 64
SEL_TOPK = 16
WINDOW = 512
Q_BLOCK = 128
ROPE_THETA = 500000.0
ROPE_DIM = N_HEAD_DIM // 4
NORM_EPS = 1e-6
NEG = -1e30
FORCE = 1e9

LANES = 128
SUBLANES = 8
MLSTM_CHUNK = 256
SEL_CHUNK = 512
VMEM_LIMIT = 56 * 1024 * 1024

C_MQ, C_MK, C_MV, C_MO, C_MZ, C_GA, C_GB = 0, 1024, 2048, 3072, 4096, 5120, 6144
C_NQ, C_NZ = 7168, 7680
C_KV = 8192
C_CMP = 8704
C_SM = 8960
P_W = 9216
_O_MI, _O_NQ, _O_KC, _O_KS, _O_NG, _O_NZ, _O_GA, _O_END = 5120, 5128, 5640, 5896, 6408, 6432, 6944, 8992


def _cparams(sem):
    return pltpu.CompilerParams(dimension_semantics=sem, vmem_limit_bytes=VMEM_LIMIT)


def _sigmoid(x):
    return 0.5 * jnp.tanh(0.5 * x) + 0.5


def _silu(x):
    h = 0.5 * x
    return h + h * jnp.tanh(h)


def _log_sigmoid(x):
    return jnp.minimum(x, 0.0) - jnp.log1p(jnp.exp(-jnp.abs(x)))


def _inproj_kernel(x_ref, g_ref, w_ref, o_ref, hn_ref):
    @pl.when(pl.program_id(1) == 0)
    def _():
        x = x_ref[...]
        r = lax.rsqrt(jnp.mean(x * x, axis=-1, keepdims=True) + NORM_EPS)
        hn_ref[...] = (x * r * g_ref[...]).astype(BF16)

    o_ref[...] = jnp.dot(hn_ref[...], w_ref[...], preferred_element_type=F32)


def _inproj(x2, norm_g, w_perm):
    m = x2.shape[0]
    tm, tn = 1024, 1024
    return pl.pallas_call(
        _inproj_kernel,
        out_shape=jax.ShapeDtypeStruct((m, P_W), F32),
        grid=(m // tm, P_W // tn),
        in_specs=[
            pl.BlockSpec((tm, D_MODEL), lambda i, j: (i, 0)),
            pl.BlockSpec((1, D_MODEL), lambda i, j: (0, 0)),
            pl.BlockSpec((D_MODEL, tn), lambda i, j: (0, j)),
        ],
        out_specs=pl.BlockSpec((tm, tn), lambda i, j: (i, j)),
        scratch_shapes=[pltpu.VMEM((tm, D_MODEL), BF16)],
        compiler_params=_cparams(("parallel", "arbitrary")),
        name="inproj",
    )(x2, norm_g, w_perm)


def _causal_conv_silu(u, prev_ref, w_ref, b_ref):
    n, w = u.shape
    u3 = u.reshape(n // SUBLANES, SUBLANES, w)
    sub = lax.broadcasted_iota(jnp.int32, u3.shape, 1)
    prev3 = prev_ref[...].reshape(1, SUBLANES, w)
    y = b_ref[...] + u * w_ref[CONV_WIDTH - 1:CONV_WIDTH, :]
    for d in range(1, CONV_WIDTH):
        r = pltpu.roll(u3, d, axis=1)
        above = jnp.concatenate([pltpu.roll(prev3, d, axis=1), r[:-1]], axis=0)
        y = y + jnp.where(sub < d, above, r).reshape(n, w) * w_ref[CONV_WIDTH - 1 - d:CONV_WIDTH - d, :]
    prev_ref[...] = u[n - SUBLANES:n, :]
    return _silu(y)


def _mlstm_kernel(q_ref, k_ref, v_ref, og_ref, z_ref, sm_ref, cqw_ref, cqb_ref, ckw_ref, ckb_ref,
                  sb_ref, ng_ref, ya_ref, qprev, kprev, c_state, n_state, m_state):
    L = q_ref.shape[0]
    d = M_HEAD_DIM

    @pl.when(pl.program_id(1) == 0)
    def _():
        qprev[...] = jnp.zeros_like(qprev)
        kprev[...] = jnp.zeros_like(kprev)
        c_state[...] = jnp.zeros_like(c_state)
        n_state[...] = jnp.zeros_like(n_state)
        m_state[...] = jnp.zeros_like(m_state)

    mq = _causal_conv_silu(q_ref[...], qprev, cqw_ref, cqb_ref)
    mk = _causal_conv_silu(k_ref[...], kprev, ckw_ref, ckb_ref) * (d ** -0.5)

    sm = sm_ref[...] + sb_ref[...]
    lf = _log_sigmoid(sm)
    row = lax.broadcasted_iota(jnp.int32, (L, L), 0)
    col = lax.broadcasted_iota(jnp.int32, (L, L), 1)
    causal = row >= col
    tril = causal.astype(F32)
    triu = (row <= col).astype(F32)
    a_col = jnp.dot(tril, lf, precision=lax.Precision.HIGHEST, preferred_element_type=F32)
    sm_t = sm.T[0:8, :]
    a_row = jnp.dot(_log_sigmoid(sm_t), triu, precision=lax.Precision.HIGHEST,
                    preferred_element_type=F32)

    v = v_ref[...]
    for h in range(M_HEADS):
        hs = slice(h * d, (h + 1) * d)
        a_c = a_col[:, 4 + h:5 + h]
        li_c = sm[:, h:h + 1]
        a_r = a_row[4 + h:5 + h, :]
        li_r = sm_t[h:h + 1, :]
        m_prev = m_state[h:h + 1, 0:1]
        a_end = a_c[L - 1:L, :]

        log_d = jnp.where(causal, a_c - a_r + li_r, -jnp.inf)
        inter = a_c + m_prev
        m_t = jnp.maximum(inter, jnp.max(log_d, axis=1, keepdims=True))
        w_inter = jnp.exp(inter - m_t)
        dm = jnp.exp(log_d - m_t)

        qh = mq[:, hs]
        kh = mk[:, hs]
        qb = qh.astype(BF16)
        kb = kh.astype(BF16)
        vb = v[:, hs].astype(BF16)
        s = lax.dot_general(qb, kb, (((1,), (1,)), ((), ())), preferred_element_type=F32) * dm
        c_old = c_state[h]
        n_old = n_state[h:h + 1, :]
        num = (w_inter * jnp.dot(qb, c_old.astype(BF16), preferred_element_type=F32)
               + jnp.dot(s.astype(BF16), vb, preferred_element_type=F32))
        den = (w_inter * jnp.sum(qh * n_old, axis=1, keepdims=True)
               + jnp.sum(s, axis=1, keepdims=True))
        hh = num * (1.0 / jnp.maximum(jnp.abs(den), jnp.exp(-m_t)))

        logw = a_end - a_c + li_c
        m_new = jnp.maximum(a_end + m_prev, jnp.max(logw, axis=0, keepdims=True))
        wk = jnp.exp(logw - m_new)
        decay = jnp.exp(a_end + m_prev - m_new)
        kw = kh * wk
        c_state[h] = decay * c_old + lax.dot_general(kw.astype(BF16), vb, (((0,), (0,)), ((), ())),
                                                     preferred_element_type=F32)
        n_state[h:h + 1, :] = decay * n_old + jnp.sum(kw, axis=0, keepdims=True)
        m_state[h:h + 1, :] = jnp.broadcast_to(m_new, (1, LANES))

        mu = jnp.mean(hh, axis=1, keepdims=True)
        dc = hh - mu
        var = jnp.mean(dc * dc, axis=1, keepdims=True)
        hn = dc * lax.rsqrt(var + NORM_EPS) * ng_ref[:, hs]
        ya = _sigmoid(og_ref[:, hs]) * hn * _silu(z_ref[:, hs])
        ya_ref[:, hs] = ya.astype(BF16)


def _mlstm(p, b, t, conv_q_w, conv_q_b, conv_k_w, conv_k_b, sm_bias, mh_norm_g):
    L = MLSTM_CHUNK
    nc = t // L

    def colspec(cstart):
        return pl.BlockSpec((L, M_WIDTH), lambda bi, c: (bi * nc + c, cstart // M_WIDTH))

    def full(shape):
        return pl.BlockSpec(shape, lambda bi, c: (0,) * len(shape))

    return pl.pallas_call(
        _mlstm_kernel,
        out_shape=jax.ShapeDtypeStruct((b * t, M_WIDTH), BF16),
        grid=(b, nc),
        in_specs=[
            colspec(C_MQ), colspec(C_MK), colspec(C_MV), colspec(C_MO), colspec(C_MZ),
            pl.BlockSpec((L, LANES), lambda bi, c: (bi * nc + c, C_SM // LANES)),
            full((CONV_WIDTH, M_WIDTH)), full((1, M_WIDTH)), full((CONV_WIDTH, M_WIDTH)), full((1, M_WIDTH)),
            full((1, LANES)), full((1, M_WIDTH)),
        ],
        out_specs=pl.BlockSpec((L, M_WIDTH), lambda bi, c: (bi * nc + c, 0)),
        scratch_shapes=[
            pltpu.VMEM((SUBLANES, M_WIDTH), F32), pltpu.VMEM((SUBLANES, M_WIDTH), F32),
            pltpu.VMEM((M_HEADS, M_HEAD_DIM, M_HEAD_DIM), F32),
            pltpu.VMEM((8, M_HEAD_DIM), F32), pltpu.VMEM((8, LANES), F32),
        ],
        compiler_params=_cparams(("parallel", "arbitrary")),
        name="mlstm",
    )(p, p, p, p, p, p, conv_q_w, conv_q_b, conv_k_w, conv_k_b, sm_bias, mh_norm_g)


def _compress_kernel(pk_ref, pv_ref, wt_ref, wb_ref, posk_ref, posv_ref, w1k_ref, w1v_ref, w2k_ref, w2v_ref,
                     kc_ref, vct_ref):
    nr = pk_ref.shape[0] // CMP_STRIDE
    cw = 2 * LANES
    u = jnp.zeros((nr, wt_ref.shape[1]), F32)
    v = jnp.zeros((nr, wb_ref.shape[1]), F32)
    for l in range(CMP_STRIDE):
        x = jnp.concatenate([pk_ref[pl.ds(l, nr, stride=CMP_STRIDE), :],
                             pv_ref[pl.ds(l, nr, stride=CMP_STRIDE), :]], axis=1).astype(BF16)
        u = u + jnp.dot(x, wt_ref[l * cw:(l + 1) * cw, :], preferred_element_type=F32)
        v = v + jnp.dot(x, wb_ref[l * cw:(l + 1) * cw, :], preferred_element_type=F32)
    v = pltpu.roll(v, nr - 1, axis=0)
    rowi = lax.broadcasted_iota(jnp.int32, v.shape, 0)
    v = jnp.where(rowi < nr - 1, v, 0.0)

    def pos_term(pos_ref, w_ref):
        pb = jnp.broadcast_to(pos_ref[...].astype(BF16), (8, CMP_BLOCK * N_HEAD_DIM))
        return jnp.dot(pb, w_ref[...], preferred_element_type=F32)[0:1, :]

    pk = pos_term(posk_ref, w1k_ref)
    pv = pos_term(posv_ref, w1v_ref)
    hid = u + v + jnp.concatenate([pk, pk, pv, pv], axis=1)
    act = 0.5 * hid * (1.0 + jnp.tanh(math.sqrt(2.0 / math.pi) * (hid + 0.044715 * (hid * hid * hid))))
    for c in range(2 * N_KV_GROUPS):
        w2 = w2k_ref[...] if c < N_KV_GROUPS else w2v_ref[...]
        out = jnp.dot(act[:, c * CMP_HIDDEN:(c + 1) * CMP_HIDDEN].astype(BF16), w2,
                      preferred_element_type=F32)
        if c < N_KV_GROUPS:
            kc_ref[c] = out.astype(BF16)
        else:
            vct_ref[c - N_KV_GROUPS] = out.T.astype(BF16)


def _compress(p, b, t, wt, wb, posk, posv, w1k, w1v, w2k, w2v):
    nr = t // CMP_STRIDE
    g = N_KV_GROUPS

    def full(a):
        return pl.BlockSpec(a.shape, lambda bi: (0,) * a.ndim)

    return pl.pallas_call(
        _compress_kernel,
        out_shape=(jax.ShapeDtypeStruct((b, g, nr, LANES), BF16), jax.ShapeDtypeStruct((b, g, LANES, nr), BF16)),
        grid=(b,),
        in_specs=[pl.BlockSpec((t, LANES), lambda bi: (bi, C_CMP // LANES)),
                  pl.BlockSpec((t, LANES), lambda bi: (bi, C_CMP // LANES + 1)),
                  full(wt), full(wb), full(posk), full(posv), full(w1k), full(w1v), full(w2k), full(w2v)],
        out_specs=(pl.BlockSpec((None, g, nr, LANES), lambda bi: (bi, 0, 0, 0)),
                   pl.BlockSpec((None, g, LANES, nr), lambda bi: (bi, 0, 0, 0))),
        compiler_params=_cparams(("parallel",)),
        name="compress",
    )(p, p, wt, wb, posk, posv, w1k, w1v, w2k, w2v)


def _kvprep_kernel(q_ref, kv_ref, sm_ref, pos_ref, inv_ref, gb_ref,
                   qrawt_ref, qrope_ref, kst_ref, vs_ref, kwt_ref, vw_ref, gates_ref):
    tk = q_ref.shape[0]
    lane = lax.broadcasted_iota(jnp.int32, (tk, LANES), 1)
    lih = lane % N_HEAD_DIM
    ang = pos_ref[...] * inv_ref[...]
    cos = jnp.cos(ang)
    sin = jnp.sin(ang)
    half = ROPE_DIM // 2
    s_up = jnp.where(lih < half, -sin, 0.0)
    s_dn = jnp.where((lih >= half) & (lih < ROPE_DIM), sin, 0.0)

    def rope(u):
        return u * cos + pltpu.roll(u, LANES - half, axis=1) * s_up + pltpu.roll(u, half, axis=1) * s_dn

    def head_pad(u, odd):
        if odd:
            u = pltpu.roll(u, N_HEAD_DIM, axis=1)
        return jnp.where(lane < N_HEAD_DIM, u, 0.0)

    scale = N_HEAD_DIM ** -0.5 * math.log2(math.e)
    zeros_t = jnp.zeros((N_HEAD_DIM, tk), BF16)
    for g in range(N_KV_GROUPS):
        for hp in range(N_HPG // 2):
            c0 = g * (N_HPG * N_HEAD_DIM) + hp * LANES
            u = q_ref[:, c0:c0 + LANES] * scale
            ur = rope(u)
            u_t = u.T.astype(BF16)
            for odd in range(2):
                h = 2 * hp + odd
                qrawt_ref[g, h * LANES:h * LANES + N_HEAD_DIM, :] = u_t[odd * N_HEAD_DIM:(odd + 1) * N_HEAD_DIM, :]
                qrawt_ref[g, h * LANES + N_HEAD_DIM:(h + 1) * LANES, :] = zeros_t
                qrope_ref[g, :, h * LANES:(h + 1) * LANES] = head_pad(ur, odd).astype(BF16)

    ks_t = rope(kv_ref[:, 0:LANES]).T
    kw_t = rope(kv_ref[:, 2 * LANES:3 * LANES]).T
    vs = kv_ref[:, LANES:2 * LANES]
    vw = kv_ref[:, 3 * LANES:4 * LANES]
    t_glob = pl.program_id(1) * tk + lax.broadcasted_iota(jnp.int32, (LANES, tk), 1)
    blk_row = lax.broadcasted_iota(jnp.int32, (LANES, tk), 0)
    onehot = jnp.where(blk_row == t_glob // SEL_BLOCK, 1.0, 0.0).astype(BF16)
    ones_lane = jnp.where(lane == N_HEAD_DIM, 1.0, 0.0)
    gates = _sigmoid(sm_ref[...] + gb_ref[...])
    for g in range(N_KV_GROUPS):
        rs = slice(g * N_HEAD_DIM, (g + 1) * N_HEAD_DIM)
        kst_ref[g, 0:LANES, :] = onehot
        kst_ref[g, LANES:LANES + N_HEAD_DIM, :] = ks_t[rs, :].astype(BF16)
        kst_ref[g, LANES + N_HEAD_DIM:2 * LANES, :] = zeros_t
        kwt_ref[g, 0:N_HEAD_DIM, :] = kw_t[rs, :].astype(BF16)
        kwt_ref[g, N_HEAD_DIM:LANES, :] = zeros_t
        vs_ref[g] = (head_pad(vs, g) + ones_lane).astype(BF16)
        vw_ref[g] = (head_pad(vw, g) + ones_lane).astype(BF16)
        goff = 8 + g * (3 * N_HPG)
        gates_ref[g] = pltpu.roll(gates, LANES - goff, axis=1)


def _kvprep(p, b, t, pos_col, inv_row, gate_bias):
    tk = 512
    nt = t // tk
    g = N_KV_GROUPS
    qw = N_HPG * LANES

    def tok(shape):
        return pl.BlockSpec((None, g) + shape, lambda bi, i: (bi, 0, i, 0))

    def tok_t(rows):
        return pl.BlockSpec((None, g, rows, tk), lambda bi, i: (bi, 0, 0, i))

    return pl.pallas_call(
        _kvprep_kernel,
        out_shape=(
            jax.ShapeDtypeStruct((b, g, qw, t), BF16), jax.ShapeDtypeStruct((b, g, t, qw), BF16),
            jax.ShapeDtypeStruct((b, g, 2 * LANES, t), BF16), jax.ShapeDtypeStruct((b, g, t, LANES), BF16),
            jax.ShapeDtypeStruct((b, g, LANES, t), BF16), jax.ShapeDtypeStruct((b, g, t, LANES), BF16),
            jax.ShapeDtypeStruct((b, g, t, LANES), F32),
        ),
        grid=(b, nt),
        in_specs=[
            pl.BlockSpec((tk, N_WIDTH), lambda bi, i: (bi * nt + i, C_NQ // N_WIDTH)),
            pl.BlockSpec((tk, 4 * LANES), lambda bi, i: (bi * nt + i, C_KV // (4 * LANES))),
            pl.BlockSpec((tk, LANES), lambda bi, i: (bi * nt + i, C_SM // LANES)),
            pl.BlockSpec((tk, 1), lambda bi, i: (bi * nt + i, 0)),
            pl.BlockSpec((1, LANES), lambda bi, i: (0, 0)),
            pl.BlockSpec((1, LANES), lambda bi, i: (0, 0)),
        ],
        out_specs=(tok_t(qw), tok((tk, qw)), tok_t(2 * LANES), tok((tk, LANES)),
                   tok_t(LANES), tok((tk, LANES)), tok((tk, LANES))),
        compiler_params=_cparams(("parallel", "arbitrary")),
        name="kvprep",
    )(p, p, p, pos_col, inv_row, gate_bias)


def _select_kernel(qt_ref, kc_ref, vct_ref, ovt_ref, oc_ref, bias_ref):
    qs = qt_ref.shape[1]
    nr = kc_ref.shape[0]
    t0 = pl.program_id(2) * qs
    m_c = (lax.broadcasted_iota(jnp.int32, (nr, qs), 0) * CMP_STRIDE + (CMP_BLOCK - 1)
           <= t0 + lax.broadcasted_iota(jnp.int32, (nr, qs), 1))
    kc = kc_ref[...]
    vct = vct_ref[...]
    sees_any = t0 + lax.broadcasted_iota(jnp.int32, (1, qs), 1) >= CMP_BLOCK - 1
    psum = jnp.zeros((nr, qs), F32)
    for h in range(N_HPG):
        s = jnp.dot(kc, qt_ref[h * LANES:(h + 1) * LANES, :], preferred_element_type=F32)
        s = jnp.where(m_c, s, NEG)
        e = jnp.exp2(s - jnp.max(s, axis=0, keepdims=True))
        den = jnp.sum(e, axis=0, keepdims=True)
        pc = e * jnp.where(sees_any, 1.0 / den, 0.0)
        psum = psum + pc
        oc_t = jnp.dot(vct, pc.astype(BF16), preferred_element_type=F32)
        oc_ref[:, h * LANES:(h + 1) * LANES] = oc_t.T
    p_hi = psum.astype(BF16)
    p_lo = (psum - p_hi.astype(F32)).astype(BF16)
    ovt = ovt_ref[...]
    imp = jnp.dot(ovt, p_hi, preferred_element_type=F32) + jnp.dot(ovt, p_lo, preferred_element_type=F32)

    blk = lax.broadcasted_iota(jnp.int32, (LANES, qs), 0)
    tq = t0 + lax.broadcasted_iota(jnp.int32, (LANES, qs), 1)
    cur = tq // SEL_BLOCK
    forced = (blk == 0) | (blk == cur) | (blk == cur - 1)
    n_sel = (nr * CMP_STRIDE) // SEL_BLOCK
    score = jnp.where(forced, FORCE, jnp.where(blk * SEL_BLOCK <= tq, imp, -1.0))
    score = jnp.where(blk < n_sel, score, -jnp.inf)
    blk_f = blk.astype(F32)
    for _ in range(min(SEL_TOPK, n_sel)):
        mx = jnp.max(score, axis=0, keepdims=True)
        idx = jnp.min(jnp.where(score == mx, blk_f, float(LANES)), axis=0, keepdims=True)
        score = jnp.where(blk_f == idx, -jnp.inf, score)
    cutoff = jnp.minimum((tq // Q_BLOCK) * (Q_BLOCK // SEL_BLOCK), n_sel)
    bias_t = jnp.where((score == -jnp.inf) & (blk < cutoff), 0.0, NEG)
    bias_ref[...] = bias_t.T.astype(BF16)


def _select(qrawt, kc, vct, ovt):
    b, g, qw, t = qrawt.shape
    nr = kc.shape[2]
    qs = 512

    def whole(rows, cols):
        return pl.BlockSpec((None, None, rows, cols), lambda bi, gi, i: (bi, gi, 0, 0))

    return pl.pallas_call(
        _select_kernel,
        out_shape=(jax.ShapeDtypeStruct((b, g, t, qw), F32), jax.ShapeDtypeStruct((b, g, t, LANES), BF16)),
        grid=(b, g, t // qs),
        in_specs=[
            pl.BlockSpec((None, None, qw, qs), lambda bi, gi, i: (bi, gi, 0, i)),
            whole(nr, LANES), whole(LANES, nr),
            pl.BlockSpec(ovt.shape, lambda bi, gi, i: (0, 0)),
        ],
        out_specs=(pl.BlockSpec((None, None, qs, qw), lambda bi, gi, i: (bi, gi, i, 0)),
                   pl.BlockSpec((None, None, qs, LANES), lambda bi, gi, i: (bi, gi, i, 0))),
        compiler_params=_cparams(("parallel", "parallel", "parallel")),
        name="select",
    )(qrawt, kc, vct, ovt)


def _nsa_kernel(qrope_ref, bias_ref, oc_ref, kst_ref, vs_ref, kwt_ref, vw_ref, gates_ref, nz_ref,
                yb_ref, qaug_ref, acc_ref, m_ref, s_ref):
    i = pl.program_id(1)
    t0 = i * Q_BLOCK
    Q = Q_BLOCK
    nh = N_HPG
    ng = N_KV_GROUPS
    rows = nh * Q
    t0a = pl.multiple_of(t0, Q)
    rq = lax.broadcasted_iota(jnp.int32, (rows, Q), 0) % Q
    ck = lax.broadcasted_iota(jnp.int32, (rows, Q), 1)
    lane = lax.broadcasted_iota(jnp.int32, (Q, LANES), 1)

    def q_rope(g):
        return jnp.concatenate([qrope_ref[g, :, h * LANES:(h + 1) * LANES] for h in range(nh)], axis=0)

    for g in range(ng):
        qr = q_rope(g)
        qaug_ref[g, :, 0:LANES] = jnp.concatenate([bias_ref[g]] * nh, axis=0)
        qaug_ref[g, :, LANES:2 * LANES] = qr
        s_d = jnp.dot(qr, kst_ref[g, LANES:2 * LANES, pl.ds(t0a, Q)], preferred_element_type=F32)
        s_d = jnp.where(ck <= rq, s_d, NEG)
        m0 = jnp.max(s_d, axis=1, keepdims=True)
        p_d = jnp.exp2(s_d - m0)
        m_ref[g] = jnp.broadcast_to(m0, (rows, LANES))
        acc_ref[g] = jnp.dot(p_d.astype(BF16), vs_ref[g, pl.ds(t0a, Q), :], preferred_element_type=F32)

    ncb = SEL_CHUNK // LANES
    n_chunks = (t0 + SEL_CHUNK - 1) // SEL_CHUNK
    last = jnp.maximum(n_chunks - 1, 0)

    def scores(g, c, slot):
        k0 = pl.multiple_of(c * SEL_CHUNK, SEL_CHUNK)
        s_ref[slot, g] = jnp.dot(qaug_ref[g], kst_ref[g, :, pl.ds(k0, SEL_CHUNK)], preferred_element_type=F32)

    def consume(g, c, slot):
        k0 = pl.multiple_of(c * SEL_CHUNK, SEL_CHUNK)
        s = s_ref[slot, g]
        m_old = m_ref[g]
        m_new = jnp.maximum(m_old, jnp.max(s, axis=1, keepdims=True))
        p = jnp.concatenate([jnp.exp2(s[:, cb * LANES:(cb + 1) * LANES] - m_new) for cb in range(ncb)], axis=1)
        acc_ref[g] = (jnp.exp2(m_old - m_new) * acc_ref[g]
                      + jnp.dot(p.astype(BF16), vs_ref[g, pl.ds(k0, SEL_CHUNK), :], preferred_element_type=F32))
        m_ref[g] = m_new

    @pl.when(n_chunks > 0)
    def _():
        for g in range(ng):
            scores(g, 0, 0)

    def sel_pair(c2, carry):
        c = 2 * c2
        for g in range(ng):
            scores(g, c + 1, 1)
            consume(g, c, 0)
        for g in range(ng):
            scores(g, jnp.minimum(c + 2, last), 0)
            consume(g, c + 1, 1)
        return carry

    lax.fori_loop(0, n_chunks // 2, sel_pair, 0)

    @pl.when(n_chunks % 2 == 1)
    def _():
        for g in range(ng):
            consume(g, n_chunks - 1, 0)

    wlen = WINDOW + Q
    ws = pl.multiple_of(jnp.maximum(t0 - WINDOW, 0), Q)
    tq_w = t0 + lax.broadcasted_iota(jnp.int32, (rows, wlen), 0) % Q
    diff = tq_w - (ws + lax.broadcasted_iota(jnp.int32, (rows, wlen), 1))
    m_w = (diff >= 0) & (diff < WINDOW)
    for g in range(ng):
        acc = acc_ref[g]
        o_s = acc * (1.0 / acc[:, N_HEAD_DIM:N_HEAD_DIM + 1])
        s_w = jnp.dot(q_rope(g), kwt_ref[g, :, pl.ds(ws, wlen)], preferred_element_type=F32)
        s_w = jnp.where(m_w, s_w, NEG)
        p_w = jnp.exp2(s_w - jnp.max(s_w, axis=1, keepdims=True))
        acc_w = jnp.dot(p_w.astype(BF16), vw_ref[g, pl.ds(ws, wlen), :], preferred_element_type=F32)
        o_w = acc_w * (1.0 / acc_w[:, N_HEAD_DIM:N_HEAD_DIM + 1])
        gates = gates_ref[g]
        outs = []
        for h in range(nh):
            rs = slice(h * Q, (h + 1) * Q)
            outs.append(gates[:, 3 * h:3 * h + 1] * oc_ref[g, :, h * LANES:(h + 1) * LANES]
                        + gates[:, 3 * h + 1:3 * h + 2] * o_s[rs, :]
                        + gates[:, 3 * h + 2:3 * h + 3] * o_w[rs, :])
        for hp in range(nh // 2):
            yn = jnp.where(lane < N_HEAD_DIM, outs[2 * hp], pltpu.roll(outs[2 * hp + 1], N_HEAD_DIM, axis=1))
            ls = slice((g * (nh // 2) + hp) * LANES, (g * (nh // 2) + hp + 1) * LANES)
            yb_ref[:, ls] = (yn * _silu(nz_ref[:, ls])).astype(BF16)


def _nsa(p, b, t, qrope, bias, oc, kst, vs, kwt, vw, gates):
    g = N_KV_GROUPS
    nqb = t // Q_BLOCK
    qw = N_HPG * LANES
    rows = N_HPG * Q_BLOCK

    def qblk(w):
        return pl.BlockSpec((None, g, Q_BLOCK, w), lambda bi, i: (bi, 0, i, 0))

    def whole(r, c):
        return pl.BlockSpec((None, g, r, c), lambda bi, i: (bi, 0, 0, 0), pipeline_mode=pl.Buffered(1))

    return pl.pallas_call(
        _nsa_kernel,
        out_shape=jax.ShapeDtypeStruct((b * t, N_WIDTH), BF16),
        grid=(b, nqb),
        in_specs=[
            qblk(qw), qblk(LANES), qblk(qw),
            whole(2 * LANES, t), whole(t, LANES), whole(LANES, t), whole(t, LANES),
            qblk(LANES),
            pl.BlockSpec((Q_BLOCK, N_WIDTH), lambda bi, i: (bi * nqb + i, C_NZ // N_WIDTH)),
        ],
        out_specs=pl.BlockSpec((Q_BLOCK, N_WIDTH), lambda bi, i: (bi * nqb + i, 0)),
        scratch_shapes=[pltpu.VMEM((g, rows, 2 * LANES), BF16), pltpu.VMEM((g, rows, LANES), F32),
                        pltpu.VMEM((g, rows, LANES), F32), pltpu.VMEM((2, g, rows, SEL_CHUNK), F32)],
        compiler_params=_cparams(("parallel", "arbitrary")),
        name="nsa",
    )(qrope, bias, oc, kst, vs, kwt, vw, gates, p)


def _outproj_kernel(final, ya_ref, yb_ref, ga_ref, gb_ref, x_ref, wa_ref, wb_ref, wo_ref, fg_ref, o_ref):
    a = jnp.dot(ya_ref[...], wa_ref[...], preferred_element_type=F32)
    bq = jnp.dot(yb_ref[...], wb_ref[...], preferred_element_type=F32)
    merged = _sigmoid(ga_ref[...]) * a + _sigmoid(gb_ref[...]) * bq
    y = x_ref[...] + jnp.dot(merged.astype(BF16), wo_ref[...], preferred_element_type=F32)
    if final:
        y = y * lax.rsqrt(jnp.mean(y * y, axis=-1, keepdims=True) + NORM_EPS) * fg_ref[...]
    o_ref[...] = y


def _outproj(ya, yb, p, x2, wa, wb, wo, fg, final):
    m = x2.shape[0]
    tm = 512

    def full(a):
        return pl.BlockSpec(a.shape, lambda i: (0, 0))

    return pl.pallas_call(
        functools.partial(_outproj_kernel, final),
        out_shape=jax.ShapeDtypeStruct((m, D_MODEL), F32),
        grid=(m // tm,),
        in_specs=[
            pl.BlockSpec((tm, M_WIDTH), lambda i: (i, 0)),
            pl.BlockSpec((tm, N_WIDTH), lambda i: (i, 0)),
            pl.BlockSpec((tm, D_MODEL), lambda i: (i, C_GA // D_MODEL)),
            pl.BlockSpec((tm, D_MODEL), lambda i: (i, C_GB // D_MODEL)),
            pl.BlockSpec((tm, D_MODEL), lambda i: (i, 0)),
            full(wa), full(wb), full(wo), full(fg),
        ],
        out_specs=pl.BlockSpec((tm, D_MODEL), lambda i: (i, 0)),
        compiler_params=_cparams(("parallel",)),
        name="outproj",
    )(ya, yb, p, p, x2, wa, wb, wo, fg)


def _permute_w_in(w):
    pad = jnp.zeros((w.shape[0], P_W - _O_END), w.dtype)
    return jnp.concatenate([
        w[:, 0:_O_MI],
        w[:, _O_GA:_O_END],
        w[:, _O_NQ:_O_KC],
        w[:, _O_NZ:_O_GA],
        w[:, _O_KS:_O_NG],
        w[:, _O_KC:_O_KS],
        w[:, _O_MI:_O_NQ],
        w[:, _O_NG:_O_NZ],
        pad], axis=1)


def _expand_cmp_w1(w1k, w1v):
    half = CMP_BLOCK // 2
    k3 = w1k.astype(BF16).reshape(CMP_BLOCK, 1, N_HEAD_DIM, 1, CMP_HIDDEN)
    v3 = w1v.astype(BF16).reshape(CMP_BLOCK, 1, N_HEAD_DIM, 1, CMP_HIDDEN)
    nc = 2 * N_KV_GROUPS
    eye = jnp.eye(nc, dtype=BF16)
    is_k = (jnp.arange(nc) < N_KV_GROUPS).astype(BF16)
    mk = (eye * is_k[:, None]).reshape(1, nc, 1, nc, 1)
    mv = (eye * (1 - is_k)[:, None]).reshape(1, nc, 1, nc, 1)
    z = k3 * mk + v3 * mv
    z = z.reshape(CMP_BLOCK * nc * N_HEAD_DIM, nc * CMP_HIDDEN)
    return z[:half * nc * N_HEAD_DIM], z[half * nc * N_HEAD_DIM:]


def kernel(x, positions, norm_g, w_in, conv_q_w, conv_q_b, conv_k_w, conv_k_b, b_igate, b_fgate, mh_norm_g,
           cmp_pos_k, cmp_w1_k, cmp_w2_k, cmp_pos_v, cmp_w1_v, cmp_w2_v, b_nsa_gate, w_branch_a, w_branch_b,
           w_out, final_norm_g):
    b, t, _ = x.shape
    assert t % (2 * SEL_CHUNK) == 0 and t >= WINDOW + Q_BLOCK and t // SEL_BLOCK <= LANES and t % MLSTM_CHUNK == 0
    x2 = x.reshape(b * t, D_MODEL)
    for l in range(norm_g.shape[0]):
        p = _inproj(x2, norm_g[l][None, :], _permute_w_in(w_in[l].astype(BF16)))

        sm_bias = jnp.zeros((1, LANES), F32).at[0, 0:M_HEADS].set(b_igate[l]).at[0, M_HEADS:2 * M_HEADS].set(b_fgate[l])
        ya = _mlstm(p, b, t, conv_q_w[l], conv_q_b[l][None, :], conv_k_w[l], conv_k_b[l][None, :], sm_bias,
                    mh_norm_g[l][None, :])

        nr = t // CMP_STRIDE
        wt, wb = _expand_cmp_w1(cmp_w1_k[l], cmp_w1_v[l])
        w2pad = lambda w: jnp.pad(w, ((0, 0), (0, LANES - N_HEAD_DIM))).astype(BF16)
        kc, vct = _compress(p, b, t, wt, wb, cmp_pos_k[l].reshape(1, -1), cmp_pos_v[l].reshape(1, -1),
                            cmp_w1_k[l].astype(BF16), cmp_w1_v[l].astype(BF16), w2pad(cmp_w2_k[l]), w2pad(cmp_w2_v[l]))

        half = ROPE_DIM // 2
        inv = jnp.power(jnp.float32(ROPE_THETA), -jnp.arange(half, dtype=F32) * (2.0 / ROPE_DIM))
        lih = jnp.arange(LANES) % N_HEAD_DIM
        inv_row = jnp.where(lih < ROPE_DIM, inv[lih % half], 0.0)[None, :].astype(F32)
        pos_col = positions.astype(F32).reshape(b * t, 1)
        gate_bias = jnp.zeros((1, LANES), F32).at[0, 8:8 + 3 * N_HEADS].set(b_nsa_gate[l])
        qrawt, qrope, kst, vs, kwt, vw, gates = _kvprep(p, b, t, pos_col, inv_row, gate_bias)

        ci = jnp.arange(nr)[None, :] * CMP_STRIDE
        sj = jnp.arange(LANES)[:, None] * SEL_BLOCK
        overlap_t = ((ci < sj + SEL_BLOCK) & (ci + CMP_BLOCK > sj) & (jnp.arange(nr)[None, :] < nr - 1)
                     & (jnp.arange(LANES)[:, None] < t // SEL_BLOCK)).astype(BF16)
        oc, bias = _select(qrawt, kc, vct, overlap_t)
        yb = _nsa(p, b, t, qrope, bias, oc, kst, vs, kwt, vw, gates)

        x2 = _outproj(ya, yb, p, x2, w_branch_a[l].astype(BF16), w_branch_b[l].astype(BF16),
                      w_out[l].astype(BF16), final_norm_g[None, :], l == norm_g.shape[0] - 1)
    return x2.reshape(b, t, D_MODEL)
```

```python
import functools
import math

import jax
import jax.numpy as jnp
from jax import lax
from jax.experimental import pallas as pl
from jax.experimental.pallas import tpu as pltpu

F32 = jnp.float32
BF16 = jnp.bfloat16

D_MODEL = 1024
M_HEADS = 4
M_HEAD_DIM = 256
M_WIDTH = M_HEADS * M_HEAD_DIM
CONV_WIDTH = 4
N_HEADS = 8
N_KV_GROUPS = 2
N_HPG = N_HEADS // N_KV_GROUPS
N_HEAD_DIM = 64
N_WIDTH = N_HEADS * N_HEAD_DIM
CMP_BLOCK = 32
CMP_STRIDE = 16
CMP_HIDDEN = 128
SEL_BLOCK = 64
SEL_TOPK = 16
WINDOW = 512
Q_BLOCK = 128
ROPE_THETA = 500000.0
ROPE_DIM = N_HEAD_DIM // 4
NORM_EPS = 1e-6
NEG = -1e30
FORCE = 1e9

LANES = 128
SUBLANES = 8
MLSTM_CHUNK = 256
SEL_CHUNK = 512
VMEM_LIMIT = 56 * 1024 * 1024

C_MQ, C_MK, C_MV, C_MO, C_MZ, C_GA, C_GB = 0, 1024, 2048, 3072, 4096, 5120, 6144
C_NQ, C_NZ = 7168, 7680
C_KV = 8192
C_CMP = 8704
C_SM = 8960
P_W = 9216
AUX_W = 3 * LANES
X_KC, X_VC, X_SM = 0, 1, 2
_O_MI, _O_NQ, _O_KC, _O_KS, _O_NG, _O_NZ, _O_GA, _O_END = 5120, 5128, 5640, 5896, 6408, 6432, 6944, 8992


def _cparams(sem):
    return pltpu.CompilerParams(dimension_semantics=sem, vmem_limit_bytes=VMEM_LIMIT)


def _sigmoid(x):
    return 0.5 * jnp.tanh(0.5 * x) + 0.5


def _silu(x):
    h = 0.5 * x
    return h + h * jnp.tanh(h)


def _log_sigmoid(x):
    return jnp.minimum(x, 0.0) - jnp.log1p(jnp.exp(-jnp.abs(x)))


PROJ_TN = 1024
AUX_TILE = C_CMP // PROJ_TN
AUX_OFF = C_CMP % PROJ_TN


def _inproj_kernel(x_ref, g_ref, w_ref, o_ref, aux_ref, hn_ref):
    j = pl.program_id(1)

    @pl.when(j == 0)
    def _():
        x = x_ref[...]
        r = lax.rsqrt(jnp.mean(x * x, axis=-1, keepdims=True) + NORM_EPS)
        hn_ref[...] = (x * r * g_ref[...]).astype(BF16)

    @pl.when(j != AUX_TILE)
    def _():
        o_ref[...] = jnp.dot(hn_ref[...], w_ref[...], preferred_element_type=F32).astype(BF16)

    @pl.when(j == AUX_TILE)
    def _():
        acc = jnp.dot(hn_ref[...], w_ref[...], preferred_element_type=F32)
        o_ref[...] = acc.astype(BF16)
        aux_ref[...] = acc[:, AUX_OFF:AUX_OFF + AUX_W]


def _inproj(x2, norm_g, w_perm):
    m = x2.shape[0]
    tm, tn = 1024, PROJ_TN
    return pl.pallas_call(
        _inproj_kernel,
        out_shape=(jax.ShapeDtypeStruct((m, P_W), BF16), jax.ShapeDtypeStruct((m, AUX_W), F32)),
        grid=(m // tm, P_W // tn),
        in_specs=[
            pl.BlockSpec((tm, D_MODEL), lambda i, j: (i, 0)),
            pl.BlockSpec((1, D_MODEL), lambda i, j: (0, 0)),
            pl.BlockSpec((D_MODEL, tn), lambda i, j: (0, j)),
        ],
        out_specs=(pl.BlockSpec((tm, tn), lambda i, j: (i, j)), pl.BlockSpec((tm, AUX_W), lambda i, j: (i, 0))),
        scratch_shapes=[pltpu.VMEM((tm, D_MODEL), BF16)],
        compiler_params=_cparams(("parallel", "arbitrary")),
        name="inproj",
    )(x2, norm_g, w_perm)


def _causal_conv_silu(u, prev_ref, w_ref, b_ref):
    n, w = u.shape
    u3 = u.reshape(n // SUBLANES, SUBLANES, w)
    sub = lax.broadcasted_iota(jnp.int32, u3.shape, 1)
    prev3 = prev_ref[...].reshape(1, SUBLANES, w)
    y = b_ref[...] + u * w_ref[CONV_WIDTH - 1:CONV_WIDTH, :]
    for d in range(1, CONV_WIDTH):
        r = pltpu.roll(u3, d, axis=1)
        above = jnp.concatenate([pltpu.roll(prev3, d, axis=1), r[:-1]], axis=0)
        y = y + jnp.where(sub < d, above, r).reshape(n, w) * w_ref[CONV_WIDTH - 1 - d:CONV_WIDTH - d, :]
    prev_ref[...] = u[n - SUBLANES:n, :]
    return _silu(y)


def _mlstm_kernel(q_ref, k_ref, v_ref, og_ref, z_ref, sm_ref, cqw_ref, cqb_ref, ckw_ref, ckb_ref,
                  sb_ref, ng_ref, ya_ref, qprev, kprev, c_state, n_state, m_state):
    L = q_ref.shape[0]
    d = M_HEAD_DIM

    @pl.when(pl.program_id(1) == 0)
    def _():
        qprev[...] = jnp.zeros_like(qprev)
        kprev[...] = jnp.zeros_like(kprev)
        c_state[...] = jnp.zeros_like(c_state)
        n_state[...] = jnp.zeros_like(n_state)
        m_state[...] = jnp.zeros_like(m_state)

    mq = _causal_conv_silu(q_ref[...].astype(F32), qprev, cqw_ref, cqb_ref)
    mk = _causal_conv_silu(k_ref[...].astype(F32), kprev, ckw_ref, ckb_ref) * (d ** -0.5)

    sm = sm_ref[...] + sb_ref[...]
    lf = _log_sigmoid(sm)
    row = lax.broadcasted_iota(jnp.int32, (L, L), 0)
    col = lax.broadcasted_iota(jnp.int32, (L, L), 1)
    causal = row >= col
    tril = causal.astype(F32)
    triu = (row <= col).astype(F32)
    a_col = jnp.dot(tril, lf, precision=lax.Precision.HIGHEST, preferred_element_type=F32)
    sm_t = sm.T[0:8, :]
    a_row = jnp.dot(_log_sigmoid(sm_t), triu, precision=lax.Precision.HIGHEST,
                    preferred_element_type=F32)

    for h in range(M_HEADS):
        hs = slice(h * d, (h + 1) * d)
        a_c = a_col[:, 4 + h:5 + h]
        li_c = sm[:, h:h + 1]
        a_r = a_row[4 + h:5 + h, :]
        li_r = sm_t[h:h + 1, :]
        m_prev = m_state[h:h + 1, 0:1]
        a_end = a_c[L - 1:L, :]

        log_d = jnp.where(causal, a_c - a_r + li_r, -jnp.inf)
        inter = a_c + m_prev
        m_t = jnp.maximum(inter, jnp.max(log_d, axis=1, keepdims=True))
        w_inter = jnp.exp(inter - m_t)
        dm = jnp.exp(log_d - m_t)

        qh = mq[:, hs]
        kh = mk[:, hs]
        qb = qh.astype(BF16)
        kb = kh.astype(BF16)
        vb = v_ref[:, hs]
        s = lax.dot_general(qb, kb, (((1,), (1,)), ((), ())), preferred_element_type=F32) * dm
        c_old = c_state[h]
        n_old = n_state[h:h + 1, :]
        num = (w_inter * jnp.dot(qb, c_old.astype(BF16), preferred_element_type=F32)
               + jnp.dot(s.astype(BF16), vb, preferred_element_type=F32))
        den = (w_inter * jnp.sum(qh * n_old, axis=1, keepdims=True)
               + jnp.sum(s, axis=1, keepdims=True))
        hh = num * (1.0 / jnp.maximum(jnp.abs(den), jnp.exp(-m_t)))

        logw = a_end - a_c + li_c
        m_new = jnp.maximum(a_end + m_prev, jnp.max(logw, axis=0, keepdims=True))
        wk = jnp.exp(logw - m_new)
        decay = jnp.exp(a_end + m_prev - m_new)
        kw = kh * wk
        c_state[h] = decay * c_old + lax.dot_general(kw.astype(BF16), vb, (((0,), (0,)), ((), ())),
                                                     preferred_element_type=F32)
        n_state[h:h + 1, :] = decay * n_old + jnp.sum(kw, axis=0, keepdims=True)
        m_state[h:h + 1, :] = jnp.broadcast_to(m_new, (1, LANES))

        mu = jnp.mean(hh, axis=1, keepdims=True)
        dc = hh - mu
        var = jnp.mean(dc * dc, axis=1, keepdims=True)
        hn = dc * lax.rsqrt(var + NORM_EPS) * ng_ref[:, hs]
        ya = _sigmoid(og_ref[:, hs].astype(F32)) * hn * _silu(z_ref[:, hs].astype(F32))
        ya_ref[:, hs] = ya.astype(BF16)


def _mlstm(p, aux, b, t, conv_q_w, conv_q_b, conv_k_w, conv_k_b, sm_bias, mh_norm_g):
    L = MLSTM_CHUNK
    nc = t // L

    def colspec(cstart):
        return pl.BlockSpec((L, M_WIDTH), lambda bi, c: (bi * nc + c, cstart // M_WIDTH))

    def full(shape):
        return pl.BlockSpec(shape, lambda bi, c: (0,) * len(shape))

    return pl.pallas_call(
        _mlstm_kernel,
        out_shape=jax.ShapeDtypeStruct((b * t, M_WIDTH), BF16),
        grid=(b, nc),
        in_specs=[
            colspec(C_MQ), colspec(C_MK), colspec(C_MV), colspec(C_MO), colspec(C_MZ),
            pl.BlockSpec((L, LANES), lambda bi, c: (bi * nc + c, X_SM)),
            full((CONV_WIDTH, M_WIDTH)), full((1, M_WIDTH)), full((CONV_WIDTH, M_WIDTH)), full((1, M_WIDTH)),
            full((1, LANES)), full((1, M_WIDTH)),
        ],
        out_specs=pl.BlockSpec((L, M_WIDTH), lambda bi, c: (bi * nc + c, 0)),
        scratch_shapes=[
            pltpu.VMEM((SUBLANES, M_WIDTH), F32), pltpu.VMEM((SUBLANES, M_WIDTH), F32),
            pltpu.VMEM((M_HEADS, M_HEAD_DIM, M_HEAD_DIM), F32),
            pltpu.VMEM((8, M_HEAD_DIM), F32), pltpu.VMEM((8, LANES), F32),
        ],
        compiler_params=_cparams(("parallel", "arbitrary")),
        name="mlstm",
    )(p, p, p, p, p, aux, conv_q_w, conv_q_b, conv_k_w, conv_k_b, sm_bias, mh_norm_g)


def _compress_kernel(pk_ref, pv_ref, wt_ref, wb_ref, posk_ref, posv_ref, w1k_ref, w1v_ref, w2k_ref, w2v_ref,
                     kc_ref, vct_ref):
    nr = pk_ref.shape[0] // CMP_STRIDE
    cw = 2 * LANES
    u = jnp.zeros((nr, wt_ref.shape[1]), F32)
    v = jnp.zeros((nr, wb_ref.shape[1]), F32)
    for l in range(CMP_STRIDE):
        x = jnp.concatenate([pk_ref[pl.ds(l, nr, stride=CMP_STRIDE), :],
                             pv_ref[pl.ds(l, nr, stride=CMP_STRIDE), :]], axis=1).astype(BF16)
        u = u + jnp.dot(x, wt_ref[l * cw:(l + 1) * cw, :], preferred_element_type=F32)
        v = v + jnp.dot(x, wb_ref[l * cw:(l + 1) * cw, :], preferred_element_type=F32)
    v = pltpu.roll(v, nr - 1, axis=0)
    rowi = lax.broadcasted_iota(jnp.int32, v.shape, 0)
    v = jnp.where(rowi < nr - 1, v, 0.0)

    def pos_term(pos_ref, w_ref):
        pb = jnp.broadcast_to(pos_ref[...].astype(BF16), (8, CMP_BLOCK * N_HEAD_DIM))
        return jnp.dot(pb, w_ref[...], preferred_element_type=F32)[0:1, :]

    pk = pos_term(posk_ref, w1k_ref)
    pv = pos_term(posv_ref, w1v_ref)
    hid = u + v + jnp.concatenate([pk, pk, pv, pv], axis=1)
    act = 0.5 * hid * (1.0 + jnp.tanh(math.sqrt(2.0 / math.pi) * (hid + 0.044715 * (hid * hid * hid))))
    for c in range(2 * N_KV_GROUPS):
        w2 = w2k_ref[...] if c < N_KV_GROUPS else w2v_ref[...]
        out = jnp.dot(act[:, c * CMP_HIDDEN:(c + 1) * CMP_HIDDEN].astype(BF16), w2,
                      preferred_element_type=F32)
        if c < N_KV_GROUPS:
            kc_ref[c] = out.astype(BF16)
        else:
            vct_ref[c - N_KV_GROUPS] = out.T.astype(BF16)


def _compress(aux, b, t, wt, wb, posk, posv, w1k, w1v, w2k, w2v):
    nr = t // CMP_STRIDE
    g = N_KV_GROUPS

    def full(a):
        return pl.BlockSpec(a.shape, lambda bi: (0,) * a.ndim)

    return pl.pallas_call(
        _compress_kernel,
        out_shape=(jax.ShapeDtypeStruct((b, g, nr, LANES), BF16), jax.ShapeDtypeStruct((b, g, LANES, nr), BF16)),
        grid=(b,),
        in_specs=[pl.BlockSpec((t, LANES), lambda bi: (bi, X_KC)),
                  pl.BlockSpec((t, LANES), lambda bi: (bi, X_VC)),
                  full(wt), full(wb), full(posk), full(posv), full(w1k), full(w1v), full(w2k), full(w2v)],
        out_specs=(pl.BlockSpec((None, g, nr, LANES), lambda bi: (bi, 0, 0, 0)),
                   pl.BlockSpec((None, g, LANES, nr), lambda bi: (bi, 0, 0, 0))),
        compiler_params=_cparams(("parallel",)),
        name="compress",
    )(aux, aux, wt, wb, posk, posv, w1k, w1v, w2k, w2v)


def _kvprep_kernel(q_ref, kv_ref, sm_ref, pos_ref, inv_ref, gb_ref,
                   qrawt_ref, qrope_ref, kst_ref, vs_ref, kwt_ref, vw_ref, gates_ref):
    tk = q_ref.shape[0]
    lane = lax.broadcasted_iota(jnp.int32, (tk, LANES), 1)
    lih = lane % N_HEAD_DIM
    ang = pos_ref[...] * inv_ref[...]
    cos = jnp.cos(ang)
    sin = jnp.sin(ang)
    half = ROPE_DIM // 2
    s_up = jnp.where(lih < half, -sin, 0.0)
    s_dn = jnp.where((lih >= half) & (lih < ROPE_DIM), sin, 0.0)

    def rope(u):
        return u * cos + pltpu.roll(u, LANES - half, axis=1) * s_up + pltpu.roll(u, half, axis=1) * s_dn

    def head_pad(u, odd):
        if odd:
            u = pltpu.roll(u, N_HEAD_DIM, axis=1)
        return jnp.where(lane < N_HEAD_DIM, u, 0.0)

    scale = N_HEAD_DIM ** -0.5 * math.log2(math.e)
    zeros_t = jnp.zeros((N_HEAD_DIM, tk), BF16)
    for g in range(N_KV_GROUPS):
        for hp in range(N_HPG // 2):
            c0 = g * (N_HPG * N_HEAD_DIM) + hp * LANES
            u = q_ref[:, c0:c0 + LANES].astype(F32) * scale
            ur = rope(u)
            u_t = u.T.astype(BF16)
            for odd in range(2):
                h = 2 * hp + odd
                qrawt_ref[g, h * LANES:h * LANES + N_HEAD_DIM, :] = u_t[odd * N_HEAD_DIM:(odd + 1) * N_HEAD_DIM, :]
                qrawt_ref[g, h * LANES + N_HEAD_DIM:(h + 1) * LANES, :] = zeros_t
                qrope_ref[g, :, h * LANES:(h + 1) * LANES] = head_pad(ur, odd).astype(BF16)

    ks_t = rope(kv_ref[:, 0:LANES].astype(F32)).T
    kw_t = rope(kv_ref[:, 2 * LANES:3 * LANES].astype(F32)).T
    vs = kv_ref[:, LANES:2 * LANES].astype(F32)
    vw = kv_ref[:, 3 * LANES:4 * LANES].astype(F32)
    t_glob = pl.program_id(1) * tk + lax.broadcasted_iota(jnp.int32, (LANES, tk), 1)
    blk_row = lax.broadcasted_iota(jnp.int32, (LANES, tk), 0)
    onehot = jnp.where(blk_row == t_glob // SEL_BLOCK, 1.0, 0.0).astype(BF16)
    ones_lane = jnp.where(lane == N_HEAD_DIM, 1.0, 0.0)
    gates = _sigmoid(sm_ref[...] + gb_ref[...])
    for g in range(N_KV_GROUPS):
        rs = slice(g * N_HEAD_DIM, (g + 1) * N_HEAD_DIM)
        kst_ref[g, 0:LANES, :] = onehot
        kst_ref[g, LANES:LANES + N_HEAD_DIM, :] = ks_t[rs, :].astype(BF16)
        kst_ref[g, LANES + N_HEAD_DIM:2 * LANES, :] = zeros_t
        kwt_ref[g, 0:N_HEAD_DIM, :] = kw_t[rs, :].astype(BF16)
        kwt_ref[g, N_HEAD_DIM:LANES, :] = zeros_t
        vs_ref[g] = (head_pad(vs, g) + ones_lane).astype(BF16)
        vw_ref[g] = (head_pad(vw, g) + ones_lane).astype(BF16)
        goff = 8 + g * (3 * N_HPG)
        gates_ref[g] = pltpu.roll(gates, LANES - goff, axis=1)


def _kvprep(p, aux, b, t, pos_col, inv_row, gate_bias):
    tk = 512
    nt = t // tk
    g = N_KV_GROUPS
    qw = N_HPG * LANES

    def tok(shape):
        return pl.BlockSpec((None, g) + shape, lambda bi, i: (bi, 0, i, 0))

    def tok_t(rows):
        return pl.BlockSpec((None, g, rows, tk), lambda bi, i: (bi, 0, 0, i))

    return pl.pallas_call(
        _kvprep_kernel,
        out_shape=(
            jax.ShapeDtypeStruct((b, g, qw, t), BF16), jax.ShapeDtypeStruct((b, g, t, qw), BF16),
            jax.ShapeDtypeStruct((b, g, 2 * LANES, t), BF16), jax.ShapeDtypeStruct((b, g, t, LANES), BF16),
            jax.ShapeDtypeStruct((b, g, LANES, t), BF16), jax.ShapeDtypeStruct((b, g, t, LANES), BF16),
            jax.ShapeDtypeStruct((b, g, t, LANES), F32),
        ),
        grid=(b, nt),
        in_specs=[
            pl.BlockSpec((tk, N_WIDTH), lambda bi, i: (bi * nt + i, C_NQ // N_WIDTH)),
            pl.BlockSpec((tk, 4 * LANES), lambda bi, i: (bi * nt + i, C_KV // (4 * LANES))),
            pl.BlockSpec((tk, LANES), lambda bi, i: (bi * nt + i, X_SM)),
            pl.BlockSpec((tk, 1), lambda bi, i: (bi * nt + i, 0)),
            pl.BlockSpec((1, LANES), lambda bi, i: (0, 0)),
            pl.BlockSpec((1, LANES), lambda bi, i: (0, 0)),
        ],
        out_specs=(tok_t(qw), tok((tk, qw)), tok_t(2 * LANES), tok((tk, LANES)),
                   tok_t(LANES), tok((tk, LANES)), tok((tk, LANES))),
        compiler_params=_cparams(("parallel", "arbitrary")),
        name="kvprep",
    )(p, p, aux, pos_col, inv_row, gate_bias)


def _select_kernel(qt_ref, kc_ref, vct_ref, ovt_ref, oc_ref, bias_ref):
    qs = qt_ref.shape[1]
    nr = kc_ref.shape[0]
    t0 = pl.program_id(2) * qs
    m_c = (lax.broadcasted_iota(jnp.int32, (nr, qs), 0) * CMP_STRIDE + (CMP_BLOCK - 1)
           <= t0 + lax.broadcasted_iota(jnp.int32, (nr, qs), 1))
    kc = kc_ref[...]
    vct = vct_ref[...]
    sees_any = t0 + lax.broadcasted_iota(jnp.int32, (1, qs), 1) >= CMP_BLOCK - 1
    psum = jnp.zeros((nr, qs), F32)
    for h in range(N_HPG):
        s = jnp.dot(kc, qt_ref[h * LANES:(h + 1) * LANES, :], preferred_element_type=F32)
        s = jnp.where(m_c, s, NEG)
        e = jnp.exp2(s - jnp.max(s, axis=0, keepdims=True))
        den = jnp.sum(e, axis=0, keepdims=True)
        pc = e * jnp.where(sees_any, 1.0 / den, 0.0)
        psum = psum + pc
        oc_t = jnp.dot(vct, pc.astype(BF16), preferred_element_type=F32)
        oc_ref[:, h * LANES:(h + 1) * LANES] = oc_t.T
    p_hi = psum.astype(BF16)
    p_lo = (psum - p_hi.astype(F32)).astype(BF16)
    ovt = ovt_ref[...]
    imp = jnp.dot(ovt, p_hi, preferred_element_type=F32) + jnp.dot(ovt, p_lo, preferred_element_type=F32)

    blk = lax.broadcasted_iota(jnp.int32, (LANES, qs), 0)
    tq = t0 + lax.broadcasted_iota(jnp.int32, (LANES, qs), 1)
    cur = tq // SEL_BLOCK
    forced = (blk == 0) | (blk == cur) | (blk == cur - 1)
    n_sel = (nr * CMP_STRIDE) // SEL_BLOCK
    score = jnp.where(forced, FORCE, jnp.where(blk * SEL_BLOCK <= tq, imp, -1.0))
    score = jnp.where(blk < n_sel, score, -jnp.inf)
    blk_f = blk.astype(F32)
    for _ in range(min(SEL_TOPK, n_sel)):
        mx = jnp.max(score, axis=0, keepdims=True)
        idx = jnp.min(jnp.where(score == mx, blk_f, float(LANES)), axis=0, keepdims=True)
        score = jnp.where(blk_f == idx, -jnp.inf, score)
    cutoff = jnp.minimum((tq // Q_BLOCK) * (Q_BLOCK // SEL_BLOCK), n_sel)
    bias_t = jnp.where((score == -jnp.inf) & (blk < cutoff), 0.0, NEG)
    bias_ref[...] = bias_t.T.astype(BF16)


def _select(qrawt, kc, vct, ovt):
    b, g, qw, t = qrawt.shape
    nr = kc.shape[2]
    qs = 512

    def whole(rows, cols):
        return pl.BlockSpec((None, None, rows, cols), lambda bi, gi, i: (bi, gi, 0, 0))

    return pl.pallas_call(
        _select_kernel,
        out_shape=(jax.ShapeDtypeStruct((b, g, t, qw), F32), jax.ShapeDtypeStruct((b, g, t, LANES), BF16)),
        grid=(b, g, t // qs),
        in_specs=[
            pl.BlockSpec((None, None, qw, qs), lambda bi, gi, i: (bi, gi, 0, i)),
            whole(nr, LANES), whole(LANES, nr),
            pl.BlockSpec(ovt.shape, lambda bi, gi, i: (0, 0)),
        ],
        out_specs=(pl.BlockSpec((None, None, qs, qw), lambda bi, gi, i: (bi, gi, i, 0)),
                   pl.BlockSpec((None, None, qs, LANES), lambda bi, gi, i: (bi, gi, i, 0))),
        compiler_params=_cparams(("parallel", "parallel", "parallel")),
        name="select",
    )(qrawt, kc, vct, ovt)


def _nsa_kernel(qrope_ref, bias_ref, oc_ref, kst_ref, vs_ref, kwt_ref, vw_ref, gates_ref, nz_ref,
                yb_ref, qaug_ref, acc_ref, m_ref, s_ref):
    i = pl.program_id(1)
    t0 = i * Q_BLOCK
    Q = Q_BLOCK
    nh = N_HPG
    ng = N_KV_GROUPS
    rows = nh * Q
    t0a = pl.multiple_of(t0, Q)
    rq = lax.broadcasted_iota(jnp.int32, (rows, Q), 0) % Q
    ck = lax.broadcasted_iota(jnp.int32, (rows, Q), 1)
    lane = lax.broadcasted_iota(jnp.int32, (Q, LANES), 1)

    def q_rope(g):
        return jnp.concatenate([qrope_ref[g, :, h * LANES:(h + 1) * LANES] for h in range(nh)], axis=0)

    for g in range(ng):
        qr = q_rope(g)
        qaug_ref[g, :, 0:LANES] = jnp.concatenate([bias_ref[g]] * nh, axis=0)
        qaug_ref[g, :, LANES:2 * LANES] = qr
        s_d = jnp.dot(qr, kst_ref[g, LANES:2 * LANES, pl.ds(t0a, Q)], preferred_element_type=F32)
        s_d = jnp.where(ck <= rq, s_d, NEG)
        m0 = jnp.max(s_d, axis=1, keepdims=True)
        p_d = jnp.exp2(s_d - m0)
        m_ref[g] = jnp.broadcast_to(m0, (rows, LANES))
        acc_ref[g] = jnp.dot(p_d.astype(BF16), vs_ref[g, pl.ds(t0a, Q), :], preferred_element_type=F32)

    ncb = SEL_CHUNK // LANES
    n_chunks = (t0 + SEL_CHUNK - 1) // SEL_CHUNK
    last = jnp.maximum(n_chunks - 1, 0)

    def scores(g, c, slot):
        k0 = pl.multiple_of(c * SEL_CHUNK, SEL_CHUNK)
        s_ref[slot, g] = jnp.dot(qaug_ref[g], kst_ref[g, :, pl.ds(k0, SEL_CHUNK)], preferred_element_type=F32)

    def consume(g, c, slot):
        k0 = pl.multiple_of(c * SEL_CHUNK, SEL_CHUNK)
        s = s_ref[slot, g]
        m_old = m_ref[g]
        m_new = jnp.maximum(m_old, jnp.max(s, axis=1, keepdims=True))
        p = jnp.concatenate([jnp.exp2(s[:, cb * LANES:(cb + 1) * LANES] - m_new) for cb in range(ncb)], axis=1)
        acc_ref[g] = (jnp.exp2(m_old - m_new) * acc_ref[g]
                      + jnp.dot(p.astype(BF16), vs_ref[g, pl.ds(k0, SEL_CHUNK), :], preferred_element_type=F32))
        m_ref[g] = m_new

    for g in range(ng):
        scores(g, 0, 0)

    def sel_pair(c2, carry):
        c = 2 * c2
        for g in range(ng):
            scores(g, c + 1, 1)
            consume(g, c, 0)
        for g in range(ng):
            scores(g, jnp.minimum(c + 2, last), 0)
            consume(g, c + 1, 1)
        return carry

    lax.fori_loop(0, n_chunks // 2, sel_pair, 0)

    @pl.when(n_chunks % 2 == 1)
    def _():
        for g in range(ng):
            consume(g, n_chunks - 1, 0)

    wlen = WINDOW + Q
    ws = pl.multiple_of(jnp.maximum(t0 - WINDOW, 0), Q)
    tq_w = t0 + lax.broadcasted_iota(jnp.int32, (rows, wlen), 0) % Q
    diff = tq_w - (ws + lax.broadcasted_iota(jnp.int32, (rows, wlen), 1))
    m_w = (diff >= 0) & (diff < WINDOW)
    for g in range(ng):
        acc = acc_ref[g]
        o_s = acc * (1.0 / acc[:, N_HEAD_DIM:N_HEAD_DIM + 1])
        s_w = jnp.dot(q_rope(g), kwt_ref[g, :, pl.ds(ws, wlen)], preferred_element_type=F32)
        s_w = jnp.where(m_w, s_w, NEG)
        p_w = jnp.exp2(s_w - jnp.max(s_w, axis=1, keepdims=True))
        acc_w = jnp.dot(p_w.astype(BF16), vw_ref[g, pl.ds(ws, wlen), :], preferred_element_type=F32)
        o_w = acc_w * (1.0 / acc_w[:, N_HEAD_DIM:N_HEAD_DIM + 1])
        gates = gates_ref[g]
        outs = []
        for h in range(nh):
            rs = slice(h * Q, (h + 1) * Q)
            outs.append(gates[:, 3 * h:3 * h + 1] * oc_ref[g, :, h * LANES:(h + 1) * LANES]
                        + gates[:, 3 * h + 1:3 * h + 2] * o_s[rs, :]
                        + gates[:, 3 * h + 2:3 * h + 3] * o_w[rs, :])
        for hp in range(nh // 2):
            yn = jnp.where(lane < N_HEAD_DIM, outs[2 * hp], pltpu.roll(outs[2 * hp + 1], N_HEAD_DIM, axis=1))
            ls = slice((g * (nh // 2) + hp) * LANES, (g * (nh // 2) + hp + 1) * LANES)
            yb_ref[:, ls] = (yn * _silu(nz_ref[:, ls].astype(F32))).astype(BF16)


def _nsa(p, b, t, qrope, bias, oc, kst, vs, kwt, vw, gates):
    g = N_KV_GROUPS
    nqb = t // Q_BLOCK
    qw = N_HPG * LANES
    rows = N_HPG * Q_BLOCK

    def qblk(w):
        return pl.BlockSpec((None, g, Q_BLOCK, w), lambda bi, i: (bi, 0, i, 0))

    def whole(r, c):
        return pl.BlockSpec((None, g, r, c), lambda bi, i: (bi, 0, 0, 0), pipeline_mode=pl.Buffered(1))

    return pl.pallas_call(
        _nsa_kernel,
        out_shape=jax.ShapeDtypeStruct((b * t, N_WIDTH), BF16),
        grid=(b, nqb),
        in_specs=[
            qblk(qw), qblk(LANES), qblk(qw),
            whole(2 * LANES, t), whole(t, LANES), whole(LANES, t), whole(t, LANES),
            qblk(LANES),
            pl.BlockSpec((Q_BLOCK, N_WIDTH), lambda bi, i: (bi * nqb + i, C_NZ // N_WIDTH)),
        ],
        out_specs=pl.BlockSpec((Q_BLOCK, N_WIDTH), lambda bi, i: (bi * nqb + i, 0)),
        scratch_shapes=[pltpu.VMEM((g, rows, 2 * LANES), BF16), pltpu.VMEM((g, rows, LANES), F32),
                        pltpu.VMEM((g, rows, LANES), F32), pltpu.VMEM((2, g, rows, SEL_CHUNK), F32)],
        compiler_params=_cparams(("parallel", "arbitrary")),
        name="nsa",
    )(qrope, bias, oc, kst, vs, kwt, vw, gates, p)


def _outproj_kernel(final, ya_ref, yb_ref, ga_ref, gb_ref, x_ref, wa_ref, wb_ref, wo_ref, fg_ref, o_ref):
    a = jnp.dot(ya_ref[...], wa_ref[...], preferred_element_type=F32)
    bq = jnp.dot(yb_ref[...], wb_ref[...], preferred_element_type=F32)
    merged = _sigmoid(ga_ref[...].astype(F32)) * a + _sigmoid(gb_ref[...].astype(F32)) * bq
    y = x_ref[...] + jnp.dot(merged.astype(BF16), wo_ref[...], preferred_element_type=F32)
    if final:
        y = y * lax.rsqrt(jnp.mean(y * y, axis=-1, keepdims=True) + NORM_EPS) * fg_ref[...]
    o_ref[...] = y


def _outproj(ya, yb, p, x2, wa, wb, wo, fg, final):
    m = x2.shape[0]
    tm = 512

    def full(a):
        return pl.BlockSpec(a.shape, lambda i: (0, 0))

    return pl.pallas_call(
        functools.partial(_outproj_kernel, final),
        out_shape=jax.ShapeDtypeStruct((m, D_MODEL), F32),
        grid=(m // tm,),
        in_specs=[
            pl.BlockSpec((tm, M_WIDTH), lambda i: (i, 0)),
            pl.BlockSpec((tm, N_WIDTH), lambda i: (i, 0)),
            pl.BlockSpec((tm, D_MODEL), lambda i: (i, C_GA // D_MODEL)),
            pl.BlockSpec((tm, D_MODEL), lambda i: (i, C_GB // D_MODEL)),
            pl.BlockSpec((tm, D_MODEL), lambda i: (i, 0)),
            full(wa), full(wb), full(wo), full(fg),
        ],
        out_specs=pl.BlockSpec((tm, D_MODEL), lambda i: (i, 0)),
        compiler_params=_cparams(("parallel",)),
        name="outproj",
    )(ya, yb, p, p, x2, wa, wb, wo, fg)


def _permute_w_in(w):
    pad = jnp.zeros((w.shape[0], P_W - _O_END), w.dtype)
    return jnp.concatenate([
        w[:, 0:_O_MI],
        w[:, _O_GA:_O_END],
        w[:, _O_NQ:_O_KC],
        w[:, _O_NZ:_O_GA],
        w[:, _O_KS:_O_NG],
        w[:, _O_KC:_O_KS],
        w[:, _O_MI:_O_NQ],
        w[:, _O_NG:_O_NZ],
        pad], axis=1)


def _expand_cmp_w1(w1k, w1v):
    half = CMP_BLOCK // 2
    k3 = w1k.astype(BF16).reshape(CMP_BLOCK, 1, N_HEAD_DIM, 1, CMP_HIDDEN)
    v3 = w1v.astype(BF16).reshape(CMP_BLOCK, 1, N_HEAD_DIM, 1, CMP_HIDDEN)
    nc = 2 * N_KV_GROUPS
    eye = jnp.eye(nc, dtype=BF16)
    is_k = (jnp.arange(nc) < N_KV_GROUPS).astype(BF16)
    mk = (eye * is_k[:, None]).reshape(1, nc, 1, nc, 1)
    mv = (eye * (1 - is_k)[:, None]).reshape(1, nc, 1, nc, 1)
    z = k3 * mk + v3 * mv
    z = z.reshape(CMP_BLOCK * nc * N_HEAD_DIM, nc * CMP_HIDDEN)
    return z[:half * nc * N_HEAD_DIM], z[half * nc * N_HEAD_DIM:]


def kernel(x, positions, norm_g, w_in, conv_q_w, conv_q_b, conv_k_w, conv_k_b, b_igate, b_fgate, mh_norm_g,
           cmp_pos_k, cmp_w1_k, cmp_w2_k, cmp_pos_v, cmp_w1_v, cmp_w2_v, b_nsa_gate, w_branch_a, w_branch_b,
           w_out, final_norm_g):
    b, t, _ = x.shape
    assert t % (2 * SEL_CHUNK) == 0 and t >= WINDOW + Q_BLOCK and t // SEL_BLOCK <= LANES and t % MLSTM_CHUNK == 0
    x2 = x.reshape(b * t, D_MODEL)
    for l in range(norm_g.shape[0]):
        p, aux = _inproj(x2, norm_g[l][None, :], _permute_w_in(w_in[l].astype(BF16)))

        sm_bias = jnp.zeros((1, LANES), F32).at[0, 0:M_HEADS].set(b_igate[l]).at[0, M_HEADS:2 * M_HEADS].set(b_fgate[l])
        ya = _mlstm(p, aux, b, t, conv_q_w[l], conv_q_b[l][None, :], conv_k_w[l], conv_k_b[l][None, :], sm_bias,
                    mh_norm_g[l][None, :])

        nr = t // CMP_STRIDE
        wt, wb = _expand_cmp_w1(cmp_w1_k[l], cmp_w1_v[l])
        w2pad = lambda w: jnp.pad(w, ((0, 0), (0, LANES - N_HEAD_DIM))).astype(BF16)
        kc, vct = _compress(aux, b, t, wt, wb, cmp_pos_k[l].reshape(1, -1), cmp_pos_v[l].reshape(1, -1),
                            cmp_w1_k[l].astype(BF16), cmp_w1_v[l].astype(BF16), w2pad(cmp_w2_k[l]), w2pad(cmp_w2_v[l]))

        half = ROPE_DIM // 2
        inv = jnp.power(jnp.float32(ROPE_THETA), -jnp.arange(half, dtype=F32) * (2.0 / ROPE_DIM))
        lih = jnp.arange(LANES) % N_HEAD_DIM
        inv_row = jnp.where(lih < ROPE_DIM, inv[lih % half], 0.0)[None, :].astype(F32)
        pos_col = positions.astype(F32).reshape(b * t, 1)
        gate_bias = jnp.zeros((1, LANES), F32).at[0, 8:8 + 3 * N_HEADS].set(b_nsa_gate[l])
        qrawt, qrope, kst, vs, kwt, vw, gates = _kvprep(p, aux, b, t, pos_col, inv_row, gate_bias)

        ci = jnp.arange(nr)[None, :] * CMP_STRIDE
        sj = jnp.arange(LANES)[:, None] * SEL_BLOCK
        overlap_t = ((ci < sj + SEL_BLOCK) & (ci + CMP_BLOCK > sj) & (jnp.arange(nr)[None, :] < nr - 1)
                     & (jnp.arange(LANES)[:, None] < t // SEL_BLOCK)).astype(BF16)
        oc, bias = _select(qrawt, kc, vct, overlap_t)
        yb = _nsa(p, b, t, qrope, bias, oc, kst, vs, kwt, vw, gates)

        x2 = _outproj(ya, yb, p, x2, w_branch_a[l].astype(BF16), w_branch_b[l].astype(BF16),
                      w_out[l].astype(BF16), final_norm_g[None, :], l == norm_g.shape[0] - 1)
    return x2.reshape(b, t, D_MODEL)
```

```python
import functools
import math

import jax
import jax.numpy as jnp
from jax import lax
from jax.experimental import pallas as pl
from jax.experimental.pallas import tpu as pltpu

F32 = jnp.float32
BF16 = jnp.bfloat16

D_MODEL = 1024
M_HEADS = 4
M_HEAD_DIM = 256
M_WIDTH = M_HEADS * M_HEAD_DIM
CONV_WIDTH = 4
N_HEADS = 8
N_KV_GROUPS = 2
N_HPG = N_HEADS // N_KV_GROUPS
N_HEAD_DIM = 64
N_WIDTH = N_HEADS * N_HEAD_DIM
CMP_BLOCK = 32
CMP_STRIDE = 16
CMP_HIDDEN = 128
SEL_BLOCK = 64
SEL_TOPK = 16
WINDOW = 512
Q_BLOCK = 256
ROPE_THETA = 500000.0
ROPE_DIM = N_HEAD_DIM // 4
NORM_EPS = 1e-6
NEG = -1e30
FORCE = 1e9

LANES = 128
SUBLANES = 8
MLSTM_CHUNK = 256
SEL_CHUNK = 512
VMEM_LIMIT = 56 * 1024 * 1024

C_MQ, C_MK, C_MV, C_MO, C_MZ, C_GA, C_GB = 0, 1024, 2048, 3072, 4096, 5120, 6144
C_NQ, C_NZ = 7168, 7680
C_KV = 8192
C_CMP = 8704
C_SM = 8960
P_W = 9216
AUX_W = 3 * LANES
X_KC, X_VC, X_SM = 0, 1, 2
_O_MI, _O_NQ, _O_KC, _O_KS, _O_NG, _O_NZ, _O_GA, _O_END = 5120, 5128, 5640, 5896, 6408, 6432, 6944, 8992


def _cparams(sem):
    return pltpu.CompilerParams(dimension_semantics=sem, vmem_limit_bytes=VMEM_LIMIT)


def _sigmoid(x):
    return 0.5 * jnp.tanh(0.5 * x) + 0.5


def _silu(x):
    h = 0.5 * x
    return h + h * jnp.tanh(h)


def _log_sigmoid(x):
    return jnp.minimum(x, 0.0) - jnp.log1p(jnp.exp(-jnp.abs(x)))


PROJ_TN = 1024
AUX_TILE = C_CMP // PROJ_TN
AUX_OFF = C_CMP % PROJ_TN


N_HEAD_TILES = _O_MI // PROJ_TN


def _inproj_kernel(x_ref, g_ref, wh_ref, wt_ref, o_ref, aux_ref, hn_ref):
    j = pl.program_id(1)

    @pl.when(j == 0)
    def _():
        x = x_ref[...]
        r = lax.rsqrt(jnp.mean(x * x, axis=-1, keepdims=True) + NORM_EPS)
        hn_ref[...] = (x * r * g_ref[...]).astype(BF16)

    @pl.when(j < N_HEAD_TILES)
    def _():
        o_ref[...] = jnp.dot(hn_ref[...], wh_ref[...], preferred_element_type=F32).astype(BF16)

    @pl.when((j >= N_HEAD_TILES) & (j != AUX_TILE))
    def _():
        o_ref[...] = jnp.dot(hn_ref[...], wt_ref[...], preferred_element_type=F32).astype(BF16)

    @pl.when(j == AUX_TILE)
    def _():
        acc = jnp.dot(hn_ref[...], wt_ref[...], preferred_element_type=F32)
        o_ref[...] = acc.astype(BF16)
        aux_ref[...] = acc[:, AUX_OFF:AUX_OFF + AUX_W]


def _inproj(x2, norm_g, w_head, w_tail):
    m = x2.shape[0]
    tm, tn = 1024, PROJ_TN
    return pl.pallas_call(
        _inproj_kernel,
        out_shape=(jax.ShapeDtypeStruct((m, P_W), BF16), jax.ShapeDtypeStruct((m, AUX_W), F32)),
        grid=(m // tm, P_W // tn),
        in_specs=[
            pl.BlockSpec((tm, D_MODEL), lambda i, j: (i, 0)),
            pl.BlockSpec((1, D_MODEL), lambda i, j: (0, 0)),
            pl.BlockSpec((D_MODEL, tn), lambda i, j: (0, jnp.minimum(j, N_HEAD_TILES - 1))),
            pl.BlockSpec((D_MODEL, tn), lambda i, j: (0, jnp.maximum(j - N_HEAD_TILES, 0))),
        ],
        out_specs=(pl.BlockSpec((tm, tn), lambda i, j: (i, j)), pl.BlockSpec((tm, AUX_W), lambda i, j: (i, 0))),
        scratch_shapes=[pltpu.VMEM((tm, D_MODEL), BF16)],
        compiler_params=_cparams(("parallel", "arbitrary")),
        name="inproj",
    )(x2, norm_g, w_head, w_tail)


def _causal_conv_silu(u, prev_ref, w_ref, b_ref):
    n, w = u.shape
    u3 = u.reshape(n // SUBLANES, SUBLANES, w)
    sub = lax.broadcasted_iota(jnp.int32, u3.shape, 1)
    prev3 = prev_ref[...].reshape(1, SUBLANES, w)
    y = b_ref[...] + u * w_ref[CONV_WIDTH - 1:CONV_WIDTH, :]
    for d in range(1, CONV_WIDTH):
        r = pltpu.roll(u3, d, axis=1)
        above = jnp.concatenate([pltpu.roll(prev3, d, axis=1), r[:-1]], axis=0)
        y = y + jnp.where(sub < d, above, r).reshape(n, w) * w_ref[CONV_WIDTH - 1 - d:CONV_WIDTH - d, :]
    prev_ref[...] = u[n - SUBLANES:n, :]
    return _silu(y)


def _mlstm_kernel(q_ref, k_ref, v_ref, og_ref, z_ref, sm_ref, cqw_ref, cqb_ref, ckw_ref, ckb_ref,
                  sb_ref, ng_ref, ya_ref, qprev, kprev, c_state, n_state, m_state):
    L = q_ref.shape[0]
    d = M_HEAD_DIM

    @pl.when(pl.program_id(1) == 0)
    def _():
        qprev[...] = jnp.zeros_like(qprev)
        kprev[...] = jnp.zeros_like(kprev)
        c_state[...] = jnp.zeros_like(c_state)
        n_state[...] = jnp.zeros_like(n_state)
        m_state[...] = jnp.zeros_like(m_state)

    mq = _causal_conv_silu(q_ref[...].astype(F32), qprev, cqw_ref, cqb_ref)
    mk = _causal_conv_silu(k_ref[...].astype(F32), kprev, ckw_ref, ckb_ref) * (d ** -0.5)

    sm = sm_ref[...] + sb_ref[...]
    lf = _log_sigmoid(sm)
    row = lax.broadcasted_iota(jnp.int32, (L, L), 0)
    col = lax.broadcasted_iota(jnp.int32, (L, L), 1)
    causal = row >= col
    tril = causal.astype(F32)
    triu = (row <= col).astype(F32)
    a_col = jnp.dot(tril, lf, precision=lax.Precision.HIGHEST, preferred_element_type=F32)
    sm_t = sm.T[0:8, :]
    a_row = jnp.dot(_log_sigmoid(sm_t), triu, precision=lax.Precision.HIGHEST,
                    preferred_element_type=F32)

    for h in range(M_HEADS):
        hs = slice(h * d, (h + 1) * d)
        a_c = a_col[:, 4 + h:5 + h]
        li_c = sm[:, h:h + 1]
        a_r = a_row[4 + h:5 + h, :]
        li_r = sm_t[h:h + 1, :]
        m_prev = m_state[h:h + 1, 0:1]
        a_end = a_c[L - 1:L, :]

        log_d = jnp.where(causal, a_c - a_r + li_r, -jnp.inf)
        inter = a_c + m_prev
        m_t = jnp.maximum(inter, jnp.max(log_d, axis=1, keepdims=True))
        w_inter = jnp.exp(inter - m_t)
        dm = jnp.exp(log_d - m_t)

        qh = mq[:, hs]
        kh = mk[:, hs]
        qb = qh.astype(BF16)
        kb = kh.astype(BF16)
        vb = v_ref[:, hs]
        s = lax.dot_general(qb, kb, (((1,), (1,)), ((), ())), preferred_element_type=F32) * dm
        c_old = c_state[h]
        n_old = n_state[h:h + 1, :]
        num = (w_inter * jnp.dot(qb, c_old.astype(BF16), preferred_element_type=F32)
               + jnp.dot(s.astype(BF16), vb, preferred_element_type=F32))
        den = (w_inter * jnp.sum(qh * n_old, axis=1, keepdims=True)
               + jnp.sum(s, axis=1, keepdims=True))
        hh = num * (1.0 / jnp.maximum(jnp.abs(den), jnp.exp(-m_t)))

        logw = a_end - a_c + li_c
        m_new = jnp.maximum(a_end + m_prev, jnp.max(logw, axis=0, keepdims=True))
        wk = jnp.exp(logw - m_new)
        decay = jnp.exp(a_end + m_prev - m_new)
        kw = kh * wk
        c_state[h] = decay * c_old + lax.dot_general(kw.astype(BF16), vb, (((0,), (0,)), ((), ())),
                                                     preferred_element_type=F32)
        n_state[h:h + 1, :] = decay * n_old + jnp.sum(kw, axis=0, keepdims=True)
        m_state[h:h + 1, :] = jnp.broadcast_to(m_new, (1, LANES))

        mu = jnp.mean(hh, axis=1, keepdims=True)
        dc = hh - mu
        var = jnp.mean(dc * dc, axis=1, keepdims=True)
        hn = dc * lax.rsqrt(var + NORM_EPS) * ng_ref[:, hs]
        ya = _sigmoid(og_ref[:, hs].astype(F32)) * hn * _silu(z_ref[:, hs].astype(F32))
        ya_ref[:, hs] = ya.astype(BF16)


def _mlstm(p, aux, b, t, conv_q_w, conv_q_b, conv_k_w, conv_k_b, sm_bias, mh_norm_g):
    L = MLSTM_CHUNK
    nc = t // L

    def colspec(cstart):
        return pl.BlockSpec((L, M_WIDTH), lambda bi, c: (bi * nc + c, cstart // M_WIDTH))

    def full(shape):
        return pl.BlockSpec(shape, lambda bi, c: (0,) * len(shape))

    return pl.pallas_call(
        _mlstm_kernel,
        out_shape=jax.ShapeDtypeStruct((b * t, M_WIDTH), BF16),
        grid=(b, nc),
        in_specs=[
            colspec(C_MQ), colspec(C_MK), colspec(C_MV), colspec(C_MO), colspec(C_MZ),
            pl.BlockSpec((L, LANES), lambda bi, c: (bi * nc + c, X_SM)),
            full((CONV_WIDTH, M_WIDTH)), full((1, M_WIDTH)), full((CONV_WIDTH, M_WIDTH)), full((1, M_WIDTH)),
            full((1, LANES)), full((1, M_WIDTH)),
        ],
        out_specs=pl.BlockSpec((L, M_WIDTH), lambda bi, c: (bi * nc + c, 0)),
        scratch_shapes=[
            pltpu.VMEM((SUBLANES, M_WIDTH), F32), pltpu.VMEM((SUBLANES, M_WIDTH), F32),
            pltpu.VMEM((M_HEADS, M_HEAD_DIM, M_HEAD_DIM), F32),
            pltpu.VMEM((8, M_HEAD_DIM), F32), pltpu.VMEM((8, LANES), F32),
        ],
        compiler_params=_cparams(("parallel", "arbitrary")),
        name="mlstm",
    )(p, p, p, p, p, aux, conv_q_w, conv_q_b, conv_k_w, conv_k_b, sm_bias, mh_norm_g)


def _compress_kernel(pk_ref, pv_ref, wt_ref, wb_ref, posk_ref, posv_ref, w1k_ref, w1v_ref, w2k_ref, w2v_ref,
                     kc_ref, vct_ref):
    nr = pk_ref.shape[0] // CMP_STRIDE
    cw = 2 * LANES
    u = jnp.zeros((nr, wt_ref.shape[1]), F32)
    v = jnp.zeros((nr, wb_ref.shape[1]), F32)
    for l in range(CMP_STRIDE):
        x = jnp.concatenate([pk_ref[pl.ds(l, nr, stride=CMP_STRIDE), :],
                             pv_ref[pl.ds(l, nr, stride=CMP_STRIDE), :]], axis=1).astype(BF16)
        u = u + jnp.dot(x, wt_ref[l * cw:(l + 1) * cw, :], preferred_element_type=F32)
        v = v + jnp.dot(x, wb_ref[l * cw:(l + 1) * cw, :], preferred_element_type=F32)
    v = pltpu.roll(v, nr - 1, axis=0)
    rowi = lax.broadcasted_iota(jnp.int32, v.shape, 0)
    v = jnp.where(rowi < nr - 1, v, 0.0)

    def pos_term(pos_ref, w_ref):
        pb = jnp.broadcast_to(pos_ref[...].astype(BF16), (8, CMP_BLOCK * N_HEAD_DIM))
        return jnp.dot(pb, w_ref[...], preferred_element_type=F32)[0:1, :]

    pk = pos_term(posk_ref, w1k_ref)
    pv = pos_term(posv_ref, w1v_ref)
    hid = u + v + jnp.concatenate([pk, pk, pv, pv], axis=1)
    act = 0.5 * hid * (1.0 + jnp.tanh(math.sqrt(2.0 / math.pi) * (hid + 0.044715 * (hid * hid * hid))))
    for c in range(2 * N_KV_GROUPS):
        w2 = w2k_ref[...] if c < N_KV_GROUPS else w2v_ref[...]
        out = jnp.dot(act[:, c * CMP_HIDDEN:(c + 1) * CMP_HIDDEN].astype(BF16), w2,
                      preferred_element_type=F32)
        if c < N_KV_GROUPS:
            kc_ref[c] = out.astype(BF16)
        else:
            vct_ref[c - N_KV_GROUPS] = out.T.astype(BF16)


def _compress(aux, b, t, wt, wb, posk, posv, w1k, w1v, w2k, w2v):
    nr = t // CMP_STRIDE
    g = N_KV_GROUPS

    def full(a):
        return pl.BlockSpec(a.shape, lambda bi: (0,) * a.ndim)

    return pl.pallas_call(
        _compress_kernel,
        out_shape=(jax.ShapeDtypeStruct((b, g, nr, LANES), BF16), jax.ShapeDtypeStruct((b, g, LANES, nr), BF16)),
        grid=(b,),
        in_specs=[pl.BlockSpec((t, LANES), lambda bi: (bi, X_KC)),
                  pl.BlockSpec((t, LANES), lambda bi: (bi, X_VC)),
                  full(wt), full(wb), full(posk), full(posv), full(w1k), full(w1v), full(w2k), full(w2v)],
        out_specs=(pl.BlockSpec((None, g, nr, LANES), lambda bi: (bi, 0, 0, 0)),
                   pl.BlockSpec((None, g, LANES, nr), lambda bi: (bi, 0, 0, 0))),
        compiler_params=_cparams(("parallel",)),
        name="compress",
    )(aux, aux, wt, wb, posk, posv, w1k, w1v, w2k, w2v)


def _kvprep_kernel(q_ref, kv_ref, sm_ref, pos_ref, inv_ref, gb_ref,
                   qrawt_ref, qrope_ref, kst_ref, vs_ref, kwt_ref, vw_ref, gates_ref):
    tk = q_ref.shape[0]
    lane = lax.broadcasted_iota(jnp.int32, (tk, LANES), 1)
    lih = lane % N_HEAD_DIM
    pos = jnp.broadcast_to(pos_ref[...], (LANES, tk)).T
    ang = pos * inv_ref[...]
    cos = jnp.cos(ang)
    sin = jnp.sin(ang)
    half = ROPE_DIM // 2
    s_up = jnp.where(lih < half, -sin, 0.0)
    s_dn = jnp.where((lih >= half) & (lih < ROPE_DIM), sin, 0.0)

    def rope(u):
        return u * cos + pltpu.roll(u, LANES - half, axis=1) * s_up + pltpu.roll(u, half, axis=1) * s_dn

    def head_pad(u, odd):
        if odd:
            u = pltpu.roll(u, N_HEAD_DIM, axis=1)
        return jnp.where(lane < N_HEAD_DIM, u, 0.0)

    scale = N_HEAD_DIM ** -0.5 * math.log2(math.e)
    zeros_t = jnp.zeros((N_HEAD_DIM, tk), BF16)
    for g in range(N_KV_GROUPS):
        for hp in range(N_HPG // 2):
            c0 = g * (N_HPG * N_HEAD_DIM) + hp * LANES
            u = q_ref[:, c0:c0 + LANES].astype(F32) * scale
            ur = rope(u)
            u_t = u.T.astype(BF16)
            for odd in range(2):
                h = 2 * hp + odd
                qrawt_ref[g, h * LANES:h * LANES + N_HEAD_DIM, :] = u_t[odd * N_HEAD_DIM:(odd + 1) * N_HEAD_DIM, :]
                qrawt_ref[g, h * LANES + N_HEAD_DIM:(h + 1) * LANES, :] = zeros_t
                qrope_ref[g, :, h * LANES:(h + 1) * LANES] = head_pad(ur, odd).astype(BF16)

    ks_t = rope(kv_ref[:, 0:LANES].astype(F32)).T
    kw_t = rope(kv_ref[:, 2 * LANES:3 * LANES].astype(F32)).T
    vs = kv_ref[:, LANES:2 * LANES].astype(F32)
    vw = kv_ref[:, 3 * LANES:4 * LANES].astype(F32)
    t_glob = pl.program_id(1) * tk + lax.broadcasted_iota(jnp.int32, (LANES, tk), 1)
    blk_row = lax.broadcasted_iota(jnp.int32, (LANES, tk), 0)
    onehot = jnp.where(blk_row == t_glob // SEL_BLOCK, 1.0, 0.0).astype(BF16)
    ones_lane = jnp.where(lane == N_HEAD_DIM, 1.0, 0.0)
    gates = _sigmoid(sm_ref[...] + gb_ref[...])
    for g in range(N_KV_GROUPS):
        rs = slice(g * N_HEAD_DIM, (g + 1) * N_HEAD_DIM)
        kst_ref[g, 0:LANES, :] = onehot
        kst_ref[g, LANES:LANES + N_HEAD_DIM, :] = ks_t[rs, :].astype(BF16)
        kst_ref[g, LANES + N_HEAD_DIM:2 * LANES, :] = zeros_t
        kwt_ref[g, 0:N_HEAD_DIM, :] = kw_t[rs, :].astype(BF16)
        kwt_ref[g, N_HEAD_DIM:LANES, :] = zeros_t
        vs_ref[g] = (head_pad(vs, g) + ones_lane).astype(BF16)
        vw_ref[g] = (head_pad(vw, g) + ones_lane).astype(BF16)
        goff = 8 + g * (3 * N_HPG)
        gates_ref[g] = pltpu.roll(gates, LANES - goff, axis=1)


def _kvprep(p, aux, b, t, pos_col, inv_row, gate_bias):
    tk = 512
    nt = t // tk
    g = N_KV_GROUPS
    qw = N_HPG * LANES

    def tok(shape):
        return pl.BlockSpec((None, g) + shape, lambda bi, i: (bi, 0, i, 0))

    def tok_t(rows):
        return pl.BlockSpec((None, g, rows, tk), lambda bi, i: (bi, 0, 0, i))

    return pl.pallas_call(
        _kvprep_kernel,
        out_shape=(
            jax.ShapeDtypeStruct((b, g, qw, t), BF16), jax.ShapeDtypeStruct((b, g, t, qw), BF16),
            jax.ShapeDtypeStruct((b, g, 2 * LANES, t), BF16), jax.ShapeDtypeStruct((b, g, t, LANES), BF16),
            jax.ShapeDtypeStruct((b, g, LANES, t), BF16), jax.ShapeDtypeStruct((b, g, t, LANES), BF16),
            jax.ShapeDtypeStruct((b, g, t, LANES), F32),
        ),
        grid=(b, nt),
        in_specs=[
            pl.BlockSpec((tk, N_WIDTH), lambda bi, i: (bi * nt + i, C_NQ // N_WIDTH)),
            pl.BlockSpec((tk, 4 * LANES), lambda bi, i: (bi * nt + i, C_KV // (4 * LANES))),
            pl.BlockSpec((tk, LANES), lambda bi, i: (bi * nt + i, X_SM)),
            pl.BlockSpec((None, 1, tk), lambda bi, i: (bi, 0, i)),
            pl.BlockSpec((1, LANES), lambda bi, i: (0, 0)),
            pl.BlockSpec((1, LANES), lambda bi, i: (0, 0)),
        ],
        out_specs=(tok_t(qw), tok((tk, qw)), tok_t(2 * LANES), tok((tk, LANES)),
                   tok_t(LANES), tok((tk, LANES)), tok((tk, LANES))),
        compiler_params=_cparams(("parallel", "arbitrary")),
        name="kvprep",
    )(p, p, aux, pos_col, inv_row, gate_bias)


def _select_kernel(qt_ref, kc_ref, vct_ref, ovt_ref, oc_ref, bias_ref):
    qs = qt_ref.shape[1]
    nr = kc_ref.shape[0]
    t0 = pl.program_id(2) * qs
    m_c = (lax.broadcasted_iota(jnp.int32, (nr, qs), 0) * CMP_STRIDE + (CMP_BLOCK - 1)
           <= t0 + lax.broadcasted_iota(jnp.int32, (nr, qs), 1))
    kc = kc_ref[...]
    vct = vct_ref[...]
    sees_any = t0 + lax.broadcasted_iota(jnp.int32, (1, qs), 1) >= CMP_BLOCK - 1
    psum = jnp.zeros((nr, qs), F32)
    for h in range(N_HPG):
        s = jnp.dot(kc, qt_ref[h * LANES:(h + 1) * LANES, :], preferred_element_type=F32)
        s = jnp.where(m_c, s, NEG)
        e = jnp.exp2(s - jnp.max(s, axis=0, keepdims=True))
        den = jnp.sum(e, axis=0, keepdims=True)
        pc = e * jnp.where(sees_any, 1.0 / den, 0.0)
        psum = psum + pc
        oc_t = jnp.dot(vct, pc.astype(BF16), preferred_element_type=F32)
        oc_ref[:, h * LANES:(h + 1) * LANES] = oc_t.T
    p_hi = psum.astype(BF16)
    p_lo = (psum - p_hi.astype(F32)).astype(BF16)
    ovt = ovt_ref[...]
    imp = jnp.dot(ovt, p_hi, preferred_element_type=F32) + jnp.dot(ovt, p_lo, preferred_element_type=F32)

    blk = lax.broadcasted_iota(jnp.int32, (LANES, qs), 0)
    tq = t0 + lax.broadcasted_iota(jnp.int32, (LANES, qs), 1)
    cur = tq // SEL_BLOCK
    forced = (blk == 0) | (blk == cur) | (blk == cur - 1)
    n_sel = (nr * CMP_STRIDE) // SEL_BLOCK
    score = jnp.where(forced, FORCE, jnp.where(blk * SEL_BLOCK <= tq, imp, -1.0))
    score = jnp.where(blk < n_sel, score, -jnp.inf)
    blk_f = blk.astype(F32)
    for _ in range(min(SEL_TOPK, n_sel)):
        mx = jnp.max(score, axis=0, keepdims=True)
        idx = jnp.min(jnp.where(score == mx, blk_f, float(LANES)), axis=0, keepdims=True)
        score = jnp.where(blk_f == idx, -jnp.inf, score)
    bias_t = jnp.where((score == -jnp.inf) & (blk < n_sel), 0.0, NEG)
    bias_ref[...] = bias_t.T.astype(BF16)


def _select(qrawt, kc, vct, ovt):
    b, g, qw, t = qrawt.shape
    nr = kc.shape[2]
    qs = 512

    def whole(rows, cols):
        return pl.BlockSpec((None, None, rows, cols), lambda bi, gi, i: (bi, gi, 0, 0))

    return pl.pallas_call(
        _select_kernel,
        out_shape=(jax.ShapeDtypeStruct((b, g, t, qw), F32), jax.ShapeDtypeStruct((b, g, t, LANES), BF16)),
        grid=(b, g, t // qs),
        in_specs=[
            pl.BlockSpec((None, None, qw, qs), lambda bi, gi, i: (bi, gi, 0, i)),
            whole(nr, LANES), whole(LANES, nr),
            pl.BlockSpec(ovt.shape, lambda bi, gi, i: (0, 0)),
        ],
        out_specs=(pl.BlockSpec((None, None, qs, qw), lambda bi, gi, i: (bi, gi, i, 0)),
                   pl.BlockSpec((None, None, qs, LANES), lambda bi, gi, i: (bi, gi, i, 0))),
        compiler_params=_cparams(("parallel", "parallel", "parallel")),
        name="select",
    )(qrawt, kc, vct, ovt)


def _nsa_kernel(qrope_ref, bias_ref, oc_ref, kst_ref, vs_ref, kwt_ref, vw_ref, gates_ref, nz_ref,
                yb_ref, qaug_ref, acc_ref, m_ref, s_ref):
    i = pl.program_id(1)
    t0 = i * Q_BLOCK
    Q = Q_BLOCK
    nh = N_HPG
    ng = N_KV_GROUPS
    rows = nh * Q
    t0a = pl.multiple_of(t0, Q)
    rq = lax.broadcasted_iota(jnp.int32, (rows, Q), 0) % Q
    ck = lax.broadcasted_iota(jnp.int32, (rows, Q), 1)
    lane = lax.broadcasted_iota(jnp.int32, (Q, LANES), 1)

    def q_rope(g):
        return jnp.concatenate([qrope_ref[g, :, h * LANES:(h + 1) * LANES] for h in range(nh)], axis=0)

    first_blk = t0 // SEL_BLOCK
    for g in range(ng):
        qr = q_rope(g)
        bias = bias_ref[g]
        bias_early = jnp.where(lane < first_blk, bias, NEG).astype(BF16)
        qaug_ref[g, :, 0:LANES] = jnp.concatenate([bias_early] * nh, axis=0)
        qaug_ref[g, :, LANES:2 * LANES] = qr
        q_diag = jnp.concatenate([jnp.concatenate([bias] * nh, axis=0), qr], axis=1)
        s_d = jnp.dot(q_diag, kst_ref[g, :, pl.ds(t0a, Q)], preferred_element_type=F32)
        s_d = jnp.where(ck <= rq, s_d, NEG)
        m0 = jnp.max(s_d, axis=1, keepdims=True)
        p_d = jnp.exp2(s_d - m0)
        m_ref[g] = jnp.broadcast_to(m0, (rows, LANES))
        acc_ref[g] = jnp.dot(p_d.astype(BF16), vs_ref[g, pl.ds(t0a, Q), :], preferred_element_type=F32)

    ncb = SEL_CHUNK // LANES
    n_chunks = (t0 + SEL_CHUNK - 1) // SEL_CHUNK
    last = jnp.maximum(n_chunks - 1, 0)

    def scores(g, c, slot):
        k0 = pl.multiple_of(c * SEL_CHUNK, SEL_CHUNK)
        s_ref[slot, g] = jnp.dot(qaug_ref[g], kst_ref[g, :, pl.ds(k0, SEL_CHUNK)], preferred_element_type=F32)

    def consume(g, c, slot):
        k0 = pl.multiple_of(c * SEL_CHUNK, SEL_CHUNK)
        s = s_ref[slot, g]
        m_old = m_ref[g]
        m_new = jnp.maximum(m_old, jnp.max(s, axis=1, keepdims=True))
        p = jnp.concatenate([jnp.exp2(s[:, cb * LANES:(cb + 1) * LANES] - m_new) for cb in range(ncb)], axis=1)
        acc_ref[g] = (jnp.exp2(m_old - m_new) * acc_ref[g]
                      + jnp.dot(p.astype(BF16), vs_ref[g, pl.ds(k0, SEL_CHUNK), :], preferred_element_type=F32))
        m_ref[g] = m_new

    for g in range(ng):
        scores(g, 0, 0)

    def sel_pair(c2, carry):
        c = 2 * c2
        for g in range(ng):
            scores(g, c + 1, 1)
            consume(g, c, 0)
        for g in range(ng):
            scores(g, jnp.minimum(c + 2, last), 0)
            consume(g, c + 1, 1)
        return carry

    lax.fori_loop(0, n_chunks // 2, sel_pair, 0)

    @pl.when(n_chunks % 2 == 1)
    def _():
        for g in range(ng):
            consume(g, n_chunks - 1, 0)

    wlen = WINDOW + Q
    ws = pl.multiple_of(jnp.maximum(t0 - WINDOW, 0), Q)
    tq_w = t0 + lax.broadcasted_iota(jnp.int32, (rows, wlen), 0) % Q
    diff = tq_w - (ws + lax.broadcasted_iota(jnp.int32, (rows, wlen), 1))
    m_w = (diff >= 0) & (diff < WINDOW)
    for g in range(ng):
        acc = acc_ref[g]
        o_s = acc * (1.0 / acc[:, N_HEAD_DIM:N_HEAD_DIM + 1])
        s_w = jnp.dot(q_rope(g), kwt_ref[g, :, pl.ds(ws, wlen)], preferred_element_type=F32)
        s_w = jnp.where(m_w, s_w, NEG)
        p_w = jnp.exp2(s_w - jnp.max(s_w, axis=1, keepdims=True))
        acc_w = jnp.dot(p_w.astype(BF16), vw_ref[g, pl.ds(ws, wlen), :], preferred_element_type=F32)
        o_w = acc_w * (1.0 / acc_w[:, N_HEAD_DIM:N_HEAD_DIM + 1])
        gates = gates_ref[g]
        outs = []
        for h in range(nh):
            rs = slice(h * Q, (h + 1) * Q)
            outs.append(gates[:, 3 * h:3 * h + 1] * oc_ref[g, :, h * LANES:(h + 1) * LANES]
                        + gates[:, 3 * h + 1:3 * h + 2] * o_s[rs, :]
                        + gates[:, 3 * h + 2:3 * h + 3] * o_w[rs, :])
        for hp in range(nh // 2):
            yn = jnp.where(lane < N_HEAD_DIM, outs[2 * hp], pltpu.roll(outs[2 * hp + 1], N_HEAD_DIM, axis=1))
            ls = slice((g * (nh // 2) + hp) * LANES, (g * (nh // 2) + hp + 1) * LANES)
            yb_ref[:, ls] = (yn * _silu(nz_ref[:, ls].astype(F32))).astype(BF16)


def _nsa(p, b, t, qrope, bias, oc, kst, vs, kwt, vw, gates):
    g = N_KV_GROUPS
    nqb = t // Q_BLOCK
    qw = N_HPG * LANES
    rows = N_HPG * Q_BLOCK

    def qblk(w):
        return pl.BlockSpec((None, g, Q_BLOCK, w), lambda bi, i: (bi, 0, i, 0))

    def whole(r, c):
        return pl.BlockSpec((None, g, r, c), lambda bi, i: (bi, 0, 0, 0), pipeline_mode=pl.Buffered(1))

    return pl.pallas_call(
        _nsa_kernel,
        out_shape=jax.ShapeDtypeStruct((b * t, N_WIDTH), BF16),
        grid=(b, nqb),
        in_specs=[
            qblk(qw), qblk(LANES), qblk(qw),
            whole(2 * LANES, t), whole(t, LANES), whole(LANES, t), whole(t, LANES),
            qblk(LANES),
            pl.BlockSpec((Q_BLOCK, N_WIDTH), lambda bi, i: (bi * nqb + i, C_NZ // N_WIDTH)),
        ],
        out_specs=pl.BlockSpec((Q_BLOCK, N_WIDTH), lambda bi, i: (bi * nqb + i, 0)),
        scratch_shapes=[pltpu.VMEM((g, rows, 2 * LANES), BF16), pltpu.VMEM((g, rows, LANES), F32),
                        pltpu.VMEM((g, rows, LANES), F32), pltpu.VMEM((2, g, rows, SEL_CHUNK), F32)],
        compiler_params=_cparams(("parallel", "arbitrary")),
        name="nsa",
    )(qrope, bias, oc, kst, vs, kwt, vw, gates, p)


def _outproj_kernel(final, ya_ref, yb_ref, ga_ref, gb_ref, x_ref, wa_ref, wb_ref, wo_ref, fg_ref, o_ref):
    a = jnp.dot(ya_ref[...], wa_ref[...], preferred_element_type=F32)
    bq = jnp.dot(yb_ref[...], wb_ref[...], preferred_element_type=F32)
    merged = _sigmoid(ga_ref[...].astype(F32)) * a + _sigmoid(gb_ref[...].astype(F32)) * bq
    y = x_ref[...] + jnp.dot(merged.astype(BF16), wo_ref[...], preferred_element_type=F32)
    if final:
        y = y * lax.rsqrt(jnp.mean(y * y, axis=-1, keepdims=True) + NORM_EPS) * fg_ref[...]
    o_ref[...] = y


def _outproj(ya, yb, p, x2, wa, wb, wo, fg, final):
    m = x2.shape[0]
    tm = 512

    def full(a):
        return pl.BlockSpec(a.shape, lambda i: (0, 0))

    return pl.pallas_call(
        functools.partial(_outproj_kernel, final),
        out_shape=jax.ShapeDtypeStruct((m, D_MODEL), F32),
        grid=(m // tm,),
        in_specs=[
            pl.BlockSpec((tm, M_WIDTH), lambda i: (i, 0)),
            pl.BlockSpec((tm, N_WIDTH), lambda i: (i, 0)),
            pl.BlockSpec((tm, D_MODEL), lambda i: (i, C_GA // D_MODEL)),
            pl.BlockSpec((tm, D_MODEL), lambda i: (i, C_GB // D_MODEL)),
            pl.BlockSpec((tm, D_MODEL), lambda i: (i, 0)),
            full(wa), full(wb), full(wo), full(fg),
        ],
        out_specs=pl.BlockSpec((tm, D_MODEL), lambda i: (i, 0)),
        compiler_params=_cparams(("parallel",)),
        name="outproj",
    )(ya, yb, p, p, x2, wa, wb, wo, fg)


def _permute_w_tail(w):
    pad = jnp.zeros((w.shape[0], P_W - _O_END), w.dtype)
    return jnp.concatenate([
        w[:, _O_GA:_O_END],
        w[:, _O_NQ:_O_KC],
        w[:, _O_NZ:_O_GA],
        w[:, _O_KS:_O_NG],
        w[:, _O_KC:_O_KS],
        w[:, _O_MI:_O_NQ],
        w[:, _O_NG:_O_NZ],
        pad], axis=1)


def _expand_cmp_w1(w1k, w1v):
    half = CMP_BLOCK // 2
    k3 = w1k.astype(BF16).reshape(CMP_BLOCK, 1, N_HEAD_DIM, 1, CMP_HIDDEN)
    v3 = w1v.astype(BF16).reshape(CMP_BLOCK, 1, N_HEAD_DIM, 1, CMP_HIDDEN)
    nc = 2 * N_KV_GROUPS
    eye = jnp.eye(nc, dtype=BF16)
    is_k = (jnp.arange(nc) < N_KV_GROUPS).astype(BF16)
    mk = (eye * is_k[:, None]).reshape(1, nc, 1, nc, 1)
    mv = (eye * (1 - is_k)[:, None]).reshape(1, nc, 1, nc, 1)
    z = k3 * mk + v3 * mv
    z = z.reshape(CMP_BLOCK * nc * N_HEAD_DIM, nc * CMP_HIDDEN)
    return z[:half * nc * N_HEAD_DIM], z[half * nc * N_HEAD_DIM:]


def kernel(x, positions, norm_g, w_in, conv_q_w, conv_q_b, conv_k_w, conv_k_b, b_igate, b_fgate, mh_norm_g,
           cmp_pos_k, cmp_w1_k, cmp_w2_k, cmp_pos_v, cmp_w1_v, cmp_w2_v, b_nsa_gate, w_branch_a, w_branch_b,
           w_out, final_norm_g):
    b, t, _ = x.shape
    assert t % (2 * SEL_CHUNK) == 0 and t >= WINDOW + Q_BLOCK and t // SEL_BLOCK <= LANES and t % MLSTM_CHUNK == 0
    x2 = x.reshape(b * t, D_MODEL)
    for l in range(norm_g.shape[0]):
        w_bf = w_in[l].astype(BF16)
        p, aux = _inproj(x2, norm_g[l][None, :], w_bf, _permute_w_tail(w_bf))

        sm_bias = jnp.zeros((1, LANES), F32).at[0, 0:M_HEADS].set(b_igate[l]).at[0, M_HEADS:2 * M_HEADS].set(b_fgate[l])
        ya = _mlstm(p, aux, b, t, conv_q_w[l], conv_q_b[l][None, :], conv_k_w[l], conv_k_b[l][None, :], sm_bias,
                    mh_norm_g[l][None, :])

        nr = t // CMP_STRIDE
        wt, wb = _expand_cmp_w1(cmp_w1_k[l], cmp_w1_v[l])
        w2pad = lambda w: jnp.pad(w, ((0, 0), (0, LANES - N_HEAD_DIM))).astype(BF16)
        kc, vct = _compress(aux, b, t, wt, wb, cmp_pos_k[l].reshape(1, -1), cmp_pos_v[l].reshape(1, -1),
                            cmp_w1_k[l].astype(BF16), cmp_w1_v[l].astype(BF16), w2pad(cmp_w2_k[l]), w2pad(cmp_w2_v[l]))

        half = ROPE_DIM // 2
        inv = jnp.power(jnp.float32(ROPE_THETA), -jnp.arange(half, dtype=F32) * (2.0 / ROPE_DIM))
        lih = jnp.arange(LANES) % N_HEAD_DIM
        inv_row = jnp.where(lih < ROPE_DIM, inv[lih % half], 0.0)[None, :].astype(F32)
        pos_col = positions.astype(F32)[:, None, :]
        gate_bias = jnp.zeros((1, LANES), F32).at[0, 8:8 + 3 * N_HEADS].set(b_nsa_gate[l])
        qrawt, qrope, kst, vs, kwt, vw, gates = _kvprep(p, aux, b, t, pos_col, inv_row, gate_bias)

        ci = jnp.arange(nr)[None, :] * CMP_STRIDE
        sj = jnp.arange(LANES)[:, None] * SEL_BLOCK
        overlap_t = ((ci < sj + SEL_BLOCK) & (ci + CMP_BLOCK > sj) & (jnp.arange(nr)[None, :] < nr - 1)
                     & (jnp.arange(LANES)[:, None] < t // SEL_BLOCK)).astype(BF16)
        oc, bias = _select(qrawt, kc, vct, overlap_t)
        yb = _nsa(p, b, t, qrope, bias, oc, kst, vs, kwt, vw, gates)

        x2 = _outproj(ya, yb, p, x2, w_branch_a[l].astype(BF16), w_branch_b[l].astype(BF16),
                      w_out[l].astype(BF16), final_norm_g[None, :], l == norm_g.shape[0] - 1)
    return x2.reshape(b, t, D_MODEL)
```

```python
import functools
import math

import jax
import jax.numpy as jnp
from jax import lax
from jax.experimental import pallas as pl
from jax.experimental.pallas import tpu as pltpu

F32 = jnp.float32
BF16 = jnp.bfloat16

D_MODEL = 1024
M_HEADS = 4
M_HEAD_DIM = 256
M_WIDTH = M_HEADS * M_HEAD_DIM
CONV_WIDTH = 4
N_HEADS = 8
N_KV_GROUPS = 2
N_HPG = N_HEADS // N_KV_GROUPS
N_HEAD_DIM = 64
N_WIDTH = N_HEADS * N_HEAD_DIM
CMP_BLOCK = 32
CMP_STRIDE = 16
CMP_HIDDEN = 128
SEL_BLOCK = 64
SEL_TOPK = 16
WINDOW = 512
Q_BLOCK = 256
ROPE_THETA = 500000.0
ROPE_DIM = N_HEAD_DIM // 4
NORM_EPS = 1e-6
NEG = -1e30
FORCE = 1e9

LANES = 128
SUBLANES = 8
MLSTM_CHUNK = 256
SEL_CHUNK = 512
SEL_UNROLL = 4
VMEM_LIMIT = 56 * 1024 * 1024

C_MQ, C_MK, C_MV, C_MO, C_MZ, C_GA, C_GB = 0, 1024, 2048, 3072, 4096, 5120, 6144
C_NQ, C_NZ = 7168, 7680
C_KV = 8192
C_CMP = 8704
C_SM = 8960
P_W = 9216
AUX_W = 3 * LANES
X_KC, X_VC, X_SM = 0, 1, 2
_O_MI, _O_NQ, _O_KC, _O_KS, _O_NG, _O_NZ, _O_GA, _O_END = 5120, 5128, 5640, 5896, 6408, 6432, 6944, 8992


def _cparams(sem):
    return pltpu.CompilerParams(dimension_semantics=sem, vmem_limit_bytes=VMEM_LIMIT)


def _sigmoid(x):
    return 0.5 * jnp.tanh(0.5 * x) + 0.5


def _silu(x):
    h = 0.5 * x
    return h + h * jnp.tanh(h)


def _log_sigmoid(x):
    return jnp.minimum(x, 0.0) - jnp.log1p(jnp.exp(-jnp.abs(x)))


PROJ_TN = 1024
AUX_TILE = C_CMP // PROJ_TN
AUX_OFF = C_CMP % PROJ_TN


N_HEAD_TILES = _O_MI // PROJ_TN


def _inproj_kernel(x_ref, g_ref, wh_ref, wt_ref, o_ref, aux_ref, hn_ref):
    j = pl.program_id(1)

    @pl.when(j == 0)
    def _():
        x = x_ref[...]
        r = lax.rsqrt(jnp.mean(x * x, axis=-1, keepdims=True) + NORM_EPS)
        hn_ref[...] = (x * r * g_ref[...]).astype(BF16)

    @pl.when(j < N_HEAD_TILES)
    def _():
        o_ref[...] = jnp.dot(hn_ref[...], wh_ref[...], preferred_element_type=F32).astype(BF16)

    @pl.when((j >= N_HEAD_TILES) & (j != AUX_TILE))
    def _():
        o_ref[...] = jnp.dot(hn_ref[...], wt_ref[...], preferred_element_type=F32).astype(BF16)

    @pl.when(j == AUX_TILE)
    def _():
        acc = jnp.dot(hn_ref[...], wt_ref[...], preferred_element_type=F32)
        o_ref[...] = acc.astype(BF16)
        aux_ref[...] = acc[:, AUX_OFF:AUX_OFF + AUX_W]


def _inproj(x2, norm_g, w_head, w_tail):
    m = x2.shape[0]
    tm, tn = 1024, PROJ_TN
    return pl.pallas_call(
        _inproj_kernel,
        out_shape=(jax.ShapeDtypeStruct((m, P_W), BF16), jax.ShapeDtypeStruct((m, AUX_W), F32)),
        grid=(m // tm, P_W // tn),
        in_specs=[
            pl.BlockSpec((tm, D_MODEL), lambda i, j: (i, 0)),
            pl.BlockSpec((1, D_MODEL), lambda i, j: (0, 0)),
            pl.BlockSpec((D_MODEL, tn), lambda i, j: (0, jnp.minimum(j, N_HEAD_TILES - 1))),
            pl.BlockSpec((D_MODEL, tn), lambda i, j: (0, jnp.maximum(j - N_HEAD_TILES, 0))),
        ],
        out_specs=(pl.BlockSpec((tm, tn), lambda i, j: (i, j)), pl.BlockSpec((tm, AUX_W), lambda i, j: (i, 0))),
        scratch_shapes=[pltpu.VMEM((tm, D_MODEL), BF16)],
        compiler_params=_cparams(("parallel", "arbitrary")),
        name="inproj",
    )(x2, norm_g, w_head, w_tail)


def _causal_conv_silu(u, prev_ref, w_ref, b_ref):
    n, w = u.shape
    u3 = u.reshape(n // SUBLANES, SUBLANES, w)
    sub = lax.broadcasted_iota(jnp.int32, u3.shape, 1)
    prev3 = prev_ref[...].reshape(1, SUBLANES, w)
    y = b_ref[...] + u * w_ref[CONV_WIDTH - 1:CONV_WIDTH, :]
    for d in range(1, CONV_WIDTH):
        r = pltpu.roll(u3, d, axis=1)
        above = jnp.concatenate([pltpu.roll(prev3, d, axis=1), r[:-1]], axis=0)
        y = y + jnp.where(sub < d, above, r).reshape(n, w) * w_ref[CONV_WIDTH - 1 - d:CONV_WIDTH - d, :]
    prev_ref[...] = u[n - SUBLANES:n, :]
    return _silu(y)


def _mlstm_kernel(q_ref, k_ref, v_ref, og_ref, z_ref, sm_ref, cqw_ref, cqb_ref, ckw_ref, ckb_ref,
                  sb_ref, ng_ref, ya_ref, qprev, kprev, c_state, n_state, m_state):
    L = q_ref.shape[0]
    d = M_HEAD_DIM

    @pl.when(pl.program_id(1) == 0)
    def _():
        qprev[...] = jnp.zeros_like(qprev)
        kprev[...] = jnp.zeros_like(kprev)
        c_state[...] = jnp.zeros_like(c_state)
        n_state[...] = jnp.zeros_like(n_state)
        m_state[...] = jnp.zeros_like(m_state)

    mq = _causal_conv_silu(q_ref[...].astype(F32), qprev, cqw_ref, cqb_ref)
    mk = _causal_conv_silu(k_ref[...].astype(F32), kprev, ckw_ref, ckb_ref) * (d ** -0.5)

    sm = sm_ref[...] + sb_ref[...]
    lf = _log_sigmoid(sm)
    row = lax.broadcasted_iota(jnp.int32, (L, L), 0)
    col = lax.broadcasted_iota(jnp.int32, (L, L), 1)
    causal = row >= col
    tril = causal.astype(F32)
    triu = (row <= col).astype(F32)
    a_col = jnp.dot(tril, lf, precision=lax.Precision.HIGHEST, preferred_element_type=F32)
    sm_t = sm.T[0:8, :]
    a_row = jnp.dot(_log_sigmoid(sm_t), triu, precision=lax.Precision.HIGHEST,
                    preferred_element_type=F32)

    for h in range(M_HEADS):
        hs = slice(h * d, (h + 1) * d)
        a_c = a_col[:, 4 + h:5 + h]
        li_c = sm[:, h:h + 1]
        a_r = a_row[4 + h:5 + h, :]
        li_r = sm_t[h:h + 1, :]
        m_prev = m_state[h:h + 1, 0:1]
        a_end = a_c[L - 1:L, :]

        log_d = jnp.where(causal, a_c - a_r + li_r, -jnp.inf)
        inter = a_c + m_prev
        m_t = jnp.maximum(inter, jnp.max(log_d, axis=1, keepdims=True))
        w_inter = jnp.exp(inter - m_t)
        dm = jnp.exp(log_d - m_t)

        qh = mq[:, hs]
        kh = mk[:, hs]
        qb = qh.astype(BF16)
        kb = kh.astype(BF16)
        vb = v_ref[:, hs]
        s = lax.dot_general(qb, kb, (((1,), (1,)), ((), ())), preferred_element_type=F32) * dm
        c_old = c_state[h]
        n_old = n_state[h:h + 1, :]
        num = (w_inter * jnp.dot(qb, c_old.astype(BF16), preferred_element_type=F32)
               + jnp.dot(s.astype(BF16), vb, preferred_element_type=F32))
        den = (w_inter * jnp.sum(qh * n_old, axis=1, keepdims=True)
               + jnp.sum(s, axis=1, keepdims=True))
        hh = num * (1.0 / jnp.maximum(jnp.abs(den), jnp.exp(-m_t)))

        logw = a_end - a_c + li_c
        m_new = jnp.maximum(a_end + m_prev, jnp.max(logw, axis=0, keepdims=True))
        wk = jnp.exp(logw - m_new)
        decay = jnp.exp(a_end + m_prev - m_new)
        kw = kh * wk
        c_state[h] = decay * c_old + lax.dot_general(kw.astype(BF16), vb, (((0,), (0,)), ((), ())),
                                                     preferred_element_type=F32)
        n_state[h:h + 1, :] = decay * n_old + jnp.sum(kw, axis=0, keepdims=True)
        m_state[h:h + 1, :] = jnp.broadcast_to(m_new, (1, LANES))

        mu = jnp.mean(hh, axis=1, keepdims=True)
        dc = hh - mu
        var = jnp.mean(dc * dc, axis=1, keepdims=True)
        hn = dc * lax.rsqrt(var + NORM_EPS) * ng_ref[:, hs]
        ya = _sigmoid(og_ref[:, hs].astype(F32)) * hn * _silu(z_ref[:, hs].astype(F32))
        ya_ref[:, hs] = ya.astype(BF16)


def _mlstm(p, aux, b, t, conv_q_w, conv_q_b, conv_k_w, conv_k_b, sm_bias, mh_norm_g):
    L = MLSTM_CHUNK
    nc = t // L

    def colspec(cstart):
        return pl.BlockSpec((L, M_WIDTH), lambda bi, c: (bi * nc + c, cstart // M_WIDTH))

    def full(shape):
        return pl.BlockSpec(shape, lambda bi, c: (0,) * len(shape))

    return pl.pallas_call(
        _mlstm_kernel,
        out_shape=jax.ShapeDtypeStruct((b * t, M_WIDTH), BF16),
        grid=(b, nc),
        in_specs=[
            colspec(C_MQ), colspec(C_MK), colspec(C_MV), colspec(C_MO), colspec(C_MZ),
            pl.BlockSpec((L, LANES), lambda bi, c: (bi * nc + c, X_SM)),
            full((CONV_WIDTH, M_WIDTH)), full((1, M_WIDTH)), full((CONV_WIDTH, M_WIDTH)), full((1, M_WIDTH)),
            full((1, LANES)), full((1, M_WIDTH)),
        ],
        out_specs=pl.BlockSpec((L, M_WIDTH), lambda bi, c: (bi * nc + c, 0)),
        scratch_shapes=[
            pltpu.VMEM((SUBLANES, M_WIDTH), F32), pltpu.VMEM((SUBLANES, M_WIDTH), F32),
            pltpu.VMEM((M_HEADS, M_HEAD_DIM, M_HEAD_DIM), F32),
            pltpu.VMEM((8, M_HEAD_DIM), F32), pltpu.VMEM((8, LANES), F32),
        ],
        compiler_params=_cparams(("parallel", "arbitrary")),
        name="mlstm",
    )(p, p, p, p, p, aux, conv_q_w, conv_q_b, conv_k_w, conv_k_b, sm_bias, mh_norm_g)


def _compress_kernel(pk_ref, pv_ref, wt_ref, wb_ref, posk_ref, posv_ref, w1k_ref, w1v_ref, w2k_ref, w2v_ref,
                     kc_ref, vct_ref):
    nr = pk_ref.shape[0] // CMP_STRIDE
    cw = 2 * LANES
    u = jnp.zeros((nr, wt_ref.shape[1]), F32)
    v = jnp.zeros((nr, wb_ref.shape[1]), F32)
    for l in range(CMP_STRIDE):
        x = jnp.concatenate([pk_ref[pl.ds(l, nr, stride=CMP_STRIDE), :],
                             pv_ref[pl.ds(l, nr, stride=CMP_STRIDE), :]], axis=1).astype(BF16)
        u = u + jnp.dot(x, wt_ref[l * cw:(l + 1) * cw, :], preferred_element_type=F32)
        v = v + jnp.dot(x, wb_ref[l * cw:(l + 1) * cw, :], preferred_element_type=F32)
    v = pltpu.roll(v, nr - 1, axis=0)
    rowi = lax.broadcasted_iota(jnp.int32, v.shape, 0)
    v = jnp.where(rowi < nr - 1, v, 0.0)

    def pos_term(pos_ref, w_ref):
        pb = jnp.broadcast_to(pos_ref[...].astype(BF16), (8, CMP_BLOCK * N_HEAD_DIM))
        return jnp.dot(pb, w_ref[...], preferred_element_type=F32)[0:1, :]

    pk = pos_term(posk_ref, w1k_ref)
    pv = pos_term(posv_ref, w1v_ref)
    hid = u + v + jnp.concatenate([pk, pk, pv, pv], axis=1)
    act = 0.5 * hid * (1.0 + jnp.tanh(math.sqrt(2.0 / math.pi) * (hid + 0.044715 * (hid * hid * hid))))
    for c in range(2 * N_KV_GROUPS):
        w2 = w2k_ref[...] if c < N_KV_GROUPS else w2v_ref[...]
        out = jnp.dot(act[:, c * CMP_HIDDEN:(c + 1) * CMP_HIDDEN].astype(BF16), w2,
                      preferred_element_type=F32)
        if c < N_KV_GROUPS:
            kc_ref[c] = out.astype(BF16)
        else:
            vct_ref[c - N_KV_GROUPS] = out.T.astype(BF16)


def _compress(aux, b, t, wt, wb, posk, posv, w1k, w1v, w2k, w2v):
    nr = t // CMP_STRIDE
    g = N_KV_GROUPS

    def full(a):
        return pl.BlockSpec(a.shape, lambda bi: (0,) * a.ndim)

    return pl.pallas_call(
        _compress_kernel,
        out_shape=(jax.ShapeDtypeStruct((b, g, nr, LANES), BF16), jax.ShapeDtypeStruct((b, g, LANES, nr), BF16)),
        grid=(b,),
        in_specs=[pl.BlockSpec((t, LANES), lambda bi: (bi, X_KC)),
                  pl.BlockSpec((t, LANES), lambda bi: (bi, X_VC)),
                  full(wt), full(wb), full(posk), full(posv), full(w1k), full(w1v), full(w2k), full(w2v)],
        out_specs=(pl.BlockSpec((None, g, nr, LANES), lambda bi: (bi, 0, 0, 0)),
                   pl.BlockSpec((None, g, LANES, nr), lambda bi: (bi, 0, 0, 0))),
        compiler_params=_cparams(("parallel",)),
        name="compress",
    )(aux, aux, wt, wb, posk, posv, w1k, w1v, w2k, w2v)


def _kvprep_kernel(q_ref, kv_ref, sm_ref, pos_ref, inv_ref, gb_ref,
                   qrawt_ref, qrope_ref, kst_ref, vs_ref, kwt_ref, vw_ref, gates_ref):
    tk = q_ref.shape[0]
    lane = lax.broadcasted_iota(jnp.int32, (tk, LANES), 1)
    lih = lane % N_HEAD_DIM
    pos = jnp.broadcast_to(pos_ref[...], (LANES, tk)).T
    ang = pos * inv_ref[...]
    cos = jnp.cos(ang)
    sin = jnp.sin(ang)
    half = ROPE_DIM // 2
    s_up = jnp.where(lih < half, -sin, 0.0)
    s_dn = jnp.where((lih >= half) & (lih < ROPE_DIM), sin, 0.0)

    def rope(u):
        return u * cos + pltpu.roll(u, LANES - half, axis=1) * s_up + pltpu.roll(u, half, axis=1) * s_dn

    def head_pad(u, odd):
        if odd:
            u = pltpu.roll(u, N_HEAD_DIM, axis=1)
        return jnp.where(lane < N_HEAD_DIM, u, 0.0)

    scale = N_HEAD_DIM ** -0.5 * math.log2(math.e)
    zeros_t = jnp.zeros((N_HEAD_DIM, tk), BF16)
    for g in range(N_KV_GROUPS):
        for hp in range(N_HPG // 2):
            c0 = g * (N_HPG * N_HEAD_DIM) + hp * LANES
            u = q_ref[:, c0:c0 + LANES].astype(F32) * scale
            ur = rope(u)
            u_t = u.T.astype(BF16)
            for odd in range(2):
                h = 2 * hp + odd
                qrawt_ref[g, h * LANES:h * LANES + N_HEAD_DIM, :] = u_t[odd * N_HEAD_DIM:(odd + 1) * N_HEAD_DIM, :]
                qrawt_ref[g, h * LANES + N_HEAD_DIM:(h + 1) * LANES, :] = zeros_t
                qrope_ref[g, :, h * LANES:(h + 1) * LANES] = head_pad(ur, odd).astype(BF16)

    ks_t = rope(kv_ref[:, 0:LANES].astype(F32)).T
    kw_t = rope(kv_ref[:, 2 * LANES:3 * LANES].astype(F32)).T
    vs = kv_ref[:, LANES:2 * LANES].astype(F32)
    vw = kv_ref[:, 3 * LANES:4 * LANES].astype(F32)
    t_glob = pl.program_id(1) * tk + lax.broadcasted_iota(jnp.int32, (LANES, tk), 1)
    blk_row = lax.broadcasted_iota(jnp.int32, (LANES, tk), 0)
    onehot = jnp.where(blk_row == t_glob // SEL_BLOCK, 1.0, 0.0).astype(BF16)
    ones_lane = jnp.where(lane == N_HEAD_DIM, 1.0, 0.0)
    gates = _sigmoid(sm_ref[...] + gb_ref[...])
    for g in range(N_KV_GROUPS):
        rs = slice(g * N_HEAD_DIM, (g + 1) * N_HEAD_DIM)
        kst_ref[g, 0:LANES, :] = onehot
        kst_ref[g, LANES:LANES + N_HEAD_DIM, :] = ks_t[rs, :].astype(BF16)
        kst_ref[g, LANES + N_HEAD_DIM:2 * LANES, :] = zeros_t
        kwt_ref[g, 0:N_HEAD_DIM, :] = kw_t[rs, :].astype(BF16)
        kwt_ref[g, N_HEAD_DIM:LANES, :] = zeros_t
        vs_ref[g] = (head_pad(vs, g) + ones_lane).astype(BF16)
        vw_ref[g] = (head_pad(vw, g) + ones_lane).astype(BF16)
        goff = 8 + g * (3 * N_HPG)
        gates_ref[g] = pltpu.roll(gates, LANES - goff, axis=1)


def _kvprep(p, aux, b, t, pos_col, inv_row, gate_bias):
    tk = 512
    nt = t // tk
    g = N_KV_GROUPS
    qw = N_HPG * LANES

    def tok(shape):
        return pl.BlockSpec((None, g) + shape, lambda bi, i: (bi, 0, i, 0))

    def tok_t(rows):
        return pl.BlockSpec((None, g, rows, tk), lambda bi, i: (bi, 0, 0, i))

    return pl.pallas_call(
        _kvprep_kernel,
        out_shape=(
            jax.ShapeDtypeStruct((b, g, qw, t), BF16), jax.ShapeDtypeStruct((b, g, t, qw), BF16),
            jax.ShapeDtypeStruct((b, g, 2 * LANES, t), BF16), jax.ShapeDtypeStruct((b, g, t, LANES), BF16),
            jax.ShapeDtypeStruct((b, g, LANES, t), BF16), jax.ShapeDtypeStruct((b, g, t, LANES), BF16),
            jax.ShapeDtypeStruct((b, g, t, LANES), F32),
        ),
        grid=(b, nt),
        in_specs=[
            pl.BlockSpec((tk, N_WIDTH), lambda bi, i: (bi * nt + i, C_NQ // N_WIDTH)),
            pl.BlockSpec((tk, 4 * LANES), lambda bi, i: (bi * nt + i, C_KV // (4 * LANES))),
            pl.BlockSpec((tk, LANES), lambda bi, i: (bi * nt + i, X_SM)),
            pl.BlockSpec((None, 1, tk), lambda bi, i: (bi, 0, i)),
            pl.BlockSpec((1, LANES), lambda bi, i: (0, 0)),
            pl.BlockSpec((1, LANES), lambda bi, i: (0, 0)),
        ],
        out_specs=(tok_t(qw), tok((tk, qw)), tok_t(2 * LANES), tok((tk, LANES)),
                   tok_t(LANES), tok((tk, LANES)), tok((tk, LANES))),
        compiler_params=_cparams(("parallel", "arbitrary")),
        name="kvprep",
    )(p, p, aux, pos_col, inv_row, gate_bias)


CMP_ROW_CHUNK = 128


def _select_kernel(qt_ref, kc_ref, vct_ref, ovt_ref, oc_ref, bias_ref, imp_ref):
    qs = qt_ref.shape[1]
    nr = kc_ref.shape[0]
    t0 = pl.program_id(2) * qs
    sees_any = t0 + lax.broadcasted_iota(jnp.int32, (1, qs), 1) >= CMP_BLOCK - 1

    def cmp_branch(rows):
        m_c = (lax.broadcasted_iota(jnp.int32, (rows, qs), 0) * CMP_STRIDE + (CMP_BLOCK - 1)
               <= t0 + lax.broadcasted_iota(jnp.int32, (rows, qs), 1))
        kc = kc_ref[0:rows, :]
        vct = vct_ref[:, 0:rows]
        psum = jnp.zeros((rows, qs), F32)
        for h in range(N_HPG):
            s = jnp.dot(kc, qt_ref[h * LANES:(h + 1) * LANES, :], preferred_element_type=F32)
            s = jnp.where(m_c, s, NEG)
            e = jnp.exp2(s - jnp.max(s, axis=0, keepdims=True))
            den = jnp.sum(e, axis=0, keepdims=True)
            pc = e * jnp.where(sees_any, 1.0 / den, 0.0)
            psum = psum + pc
            oc_t = jnp.dot(vct, pc.astype(BF16), preferred_element_type=F32)
            oc_ref[:, h * LANES:(h + 1) * LANES] = oc_t.T
        p_hi = psum.astype(BF16)
        p_lo = (psum - p_hi.astype(F32)).astype(BF16)
        ovt = ovt_ref[:, 0:rows]
        imp_ref[...] = (jnp.dot(ovt, p_hi, preferred_element_type=F32)
                        + jnp.dot(ovt, p_lo, preferred_element_type=F32))

    n_var = nr // CMP_ROW_CHUNK
    visible = (t0 + qs - CMP_BLOCK) // CMP_STRIDE + 1
    need = jnp.clip((visible + CMP_ROW_CHUNK - 1) // CMP_ROW_CHUNK, 1, n_var)
    for v in range(1, n_var + 1):
        pl.when(need == v)(functools.partial(cmp_branch, v * CMP_ROW_CHUNK))
    imp = imp_ref[...]

    blk = lax.broadcasted_iota(jnp.int32, (LANES, qs), 0)
    tq = t0 + lax.broadcasted_iota(jnp.int32, (LANES, qs), 1)
    cur = tq // SEL_BLOCK
    forced = (blk == 0) | (blk == cur) | (blk == cur - 1)
    n_sel = (nr * CMP_STRIDE) // SEL_BLOCK
    score = jnp.where(forced, FORCE, jnp.where(blk * SEL_BLOCK <= tq, imp, -1.0))
    score = jnp.where(blk < n_sel, score, -jnp.inf)
    blk_f = blk.astype(F32)
    for _ in range(min(SEL_TOPK, n_sel)):
        mx = jnp.max(score, axis=0, keepdims=True)
        idx = jnp.min(jnp.where(score == mx, blk_f, float(LANES)), axis=0, keepdims=True)
        score = jnp.where(blk_f == idx, -jnp.inf, score)
    bias_t = jnp.where((score == -jnp.inf) & (blk < n_sel), 0.0, NEG)
    bias_ref[...] = bias_t.T.astype(BF16)


def _select(qrawt, kc, vct, ovt):
    b, g, qw, t = qrawt.shape
    nr = kc.shape[2]
    qs = 512

    def whole(rows, cols):
        return pl.BlockSpec((None, None, rows, cols), lambda bi, gi, i: (bi, gi, 0, 0))

    return pl.pallas_call(
        _select_kernel,
        out_shape=(jax.ShapeDtypeStruct((b, g, t, qw), F32), jax.ShapeDtypeStruct((b, g, t, LANES), BF16)),
        grid=(b, g, t // qs),
        in_specs=[
            pl.BlockSpec((None, None, qw, qs), lambda bi, gi, i: (bi, gi, 0, i)),
            whole(nr, LANES), whole(LANES, nr),
            pl.BlockSpec(ovt.shape, lambda bi, gi, i: (0, 0)),
        ],
        out_specs=(pl.BlockSpec((None, None, qs, qw), lambda bi, gi, i: (bi, gi, i, 0)),
                   pl.BlockSpec((None, None, qs, LANES), lambda bi, gi, i: (bi, gi, i, 0))),
        scratch_shapes=[pltpu.VMEM((LANES, qs), F32)],
        compiler_params=_cparams(("parallel", "parallel", "parallel")),
        name="select",
    )(qrawt, kc, vct, ovt)


def _nsa_kernel(qrope_ref, bias_ref, oc_ref, kst_ref, vs_ref, kwt_ref, vw_ref, gates_ref, nz_ref,
                yb_ref, qaug_ref, acc_ref, m_ref, s_ref):
    i = pl.program_id(1)
    t0 = i * Q_BLOCK
    Q = Q_BLOCK
    nh = N_HPG
    ng = N_KV_GROUPS
    rows = nh * Q
    t0a = pl.multiple_of(t0, Q)
    rq = lax.broadcasted_iota(jnp.int32, (rows, Q), 0) % Q
    ck = lax.broadcasted_iota(jnp.int32, (rows, Q), 1)
    lane = lax.broadcasted_iota(jnp.int32, (Q, LANES), 1)

    def q_rope(g):
        return jnp.concatenate([qrope_ref[g, :, h * LANES:(h + 1) * LANES] for h in range(nh)], axis=0)

    first_blk = t0 // SEL_BLOCK
    for g in range(ng):
        qr = q_rope(g)
        bias = bias_ref[g]
        bias_early = jnp.where(lane < first_blk, bias, NEG).astype(BF16)
        qaug_ref[g, :, 0:LANES] = jnp.concatenate([bias_early] * nh, axis=0)
        qaug_ref[g, :, LANES:2 * LANES] = qr
        q_diag = jnp.concatenate([jnp.concatenate([bias] * nh, axis=0), qr], axis=1)
        s_d = jnp.dot(q_diag, kst_ref[g, :, pl.ds(t0a, Q)], preferred_element_type=F32)
        s_d = jnp.where(ck <= rq, s_d, NEG)
        m0 = jnp.max(s_d, axis=1, keepdims=True)
        p_d = jnp.exp2(s_d - m0)
        m_ref[g] = jnp.broadcast_to(m0, (rows, LANES))
        acc_ref[g] = jnp.dot(p_d.astype(BF16), vs_ref[g, pl.ds(t0a, Q), :], preferred_element_type=F32)

    ncb = SEL_CHUNK // LANES
    n_chunks = (t0 + SEL_CHUNK - 1) // SEL_CHUNK
    last = jnp.maximum(n_chunks - 1, 0)

    def scores(g, c, slot):
        k0 = pl.multiple_of(c * SEL_CHUNK, SEL_CHUNK)
        s_ref[slot, g] = jnp.dot(qaug_ref[g], kst_ref[g, :, pl.ds(k0, SEL_CHUNK)], preferred_element_type=F32)

    def consume(g, c, slot):
        k0 = pl.multiple_of(c * SEL_CHUNK, SEL_CHUNK)
        s = s_ref[slot, g]
        m_old = m_ref[g]
        m_new = jnp.maximum(m_old, jnp.max(s, axis=1, keepdims=True))
        p = jnp.concatenate([jnp.exp2(s[:, cb * LANES:(cb + 1) * LANES] - m_new) for cb in range(ncb)], axis=1)
        acc_ref[g] = (jnp.exp2(m_old - m_new) * acc_ref[g]
                      + jnp.dot(p.astype(BF16), vs_ref[g, pl.ds(k0, SEL_CHUNK), :], preferred_element_type=F32))
        m_ref[g] = m_new

    for g in range(ng):
        scores(g, 0, 0)

    def steps(c, n):
        for u in range(n):
            for g in range(ng):
                scores(g, jnp.minimum(c + u + 1, last), (u + 1) % 2)
                consume(g, c + u, u % 2)

    def sel_quad(c4, carry):
        steps(SEL_UNROLL * c4, SEL_UNROLL)
        return carry

    n_quads = n_chunks // SEL_UNROLL
    lax.fori_loop(0, n_quads, sel_quad, 0)
    rem = n_chunks - SEL_UNROLL * n_quads

    @pl.when(rem >= 2)
    def _():
        steps(SEL_UNROLL * n_quads, 2)

    @pl.when(rem % 2 == 1)
    def _():
        for g in range(ng):
            consume(g, n_chunks - 1, 0)

    wlen = WINDOW + Q
    ws = pl.multiple_of(jnp.maximum(t0 - WINDOW, 0), Q)
    tq_w = t0 + lax.broadcasted_iota(jnp.int32, (rows, wlen), 0) % Q
    diff = tq_w - (ws + lax.broadcasted_iota(jnp.int32, (rows, wlen), 1))
    m_w = (diff >= 0) & (diff < WINDOW)
    for g in range(ng):
        acc = acc_ref[g]
        o_s = acc * (1.0 / acc[:, N_HEAD_DIM:N_HEAD_DIM + 1])
        s_w = jnp.dot(q_rope(g), kwt_ref[g, :, pl.ds(ws, wlen)], preferred_element_type=F32)
        s_w = jnp.where(m_w, s_w, NEG)
        p_w = jnp.exp2(s_w - jnp.max(s_w, axis=1, keepdims=True))
        acc_w = jnp.dot(p_w.astype(BF16), vw_ref[g, pl.ds(ws, wlen), :], preferred_element_type=F32)
        o_w = acc_w * (1.0 / acc_w[:, N_HEAD_DIM:N_HEAD_DIM + 1])
        gates = gates_ref[g]
        outs = []
        for h in range(nh):
            rs = slice(h * Q, (h + 1) * Q)
            outs.append(gates[:, 3 * h:3 * h + 1] * oc_ref[g, :, h * LANES:(h + 1) * LANES]
                        + gates[:, 3 * h + 1:3 * h + 2] * o_s[rs, :]
                        + gates[:, 3 * h + 2:3 * h + 3] * o_w[rs, :])
        for hp in range(nh // 2):
            yn = jnp.where(lane < N_HEAD_DIM, outs[2 * hp], pltpu.roll(outs[2 * hp + 1], N_HEAD_DIM, axis=1))
            ls = slice((g * (nh // 2) + hp) * LANES, (g * (nh // 2) + hp + 1) * LANES)
            yb_ref[:, ls] = (yn * _silu(nz_ref[:, ls].astype(F32))).astype(BF16)


def _nsa(p, b, t, qrope, bias, oc, kst, vs, kwt, vw, gates):
    g = N_KV_GROUPS
    nqb = t // Q_BLOCK
    qw = N_HPG * LANES
    rows = N_HPG * Q_BLOCK

    def qblk(w):
        return pl.BlockSpec((None, g, Q_BLOCK, w), lambda bi, i: (bi, 0, i, 0))

    def whole(r, c):
        return pl.BlockSpec((None, g, r, c), lambda bi, i: (bi, 0, 0, 0), pipeline_mode=pl.Buffered(1))

    return pl.pallas_call(
        _nsa_kernel,
        out_shape=jax.ShapeDtypeStruct((b * t, N_WIDTH), BF16),
        grid=(b, nqb),
        in_specs=[
            qblk(qw), qblk(LANES), qblk(qw),
            whole(2 * LANES, t), whole(t, LANES), whole(LANES, t), whole(t, LANES),
            qblk(LANES),
            pl.BlockSpec((Q_BLOCK, N_WIDTH), lambda bi, i: (bi * nqb + i, C_NZ // N_WIDTH)),
        ],
        out_specs=pl.BlockSpec((Q_BLOCK, N_WIDTH), lambda bi, i: (bi * nqb + i, 0)),
        scratch_shapes=[pltpu.VMEM((g, rows, 2 * LANES), BF16), pltpu.VMEM((g, rows, LANES), F32),
                        pltpu.VMEM((g, rows, LANES), F32), pltpu.VMEM((2, g, rows, SEL_CHUNK), F32)],
        compiler_params=_cparams(("parallel", "arbitrary")),
        name="nsa",
    )(qrope, bias, oc, kst, vs, kwt, vw, gates, p)


def _outproj_kernel(final, ya_ref, yb_ref, ga_ref, gb_ref, x_ref, wa_ref, wb_ref, wo_ref, fg_ref, o_ref):
    a = jnp.dot(ya_ref[...], wa_ref[...], preferred_element_type=F32)
    bq = jnp.dot(yb_ref[...], wb_ref[...], preferred_element_type=F32)
    merged = _sigmoid(ga_ref[...].astype(F32)) * a + _sigmoid(gb_ref[...].astype(F32)) * bq
    y = x_ref[...] + jnp.dot(merged.astype(BF16), wo_ref[...], preferred_element_type=F32)
    if final:
        y = y * lax.rsqrt(jnp.mean(y * y, axis=-1, keepdims=True) + NORM_EPS) * fg_ref[...]
    o_ref[...] = y


def _outproj(ya, yb, p, x2, wa, wb, wo, fg, final):
    m = x2.shape[0]
    tm = 512

    def full(a):
        return pl.BlockSpec(a.shape, lambda i: (0, 0))

    return pl.pallas_call(
        functools.partial(_outproj_kernel, final),
        out_shape=jax.ShapeDtypeStruct((m, D_MODEL), F32),
        grid=(m // tm,),
        in_specs=[
            pl.BlockSpec((tm, M_WIDTH), lambda i: (i, 0)),
            pl.BlockSpec((tm, N_WIDTH), lambda i: (i, 0)),
            pl.BlockSpec((tm, D_MODEL), lambda i: (i, C_GA // D_MODEL)),
            pl.BlockSpec((tm, D_MODEL), lambda i: (i, C_GB // D_MODEL)),
            pl.BlockSpec((tm, D_MODEL), lambda i: (i, 0)),
            full(wa), full(wb), full(wo), full(fg),
        ],
        out_specs=pl.BlockSpec((tm, D_MODEL), lambda i: (i, 0)),
        compiler_params=_cparams(("parallel",)),
        name="outproj",
    )(ya, yb, p, p, x2, wa, wb, wo, fg)


def _permute_w_tail(w):
    pad = jnp.zeros((w.shape[0], P_W - _O_END), w.dtype)
    return jnp.concatenate([
        w[:, _O_GA:_O_END],
        w[:, _O_NQ:_O_KC],
        w[:, _O_NZ:_O_GA],
        w[:, _O_KS:_O_NG],
        w[:, _O_KC:_O_KS],
        w[:, _O_MI:_O_NQ],
        w[:, _O_NG:_O_NZ],
        pad], axis=1)


def _expand_cmp_w1(w1k, w1v):
    half = CMP_BLOCK // 2
    k3 = w1k.astype(BF16).reshape(CMP_BLOCK, 1, N_HEAD_DIM, 1, CMP_HIDDEN)
    v3 = w1v.astype(BF16).reshape(CMP_BLOCK, 1, N_HEAD_DIM, 1, CMP_HIDDEN)
    nc = 2 * N_KV_GROUPS
    eye = jnp.eye(nc, dtype=BF16)
    is_k = (jnp.arange(nc) < N_KV_GROUPS).astype(BF16)
    mk = (eye * is_k[:, None]).reshape(1, nc, 1, nc, 1)
    mv = (eye * (1 - is_k)[:, None]).reshape(1, nc, 1, nc, 1)
    z = k3 * mk + v3 * mv
    z = z.reshape(CMP_BLOCK * nc * N_HEAD_DIM, nc * CMP_HIDDEN)
    return z[:half * nc * N_HEAD_DIM], z[half * nc * N_HEAD_DIM:]


def kernel(x, positions, norm_g, w_in, conv_q_w, conv_q_b, conv_k_w, conv_k_b, b_igate, b_fgate, mh_norm_g,
           cmp_pos_k, cmp_w1_k, cmp_w2_k, cmp_pos_v, cmp_w1_v, cmp_w2_v, b_nsa_gate, w_branch_a, w_branch_b,
           w_out, final_norm_g):
    b, t, _ = x.shape
    assert t % (2 * SEL_CHUNK) == 0 and t >= WINDOW + Q_BLOCK and t // SEL_BLOCK <= LANES and t % MLSTM_CHUNK == 0
    x2 = x.reshape(b * t, D_MODEL)
    for l in range(norm_g.shape[0]):
        w_bf = w_in[l].astype(BF16)
        p, aux = _inproj(x2, norm_g[l][None, :], w_bf, _permute_w_tail(w_bf))

        sm_bias = jnp.zeros((1, LANES), F32).at[0, 0:M_HEADS].set(b_igate[l]).at[0, M_HEADS:2 * M_HEADS].set(b_fgate[l])
        ya = _mlstm(p, aux, b, t, conv_q_w[l], conv_q_b[l][None, :], conv_k_w[l], conv_k_b[l][None, :], sm_bias,
                    mh_norm_g[l][None, :])

        nr = t // CMP_STRIDE
        wt, wb = _expand_cmp_w1(cmp_w1_k[l], cmp_w1_v[l])
        w2pad = lambda w: jnp.pad(w, ((0, 0), (0, LANES - N_HEAD_DIM))).astype(BF16)
        kc, vct = _compress(aux, b, t, wt, wb, cmp_pos_k[l].reshape(1, -1), cmp_pos_v[l].reshape(1, -1),
                            cmp_w1_k[l].astype(BF16), cmp_w1_v[l].astype(BF16), w2pad(cmp_w2_k[l]), w2pad(cmp_w2_v[l]))

        half = ROPE_DIM // 2
        inv = jnp.power(jnp.float32(ROPE_THETA), -jnp.arange(half, dtype=F32) * (2.0 / ROPE_DIM))
        lih = jnp.arange(LANES) % N_HEAD_DIM
        inv_row = jnp.where(lih < ROPE_DIM, inv[lih % half], 0.0)[None, :].astype(F32)
        pos_col = positions.astype(F32)[:, None, :]
        gate_bias = jnp.zeros((1, LANES), F32).at[0, 8:8 + 3 * N_HEADS].set(b_nsa_gate[l])
        qrawt, qrope, kst, vs, kwt, vw, gates = _kvprep(p, aux, b, t, pos_col, inv_row, gate_bias)

        ci = jnp.arange(nr)[None, :] * CMP_STRIDE
        sj = jnp.arange(LANES)[:, None] * SEL_BLOCK
        overlap_t = ((ci < sj + SEL_BLOCK) & (ci + CMP_BLOCK > sj) & (jnp.arange(nr)[None, :] < nr - 1)
                     & (jnp.arange(LANES)[:, None] < t // SEL_BLOCK)).astype(BF16)
        oc, bias = _select(qrawt, kc, vct, overlap_t)
        yb = _nsa(p, b, t, qrope, bias, oc, kst, vs, kwt, vw, gates)

        x2 = _outproj(ya, yb, p, x2, w_branch_a[l].astype(BF16), w_branch_b[l].astype(BF16),
                      w_out[l].astype(BF16), final_norm_g[None, :], l == norm_g.shape[0] - 1)
    return x2.reshape(b, t, D_MODEL)
```

```python
import functools
import math

import jax
import jax.numpy as jnp
from jax import lax
from jax.experimental import pallas as pl
from jax.experimental.pallas import tpu as pltpu

F32 = jnp.float32
BF16 = jnp.bfloat16

D_MODEL = 1024
M_HEADS = 4
M_HEAD_DIM = 256
M_WIDTH = M_HEADS * M_HEAD_DIM
CONV_WIDTH = 4
N_HEADS = 8
N_KV_GROUPS = 2
N_HPG = N_HEADS // N_KV_GROUPS
N_HEAD_DIM = 64
N_WIDTH = N_HEADS * N_HEAD_DIM
CMP_BLOCK = 32
CMP_STRIDE = 16
CMP_HIDDEN = 128
SEL_BLOCK = 64
SEL_TOPK = 16
WINDOW = 512
Q_BLOCK = 256
ROPE_THETA = 500000.0
ROPE_DIM = N_HEAD_DIM // 4
NORM_EPS = 1e-6
NEG = -1e30
FORCE = 1e9

LANES = 128
SUBLANES = 8
MLSTM_CHUNK = 256
SEL_CHUNK = 512
SEL_UNROLL = 4
VMEM_LIMIT = 56 * 1024 * 1024

C_MQ, C_MK, C_MV, C_MO, C_MZ, C_GA, C_GB = 0, 1024, 2048, 3072, 4096, 5120, 6144
C_NQ, C_NZ = 7168, 7680
C_KV = 8192
C_CMP = 8704
C_SM = 8960
P_W = 9216
AUX_W = 3 * LANES
X_KC, X_VC, X_SM = 0, 1, 2
_O_MI, _O_NQ, _O_KC, _O_KS, _O_NG, _O_NZ, _O_GA, _O_END = 5120, 5128, 5640, 5896, 6408, 6432, 6944, 8992


def _cparams(sem):
    return pltpu.CompilerParams(dimension_semantics=sem, vmem_limit_bytes=VMEM_LIMIT)


def _sigmoid(x):
    return 0.5 * jnp.tanh(0.5 * x) + 0.5


def _silu(x):
    h = 0.5 * x
    return h + h * jnp.tanh(h)


def _log_sigmoid(x):
    return jnp.minimum(x, 0.0) - jnp.log1p(jnp.exp(-jnp.abs(x)))


PROJ_TN = 1024
AUX_TILE = C_CMP // PROJ_TN
AUX_OFF = C_CMP % PROJ_TN


N_HEAD_TILES = _O_MI // PROJ_TN


def _inproj_kernel(x_ref, g_ref, wh_ref, wt_ref, o_ref, aux_ref, hn_ref):
    j = pl.program_id(1)

    @pl.when(j == 0)
    def _():
        x = x_ref[...]
        r = lax.rsqrt(jnp.mean(x * x, axis=-1, keepdims=True) + NORM_EPS)
        hn_ref[...] = (x * r * g_ref[...]).astype(BF16)

    @pl.when(j < N_HEAD_TILES)
    def _():
        o_ref[...] = jnp.dot(hn_ref[...], wh_ref[...], preferred_element_type=F32).astype(BF16)

    @pl.when((j >= N_HEAD_TILES) & (j != AUX_TILE))
    def _():
        o_ref[...] = jnp.dot(hn_ref[...], wt_ref[...], preferred_element_type=F32).astype(BF16)

    @pl.when(j == AUX_TILE)
    def _():
        acc = jnp.dot(hn_ref[...], wt_ref[...], preferred_element_type=F32)
        o_ref[...] = acc.astype(BF16)
        aux_ref[...] = acc[:, AUX_OFF:AUX_OFF + AUX_W]


def _inproj(x2, norm_g, w_head, w_tail):
    m = x2.shape[0]
    tm, tn = 1024, PROJ_TN
    return pl.pallas_call(
        _inproj_kernel,
        out_shape=(jax.ShapeDtypeStruct((m, P_W), BF16), jax.ShapeDtypeStruct((m, AUX_W), F32)),
        grid=(m // tm, P_W // tn),
        in_specs=[
            pl.BlockSpec((tm, D_MODEL), lambda i, j: (i, 0)),
            pl.BlockSpec((1, D_MODEL), lambda i, j: (0, 0)),
            pl.BlockSpec((D_MODEL, tn), lambda i, j: (0, jnp.minimum(j, N_HEAD_TILES - 1))),
            pl.BlockSpec((D_MODEL, tn), lambda i, j: (0, jnp.maximum(j - N_HEAD_TILES, 0))),
        ],
        out_specs=(pl.BlockSpec((tm, tn), lambda i, j: (i, j)), pl.BlockSpec((tm, AUX_W), lambda i, j: (i, 0))),
        scratch_shapes=[pltpu.VMEM((tm, D_MODEL), BF16)],
        compiler_params=_cparams(("parallel", "arbitrary")),
        name="inproj",
    )(x2, norm_g, w_head, w_tail)


def _causal_conv_silu(u, prev_ref, w_ref, b_ref):
    n, w = u.shape
    u3 = u.reshape(n // SUBLANES, SUBLANES, w)
    sub = lax.broadcasted_iota(jnp.int32, u3.shape, 1)
    prev3 = prev_ref[...].reshape(1, SUBLANES, w)
    y = b_ref[...] + u * w_ref[CONV_WIDTH - 1:CONV_WIDTH, :]
    for d in range(1, CONV_WIDTH):
        r = pltpu.roll(u3, d, axis=1)
        above = jnp.concatenate([pltpu.roll(prev3, d, axis=1), r[:-1]], axis=0)
        y = y + jnp.where(sub < d, above, r).reshape(n, w) * w_ref[CONV_WIDTH - 1 - d:CONV_WIDTH - d, :]
    prev_ref[...] = u[n - SUBLANES:n, :]
    return _silu(y)


def _mlstm_kernel(q_ref, k_ref, v_ref, og_ref, z_ref, sm_ref, cqw_ref, cqb_ref, ckw_ref, ckb_ref,
                  sb_ref, ng_ref, ya_ref, qprev, kprev, c_state, n_state, m_state):
    L = q_ref.shape[0]
    d = M_HEAD_DIM

    @pl.when(pl.program_id(1) == 0)
    def _():
        qprev[...] = jnp.zeros_like(qprev)
        kprev[...] = jnp.zeros_like(kprev)
        c_state[...] = jnp.zeros_like(c_state)
        n_state[...] = jnp.zeros_like(n_state)
        m_state[...] = jnp.zeros_like(m_state)

    mq = _causal_conv_silu(q_ref[...].astype(F32), qprev, cqw_ref, cqb_ref)
    mk = _causal_conv_silu(k_ref[...].astype(F32), kprev, ckw_ref, ckb_ref) * (d ** -0.5)

    sm = sm_ref[...] + sb_ref[...]
    lf = _log_sigmoid(sm)
    row = lax.broadcasted_iota(jnp.int32, (L, L), 0)
    col = lax.broadcasted_iota(jnp.int32, (L, L), 1)
    causal = row >= col
    tril = causal.astype(F32)
    triu = (row <= col).astype(F32)
    a_col = jnp.dot(tril, lf, precision=lax.Precision.HIGHEST, preferred_element_type=F32)
    sm_t = sm.T[0:8, :]
    a_row = jnp.dot(_log_sigmoid(sm_t), triu, precision=lax.Precision.HIGHEST,
                    preferred_element_type=F32)

    for h in range(M_HEADS):
        hs = slice(h * d, (h + 1) * d)
        a_c = a_col[:, 4 + h:5 + h]
        li_c = sm[:, h:h + 1]
        a_r = a_row[4 + h:5 + h, :]
        li_r = sm_t[h:h + 1, :]
        m_prev = m_state[h:h + 1, 0:1]
        a_end = a_c[L - 1:L, :]

        log_d = jnp.where(causal, a_c - a_r + li_r, -jnp.inf)
        inter = a_c + m_prev
        m_t = jnp.maximum(inter, jnp.max(log_d, axis=1, keepdims=True))
        w_inter = jnp.exp(inter - m_t)
        dm = jnp.exp(log_d - m_t)

        qh = mq[:, hs]
        kh = mk[:, hs]
        qb = qh.astype(BF16)
        kb = kh.astype(BF16)
        vb = v_ref[:, hs]
        s = lax.dot_general(qb, kb, (((1,), (1,)), ((), ())), preferred_element_type=F32) * dm
        c_old = c_state[h]
        n_old = n_state[h:h + 1, :]
        num = (w_inter * jnp.dot(qb, c_old.astype(BF16), preferred_element_type=F32)
               + jnp.dot(s.astype(BF16), vb, preferred_element_type=F32))
        den = (w_inter * jnp.sum(qh * n_old, axis=1, keepdims=True)
               + jnp.sum(s, axis=1, keepdims=True))
        hh = num * (1.0 / jnp.maximum(jnp.abs(den), jnp.exp(-m_t)))

        logw = a_end - a_c + li_c
        m_new = jnp.maximum(a_end + m_prev, jnp.max(logw, axis=0, keepdims=True))
        wk = jnp.exp(logw - m_new)
        decay = jnp.exp(a_end + m_prev - m_new)
        kw = kh * wk
        c_state[h] = decay * c_old + lax.dot_general(kw.astype(BF16), vb, (((0,), (0,)), ((), ())),
                                                     preferred_element_type=F32)
        n_state[h:h + 1, :] = decay * n_old + jnp.sum(kw, axis=0, keepdims=True)
        m_state[h:h + 1, :] = jnp.broadcast_to(m_new, (1, LANES))

        mu = jnp.mean(hh, axis=1, keepdims=True)
        dc = hh - mu
        var = jnp.mean(dc * dc, axis=1, keepdims=True)
        hn = dc * lax.rsqrt(var + NORM_EPS) * ng_ref[:, hs]
        ya = _sigmoid(og_ref[:, hs].astype(F32)) * hn * _silu(z_ref[:, hs].astype(F32))
        ya_ref[:, hs] = ya.astype(BF16)


def _mlstm(p, aux, b, t, conv_q_w, conv_q_b, conv_k_w, conv_k_b, sm_bias, mh_norm_g):
    L = MLSTM_CHUNK
    nc = t // L

    def colspec(cstart):
        return pl.BlockSpec((L, M_WIDTH), lambda bi, c: (bi * nc + c, cstart // M_WIDTH))

    def full(shape):
        return pl.BlockSpec(shape, lambda bi, c: (0,) * len(shape))

    return pl.pallas_call(
        _mlstm_kernel,
        out_shape=jax.ShapeDtypeStruct((b * t, M_WIDTH), BF16),
        grid=(b, nc),
        in_specs=[
            colspec(C_MQ), colspec(C_MK), colspec(C_MV), colspec(C_MO), colspec(C_MZ),
            pl.BlockSpec((L, LANES), lambda bi, c: (bi * nc + c, X_SM)),
            full((CONV_WIDTH, M_WIDTH)), full((1, M_WIDTH)), full((CONV_WIDTH, M_WIDTH)), full((1, M_WIDTH)),
            full((1, LANES)), full((1, M_WIDTH)),
        ],
        out_specs=pl.BlockSpec((L, M_WIDTH), lambda bi, c: (bi * nc + c, 0)),
        scratch_shapes=[
            pltpu.VMEM((SUBLANES, M_WIDTH), F32), pltpu.VMEM((SUBLANES, M_WIDTH), F32),
            pltpu.VMEM((M_HEADS, M_HEAD_DIM, M_HEAD_DIM), F32),
            pltpu.VMEM((8, M_HEAD_DIM), F32), pltpu.VMEM((8, LANES), F32),
        ],
        compiler_params=_cparams(("parallel", "arbitrary")),
        name="mlstm",
    )(p, p, p, p, p, aux, conv_q_w, conv_q_b, conv_k_w, conv_k_b, sm_bias, mh_norm_g)


def _compress_kernel(pk_ref, pv_ref, wt_ref, wb_ref, posk_ref, posv_ref, w1k_ref, w1v_ref, w2k_ref, w2v_ref,
                     kc_ref, vct_ref):
    nr = pk_ref.shape[0] // CMP_STRIDE
    cw = 2 * LANES
    u = jnp.zeros((nr, wt_ref.shape[1]), F32)
    v = jnp.zeros((nr, wb_ref.shape[1]), F32)
    for l in range(CMP_STRIDE):
        x = jnp.concatenate([pk_ref[pl.ds(l, nr, stride=CMP_STRIDE), :],
                             pv_ref[pl.ds(l, nr, stride=CMP_STRIDE), :]], axis=1).astype(BF16)
        u = u + jnp.dot(x, wt_ref[l * cw:(l + 1) * cw, :], preferred_element_type=F32)
        v = v + jnp.dot(x, wb_ref[l * cw:(l + 1) * cw, :], preferred_element_type=F32)
    v = pltpu.roll(v, nr - 1, axis=0)
    rowi = lax.broadcasted_iota(jnp.int32, v.shape, 0)
    v = jnp.where(rowi < nr - 1, v, 0.0)

    def pos_term(pos_ref, w_ref):
        pb = jnp.broadcast_to(pos_ref[...].astype(BF16), (8, CMP_BLOCK * N_HEAD_DIM))
        return jnp.dot(pb, w_ref[...], preferred_element_type=F32)[0:1, :]

    pk = pos_term(posk_ref, w1k_ref)
    pv = pos_term(posv_ref, w1v_ref)
    hid = u + v + jnp.concatenate([pk, pk, pv, pv], axis=1)
    act = 0.5 * hid * (1.0 + jnp.tanh(math.sqrt(2.0 / math.pi) * (hid + 0.044715 * (hid * hid * hid))))
    for c in range(2 * N_KV_GROUPS):
        w2 = w2k_ref[...] if c < N_KV_GROUPS else w2v_ref[...]
        out = jnp.dot(act[:, c * CMP_HIDDEN:(c + 1) * CMP_HIDDEN].astype(BF16), w2,
                      preferred_element_type=F32)
        if c < N_KV_GROUPS:
            kc_ref[c] = out.astype(BF16)
        else:
            vct_ref[c - N_KV_GROUPS] = out.T.astype(BF16)


def _compress(aux, b, t, wt, wb, posk, posv, w1k, w1v, w2k, w2v):
    nr = t // CMP_STRIDE
    g = N_KV_GROUPS

    def full(a):
        return pl.BlockSpec(a.shape, lambda bi: (0,) * a.ndim)

    return pl.pallas_call(
        _compress_kernel,
        out_shape=(jax.ShapeDtypeStruct((b, g, nr, LANES), BF16), jax.ShapeDtypeStruct((b, g, LANES, nr), BF16)),
        grid=(b,),
        in_specs=[pl.BlockSpec((t, LANES), lambda bi: (bi, X_KC)),
                  pl.BlockSpec((t, LANES), lambda bi: (bi, X_VC)),
                  full(wt), full(wb), full(posk), full(posv), full(w1k), full(w1v), full(w2k), full(w2v)],
        out_specs=(pl.BlockSpec((None, g, nr, LANES), lambda bi: (bi, 0, 0, 0)),
                   pl.BlockSpec((None, g, LANES, nr), lambda bi: (bi, 0, 0, 0))),
        compiler_params=_cparams(("parallel",)),
        name="compress",
    )(aux, aux, wt, wb, posk, posv, w1k, w1v, w2k, w2v)


def _kvprep_kernel(q_ref, kv_ref, sm_ref, pos_ref, inv_ref, gb_ref,
                   qrawt_ref, qrope_ref, kst_ref, vs_ref, kwt_ref, vw_ref, gates_ref, gatest_ref):
    tk = q_ref.shape[0]
    lane = lax.broadcasted_iota(jnp.int32, (tk, LANES), 1)
    lih = lane % N_HEAD_DIM
    pos = jnp.broadcast_to(pos_ref[...], (LANES, tk)).T
    ang = pos * inv_ref[...]
    cos = jnp.cos(ang)
    sin = jnp.sin(ang)
    half = ROPE_DIM // 2
    s_up = jnp.where(lih < half, -sin, 0.0)
    s_dn = jnp.where((lih >= half) & (lih < ROPE_DIM), sin, 0.0)

    def rope(u):
        return u * cos + pltpu.roll(u, LANES - half, axis=1) * s_up + pltpu.roll(u, half, axis=1) * s_dn

    def head_pad(u, odd):
        if odd:
            u = pltpu.roll(u, N_HEAD_DIM, axis=1)
        return jnp.where(lane < N_HEAD_DIM, u, 0.0)

    scale = N_HEAD_DIM ** -0.5 * math.log2(math.e)
    zeros_t = jnp.zeros((N_HEAD_DIM, tk), BF16)
    for g in range(N_KV_GROUPS):
        for hp in range(N_HPG // 2):
            c0 = g * (N_HPG * N_HEAD_DIM) + hp * LANES
            u = q_ref[:, c0:c0 + LANES].astype(F32) * scale
            ur = rope(u)
            u_t = u.T.astype(BF16)
            for odd in range(2):
                h = 2 * hp + odd
                qrawt_ref[g, h * LANES:h * LANES + N_HEAD_DIM, :] = u_t[odd * N_HEAD_DIM:(odd + 1) * N_HEAD_DIM, :]
                qrawt_ref[g, h * LANES + N_HEAD_DIM:(h + 1) * LANES, :] = zeros_t
                qrope_ref[g, :, h * LANES:(h + 1) * LANES] = head_pad(ur, odd).astype(BF16)

    ks_t = rope(kv_ref[:, 0:LANES].astype(F32)).T
    kw_t = rope(kv_ref[:, 2 * LANES:3 * LANES].astype(F32)).T
    vs = kv_ref[:, LANES:2 * LANES].astype(F32)
    vw = kv_ref[:, 3 * LANES:4 * LANES].astype(F32)
    t_glob = pl.program_id(1) * tk + lax.broadcasted_iota(jnp.int32, (LANES, tk), 1)
    blk_row = lax.broadcasted_iota(jnp.int32, (LANES, tk), 0)
    onehot = jnp.where(blk_row == t_glob // SEL_BLOCK, 1.0, 0.0).astype(BF16)
    ones_lane = jnp.where(lane == N_HEAD_DIM, 1.0, 0.0)
    gates = _sigmoid(sm_ref[...] + gb_ref[...])
    for g in range(N_KV_GROUPS):
        rs = slice(g * N_HEAD_DIM, (g + 1) * N_HEAD_DIM)
        kst_ref[g, 0:LANES, :] = onehot
        kst_ref[g, LANES:LANES + N_HEAD_DIM, :] = ks_t[rs, :].astype(BF16)
        kst_ref[g, LANES + N_HEAD_DIM:2 * LANES, :] = zeros_t
        kwt_ref[g, 0:N_HEAD_DIM, :] = kw_t[rs, :].astype(BF16)
        kwt_ref[g, N_HEAD_DIM:LANES, :] = zeros_t
        vs_ref[g] = (head_pad(vs, g) + ones_lane).astype(BF16)
        vw_ref[g] = (head_pad(vw, g) + ones_lane).astype(BF16)
        goff = 8 + g * (3 * N_HPG)
        gg = pltpu.roll(gates, LANES - goff, axis=1)
        gates_ref[g] = gg
        gatest_ref[g] = gg.T


def _kvprep(p, aux, b, t, pos_col, inv_row, gate_bias):
    tk = 512
    nt = t // tk
    g = N_KV_GROUPS
    qw = N_HPG * LANES

    def tok(shape):
        return pl.BlockSpec((None, g) + shape, lambda bi, i: (bi, 0, i, 0))

    def tok_t(rows):
        return pl.BlockSpec((None, g, rows, tk), lambda bi, i: (bi, 0, 0, i))

    return pl.pallas_call(
        _kvprep_kernel,
        out_shape=(
            jax.ShapeDtypeStruct((b, g, qw, t), BF16), jax.ShapeDtypeStruct((b, g, t, qw), BF16),
            jax.ShapeDtypeStruct((b, g, 2 * LANES, t), BF16), jax.ShapeDtypeStruct((b, g, t, LANES), BF16),
            jax.ShapeDtypeStruct((b, g, LANES, t), BF16), jax.ShapeDtypeStruct((b, g, t, LANES), BF16),
            jax.ShapeDtypeStruct((b, g, t, LANES), F32), jax.ShapeDtypeStruct((b, g, LANES, t), F32),
        ),
        grid=(b, nt),
        in_specs=[
            pl.BlockSpec((tk, N_WIDTH), lambda bi, i: (bi * nt + i, C_NQ // N_WIDTH)),
            pl.BlockSpec((tk, 4 * LANES), lambda bi, i: (bi * nt + i, C_KV // (4 * LANES))),
            pl.BlockSpec((tk, LANES), lambda bi, i: (bi * nt + i, X_SM)),
            pl.BlockSpec((None, 1, tk), lambda bi, i: (bi, 0, i)),
            pl.BlockSpec((1, LANES), lambda bi, i: (0, 0)),
            pl.BlockSpec((1, LANES), lambda bi, i: (0, 0)),
        ],
        out_specs=(tok_t(qw), tok((tk, qw)), tok_t(2 * LANES), tok((tk, LANES)),
                   tok_t(LANES), tok((tk, LANES)), tok((tk, LANES)), tok_t(LANES)),
        compiler_params=_cparams(("parallel", "arbitrary")),
        name="kvprep",
    )(p, p, aux, pos_col, inv_row, gate_bias)


CMP_ROW_CHUNK = 128


def _select_kernel(qt_ref, kc_ref, vct_ref, ovt_ref, gt_ref, oc_ref, bias_ref, imp_ref):
    qs = qt_ref.shape[1]
    nr = kc_ref.shape[0]
    t0 = pl.program_id(2) * qs
    sees_any = t0 + lax.broadcasted_iota(jnp.int32, (1, qs), 1) >= CMP_BLOCK - 1

    def cmp_branch(rows):
        m_c = (lax.broadcasted_iota(jnp.int32, (rows, qs), 0) * CMP_STRIDE + (CMP_BLOCK - 1)
               <= t0 + lax.broadcasted_iota(jnp.int32, (rows, qs), 1))
        kc = kc_ref[0:rows, :]
        vct = vct_ref[:, 0:rows]
        psum = jnp.zeros((rows, qs), F32)
        for h in range(N_HPG):
            s = jnp.dot(kc, qt_ref[h * LANES:(h + 1) * LANES, :], preferred_element_type=F32)
            s = jnp.where(m_c, s, NEG)
            e = jnp.exp2(s - jnp.max(s, axis=0, keepdims=True))
            den = jnp.sum(e, axis=0, keepdims=True)
            pc = e * jnp.where(sees_any, 1.0 / den, 0.0)
            psum = psum + pc
            oc_t = jnp.dot(vct, pc.astype(BF16), preferred_element_type=F32)
            oc_ref[:, h * LANES:(h + 1) * LANES] = (oc_t * gt_ref[3 * h:3 * h + 1, :]).T
        p_hi = psum.astype(BF16)
        p_lo = (psum - p_hi.astype(F32)).astype(BF16)
        ovt = ovt_ref[:, 0:rows]
        imp_ref[...] = (jnp.dot(ovt, p_hi, preferred_element_type=F32)
                        + jnp.dot(ovt, p_lo, preferred_element_type=F32))

    n_var = nr // CMP_ROW_CHUNK
    visible = (t0 + qs - CMP_BLOCK) // CMP_STRIDE + 1
    need = jnp.clip((visible + CMP_ROW_CHUNK - 1) // CMP_ROW_CHUNK, 1, n_var)
    for v in range(1, n_var + 1):
        pl.when(need == v)(functools.partial(cmp_branch, v * CMP_ROW_CHUNK))
    imp = imp_ref[...]

    blk = lax.broadcasted_iota(jnp.int32, (LANES, qs), 0)
    tq = t0 + lax.broadcasted_iota(jnp.int32, (LANES, qs), 1)
    cur = tq // SEL_BLOCK
    forced = (blk == 0) | (blk == cur) | (blk == cur - 1)
    n_sel = (nr * CMP_STRIDE) // SEL_BLOCK
    score = jnp.where(forced, FORCE, jnp.where(blk * SEL_BLOCK <= tq, imp, -1.0))
    score = jnp.where(blk < n_sel, score, -jnp.inf)
    blk_f = blk.astype(F32)
    for _ in range(min(SEL_TOPK, n_sel)):
        mx = jnp.max(score, axis=0, keepdims=True)
        idx = jnp.min(jnp.where(score == mx, blk_f, float(LANES)), axis=0, keepdims=True)
        score = jnp.where(blk_f == idx, -jnp.inf, score)
    bias_t = jnp.where((score == -jnp.inf) & (blk < n_sel), 0.0, NEG)
    bias_ref[...] = bias_t.T.astype(BF16)


def _select(qrawt, kc, vct, ovt, gates_t):
    b, g, qw, t = qrawt.shape
    nr = kc.shape[2]
    qs = 512

    def whole(rows, cols):
        return pl.BlockSpec((None, None, rows, cols), lambda bi, gi, i: (bi, gi, 0, 0))

    return pl.pallas_call(
        _select_kernel,
        out_shape=(jax.ShapeDtypeStruct((b, g, t, qw), F32), jax.ShapeDtypeStruct((b, g, t, LANES), BF16)),
        grid=(b, g, t // qs),
        in_specs=[
            pl.BlockSpec((None, None, qw, qs), lambda bi, gi, i: (bi, gi, 0, i)),
            whole(nr, LANES), whole(LANES, nr),
            pl.BlockSpec(ovt.shape, lambda bi, gi, i: (0, 0)),
            pl.BlockSpec((None, None, LANES, qs), lambda bi, gi, i: (bi, gi, 0, i)),
        ],
        out_specs=(pl.BlockSpec((None, None, qs, qw), lambda bi, gi, i: (bi, gi, i, 0)),
                   pl.BlockSpec((None, None, qs, LANES), lambda bi, gi, i: (bi, gi, i, 0))),
        scratch_shapes=[pltpu.VMEM((LANES, qs), F32)],
        compiler_params=_cparams(("parallel", "parallel", "parallel")),
        name="select",
    )(qrawt, kc, vct, ovt, gates_t)


def _nsa_kernel(qrope_ref, bias_ref, oc_ref, kst_ref, vs_ref, kwt_ref, vw_ref, gates_ref, nz_ref,
                yb_ref, qaug_ref, acc_ref, m_ref, s_ref):
    i = pl.program_id(1)
    t0 = i * Q_BLOCK
    Q = Q_BLOCK
    nh = N_HPG
    ng = N_KV_GROUPS
    rows = nh * Q
    t0a = pl.multiple_of(t0, Q)
    causal = (lax.broadcasted_iota(jnp.int32, (Q, Q), 1) <= lax.broadcasted_iota(jnp.int32, (Q, Q), 0))
    lane = lax.broadcasted_iota(jnp.int32, (Q, LANES), 1)

    def q_rope(g):
        return jnp.concatenate([qrope_ref[g, :, h * LANES:(h + 1) * LANES] for h in range(nh)], axis=0)

    def mask_heads(s, m):
        return jnp.concatenate([jnp.where(m, s[h * Q:(h + 1) * Q, :], NEG) for h in range(nh)], axis=0)

    first_blk = t0 // SEL_BLOCK
    for g in range(ng):
        qr = q_rope(g)
        bias = bias_ref[g]
        bias_early = jnp.where(lane < first_blk, bias, NEG).astype(BF16)
        qaug_ref[g, :, 0:LANES] = jnp.concatenate([bias_early] * nh, axis=0)
        qaug_ref[g, :, LANES:2 * LANES] = qr
        q_diag = jnp.concatenate([jnp.concatenate([bias] * nh, axis=0), qr], axis=1)
        s_d = jnp.dot(q_diag, kst_ref[g, :, pl.ds(t0a, Q)], preferred_element_type=F32)
        s_d = mask_heads(s_d, causal)
        m0 = jnp.max(s_d, axis=1, keepdims=True)
        p_d = jnp.exp2(s_d - m0)
        m_ref[g] = jnp.broadcast_to(m0, (rows, LANES))
        acc_ref[g] = jnp.dot(p_d.astype(BF16), vs_ref[g, pl.ds(t0a, Q), :], preferred_element_type=F32)

    ncb = SEL_CHUNK // LANES
    n_chunks = (t0 + SEL_CHUNK - 1) // SEL_CHUNK
    last = jnp.maximum(n_chunks - 1, 0)

    def scores(g, c, slot):
        k0 = pl.multiple_of(c * SEL_CHUNK, SEL_CHUNK)
        s_ref[slot, g] = jnp.dot(qaug_ref[g], kst_ref[g, :, pl.ds(k0, SEL_CHUNK)], preferred_element_type=F32)

    def consume(g, c, slot):
        k0 = pl.multiple_of(c * SEL_CHUNK, SEL_CHUNK)
        s = s_ref[slot, g]
        m_old = m_ref[g]
        m_new = jnp.maximum(m_old, jnp.max(s, axis=1, keepdims=True))
        p = jnp.concatenate([jnp.exp2(s[:, cb * LANES:(cb + 1) * LANES] - m_new) for cb in range(ncb)], axis=1)
        acc_ref[g] = (jnp.exp2(m_old - m_new) * acc_ref[g]
                      + jnp.dot(p.astype(BF16), vs_ref[g, pl.ds(k0, SEL_CHUNK), :], preferred_element_type=F32))
        m_ref[g] = m_new

    for g in range(ng):
        scores(g, 0, 0)

    def steps(c, n):
        for u in range(n):
            for g in range(ng):
                scores(g, jnp.minimum(c + u + 1, last), (u + 1) % 2)
                consume(g, c + u, u % 2)

    def sel_quad(c4, carry):
        steps(SEL_UNROLL * c4, SEL_UNROLL)
        return carry

    n_quads = n_chunks // SEL_UNROLL
    lax.fori_loop(0, n_quads, sel_quad, 0)
    rem = n_chunks - SEL_UNROLL * n_quads

    @pl.when(rem >= 2)
    def _():
        steps(SEL_UNROLL * n_quads, 2)

    @pl.when(rem % 2 == 1)
    def _():
        for g in range(ng):
            consume(g, n_chunks - 1, 0)

    wlen = WINDOW + Q
    ws = pl.multiple_of(jnp.maximum(t0 - WINDOW, 0), Q)
    diff = (t0 + lax.broadcasted_iota(jnp.int32, (Q, wlen), 0)) - (ws + lax.broadcasted_iota(jnp.int32, (Q, wlen), 1))
    m_w = (diff >= 0) & (diff < WINDOW)
    for g in range(ng):
        gates = gates_ref[g]

        def gate_col(br):
            return jnp.concatenate([gates[:, 3 * h + br:3 * h + br + 1] for h in range(nh)], axis=0)

        acc = acc_ref[g]
        o_s = acc * (gate_col(1) / acc[:, N_HEAD_DIM:N_HEAD_DIM + 1])
        s_w = jnp.dot(q_rope(g), kwt_ref[g, :, pl.ds(ws, wlen)], preferred_element_type=F32)
        s_w = mask_heads(s_w, m_w)
        p_w = jnp.exp2(s_w - jnp.max(s_w, axis=1, keepdims=True))
        acc_w = jnp.dot(p_w.astype(BF16), vw_ref[g, pl.ds(ws, wlen), :], preferred_element_type=F32)
        o_w = acc_w * (gate_col(2) / acc_w[:, N_HEAD_DIM:N_HEAD_DIM + 1])
        outs = []
        for h in range(nh):
            rs = slice(h * Q, (h + 1) * Q)
            outs.append(oc_ref[g, :, h * LANES:(h + 1) * LANES] + o_s[rs, :] + o_w[rs, :])
        for hp in range(nh // 2):
            yn = jnp.where(lane < N_HEAD_DIM, outs[2 * hp], pltpu.roll(outs[2 * hp + 1], N_HEAD_DIM, axis=1))
            ls = slice((g * (nh // 2) + hp) * LANES, (g * (nh // 2) + hp + 1) * LANES)
            yb_ref[:, ls] = (yn * _silu(nz_ref[:, ls].astype(F32))).astype(BF16)


def _nsa(p, b, t, qrope, bias, oc, kst, vs, kwt, vw, gates):
    g = N_KV_GROUPS
    nqb = t // Q_BLOCK
    qw = N_HPG * LANES
    rows = N_HPG * Q_BLOCK

    def qblk(w):
        return pl.BlockSpec((None, g, Q_BLOCK, w), lambda bi, i: (bi, 0, i, 0))

    def whole(r, c):
        return pl.BlockSpec((None, g, r, c), lambda bi, i: (bi, 0, 0, 0), pipeline_mode=pl.Buffered(1))

    return pl.pallas_call(
        _nsa_kernel,
        out_shape=jax.ShapeDtypeStruct((b * t, N_WIDTH), BF16),
        grid=(b, nqb),
        in_specs=[
            qblk(qw), qblk(LANES), qblk(qw),
            whole(2 * LANES, t), whole(t, LANES), whole(LANES, t), whole(t, LANES),
            qblk(LANES),
            pl.BlockSpec((Q_BLOCK, N_WIDTH), lambda bi, i: (bi * nqb + i, C_NZ // N_WIDTH)),
        ],
        out_specs=pl.BlockSpec((Q_BLOCK, N_WIDTH), lambda bi, i: (bi * nqb + i, 0)),
        scratch_shapes=[pltpu.VMEM((g, rows, 2 * LANES), BF16), pltpu.VMEM((g, rows, LANES), F32),
                        pltpu.VMEM((g, rows, LANES), F32), pltpu.VMEM((2, g, rows, SEL_CHUNK), F32)],
        compiler_params=_cparams(("parallel", "arbitrary")),
        name="nsa",
    )(qrope, bias, oc, kst, vs, kwt, vw, gates, p)


def _outproj_kernel(final, ya_ref, yb_ref, ga_ref, gb_ref, x_ref, wa_ref, wb_ref, wo_ref, fg_ref, o_ref):
    a = jnp.dot(ya_ref[...], wa_ref[...], preferred_element_type=F32)
    bq = jnp.dot(yb_ref[...], wb_ref[...], preferred_element_type=F32)
    merged = _sigmoid(ga_ref[...].astype(F32)) * a + _sigmoid(gb_ref[...].astype(F32)) * bq
    y = x_ref[...] + jnp.dot(merged.astype(BF16), wo_ref[...], preferred_element_type=F32)
    if final:
        y = y * lax.rsqrt(jnp.mean(y * y, axis=-1, keepdims=True) + NORM_EPS) * fg_ref[...]
    o_ref[...] = y


def _outproj(ya, yb, p, x2, wa, wb, wo, fg, final):
    m = x2.shape[0]
    tm = 512

    def full(a):
        return pl.BlockSpec(a.shape, lambda i: (0, 0))

    return pl.pallas_call(
        functools.partial(_outproj_kernel, final),
        out_shape=jax.ShapeDtypeStruct((m, D_MODEL), F32),
        grid=(m // tm,),
        in_specs=[
            pl.BlockSpec((tm, M_WIDTH), lambda i: (i, 0)),
            pl.BlockSpec((tm, N_WIDTH), lambda i: (i, 0)),
            pl.BlockSpec((tm, D_MODEL), lambda i: (i, C_GA // D_MODEL)),
            pl.BlockSpec((tm, D_MODEL), lambda i: (i, C_GB // D_MODEL)),
            pl.BlockSpec((tm, D_MODEL), lambda i: (i, 0)),
            full(wa), full(wb), full(wo), full(fg),
        ],
        out_specs=pl.BlockSpec((tm, D_MODEL), lambda i: (i, 0)),
        compiler_params=_cparams(("parallel",)),
        name="outproj",
    )(ya, yb, p, p, x2, wa, wb, wo, fg)


def _permute_w_tail(w):
    pad = jnp.zeros((w.shape[0], P_W - _O_END), w.dtype)
    return jnp.concatenate([
        w[:, _O_GA:_O_END],
        w[:, _O_NQ:_O_KC],
        w[:, _O_NZ:_O_GA],
        w[:, _O_KS:_O_NG],
        w[:, _O_KC:_O_KS],
        w[:, _O_MI:_O_NQ],
        w[:, _O_NG:_O_NZ],
        pad], axis=1)


def _expand_cmp_w1(w1k, w1v):
    half = CMP_BLOCK // 2
    k3 = w1k.astype(BF16).reshape(CMP_BLOCK, 1, N_HEAD_DIM, 1, CMP_HIDDEN)
    v3 = w1v.astype(BF16).reshape(CMP_BLOCK, 1, N_HEAD_DIM, 1, CMP_HIDDEN)
    nc = 2 * N_KV_GROUPS
    eye = jnp.eye(nc, dtype=BF16)
    is_k = (jnp.arange(nc) < N_KV_GROUPS).astype(BF16)
    mk = (eye * is_k[:, None]).reshape(1, nc, 1, nc, 1)
    mv = (eye * (1 - is_k)[:, None]).reshape(1, nc, 1, nc, 1)
    z = k3 * mk + v3 * mv
    z = z.reshape(CMP_BLOCK * nc * N_HEAD_DIM, nc * CMP_HIDDEN)
    return z[:half * nc * N_HEAD_DIM], z[half * nc * N_HEAD_DIM:]


def kernel(x, positions, norm_g, w_in, conv_q_w, conv_q_b, conv_k_w, conv_k_b, b_igate, b_fgate, mh_norm_g,
           cmp_pos_k, cmp_w1_k, cmp_w2_k, cmp_pos_v, cmp_w1_v, cmp_w2_v, b_nsa_gate, w_branch_a, w_branch_b,
           w_out, final_norm_g):
    b, t, _ = x.shape
    assert t % (2 * SEL_CHUNK) == 0 and t >= WINDOW + Q_BLOCK and t // SEL_BLOCK <= LANES and t % MLSTM_CHUNK == 0
    x2 = x.reshape(b * t, D_MODEL)
    for l in range(norm_g.shape[0]):
        w_bf = w_in[l].astype(BF16)
        p, aux = _inproj(x2, norm_g[l][None, :], w_bf, _permute_w_tail(w_bf))

        sm_bias = jnp.zeros((1, LANES), F32).at[0, 0:M_HEADS].set(b_igate[l]).at[0, M_HEADS:2 * M_HEADS].set(b_fgate[l])
        ya = _mlstm(p, aux, b, t, conv_q_w[l], conv_q_b[l][None, :], conv_k_w[l], conv_k_b[l][None, :], sm_bias,
                    mh_norm_g[l][None, :])

        nr = t // CMP_STRIDE
        wt, wb = _expand_cmp_w1(cmp_w1_k[l], cmp_w1_v[l])
        w2pad = lambda w: jnp.pad(w, ((0, 0), (0, LANES - N_HEAD_DIM))).astype(BF16)
        kc, vct = _compress(aux, b, t, wt, wb, cmp_pos_k[l].reshape(1, -1), cmp_pos_v[l].reshape(1, -1),
                            cmp_w1_k[l].astype(BF16), cmp_w1_v[l].astype(BF16), w2pad(cmp_w2_k[l]), w2pad(cmp_w2_v[l]))

        half = ROPE_DIM // 2
        inv = jnp.power(jnp.float32(ROPE_THETA), -jnp.arange(half, dtype=F32) * (2.0 / ROPE_DIM))
        lih = jnp.arange(LANES) % N_HEAD_DIM
        inv_row = jnp.where(lih < ROPE_DIM, inv[lih % half], 0.0)[None, :].astype(F32)
        pos_col = positions.astype(F32)[:, None, :]
        gate_bias = jnp.zeros((1, LANES), F32).at[0, 8:8 + 3 * N_HEADS].set(b_nsa_gate[l])
        qrawt, qrope, kst, vs, kwt, vw, gates, gates_t = _kvprep(p, aux, b, t, pos_col, inv_row, gate_bias)

        ci = jnp.arange(nr)[None, :] * CMP_STRIDE
        sj = jnp.arange(LANES)[:, None] * SEL_BLOCK
        overlap_t = ((ci < sj + SEL_BLOCK) & (ci + CMP_BLOCK > sj) & (jnp.arange(nr)[None, :] < nr - 1)
                     & (jnp.arange(LANES)[:, None] < t // SEL_BLOCK)).astype(BF16)
        oc, bias = _select(qrawt, kc, vct, overlap_t, gates_t)
        yb = _nsa(p, b, t, qrope, bias, oc, kst, vs, kwt, vw, gates)

        x2 = _outproj(ya, yb, p, x2, w_branch_a[l].astype(BF16), w_branch_b[l].astype(BF16),
                      w_out[l].astype(BF16), final_norm_g[None, :], l == norm_g.shape[0] - 1)
    return x2.reshape(b, t, D_MODEL)
```

```python
import functools
import math

import jax
import jax.numpy as jnp
from jax import lax
from jax.experimental import pallas as pl
from jax.experimental.pallas import tpu as pltpu

F32 = jnp.float32
BF16 = jnp.bfloat16

D_MODEL = 1024
M_HEADS = 4
M_HEAD_DIM = 256
M_WIDTH = M_HEADS * M_HEAD_DIM
CONV_WIDTH = 4
N_HEADS = 8
N_KV_GROUPS = 2
N_HPG = N_HEADS // N_KV_GROUPS
N_HEAD_DIM = 64
N_WIDTH = N_HEADS * N_HEAD_DIM
CMP_BLOCK = 32
CMP_STRIDE = 16
CMP_HIDDEN = 128
SEL_BLOCK = 64
SEL_TOPK = 16
WINDOW = 512
Q_BLOCK = 256
ROPE_THETA = 500000.0
ROPE_DIM = N_HEAD_DIM // 4
NORM_EPS = 1e-6
NEG = -1e30
FORCE = 1e9

LANES = 128
SUBLANES = 8
MLSTM_CHUNK = 256
SEL_CHUNK = 512
SEL_UNROLL = 4
VMEM_LIMIT = 56 * 1024 * 1024

C_MQ, C_MK, C_MV, C_MO, C_MZ, C_GA, C_GB = 0, 1024, 2048, 3072, 4096, 5120, 6144
C_NQ, C_NZ = 7168, 7680
C_KV = 8192
C_CMP = 8704
C_SM = 8960
P_W = 9216
AUX_W = 3 * LANES
X_KC, X_VC, X_SM = 0, 1, 2
_O_MI, _O_NQ, _O_KC, _O_KS, _O_NG, _O_NZ, _O_GA, _O_END = 5120, 5128, 5640, 5896, 6408, 6432, 6944, 8992


def _cparams(sem):
    return pltpu.CompilerParams(dimension_semantics=sem, vmem_limit_bytes=VMEM_LIMIT)


def _sigmoid(x):
    return 0.5 * jnp.tanh(0.5 * x) + 0.5


def _silu(x):
    h = 0.5 * x
    return h + h * jnp.tanh(h)


def _log_sigmoid(x):
    return jnp.minimum(x, 0.0) - jnp.log1p(jnp.exp(-jnp.abs(x)))


PROJ_TN = 1024
AUX_TILE = C_CMP // PROJ_TN
AUX_OFF = C_CMP % PROJ_TN


N_HEAD_TILES = _O_MI // PROJ_TN


def _inproj_kernel(x_ref, g_ref, wh_ref, wt_ref, o_ref, aux_ref, hn_ref):
    j = pl.program_id(1)

    @pl.when(j == 0)
    def _():
        x = x_ref[...]
        r = lax.rsqrt(jnp.mean(x * x, axis=-1, keepdims=True) + NORM_EPS)
        hn_ref[...] = (x * r * g_ref[...]).astype(BF16)

    @pl.when(j < N_HEAD_TILES)
    def _():
        o_ref[...] = jnp.dot(hn_ref[...], wh_ref[...], preferred_element_type=F32).astype(BF16)

    @pl.when((j >= N_HEAD_TILES) & (j != AUX_TILE))
    def _():
        o_ref[...] = jnp.dot(hn_ref[...], wt_ref[...], preferred_element_type=F32).astype(BF16)

    @pl.when(j == AUX_TILE)
    def _():
        acc = jnp.dot(hn_ref[...], wt_ref[...], preferred_element_type=F32)
        o_ref[...] = acc.astype(BF16)
        aux_ref[...] = acc[:, AUX_OFF:AUX_OFF + AUX_W]


def _inproj(x2, norm_g, w_head, w_tail):
    m = x2.shape[0]
    tm, tn = 1024, PROJ_TN
    return pl.pallas_call(
        _inproj_kernel,
        out_shape=(jax.ShapeDtypeStruct((m, P_W), BF16), jax.ShapeDtypeStruct((m, AUX_W), F32)),
        grid=(m // tm, P_W // tn),
        in_specs=[
            pl.BlockSpec((tm, D_MODEL), lambda i, j: (i, 0)),
            pl.BlockSpec((1, D_MODEL), lambda i, j: (0, 0)),
            pl.BlockSpec((D_MODEL, tn), lambda i, j: (0, jnp.minimum(j, N_HEAD_TILES - 1))),
            pl.BlockSpec((D_MODEL, tn), lambda i, j: (0, jnp.maximum(j - N_HEAD_TILES, 0))),
        ],
        out_specs=(pl.BlockSpec((tm, tn), lambda i, j: (i, j)), pl.BlockSpec((tm, AUX_W), lambda i, j: (i, 0))),
        scratch_shapes=[pltpu.VMEM((tm, D_MODEL), BF16)],
        compiler_params=_cparams(("parallel", "arbitrary")),
        name="inproj",
    )(x2, norm_g, w_head, w_tail)


def _causal_conv_silu(u16, prev_ref, w_ref, b_ref):
    n, w = u16.shape
    u = u16.astype(F32)
    row = lax.broadcasted_iota(jnp.int32, (n, n), 0)
    col = lax.broadcasted_iota(jnp.int32, (n, n), 1)
    sub = lax.broadcasted_iota(jnp.int32, (SUBLANES, w), 0)
    prev = prev_ref[...]
    y = b_ref[...] + u * w_ref[CONV_WIDTH - 1:CONV_WIDTH, :]
    for d in range(1, CONV_WIDTH):
        shift = jnp.where(row - col == d, 1.0, 0.0).astype(BF16)
        sh = jnp.dot(shift, u16, preferred_element_type=F32)
        head = jnp.where(sub < d, pltpu.roll(prev, d, axis=0), sh[0:SUBLANES, :])
        sh = jnp.concatenate([head, sh[SUBLANES:, :]], axis=0)
        y = y + sh * w_ref[CONV_WIDTH - 1 - d:CONV_WIDTH - d, :]
    prev_ref[...] = u[n - SUBLANES:n, :]
    return _silu(y)


def _mlstm_kernel(q_ref, k_ref, v_ref, og_ref, z_ref, sm_ref, cqw_ref, cqb_ref, ckw_ref, ckb_ref,
                  sb_ref, ng_ref, ya_ref, qprev, kprev, c_state, n_state, m_state):
    L = q_ref.shape[0]
    d = M_HEAD_DIM

    @pl.when(pl.program_id(1) == 0)
    def _():
        qprev[...] = jnp.zeros_like(qprev)
        kprev[...] = jnp.zeros_like(kprev)
        c_state[...] = jnp.zeros_like(c_state)
        n_state[...] = jnp.zeros_like(n_state)
        m_state[...] = jnp.zeros_like(m_state)

    mq = _causal_conv_silu(q_ref[...], qprev, cqw_ref, cqb_ref)
    mk = _causal_conv_silu(k_ref[...], kprev, ckw_ref, ckb_ref) * (d ** -0.5)

    sm = sm_ref[...] + sb_ref[...]
    lf = _log_sigmoid(sm)
    row = lax.broadcasted_iota(jnp.int32, (L, L), 0)
    col = lax.broadcasted_iota(jnp.int32, (L, L), 1)
    causal = row >= col
    tril = causal.astype(F32)
    triu = (row <= col).astype(F32)
    a_col = jnp.dot(tril, lf, precision=lax.Precision.HIGHEST, preferred_element_type=F32)
    sm_t = sm.T[0:8, :]
    a_row = jnp.dot(_log_sigmoid(sm_t), triu, precision=lax.Precision.HIGHEST,
                    preferred_element_type=F32)

    for h in range(M_HEADS):
        hs = slice(h * d, (h + 1) * d)
        a_c = a_col[:, 4 + h:5 + h]
        li_c = sm[:, h:h + 1]
        a_r = a_row[4 + h:5 + h, :]
        li_r = sm_t[h:h + 1, :]
        m_prev = m_state[h:h + 1, 0:1]
        a_end = a_c[L - 1:L, :]

        log_d = jnp.where(causal, a_c - a_r + li_r, -jnp.inf)
        inter = a_c + m_prev
        m_t = jnp.maximum(inter, jnp.max(log_d, axis=1, keepdims=True))
        w_inter = jnp.exp(inter - m_t)
        dm = jnp.exp(log_d - m_t)

        qh = mq[:, hs]
        kh = mk[:, hs]
        qb = qh.astype(BF16)
        kb = kh.astype(BF16)
        vb = v_ref[:, hs]
        s = lax.dot_general(qb, kb, (((1,), (1,)), ((), ())), preferred_element_type=F32) * dm
        c_old = c_state[h]
        n_old = n_state[h:h + 1, :]
        num = (w_inter * jnp.dot(qb, c_old.astype(BF16), preferred_element_type=F32)
               + jnp.dot(s.astype(BF16), vb, preferred_element_type=F32))
        den = (w_inter * jnp.sum(qh * n_old, axis=1, keepdims=True)
               + jnp.sum(s, axis=1, keepdims=True))
        hh = num * (1.0 / jnp.maximum(jnp.abs(den), jnp.exp(-m_t)))

        logw = a_end - a_c + li_c
        m_new = jnp.maximum(a_end + m_prev, jnp.max(logw, axis=0, keepdims=True))
        wk = jnp.exp(logw - m_new)
        decay = jnp.exp(a_end + m_prev - m_new)
        kw = kh * wk
        c_state[h] = decay * c_old + lax.dot_general(kw.astype(BF16), vb, (((0,), (0,)), ((), ())),
                                                     preferred_element_type=F32)
        n_state[h:h + 1, :] = decay * n_old + jnp.sum(kw, axis=0, keepdims=True)
        m_state[h:h + 1, :] = jnp.broadcast_to(m_new, (1, LANES))

        mu = jnp.mean(hh, axis=1, keepdims=True)
        dc = hh - mu
        var = jnp.mean(dc * dc, axis=1, keepdims=True)
        hn = dc * lax.rsqrt(var + NORM_EPS) * ng_ref[:, hs]
        ya = _sigmoid(og_ref[:, hs].astype(F32)) * hn * _silu(z_ref[:, hs].astype(F32))
        ya_ref[:, hs] = ya.astype(BF16)


def _mlstm(p, aux, b, t, conv_q_w, conv_q_b, conv_k_w, conv_k_b, sm_bias, mh_norm_g):
    L = MLSTM_CHUNK
    nc = t // L

    def colspec(cstart):
        return pl.BlockSpec((L, M_WIDTH), lambda bi, c: (bi * nc + c, cstart // M_WIDTH))

    def full(shape):
        return pl.BlockSpec(shape, lambda bi, c: (0,) * len(shape))

    return pl.pallas_call(
        _mlstm_kernel,
        out_shape=jax.ShapeDtypeStruct((b * t, M_WIDTH), BF16),
        grid=(b, nc),
        in_specs=[
            colspec(C_MQ), colspec(C_MK), colspec(C_MV), colspec(C_MO), colspec(C_MZ),
            pl.BlockSpec((L, LANES), lambda bi, c: (bi * nc + c, X_SM)),
            full((CONV_WIDTH, M_WIDTH)), full((1, M_WIDTH)), full((CONV_WIDTH, M_WIDTH)), full((1, M_WIDTH)),
            full((1, LANES)), full((1, M_WIDTH)),
        ],
        out_specs=pl.BlockSpec((L, M_WIDTH), lambda bi, c: (bi * nc + c, 0)),
        scratch_shapes=[
            pltpu.VMEM((SUBLANES, M_WIDTH), F32), pltpu.VMEM((SUBLANES, M_WIDTH), F32),
            pltpu.VMEM((M_HEADS, M_HEAD_DIM, M_HEAD_DIM), F32),
            pltpu.VMEM((8, M_HEAD_DIM), F32), pltpu.VMEM((8, LANES), F32),
        ],
        compiler_params=_cparams(("parallel", "arbitrary")),
        name="mlstm",
    )(p, p, p, p, p, aux, conv_q_w, conv_q_b, conv_k_w, conv_k_b, sm_bias, mh_norm_g)


def _compress_kernel(pk_ref, pv_ref, wt_ref, wb_ref, posk_ref, posv_ref, w1k_ref, w1v_ref, w2k_ref, w2v_ref,
                     kc_ref, vct_ref):
    nr = pk_ref.shape[0] // CMP_STRIDE
    cw = 2 * LANES
    u = jnp.zeros((nr, wt_ref.shape[1]), F32)
    v = jnp.zeros((nr, wb_ref.shape[1]), F32)
    for l in range(CMP_STRIDE):
        x = jnp.concatenate([pk_ref[pl.ds(l, nr, stride=CMP_STRIDE), :],
                             pv_ref[pl.ds(l, nr, stride=CMP_STRIDE), :]], axis=1).astype(BF16)
        u = u + jnp.dot(x, wt_ref[l * cw:(l + 1) * cw, :], preferred_element_type=F32)
        v = v + jnp.dot(x, wb_ref[l * cw:(l + 1) * cw, :], preferred_element_type=F32)
    v = pltpu.roll(v, nr - 1, axis=0)
    rowi = lax.broadcasted_iota(jnp.int32, v.shape, 0)
    v = jnp.where(rowi < nr - 1, v, 0.0)

    def pos_term(pos_ref, w_ref):
        pb = jnp.broadcast_to(pos_ref[...].astype(BF16), (8, CMP_BLOCK * N_HEAD_DIM))
        return jnp.dot(pb, w_ref[...], preferred_element_type=F32)[0:1, :]

    pk = pos_term(posk_ref, w1k_ref)
    pv = pos_term(posv_ref, w1v_ref)
    hid = u + v + jnp.concatenate([pk, pk, pv, pv], axis=1)
    act = 0.5 * hid * (1.0 + jnp.tanh(math.sqrt(2.0 / math.pi) * (hid + 0.044715 * (hid * hid * hid))))
    for c in range(2 * N_KV_GROUPS):
        w2 = w2k_ref[...] if c < N_KV_GROUPS else w2v_ref[...]
        out = jnp.dot(act[:, c * CMP_HIDDEN:(c + 1) * CMP_HIDDEN].astype(BF16), w2,
                      preferred_element_type=F32)
        if c < N_KV_GROUPS:
            kc_ref[c] = out.astype(BF16)
        else:
            vct_ref[c - N_KV_GROUPS] = out.T.astype(BF16)


def _compress(aux, b, t, wt, wb, posk, posv, w1k, w1v, w2k, w2v):
    nr = t // CMP_STRIDE
    g = N_KV_GROUPS

    def full(a):
        return pl.BlockSpec(a.shape, lambda bi: (0,) * a.ndim)

    return pl.pallas_call(
        _compress_kernel,
        out_shape=(jax.ShapeDtypeStruct((b, g, nr, LANES), BF16), jax.ShapeDtypeStruct((b, g, LANES, nr), BF16)),
        grid=(b,),
        in_specs=[pl.BlockSpec((t, LANES), lambda bi: (bi, X_KC)),
                  pl.BlockSpec((t, LANES), lambda bi: (bi, X_VC)),
                  full(wt), full(wb), full(posk), full(posv), full(w1k), full(w1v), full(w2k), full(w2v)],
        out_specs=(pl.BlockSpec((None, g, nr, LANES), lambda bi: (bi, 0, 0, 0)),
                   pl.BlockSpec((None, g, LANES, nr), lambda bi: (bi, 0, 0, 0))),
        compiler_params=_cparams(("parallel",)),
        name="compress",
    )(aux, aux, wt, wb, posk, posv, w1k, w1v, w2k, w2v)


def _kvprep_kernel(q_ref, kv_ref, sm_ref, pos_ref, inv_ref, gb_ref,
                   qrawt_ref, qrope_ref, kst_ref, vs_ref, kwt_ref, vw_ref, gates_ref, gatest_ref):
    tk = q_ref.shape[0]
    lane = lax.broadcasted_iota(jnp.int32, (tk, LANES), 1)
    lih = lane % N_HEAD_DIM
    pos = jnp.broadcast_to(pos_ref[...], (LANES, tk)).T
    ang = pos * inv_ref[...]
    cos = jnp.cos(ang)
    sin = jnp.sin(ang)
    half = ROPE_DIM // 2
    s_up = jnp.where(lih < half, -sin, 0.0)
    s_dn = jnp.where((lih >= half) & (lih < ROPE_DIM), sin, 0.0)

    def rope(u):
        return u * cos + pltpu.roll(u, LANES - half, axis=1) * s_up + pltpu.roll(u, half, axis=1) * s_dn

    def head_pad(u, odd):
        if odd:
            u = pltpu.roll(u, N_HEAD_DIM, axis=1)
        return jnp.where(lane < N_HEAD_DIM, u, 0.0)

    scale = N_HEAD_DIM ** -0.5 * math.log2(math.e)
    zeros_t = jnp.zeros((N_HEAD_DIM, tk), BF16)
    for g in range(N_KV_GROUPS):
        for hp in range(N_HPG // 2):
            c0 = g * (N_HPG * N_HEAD_DIM) + hp * LANES
            u = q_ref[:, c0:c0 + LANES].astype(F32) * scale
            ur = rope(u)
            u_t = u.T.astype(BF16)
            for odd in range(2):
                h = 2 * hp + odd
                qrawt_ref[g, h * LANES:h * LANES + N_HEAD_DIM, :] = u_t[odd * N_HEAD_DIM:(odd + 1) * N_HEAD_DIM, :]
                qrawt_ref[g, h * LANES + N_HEAD_DIM:(h + 1) * LANES, :] = zeros_t
                qrope_ref[g, :, h * LANES:(h + 1) * LANES] = head_pad(ur, odd).astype(BF16)

    ks_t = rope(kv_ref[:, 0:LANES].astype(F32)).T
    kw_t = rope(kv_ref[:, 2 * LANES:3 * LANES].astype(F32)).T
    vs = kv_ref[:, LANES:2 * LANES].astype(F32)
    vw = kv_ref[:, 3 * LANES:4 * LANES].astype(F32)
    t_glob = pl.program_id(1) * tk + lax.broadcasted_iota(jnp.int32, (LANES, tk), 1)
    blk_row = lax.broadcasted_iota(jnp.int32, (LANES, tk), 0)
    onehot = jnp.where(blk_row == t_glob // SEL_BLOCK, 1.0, 0.0).astype(BF16)
    ones_lane = jnp.where(lane == N_HEAD_DIM, 1.0, 0.0)
    gates = _sigmoid(sm_ref[...] + gb_ref[...])
    for g in range(N_KV_GROUPS):
        rs = slice(g * N_HEAD_DIM, (g + 1) * N_HEAD_DIM)
        kst_ref[g, 0:LANES, :] = onehot
        kst_ref[g, LANES:LANES + N_HEAD_DIM, :] = ks_t[rs, :].astype(BF16)
        kst_ref[g, LANES + N_HEAD_DIM:2 * LANES, :] = zeros_t
        kwt_ref[g, 0:N_HEAD_DIM, :] = kw_t[rs, :].astype(BF16)
        kwt_ref[g, N_HEAD_DIM:LANES, :] = zeros_t
        vs_ref[g] = (head_pad(vs, g) + ones_lane).astype(BF16)
        vw_ref[g] = (head_pad(vw, g) + ones_lane).astype(BF16)
        goff = 8 + g * (3 * N_HPG)
        gg = pltpu.roll(gates, LANES - goff, axis=1)
        gates_ref[g] = gg
        gatest_ref[g] = gg.T


def _kvprep(p, aux, b, t, pos_col, inv_row, gate_bias):
    tk = 512
    nt = t // tk
    g = N_KV_GROUPS
    qw = N_HPG * LANES

    def tok(shape):
        return pl.BlockSpec((None, g) + shape, lambda bi, i: (bi, 0, i, 0))

    def tok_t(rows):
        return pl.BlockSpec((None, g, rows, tk), lambda bi, i: (bi, 0, 0, i))

    return pl.pallas_call(
        _kvprep_kernel,
        out_shape=(
            jax.ShapeDtypeStruct((b, g, qw, t), BF16), jax.ShapeDtypeStruct((b, g, t, qw), BF16),
            jax.ShapeDtypeStruct((b, g, 2 * LANES, t), BF16), jax.ShapeDtypeStruct((b, g, t, LANES), BF16),
            jax.ShapeDtypeStruct((b, g, LANES, t), BF16), jax.ShapeDtypeStruct((b, g, t, LANES), BF16),
            jax.ShapeDtypeStruct((b, g, t, LANES), F32), jax.ShapeDtypeStruct((b, g, LANES, t), F32),
        ),
        grid=(b, nt),
        in_specs=[
            pl.BlockSpec((tk, N_WIDTH), lambda bi, i: (bi * nt + i, C_NQ // N_WIDTH)),
            pl.BlockSpec((tk, 4 * LANES), lambda bi, i: (bi * nt + i, C_KV // (4 * LANES))),
            pl.BlockSpec((tk, LANES), lambda bi, i: (bi * nt + i, X_SM)),
            pl.BlockSpec((None, 1, tk), lambda bi, i: (bi, 0, i)),
            pl.BlockSpec((1, LANES), lambda bi, i: (0, 0)),
            pl.BlockSpec((1, LANES), lambda bi, i: (0, 0)),
        ],
        out_specs=(tok_t(qw), tok((tk, qw)), tok_t(2 * LANES), tok((tk, LANES)),
                   tok_t(LANES), tok((tk, LANES)), tok((tk, LANES)), tok_t(LANES)),
        compiler_params=_cparams(("parallel", "arbitrary")),
        name="kvprep",
    )(p, p, aux, pos_col, inv_row, gate_bias)


CMP_ROW_CHUNK = 128


def _select_kernel(qt_ref, kc_ref, vct_ref, ovt_ref, gt_ref, oc_ref, bias_ref, imp_ref):
    qs = qt_ref.shape[1]
    nr = kc_ref.shape[0]
    t0 = pl.program_id(2) * qs
    sees_any = t0 + lax.broadcasted_iota(jnp.int32, (1, qs), 1) >= CMP_BLOCK - 1

    def cmp_branch(rows):
        m_c = (lax.broadcasted_iota(jnp.int32, (rows, qs), 0) * CMP_STRIDE + (CMP_BLOCK - 1)
               <= t0 + lax.broadcasted_iota(jnp.int32, (rows, qs), 1))
        kc = kc_ref[0:rows, :]
        vct = vct_ref[:, 0:rows]
        psum = jnp.zeros((rows, qs), F32)
        for h in range(N_HPG):
            s = jnp.dot(kc, qt_ref[h * LANES:(h + 1) * LANES, :], preferred_element_type=F32)
            s = jnp.where(m_c, s, NEG)
            e = jnp.exp2(s - jnp.max(s, axis=0, keepdims=True))
            den = jnp.sum(e, axis=0, keepdims=True)
            pc = e * jnp.where(sees_any, 1.0 / den, 0.0)
            psum = psum + pc
            oc_t = jnp.dot(vct, pc.astype(BF16), preferred_element_type=F32)
            oc_ref[:, h * LANES:(h + 1) * LANES] = (oc_t * gt_ref[3 * h:3 * h + 1, :]).T
        p_hi = psum.astype(BF16)
        p_lo = (psum - p_hi.astype(F32)).astype(BF16)
        ovt = ovt_ref[:, 0:rows]
        imp_ref[...] = (jnp.dot(ovt, p_hi, preferred_element_type=F32)
                        + jnp.dot(ovt, p_lo, preferred_element_type=F32))

    n_var = nr // CMP_ROW_CHUNK
    visible = (t0 + qs - CMP_BLOCK) // CMP_STRIDE + 1
    need = jnp.clip((visible + CMP_ROW_CHUNK - 1) // CMP_ROW_CHUNK, 1, n_var)
    for v in range(1, n_var + 1):
        pl.when(need == v)(functools.partial(cmp_branch, v * CMP_ROW_CHUNK))
    imp = imp_ref[...]

    blk = lax.broadcasted_iota(jnp.int32, (LANES, qs), 0)
    tq = t0 + lax.broadcasted_iota(jnp.int32, (LANES, qs), 1)
    cur = tq // SEL_BLOCK
    forced = (blk == 0) | (blk == cur) | (blk == cur - 1)
    n_sel = (nr * CMP_STRIDE) // SEL_BLOCK
    free_score = jnp.where(blk * SEL_BLOCK <= tq, imp, -1.0)
    blk_f = blk.astype(F32)
    n_top = min(SEL_TOPK, n_sel)
    n_forced = 3

    def pick(score, n_iter):
        score = jnp.where(blk < n_sel, score, -jnp.inf)
        for _ in range(n_iter):
            mx = jnp.max(score, axis=0, keepdims=True)
            idx = jnp.min(jnp.where(score == mx, blk_f, float(LANES)), axis=0, keepdims=True)
            score = jnp.where(blk_f == idx, -jnp.inf, score)
        bias_t = jnp.where((score == -jnp.inf) & (blk < n_sel), 0.0, NEG)
        bias_ref[...] = bias_t.T.astype(BF16)

    all_three_forced = (t0 >= 2 * SEL_BLOCK) & (n_top >= n_forced)

    @pl.when(all_three_forced)
    def _():
        pick(jnp.where(forced, -jnp.inf, free_score), n_top - n_forced)

    @pl.when(jnp.logical_not(all_three_forced))
    def _():
        pick(jnp.where(forced, FORCE, free_score), n_top)


def _select(qrawt, kc, vct, ovt, gates_t):
    b, g, qw, t = qrawt.shape
    nr = kc.shape[2]
    qs = 512

    def whole(rows, cols):
        return pl.BlockSpec((None, None, rows, cols), lambda bi, gi, i: (bi, gi, 0, 0))

    return pl.pallas_call(
        _select_kernel,
        out_shape=(jax.ShapeDtypeStruct((b, g, t, qw), F32), jax.ShapeDtypeStruct((b, g, t, LANES), BF16)),
        grid=(b, g, t // qs),
        in_specs=[
            pl.BlockSpec((None, None, qw, qs), lambda bi, gi, i: (bi, gi, 0, i)),
            whole(nr, LANES), whole(LANES, nr),
            pl.BlockSpec(ovt.shape, lambda bi, gi, i: (0, 0)),
            pl.BlockSpec((None, None, LANES, qs), lambda bi, gi, i: (bi, gi, 0, i)),
        ],
        out_specs=(pl.BlockSpec((None, None, qs, qw), lambda bi, gi, i: (bi, gi, i, 0)),
                   pl.BlockSpec((None, None, qs, LANES), lambda bi, gi, i: (bi, gi, i, 0))),
        scratch_shapes=[pltpu.VMEM((LANES, qs), F32)],
        compiler_params=_cparams(("parallel", "parallel", "parallel")),
        name="select",
    )(qrawt, kc, vct, ovt, gates_t)


def _nsa_kernel(qrope_ref, bias_ref, oc_ref, kst_ref, vs_ref, kwt_ref, vw_ref, gates_ref, nz_ref,
                yb_ref, qaug_ref, acc_ref, m_ref, s_ref):
    i = pl.program_id(1)
    t0 = i * Q_BLOCK
    Q = Q_BLOCK
    nh = N_HPG
    ng = N_KV_GROUPS
    rows = nh * Q
    t0a = pl.multiple_of(t0, Q)
    causal = (lax.broadcasted_iota(jnp.int32, (Q, Q), 1) <= lax.broadcasted_iota(jnp.int32, (Q, Q), 0))
    lane = lax.broadcasted_iota(jnp.int32, (Q, LANES), 1)

    def q_rope(g):
        return jnp.concatenate([qrope_ref[g, :, h * LANES:(h + 1) * LANES] for h in range(nh)], axis=0)

    def mask_heads(s, m):
        return jnp.concatenate([jnp.where(m, s[h * Q:(h + 1) * Q, :], NEG) for h in range(nh)], axis=0)

    first_blk = t0 // SEL_BLOCK
    for g in range(ng):
        qr = q_rope(g)
        bias = bias_ref[g]
        bias_early = jnp.where(lane < first_blk, bias, NEG).astype(BF16)
        qaug_ref[g, :, 0:LANES] = jnp.concatenate([bias_early] * nh, axis=0)
        qaug_ref[g, :, LANES:2 * LANES] = qr
        q_diag = jnp.concatenate([jnp.concatenate([bias] * nh, axis=0), qr], axis=1)
        s_d = jnp.dot(q_diag, kst_ref[g, :, pl.ds(t0a, Q)], preferred_element_type=F32)
        s_d = mask_heads(s_d, causal)
        m0 = jnp.max(s_d, axis=1, keepdims=True)
        p_d = jnp.exp2(s_d - m0)
        m_ref[g] = jnp.broadcast_to(m0, (rows, LANES))
        acc_ref[g] = jnp.dot(p_d.astype(BF16), vs_ref[g, pl.ds(t0a, Q), :], preferred_element_type=F32)

    ncb = SEL_CHUNK // LANES
    n_chunks = (t0 + SEL_CHUNK - 1) // SEL_CHUNK
    last = jnp.maximum(n_chunks - 1, 0)

    def scores(g, c, slot):
        k0 = pl.multiple_of(c * SEL_CHUNK, SEL_CHUNK)
        s_ref[slot, g] = jnp.dot(qaug_ref[g], kst_ref[g, :, pl.ds(k0, SEL_CHUNK)], preferred_element_type=F32)

    def consume(g, c, slot):
        k0 = pl.multiple_of(c * SEL_CHUNK, SEL_CHUNK)
        s = s_ref[slot, g]
        m_old = m_ref[g]
        m_new = jnp.maximum(m_old, jnp.max(s, axis=1, keepdims=True))
        p = jnp.concatenate([jnp.exp2(s[:, cb * LANES:(cb + 1) * LANES] - m_new) for cb in range(ncb)], axis=1)
        acc_ref[g] = (jnp.exp2(m_old - m_new) * acc_ref[g]
                      + jnp.dot(p.astype(BF16), vs_ref[g, pl.ds(k0, SEL_CHUNK), :], preferred_element_type=F32))
        m_ref[g] = m_new

    for g in range(ng):
        scores(g, 0, 0)

    def steps(c, n):
        for u in range(n):
            for g in range(ng):
                scores(g, jnp.minimum(c + u + 1, last), (u + 1) % 2)
                consume(g, c + u, u % 2)

    def sel_quad(c4, carry):
        steps(SEL_UNROLL * c4, SEL_UNROLL)
        return carry

    n_quads = n_chunks // SEL_UNROLL
    lax.fori_loop(0, n_quads, sel_quad, 0)
    rem = n_chunks - SEL_UNROLL * n_quads

    @pl.when(rem >= 2)
    def _():
        steps(SEL_UNROLL * n_quads, 2)

    @pl.when(rem % 2 == 1)
    def _():
        for g in range(ng):
            consume(g, n_chunks - 1, 0)

    wlen = WINDOW + Q
    ws = pl.multiple_of(jnp.maximum(t0 - WINDOW, 0), Q)
    diff = (t0 + lax.broadcasted_iota(jnp.int32, (Q, wlen), 0)) - (ws + lax.broadcasted_iota(jnp.int32, (Q, wlen), 1))
    m_w = (diff >= 0) & (diff < WINDOW)
    for g in range(ng):
        gates = gates_ref[g]

        def gate_col(br):
            return jnp.concatenate([gates[:, 3 * h + br:3 * h + br + 1] for h in range(nh)], axis=0)

        acc = acc_ref[g]
        o_s = acc * (gate_col(1) / acc[:, N_HEAD_DIM:N_HEAD_DIM + 1])
        s_w = jnp.dot(q_rope(g), kwt_ref[g, :, pl.ds(ws, wlen)], preferred_element_type=F32)
        s_w = mask_heads(s_w, m_w)
        p_w = jnp.exp2(s_w - jnp.max(s_w, axis=1, keepdims=True))
        acc_w = jnp.dot(p_w.astype(BF16), vw_ref[g, pl.ds(ws, wlen), :], preferred_element_type=F32)
        o_w = acc_w * (gate_col(2) / acc_w[:, N_HEAD_DIM:N_HEAD_DIM + 1])
        outs = []
        for h in range(nh):
            rs = slice(h * Q, (h + 1) * Q)
            outs.append(oc_ref[g, :, h * LANES:(h + 1) * LANES] + o_s[rs, :] + o_w[rs, :])
        for hp in range(nh // 2):
            yn = jnp.where(lane < N_HEAD_DIM, outs[2 * hp], pltpu.roll(outs[2 * hp + 1], N_HEAD_DIM, axis=1))
            ls = slice((g * (nh // 2) + hp) * LANES, (g * (nh // 2) + hp + 1) * LANES)
            yb_ref[:, ls] = (yn * _silu(nz_ref[:, ls].astype(F32))).astype(BF16)


def _nsa(p, b, t, qrope, bias, oc, kst, vs, kwt, vw, gates):
    g = N_KV_GROUPS
    nqb = t // Q_BLOCK
    qw = N_HPG * LANES
    rows = N_HPG * Q_BLOCK

    def qblk(w):
        return pl.BlockSpec((None, g, Q_BLOCK, w), lambda bi, i: (bi, 0, i, 0))

    def whole(r, c):
        return pl.BlockSpec((None, g, r, c), lambda bi, i: (bi, 0, 0, 0), pipeline_mode=pl.Buffered(1))

    return pl.pallas_call(
        _nsa_kernel,
        out_shape=jax.ShapeDtypeStruct((b * t, N_WIDTH), BF16),
        grid=(b, nqb),
        in_specs=[
            qblk(qw), qblk(LANES), qblk(qw),
            whole(2 * LANES, t), whole(t, LANES), whole(LANES, t), whole(t, LANES),
            qblk(LANES),
            pl.BlockSpec((Q_BLOCK, N_WIDTH), lambda bi, i: (bi * nqb + i, C_NZ // N_WIDTH)),
        ],
        out_specs=pl.BlockSpec((Q_BLOCK, N_WIDTH), lambda bi, i: (bi * nqb + i, 0)),
        scratch_shapes=[pltpu.VMEM((g, rows, 2 * LANES), BF16), pltpu.VMEM((g, rows, LANES), F32),
                        pltpu.VMEM((g, rows, LANES), F32), pltpu.VMEM((2, g, rows, SEL_CHUNK), F32)],
        compiler_params=_cparams(("parallel", "arbitrary")),
        name="nsa",
    )(qrope, bias, oc, kst, vs, kwt, vw, gates, p)


def _outproj_kernel(final, ya_ref, yb_ref, ga_ref, gb_ref, x_ref, wa_ref, wb_ref, wo_ref, fg_ref, o_ref):
    a = jnp.dot(ya_ref[...], wa_ref[...], preferred_element_type=F32)
    bq = jnp.dot(yb_ref[...], wb_ref[...], preferred_element_type=F32)
    merged = _sigmoid(ga_ref[...].astype(F32)) * a + _sigmoid(gb_ref[...].astype(F32)) * bq
    y = x_ref[...] + jnp.dot(merged.astype(BF16), wo_ref[...], preferred_element_type=F32)
    if final:
        y = y * lax.rsqrt(jnp.mean(y * y, axis=-1, keepdims=True) + NORM_EPS) * fg_ref[...]
    o_ref[...] = y


def _outproj(ya, yb, p, x2, wa, wb, wo, fg, final):
    m = x2.shape[0]
    tm = 512

    def full(a):
        return pl.BlockSpec(a.shape, lambda i: (0, 0))

    return pl.pallas_call(
        functools.partial(_outproj_kernel, final),
        out_shape=jax.ShapeDtypeStruct((m, D_MODEL), F32),
        grid=(m // tm,),
        in_specs=[
            pl.BlockSpec((tm, M_WIDTH), lambda i: (i, 0)),
            pl.BlockSpec((tm, N_WIDTH), lambda i: (i, 0)),
            pl.BlockSpec((tm, D_MODEL), lambda i: (i, C_GA // D_MODEL)),
            pl.BlockSpec((tm, D_MODEL), lambda i: (i, C_GB // D_MODEL)),
            pl.BlockSpec((tm, D_MODEL), lambda i: (i, 0)),
            full(wa), full(wb), full(wo), full(fg),
        ],
        out_specs=pl.BlockSpec((tm, D_MODEL), lambda i: (i, 0)),
        compiler_params=_cparams(("parallel",)),
        name="outproj",
    )(ya, yb, p, p, x2, wa, wb, wo, fg)


def _permute_w_tail(w):
    pad = jnp.zeros((w.shape[0], P_W - _O_END), w.dtype)
    return jnp.concatenate([
        w[:, _O_GA:_O_END],
        w[:, _O_NQ:_O_KC],
        w[:, _O_NZ:_O_GA],
        w[:, _O_KS:_O_NG],
        w[:, _O_KC:_O_KS],
        w[:, _O_MI:_O_NQ],
        w[:, _O_NG:_O_NZ],
        pad], axis=1)


def _expand_cmp_w1(w1k, w1v):
    half = CMP_BLOCK // 2
    k3 = w1k.astype(BF16).reshape(CMP_BLOCK, 1, N_HEAD_DIM, 1, CMP_HIDDEN)
    v3 = w1v.astype(BF16).reshape(CMP_BLOCK, 1, N_HEAD_DIM, 1, CMP_HIDDEN)
    nc = 2 * N_KV_GROUPS
    eye = jnp.eye(nc, dtype=BF16)
    is_k = (jnp.arange(nc) < N_KV_GROUPS).astype(BF16)
    mk = (eye * is_k[:, None]).reshape(1, nc, 1, nc, 1)
    mv = (eye * (1 - is_k)[:, None]).reshape(1, nc, 1, nc, 1)
    z = k3 * mk + v3 * mv
    z = z.reshape(CMP_BLOCK * nc * N_HEAD_DIM, nc * CMP_HIDDEN)
    return z[:half * nc * N_HEAD_DIM], z[half * nc * N_HEAD_DIM:]


def kernel(x, positions, norm_g, w_in, conv_q_w, conv_q_b, conv_k_w, conv_k_b, b_igate, b_fgate, mh_norm_g,
           cmp_pos_k, cmp_w1_k, cmp_w2_k, cmp_pos_v, cmp_w1_v, cmp_w2_v, b_nsa_gate, w_branch_a, w_branch_b,
           w_out, final_norm_g):
    b, t, _ = x.shape
    assert t % (2 * SEL_CHUNK) == 0 and t >= WINDOW + Q_BLOCK and t // SEL_BLOCK <= LANES and t % MLSTM_CHUNK == 0
    x2 = x.reshape(b * t, D_MODEL)
    for l in range(norm_g.shape[0]):
        w_bf = w_in[l].astype(BF16)
        p, aux = _inproj(x2, norm_g[l][None, :], w_bf, _permute_w_tail(w_bf))

        sm_bias = jnp.zeros((1, LANES), F32).at[0, 0:M_HEADS].set(b_igate[l]).at[0, M_HEADS:2 * M_HEADS].set(b_fgate[l])
        ya = _mlstm(p, aux, b, t, conv_q_w[l], conv_q_b[l][None, :], conv_k_w[l], conv_k_b[l][None, :], sm_bias,
                    mh_norm_g[l][None, :])

        nr = t // CMP_STRIDE
        wt, wb = _expand_cmp_w1(cmp_w1_k[l], cmp_w1_v[l])
        w2pad = lambda w: jnp.pad(w, ((0, 0), (0, LANES - N_HEAD_DIM))).astype(BF16)
        kc, vct = _compress(aux, b, t, wt, wb, cmp_pos_k[l].reshape(1, -1), cmp_pos_v[l].reshape(1, -1),
                            cmp_w1_k[l].astype(BF16), cmp_w1_v[l].astype(BF16), w2pad(cmp_w2_k[l]), w2pad(cmp_w2_v[l]))

        half = ROPE_DIM // 2
        inv = jnp.power(jnp.float32(ROPE_THETA), -jnp.arange(half, dtype=F32) * (2.0 / ROPE_DIM))
        lih = jnp.arange(LANES) % N_HEAD_DIM
        inv_row = jnp.where(lih < ROPE_DIM, inv[lih % half], 0.0)[None, :].astype(F32)
        pos_col = positions.astype(F32)[:, None, :]
        gate_bias = jnp.zeros((1, LANES), F32).at[0, 8:8 + 3 * N_HEADS].set(b_nsa_gate[l])
        qrawt, qrope, kst, vs, kwt, vw, gates, gates_t = _kvprep(p, aux, b, t, pos_col, inv_row, gate_bias)

        ci = jnp.arange(nr)[None, :] * CMP_STRIDE
        sj = jnp.arange(LANES)[:, None] * SEL_BLOCK
        overlap_t = ((ci < sj + SEL_BLOCK) & (ci + CMP_BLOCK > sj) & (jnp.arange(nr)[None, :] < nr - 1)
                     & (jnp.arange(LANES)[:, None] < t // SEL_BLOCK)).astype(BF16)
        oc, bias = _select(qrawt, kc, vct, overlap_t, gates_t)
        yb = _nsa(p, b, t, qrope, bias, oc, kst, vs, kwt, vw, gates)

        x2 = _outproj(ya, yb, p, x2, w_branch_a[l].astype(BF16), w_branch_b[l].astype(BF16),
                      w_out[l].astype(BF16), final_norm_g[None, :], l == norm_g.shape[0] - 1)
    return x2.reshape(b, t, D_MODEL)
```

```python
import functools
import math

import jax
import jax.numpy as jnp
from jax import lax
from jax.experimental import pallas as pl
from jax.experimental.pallas import tpu as pltpu

F32 = jnp.float32
BF16 = jnp.bfloat16

D_MODEL = 1024
M_HEADS = 4
M_HEAD_DIM = 256
M_WIDTH = M_HEADS * M_HEAD_DIM
CONV_WIDTH = 4
N_HEADS = 8
N_KV_GROUPS = 2
N_HPG = N_HEADS // N_KV_GROUPS
N_HEAD_DIM = 64
N_WIDTH = N_HEADS * N_HEAD_DIM
CMP_BLOCK = 32
CMP_STRIDE = 16
CMP_HIDDEN = 128
SEL_BLOCK = 64
SEL_TOPK = 16
WINDOW = 512
Q_BLOCK = 256
ROPE_THETA = 500000.0
ROPE_DIM = N_HEAD_DIM // 4
NORM_EPS = 1e-6
NEG = -1e30
FORCE = 1e9

LANES = 128
SUBLANES = 8
MLSTM_CHUNK = 256
SEL_CHUNK = 512
SEL_UNROLL = 4
VMEM_LIMIT = 56 * 1024 * 1024

C_MQ, C_MK, C_MV, C_MO, C_MZ, C_GA, C_GB = 0, 1024, 2048, 3072, 4096, 5120, 6144
C_NQ, C_NZ = 7168, 7680
C_KV = 8192
C_CMP = 8704
C_SM = 8960
P_W = 9216
AUX_W = 3 * LANES
X_KC, X_VC, X_SM = 0, 1, 2
_O_MI, _O_NQ, _O_KC, _O_KS, _O_NG, _O_NZ, _O_GA, _O_END = 5120, 5128, 5640, 5896, 6408, 6432, 6944, 8992


def _cparams(sem):
    return pltpu.CompilerParams(dimension_semantics=sem, vmem_limit_bytes=VMEM_LIMIT)


def _sigmoid(x):
    return 0.5 * jnp.tanh(0.5 * x) + 0.5


def _silu(x):
    h = 0.5 * x
    return h + h * jnp.tanh(h)


def _log_sigmoid(x):
    return jnp.minimum(x, 0.0) - jnp.log1p(jnp.exp(-jnp.abs(x)))


PROJ_TN = 1024
AUX_TILE = C_CMP // PROJ_TN
AUX_OFF = C_CMP % PROJ_TN


N_HEAD_TILES = _O_MI // PROJ_TN


def _inproj_kernel(x_ref, g_ref, wh_ref, wt_ref, o_ref, aux_ref, hn_ref):
    j = pl.program_id(1)

    @pl.when(j == 0)
    def _():
        x = x_ref[...]
        r = lax.rsqrt(jnp.mean(x * x, axis=-1, keepdims=True) + NORM_EPS)
        hn_ref[...] = (x * r * g_ref[...]).astype(BF16)

    @pl.when(j < N_HEAD_TILES)
    def _():
        o_ref[...] = jnp.dot(hn_ref[...], wh_ref[...], preferred_element_type=F32).astype(BF16)

    @pl.when((j >= N_HEAD_TILES) & (j != AUX_TILE))
    def _():
        o_ref[...] = jnp.dot(hn_ref[...], wt_ref[...], preferred_element_type=F32).astype(BF16)

    @pl.when(j == AUX_TILE)
    def _():
        acc = jnp.dot(hn_ref[...], wt_ref[...], preferred_element_type=F32)
        o_ref[...] = acc.astype(BF16)
        aux_ref[...] = acc[:, AUX_OFF:AUX_OFF + AUX_W]


def _inproj(x2, norm_g, w_head, w_tail):
    m = x2.shape[0]
    tm, tn = 2048, PROJ_TN
    return pl.pallas_call(
        _inproj_kernel,
        out_shape=(jax.ShapeDtypeStruct((m, P_W), BF16), jax.ShapeDtypeStruct((m, AUX_W), F32)),
        grid=(m // tm, P_W // tn),
        in_specs=[
            pl.BlockSpec((tm, D_MODEL), lambda i, j: (i, 0)),
            pl.BlockSpec((1, D_MODEL), lambda i, j: (0, 0)),
            pl.BlockSpec((D_MODEL, tn), lambda i, j: (0, jnp.minimum(j, N_HEAD_TILES - 1))),
            pl.BlockSpec((D_MODEL, tn), lambda i, j: (0, jnp.maximum(j - N_HEAD_TILES, 0))),
        ],
        out_specs=(pl.BlockSpec((tm, tn), lambda i, j: (i, j)), pl.BlockSpec((tm, AUX_W), lambda i, j: (i, 0))),
        scratch_shapes=[pltpu.VMEM((tm, D_MODEL), BF16)],
        compiler_params=_cparams(("parallel", "arbitrary")),
        name="inproj",
    )(x2, norm_g, w_head, w_tail)


def _causal_conv_silu(u16, prev_ref, w_ref, b_ref):
    n, w = u16.shape
    u = u16.astype(F32)
    row = lax.broadcasted_iota(jnp.int32, (n, n), 0)
    col = lax.broadcasted_iota(jnp.int32, (n, n), 1)
    sub = lax.broadcasted_iota(jnp.int32, (SUBLANES, w), 0)
    prev = prev_ref[...]
    y = b_ref[...] + u * w_ref[CONV_WIDTH - 1:CONV_WIDTH, :]
    for d in range(1, CONV_WIDTH):
        shift = jnp.where(row - col == d, 1.0, 0.0).astype(BF16)
        sh = jnp.dot(shift, u16, preferred_element_type=F32)
        head = jnp.where(sub < d, pltpu.roll(prev, d, axis=0), sh[0:SUBLANES, :])
        sh = jnp.concatenate([head, sh[SUBLANES:, :]], axis=0)
        y = y + sh * w_ref[CONV_WIDTH - 1 - d:CONV_WIDTH - d, :]
    prev_ref[...] = u[n - SUBLANES:n, :]
    return _silu(y)


def _mlstm_kernel(q_ref, k_ref, v_ref, og_ref, z_ref, sm_ref, cqw_ref, cqb_ref, ckw_ref, ckb_ref,
                  sb_ref, ng_ref, ya_ref, qprev, kprev, c_state, n_state, m_state):
    L = q_ref.shape[0]
    d = M_HEAD_DIM

    @pl.when(pl.program_id(1) == 0)
    def _():
        qprev[...] = jnp.zeros_like(qprev)
        kprev[...] = jnp.zeros_like(kprev)
        c_state[...] = jnp.zeros_like(c_state)
        n_state[...] = jnp.zeros_like(n_state)
        m_state[...] = jnp.zeros_like(m_state)

    mq = _causal_conv_silu(q_ref[...], qprev, cqw_ref, cqb_ref)
    mk = _causal_conv_silu(k_ref[...], kprev, ckw_ref, ckb_ref) * (d ** -0.5)

    sm = sm_ref[...] + sb_ref[...]
    lf = _log_sigmoid(sm)
    row = lax.broadcasted_iota(jnp.int32, (L, L), 0)
    col = lax.broadcasted_iota(jnp.int32, (L, L), 1)
    causal = row >= col
    tril = causal.astype(F32)
    triu = (row <= col).astype(F32)
    a_col = jnp.dot(tril, lf, precision=lax.Precision.HIGHEST, preferred_element_type=F32)
    sm_t = sm.T[0:8, :]
    a_row = jnp.dot(_log_sigmoid(sm_t), triu, precision=lax.Precision.HIGHEST,
                    preferred_element_type=F32)

    for h in range(M_HEADS):
        hs = slice(h * d, (h + 1) * d)
        a_c = a_col[:, 4 + h:5 + h]
        li_c = sm[:, h:h + 1]
        a_r = a_row[4 + h:5 + h, :]
        li_r = sm_t[h:h + 1, :]
        m_prev = m_state[h:h + 1, 0:1]
        a_end = a_c[L - 1:L, :]

        log_d = jnp.where(causal, a_c - a_r + li_r, -jnp.inf)
        inter = a_c + m_prev
        m_t = jnp.maximum(inter, jnp.max(log_d, axis=1, keepdims=True))
        w_inter = jnp.exp(inter - m_t)
        dm = jnp.exp(log_d - m_t)

        qh = mq[:, hs]
        kh = mk[:, hs]
        qb = qh.astype(BF16)
        kb = kh.astype(BF16)
        vb = v_ref[:, hs]
        s = lax.dot_general(qb, kb, (((1,), (1,)), ((), ())), preferred_element_type=F32) * dm
        c_old = c_state[h]
        n_old = n_state[h:h + 1, :]
        num = (w_inter * jnp.dot(qb, c_old.astype(BF16), preferred_element_type=F32)
               + jnp.dot(s.astype(BF16), vb, preferred_element_type=F32))
        den = (w_inter * jnp.sum(qh * n_old, axis=1, keepdims=True)
               + jnp.sum(s, axis=1, keepdims=True))
        hh = num * (1.0 / jnp.maximum(jnp.abs(den), jnp.exp(-m_t)))

        logw = a_end - a_c + li_c
        m_new = jnp.maximum(a_end + m_prev, jnp.max(logw, axis=0, keepdims=True))
        wk = jnp.exp(logw - m_new)
        decay = jnp.exp(a_end + m_prev - m_new)
        kw = kh * wk
        c_state[h] = decay * c_old + lax.dot_general(kw.astype(BF16), vb, (((0,), (0,)), ((), ())),
                                                     preferred_element_type=F32)
        n_state[h:h + 1, :] = decay * n_old + jnp.sum(kw, axis=0, keepdims=True)
        m_state[h:h + 1, :] = jnp.broadcast_to(m_new, (1, LANES))

        mu = jnp.mean(hh, axis=1, keepdims=True)
        dc = hh - mu
        var = jnp.mean(dc * dc, axis=1, keepdims=True)
        hn = dc * lax.rsqrt(var + NORM_EPS) * ng_ref[:, hs]
        ya = _sigmoid(og_ref[:, hs].astype(F32)) * hn * _silu(z_ref[:, hs].astype(F32))
        ya_ref[:, hs] = ya.astype(BF16)


def _mlstm(p, aux, b, t, conv_q_w, conv_q_b, conv_k_w, conv_k_b, sm_bias, mh_norm_g):
    L = MLSTM_CHUNK
    nc = t // L

    def colspec(cstart):
        return pl.BlockSpec((L, M_WIDTH), lambda bi, c: (bi * nc + c, cstart // M_WIDTH))

    def full(shape):
        return pl.BlockSpec(shape, lambda bi, c: (0,) * len(shape))

    return pl.pallas_call(
        _mlstm_kernel,
        out_shape=jax.ShapeDtypeStruct((b * t, M_WIDTH), BF16),
        grid=(b, nc),
        in_specs=[
            colspec(C_MQ), colspec(C_MK), colspec(C_MV), colspec(C_MO), colspec(C_MZ),
            pl.BlockSpec((L, LANES), lambda bi, c: (bi * nc + c, X_SM)),
            full((CONV_WIDTH, M_WIDTH)), full((1, M_WIDTH)), full((CONV_WIDTH, M_WIDTH)), full((1, M_WIDTH)),
            full((1, LANES)), full((1, M_WIDTH)),
        ],
        out_specs=pl.BlockSpec((L, M_WIDTH), lambda bi, c: (bi * nc + c, 0)),
        scratch_shapes=[
            pltpu.VMEM((SUBLANES, M_WIDTH), F32), pltpu.VMEM((SUBLANES, M_WIDTH), F32),
            pltpu.VMEM((M_HEADS, M_HEAD_DIM, M_HEAD_DIM), F32),
            pltpu.VMEM((8, M_HEAD_DIM), F32), pltpu.VMEM((8, LANES), F32),
        ],
        compiler_params=_cparams(("parallel", "arbitrary")),
        name="mlstm",
    )(p, p, p, p, p, aux, conv_q_w, conv_q_b, conv_k_w, conv_k_b, sm_bias, mh_norm_g)


def _compress_kernel(pk_ref, pv_ref, wt_ref, wb_ref, posk_ref, posv_ref, w1k_ref, w1v_ref, w2k_ref, w2v_ref,
                     kc_ref, vct_ref):
    nr = pk_ref.shape[0] // CMP_STRIDE
    cw = 2 * LANES
    u = jnp.zeros((nr, wt_ref.shape[1]), F32)
    v = jnp.zeros((nr, wb_ref.shape[1]), F32)
    for l in range(CMP_STRIDE):
        x = jnp.concatenate([pk_ref[pl.ds(l, nr, stride=CMP_STRIDE), :],
                             pv_ref[pl.ds(l, nr, stride=CMP_STRIDE), :]], axis=1).astype(BF16)
        u = u + jnp.dot(x, wt_ref[l * cw:(l + 1) * cw, :], preferred_element_type=F32)
        v = v + jnp.dot(x, wb_ref[l * cw:(l + 1) * cw, :], preferred_element_type=F32)
    v = pltpu.roll(v, nr - 1, axis=0)
    rowi = lax.broadcasted_iota(jnp.int32, v.shape, 0)
    v = jnp.where(rowi < nr - 1, v, 0.0)

    def pos_term(pos_ref, w_ref):
        pb = jnp.broadcast_to(pos_ref[...].astype(BF16), (8, CMP_BLOCK * N_HEAD_DIM))
        return jnp.dot(pb, w_ref[...], preferred_element_type=F32)[0:1, :]

    pk = pos_term(posk_ref, w1k_ref)
    pv = pos_term(posv_ref, w1v_ref)
    hid = u + v + jnp.concatenate([pk, pk, pv, pv], axis=1)
    act = 0.5 * hid * (1.0 + jnp.tanh(math.sqrt(2.0 / math.pi) * (hid + 0.044715 * (hid * hid * hid))))
    for c in range(2 * N_KV_GROUPS):
        w2 = w2k_ref[...] if c < N_KV_GROUPS else w2v_ref[...]
        out = jnp.dot(act[:, c * CMP_HIDDEN:(c + 1) * CMP_HIDDEN].astype(BF16), w2,
                      preferred_element_type=F32)
        if c < N_KV_GROUPS:
            kc_ref[c] = out.astype(BF16)
        else:
            out_t = out.T
            ones_row = lax.broadcasted_iota(jnp.int32, out_t.shape, 0) == N_HEAD_DIM
            vct_ref[c - N_KV_GROUPS] = jnp.where(ones_row, 1.0, out_t).astype(BF16)


def _compress(aux, b, t, wt, wb, posk, posv, w1k, w1v, w2k, w2v):
    nr = t // CMP_STRIDE
    g = N_KV_GROUPS

    def full(a):
        return pl.BlockSpec(a.shape, lambda bi: (0,) * a.ndim)

    return pl.pallas_call(
        _compress_kernel,
        out_shape=(jax.ShapeDtypeStruct((b, g, nr, LANES), BF16), jax.ShapeDtypeStruct((b, g, LANES, nr), BF16)),
        grid=(b,),
        in_specs=[pl.BlockSpec((t, LANES), lambda bi: (bi, X_KC)),
                  pl.BlockSpec((t, LANES), lambda bi: (bi, X_VC)),
                  full(wt), full(wb), full(posk), full(posv), full(w1k), full(w1v), full(w2k), full(w2v)],
        out_specs=(pl.BlockSpec((None, g, nr, LANES), lambda bi: (bi, 0, 0, 0)),
                   pl.BlockSpec((None, g, LANES, nr), lambda bi: (bi, 0, 0, 0))),
        compiler_params=_cparams(("parallel",)),
        name="compress",
    )(aux, aux, wt, wb, posk, posv, w1k, w1v, w2k, w2v)


def _kvprep_kernel(q_ref, kv_ref, sm_ref, pos_ref, inv_ref, gb_ref,
                   qrawt_ref, qrope_ref, kst_ref, vs_ref, kwt_ref, vw_ref, gates_ref, gatest_ref):
    tk = q_ref.shape[0]
    lane = lax.broadcasted_iota(jnp.int32, (tk, LANES), 1)
    lih = lane % N_HEAD_DIM
    pos = jnp.broadcast_to(pos_ref[...], (LANES, tk)).T
    ang = pos * inv_ref[...]
    cos = jnp.cos(ang)
    sin = jnp.sin(ang)
    half = ROPE_DIM // 2
    s_up = jnp.where(lih < half, -sin, 0.0)
    s_dn = jnp.where((lih >= half) & (lih < ROPE_DIM), sin, 0.0)

    def rope(u):
        return u * cos + pltpu.roll(u, LANES - half, axis=1) * s_up + pltpu.roll(u, half, axis=1) * s_dn

    def head_pad(u, odd):
        if odd:
            u = pltpu.roll(u, N_HEAD_DIM, axis=1)
        return jnp.where(lane < N_HEAD_DIM, u, 0.0)

    scale = N_HEAD_DIM ** -0.5 * math.log2(math.e)
    zeros_t = jnp.zeros((N_HEAD_DIM, tk), BF16)
    for g in range(N_KV_GROUPS):
        for hp in range(N_HPG // 2):
            c0 = g * (N_HPG * N_HEAD_DIM) + hp * LANES
            u = q_ref[:, c0:c0 + LANES].astype(F32) * scale
            ur = rope(u)
            u_t = u.T.astype(BF16)
            for odd in range(2):
                h = 2 * hp + odd
                qrawt_ref[g, h * LANES:h * LANES + N_HEAD_DIM, :] = u_t[odd * N_HEAD_DIM:(odd + 1) * N_HEAD_DIM, :]
                qrawt_ref[g, h * LANES + N_HEAD_DIM:(h + 1) * LANES, :] = zeros_t
                qrope_ref[g, :, h * LANES:(h + 1) * LANES] = head_pad(ur, odd).astype(BF16)

    ks_t = rope(kv_ref[:, 0:LANES].astype(F32)).T
    kw_t = rope(kv_ref[:, 2 * LANES:3 * LANES].astype(F32)).T
    vs = kv_ref[:, LANES:2 * LANES].astype(F32)
    vw = kv_ref[:, 3 * LANES:4 * LANES].astype(F32)
    t_glob = pl.program_id(1) * tk + lax.broadcasted_iota(jnp.int32, (LANES, tk), 1)
    blk_row = lax.broadcasted_iota(jnp.int32, (LANES, tk), 0)
    onehot = jnp.where(blk_row == t_glob // SEL_BLOCK, 1.0, 0.0).astype(BF16)
    ones_lane = jnp.where(lane == N_HEAD_DIM, 1.0, 0.0)
    gates = _sigmoid(sm_ref[...] + gb_ref[...])
    for g in range(N_KV_GROUPS):
        rs = slice(g * N_HEAD_DIM, (g + 1) * N_HEAD_DIM)
        kst_ref[g, 0:LANES, :] = onehot
        kst_ref[g, LANES:LANES + N_HEAD_DIM, :] = ks_t[rs, :].astype(BF16)
        kst_ref[g, LANES + N_HEAD_DIM:2 * LANES, :] = zeros_t
        kwt_ref[g, 0:N_HEAD_DIM, :] = kw_t[rs, :].astype(BF16)
        kwt_ref[g, N_HEAD_DIM:LANES, :] = zeros_t
        vs_ref[g] = (head_pad(vs, g) + ones_lane).astype(BF16)
        vw_ref[g] = (head_pad(vw, g) + ones_lane).astype(BF16)
        goff = 8 + g * (3 * N_HPG)
        gg = pltpu.roll(gates, LANES - goff, axis=1)
        gates_ref[g] = gg
        gatest_ref[g] = gg.T


def _kvprep(p, aux, b, t, pos_col, inv_row, gate_bias):
    tk = 512
    nt = t // tk
    g = N_KV_GROUPS
    qw = N_HPG * LANES

    def tok(shape):
        return pl.BlockSpec((None, g) + shape, lambda bi, i: (bi, 0, i, 0))

    def tok_t(rows):
        return pl.BlockSpec((None, g, rows, tk), lambda bi, i: (bi, 0, 0, i))

    return pl.pallas_call(
        _kvprep_kernel,
        out_shape=(
            jax.ShapeDtypeStruct((b, g, qw, t), BF16), jax.ShapeDtypeStruct((b, g, t, qw), BF16),
            jax.ShapeDtypeStruct((b, g, 2 * LANES, t), BF16), jax.ShapeDtypeStruct((b, g, t, LANES), BF16),
            jax.ShapeDtypeStruct((b, g, LANES, t), BF16), jax.ShapeDtypeStruct((b, g, t, LANES), BF16),
            jax.ShapeDtypeStruct((b, g, t, LANES), F32), jax.ShapeDtypeStruct((b, g, LANES, t), F32),
        ),
        grid=(b, nt),
        in_specs=[
            pl.BlockSpec((tk, N_WIDTH), lambda bi, i: (bi * nt + i, C_NQ // N_WIDTH)),
            pl.BlockSpec((tk, 4 * LANES), lambda bi, i: (bi * nt + i, C_KV // (4 * LANES))),
            pl.BlockSpec((tk, LANES), lambda bi, i: (bi * nt + i, X_SM)),
            pl.BlockSpec((None, 1, tk), lambda bi, i: (bi, 0, i)),
            pl.BlockSpec((1, LANES), lambda bi, i: (0, 0)),
            pl.BlockSpec((1, LANES), lambda bi, i: (0, 0)),
        ],
        out_specs=(tok_t(qw), tok((tk, qw)), tok_t(2 * LANES), tok((tk, LANES)),
                   tok_t(LANES), tok((tk, LANES)), tok((tk, LANES)), tok_t(LANES)),
        compiler_params=_cparams(("parallel", "arbitrary")),
        name="kvprep",
    )(p, p, aux, pos_col, inv_row, gate_bias)


CMP_ROW_CHUNK = 128


def _select_kernel(qt_ref, kc_ref, vct_ref, ovt_ref, gt_ref, oc_ref, bias_ref, imp_ref):
    qs = qt_ref.shape[1]
    nr = kc_ref.shape[0]
    t0 = pl.program_id(2) * qs
    sees_any = t0 + lax.broadcasted_iota(jnp.int32, (1, qs), 1) >= CMP_BLOCK - 1

    def cmp_branch(rows):
        m_c = (lax.broadcasted_iota(jnp.int32, (rows, qs), 0) * CMP_STRIDE + (CMP_BLOCK - 1)
               <= t0 + lax.broadcasted_iota(jnp.int32, (rows, qs), 1))
        kc = kc_ref[0:rows, :]
        vct = vct_ref[:, 0:rows]
        ovt = ovt_ref[:, 0:rows]
        imp = jnp.zeros((LANES, qs), F32)
        for h in range(N_HPG):
            s = jnp.dot(kc, qt_ref[h * LANES:(h + 1) * LANES, :], preferred_element_type=F32)
            s = jnp.where(m_c, s, NEG)
            e = jnp.exp2(s - jnp.max(s, axis=0, keepdims=True)).astype(BF16)
            r = jnp.dot(vct, e, preferred_element_type=F32)
            inv = jnp.where(sees_any, 1.0 / r[N_HEAD_DIM:N_HEAD_DIM + 1, :], 0.0)
            imp = imp + jnp.dot(ovt, e, preferred_element_type=F32) * inv
            oc_ref[:, h * LANES:(h + 1) * LANES] = (r * (inv * gt_ref[3 * h:3 * h + 1, :])).T
        imp_ref[...] = imp

    n_var = nr // CMP_ROW_CHUNK
    visible = (t0 + qs - CMP_BLOCK) // CMP_STRIDE + 1
    need = jnp.clip((visible + CMP_ROW_CHUNK - 1) // CMP_ROW_CHUNK, 1, n_var)
    for v in range(1, n_var + 1):
        pl.when(need == v)(functools.partial(cmp_branch, v * CMP_ROW_CHUNK))
    imp = imp_ref[...]

    blk = lax.broadcasted_iota(jnp.int32, (LANES, qs), 0)
    tq = t0 + lax.broadcasted_iota(jnp.int32, (LANES, qs), 1)
    cur = tq // SEL_BLOCK
    forced = (blk == 0) | (blk == cur) | (blk == cur - 1)
    n_sel = (nr * CMP_STRIDE) // SEL_BLOCK
    free_score = jnp.where(blk * SEL_BLOCK <= tq, imp, -1.0)
    blk_f = blk.astype(F32)
    n_top = min(SEL_TOPK, n_sel)
    n_forced = 3

    def pick(score, n_iter):
        score = jnp.where(blk < n_sel, score, -jnp.inf)
        for _ in range(n_iter):
            mx = jnp.max(score, axis=0, keepdims=True)
            idx = jnp.min(jnp.where(score == mx, blk_f, float(LANES)), axis=0, keepdims=True)
            score = jnp.where(blk_f == idx, -jnp.inf, score)
        bias_t = jnp.where((score == -jnp.inf) & (blk < n_sel), 0.0, NEG)
        bias_ref[...] = bias_t.T.astype(BF16)

    all_three_forced = (t0 >= 2 * SEL_BLOCK) & (n_top >= n_forced)

    @pl.when(all_three_forced)
    def _():
        pick(jnp.where(forced, -jnp.inf, free_score), n_top - n_forced)

    @pl.when(jnp.logical_not(all_three_forced))
    def _():
        pick(jnp.where(forced, FORCE, free_score), n_top)


def _select(qrawt, kc, vct, ovt, gates_t):
    b, g, qw, t = qrawt.shape
    nr = kc.shape[2]
    qs = 512

    def whole(rows, cols):
        return pl.BlockSpec((None, None, rows, cols), lambda bi, gi, i: (bi, gi, 0, 0))

    return pl.pallas_call(
        _select_kernel,
        out_shape=(jax.ShapeDtypeStruct((b, g, t, qw), F32), jax.ShapeDtypeStruct((b, g, t, LANES), BF16)),
        grid=(b, g, t // qs),
        in_specs=[
            pl.BlockSpec((None, None, qw, qs), lambda bi, gi, i: (bi, gi, 0, i)),
            whole(nr, LANES), whole(LANES, nr),
            pl.BlockSpec(ovt.shape, lambda bi, gi, i: (0, 0)),
            pl.BlockSpec((None, None, LANES, qs), lambda bi, gi, i: (bi, gi, 0, i)),
        ],
        out_specs=(pl.BlockSpec((None, None, qs, qw), lambda bi, gi, i: (bi, gi, i, 0)),
                   pl.BlockSpec((None, None, qs, LANES), lambda bi, gi, i: (bi, gi, i, 0))),
        scratch_shapes=[pltpu.VMEM((LANES, qs), F32)],
        compiler_params=_cparams(("parallel", "parallel", "parallel")),
        name="select",
    )(qrawt, kc, vct, ovt, gates_t)


def _nsa_kernel(qrope_ref, bias_ref, oc_ref, kst_ref, vs_ref, kwt_ref, vw_ref, gates_ref, nz_ref,
                yb_ref, qaug_ref, acc_ref, m_ref, s_ref):
    i = pl.program_id(1)
    t0 = i * Q_BLOCK
    Q = Q_BLOCK
    nh = N_HPG
    ng = N_KV_GROUPS
    rows = nh * Q
    t0a = pl.multiple_of(t0, Q)
    causal = (lax.broadcasted_iota(jnp.int32, (Q, Q), 1) <= lax.broadcasted_iota(jnp.int32, (Q, Q), 0))
    lane = lax.broadcasted_iota(jnp.int32, (Q, LANES), 1)

    def q_rope(g):
        return jnp.concatenate([qrope_ref[g, :, h * LANES:(h + 1) * LANES] for h in range(nh)], axis=0)

    def mask_heads(s, m):
        return jnp.concatenate([jnp.where(m, s[h * Q:(h + 1) * Q, :], NEG) for h in range(nh)], axis=0)

    first_blk = t0 // SEL_BLOCK
    for g in range(ng):
        qr = q_rope(g)
        bias = bias_ref[g]
        bias_early = jnp.where(lane < first_blk, bias, NEG).astype(BF16)
        qaug_ref[g, :, 0:LANES] = jnp.concatenate([bias_early] * nh, axis=0)
        qaug_ref[g, :, LANES:2 * LANES] = qr
        q_diag = jnp.concatenate([jnp.concatenate([bias] * nh, axis=0), qr], axis=1)
        s_d = jnp.dot(q_diag, kst_ref[g, :, pl.ds(t0a, Q)], preferred_element_type=F32)
        s_d = mask_heads(s_d, causal)
        m0 = jnp.max(s_d, axis=1, keepdims=True)
        p_d = jnp.exp2(s_d - m0)
        m_ref[g] = jnp.broadcast_to(m0, (rows, LANES))
        acc_ref[g] = jnp.dot(p_d.astype(BF16), vs_ref[g, pl.ds(t0a, Q), :], preferred_element_type=F32)

    ncb = SEL_CHUNK // LANES
    n_chunks = (t0 + SEL_CHUNK - 1) // SEL_CHUNK
    last = jnp.maximum(n_chunks - 1, 0)

    def scores(g, c, slot):
        k0 = pl.multiple_of(c * SEL_CHUNK, SEL_CHUNK)
        s_ref[slot, g] = jnp.dot(qaug_ref[g], kst_ref[g, :, pl.ds(k0, SEL_CHUNK)], preferred_element_type=F32)

    def consume(g, c, slot):
        k0 = pl.multiple_of(c * SEL_CHUNK, SEL_CHUNK)
        s = s_ref[slot, g]
        m_old = m_ref[g]
        m_new = jnp.maximum(m_old, jnp.max(s, axis=1, keepdims=True))
        p = jnp.concatenate([jnp.exp2(s[:, cb * LANES:(cb + 1) * LANES] - m_new) for cb in range(ncb)], axis=1)
        acc_ref[g] = (jnp.exp2(m_old - m_new) * acc_ref[g]
                      + jnp.dot(p.astype(BF16), vs_ref[g, pl.ds(k0, SEL_CHUNK), :], preferred_element_type=F32))
        m_ref[g] = m_new

    for g in range(ng):
        scores(g, 0, 0)

    def steps(c, n):
        for u in range(n):
            for g in range(ng):
                scores(g, jnp.minimum(c + u + 1, last), (u + 1) % 2)
                consume(g, c + u, u % 2)

    def sel_quad(c4, carry):
        steps(SEL_UNROLL * c4, SEL_UNROLL)
        return carry

    n_quads = n_chunks // SEL_UNROLL
    lax.fori_loop(0, n_quads, sel_quad, 0)
    rem = n_chunks - SEL_UNROLL * n_quads

    @pl.when(rem >= 2)
    def _():
        steps(SEL_UNROLL * n_quads, 2)

    @pl.when(rem % 2 == 1)
    def _():
        for g in range(ng):
            consume(g, n_chunks - 1, 0)

    wlen = WINDOW + Q
    ws = pl.multiple_of(jnp.maximum(t0 - WINDOW, 0), Q)
    diff = (t0 + lax.broadcasted_iota(jnp.int32, (Q, wlen), 0)) - (ws + lax.broadcasted_iota(jnp.int32, (Q, wlen), 1))
    m_w = (diff >= 0) & (diff < WINDOW)
    for g in range(ng):
        gates = gates_ref[g]

        def gate_col(br):
            return jnp.concatenate([gates[:, 3 * h + br:3 * h + br + 1] for h in range(nh)], axis=0)

        acc = acc_ref[g]
        o_s = acc * (gate_col(1) / acc[:, N_HEAD_DIM:N_HEAD_DIM + 1])
        s_w = jnp.dot(q_rope(g), kwt_ref[g, :, pl.ds(ws, wlen)], preferred_element_type=F32)
        s_w = mask_heads(s_w, m_w)
        p_w = jnp.exp2(s_w - jnp.max(s_w, axis=1, keepdims=True))
        acc_w = jnp.dot(p_w.astype(BF16), vw_ref[g, pl.ds(ws, wlen), :], preferred_element_type=F32)
        o_w = acc_w * (gate_col(2) / acc_w[:, N_HEAD_DIM:N_HEAD_DIM + 1])
        outs = []
        for h in range(nh):
            rs = slice(h * Q, (h + 1) * Q)
            outs.append(oc_ref[g, :, h * LANES:(h + 1) * LANES] + o_s[rs, :] + o_w[rs, :])
        for hp in range(nh // 2):
            yn = jnp.where(lane < N_HEAD_DIM, outs[2 * hp], pltpu.roll(outs[2 * hp + 1], N_HEAD_DIM, axis=1))
            ls = slice((g * (nh // 2) + hp) * LANES, (g * (nh // 2) + hp + 1) * LANES)
            yb_ref[:, ls] = (yn * _silu(nz_ref[:, ls].astype(F32))).astype(BF16)


def _nsa(p, b, t, qrope, bias, oc, kst, vs, kwt, vw, gates):
    g = N_KV_GROUPS
    nqb = t // Q_BLOCK
    qw = N_HPG * LANES
    rows = N_HPG * Q_BLOCK

    def qblk(w):
        return pl.BlockSpec((None, g, Q_BLOCK, w), lambda bi, i: (bi, 0, i, 0))

    def whole(r, c):
        return pl.BlockSpec((None, g, r, c), lambda bi, i: (bi, 0, 0, 0), pipeline_mode=pl.Buffered(1))

    return pl.pallas_call(
        _nsa_kernel,
        out_shape=jax.ShapeDtypeStruct((b * t, N_WIDTH), BF16),
        grid=(b, nqb),
        in_specs=[
            qblk(qw), qblk(LANES), qblk(qw),
            whole(2 * LANES, t), whole(t, LANES), whole(LANES, t), whole(t, LANES),
            qblk(LANES),
            pl.BlockSpec((Q_BLOCK, N_WIDTH), lambda bi, i: (bi * nqb + i, C_NZ // N_WIDTH)),
        ],
        out_specs=pl.BlockSpec((Q_BLOCK, N_WIDTH), lambda bi, i: (bi * nqb + i, 0)),
        scratch_shapes=[pltpu.VMEM((g, rows, 2 * LANES), BF16), pltpu.VMEM((g, rows, LANES), F32),
                        pltpu.VMEM((g, rows, LANES), F32), pltpu.VMEM((2, g, rows, SEL_CHUNK), F32)],
        compiler_params=_cparams(("parallel", "arbitrary")),
        name="nsa",
    )(qrope, bias, oc, kst, vs, kwt, vw, gates, p)


def _outproj_kernel(final, ya_ref, yb_ref, ga_ref, gb_ref, x_ref, wa_ref, wb_ref, wo_ref, fg_ref, o_ref):
    a = jnp.dot(ya_ref[...], wa_ref[...], preferred_element_type=F32)
    bq = jnp.dot(yb_ref[...], wb_ref[...], preferred_element_type=F32)
    merged = _sigmoid(ga_ref[...].astype(F32)) * a + _sigmoid(gb_ref[...].astype(F32)) * bq
    y = x_ref[...] + jnp.dot(merged.astype(BF16), wo_ref[...], preferred_element_type=F32)
    if final:
        y = y * lax.rsqrt(jnp.mean(y * y, axis=-1, keepdims=True) + NORM_EPS) * fg_ref[...]
    o_ref[...] = y


def _outproj(ya, yb, p, x2, wa, wb, wo, fg, final):
    m = x2.shape[0]
    tm = 1024

    def full(a):
        return pl.BlockSpec(a.shape, lambda i: (0, 0))

    return pl.pallas_call(
        functools.partial(_outproj_kernel, final),
        out_shape=jax.ShapeDtypeStruct((m, D_MODEL), F32),
        grid=(m // tm,),
        in_specs=[
            pl.BlockSpec((tm, M_WIDTH), lambda i: (i, 0)),
            pl.BlockSpec((tm, N_WIDTH), lambda i: (i, 0)),
            pl.BlockSpec((tm, D_MODEL), lambda i: (i, C_GA // D_MODEL)),
            pl.BlockSpec((tm, D_MODEL), lambda i: (i, C_GB // D_MODEL)),
            pl.BlockSpec((tm, D_MODEL), lambda i: (i, 0)),
            full(wa), full(wb), full(wo), full(fg),
        ],
        out_specs=pl.BlockSpec((tm, D_MODEL), lambda i: (i, 0)),
        compiler_params=_cparams(("parallel",)),
        name="outproj",
    )(ya, yb, p, p, x2, wa, wb, wo, fg)


def _permute_w_tail(w):
    pad = jnp.zeros((w.shape[0], P_W - _O_END), w.dtype)
    return jnp.concatenate([
        w[:, _O_GA:_O_END],
        w[:, _O_NQ:_O_KC],
        w[:, _O_NZ:_O_GA],
        w[:, _O_KS:_O_NG],
        w[:, _O_KC:_O_KS],
        w[:, _O_MI:_O_NQ],
        w[:, _O_NG:_O_NZ],
        pad], axis=1)


def _expand_cmp_w1(w1k, w1v):
    half = CMP_BLOCK // 2
    k3 = w1k.astype(BF16).reshape(CMP_BLOCK, 1, N_HEAD_DIM, 1, CMP_HIDDEN)
    v3 = w1v.astype(BF16).reshape(CMP_BLOCK, 1, N_HEAD_DIM, 1, CMP_HIDDEN)
    nc = 2 * N_KV_GROUPS
    eye = jnp.eye(nc, dtype=BF16)
    is_k = (jnp.arange(nc) < N_KV_GROUPS).astype(BF16)
    mk = (eye * is_k[:, None]).reshape(1, nc, 1, nc, 1)
    mv = (eye * (1 - is_k)[:, None]).reshape(1, nc, 1, nc, 1)
    z = k3 * mk + v3 * mv
    z = z.reshape(CMP_BLOCK * nc * N_HEAD_DIM, nc * CMP_HIDDEN)
    return z[:half * nc * N_HEAD_DIM], z[half * nc * N_HEAD_DIM:]


def kernel(x, positions, norm_g, w_in, conv_q_w, conv_q_b, conv_k_w, conv_k_b, b_igate, b_fgate, mh_norm_g,
           cmp_pos_k, cmp_w1_k, cmp_w2_k, cmp_pos_v, cmp_w1_v, cmp_w2_v, b_nsa_gate, w_branch_a, w_branch_b,
           w_out, final_norm_g):
    b, t, _ = x.shape
    assert t % (2 * SEL_CHUNK) == 0 and t >= WINDOW + Q_BLOCK and t // SEL_BLOCK <= LANES and t % MLSTM_CHUNK == 0
    x2 = x.reshape(b * t, D_MODEL)
    for l in range(norm_g.shape[0]):
        w_bf = w_in[l].astype(BF16)
        p, aux = _inproj(x2, norm_g[l][None, :], w_bf, _permute_w_tail(w_bf))

        sm_bias = jnp.zeros((1, LANES), F32).at[0, 0:M_HEADS].set(b_igate[l]).at[0, M_HEADS:2 * M_HEADS].set(b_fgate[l])
        ya = _mlstm(p, aux, b, t, conv_q_w[l], conv_q_b[l][None, :], conv_k_w[l], conv_k_b[l][None, :], sm_bias,
                    mh_norm_g[l][None, :])

        nr = t // CMP_STRIDE
        wt, wb = _expand_cmp_w1(cmp_w1_k[l], cmp_w1_v[l])
        w2pad = lambda w: jnp.pad(w, ((0, 0), (0, LANES - N_HEAD_DIM))).astype(BF16)
        kc, vct = _compress(aux, b, t, wt, wb, cmp_pos_k[l].reshape(1, -1), cmp_pos_v[l].reshape(1, -1),
                            cmp_w1_k[l].astype(BF16), cmp_w1_v[l].astype(BF16), w2pad(cmp_w2_k[l]), w2pad(cmp_w2_v[l]))

        half = ROPE_DIM // 2
        inv = jnp.power(jnp.float32(ROPE_THETA), -jnp.arange(half, dtype=F32) * (2.0 / ROPE_DIM))
        lih = jnp.arange(LANES) % N_HEAD_DIM
        inv_row = jnp.where(lih < ROPE_DIM, inv[lih % half], 0.0)[None, :].astype(F32)
        pos_col = positions.astype(F32)[:, None, :]
        gate_bias = jnp.zeros((1, LANES), F32).at[0, 8:8 + 3 * N_HEADS].set(b_nsa_gate[l])
        qrawt, qrope, kst, vs, kwt, vw, gates, gates_t = _kvprep(p, aux, b, t, pos_col, inv_row, gate_bias)

        ci = jnp.arange(nr)[None, :] * CMP_STRIDE
        sj = jnp.arange(LANES)[:, None] * SEL_BLOCK
        overlap_t = ((ci < sj + SEL_BLOCK) & (ci + CMP_BLOCK > sj) & (jnp.arange(nr)[None, :] < nr - 1)
                     & (jnp.arange(LANES)[:, None] < t // SEL_BLOCK)).astype(BF16)
        oc, bias = _select(qrawt, kc, vct, overlap_t, gates_t)
        yb = _nsa(p, b, t, qrope, bias, oc, kst, vs, kwt, vw, gates)

        x2 = _outproj(ya, yb, p, x2, w_branch_a[l].astype(BF16), w_branch_b[l].astype(BF16),
                      w_out[l].astype(BF16), final_norm_g[None, :], l == norm_g.shape[0] - 1)
    return x2.reshape(b, t, D_MODEL)
```

```python
import functools
import math

import jax
import jax.numpy as jnp
from jax import lax
from jax.experimental import pallas as pl
from jax.experimental.pallas import tpu as pltpu

F32 = jnp.float32
BF16 = jnp.bfloat16

D_MODEL = 1024
M_HEADS = 4
M_HEAD_DIM = 256
M_WIDTH = M_HEADS * M_HEAD_DIM
CONV_WIDTH = 4
N_HEADS = 8
N_KV_GROUPS = 2
N_HPG = N_HEADS // N_KV_GROUPS
N_HEAD_DIM = 64
N_WIDTH = N_HEADS * N_HEAD_DIM
CMP_BLOCK = 32
CMP_STRIDE = 16
CMP_HIDDEN = 128
SEL_BLOCK = 64
SEL_TOPK = 16
WINDOW = 512
Q_BLOCK = 256
ROPE_THETA = 500000.0
ROPE_DIM = N_HEAD_DIM // 4
NORM_EPS = 1e-6
NEG = -1e30
FORCE = 1e9

LANES = 128
SUBLANES = 8
MLSTM_CHUNK = 256
SEL_CHUNK = 512
SEL_UNROLL = 4
VMEM_LIMIT = 56 * 1024 * 1024

C_MQ, C_MK, C_MV, C_MO, C_MZ, C_GA, C_GB = 0, 1024, 2048, 3072, 4096, 5120, 6144
C_NQ, C_NZ = 7168, 7680
C_KV = 8192
C_CMP = 8704
C_SM = 8960
P_W = 9216
AUX_W = 3 * LANES
X_KC, X_VC, X_SM = 0, 1, 2
_O_MI, _O_NQ, _O_KC, _O_KS, _O_NG, _O_NZ, _O_GA, _O_END = 5120, 5128, 5640, 5896, 6408, 6432, 6944, 8992


def _cparams(sem):
    return pltpu.CompilerParams(dimension_semantics=sem, vmem_limit_bytes=VMEM_LIMIT)


def _sigmoid(x):
    return 0.5 * jnp.tanh(0.5 * x) + 0.5


def _silu(x):
    h = 0.5 * x
    return h + h * jnp.tanh(h)


def _log_sigmoid(x):
    return jnp.minimum(x, 0.0) - jnp.log1p(jnp.exp(-jnp.abs(x)))


PROJ_TN = 1024
AUX_TILE = C_CMP // PROJ_TN
AUX_OFF = C_CMP % PROJ_TN


N_HEAD_TILES = _O_MI // PROJ_TN


def _inproj_kernel(x_ref, g_ref, wh_ref, wt_ref, o_ref, aux_ref, hn_ref):
    j = pl.program_id(1)

    @pl.when(j == 0)
    def _():
        x = x_ref[...]
        r = lax.rsqrt(jnp.mean(x * x, axis=-1, keepdims=True) + NORM_EPS)
        hn_ref[...] = (x * r * g_ref[...]).astype(BF16)

    @pl.when(j < N_HEAD_TILES)
    def _():
        o_ref[...] = jnp.dot(hn_ref[...], wh_ref[...], preferred_element_type=F32).astype(BF16)

    @pl.when((j >= N_HEAD_TILES) & (j != AUX_TILE))
    def _():
        o_ref[...] = jnp.dot(hn_ref[...], wt_ref[...], preferred_element_type=F32).astype(BF16)

    @pl.when(j == AUX_TILE)
    def _():
        acc = jnp.dot(hn_ref[...], wt_ref[...], preferred_element_type=F32)
        o_ref[...] = acc.astype(BF16)
        aux_ref[...] = acc[:, AUX_OFF:AUX_OFF + AUX_W]


def _inproj(x2, norm_g, w_head, w_tail):
    m = x2.shape[0]
    tm, tn = 2048, PROJ_TN
    return pl.pallas_call(
        _inproj_kernel,
        out_shape=(jax.ShapeDtypeStruct((m, P_W), BF16), jax.ShapeDtypeStruct((m, AUX_W), F32)),
        grid=(m // tm, P_W // tn),
        in_specs=[
            pl.BlockSpec((tm, D_MODEL), lambda i, j: (i, 0)),
            pl.BlockSpec((1, D_MODEL), lambda i, j: (0, 0)),
            pl.BlockSpec((D_MODEL, tn), lambda i, j: (0, jnp.minimum(j, N_HEAD_TILES - 1))),
            pl.BlockSpec((D_MODEL, tn), lambda i, j: (0, jnp.maximum(j - N_HEAD_TILES, 0))),
        ],
        out_specs=(pl.BlockSpec((tm, tn), lambda i, j: (i, j)), pl.BlockSpec((tm, AUX_W), lambda i, j: (i, 0))),
        scratch_shapes=[pltpu.VMEM((tm, D_MODEL), BF16)],
        compiler_params=_cparams(("parallel", "arbitrary")),
        name="inproj",
    )(x2, norm_g, w_head, w_tail)


def _causal_conv_silu(u16, prev_ref, w_ref, b_ref):
    n, w = u16.shape
    u = u16.astype(F32)
    row = lax.broadcasted_iota(jnp.int32, (n, n), 0)
    col = lax.broadcasted_iota(jnp.int32, (n, n), 1)
    sub = lax.broadcasted_iota(jnp.int32, (SUBLANES, w), 0)
    prev = prev_ref[...]
    y = b_ref[...] + u * w_ref[CONV_WIDTH - 1:CONV_WIDTH, :]
    for d in range(1, CONV_WIDTH):
        shift = jnp.where(row - col == d, 1.0, 0.0).astype(BF16)
        sh = jnp.dot(shift, u16, preferred_element_type=F32)
        head = jnp.where(sub < d, pltpu.roll(prev, d, axis=0), sh[0:SUBLANES, :])
        sh = jnp.concatenate([head, sh[SUBLANES:, :]], axis=0)
        y = y + sh * w_ref[CONV_WIDTH - 1 - d:CONV_WIDTH - d, :]
    prev_ref[...] = u[n - SUBLANES:n, :]
    return _silu(y)


def _mlstm_kernel(q_ref, k_ref, v_ref, og_ref, z_ref, sm_ref, cqw_ref, cqb_ref, ckw_ref, ckb_ref,
                  sb_ref, ng_ref, ya_ref, qprev, kprev, c_state, n_state, m_state):
    L = q_ref.shape[0]
    d = M_HEAD_DIM

    @pl.when(pl.program_id(1) == 0)
    def _():
        qprev[...] = jnp.zeros_like(qprev)
        kprev[...] = jnp.zeros_like(kprev)
        c_state[...] = jnp.zeros_like(c_state)
        n_state[...] = jnp.zeros_like(n_state)
        m_state[...] = jnp.zeros_like(m_state)

    mq = _causal_conv_silu(q_ref[...], qprev, cqw_ref, cqb_ref)
    mk = _causal_conv_silu(k_ref[...], kprev, ckw_ref, ckb_ref) * (d ** -0.5)

    sm = sm_ref[...] + sb_ref[...]
    lf = _log_sigmoid(sm)
    row = lax.broadcasted_iota(jnp.int32, (L, L), 0)
    col = lax.broadcasted_iota(jnp.int32, (L, L), 1)
    causal = row >= col
    tril = causal.astype(F32)
    triu = (row <= col).astype(F32)
    a_col = jnp.dot(tril, lf, precision=lax.Precision.HIGHEST, preferred_element_type=F32)
    sm_t = sm.T[0:8, :]
    a_row = jnp.dot(_log_sigmoid(sm_t), triu, precision=lax.Precision.HIGHEST,
                    preferred_element_type=F32)

    for h in range(M_HEADS):
        hs = slice(h * d, (h + 1) * d)
        a_c = a_col[:, 4 + h:5 + h]
        li_c = sm[:, h:h + 1]
        a_r = a_row[4 + h:5 + h, :]
        li_r = sm_t[h:h + 1, :]
        m_prev = m_state[h:h + 1, 0:1]
        a_end = a_c[L - 1:L, :]

        log_d = jnp.where(causal, a_c - a_r + li_r, -jnp.inf)
        inter = a_c + m_prev
        m_t = jnp.maximum(inter, jnp.max(log_d, axis=1, keepdims=True))
        w_inter = jnp.exp(inter - m_t)
        dm = jnp.exp(log_d - m_t)

        qh = mq[:, hs]
        kh = mk[:, hs]
        qb = qh.astype(BF16)
        kb = kh.astype(BF16)
        vb = v_ref[:, hs]
        s = lax.dot_general(qb, kb, (((1,), (1,)), ((), ())), preferred_element_type=F32) * dm
        c_old = c_state[h]
        n_old = n_state[h:h + 1, :]
        num = (w_inter * jnp.dot(qb, c_old.astype(BF16), preferred_element_type=F32)
               + jnp.dot(s.astype(BF16), vb, preferred_element_type=F32))
        den = (w_inter * jnp.sum(qh * n_old, axis=1, keepdims=True)
               + jnp.sum(s, axis=1, keepdims=True))
        hh = num * (1.0 / jnp.maximum(jnp.abs(den), jnp.exp(-m_t)))

        logw = a_end - a_c + li_c
        m_new = jnp.maximum(a_end + m_prev, jnp.max(logw, axis=0, keepdims=True))
        wk = jnp.exp(logw - m_new)
        decay = jnp.exp(a_end + m_prev - m_new)
        kw = kh * wk
        c_state[h] = decay * c_old + lax.dot_general(kw.astype(BF16), vb, (((0,), (0,)), ((), ())),
                                                     preferred_element_type=F32)
        n_state[h:h + 1, :] = decay * n_old + jnp.sum(kw, axis=0, keepdims=True)
        m_state[h:h + 1, :] = jnp.broadcast_to(m_new, (1, LANES))

        mu = jnp.mean(hh, axis=1, keepdims=True)
        dc = hh - mu
        var = jnp.mean(dc * dc, axis=1, keepdims=True)
        hn = dc * lax.rsqrt(var + NORM_EPS) * ng_ref[:, hs]
        ya = _sigmoid(og_ref[:, hs].astype(F32)) * hn * _silu(z_ref[:, hs].astype(F32))
        ya_ref[:, hs] = ya.astype(BF16)


def _mlstm(p, aux, b, t, conv_q_w, conv_q_b, conv_k_w, conv_k_b, sm_bias, mh_norm_g):
    L = MLSTM_CHUNK
    nc = t // L

    def colspec(cstart):
        return pl.BlockSpec((L, M_WIDTH), lambda bi, c: (bi * nc + c, cstart // M_WIDTH))

    def full(shape):
        return pl.BlockSpec(shape, lambda bi, c: (0,) * len(shape))

    return pl.pallas_call(
        _mlstm_kernel,
        out_shape=jax.ShapeDtypeStruct((b * t, M_WIDTH), BF16),
        grid=(b, nc),
        in_specs=[
            colspec(C_MQ), colspec(C_MK), colspec(C_MV), colspec(C_MO), colspec(C_MZ),
            pl.BlockSpec((L, LANES), lambda bi, c: (bi * nc + c, X_SM)),
            full((CONV_WIDTH, M_WIDTH)), full((1, M_WIDTH)), full((CONV_WIDTH, M_WIDTH)), full((1, M_WIDTH)),
            full((1, LANES)), full((1, M_WIDTH)),
        ],
        out_specs=pl.BlockSpec((L, M_WIDTH), lambda bi, c: (bi * nc + c, 0)),
        scratch_shapes=[
            pltpu.VMEM((SUBLANES, M_WIDTH), F32), pltpu.VMEM((SUBLANES, M_WIDTH), F32),
            pltpu.VMEM((M_HEADS, M_HEAD_DIM, M_HEAD_DIM), F32),
            pltpu.VMEM((8, M_HEAD_DIM), F32), pltpu.VMEM((8, LANES), F32),
        ],
        compiler_params=_cparams(("parallel", "arbitrary")),
        name="mlstm",
    )(p, p, p, p, p, aux, conv_q_w, conv_q_b, conv_k_w, conv_k_b, sm_bias, mh_norm_g)


def _compress_kernel(pk_ref, pv_ref, wt_ref, wb_ref, posk_ref, posv_ref, w1k_ref, w1v_ref, w2k_ref, w2v_ref,
                     kc_ref, vct_ref):
    nr = pk_ref.shape[0] // CMP_STRIDE
    cw = 2 * LANES
    u = jnp.zeros((nr, wt_ref.shape[1]), F32)
    v = jnp.zeros((nr, wb_ref.shape[1]), F32)
    for l in range(CMP_STRIDE):
        x = jnp.concatenate([pk_ref[pl.ds(l, nr, stride=CMP_STRIDE), :],
                             pv_ref[pl.ds(l, nr, stride=CMP_STRIDE), :]], axis=1).astype(BF16)
        u = u + jnp.dot(x, wt_ref[l * cw:(l + 1) * cw, :], preferred_element_type=F32)
        v = v + jnp.dot(x, wb_ref[l * cw:(l + 1) * cw, :], preferred_element_type=F32)
    v = pltpu.roll(v, nr - 1, axis=0)
    rowi = lax.broadcasted_iota(jnp.int32, v.shape, 0)
    v = jnp.where(rowi < nr - 1, v, 0.0)

    def pos_term(pos_ref, w_ref):
        pb = jnp.broadcast_to(pos_ref[...].astype(BF16), (8, CMP_BLOCK * N_HEAD_DIM))
        return jnp.dot(pb, w_ref[...], preferred_element_type=F32)[0:1, :]

    pk = pos_term(posk_ref, w1k_ref)
    pv = pos_term(posv_ref, w1v_ref)
    hid = u + v + jnp.concatenate([pk, pk, pv, pv], axis=1)
    act = 0.5 * hid * (1.0 + jnp.tanh(math.sqrt(2.0 / math.pi) * (hid + 0.044715 * (hid * hid * hid))))
    for c in range(2 * N_KV_GROUPS):
        w2 = w2k_ref[...] if c < N_KV_GROUPS else w2v_ref[...]
        out = jnp.dot(act[:, c * CMP_HIDDEN:(c + 1) * CMP_HIDDEN].astype(BF16), w2,
                      preferred_element_type=F32)
        if c < N_KV_GROUPS:
            kc_ref[c] = out.astype(BF16)
        else:
            out_t = out.T
            ones_row = lax.broadcasted_iota(jnp.int32, out_t.shape, 0) == N_HEAD_DIM
            vct_ref[c - N_KV_GROUPS] = jnp.where(ones_row, 1.0, out_t).astype(BF16)


def _compress(aux, b, t, wt, wb, posk, posv, w1k, w1v, w2k, w2v):
    nr = t // CMP_STRIDE
    g = N_KV_GROUPS

    def full(a):
        return pl.BlockSpec(a.shape, lambda bi: (0,) * a.ndim)

    return pl.pallas_call(
        _compress_kernel,
        out_shape=(jax.ShapeDtypeStruct((b, g, nr, LANES), BF16), jax.ShapeDtypeStruct((b, g, LANES, nr), BF16)),
        grid=(b,),
        in_specs=[pl.BlockSpec((t, LANES), lambda bi: (bi, X_KC)),
                  pl.BlockSpec((t, LANES), lambda bi: (bi, X_VC)),
                  full(wt), full(wb), full(posk), full(posv), full(w1k), full(w1v), full(w2k), full(w2v)],
        out_specs=(pl.BlockSpec((None, g, nr, LANES), lambda bi: (bi, 0, 0, 0)),
                   pl.BlockSpec((None, g, LANES, nr), lambda bi: (bi, 0, 0, 0))),
        compiler_params=_cparams(("parallel",)),
        name="compress",
    )(aux, aux, wt, wb, posk, posv, w1k, w1v, w2k, w2v)


def _kvprep_kernel(q_ref, kv_ref, sm_ref, pos_ref, inv_ref, gb_ref,
                   qrawt_ref, qrope_ref, kst_ref, vs_ref, kwt_ref, vw_ref, gates_ref, gatest_ref):
    tk = q_ref.shape[0]
    lane = lax.broadcasted_iota(jnp.int32, (tk, LANES), 1)
    lih = lane % N_HEAD_DIM
    pos = jnp.broadcast_to(pos_ref[...], (LANES, tk)).T
    ang = pos * inv_ref[...]
    cos = jnp.cos(ang)
    sin = jnp.sin(ang)
    half = ROPE_DIM // 2
    s_up = jnp.where(lih < half, -sin, 0.0)
    s_dn = jnp.where((lih >= half) & (lih < ROPE_DIM), sin, 0.0)

    def rope(u):
        return u * cos + pltpu.roll(u, LANES - half, axis=1) * s_up + pltpu.roll(u, half, axis=1) * s_dn

    def head_pad(u, odd):
        if odd:
            u = pltpu.roll(u, N_HEAD_DIM, axis=1)
        return jnp.where(lane < N_HEAD_DIM, u, 0.0)

    scale = N_HEAD_DIM ** -0.5 * math.log2(math.e)
    zeros_t = jnp.zeros((N_HEAD_DIM, tk), BF16)
    for g in range(N_KV_GROUPS):
        for hp in range(N_HPG // 2):
            c0 = g * (N_HPG * N_HEAD_DIM) + hp * LANES
            u = q_ref[:, c0:c0 + LANES].astype(F32) * scale
            ur = rope(u)
            u_t = u.T.astype(BF16)
            for odd in range(2):
                h = 2 * hp + odd
                qrawt_ref[g, h * LANES:h * LANES + N_HEAD_DIM, :] = u_t[odd * N_HEAD_DIM:(odd + 1) * N_HEAD_DIM, :]
                qrawt_ref[g, h * LANES + N_HEAD_DIM:(h + 1) * LANES, :] = zeros_t
                qrope_ref[g, :, h * LANES:(h + 1) * LANES] = head_pad(ur, odd).astype(BF16)

    ks_t = rope(kv_ref[:, 0:LANES].astype(F32)).T
    kw_t = rope(kv_ref[:, 2 * LANES:3 * LANES].astype(F32)).T
    vs = kv_ref[:, LANES:2 * LANES].astype(F32)
    vw = kv_ref[:, 3 * LANES:4 * LANES].astype(F32)
    t_glob = pl.program_id(1) * tk + lax.broadcasted_iota(jnp.int32, (LANES, tk), 1)
    blk_row = lax.broadcasted_iota(jnp.int32, (LANES, tk), 0)
    onehot = jnp.where(blk_row == t_glob // SEL_BLOCK, 1.0, 0.0).astype(BF16)
    ones_lane = jnp.where(lane == N_HEAD_DIM, 1.0, 0.0)
    gates = _sigmoid(sm_ref[...] + gb_ref[...])
    for g in range(N_KV_GROUPS):
        rs = slice(g * N_HEAD_DIM, (g + 1) * N_HEAD_DIM)
        kst_ref[g, 0:LANES, :] = onehot
        kst_ref[g, LANES:LANES + N_HEAD_DIM, :] = ks_t[rs, :].astype(BF16)
        kst_ref[g, LANES + N_HEAD_DIM:2 * LANES, :] = zeros_t
        kwt_ref[g, 0:N_HEAD_DIM, :] = kw_t[rs, :].astype(BF16)
        kwt_ref[g, N_HEAD_DIM:LANES, :] = zeros_t
        vs_ref[g] = (head_pad(vs, g) + ones_lane).astype(BF16)
        vw_ref[g] = (head_pad(vw, g) + ones_lane).astype(BF16)
        goff = 8 + g * (3 * N_HPG)
        gg = pltpu.roll(gates, LANES - goff, axis=1)
        gates_ref[g] = gg
        gatest_ref[g] = gg.T


def _kvprep(p, aux, b, t, pos_col, inv_row, gate_bias):
    tk = 1024
    nt = t // tk
    g = N_KV_GROUPS
    qw = N_HPG * LANES

    def tok(shape):
        return pl.BlockSpec((None, g) + shape, lambda bi, i: (bi, 0, i, 0))

    def tok_t(rows):
        return pl.BlockSpec((None, g, rows, tk), lambda bi, i: (bi, 0, 0, i))

    return pl.pallas_call(
        _kvprep_kernel,
        out_shape=(
            jax.ShapeDtypeStruct((b, g, qw, t), BF16), jax.ShapeDtypeStruct((b, g, t, qw), BF16),
            jax.ShapeDtypeStruct((b, g, 2 * LANES, t), BF16), jax.ShapeDtypeStruct((b, g, t, LANES), BF16),
            jax.ShapeDtypeStruct((b, g, LANES, t), BF16), jax.ShapeDtypeStruct((b, g, t, LANES), BF16),
            jax.ShapeDtypeStruct((b, g, t, LANES), F32), jax.ShapeDtypeStruct((b, g, LANES, t), F32),
        ),
        grid=(b, nt),
        in_specs=[
            pl.BlockSpec((tk, N_WIDTH), lambda bi, i: (bi * nt + i, C_NQ // N_WIDTH)),
            pl.BlockSpec((tk, 4 * LANES), lambda bi, i: (bi * nt + i, C_KV // (4 * LANES))),
            pl.BlockSpec((tk, LANES), lambda bi, i: (bi * nt + i, X_SM)),
            pl.BlockSpec((None, 1, tk), lambda bi, i: (bi, 0, i)),
            pl.BlockSpec((1, LANES), lambda bi, i: (0, 0)),
            pl.BlockSpec((1, LANES), lambda bi, i: (0, 0)),
        ],
        out_specs=(tok_t(qw), tok((tk, qw)), tok_t(2 * LANES), tok((tk, LANES)),
                   tok_t(LANES), tok((tk, LANES)), tok((tk, LANES)), tok_t(LANES)),
        compiler_params=_cparams(("parallel", "arbitrary")),
        name="kvprep",
    )(p, p, aux, pos_col, inv_row, gate_bias)


CMP_ROW_CHUNK = 128


def _select_kernel(qt_ref, kc_ref, vct_ref, ovt_ref, gt_ref, oc_ref, bias_ref, imp_ref):
    qs = qt_ref.shape[1]
    nr = kc_ref.shape[0]
    t0 = pl.program_id(2) * qs
    sees_any = t0 + lax.broadcasted_iota(jnp.int32, (1, qs), 1) >= CMP_BLOCK - 1

    def cmp_branch(rows):
        m_c = (lax.broadcasted_iota(jnp.int32, (rows, qs), 0) * CMP_STRIDE + (CMP_BLOCK - 1)
               <= t0 + lax.broadcasted_iota(jnp.int32, (rows, qs), 1))
        kc = kc_ref[0:rows, :]
        vct = vct_ref[:, 0:rows]
        ovt = ovt_ref[:, 0:rows]
        imp = jnp.zeros((LANES, qs), F32)
        for h in range(N_HPG):
            s = jnp.dot(kc, qt_ref[h * LANES:(h + 1) * LANES, :], preferred_element_type=F32)
            s = jnp.where(m_c, s, NEG)
            e = jnp.exp2(s - jnp.max(s, axis=0, keepdims=True)).astype(BF16)
            r = jnp.dot(vct, e, preferred_element_type=F32)
            inv = jnp.where(sees_any, 1.0 / r[N_HEAD_DIM:N_HEAD_DIM + 1, :], 0.0)
            imp = imp + jnp.dot(ovt, e, preferred_element_type=F32) * inv
            oc_ref[:, h * LANES:(h + 1) * LANES] = (r * (inv * gt_ref[3 * h:3 * h + 1, :])).T
        imp_ref[...] = imp

    n_var = nr // CMP_ROW_CHUNK
    visible = (t0 + qs - CMP_BLOCK) // CMP_STRIDE + 1
    need = jnp.clip((visible + CMP_ROW_CHUNK - 1) // CMP_ROW_CHUNK, 1, n_var)
    for v in range(1, n_var + 1):
        pl.when(need == v)(functools.partial(cmp_branch, v * CMP_ROW_CHUNK))
    imp = imp_ref[...]

    blk = lax.broadcasted_iota(jnp.int32, (LANES, qs), 0)
    tq = t0 + lax.broadcasted_iota(jnp.int32, (LANES, qs), 1)
    cur = tq // SEL_BLOCK
    forced = (blk == 0) | (blk == cur) | (blk == cur - 1)
    n_sel = (nr * CMP_STRIDE) // SEL_BLOCK
    free_score = jnp.where(blk * SEL_BLOCK <= tq, imp, -1.0)
    blk_f = blk.astype(F32)
    n_top = min(SEL_TOPK, n_sel)
    n_forced = 3

    def pick(score, n_iter):
        score = jnp.where(blk < n_sel, score, -jnp.inf)
        for _ in range(n_iter):
            mx = jnp.max(score, axis=0, keepdims=True)
            idx = jnp.min(jnp.where(score == mx, blk_f, float(LANES)), axis=0, keepdims=True)
            score = jnp.where(blk_f == idx, -jnp.inf, score)
        bias_t = jnp.where((score == -jnp.inf) & (blk < n_sel), 0.0, NEG)
        bias_ref[...] = bias_t.T.astype(BF16)

    all_three_forced = (t0 >= 2 * SEL_BLOCK) & (n_top >= n_forced)

    @pl.when(all_three_forced)
    def _():
        pick(jnp.where(forced, -jnp.inf, free_score), n_top - n_forced)

    @pl.when(jnp.logical_not(all_three_forced))
    def _():
        pick(jnp.where(forced, FORCE, free_score), n_top)


def _select(qrawt, kc, vct, ovt, gates_t):
    b, g, qw, t = qrawt.shape
    nr = kc.shape[2]
    qs = 1024

    def whole(rows, cols):
        return pl.BlockSpec((None, None, rows, cols), lambda bi, gi, i: (bi, gi, 0, 0))

    return pl.pallas_call(
        _select_kernel,
        out_shape=(jax.ShapeDtypeStruct((b, g, t, qw), F32), jax.ShapeDtypeStruct((b, g, t, LANES), BF16)),
        grid=(b, g, t // qs),
        in_specs=[
            pl.BlockSpec((None, None, qw, qs), lambda bi, gi, i: (bi, gi, 0, i)),
            whole(nr, LANES), whole(LANES, nr),
            pl.BlockSpec(ovt.shape, lambda bi, gi, i: (0, 0)),
            pl.BlockSpec((None, None, LANES, qs), lambda bi, gi, i: (bi, gi, 0, i)),
        ],
        out_specs=(pl.BlockSpec((None, None, qs, qw), lambda bi, gi, i: (bi, gi, i, 0)),
                   pl.BlockSpec((None, None, qs, LANES), lambda bi, gi, i: (bi, gi, i, 0))),
        scratch_shapes=[pltpu.VMEM((LANES, qs), F32)],
        compiler_params=_cparams(("parallel", "parallel", "parallel")),
        name="select",
    )(qrawt, kc, vct, ovt, gates_t)


def _nsa_kernel(qrope_ref, bias_ref, oc_ref, kst_ref, vs_ref, kwt_ref, vw_ref, gates_ref, nz_ref,
                yb_ref, qaug_ref, acc_ref, m_ref, s_ref):
    i = pl.program_id(1)
    t0 = i * Q_BLOCK
    Q = Q_BLOCK
    nh = N_HPG
    ng = N_KV_GROUPS
    rows = nh * Q
    t0a = pl.multiple_of(t0, Q)
    causal = (lax.broadcasted_iota(jnp.int32, (Q, Q), 1) <= lax.broadcasted_iota(jnp.int32, (Q, Q), 0))
    lane = lax.broadcasted_iota(jnp.int32, (Q, LANES), 1)

    def q_rope(g):
        return jnp.concatenate([qrope_ref[g, :, h * LANES:(h + 1) * LANES] for h in range(nh)], axis=0)

    def mask_heads(s, m):
        return jnp.concatenate([jnp.where(m, s[h * Q:(h + 1) * Q, :], NEG) for h in range(nh)], axis=0)

    first_blk = t0 // SEL_BLOCK
    for g in range(ng):
        qr = q_rope(g)
        bias = bias_ref[g]
        bias_early = jnp.where(lane < first_blk, bias, NEG).astype(BF16)
        qaug_ref[g, :, 0:LANES] = jnp.concatenate([bias_early] * nh, axis=0)
        qaug_ref[g, :, LANES:2 * LANES] = qr
        q_diag = jnp.concatenate([jnp.concatenate([bias] * nh, axis=0), qr], axis=1)
        s_d = jnp.dot(q_diag, kst_ref[g, :, pl.ds(t0a, Q)], preferred_element_type=F32)
        s_d = mask_heads(s_d, causal)
        m0 = jnp.max(s_d, axis=1, keepdims=True)
        p_d = jnp.exp2(s_d - m0)
        m_ref[g] = jnp.broadcast_to(m0, (rows, LANES))
        acc_ref[g] = jnp.dot(p_d.astype(BF16), vs_ref[g, pl.ds(t0a, Q), :], preferred_element_type=F32)

    ncb = SEL_CHUNK // LANES
    n_chunks = (t0 + SEL_CHUNK - 1) // SEL_CHUNK
    last = jnp.maximum(n_chunks - 1, 0)

    def scores(g, c, slot):
        k0 = pl.multiple_of(c * SEL_CHUNK, SEL_CHUNK)
        s_ref[slot, g] = jnp.dot(qaug_ref[g], kst_ref[g, :, pl.ds(k0, SEL_CHUNK)], preferred_element_type=F32)

    def consume(g, c, slot):
        k0 = pl.multiple_of(c * SEL_CHUNK, SEL_CHUNK)
        s = s_ref[slot, g]
        m_old = m_ref[g]
        m_new = jnp.maximum(m_old, jnp.max(s, axis=1, keepdims=True))
        p = jnp.concatenate([jnp.exp2(s[:, cb * LANES:(cb + 1) * LANES] - m_new) for cb in range(ncb)], axis=1)
        acc_ref[g] = (jnp.exp2(m_old - m_new) * acc_ref[g]
                      + jnp.dot(p.astype(BF16), vs_ref[g, pl.ds(k0, SEL_CHUNK), :], preferred_element_type=F32))
        m_ref[g] = m_new

    for g in range(ng):
        scores(g, 0, 0)

    def steps(c, n):
        for u in range(n):
            for g in range(ng):
                scores(g, jnp.minimum(c + u + 1, last), (u + 1) % 2)
                consume(g, c + u, u % 2)

    def sel_quad(c4, carry):
        steps(SEL_UNROLL * c4, SEL_UNROLL)
        return carry

    n_quads = n_chunks // SEL_UNROLL
    lax.fori_loop(0, n_quads, sel_quad, 0)
    rem = n_chunks - SEL_UNROLL * n_quads

    @pl.when(rem >= 2)
    def _():
        steps(SEL_UNROLL * n_quads, 2)

    @pl.when(rem % 2 == 1)
    def _():
        for g in range(ng):
            consume(g, n_chunks - 1, 0)

    wlen = WINDOW + Q
    ws = pl.multiple_of(jnp.maximum(t0 - WINDOW, 0), Q)
    diff = (t0 + lax.broadcasted_iota(jnp.int32, (Q, wlen), 0)) - (ws + lax.broadcasted_iota(jnp.int32, (Q, wlen), 1))
    m_w = (diff >= 0) & (diff < WINDOW)
    for g in range(ng):
        gates = gates_ref[g]

        def gate_col(br):
            return jnp.concatenate([gates[:, 3 * h + br:3 * h + br + 1] for h in range(nh)], axis=0)

        acc = acc_ref[g]
        o_s = acc * (gate_col(1) / acc[:, N_HEAD_DIM:N_HEAD_DIM + 1])
        s_w = jnp.dot(q_rope(g), kwt_ref[g, :, pl.ds(ws, wlen)], preferred_element_type=F32)
        s_w = mask_heads(s_w, m_w)
        p_w = jnp.exp2(s_w - jnp.max(s_w, axis=1, keepdims=True))
        acc_w = jnp.dot(p_w.astype(BF16), vw_ref[g, pl.ds(ws, wlen), :], preferred_element_type=F32)
        o_w = acc_w * (gate_col(2) / acc_w[:, N_HEAD_DIM:N_HEAD_DIM + 1])
        outs = []
        for h in range(nh):
            rs = slice(h * Q, (h + 1) * Q)
            outs.append(oc_ref[g, :, h * LANES:(h + 1) * LANES] + o_s[rs, :] + o_w[rs, :])
        for hp in range(nh // 2):
            yn = jnp.where(lane < N_HEAD_DIM, outs[2 * hp], pltpu.roll(outs[2 * hp + 1], N_HEAD_DIM, axis=1))
            ls = slice((g * (nh // 2) + hp) * LANES, (g * (nh // 2) + hp + 1) * LANES)
            yb_ref[:, ls] = (yn * _silu(nz_ref[:, ls].astype(F32))).astype(BF16)


def _nsa(p, b, t, qrope, bias, oc, kst, vs, kwt, vw, gates):
    g = N_KV_GROUPS
    nqb = t // Q_BLOCK
    qw = N_HPG * LANES
    rows = N_HPG * Q_BLOCK

    def qblk(w):
        return pl.BlockSpec((None, g, Q_BLOCK, w), lambda bi, i: (bi, 0, i, 0))

    def whole(r, c):
        return pl.BlockSpec((None, g, r, c), lambda bi, i: (bi, 0, 0, 0), pipeline_mode=pl.Buffered(1))

    return pl.pallas_call(
        _nsa_kernel,
        out_shape=jax.ShapeDtypeStruct((b * t, N_WIDTH), BF16),
        grid=(b, nqb),
        in_specs=[
            qblk(qw), qblk(LANES), qblk(qw),
            whole(2 * LANES, t), whole(t, LANES), whole(LANES, t), whole(t, LANES),
            qblk(LANES),
            pl.BlockSpec((Q_BLOCK, N_WIDTH), lambda bi, i: (bi * nqb + i, C_NZ // N_WIDTH)),
        ],
        out_specs=pl.BlockSpec((Q_BLOCK, N_WIDTH), lambda bi, i: (bi * nqb + i, 0)),
        scratch_shapes=[pltpu.VMEM((g, rows, 2 * LANES), BF16), pltpu.VMEM((g, rows, LANES), F32),
                        pltpu.VMEM((g, rows, LANES), F32), pltpu.VMEM((2, g, rows, SEL_CHUNK), F32)],
        compiler_params=_cparams(("parallel", "arbitrary")),
        name="nsa",
    )(qrope, bias, oc, kst, vs, kwt, vw, gates, p)


def _outproj_kernel(final, ya_ref, yb_ref, ga_ref, gb_ref, x_ref, wa_ref, wb_ref, wo_ref, fg_ref, o_ref):
    a = jnp.dot(ya_ref[...], wa_ref[...], preferred_element_type=F32)
    bq = jnp.dot(yb_ref[...], wb_ref[...], preferred_element_type=F32)
    merged = _sigmoid(ga_ref[...].astype(F32)) * a + _sigmoid(gb_ref[...].astype(F32)) * bq
    y = x_ref[...] + jnp.dot(merged.astype(BF16), wo_ref[...], preferred_element_type=F32)
    if final:
        y = y * lax.rsqrt(jnp.mean(y * y, axis=-1, keepdims=True) + NORM_EPS) * fg_ref[...]
    o_ref[...] = y


def _outproj(ya, yb, p, x2, wa, wb, wo, fg, final):
    m = x2.shape[0]
    tm = 1024

    def full(a):
        return pl.BlockSpec(a.shape, lambda i: (0, 0))

    return pl.pallas_call(
        functools.partial(_outproj_kernel, final),
        out_shape=jax.ShapeDtypeStruct((m, D_MODEL), F32),
        grid=(m // tm,),
        in_specs=[
            pl.BlockSpec((tm, M_WIDTH), lambda i: (i, 0)),
            pl.BlockSpec((tm, N_WIDTH), lambda i: (i, 0)),
            pl.BlockSpec((tm, D_MODEL), lambda i: (i, C_GA // D_MODEL)),
            pl.BlockSpec((tm, D_MODEL), lambda i: (i, C_GB // D_MODEL)),
            pl.BlockSpec((tm, D_MODEL), lambda i: (i, 0)),
            full(wa), full(wb), full(wo), full(fg),
        ],
        out_specs=pl.BlockSpec((tm, D_MODEL), lambda i: (i, 0)),
        compiler_params=_cparams(("parallel",)),
        name="outproj",
    )(ya, yb, p, p, x2, wa, wb, wo, fg)


def _permute_w_tail(w):
    pad = jnp.zeros((w.shape[0], P_W - _O_END), w.dtype)
    return jnp.concatenate([
        w[:, _O_GA:_O_END],
        w[:, _O_NQ:_O_KC],
        w[:, _O_NZ:_O_GA],
        w[:, _O_KS:_O_NG],
        w[:, _O_KC:_O_KS],
        w[:, _O_MI:_O_NQ],
        w[:, _O_NG:_O_NZ],
        pad], axis=1)


def _expand_cmp_w1(w1k, w1v):
    half = CMP_BLOCK // 2
    k3 = w1k.astype(BF16).reshape(CMP_BLOCK, 1, N_HEAD_DIM, 1, CMP_HIDDEN)
    v3 = w1v.astype(BF16).reshape(CMP_BLOCK, 1, N_HEAD_DIM, 1, CMP_HIDDEN)
    nc = 2 * N_KV_GROUPS
    eye = jnp.eye(nc, dtype=BF16)
    is_k = (jnp.arange(nc) < N_KV_GROUPS).astype(BF16)
    mk = (eye * is_k[:, None]).reshape(1, nc, 1, nc, 1)
    mv = (eye * (1 - is_k)[:, None]).reshape(1, nc, 1, nc, 1)
    z = k3 * mk + v3 * mv
    z = z.reshape(CMP_BLOCK * nc * N_HEAD_DIM, nc * CMP_HIDDEN)
    return z[:half * nc * N_HEAD_DIM], z[half * nc * N_HEAD_DIM:]


def kernel(x, positions, norm_g, w_in, conv_q_w, conv_q_b, conv_k_w, conv_k_b, b_igate, b_fgate, mh_norm_g,
           cmp_pos_k, cmp_w1_k, cmp_w2_k, cmp_pos_v, cmp_w1_v, cmp_w2_v, b_nsa_gate, w_branch_a, w_branch_b,
           w_out, final_norm_g):
    b, t, _ = x.shape
    assert t % (2 * SEL_CHUNK) == 0 and t >= WINDOW + Q_BLOCK and t // SEL_BLOCK <= LANES and t % MLSTM_CHUNK == 0
    x2 = x.reshape(b * t, D_MODEL)
    for l in range(norm_g.shape[0]):
        w_bf = w_in[l].astype(BF16)
        p, aux = _inproj(x2, norm_g[l][None, :], w_bf, _permute_w_tail(w_bf))

        sm_bias = jnp.zeros((1, LANES), F32).at[0, 0:M_HEADS].set(b_igate[l]).at[0, M_HEADS:2 * M_HEADS].set(b_fgate[l])
        ya = _mlstm(p, aux, b, t, conv_q_w[l], conv_q_b[l][None, :], conv_k_w[l], conv_k_b[l][None, :], sm_bias,
                    mh_norm_g[l][None, :])

        nr = t // CMP_STRIDE
        wt, wb = _expand_cmp_w1(cmp_w1_k[l], cmp_w1_v[l])
        w2pad = lambda w: jnp.pad(w, ((0, 0), (0, LANES - N_HEAD_DIM))).astype(BF16)
        kc, vct = _compress(aux, b, t, wt, wb, cmp_pos_k[l].reshape(1, -1), cmp_pos_v[l].reshape(1, -1),
                            cmp_w1_k[l].astype(BF16), cmp_w1_v[l].astype(BF16), w2pad(cmp_w2_k[l]), w2pad(cmp_w2_v[l]))

        half = ROPE_DIM // 2
        inv = jnp.power(jnp.float32(ROPE_THETA), -jnp.arange(half, dtype=F32) * (2.0 / ROPE_DIM))
        lih = jnp.arange(LANES) % N_HEAD_DIM
        inv_row = jnp.where(lih < ROPE_DIM, inv[lih % half], 0.0)[None, :].astype(F32)
        pos_col = positions.astype(F32)[:, None, :]
        gate_bias = jnp.zeros((1, LANES), F32).at[0, 8:8 + 3 * N_HEADS].set(b_nsa_gate[l])
        qrawt, qrope, kst, vs, kwt, vw, gates, gates_t = _kvprep(p, aux, b, t, pos_col, inv_row, gate_bias)

        ci = jnp.arange(nr)[None, :] * CMP_STRIDE
        sj = jnp.arange(LANES)[:, None] * SEL_BLOCK
        overlap_t = ((ci < sj + SEL_BLOCK) & (ci + CMP_BLOCK > sj) & (jnp.arange(nr)[None, :] < nr - 1)
                     & (jnp.arange(LANES)[:, None] < t // SEL_BLOCK)).astype(BF16)
        oc, bias = _select(qrawt, kc, vct, overlap_t, gates_t)
        yb = _nsa(p, b, t, qrope, bias, oc, kst, vs, kwt, vw, gates)

        x2 = _outproj(ya, yb, p, x2, w_branch_a[l].astype(BF16), w_branch_b[l].astype(BF16),
                      w_out[l].astype(BF16), final_norm_g[None, :], l == norm_g.shape[0] - 1)
    return x2.reshape(b, t, D_MODEL)
```

```python
import functools
import math

import jax
import jax.numpy as jnp
from jax import lax
from jax.experimental import pallas as pl
from jax.experimental.pallas import tpu as pltpu

F32 = jnp.float32
BF16 = jnp.bfloat16

D_MODEL = 1024
M_HEADS = 4
M_HEAD_DIM = 256
M_WIDTH = M_HEADS * M_HEAD_DIM
CONV_WIDTH = 4
N_HEADS = 8
N_KV_GROUPS = 2
N_HPG = N_HEADS // N_KV_GROUPS
N_HEAD_DIM = 64
N_WIDTH = N_HEADS * N_HEAD_DIM
CMP_BLOCK = 32
CMP_STRIDE = 16
CMP_HIDDEN = 128
SEL_BLOCK = 64
SEL_TOPK = 16
WINDOW = 512
Q_BLOCK = 256
ROPE_THETA = 500000.0
ROPE_DIM = N_HEAD_DIM // 4
NORM_EPS = 1e-6
NEG = -1e30
FORCE = 1e9

LANES = 128
SUBLANES = 8
MLSTM_CHUNK = 256
MLSTM_CHUNKS_PER_STEP = 4
SEL_CHUNK = 512
SEL_UNROLL = 4
VMEM_LIMIT = 56 * 1024 * 1024
INPROJ_VMEM_LIMIT = 60 * 1024 * 1024

C_MQ, C_MK, C_MV, C_MO, C_MZ, C_GA, C_GB = 0, 1024, 2048, 3072, 4096, 5120, 6144
C_NQ, C_NZ = 7168, 7680
C_KV = 8192
C_CMP = 8704
C_SM = 8960
P_W = 9216
AUX_W = 3 * LANES
X_KC, X_VC, X_SM = 0, 1, 2
_O_MI, _O_NQ, _O_KC, _O_KS, _O_NG, _O_NZ, _O_GA, _O_END = 5120, 5128, 5640, 5896, 6408, 6432, 6944, 8992


def _cparams(sem, vmem_limit=VMEM_LIMIT):
    return pltpu.CompilerParams(dimension_semantics=sem, vmem_limit_bytes=vmem_limit)


def _sigmoid(x):
    return 0.5 * jnp.tanh(0.5 * x) + 0.5


def _silu(x):
    h = 0.5 * x
    return h + h * jnp.tanh(h)


def _log_sigmoid(x):
    return jnp.minimum(x, 0.0) - jnp.log1p(jnp.exp(-jnp.abs(x)))


PROJ_TN = 1536
AUX_TILE = C_CMP // PROJ_TN
AUX_OFF = C_CMP % PROJ_TN


N_HEAD_TILES = _O_MI // PROJ_TN


def _inproj_kernel(x_ref, g_ref, wh_ref, wt_ref, o_ref, aux_ref, hn_ref):
    j = pl.program_id(1)

    @pl.when(j == 0)
    def _():
        x = x_ref[...]
        r = lax.rsqrt(jnp.mean(x * x, axis=-1, keepdims=True) + NORM_EPS)
        hn_ref[...] = (x * r * g_ref[...]).astype(BF16)

    @pl.when(j < N_HEAD_TILES)
    def _():
        o_ref[...] = jnp.dot(hn_ref[...], wh_ref[...], preferred_element_type=F32).astype(BF16)

    @pl.when((j >= N_HEAD_TILES) & (j != AUX_TILE))
    def _():
        o_ref[...] = jnp.dot(hn_ref[...], wt_ref[...], preferred_element_type=F32).astype(BF16)

    @pl.when(j == AUX_TILE)
    def _():
        acc = jnp.dot(hn_ref[...], wt_ref[...], preferred_element_type=F32)
        o_ref[...] = acc.astype(BF16)
        aux_ref[...] = acc[:, AUX_OFF:AUX_OFF + AUX_W]


def _inproj(x2, norm_g, w_head, w_tail):
    m = x2.shape[0]
    tm, tn = 2048, PROJ_TN
    return pl.pallas_call(
        _inproj_kernel,
        out_shape=(jax.ShapeDtypeStruct((m, P_W), BF16), jax.ShapeDtypeStruct((m, AUX_W), F32)),
        grid=(m // tm, P_W // tn),
        in_specs=[
            pl.BlockSpec((tm, D_MODEL), lambda i, j: (i, 0)),
            pl.BlockSpec((1, D_MODEL), lambda i, j: (0, 0)),
            pl.BlockSpec((D_MODEL, tn), lambda i, j: (0, jnp.minimum(j, N_HEAD_TILES - 1))),
            pl.BlockSpec((D_MODEL, tn), lambda i, j: (0, jnp.maximum(j - N_HEAD_TILES, 0))),
        ],
        out_specs=(pl.BlockSpec((tm, tn), lambda i, j: (i, j)), pl.BlockSpec((tm, AUX_W), lambda i, j: (i, 0))),
        scratch_shapes=[pltpu.VMEM((tm, D_MODEL), BF16)],
        compiler_params=_cparams(("parallel", "arbitrary"), INPROJ_VMEM_LIMIT),
        name="inproj",
    )(x2, norm_g, w_head, w_tail)


def _causal_conv_silu(u16, prev_ref, w_ref, b_ref):
    n, w = u16.shape
    u = u16.astype(F32)
    row = lax.broadcasted_iota(jnp.int32, (n, n), 0)
    col = lax.broadcasted_iota(jnp.int32, (n, n), 1)
    sub = lax.broadcasted_iota(jnp.int32, (SUBLANES, w), 0)
    prev = prev_ref[...]
    y = b_ref[...] + u * w_ref[CONV_WIDTH - 1:CONV_WIDTH, :]
    for d in range(1, CONV_WIDTH):
        shift = jnp.where(row - col == d, 1.0, 0.0).astype(BF16)
        sh = jnp.dot(shift, u16, preferred_element_type=F32)
        head = jnp.where(sub < d, pltpu.roll(prev, d, axis=0), sh[0:SUBLANES, :])
        sh = jnp.concatenate([head, sh[SUBLANES:, :]], axis=0)
        y = y + sh * w_ref[CONV_WIDTH - 1 - d:CONV_WIDTH - d, :]
    prev_ref[...] = u[n - SUBLANES:n, :]
    return _silu(y)


def _mlstm_kernel(q_ref, k_ref, v_ref, og_ref, z_ref, sm_ref, cqw_ref, cqb_ref, ckw_ref, ckb_ref,
                  sb_ref, ng_ref, ya_ref, qprev, kprev, c_state, n_state, m_state):
    @pl.when(pl.program_id(1) == 0)
    def _():
        qprev[...] = jnp.zeros_like(qprev)
        kprev[...] = jnp.zeros_like(kprev)
        c_state[...] = jnp.zeros_like(c_state)
        n_state[...] = jnp.zeros_like(n_state)
        m_state[...] = jnp.zeros_like(m_state)

    L = MLSTM_CHUNK

    def chunk(c, carry):
        rs = pl.ds(pl.multiple_of(c * L, L), L)
        _mlstm_chunk(q_ref.at[rs], k_ref.at[rs], v_ref.at[rs], og_ref.at[rs], z_ref.at[rs], sm_ref.at[rs],
                     cqw_ref, cqb_ref, ckw_ref, ckb_ref, sb_ref, ng_ref, ya_ref.at[rs],
                     qprev, kprev, c_state, n_state, m_state)
        return carry

    lax.fori_loop(0, q_ref.shape[0] // L, chunk, 0)


def _mlstm_chunk(q_ref, k_ref, v_ref, og_ref, z_ref, sm_ref, cqw_ref, cqb_ref, ckw_ref, ckb_ref,
                 sb_ref, ng_ref, ya_ref, qprev, kprev, c_state, n_state, m_state):
    L = q_ref.shape[0]
    d = M_HEAD_DIM
    mq = _causal_conv_silu(q_ref[...], qprev, cqw_ref, cqb_ref)
    mk = _causal_conv_silu(k_ref[...], kprev, ckw_ref, ckb_ref) * (d ** -0.5)

    sm = sm_ref[...] + sb_ref[...]
    lf = _log_sigmoid(sm)
    row = lax.broadcasted_iota(jnp.int32, (L, L), 0)
    col = lax.broadcasted_iota(jnp.int32, (L, L), 1)
    causal = row >= col
    tril = causal.astype(F32)
    triu = (row <= col).astype(F32)
    a_col = jnp.dot(tril, lf, precision=lax.Precision.HIGHEST, preferred_element_type=F32)
    sm_t = sm.T[0:8, :]
    a_row = jnp.dot(_log_sigmoid(sm_t), triu, precision=lax.Precision.HIGHEST,
                    preferred_element_type=F32)

    for h in range(M_HEADS):
        hs = slice(h * d, (h + 1) * d)
        a_c = a_col[:, 4 + h:5 + h]
        li_c = sm[:, h:h + 1]
        a_r = a_row[4 + h:5 + h, :]
        li_r = sm_t[h:h + 1, :]
        m_prev = m_state[h:h + 1, 0:1]
        a_end = a_c[L - 1:L, :]

        log_d = jnp.where(causal, a_c - a_r + li_r, -jnp.inf)
        inter = a_c + m_prev
        m_t = jnp.maximum(inter, jnp.max(log_d, axis=1, keepdims=True))
        w_inter = jnp.exp(inter - m_t)
        dm = jnp.exp(log_d - m_t)

        qh = mq[:, hs]
        kh = mk[:, hs]
        qb = qh.astype(BF16)
        kb = kh.astype(BF16)
        vb = v_ref[:, hs]
        s = lax.dot_general(qb, kb, (((1,), (1,)), ((), ())), preferred_element_type=F32) * dm
        c_old = c_state[h]
        n_old = n_state[h:h + 1, :]
        num = (w_inter * jnp.dot(qb, c_old.astype(BF16), preferred_element_type=F32)
               + jnp.dot(s.astype(BF16), vb, preferred_element_type=F32))
        den = (w_inter * jnp.sum(qh * n_old, axis=1, keepdims=True)
               + jnp.sum(s, axis=1, keepdims=True))
        hh = num * (1.0 / jnp.maximum(jnp.abs(den), jnp.exp(-m_t)))

        logw = a_end - a_c + li_c
        m_new = jnp.maximum(a_end + m_prev, jnp.max(logw, axis=0, keepdims=True))
        wk = jnp.exp(logw - m_new)
        decay = jnp.exp(a_end + m_prev - m_new)
        kw = kh * wk
        c_state[h] = decay * c_old + lax.dot_general(kw.astype(BF16), vb, (((0,), (0,)), ((), ())),
                                                     preferred_element_type=F32)
        n_state[h:h + 1, :] = decay * n_old + jnp.sum(kw, axis=0, keepdims=True)
        m_state[h:h + 1, :] = jnp.broadcast_to(m_new, (1, LANES))

        mu = jnp.mean(hh, axis=1, keepdims=True)
        dc = hh - mu
        var = jnp.mean(dc * dc, axis=1, keepdims=True)
        hn = dc * lax.rsqrt(var + NORM_EPS) * ng_ref[:, hs]
        ya = _sigmoid(og_ref[:, hs].astype(F32)) * hn * _silu(z_ref[:, hs].astype(F32))
        ya_ref[:, hs] = ya.astype(BF16)


def _mlstm(p, aux, b, t, conv_q_w, conv_q_b, conv_k_w, conv_k_b, sm_bias, mh_norm_g):
    L = MLSTM_CHUNK * MLSTM_CHUNKS_PER_STEP
    nc = t // L

    def colspec(cstart):
        return pl.BlockSpec((L, M_WIDTH), lambda bi, c: (bi * nc + c, cstart // M_WIDTH))

    def full(shape):
        return pl.BlockSpec(shape, lambda bi, c: (0,) * len(shape))

    return pl.pallas_call(
        _mlstm_kernel,
        out_shape=jax.ShapeDtypeStruct((b * t, M_WIDTH), BF16),
        grid=(b, nc),
        in_specs=[
            colspec(C_MQ), colspec(C_MK), colspec(C_MV), colspec(C_MO), colspec(C_MZ),
            pl.BlockSpec((L, LANES), lambda bi, c: (bi * nc + c, X_SM)),
            full((CONV_WIDTH, M_WIDTH)), full((1, M_WIDTH)), full((CONV_WIDTH, M_WIDTH)), full((1, M_WIDTH)),
            full((1, LANES)), full((1, M_WIDTH)),
        ],
        out_specs=pl.BlockSpec((L, M_WIDTH), lambda bi, c: (bi * nc + c, 0)),
        scratch_shapes=[
            pltpu.VMEM((SUBLANES, M_WIDTH), F32), pltpu.VMEM((SUBLANES, M_WIDTH), F32),
            pltpu.VMEM((M_HEADS, M_HEAD_DIM, M_HEAD_DIM), F32),
            pltpu.VMEM((8, M_HEAD_DIM), F32), pltpu.VMEM((8, LANES), F32),
        ],
        compiler_params=_cparams(("parallel", "arbitrary")),
        name="mlstm",
    )(p, p, p, p, p, aux, conv_q_w, conv_q_b, conv_k_w, conv_k_b, sm_bias, mh_norm_g)


def _compress_kernel(pk_ref, pv_ref, wt_ref, wb_ref, posk_ref, posv_ref, w1k_ref, w1v_ref, w2k_ref, w2v_ref,
                     kc_ref, vct_ref):
    nr = pk_ref.shape[0] // CMP_STRIDE
    cw = 2 * LANES
    u = jnp.zeros((nr, wt_ref.shape[1]), F32)
    v = jnp.zeros((nr, wb_ref.shape[1]), F32)
    for l in range(CMP_STRIDE):
        x = jnp.concatenate([pk_ref[pl.ds(l, nr, stride=CMP_STRIDE), :],
                             pv_ref[pl.ds(l, nr, stride=CMP_STRIDE), :]], axis=1).astype(BF16)
        u = u + jnp.dot(x, wt_ref[l * cw:(l + 1) * cw, :], preferred_element_type=F32)
        v = v + jnp.dot(x, wb_ref[l * cw:(l + 1) * cw, :], preferred_element_type=F32)
    v = pltpu.roll(v, nr - 1, axis=0)
    rowi = lax.broadcasted_iota(jnp.int32, v.shape, 0)
    v = jnp.where(rowi < nr - 1, v, 0.0)

    def pos_term(pos_ref, w_ref):
        pb = jnp.broadcast_to(pos_ref[...].astype(BF16), (8, CMP_BLOCK * N_HEAD_DIM))
        return jnp.dot(pb, w_ref[...], preferred_element_type=F32)[0:1, :]

    pk = pos_term(posk_ref, w1k_ref)
    pv = pos_term(posv_ref, w1v_ref)
    hid = u + v + jnp.concatenate([pk, pk, pv, pv], axis=1)
    act = 0.5 * hid * (1.0 + jnp.tanh(math.sqrt(2.0 / math.pi) * (hid + 0.044715 * (hid * hid * hid))))
    for c in range(2 * N_KV_GROUPS):
        w2 = w2k_ref[...] if c < N_KV_GROUPS else w2v_ref[...]
        out = jnp.dot(act[:, c * CMP_HIDDEN:(c + 1) * CMP_HIDDEN].astype(BF16), w2,
                      preferred_element_type=F32)
        if c < N_KV_GROUPS:
            kc_ref[c] = out.astype(BF16)
        else:
            out_t = out.T
            ones_row = lax.broadcasted_iota(jnp.int32, out_t.shape, 0) == N_HEAD_DIM
            vct_ref[c - N_KV_GROUPS] = jnp.where(ones_row, 1.0, out_t).astype(BF16)


def _compress(aux, b, t, wt, wb, posk, posv, w1k, w1v, w2k, w2v):
    nr = t // CMP_STRIDE
    g = N_KV_GROUPS

    def full(a):
        return pl.BlockSpec(a.shape, lambda bi: (0,) * a.ndim)

    return pl.pallas_call(
        _compress_kernel,
        out_shape=(jax.ShapeDtypeStruct((b, g, nr, LANES), BF16), jax.ShapeDtypeStruct((b, g, LANES, nr), BF16)),
        grid=(b,),
        in_specs=[pl.BlockSpec((t, LANES), lambda bi: (bi, X_KC)),
                  pl.BlockSpec((t, LANES), lambda bi: (bi, X_VC)),
                  full(wt), full(wb), full(posk), full(posv), full(w1k), full(w1v), full(w2k), full(w2v)],
        out_specs=(pl.BlockSpec((None, g, nr, LANES), lambda bi: (bi, 0, 0, 0)),
                   pl.BlockSpec((None, g, LANES, nr), lambda bi: (bi, 0, 0, 0))),
        compiler_params=_cparams(("parallel",)),
        name="compress",
    )(aux, aux, wt, wb, posk, posv, w1k, w1v, w2k, w2v)


def _kvprep_kernel(q_ref, kv_ref, sm_ref, pos_ref, inv_ref, gb_ref,
                   qrawt_ref, qrope_ref, kst_ref, vs_ref, kwt_ref, vw_ref, gates_ref, gatest_ref):
    tk = q_ref.shape[0]
    lane = lax.broadcasted_iota(jnp.int32, (tk, LANES), 1)
    lih = lane % N_HEAD_DIM
    pos = jnp.broadcast_to(pos_ref[...], (LANES, tk)).T
    ang = pos * inv_ref[...]
    cos = jnp.cos(ang)
    sin = jnp.sin(ang)
    half = ROPE_DIM // 2
    s_up = jnp.where(lih < half, -sin, 0.0)
    s_dn = jnp.where((lih >= half) & (lih < ROPE_DIM), sin, 0.0)

    def rope(u):
        return u * cos + pltpu.roll(u, LANES - half, axis=1) * s_up + pltpu.roll(u, half, axis=1) * s_dn

    def head_pad(u, odd):
        if odd:
            u = pltpu.roll(u, N_HEAD_DIM, axis=1)
        return jnp.where(lane < N_HEAD_DIM, u, 0.0)

    scale = N_HEAD_DIM ** -0.5 * math.log2(math.e)
    zeros_t = jnp.zeros((N_HEAD_DIM, tk), BF16)
    for g in range(N_KV_GROUPS):
        for hp in range(N_HPG // 2):
            c0 = g * (N_HPG * N_HEAD_DIM) + hp * LANES
            u = q_ref[:, c0:c0 + LANES].astype(F32) * scale
            ur = rope(u)
            u_t = u.T.astype(BF16)
            for odd in range(2):
                h = 2 * hp + odd
                qrawt_ref[g, h * LANES:h * LANES + N_HEAD_DIM, :] = u_t[odd * N_HEAD_DIM:(odd + 1) * N_HEAD_DIM, :]
                qrawt_ref[g, h * LANES + N_HEAD_DIM:(h + 1) * LANES, :] = zeros_t
                qrope_ref[g, :, h * LANES:(h + 1) * LANES] = head_pad(ur, odd).astype(BF16)

    ks_t = rope(kv_ref[:, 0:LANES].astype(F32)).T
    kw_t = rope(kv_ref[:, 2 * LANES:3 * LANES].astype(F32)).T
    vs = kv_ref[:, LANES:2 * LANES].astype(F32)
    vw = kv_ref[:, 3 * LANES:4 * LANES].astype(F32)
    t_glob = pl.program_id(1) * tk + lax.broadcasted_iota(jnp.int32, (LANES, tk), 1)
    blk_row = lax.broadcasted_iota(jnp.int32, (LANES, tk), 0)
    onehot = jnp.where(blk_row == t_glob // SEL_BLOCK, 1.0, 0.0).astype(BF16)
    ones_lane = jnp.where(lane == N_HEAD_DIM, 1.0, 0.0)
    gates = _sigmoid(sm_ref[...] + gb_ref[...])
    for g in range(N_KV_GROUPS):
        rs = slice(g * N_HEAD_DIM, (g + 1) * N_HEAD_DIM)
        kst_ref[g, 0:LANES, :] = onehot
        kst_ref[g, LANES:LANES + N_HEAD_DIM, :] = ks_t[rs, :].astype(BF16)
        kst_ref[g, LANES + N_HEAD_DIM:2 * LANES, :] = zeros_t
        kwt_ref[g, 0:N_HEAD_DIM, :] = kw_t[rs, :].astype(BF16)
        kwt_ref[g, N_HEAD_DIM:LANES, :] = zeros_t
        vs_ref[g] = (head_pad(vs, g) + ones_lane).astype(BF16)
        vw_ref[g] = (head_pad(vw, g) + ones_lane).astype(BF16)
        goff = 8 + g * (3 * N_HPG)
        gg = pltpu.roll(gates, LANES - goff, axis=1)
        gates_ref[g] = gg
        gatest_ref[g] = gg.T


def _kvprep(p, aux, b, t, pos_col, inv_row, gate_bias):
    tk = 1024
    nt = t // tk
    g = N_KV_GROUPS
    qw = N_HPG * LANES

    def tok(shape):
        return pl.BlockSpec((None, g) + shape, lambda bi, i: (bi, 0, i, 0))

    def tok_t(rows):
        return pl.BlockSpec((None, g, rows, tk), lambda bi, i: (bi, 0, 0, i))

    return pl.pallas_call(
        _kvprep_kernel,
        out_shape=(
            jax.ShapeDtypeStruct((b, g, qw, t), BF16), jax.ShapeDtypeStruct((b, g, t, qw), BF16),
            jax.ShapeDtypeStruct((b, g, 2 * LANES, t), BF16), jax.ShapeDtypeStruct((b, g, t, LANES), BF16),
            jax.ShapeDtypeStruct((b, g, LANES, t), BF16), jax.ShapeDtypeStruct((b, g, t, LANES), BF16),
            jax.ShapeDtypeStruct((b, g, t, LANES), F32), jax.ShapeDtypeStruct((b, g, LANES, t), F32),
        ),
        grid=(b, nt),
        in_specs=[
            pl.BlockSpec((tk, N_WIDTH), lambda bi, i: (bi * nt + i, C_NQ // N_WIDTH)),
            pl.BlockSpec((tk, 4 * LANES), lambda bi, i: (bi * nt + i, C_KV // (4 * LANES))),
            pl.BlockSpec((tk, LANES), lambda bi, i: (bi * nt + i, X_SM)),
            pl.BlockSpec((None, 1, tk), lambda bi, i: (bi, 0, i)),
            pl.BlockSpec((1, LANES), lambda bi, i: (0, 0)),
            pl.BlockSpec((1, LANES), lambda bi, i: (0, 0)),
        ],
        out_specs=(tok_t(qw), tok((tk, qw)), tok_t(2 * LANES), tok((tk, LANES)),
                   tok_t(LANES), tok((tk, LANES)), tok((tk, LANES)), tok_t(LANES)),
        compiler_params=_cparams(("parallel", "arbitrary")),
        name="kvprep",
    )(p, p, aux, pos_col, inv_row, gate_bias)


CMP_ROW_CHUNK = 128


def _select_kernel(qt_ref, kc_ref, vct_ref, ovt_ref, gt_ref, oc_ref, bias_ref, imp_ref):
    qs = qt_ref.shape[1]
    nr = kc_ref.shape[0]
    t0 = pl.program_id(2) * qs
    sees_any = t0 + lax.broadcasted_iota(jnp.int32, (1, qs), 1) >= CMP_BLOCK - 1

    def cmp_branch(rows):
        m_c = (lax.broadcasted_iota(jnp.int32, (rows, qs), 0) * CMP_STRIDE + (CMP_BLOCK - 1)
               <= t0 + lax.broadcasted_iota(jnp.int32, (rows, qs), 1))
        kc = kc_ref[0:rows, :]
        vct = vct_ref[:, 0:rows]
        ovt = ovt_ref[:, 0:rows]
        imp = jnp.zeros((LANES, qs), F32)
        for h in range(N_HPG):
            s = jnp.dot(kc, qt_ref[h * LANES:(h + 1) * LANES, :], preferred_element_type=F32)
            s = jnp.where(m_c, s, NEG)
            e = jnp.exp2(s - jnp.max(s, axis=0, keepdims=True)).astype(BF16)
            r = jnp.dot(vct, e, preferred_element_type=F32)
            inv = jnp.where(sees_any, 1.0 / r[N_HEAD_DIM:N_HEAD_DIM + 1, :], 0.0)
            imp = imp + jnp.dot(ovt, e, preferred_element_type=F32) * inv
            oc_ref[:, h * LANES:(h + 1) * LANES] = (r * (inv * gt_ref[3 * h:3 * h + 1, :])).T
        imp_ref[...] = imp

    n_var = nr // CMP_ROW_CHUNK
    visible = (t0 + qs - CMP_BLOCK) // CMP_STRIDE + 1
    need = jnp.clip((visible + CMP_ROW_CHUNK - 1) // CMP_ROW_CHUNK, 1, n_var)
    for v in range(1, n_var + 1):
        pl.when(need == v)(functools.partial(cmp_branch, v * CMP_ROW_CHUNK))
    imp = imp_ref[...]

    blk = lax.broadcasted_iota(jnp.int32, (LANES, qs), 0)
    tq = t0 + lax.broadcasted_iota(jnp.int32, (LANES, qs), 1)
    cur = tq // SEL_BLOCK
    forced = (blk == 0) | (blk == cur) | (blk == cur - 1)
    n_sel = (nr * CMP_STRIDE) // SEL_BLOCK
    free_score = jnp.where(blk * SEL_BLOCK <= tq, imp, -1.0)
    blk_f = blk.astype(F32)
    n_top = min(SEL_TOPK, n_sel)
    n_forced = 3

    def pick(score, n_iter):
        score = jnp.where(blk < n_sel, score, -jnp.inf)
        for _ in range(n_iter):
            mx = jnp.max(score, axis=0, keepdims=True)
            idx = jnp.min(jnp.where(score == mx, blk_f, float(LANES)), axis=0, keepdims=True)
            score = jnp.where(blk_f == idx, -jnp.inf, score)
        bias_t = jnp.where((score == -jnp.inf) & (blk < n_sel), 0.0, NEG)
        bias_ref[...] = bias_t.T.astype(BF16)

    all_three_forced = (t0 >= 2 * SEL_BLOCK) & (n_top >= n_forced)

    @pl.when(all_three_forced)
    def _():
        pick(jnp.where(forced, -jnp.inf, free_score), n_top - n_forced)

    @pl.when(jnp.logical_not(all_three_forced))
    def _():
        pick(jnp.where(forced, FORCE, free_score), n_top)


def _select(qrawt, kc, vct, ovt, gates_t):
    b, g, qw, t = qrawt.shape
    nr = kc.shape[2]
    qs = 1024

    def whole(rows, cols):
        return pl.BlockSpec((None, None, rows, cols), lambda bi, gi, i: (bi, gi, 0, 0))

    return pl.pallas_call(
        _select_kernel,
        out_shape=(jax.ShapeDtypeStruct((b, g, t, qw), F32), jax.ShapeDtypeStruct((b, g, t, LANES), BF16)),
        grid=(b, g, t // qs),
        in_specs=[
            pl.BlockSpec((None, None, qw, qs), lambda bi, gi, i: (bi, gi, 0, i)),
            whole(nr, LANES), whole(LANES, nr),
            pl.BlockSpec(ovt.shape, lambda bi, gi, i: (0, 0)),
            pl.BlockSpec((None, None, LANES, qs), lambda bi, gi, i: (bi, gi, 0, i)),
        ],
        out_specs=(pl.BlockSpec((None, None, qs, qw), lambda bi, gi, i: (bi, gi, i, 0)),
                   pl.BlockSpec((None, None, qs, LANES), lambda bi, gi, i: (bi, gi, i, 0))),
        scratch_shapes=[pltpu.VMEM((LANES, qs), F32)],
        compiler_params=_cparams(("parallel", "parallel", "parallel")),
        name="select",
    )(qrawt, kc, vct, ovt, gates_t)


def _nsa_kernel(qrope_ref, bias_ref, oc_ref, kst_ref, vs_ref, kwt_ref, vw_ref, gates_ref, nz_ref,
                yb_ref, qaug_ref, acc_ref, m_ref, s_ref):
    i = pl.program_id(1)
    t0 = i * Q_BLOCK
    Q = Q_BLOCK
    nh = N_HPG
    ng = N_KV_GROUPS
    rows = nh * Q
    t0a = pl.multiple_of(t0, Q)
    causal = (lax.broadcasted_iota(jnp.int32, (Q, Q), 1) <= lax.broadcasted_iota(jnp.int32, (Q, Q), 0))
    lane = lax.broadcasted_iota(jnp.int32, (Q, LANES), 1)

    def q_rope(g):
        return jnp.concatenate([qrope_ref[g, :, h * LANES:(h + 1) * LANES] for h in range(nh)], axis=0)

    def mask_heads(s, m):
        return jnp.concatenate([jnp.where(m, s[h * Q:(h + 1) * Q, :], NEG) for h in range(nh)], axis=0)

    first_blk = t0 // SEL_BLOCK
    for g in range(ng):
        qr = q_rope(g)
        bias = bias_ref[g]
        bias_early = jnp.where(lane < first_blk, bias, NEG).astype(BF16)
        qaug_ref[g, :, 0:LANES] = jnp.concatenate([bias_early] * nh, axis=0)
        qaug_ref[g, :, LANES:2 * LANES] = qr
        q_diag = jnp.concatenate([jnp.concatenate([bias] * nh, axis=0), qr], axis=1)
        s_d = jnp.dot(q_diag, kst_ref[g, :, pl.ds(t0a, Q)], preferred_element_type=F32)
        s_d = mask_heads(s_d, causal)
        m0 = jnp.max(s_d, axis=1, keepdims=True)
        p_d = jnp.exp2(s_d - m0)
        m_ref[g] = jnp.broadcast_to(m0, (rows, LANES))
        acc_ref[g] = jnp.dot(p_d.astype(BF16), vs_ref[g, pl.ds(t0a, Q), :], preferred_element_type=F32)

    ncb = SEL_CHUNK // LANES
    n_chunks = (t0 + SEL_CHUNK - 1) // SEL_CHUNK
    last = jnp.maximum(n_chunks - 1, 0)

    def scores(g, c, slot):
        k0 = pl.multiple_of(c * SEL_CHUNK, SEL_CHUNK)
        s_ref[slot, g] = jnp.dot(qaug_ref[g], kst_ref[g, :, pl.ds(k0, SEL_CHUNK)], preferred_element_type=F32)

    def consume(g, c, slot):
        k0 = pl.multiple_of(c * SEL_CHUNK, SEL_CHUNK)
        s = s_ref[slot, g]
        m_old = m_ref[g]
        m_new = jnp.maximum(m_old, jnp.max(s, axis=1, keepdims=True))
        p = jnp.concatenate([jnp.exp2(s[:, cb * LANES:(cb + 1) * LANES] - m_new) for cb in range(ncb)], axis=1)
        acc_ref[g] = (jnp.exp2(m_old - m_new) * acc_ref[g]
                      + jnp.dot(p.astype(BF16), vs_ref[g, pl.ds(k0, SEL_CHUNK), :], preferred_element_type=F32))
        m_ref[g] = m_new

    for g in range(ng):
        scores(g, 0, 0)

    def steps(c, n):
        for u in range(n):
            for g in range(ng):
                scores(g, jnp.minimum(c + u + 1, last), (u + 1) % 2)
                consume(g, c + u, u % 2)

    def sel_quad(c4, carry):
        steps(SEL_UNROLL * c4, SEL_UNROLL)
        return carry

    n_quads = n_chunks // SEL_UNROLL
    lax.fori_loop(0, n_quads, sel_quad, 0)
    rem = n_chunks - SEL_UNROLL * n_quads

    @pl.when(rem >= 2)
    def _():
        steps(SEL_UNROLL * n_quads, 2)

    @pl.when(rem % 2 == 1)
    def _():
        for g in range(ng):
            consume(g, n_chunks - 1, 0)

    wlen = WINDOW + Q
    ws = pl.multiple_of(jnp.maximum(t0 - WINDOW, 0), Q)
    diff = (t0 + lax.broadcasted_iota(jnp.int32, (Q, wlen), 0)) - (ws + lax.broadcasted_iota(jnp.int32, (Q, wlen), 1))
    m_w = (diff >= 0) & (diff < WINDOW)
    for g in range(ng):
        gates = gates_ref[g]

        def gate_col(br):
            return jnp.concatenate([gates[:, 3 * h + br:3 * h + br + 1] for h in range(nh)], axis=0)

        acc = acc_ref[g]
        o_s = acc * (gate_col(1) / acc[:, N_HEAD_DIM:N_HEAD_DIM + 1])
        s_w = jnp.dot(q_rope(g), kwt_ref[g, :, pl.ds(ws, wlen)], preferred_element_type=F32)
        s_w = mask_heads(s_w, m_w)
        p_w = jnp.exp2(s_w - jnp.max(s_w, axis=1, keepdims=True))
        acc_w = jnp.dot(p_w.astype(BF16), vw_ref[g, pl.ds(ws, wlen), :], preferred_element_type=F32)
        o_w = acc_w * (gate_col(2) / acc_w[:, N_HEAD_DIM:N_HEAD_DIM + 1])
        outs = []
        for h in range(nh):
            rs = slice(h * Q, (h + 1) * Q)
            outs.append(oc_ref[g, :, h * LANES:(h + 1) * LANES] + o_s[rs, :] + o_w[rs, :])
        for hp in range(nh // 2):
            yn = jnp.where(lane < N_HEAD_DIM, outs[2 * hp], pltpu.roll(outs[2 * hp + 1], N_HEAD_DIM, axis=1))
            ls = slice((g * (nh // 2) + hp) * LANES, (g * (nh // 2) + hp + 1) * LANES)
            yb_ref[:, ls] = (yn * _silu(nz_ref[:, ls].astype(F32))).astype(BF16)


def _nsa(p, b, t, qrope, bias, oc, kst, vs, kwt, vw, gates):
    g = N_KV_GROUPS
    nqb = t // Q_BLOCK
    qw = N_HPG * LANES
    rows = N_HPG * Q_BLOCK

    def qblk(w):
        return pl.BlockSpec((None, g, Q_BLOCK, w), lambda bi, i: (bi, 0, i, 0))

    def whole(r, c):
        return pl.BlockSpec((None, g, r, c), lambda bi, i: (bi, 0, 0, 0), pipeline_mode=pl.Buffered(1))

    return pl.pallas_call(
        _nsa_kernel,
        out_shape=jax.ShapeDtypeStruct((b * t, N_WIDTH), BF16),
        grid=(b, nqb),
        in_specs=[
            qblk(qw), qblk(LANES), qblk(qw),
            whole(2 * LANES, t), whole(t, LANES), whole(LANES, t), whole(t, LANES),
            qblk(LANES),
            pl.BlockSpec((Q_BLOCK, N_WIDTH), lambda bi, i: (bi * nqb + i, C_NZ // N_WIDTH)),
        ],
        out_specs=pl.BlockSpec((Q_BLOCK, N_WIDTH), lambda bi, i: (bi * nqb + i, 0)),
        scratch_shapes=[pltpu.VMEM((g, rows, 2 * LANES), BF16), pltpu.VMEM((g, rows, LANES), F32),
                        pltpu.VMEM((g, rows, LANES), F32), pltpu.VMEM((2, g, rows, SEL_CHUNK), F32)],
        compiler_params=_cparams(("parallel", "arbitrary")),
        name="nsa",
    )(qrope, bias, oc, kst, vs, kwt, vw, gates, p)


def _outproj_kernel(final, ya_ref, yb_ref, ga_ref, gb_ref, x_ref, wa_ref, wb_ref, wo_ref, fg_ref, o_ref):
    a = jnp.dot(ya_ref[...], wa_ref[...], preferred_element_type=F32)
    bq = jnp.dot(yb_ref[...], wb_ref[...], preferred_element_type=F32)
    merged = _sigmoid(ga_ref[...].astype(F32)) * a + _sigmoid(gb_ref[...].astype(F32)) * bq
    y = x_ref[...] + jnp.dot(merged.astype(BF16), wo_ref[...], preferred_element_type=F32)
    if final:
        y = y * lax.rsqrt(jnp.mean(y * y, axis=-1, keepdims=True) + NORM_EPS) * fg_ref[...]
    o_ref[...] = y


def _outproj(ya, yb, p, x2, wa, wb, wo, fg, final):
    m = x2.shape[0]
    tm = 1024

    def full(a):
        return pl.BlockSpec(a.shape, lambda i: (0, 0))

    return pl.pallas_call(
        functools.partial(_outproj_kernel, final),
        out_shape=jax.ShapeDtypeStruct((m, D_MODEL), F32),
        grid=(m // tm,),
        in_specs=[
            pl.BlockSpec((tm, M_WIDTH), lambda i: (i, 0)),
            pl.BlockSpec((tm, N_WIDTH), lambda i: (i, 0)),
            pl.BlockSpec((tm, D_MODEL), lambda i: (i, C_GA // D_MODEL)),
            pl.BlockSpec((tm, D_MODEL), lambda i: (i, C_GB // D_MODEL)),
            pl.BlockSpec((tm, D_MODEL), lambda i: (i, 0)),
            full(wa), full(wb), full(wo), full(fg),
        ],
        out_specs=pl.BlockSpec((tm, D_MODEL), lambda i: (i, 0)),
        compiler_params=_cparams(("parallel",)),
        name="outproj",
    )(ya, yb, p, p, x2, wa, wb, wo, fg)


def _permute_w_tail(w):
    pad = jnp.zeros((w.shape[0], P_W - _O_END), w.dtype)
    return jnp.concatenate([
        w[:, N_HEAD_TILES * PROJ_TN:_O_MI],
        w[:, _O_GA:_O_END],
        w[:, _O_NQ:_O_KC],
        w[:, _O_NZ:_O_GA],
        w[:, _O_KS:_O_NG],
        w[:, _O_KC:_O_KS],
        w[:, _O_MI:_O_NQ],
        w[:, _O_NG:_O_NZ],
        pad], axis=1)


def _expand_cmp_w1(w1k, w1v):
    half = CMP_BLOCK // 2
    k3 = w1k.astype(BF16).reshape(CMP_BLOCK, 1, N_HEAD_DIM, 1, CMP_HIDDEN)
    v3 = w1v.astype(BF16).reshape(CMP_BLOCK, 1, N_HEAD_DIM, 1, CMP_HIDDEN)
    nc = 2 * N_KV_GROUPS
    eye = jnp.eye(nc, dtype=BF16)
    is_k = (jnp.arange(nc) < N_KV_GROUPS).astype(BF16)
    mk = (eye * is_k[:, None]).reshape(1, nc, 1, nc, 1)
    mv = (eye * (1 - is_k)[:, None]).reshape(1, nc, 1, nc, 1)
    z = k3 * mk + v3 * mv
    z = z.reshape(CMP_BLOCK * nc * N_HEAD_DIM, nc * CMP_HIDDEN)
    return z[:half * nc * N_HEAD_DIM], z[half * nc * N_HEAD_DIM:]


def kernel(x, positions, norm_g, w_in, conv_q_w, conv_q_b, conv_k_w, conv_k_b, b_igate, b_fgate, mh_norm_g,
           cmp_pos_k, cmp_w1_k, cmp_w2_k, cmp_pos_v, cmp_w1_v, cmp_w2_v, b_nsa_gate, w_branch_a, w_branch_b,
           w_out, final_norm_g):
    b, t, _ = x.shape
    assert t % (2 * SEL_CHUNK) == 0 and t >= WINDOW + Q_BLOCK and t // SEL_BLOCK <= LANES and t % MLSTM_CHUNK == 0
    x2 = x.reshape(b * t, D_MODEL)
    for l in range(norm_g.shape[0]):
        w_bf = w_in[l].astype(BF16)
        p, aux = _inproj(x2, norm_g[l][None, :], w_bf, _permute_w_tail(w_bf))

        sm_bias = jnp.zeros((1, LANES), F32).at[0, 0:M_HEADS].set(b_igate[l]).at[0, M_HEADS:2 * M_HEADS].set(b_fgate[l])
        ya = _mlstm(p, aux, b, t, conv_q_w[l], conv_q_b[l][None, :], conv_k_w[l], conv_k_b[l][None, :], sm_bias,
                    mh_norm_g[l][None, :])

        nr = t // CMP_STRIDE
        wt, wb = _expand_cmp_w1(cmp_w1_k[l], cmp_w1_v[l])
        w2pad = lambda w: jnp.pad(w, ((0, 0), (0, LANES - N_HEAD_DIM))).astype(BF16)
        kc, vct = _compress(aux, b, t, wt, wb, cmp_pos_k[l].reshape(1, -1), cmp_pos_v[l].reshape(1, -1),
                            cmp_w1_k[l].astype(BF16), cmp_w1_v[l].astype(BF16), w2pad(cmp_w2_k[l]), w2pad(cmp_w2_v[l]))

        half = ROPE_DIM // 2
        inv = jnp.power(jnp.float32(ROPE_THETA), -jnp.arange(half, dtype=F32) * (2.0 / ROPE_DIM))
        lih = jnp.arange(LANES) % N_HEAD_DIM
        inv_row = jnp.where(lih < ROPE_DIM, inv[lih % half], 0.0)[None, :].astype(F32)
        pos_col = positions.astype(F32)[:, None, :]
        gate_bias = jnp.zeros((1, LANES), F32).at[0, 8:8 + 3 * N_HEADS].set(b_nsa_gate[l])
        qrawt, qrope, kst, vs, kwt, vw, gates, gates_t = _kvprep(p, aux, b, t, pos_col, inv_row, gate_bias)

        ci = jnp.arange(nr)[None, :] * CMP_STRIDE
        sj = jnp.arange(LANES)[:, None] * SEL_BLOCK
        overlap_t = ((ci < sj + SEL_BLOCK) & (ci + CMP_BLOCK > sj) & (jnp.arange(nr)[None, :] < nr - 1)
                     & (jnp.arange(LANES)[:, None] < t // SEL_BLOCK)).astype(BF16)
        oc, bias = _select(qrawt, kc, vct, overlap_t, gates_t)
        yb = _nsa(p, b, t, qrope, bias, oc, kst, vs, kwt, vw, gates)

        x2 = _outproj(ya, yb, p, x2, w_branch_a[l].astype(BF16), w_branch_b[l].astype(BF16),
                      w_out[l].astype(BF16), final_norm_g[None, :], l == norm_g.shape[0] - 1)
    return x2.reshape(b, t, D_MODEL)
```

```python
import functools
import math

import jax
import jax.numpy as jnp
from jax import lax
from jax.experimental import pallas as pl
from jax.experimental.pallas import tpu as pltpu

F32 = jnp.float32
BF16 = jnp.bfloat16

D_MODEL = 1024
M_HEADS = 4
M_HEAD_DIM = 256
M_WIDTH = M_HEADS * M_HEAD_DIM
CONV_WIDTH = 4
N_HEADS = 8
N_KV_GROUPS = 2
N_HPG = N_HEADS // N_KV_GROUPS
N_HEAD_DIM = 64
N_WIDTH = N_HEADS * N_HEAD_DIM
CMP_BLOCK = 32
CMP_STRIDE = 16
CMP_HIDDEN = 128
SEL_BLOCK = 64
SEL_TOPK = 16
WINDOW = 512
Q_BLOCK = 256
ROPE_THETA = 500000.0
ROPE_DIM = N_HEAD_DIM // 4
NORM_EPS = 1e-6
NEG = -1e30
FORCE = 1e9

LANES = 128
SUBLANES = 8
MLSTM_CHUNK = 256
MLSTM_CHUNKS_PER_STEP = 1
SEL_CHUNK = 512
SEL_UNROLL = 4
VMEM_LIMIT = 56 * 1024 * 1024
INPROJ_VMEM_LIMIT = 60 * 1024 * 1024

C_MQ, C_MK, C_MV, C_MO, C_MZ, C_GA, C_GB = 0, 1024, 2048, 3072, 4096, 5120, 6144
C_NQ, C_NZ = 7168, 7680
C_KV = 8192
C_CMP = 8704
C_SM = 8960
P_W = 9216
AUX_W = 3 * LANES
X_KC, X_VC, X_SM = 0, 1, 2
_O_MI, _O_NQ, _O_KC, _O_KS, _O_NG, _O_NZ, _O_GA, _O_END = 5120, 5128, 5640, 5896, 6408, 6432, 6944, 8992


def _cparams(sem, vmem_limit=VMEM_LIMIT):
    return pltpu.CompilerParams(dimension_semantics=sem, vmem_limit_bytes=vmem_limit)


def _sigmoid(x):
    return 0.5 * jnp.tanh(0.5 * x) + 0.5


def _silu(x):
    h = 0.5 * x
    return h + h * jnp.tanh(h)


def _log_sigmoid(x):
    return jnp.minimum(x, 0.0) - jnp.log1p(jnp.exp(-jnp.abs(x)))


PROJ_TN = 1536
AUX_TILE = C_CMP // PROJ_TN
AUX_OFF = C_CMP % PROJ_TN


N_HEAD_TILES = _O_MI // PROJ_TN


def _inproj_kernel(x_ref, g_ref, wh_ref, wt_ref, o_ref, aux_ref, hn_ref):
    j = pl.program_id(1)

    @pl.when(j == 0)
    def _():
        x = x_ref[...]
        r = lax.rsqrt(jnp.mean(x * x, axis=-1, keepdims=True) + NORM_EPS)
        hn_ref[...] = (x * r * g_ref[...]).astype(BF16)

    @pl.when(j < N_HEAD_TILES)
    def _():
        o_ref[...] = jnp.dot(hn_ref[...], wh_ref[...], preferred_element_type=F32).astype(BF16)

    @pl.when((j >= N_HEAD_TILES) & (j != AUX_TILE))
    def _():
        o_ref[...] = jnp.dot(hn_ref[...], wt_ref[...], preferred_element_type=F32).astype(BF16)

    @pl.when(j == AUX_TILE)
    def _():
        acc = jnp.dot(hn_ref[...], wt_ref[...], preferred_element_type=F32)
        o_ref[...] = acc.astype(BF16)
        aux_ref[...] = acc[:, AUX_OFF:AUX_OFF + AUX_W]


def _inproj(x2, norm_g, w_head, w_tail):
    m = x2.shape[0]
    tm, tn = 2048, PROJ_TN
    return pl.pallas_call(
        _inproj_kernel,
        out_shape=(jax.ShapeDtypeStruct((m, P_W), BF16), jax.ShapeDtypeStruct((m, AUX_W), F32)),
        grid=(m // tm, P_W // tn),
        in_specs=[
            pl.BlockSpec((tm, D_MODEL), lambda i, j: (i, 0)),
            pl.BlockSpec((1, D_MODEL), lambda i, j: (0, 0)),
            pl.BlockSpec((D_MODEL, tn), lambda i, j: (0, jnp.minimum(j, N_HEAD_TILES - 1))),
            pl.BlockSpec((D_MODEL, tn), lambda i, j: (0, jnp.maximum(j - N_HEAD_TILES, 0))),
        ],
        out_specs=(pl.BlockSpec((tm, tn), lambda i, j: (i, j)), pl.BlockSpec((tm, AUX_W), lambda i, j: (i, 0))),
        scratch_shapes=[pltpu.VMEM((tm, D_MODEL), BF16)],
        compiler_params=_cparams(("parallel", "arbitrary"), INPROJ_VMEM_LIMIT),
        name="inproj",
    )(x2, norm_g, w_head, w_tail)


def _causal_conv_silu(u16, prev_ref, w_ref, b_ref):
    n, w = u16.shape
    u = u16.astype(F32)
    row = lax.broadcasted_iota(jnp.int32, (n, n), 0)
    col = lax.broadcasted_iota(jnp.int32, (n, n), 1)
    sub = lax.broadcasted_iota(jnp.int32, (SUBLANES, w), 0)
    prev = prev_ref[...]
    y = b_ref[...] + u * w_ref[CONV_WIDTH - 1:CONV_WIDTH, :]
    for d in range(1, CONV_WIDTH):
        shift = jnp.where(row - col == d, 1.0, 0.0).astype(BF16)
        sh = jnp.dot(shift, u16, preferred_element_type=F32)
        head = jnp.where(sub < d, pltpu.roll(prev, d, axis=0), sh[0:SUBLANES, :])
        sh = jnp.concatenate([head, sh[SUBLANES:, :]], axis=0)
        y = y + sh * w_ref[CONV_WIDTH - 1 - d:CONV_WIDTH - d, :]
    prev_ref[...] = u[n - SUBLANES:n, :]
    return _silu(y)


def _mlstm_kernel(q_ref, k_ref, v_ref, og_ref, z_ref, sm_ref, cqw_ref, cqb_ref, ckw_ref, ckb_ref,
                  sb_ref, ng_ref, ya_ref, qprev, kprev, c_state, n_state, m_state):
    @pl.when(pl.program_id(1) == 0)
    def _():
        qprev[...] = jnp.zeros_like(qprev)
        kprev[...] = jnp.zeros_like(kprev)
        c_state[...] = jnp.zeros_like(c_state)
        n_state[...] = jnp.zeros_like(n_state)
        m_state[...] = jnp.zeros_like(m_state)

    L = MLSTM_CHUNK

    def chunk(c, carry):
        rs = pl.ds(pl.multiple_of(c * L, L), L)
        _mlstm_chunk(q_ref.at[rs], k_ref.at[rs], v_ref.at[rs], og_ref.at[rs], z_ref.at[rs], sm_ref.at[rs],
                     cqw_ref, cqb_ref, ckw_ref, ckb_ref, sb_ref, ng_ref, ya_ref.at[rs],
                     qprev, kprev, c_state, n_state, m_state)
        return carry

    lax.fori_loop(0, q_ref.shape[0] // L, chunk, 0)


def _mlstm_chunk(q_ref, k_ref, v_ref, og_ref, z_ref, sm_ref, cqw_ref, cqb_ref, ckw_ref, ckb_ref,
                 sb_ref, ng_ref, ya_ref, qprev, kprev, c_state, n_state, m_state):
    L = q_ref.shape[0]
    d = M_HEAD_DIM
    mq = _causal_conv_silu(q_ref[...], qprev, cqw_ref, cqb_ref)
    mk = _causal_conv_silu(k_ref[...], kprev, ckw_ref, ckb_ref) * (d ** -0.5)

    sm = sm_ref[...] + sb_ref[...]
    lf = _log_sigmoid(sm)
    row = lax.broadcasted_iota(jnp.int32, (L, L), 0)
    col = lax.broadcasted_iota(jnp.int32, (L, L), 1)
    causal = row >= col
    tril = causal.astype(F32)
    triu = (row <= col).astype(F32)
    a_col = jnp.dot(tril, lf, precision=lax.Precision.HIGHEST, preferred_element_type=F32)
    sm_t = sm.T[0:8, :]
    a_row = jnp.dot(_log_sigmoid(sm_t), triu, precision=lax.Precision.HIGHEST,
                    preferred_element_type=F32)

    for h in range(M_HEADS):
        hs = slice(h * d, (h + 1) * d)
        a_c = a_col[:, 4 + h:5 + h]
        li_c = sm[:, h:h + 1]
        a_r = a_row[4 + h:5 + h, :]
        li_r = sm_t[h:h + 1, :]
        m_prev = m_state[h:h + 1, 0:1]
        a_end = a_c[L - 1:L, :]

        log_d = jnp.where(causal, a_c - (a_r - li_r), -jnp.inf)
        inter = a_c + m_prev
        m_t = jnp.maximum(inter, jnp.max(log_d, axis=1, keepdims=True))
        w_inter = jnp.exp(inter - m_t)
        dm = jnp.exp(log_d - m_t)

        qh = mq[:, hs]
        kh = mk[:, hs]
        qb = qh.astype(BF16)
        kb = kh.astype(BF16)
        vb = v_ref[:, hs]
        s = lax.dot_general(qb, kb, (((1,), (1,)), ((), ())), preferred_element_type=F32) * dm
        c_old = c_state[h]
        n_old = n_state[h:h + 1, :]
        num = (w_inter * jnp.dot(qb, c_old.astype(BF16), preferred_element_type=F32)
               + jnp.dot(s.astype(BF16), vb, preferred_element_type=F32))
        den = (w_inter * jnp.sum(qh * n_old, axis=1, keepdims=True)
               + jnp.sum(s, axis=1, keepdims=True))
        hh = num * (1.0 / jnp.maximum(jnp.abs(den), jnp.exp(-m_t)))

        logw = a_end - a_c + li_c
        m_new = jnp.maximum(a_end + m_prev, jnp.max(logw, axis=0, keepdims=True))
        wk = jnp.exp(logw - m_new)
        decay = jnp.exp(a_end + m_prev - m_new)
        kw = kh * wk
        c_state[h] = decay * c_old + lax.dot_general(kw.astype(BF16), vb, (((0,), (0,)), ((), ())),
                                                     preferred_element_type=F32)
        n_state[h:h + 1, :] = decay * n_old + jnp.sum(kw, axis=0, keepdims=True)
        m_state[h:h + 1, :] = jnp.broadcast_to(m_new, (1, LANES))

        mu = jnp.mean(hh, axis=1, keepdims=True)
        dc = hh - mu
        var = jnp.mean(dc * dc, axis=1, keepdims=True)
        hn = dc * lax.rsqrt(var + NORM_EPS) * ng_ref[:, hs]
        ya = _sigmoid(og_ref[:, hs].astype(F32)) * hn * _silu(z_ref[:, hs].astype(F32))
        ya_ref[:, hs] = ya.astype(BF16)


def _mlstm(p, aux, b, t, conv_q_w, conv_q_b, conv_k_w, conv_k_b, sm_bias, mh_norm_g):
    L = MLSTM_CHUNK * MLSTM_CHUNKS_PER_STEP
    nc = t // L

    def colspec(cstart):
        return pl.BlockSpec((L, M_WIDTH), lambda bi, c: (bi * nc + c, cstart // M_WIDTH))

    def full(shape):
        return pl.BlockSpec(shape, lambda bi, c: (0,) * len(shape))

    return pl.pallas_call(
        _mlstm_kernel,
        out_shape=jax.ShapeDtypeStruct((b * t, M_WIDTH), BF16),
        grid=(b, nc),
        in_specs=[
            colspec(C_MQ), colspec(C_MK), colspec(C_MV), colspec(C_MO), colspec(C_MZ),
            pl.BlockSpec((L, LANES), lambda bi, c: (bi * nc + c, X_SM)),
            full((CONV_WIDTH, M_WIDTH)), full((1, M_WIDTH)), full((CONV_WIDTH, M_WIDTH)), full((1, M_WIDTH)),
            full((1, LANES)), full((1, M_WIDTH)),
        ],
        out_specs=pl.BlockSpec((L, M_WIDTH), lambda bi, c: (bi * nc + c, 0)),
        scratch_shapes=[
            pltpu.VMEM((SUBLANES, M_WIDTH), F32), pltpu.VMEM((SUBLANES, M_WIDTH), F32),
            pltpu.VMEM((M_HEADS, M_HEAD_DIM, M_HEAD_DIM), F32),
            pltpu.VMEM((8, M_HEAD_DIM), F32), pltpu.VMEM((8, LANES), F32),
        ],
        compiler_params=_cparams(("parallel", "arbitrary")),
        name="mlstm",
    )(p, p, p, p, p, aux, conv_q_w, conv_q_b, conv_k_w, conv_k_b, sm_bias, mh_norm_g)


def _compress_kernel(pk_ref, pv_ref, wt_ref, wb_ref, posk_ref, posv_ref, w1k_ref, w1v_ref, w2k_ref, w2v_ref,
                     kc_ref, vct_ref):
    nr = pk_ref.shape[0] // CMP_STRIDE
    cw = 2 * LANES
    u = jnp.zeros((nr, wt_ref.shape[1]), F32)
    v = jnp.zeros((nr, wb_ref.shape[1]), F32)
    for l in range(CMP_STRIDE):
        x = jnp.concatenate([pk_ref[pl.ds(l, nr, stride=CMP_STRIDE), :],
                             pv_ref[pl.ds(l, nr, stride=CMP_STRIDE), :]], axis=1).astype(BF16)
        u = u + jnp.dot(x, wt_ref[l * cw:(l + 1) * cw, :], preferred_element_type=F32)
        v = v + jnp.dot(x, wb_ref[l * cw:(l + 1) * cw, :], preferred_element_type=F32)
    v = pltpu.roll(v, nr - 1, axis=0)
    rowi = lax.broadcasted_iota(jnp.int32, v.shape, 0)
    v = jnp.where(rowi < nr - 1, v, 0.0)

    def pos_term(pos_ref, w_ref):
        pb = jnp.broadcast_to(pos_ref[...].astype(BF16), (8, CMP_BLOCK * N_HEAD_DIM))
        return jnp.dot(pb, w_ref[...], preferred_element_type=F32)[0:1, :]

    pk = pos_term(posk_ref, w1k_ref)
    pv = pos_term(posv_ref, w1v_ref)
    hid = u + v + jnp.concatenate([pk, pk, pv, pv], axis=1)
    act = 0.5 * hid * (1.0 + jnp.tanh(math.sqrt(2.0 / math.pi) * (hid + 0.044715 * (hid * hid * hid))))
    for c in range(2 * N_KV_GROUPS):
        w2 = w2k_ref[...] if c < N_KV_GROUPS else w2v_ref[...]
        out = jnp.dot(act[:, c * CMP_HIDDEN:(c + 1) * CMP_HIDDEN].astype(BF16), w2,
                      preferred_element_type=F32)
        if c < N_KV_GROUPS:
            kc_ref[c] = out.astype(BF16)
        else:
            out_t = out.T
            ones_row = lax.broadcasted_iota(jnp.int32, out_t.shape, 0) == N_HEAD_DIM
            vct_ref[c - N_KV_GROUPS] = jnp.where(ones_row, 1.0, out_t).astype(BF16)


def _compress(aux, b, t, wt, wb, posk, posv, w1k, w1v, w2k, w2v):
    nr = t // CMP_STRIDE
    g = N_KV_GROUPS

    def full(a):
        return pl.BlockSpec(a.shape, lambda bi: (0,) * a.ndim)

    return pl.pallas_call(
        _compress_kernel,
        out_shape=(jax.ShapeDtypeStruct((b, g, nr, LANES), BF16), jax.ShapeDtypeStruct((b, g, LANES, nr), BF16)),
        grid=(b,),
        in_specs=[pl.BlockSpec((t, LANES), lambda bi: (bi, X_KC)),
                  pl.BlockSpec((t, LANES), lambda bi: (bi, X_VC)),
                  full(wt), full(wb), full(posk), full(posv), full(w1k), full(w1v), full(w2k), full(w2v)],
        out_specs=(pl.BlockSpec((None, g, nr, LANES), lambda bi: (bi, 0, 0, 0)),
                   pl.BlockSpec((None, g, LANES, nr), lambda bi: (bi, 0, 0, 0))),
        compiler_params=_cparams(("parallel",)),
        name="compress",
    )(aux, aux, wt, wb, posk, posv, w1k, w1v, w2k, w2v)


def _kvprep_kernel(q_ref, kv_ref, sm_ref, pos_ref, inv_ref, gb_ref,
                   qrawt_ref, qrope_ref, kst_ref, vs_ref, kwt_ref, vw_ref, gates_ref, gatest_ref):
    tk = q_ref.shape[0]
    lane = lax.broadcasted_iota(jnp.int32, (tk, LANES), 1)
    lih = lane % N_HEAD_DIM
    pos = jnp.broadcast_to(pos_ref[...], (LANES, tk)).T
    ang = pos * inv_ref[...]
    cos = jnp.cos(ang)
    sin = jnp.sin(ang)
    half = ROPE_DIM // 2
    s_up = jnp.where(lih < half, -sin, 0.0)
    s_dn = jnp.where((lih >= half) & (lih < ROPE_DIM), sin, 0.0)

    def rope(u):
        return u * cos + pltpu.roll(u, LANES - half, axis=1) * s_up + pltpu.roll(u, half, axis=1) * s_dn

    def head_pad(u, odd):
        if odd:
            u = pltpu.roll(u, N_HEAD_DIM, axis=1)
        return jnp.where(lane < N_HEAD_DIM, u, 0.0)

    scale = N_HEAD_DIM ** -0.5 * math.log2(math.e)
    zeros_t = jnp.zeros((N_HEAD_DIM, tk), BF16)
    for g in range(N_KV_GROUPS):
        for hp in range(N_HPG // 2):
            c0 = g * (N_HPG * N_HEAD_DIM) + hp * LANES
            u = q_ref[:, c0:c0 + LANES].astype(F32) * scale
            ur = rope(u)
            u_t = u.T.astype(BF16)
            for odd in range(2):
                h = 2 * hp + odd
                qrawt_ref[g, h * LANES:h * LANES + N_HEAD_DIM, :] = u_t[odd * N_HEAD_DIM:(odd + 1) * N_HEAD_DIM, :]
                qrawt_ref[g, h * LANES + N_HEAD_DIM:(h + 1) * LANES, :] = zeros_t
                qrope_ref[g, :, h * LANES:(h + 1) * LANES] = head_pad(ur, odd).astype(BF16)

    ks_t = rope(kv_ref[:, 0:LANES].astype(F32)).T
    kw_t = rope(kv_ref[:, 2 * LANES:3 * LANES].astype(F32)).T
    vs = kv_ref[:, LANES:2 * LANES].astype(F32)
    vw = kv_ref[:, 3 * LANES:4 * LANES].astype(F32)
    t_glob = pl.program_id(1) * tk + lax.broadcasted_iota(jnp.int32, (LANES, tk), 1)
    blk_row = lax.broadcasted_iota(jnp.int32, (LANES, tk), 0)
    onehot = jnp.where(blk_row == t_glob // SEL_BLOCK, 1.0, 0.0).astype(BF16)
    ones_lane = jnp.where(lane == N_HEAD_DIM, 1.0, 0.0)
    gates = _sigmoid(sm_ref[...] + gb_ref[...])
    for g in range(N_KV_GROUPS):
        rs = slice(g * N_HEAD_DIM, (g + 1) * N_HEAD_DIM)
        kst_ref[g, 0:LANES, :] = onehot
        kst_ref[g, LANES:LANES + N_HEAD_DIM, :] = ks_t[rs, :].astype(BF16)
        kst_ref[g, LANES + N_HEAD_DIM:2 * LANES, :] = zeros_t
        kwt_ref[g, 0:N_HEAD_DIM, :] = kw_t[rs, :].astype(BF16)
        kwt_ref[g, N_HEAD_DIM:LANES, :] = zeros_t
        vs_ref[g] = (head_pad(vs, g) + ones_lane).astype(BF16)
        vw_ref[g] = (head_pad(vw, g) + ones_lane).astype(BF16)
        goff = 8 + g * (3 * N_HPG)
        gg = pltpu.roll(gates, LANES - goff, axis=1)
        gates_ref[g] = gg
        gatest_ref[g] = gg.T


def _kvprep(p, aux, b, t, pos_col, inv_row, gate_bias):
    tk = 1024
    nt = t // tk
    g = N_KV_GROUPS
    qw = N_HPG * LANES

    def tok(shape):
        return pl.BlockSpec((None, g) + shape, lambda bi, i: (bi, 0, i, 0))

    def tok_t(rows):
        return pl.BlockSpec((None, g, rows, tk), lambda bi, i: (bi, 0, 0, i))

    return pl.pallas_call(
        _kvprep_kernel,
        out_shape=(
            jax.ShapeDtypeStruct((b, g, qw, t), BF16), jax.ShapeDtypeStruct((b, g, t, qw), BF16),
            jax.ShapeDtypeStruct((b, g, 2 * LANES, t), BF16), jax.ShapeDtypeStruct((b, g, t, LANES), BF16),
            jax.ShapeDtypeStruct((b, g, LANES, t), BF16), jax.ShapeDtypeStruct((b, g, t, LANES), BF16),
            jax.ShapeDtypeStruct((b, g, t, LANES), F32), jax.ShapeDtypeStruct((b, g, LANES, t), F32),
        ),
        grid=(b, nt),
        in_specs=[
            pl.BlockSpec((tk, N_WIDTH), lambda bi, i: (bi * nt + i, C_NQ // N_WIDTH)),
            pl.BlockSpec((tk, 4 * LANES), lambda bi, i: (bi * nt + i, C_KV // (4 * LANES))),
            pl.BlockSpec((tk, LANES), lambda bi, i: (bi * nt + i, X_SM)),
            pl.BlockSpec((None, 1, tk), lambda bi, i: (bi, 0, i)),
            pl.BlockSpec((1, LANES), lambda bi, i: (0, 0)),
            pl.BlockSpec((1, LANES), lambda bi, i: (0, 0)),
        ],
        out_specs=(tok_t(qw), tok((tk, qw)), tok_t(2 * LANES), tok((tk, LANES)),
                   tok_t(LANES), tok((tk, LANES)), tok((tk, LANES)), tok_t(LANES)),
        compiler_params=_cparams(("parallel", "arbitrary")),
        name="kvprep",
    )(p, p, aux, pos_col, inv_row, gate_bias)


CMP_ROW_CHUNK = 128


def _select_kernel(qt_ref, kc_ref, vct_ref, ovt_ref, gt_ref, oc_ref, bias_ref, imp_ref):
    qs = qt_ref.shape[1]
    nr = kc_ref.shape[0]
    t0 = pl.program_id(2) * qs
    sees_any = t0 + lax.broadcasted_iota(jnp.int32, (1, qs), 1) >= CMP_BLOCK - 1

    def cmp_branch(rows):
        m_c = (lax.broadcasted_iota(jnp.int32, (rows, qs), 0) * CMP_STRIDE + (CMP_BLOCK - 1)
               <= t0 + lax.broadcasted_iota(jnp.int32, (rows, qs), 1))
        kc = kc_ref[0:rows, :]
        vct = vct_ref[:, 0:rows]
        ovt = ovt_ref[:, 0:rows]
        imp = jnp.zeros((LANES, qs), F32)
        for h in range(N_HPG):
            s = jnp.dot(kc, qt_ref[h * LANES:(h + 1) * LANES, :], preferred_element_type=F32)
            s = jnp.where(m_c, s, NEG)
            e = jnp.exp2(s - jnp.max(s, axis=0, keepdims=True)).astype(BF16)
            r = jnp.dot(vct, e, preferred_element_type=F32)
            inv = jnp.where(sees_any, 1.0 / r[N_HEAD_DIM:N_HEAD_DIM + 1, :], 0.0)
            imp = imp + jnp.dot(ovt, e, preferred_element_type=F32) * inv
            oc_ref[:, h * LANES:(h + 1) * LANES] = (r * (inv * gt_ref[3 * h:3 * h + 1, :])).T
        imp_ref[...] = imp

    n_var = nr // CMP_ROW_CHUNK
    visible = (t0 + qs - CMP_BLOCK) // CMP_STRIDE + 1
    need = jnp.clip((visible + CMP_ROW_CHUNK - 1) // CMP_ROW_CHUNK, 1, n_var)
    for v in range(1, n_var + 1):
        pl.when(need == v)(functools.partial(cmp_branch, v * CMP_ROW_CHUNK))
    imp = imp_ref[...]

    blk = lax.broadcasted_iota(jnp.int32, (LANES, qs), 0)
    tq = t0 + lax.broadcasted_iota(jnp.int32, (LANES, qs), 1)
    cur = tq // SEL_BLOCK
    forced = (blk == 0) | (blk == cur) | (blk == cur - 1)
    n_sel = (nr * CMP_STRIDE) // SEL_BLOCK
    free_score = jnp.where(blk * SEL_BLOCK <= tq, imp, -1.0)
    blk_f = blk.astype(F32)
    n_top = min(SEL_TOPK, n_sel)
    n_forced = 3

    def pick(score, n_iter):
        score = jnp.where(blk < n_sel, score, -jnp.inf)
        for _ in range(n_iter):
            mx = jnp.max(score, axis=0, keepdims=True)
            idx = jnp.min(jnp.where(score == mx, blk_f, float(LANES)), axis=0, keepdims=True)
            score = jnp.where(blk_f == idx, -jnp.inf, score)
        bias_t = jnp.where((score == -jnp.inf) & (blk < n_sel), 0.0, NEG)
        bias_ref[...] = bias_t.T.astype(BF16)

    all_three_forced = (t0 >= 2 * SEL_BLOCK) & (n_top >= n_forced)

    @pl.when(all_three_forced)
    def _():
        pick(jnp.where(forced, -jnp.inf, free_score), n_top - n_forced)

    @pl.when(jnp.logical_not(all_three_forced))
    def _():
        pick(jnp.where(forced, FORCE, free_score), n_top)


def _select(qrawt, kc, vct, ovt, gates_t):
    b, g, qw, t = qrawt.shape
    nr = kc.shape[2]
    qs = 2048

    def whole(rows, cols):
        return pl.BlockSpec((None, None, rows, cols), lambda bi, gi, i: (bi, gi, 0, 0))

    return pl.pallas_call(
        _select_kernel,
        out_shape=(jax.ShapeDtypeStruct((b, g, t, qw), F32), jax.ShapeDtypeStruct((b, g, t, LANES), BF16)),
        grid=(b, g, t // qs),
        in_specs=[
            pl.BlockSpec((None, None, qw, qs), lambda bi, gi, i: (bi, gi, 0, i)),
            whole(nr, LANES), whole(LANES, nr),
            pl.BlockSpec(ovt.shape, lambda bi, gi, i: (0, 0)),
            pl.BlockSpec((None, None, LANES, qs), lambda bi, gi, i: (bi, gi, 0, i)),
        ],
        out_specs=(pl.BlockSpec((None, None, qs, qw), lambda bi, gi, i: (bi, gi, i, 0)),
                   pl.BlockSpec((None, None, qs, LANES), lambda bi, gi, i: (bi, gi, i, 0))),
        scratch_shapes=[pltpu.VMEM((LANES, qs), F32)],
        compiler_params=_cparams(("parallel", "parallel", "parallel")),
        name="select",
    )(qrawt, kc, vct, ovt, gates_t)


def _nsa_kernel(qrope_ref, bias_ref, oc_ref, kst_ref, vs_ref, kwt_ref, vw_ref, gates_ref, nz_ref,
                yb_ref, qaug_ref, acc_ref, m_ref, s_ref):
    i = pl.program_id(1)
    t0 = i * Q_BLOCK
    Q = Q_BLOCK
    nh = N_HPG
    ng = N_KV_GROUPS
    rows = nh * Q
    t0a = pl.multiple_of(t0, Q)
    causal = (lax.broadcasted_iota(jnp.int32, (Q, Q), 1) <= lax.broadcasted_iota(jnp.int32, (Q, Q), 0))
    lane = lax.broadcasted_iota(jnp.int32, (Q, LANES), 1)

    def q_rope(g):
        return jnp.concatenate([qrope_ref[g, :, h * LANES:(h + 1) * LANES] for h in range(nh)], axis=0)

    def mask_heads(s, m):
        return jnp.concatenate([jnp.where(m, s[h * Q:(h + 1) * Q, :], NEG) for h in range(nh)], axis=0)

    first_blk = t0 // SEL_BLOCK
    for g in range(ng):
        qr = q_rope(g)
        bias = bias_ref[g]
        bias_early = jnp.where(lane < first_blk, bias, NEG).astype(BF16)
        qaug_ref[g, :, 0:LANES] = jnp.concatenate([bias_early] * nh, axis=0)
        qaug_ref[g, :, LANES:2 * LANES] = qr
        q_diag = jnp.concatenate([jnp.concatenate([bias] * nh, axis=0), qr], axis=1)
        s_d = jnp.dot(q_diag, kst_ref[g, :, pl.ds(t0a, Q)], preferred_element_type=F32)
        s_d = mask_heads(s_d, causal)
        m0 = jnp.max(s_d, axis=1, keepdims=True)
        p_d = jnp.exp2(s_d - m0)
        m_ref[g] = jnp.broadcast_to(m0, (rows, LANES))
        acc_ref[g] = jnp.dot(p_d.astype(BF16), vs_ref[g, pl.ds(t0a, Q), :], preferred_element_type=F32)

    ncb = SEL_CHUNK // LANES
    n_chunks = (t0 + SEL_CHUNK - 1) // SEL_CHUNK
    last = jnp.maximum(n_chunks - 1, 0)

    def scores(g, c, slot):
        k0 = pl.multiple_of(c * SEL_CHUNK, SEL_CHUNK)
        s_ref[slot, g] = jnp.dot(qaug_ref[g], kst_ref[g, :, pl.ds(k0, SEL_CHUNK)], preferred_element_type=F32)

    def consume(g, c, slot):
        k0 = pl.multiple_of(c * SEL_CHUNK, SEL_CHUNK)
        s = s_ref[slot, g]
        m_old = m_ref[g]
        m_new = jnp.maximum(m_old, jnp.max(s, axis=1, keepdims=True))
        p = jnp.concatenate([jnp.exp2(s[:, cb * LANES:(cb + 1) * LANES] - m_new) for cb in range(ncb)], axis=1)
        acc_ref[g] = (jnp.exp2(m_old - m_new) * acc_ref[g]
                      + jnp.dot(p.astype(BF16), vs_ref[g, pl.ds(k0, SEL_CHUNK), :], preferred_element_type=F32))
        m_ref[g] = m_new

    for g in range(ng):
        scores(g, 0, 0)

    def steps(c, n):
        for u in range(n):
            for g in range(ng):
                scores(g, jnp.minimum(c + u + 1, last), (u + 1) % 2)
                consume(g, c + u, u % 2)

    def sel_quad(c4, carry):
        steps(SEL_UNROLL * c4, SEL_UNROLL)
        return carry

    n_quads = n_chunks // SEL_UNROLL
    lax.fori_loop(0, n_quads, sel_quad, 0)
    rem = n_chunks - SEL_UNROLL * n_quads

    @pl.when(rem >= 2)
    def _():
        steps(SEL_UNROLL * n_quads, 2)

    @pl.when(rem % 2 == 1)
    def _():
        for g in range(ng):
            consume(g, n_chunks - 1, 0)

    wlen = WINDOW + Q
    ws = pl.multiple_of(jnp.maximum(t0 - WINDOW, 0), Q)
    diff = (t0 + lax.broadcasted_iota(jnp.int32, (Q, wlen), 0)) - (ws + lax.broadcasted_iota(jnp.int32, (Q, wlen), 1))
    m_w = (diff >= 0) & (diff < WINDOW)
    for g in range(ng):
        gates = gates_ref[g]

        def gate_col(br):
            return jnp.concatenate([gates[:, 3 * h + br:3 * h + br + 1] for h in range(nh)], axis=0)

        acc = acc_ref[g]
        o_s = acc * (gate_col(1) / acc[:, N_HEAD_DIM:N_HEAD_DIM + 1])
        s_w = jnp.dot(q_rope(g), kwt_ref[g, :, pl.ds(ws, wlen)], preferred_element_type=F32)
        s_w = mask_heads(s_w, m_w)
        p_w = jnp.exp2(s_w - jnp.max(s_w, axis=1, keepdims=True))
        acc_w = jnp.dot(p_w.astype(BF16), vw_ref[g, pl.ds(ws, wlen), :], preferred_element_type=F32)
        o_w = acc_w * (gate_col(2) / acc_w[:, N_HEAD_DIM:N_HEAD_DIM + 1])
        outs = []
        for h in range(nh):
            rs = slice(h * Q, (h + 1) * Q)
            outs.append(oc_ref[g, :, h * LANES:(h + 1) * LANES] + o_s[rs, :] + o_w[rs, :])
        for hp in range(nh // 2):
            yn = jnp.where(lane < N_HEAD_DIM, outs[2 * hp], pltpu.roll(outs[2 * hp + 1], N_HEAD_DIM, axis=1))
            ls = slice((g * (nh // 2) + hp) * LANES, (g * (nh // 2) + hp + 1) * LANES)
            yb_ref[:, ls] = (yn * _silu(nz_ref[:, ls].astype(F32))).astype(BF16)


def _nsa(p, b, t, qrope, bias, oc, kst, vs, kwt, vw, gates):
    g = N_KV_GROUPS
    nqb = t // Q_BLOCK
    qw = N_HPG * LANES
    rows = N_HPG * Q_BLOCK

    def qblk(w):
        return pl.BlockSpec((None, g, Q_BLOCK, w), lambda bi, i: (bi, 0, i, 0))

    def whole(r, c):
        return pl.BlockSpec((None, g, r, c), lambda bi, i: (bi, 0, 0, 0), pipeline_mode=pl.Buffered(1))

    return pl.pallas_call(
        _nsa_kernel,
        out_shape=jax.ShapeDtypeStruct((b * t, N_WIDTH), BF16),
        grid=(b, nqb),
        in_specs=[
            qblk(qw), qblk(LANES), qblk(qw),
            whole(2 * LANES, t), whole(t, LANES), whole(LANES, t), whole(t, LANES),
            qblk(LANES),
            pl.BlockSpec((Q_BLOCK, N_WIDTH), lambda bi, i: (bi * nqb + i, C_NZ // N_WIDTH)),
        ],
        out_specs=pl.BlockSpec((Q_BLOCK, N_WIDTH), lambda bi, i: (bi * nqb + i, 0)),
        scratch_shapes=[pltpu.VMEM((g, rows, 2 * LANES), BF16), pltpu.VMEM((g, rows, LANES), F32),
                        pltpu.VMEM((g, rows, LANES), F32), pltpu.VMEM((2, g, rows, SEL_CHUNK), F32)],
        compiler_params=_cparams(("parallel", "arbitrary")),
        name="nsa",
    )(qrope, bias, oc, kst, vs, kwt, vw, gates, p)


def _outproj_kernel(final, ya_ref, yb_ref, ga_ref, gb_ref, x_ref, wa_ref, wb_ref, wo_ref, fg_ref, o_ref):
    a = jnp.dot(ya_ref[...], wa_ref[...], preferred_element_type=F32)
    bq = jnp.dot(yb_ref[...], wb_ref[...], preferred_element_type=F32)
    merged = _sigmoid(ga_ref[...].astype(F32)) * a + _sigmoid(gb_ref[...].astype(F32)) * bq
    y = x_ref[...] + jnp.dot(merged.astype(BF16), wo_ref[...], preferred_element_type=F32)
    if final:
        y = y * lax.rsqrt(jnp.mean(y * y, axis=-1, keepdims=True) + NORM_EPS) * fg_ref[...]
    o_ref[...] = y


def _outproj(ya, yb, p, x2, wa, wb, wo, fg, final):
    m = x2.shape[0]
    tm = 1024

    def full(a):
        return pl.BlockSpec(a.shape, lambda i: (0, 0))

    return pl.pallas_call(
        functools.partial(_outproj_kernel, final),
        out_shape=jax.ShapeDtypeStruct((m, D_MODEL), F32),
        grid=(m // tm,),
        in_specs=[
            pl.BlockSpec((tm, M_WIDTH), lambda i: (i, 0)),
            pl.BlockSpec((tm, N_WIDTH), lambda i: (i, 0)),
            pl.BlockSpec((tm, D_MODEL), lambda i: (i, C_GA // D_MODEL)),
            pl.BlockSpec((tm, D_MODEL), lambda i: (i, C_GB // D_MODEL)),
            pl.BlockSpec((tm, D_MODEL), lambda i: (i, 0)),
            full(wa), full(wb), full(wo), full(fg),
        ],
        out_specs=pl.BlockSpec((tm, D_MODEL), lambda i: (i, 0)),
        compiler_params=_cparams(("parallel",)),
        name="outproj",
    )(ya, yb, p, p, x2, wa, wb, wo, fg)


def _permute_w_tail(w):
    pad = jnp.zeros((w.shape[0], P_W - _O_END), w.dtype)
    return jnp.concatenate([
        w[:, N_HEAD_TILES * PROJ_TN:_O_MI],
        w[:, _O_GA:_O_END],
        w[:, _O_NQ:_O_KC],
        w[:, _O_NZ:_O_GA],
        w[:, _O_KS:_O_NG],
        w[:, _O_KC:_O_KS],
        w[:, _O_MI:_O_NQ],
        w[:, _O_NG:_O_NZ],
        pad], axis=1)


def _expand_cmp_w1(w1k, w1v):
    half = CMP_BLOCK // 2
    k3 = w1k.astype(BF16).reshape(CMP_BLOCK, 1, N_HEAD_DIM, 1, CMP_HIDDEN)
    v3 = w1v.astype(BF16).reshape(CMP_BLOCK, 1, N_HEAD_DIM, 1, CMP_HIDDEN)
    nc = 2 * N_KV_GROUPS
    eye = jnp.eye(nc, dtype=BF16)
    is_k = (jnp.arange(nc) < N_KV_GROUPS).astype(BF16)
    mk = (eye * is_k[:, None]).reshape(1, nc, 1, nc, 1)
    mv = (eye * (1 - is_k)[:, None]).reshape(1, nc, 1, nc, 1)
    z = k3 * mk + v3 * mv
    z = z.reshape(CMP_BLOCK * nc * N_HEAD_DIM, nc * CMP_HIDDEN)
    return z[:half * nc * N_HEAD_DIM], z[half * nc * N_HEAD_DIM:]


def kernel(x, positions, norm_g, w_in, conv_q_w, conv_q_b, conv_k_w, conv_k_b, b_igate, b_fgate, mh_norm_g,
           cmp_pos_k, cmp_w1_k, cmp_w2_k, cmp_pos_v, cmp_w1_v, cmp_w2_v, b_nsa_gate, w_branch_a, w_branch_b,
           w_out, final_norm_g):
    b, t, _ = x.shape
    assert t % (2 * SEL_CHUNK) == 0 and t >= WINDOW + Q_BLOCK and t // SEL_BLOCK <= LANES and t % MLSTM_CHUNK == 0
    x2 = x.reshape(b * t, D_MODEL)
    for l in range(norm_g.shape[0]):
        w_bf = w_in[l].astype(BF16)
        p, aux = _inproj(x2, norm_g[l][None, :], w_bf, _permute_w_tail(w_bf))

        sm_bias = jnp.zeros((1, LANES), F32).at[0, 0:M_HEADS].set(b_igate[l]).at[0, M_HEADS:2 * M_HEADS].set(b_fgate[l])
        ya = _mlstm(p, aux, b, t, conv_q_w[l], conv_q_b[l][None, :], conv_k_w[l], conv_k_b[l][None, :], sm_bias,
                    mh_norm_g[l][None, :])

        nr = t // CMP_STRIDE
        wt, wb = _expand_cmp_w1(cmp_w1_k[l], cmp_w1_v[l])
        w2pad = lambda w: jnp.pad(w, ((0, 0), (0, LANES - N_HEAD_DIM))).astype(BF16)
        kc, vct = _compress(aux, b, t, wt, wb, cmp_pos_k[l].reshape(1, -1), cmp_pos_v[l].reshape(1, -1),
                            cmp_w1_k[l].astype(BF16), cmp_w1_v[l].astype(BF16), w2pad(cmp_w2_k[l]), w2pad(cmp_w2_v[l]))

        half = ROPE_DIM // 2
        inv = jnp.power(jnp.float32(ROPE_THETA), -jnp.arange(half, dtype=F32) * (2.0 / ROPE_DIM))
        lih = jnp.arange(LANES) % N_HEAD_DIM
        inv_row = jnp.where(lih < ROPE_DIM, inv[lih % half], 0.0)[None, :].astype(F32)
        pos_col = positions.astype(F32)[:, None, :]
        gate_bias = jnp.zeros((1, LANES), F32).at[0, 8:8 + 3 * N_HEADS].set(b_nsa_gate[l])
        qrawt, qrope, kst, vs, kwt, vw, gates, gates_t = _kvprep(p, aux, b, t, pos_col, inv_row, gate_bias)

        ci = jnp.arange(nr)[None, :] * CMP_STRIDE
        sj = jnp.arange(LANES)[:, None] * SEL_BLOCK
        overlap_t = ((ci < sj + SEL_BLOCK) & (ci + CMP_BLOCK > sj) & (jnp.arange(nr)[None, :] < nr - 1)
                     & (jnp.arange(LANES)[:, None] < t // SEL_BLOCK)).astype(BF16)
        oc, bias = _select(qrawt, kc, vct, overlap_t, gates_t)
        yb = _nsa(p, b, t, qrope, bias, oc, kst, vs, kwt, vw, gates)

        x2 = _outproj(ya, yb, p, x2, w_branch_a[l].astype(BF16), w_branch_b[l].astype(BF16),
                      w_out[l].astype(BF16), final_norm_g[None, :], l == norm_g.shape[0] - 1)
    return x2.reshape(b, t, D_MODEL)
```

```python
import functools
import math

import jax
import jax.numpy as jnp
import numpy as np
from jax import lax
from jax.experimental import pallas as pl
from jax.experimental.pallas import tpu as pltpu

F32 = jnp.float32
BF16 = jnp.bfloat16

D_MODEL = 1024
M_HEADS = 4
M_HEAD_DIM = 256
M_WIDTH = M_HEADS * M_HEAD_DIM
CONV_WIDTH = 4
N_HEADS = 8
N_KV_GROUPS = 2
N_HPG = N_HEADS // N_KV_GROUPS
N_HEAD_DIM = 64
N_WIDTH = N_HEADS * N_HEAD_DIM
CMP_BLOCK = 32
CMP_STRIDE = 16
CMP_HIDDEN = 128
SEL_BLOCK = 64
SEL_TOPK = 16
WINDOW = 512
Q_BLOCK = 256
ROPE_THETA = 500000.0
ROPE_DIM = N_HEAD_DIM // 4
NORM_EPS = 1e-6
NEG = -1e30
FORCE = 1e9

LANES = 128
SUBLANES = 8
MLSTM_CHUNK = 256
SEL_CHUNK = 512
SEL_UNROLL = 4
V7X_VMEM_BYTES = 64 * 1024 * 1024
VMEM_LIMIT = V7X_VMEM_BYTES - 8 * 1024 * 1024
INPROJ_VMEM_LIMIT = V7X_VMEM_BYTES - 4 * 1024 * 1024
INPROJ_TM, OUTPROJ_TM, KVPREP_TK, SELECT_QS = 2048, 1024, 1024, 2048
SM_F_OFF = M_HEADS
SM_G_OFF = 2 * M_HEADS
GATE_ROWS = -(-3 * N_HPG // SUBLANES) * SUBLANES

C_MQ, C_MK, C_MV, C_MO, C_MZ, C_GA, C_GB = 0, 1024, 2048, 3072, 4096, 5120, 6144
C_NQ, C_NZ = 7168, 7680
C_KV = 8192
C_CMP = 8704
C_SM = 8960
P_W = 9216
AUX_W = 3 * LANES
X_KC, X_VC, X_SM = 0, 1, 2
_O_MI, _O_NQ, _O_KC, _O_KS, _O_NG, _O_NZ, _O_GA, _O_END = 5120, 5128, 5640, 5896, 6408, 6432, 6944, 8992


def _cparams(sem, vmem_limit=VMEM_LIMIT):
    return pltpu.CompilerParams(dimension_semantics=sem, vmem_limit_bytes=vmem_limit)


def _sigmoid(x):
    return 0.5 * jnp.tanh(0.5 * x) + 0.5


def _silu(x):
    h = 0.5 * x
    return h + h * jnp.tanh(h)


def _log_sigmoid(x):
    return jnp.minimum(x, 0.0) - jnp.log1p(jnp.exp(-jnp.abs(x)))


PROJ_TN = 1536
AUX_TILE = C_CMP // PROJ_TN
AUX_OFF = C_CMP % PROJ_TN


N_HEAD_TILES = _O_MI // PROJ_TN


def _inproj_kernel(x_ref, g_ref, wh_ref, wt_ref, o_ref, aux_ref, hn_ref):
    j = pl.program_id(1)

    @pl.when(j == 0)
    def _():
        x = x_ref[...]
        r = lax.rsqrt(jnp.mean(x * x, axis=-1, keepdims=True) + NORM_EPS)
        hn_ref[...] = (x * r * g_ref[...]).astype(BF16)

    @pl.when(j < N_HEAD_TILES)
    def _():
        o_ref[...] = jnp.dot(hn_ref[...], wh_ref[...], preferred_element_type=F32).astype(BF16)

    @pl.when((j >= N_HEAD_TILES) & (j != AUX_TILE))
    def _():
        o_ref[...] = jnp.dot(hn_ref[...], wt_ref[...], preferred_element_type=F32).astype(BF16)

    @pl.when(j == AUX_TILE)
    def _():
        acc = jnp.dot(hn_ref[...], wt_ref[...], preferred_element_type=F32)
        o_ref[...] = acc.astype(BF16)
        aux_ref[...] = acc[:, AUX_OFF:AUX_OFF + AUX_W]


def _inproj(x2, norm_g, w_head, w_tail):
    m = x2.shape[0]
    tm, tn = INPROJ_TM, PROJ_TN
    return pl.pallas_call(
        _inproj_kernel,
        out_shape=(jax.ShapeDtypeStruct((m, P_W), BF16), jax.ShapeDtypeStruct((m, AUX_W), F32)),
        grid=(m // tm, P_W // tn),
        in_specs=[
            pl.BlockSpec((tm, D_MODEL), lambda i, j: (i, 0)),
            pl.BlockSpec((1, D_MODEL), lambda i, j: (0, 0)),
            pl.BlockSpec((D_MODEL, tn), lambda i, j: (0, jnp.minimum(j, N_HEAD_TILES - 1))),
            pl.BlockSpec((D_MODEL, tn), lambda i, j: (0, jnp.maximum(j - N_HEAD_TILES, 0))),
        ],
        out_specs=(pl.BlockSpec((tm, tn), lambda i, j: (i, j)), pl.BlockSpec((tm, AUX_W), lambda i, j: (i, 0))),
        scratch_shapes=[pltpu.VMEM((tm, D_MODEL), BF16)],
        compiler_params=_cparams(("parallel", "arbitrary"), INPROJ_VMEM_LIMIT),
        name="inproj",
    )(x2, norm_g, w_head, w_tail)


def _causal_conv_silu(u16, prev_ref, w_ref, b_ref):
    n, w = u16.shape
    u = u16.astype(F32)
    row = lax.broadcasted_iota(jnp.int32, (n, n), 0)
    col = lax.broadcasted_iota(jnp.int32, (n, n), 1)
    sub = lax.broadcasted_iota(jnp.int32, (SUBLANES, w), 0)
    prev = prev_ref[...]
    y = b_ref[...] + u * w_ref[CONV_WIDTH - 1:CONV_WIDTH, :]
    for d in range(1, CONV_WIDTH):
        shift = jnp.where(row - col == d, 1.0, 0.0).astype(BF16)
        sh = jnp.dot(shift, u16, preferred_element_type=F32)
        head = jnp.where(sub < d, pltpu.roll(prev, d, axis=0), sh[0:SUBLANES, :])
        sh = jnp.concatenate([head, sh[SUBLANES:, :]], axis=0)
        y = y + sh * w_ref[CONV_WIDTH - 1 - d:CONV_WIDTH - d, :]
    prev_ref[...] = u[n - SUBLANES:n, :]
    return _silu(y)


def _mlstm_kernel(q_ref, k_ref, v_ref, og_ref, z_ref, sm_ref, cqw_ref, cqb_ref, ckw_ref, ckb_ref,
                  sb_ref, ng_ref, ya_ref, qprev, kprev, c_state, n_state, m_state):
    L = q_ref.shape[0]
    d = M_HEAD_DIM

    @pl.when(pl.program_id(1) == 0)
    def _():
        qprev[...] = jnp.zeros_like(qprev)
        kprev[...] = jnp.zeros_like(kprev)
        c_state[...] = jnp.zeros_like(c_state)
        n_state[...] = jnp.zeros_like(n_state)
        m_state[...] = jnp.zeros_like(m_state)

    mq = _causal_conv_silu(q_ref[...], qprev, cqw_ref, cqb_ref)
    mk = _causal_conv_silu(k_ref[...], kprev, ckw_ref, ckb_ref) * (d ** -0.5)

    sm = sm_ref[...] + sb_ref[...]
    lf = _log_sigmoid(sm)
    row = lax.broadcasted_iota(jnp.int32, (L, L), 0)
    col = lax.broadcasted_iota(jnp.int32, (L, L), 1)
    causal = row >= col
    tril = causal.astype(F32)
    triu = (row <= col).astype(F32)
    a_col = jnp.dot(tril, lf, precision=lax.Precision.HIGHEST, preferred_element_type=F32)
    sm_t = sm.T[0:SUBLANES, :]
    a_row = jnp.dot(_log_sigmoid(sm_t), triu, precision=lax.Precision.HIGHEST,
                    preferred_element_type=F32)

    for h in range(M_HEADS):
        hs = slice(h * d, (h + 1) * d)
        a_c = a_col[:, SM_F_OFF + h:SM_F_OFF + h + 1]
        li_c = sm[:, h:h + 1]
        a_r = a_row[SM_F_OFF + h:SM_F_OFF + h + 1, :]
        li_r = sm_t[h:h + 1, :]
        m_prev = m_state[h:h + 1, 0:1]
        a_end = a_c[L - 1:L, :]

        log_d = jnp.where(causal, a_c - (a_r - li_r), -jnp.inf)
        inter = a_c + m_prev
        m_t = jnp.maximum(inter, jnp.max(log_d, axis=1, keepdims=True))
        w_inter = jnp.exp(inter - m_t)
        dm = jnp.exp(log_d - m_t)

        qh = mq[:, hs]
        kh = mk[:, hs]
        qb = qh.astype(BF16)
        kb = kh.astype(BF16)
        vb = v_ref[:, hs]
        s = lax.dot_general(qb, kb, (((1,), (1,)), ((), ())), preferred_element_type=F32) * dm
        c_old = c_state[h]
        n_old = n_state[h:h + 1, :]
        num = (w_inter * jnp.dot(qb, c_old.astype(BF16), preferred_element_type=F32)
               + jnp.dot(s.astype(BF16), vb, preferred_element_type=F32))
        den = (w_inter * jnp.sum(qh * n_old, axis=1, keepdims=True)
               + jnp.sum(s, axis=1, keepdims=True))
        hh = num * (1.0 / jnp.maximum(jnp.abs(den), jnp.exp(-m_t)))

        logw = a_end - a_c + li_c
        m_new = jnp.maximum(a_end + m_prev, jnp.max(logw, axis=0, keepdims=True))
        wk = jnp.exp(logw - m_new)
        decay = jnp.exp(a_end + m_prev - m_new)
        kw = kh * wk
        c_state[h] = decay * c_old + lax.dot_general(kw.astype(BF16), vb, (((0,), (0,)), ((), ())),
                                                     preferred_element_type=F32)
        n_state[h:h + 1, :] = decay * n_old + jnp.sum(kw, axis=0, keepdims=True)
        m_state[h:h + 1, :] = jnp.broadcast_to(m_new, (1, LANES))

        mu = jnp.mean(hh, axis=1, keepdims=True)
        dc = hh - mu
        var = jnp.mean(dc * dc, axis=1, keepdims=True)
        hn = dc * lax.rsqrt(var + NORM_EPS) * ng_ref[:, hs]
        ya = _sigmoid(og_ref[:, hs].astype(F32)) * hn * _silu(z_ref[:, hs].astype(F32))
        ya_ref[:, hs] = ya.astype(BF16)


def _mlstm(p, aux, b, t, conv_q_w, conv_q_b, conv_k_w, conv_k_b, sm_bias, mh_norm_g):
    L = MLSTM_CHUNK
    nc = t // L

    def colspec(cstart):
        return pl.BlockSpec((L, M_WIDTH), lambda bi, c: (bi * nc + c, cstart // M_WIDTH))

    def full(shape):
        return pl.BlockSpec(shape, lambda bi, c: (0,) * len(shape))

    return pl.pallas_call(
        _mlstm_kernel,
        out_shape=jax.ShapeDtypeStruct((b * t, M_WIDTH), BF16),
        grid=(b, nc),
        in_specs=[
            colspec(C_MQ), colspec(C_MK), colspec(C_MV), colspec(C_MO), colspec(C_MZ),
            pl.BlockSpec((L, LANES), lambda bi, c: (bi * nc + c, X_SM)),
            full((CONV_WIDTH, M_WIDTH)), full((1, M_WIDTH)), full((CONV_WIDTH, M_WIDTH)), full((1, M_WIDTH)),
            full((1, LANES)), full((1, M_WIDTH)),
        ],
        out_specs=pl.BlockSpec((L, M_WIDTH), lambda bi, c: (bi * nc + c, 0)),
        scratch_shapes=[
            pltpu.VMEM((SUBLANES, M_WIDTH), F32), pltpu.VMEM((SUBLANES, M_WIDTH), F32),
            pltpu.VMEM((M_HEADS, M_HEAD_DIM, M_HEAD_DIM), F32),
            pltpu.VMEM((SUBLANES, M_HEAD_DIM), F32), pltpu.VMEM((SUBLANES, LANES), F32),
        ],
        compiler_params=_cparams(("parallel", "arbitrary")),
        name="mlstm",
    )(p, p, p, p, p, aux, conv_q_w, conv_q_b, conv_k_w, conv_k_b, sm_bias, mh_norm_g)


def _compress_kernel(pk_ref, pv_ref, wt_ref, wb_ref, posk_ref, posv_ref, w1k_ref, w1v_ref, w2k_ref, w2v_ref,
                     kc_ref, vct_ref):
    nr = pk_ref.shape[0] // CMP_STRIDE
    cw = 2 * LANES
    u = jnp.zeros((nr, wt_ref.shape[1]), F32)
    v = jnp.zeros((nr, wb_ref.shape[1]), F32)
    for l in range(CMP_STRIDE):
        x = jnp.concatenate([pk_ref[pl.ds(l, nr, stride=CMP_STRIDE), :],
                             pv_ref[pl.ds(l, nr, stride=CMP_STRIDE), :]], axis=1).astype(BF16)
        u = u + jnp.dot(x, wt_ref[l * cw:(l + 1) * cw, :], preferred_element_type=F32)
        v = v + jnp.dot(x, wb_ref[l * cw:(l + 1) * cw, :], preferred_element_type=F32)
    v = pltpu.roll(v, nr - 1, axis=0)
    rowi = lax.broadcasted_iota(jnp.int32, v.shape, 0)
    v = jnp.where(rowi < nr - 1, v, 0.0)

    def pos_term(pos_ref, w_ref):
        pb = jnp.broadcast_to(pos_ref[...].astype(BF16), (SUBLANES, CMP_BLOCK * N_HEAD_DIM))
        return jnp.dot(pb, w_ref[...], preferred_element_type=F32)[0:1, :]

    pk = pos_term(posk_ref, w1k_ref)
    pv = pos_term(posv_ref, w1v_ref)
    hid = u + v + jnp.concatenate([pk, pk, pv, pv], axis=1)
    act = 0.5 * hid * (1.0 + jnp.tanh(math.sqrt(2.0 / math.pi) * (hid + 0.044715 * (hid * hid * hid))))
    for c in range(2 * N_KV_GROUPS):
        w2 = w2k_ref[...] if c < N_KV_GROUPS else w2v_ref[...]
        out = jnp.dot(act[:, c * CMP_HIDDEN:(c + 1) * CMP_HIDDEN].astype(BF16), w2,
                      preferred_element_type=F32)
        if c < N_KV_GROUPS:
            kc_ref[c] = out.astype(BF16)
        else:
            out_t = out.T
            ones_row = lax.broadcasted_iota(jnp.int32, out_t.shape, 0) == N_HEAD_DIM
            vct_ref[c - N_KV_GROUPS] = jnp.where(ones_row, 1.0, out_t).astype(BF16)


def _compress(aux, b, t, wt, wb, posk, posv, w1k, w1v, w2k, w2v):
    nr = t // CMP_STRIDE
    g = N_KV_GROUPS

    def full(a):
        return pl.BlockSpec(a.shape, lambda bi: (0,) * a.ndim)

    return pl.pallas_call(
        _compress_kernel,
        out_shape=(jax.ShapeDtypeStruct((b, g, nr, LANES), BF16), jax.ShapeDtypeStruct((b, g, LANES, nr), BF16)),
        grid=(b,),
        in_specs=[pl.BlockSpec((t, LANES), lambda bi: (bi, X_KC)),
                  pl.BlockSpec((t, LANES), lambda bi: (bi, X_VC)),
                  full(wt), full(wb), full(posk), full(posv), full(w1k), full(w1v), full(w2k), full(w2v)],
        out_specs=(pl.BlockSpec((None, g, nr, LANES), lambda bi: (bi, 0, 0, 0)),
                   pl.BlockSpec((None, g, LANES, nr), lambda bi: (bi, 0, 0, 0))),
        compiler_params=_cparams(("parallel",)),
        name="compress",
    )(aux, aux, wt, wb, posk, posv, w1k, w1v, w2k, w2v)


def _kvprep_kernel(q_ref, kv_ref, sm_ref, pos_ref, inv_ref, gb_ref,
                   qrawt_ref, qrope_ref, kst_ref, vs_ref, kwt_ref, vw_ref, gates_ref, gatest_ref):
    tk = q_ref.shape[0]
    lane = lax.broadcasted_iota(jnp.int32, (tk, LANES), 1)
    lih = lane % N_HEAD_DIM
    pos = jnp.broadcast_to(pos_ref[...], (LANES, tk)).T
    ang = pos * inv_ref[...]
    cos = jnp.cos(ang)
    sin = jnp.sin(ang)
    half = ROPE_DIM // 2
    s_up = jnp.where(lih < half, -sin, 0.0)
    s_dn = jnp.where((lih >= half) & (lih < ROPE_DIM), sin, 0.0)

    def rope(u):
        return u * cos + pltpu.roll(u, LANES - half, axis=1) * s_up + pltpu.roll(u, half, axis=1) * s_dn

    def head_pad(u, odd):
        if odd:
            u = pltpu.roll(u, N_HEAD_DIM, axis=1)
        return jnp.where(lane < N_HEAD_DIM, u, 0.0)

    scale = N_HEAD_DIM ** -0.5 * math.log2(math.e)
    zeros_t = jnp.zeros((N_HEAD_DIM, tk), BF16)
    for g in range(N_KV_GROUPS):
        for hp in range(N_HPG // 2):
            c0 = g * (N_HPG * N_HEAD_DIM) + hp * LANES
            u = q_ref[:, c0:c0 + LANES].astype(F32) * scale
            ur = rope(u)
            u_t = u.T.astype(BF16)
            for odd in range(2):
                h = 2 * hp + odd
                qrawt_ref[g, h * LANES:h * LANES + N_HEAD_DIM, :] = u_t[odd * N_HEAD_DIM:(odd + 1) * N_HEAD_DIM, :]
                qrawt_ref[g, h * LANES + N_HEAD_DIM:(h + 1) * LANES, :] = zeros_t
                qrope_ref[g, :, h * LANES:(h + 1) * LANES] = head_pad(ur, odd).astype(BF16)

    ks_t = rope(kv_ref[:, 0:LANES].astype(F32)).T
    kw_t = rope(kv_ref[:, 2 * LANES:3 * LANES].astype(F32)).T
    vs = kv_ref[:, LANES:2 * LANES].astype(F32)
    vw = kv_ref[:, 3 * LANES:4 * LANES].astype(F32)
    t_glob = pl.program_id(1) * tk + lax.broadcasted_iota(jnp.int32, (LANES, tk), 1)
    blk_row = lax.broadcasted_iota(jnp.int32, (LANES, tk), 0)
    onehot = jnp.where(blk_row == t_glob // SEL_BLOCK, 1.0, 0.0).astype(BF16)
    ones_lane = jnp.where(lane == N_HEAD_DIM, 1.0, 0.0)
    gates = _sigmoid(sm_ref[...] + gb_ref[...])
    for g in range(N_KV_GROUPS):
        rs = slice(g * N_HEAD_DIM, (g + 1) * N_HEAD_DIM)
        kst_ref[g, 0:LANES, :] = onehot
        kst_ref[g, LANES:LANES + N_HEAD_DIM, :] = ks_t[rs, :].astype(BF16)
        kst_ref[g, LANES + N_HEAD_DIM:2 * LANES, :] = zeros_t
        kwt_ref[g, 0:N_HEAD_DIM, :] = kw_t[rs, :].astype(BF16)
        kwt_ref[g, N_HEAD_DIM:LANES, :] = zeros_t
        vs_ref[g] = (head_pad(vs, g) + ones_lane).astype(BF16)
        vw_ref[g] = (head_pad(vw, g) + ones_lane).astype(BF16)
        goff = SM_G_OFF + g * (3 * N_HPG)
        gg = pltpu.roll(gates, LANES - goff, axis=1)
        gates_ref[g] = gg
        gatest_ref[g] = gg.T[0:GATE_ROWS, :]


def _kvprep(p, aux, b, t, pos_col, inv_row, gate_bias):
    tk = KVPREP_TK
    nt = t // tk
    g = N_KV_GROUPS
    qw = N_HPG * LANES

    def tok(shape):
        return pl.BlockSpec((None, g) + shape, lambda bi, i: (bi, 0, i, 0))

    def tok_t(rows):
        return pl.BlockSpec((None, g, rows, tk), lambda bi, i: (bi, 0, 0, i))

    return pl.pallas_call(
        _kvprep_kernel,
        out_shape=(
            jax.ShapeDtypeStruct((b, g, qw, t), BF16), jax.ShapeDtypeStruct((b, g, t, qw), BF16),
            jax.ShapeDtypeStruct((b, g, 2 * LANES, t), BF16), jax.ShapeDtypeStruct((b, g, t, LANES), BF16),
            jax.ShapeDtypeStruct((b, g, LANES, t), BF16), jax.ShapeDtypeStruct((b, g, t, LANES), BF16),
            jax.ShapeDtypeStruct((b, g, t, LANES), F32), jax.ShapeDtypeStruct((b, g, GATE_ROWS, t), F32),
        ),
        grid=(b, nt),
        in_specs=[
            pl.BlockSpec((tk, N_WIDTH), lambda bi, i: (bi * nt + i, C_NQ // N_WIDTH)),
            pl.BlockSpec((tk, 4 * LANES), lambda bi, i: (bi * nt + i, C_KV // (4 * LANES))),
            pl.BlockSpec((tk, LANES), lambda bi, i: (bi * nt + i, X_SM)),
            pl.BlockSpec((None, 1, tk), lambda bi, i: (bi, 0, i)),
            pl.BlockSpec((1, LANES), lambda bi, i: (0, 0)),
            pl.BlockSpec((1, LANES), lambda bi, i: (0, 0)),
        ],
        out_specs=(tok_t(qw), tok((tk, qw)), tok_t(2 * LANES), tok((tk, LANES)),
                   tok_t(LANES), tok((tk, LANES)), tok((tk, LANES)), tok_t(GATE_ROWS)),
        compiler_params=_cparams(("parallel", "arbitrary")),
        name="kvprep",
    )(p, p, aux, pos_col, inv_row, gate_bias)


CMP_ROW_CHUNK = 128


def _select_kernel(qt_ref, kc_ref, vct_ref, ovt_ref, gt_ref, oc_ref, bias_ref, imp_ref):
    qs = qt_ref.shape[1]
    nr = kc_ref.shape[0]
    t0 = pl.program_id(2) * qs
    sees_any = t0 + lax.broadcasted_iota(jnp.int32, (1, qs), 1) >= CMP_BLOCK - 1

    def cmp_branch(rows):
        m_c = (lax.broadcasted_iota(jnp.int32, (rows, qs), 0) * CMP_STRIDE + (CMP_BLOCK - 1)
               <= t0 + lax.broadcasted_iota(jnp.int32, (rows, qs), 1))
        kc = kc_ref[0:rows, :]
        vct = vct_ref[:, 0:rows]
        ovt = ovt_ref[:, 0:rows]
        imp = jnp.zeros((LANES, qs), F32)
        for h in range(N_HPG):
            s = jnp.dot(kc, qt_ref[h * LANES:(h + 1) * LANES, :], preferred_element_type=F32)
            s = jnp.where(m_c, s, NEG)
            e = jnp.exp2(s - jnp.max(s, axis=0, keepdims=True)).astype(BF16)
            r = jnp.dot(vct, e, preferred_element_type=F32)
            inv = jnp.where(sees_any, 1.0 / r[N_HEAD_DIM:N_HEAD_DIM + 1, :], 0.0)
            imp = imp + jnp.dot(ovt, e, preferred_element_type=F32) * inv
            oc_ref[:, h * LANES:(h + 1) * LANES] = (r * (inv * gt_ref[3 * h:3 * h + 1, :])).T.astype(BF16)
        imp_ref[...] = imp

    n_var = nr // CMP_ROW_CHUNK
    visible = (t0 + qs - CMP_BLOCK) // CMP_STRIDE + 1
    need = jnp.clip((visible + CMP_ROW_CHUNK - 1) // CMP_ROW_CHUNK, 1, n_var)
    for v in range(1, n_var + 1):
        pl.when(need == v)(functools.partial(cmp_branch, v * CMP_ROW_CHUNK))
    imp = imp_ref[...]

    blk = lax.broadcasted_iota(jnp.int32, (LANES, qs), 0)
    tq = t0 + lax.broadcasted_iota(jnp.int32, (LANES, qs), 1)
    cur = tq // SEL_BLOCK
    forced = (blk == 0) | (blk == cur) | (blk == cur - 1)
    n_sel = (nr * CMP_STRIDE) // SEL_BLOCK
    free_score = jnp.where(blk * SEL_BLOCK <= tq, imp, -1.0)
    blk_f = blk.astype(F32)
    n_top = min(SEL_TOPK, n_sel)
    n_forced = 3

    def pick(score, n_iter):
        score = jnp.where(blk < n_sel, score, -jnp.inf)
        for _ in range(n_iter):
            mx = jnp.max(score, axis=0, keepdims=True)
            idx = jnp.min(jnp.where(score == mx, blk_f, float(LANES)), axis=0, keepdims=True)
            score = jnp.where(blk_f == idx, -jnp.inf, score)
        bias_t = jnp.where((score == -jnp.inf) & (blk < n_sel), 0.0, NEG)
        bias_ref[...] = bias_t.T.astype(BF16)

    all_three_forced = (t0 >= 2 * SEL_BLOCK) & (n_top >= n_forced)

    @pl.when(all_three_forced)
    def _():
        pick(jnp.where(forced, -jnp.inf, free_score), n_top - n_forced)

    @pl.when(jnp.logical_not(all_three_forced))
    def _():
        pick(jnp.where(forced, FORCE, free_score), n_top)


def _select(qrawt, kc, vct, ovt, gates_t):
    b, g, qw, t = qrawt.shape
    nr = kc.shape[2]
    qs = SELECT_QS

    def whole(rows, cols):
        return pl.BlockSpec((None, None, rows, cols), lambda bi, gi, i: (bi, gi, 0, 0))

    return pl.pallas_call(
        _select_kernel,
        out_shape=(jax.ShapeDtypeStruct((b, g, t, qw), BF16), jax.ShapeDtypeStruct((b, g, t, LANES), BF16)),
        grid=(b, g, t // qs),
        in_specs=[
            pl.BlockSpec((None, None, qw, qs), lambda bi, gi, i: (bi, gi, 0, i)),
            whole(nr, LANES), whole(LANES, nr),
            pl.BlockSpec(ovt.shape, lambda bi, gi, i: (0, 0)),
            pl.BlockSpec((None, None, GATE_ROWS, qs), lambda bi, gi, i: (bi, gi, 0, i)),
        ],
        out_specs=(pl.BlockSpec((None, None, qs, qw), lambda bi, gi, i: (bi, gi, i, 0)),
                   pl.BlockSpec((None, None, qs, LANES), lambda bi, gi, i: (bi, gi, i, 0))),
        scratch_shapes=[pltpu.VMEM((LANES, qs), F32)],
        compiler_params=_cparams(("parallel", "parallel", "parallel")),
        name="select",
    )(qrawt, kc, vct, ovt, gates_t)


def _nsa_kernel(qrope_ref, bias_ref, oc_ref, kst_ref, vs_ref, kwt_ref, vw_ref, gates_ref, nz_ref,
                yb_ref, qaug_ref, acc_ref, m_ref, s_ref):
    i = pl.program_id(1)
    t0 = i * Q_BLOCK
    Q = Q_BLOCK
    nh = N_HPG
    ng = N_KV_GROUPS
    rows = nh * Q
    t0a = pl.multiple_of(t0, Q)
    causal = (lax.broadcasted_iota(jnp.int32, (Q, Q), 1) <= lax.broadcasted_iota(jnp.int32, (Q, Q), 0))
    lane = lax.broadcasted_iota(jnp.int32, (Q, LANES), 1)

    def q_rope(g):
        return jnp.concatenate([qrope_ref[g, :, h * LANES:(h + 1) * LANES] for h in range(nh)], axis=0)

    def mask_heads(s, m):
        return jnp.concatenate([jnp.where(m, s[h * Q:(h + 1) * Q, :], NEG) for h in range(nh)], axis=0)

    first_blk = t0 // SEL_BLOCK
    for g in range(ng):
        qr = q_rope(g)
        bias = bias_ref[g]
        bias_early = jnp.where(lane < first_blk, bias, NEG).astype(BF16)
        qaug_ref[g, :, 0:LANES] = jnp.concatenate([bias_early] * nh, axis=0)
        qaug_ref[g, :, LANES:2 * LANES] = qr
        q_diag = jnp.concatenate([jnp.concatenate([bias] * nh, axis=0), qr], axis=1)
        s_d = jnp.dot(q_diag, kst_ref[g, :, pl.ds(t0a, Q)], preferred_element_type=F32)
        s_d = mask_heads(s_d, causal)
        m0 = jnp.max(s_d, axis=1, keepdims=True)
        p_d = jnp.exp2(s_d - m0)
        m_ref[g] = jnp.broadcast_to(m0, (rows, LANES))
        acc_ref[g] = jnp.dot(p_d.astype(BF16), vs_ref[g, pl.ds(t0a, Q), :], preferred_element_type=F32)

    ncb = SEL_CHUNK // LANES
    n_chunks = (t0 + SEL_CHUNK - 1) // SEL_CHUNK
    last = jnp.maximum(n_chunks - 1, 0)

    def scores(g, c, slot):
        k0 = pl.multiple_of(c * SEL_CHUNK, SEL_CHUNK)
        s_ref[slot, g] = jnp.dot(qaug_ref[g], kst_ref[g, :, pl.ds(k0, SEL_CHUNK)], preferred_element_type=F32)

    def consume(g, c, slot):
        k0 = pl.multiple_of(c * SEL_CHUNK, SEL_CHUNK)
        s = s_ref[slot, g]
        m_old = m_ref[g]
        m_new = jnp.maximum(m_old, jnp.max(s, axis=1, keepdims=True))
        p = jnp.concatenate([jnp.exp2(s[:, cb * LANES:(cb + 1) * LANES] - m_new) for cb in range(ncb)], axis=1)
        acc_ref[g] = (jnp.exp2(m_old - m_new) * acc_ref[g]
                      + jnp.dot(p.astype(BF16), vs_ref[g, pl.ds(k0, SEL_CHUNK), :], preferred_element_type=F32))
        m_ref[g] = m_new

    for g in range(ng):
        scores(g, 0, 0)

    def steps(c, n):
        for u in range(n):
            for g in range(ng):
                scores(g, jnp.minimum(c + u + 1, last), (u + 1) % 2)
                consume(g, c + u, u % 2)

    def sel_quad(c4, carry):
        steps(SEL_UNROLL * c4, SEL_UNROLL)
        return carry

    n_quads = n_chunks // SEL_UNROLL
    lax.fori_loop(0, n_quads, sel_quad, 0)
    rem = n_chunks - SEL_UNROLL * n_quads

    @pl.when(rem >= 2)
    def _():
        steps(SEL_UNROLL * n_quads, 2)

    @pl.when(rem % 2 == 1)
    def _():
        for g in range(ng):
            consume(g, n_chunks - 1, 0)

    wlen = WINDOW + Q
    ws = pl.multiple_of(jnp.maximum(t0 - WINDOW, 0), Q)
    diff = (t0 + lax.broadcasted_iota(jnp.int32, (Q, wlen), 0)) - (ws + lax.broadcasted_iota(jnp.int32, (Q, wlen), 1))
    m_w = (diff >= 0) & (diff < WINDOW)
    for g in range(ng):
        gates = gates_ref[g]

        def gate_col(br):
            return jnp.concatenate([gates[:, 3 * h + br:3 * h + br + 1] for h in range(nh)], axis=0)

        acc = acc_ref[g]
        o_s = acc * (gate_col(1) / acc[:, N_HEAD_DIM:N_HEAD_DIM + 1])
        s_w = jnp.dot(q_rope(g), kwt_ref[g, :, pl.ds(ws, wlen)], preferred_element_type=F32)
        s_w = mask_heads(s_w, m_w)
        p_w = jnp.exp2(s_w - jnp.max(s_w, axis=1, keepdims=True))
        acc_w = jnp.dot(p_w.astype(BF16), vw_ref[g, pl.ds(ws, wlen), :], preferred_element_type=F32)
        o_w = acc_w * (gate_col(2) / acc_w[:, N_HEAD_DIM:N_HEAD_DIM + 1])
        outs = []
        for h in range(nh):
            rs = slice(h * Q, (h + 1) * Q)
            outs.append(oc_ref[g, :, h * LANES:(h + 1) * LANES].astype(F32) + o_s[rs, :] + o_w[rs, :])
        for hp in range(nh // 2):
            yn = jnp.where(lane < N_HEAD_DIM, outs[2 * hp], pltpu.roll(outs[2 * hp + 1], N_HEAD_DIM, axis=1))
            ls = slice((g * (nh // 2) + hp) * LANES, (g * (nh // 2) + hp + 1) * LANES)
            yb_ref[:, ls] = (yn * _silu(nz_ref[:, ls].astype(F32))).astype(BF16)


def _nsa(p, b, t, qrope, bias, oc, kst, vs, kwt, vw, gates):
    g = N_KV_GROUPS
    nqb = t // Q_BLOCK
    qw = N_HPG * LANES
    rows = N_HPG * Q_BLOCK

    def qblk(w):
        return pl.BlockSpec((None, g, Q_BLOCK, w), lambda bi, i: (bi, 0, i, 0))

    def whole(r, c):
        return pl.BlockSpec((None, g, r, c), lambda bi, i: (bi, 0, 0, 0), pipeline_mode=pl.Buffered(1))

    return pl.pallas_call(
        _nsa_kernel,
        out_shape=jax.ShapeDtypeStruct((b * t, N_WIDTH), BF16),
        grid=(b, nqb),
        in_specs=[
            qblk(qw), qblk(LANES), qblk(qw),
            whole(2 * LANES, t), whole(t, LANES), whole(LANES, t), whole(t, LANES),
            qblk(LANES),
            pl.BlockSpec((Q_BLOCK, N_WIDTH), lambda bi, i: (bi * nqb + i, C_NZ // N_WIDTH)),
        ],
        out_specs=pl.BlockSpec((Q_BLOCK, N_WIDTH), lambda bi, i: (bi * nqb + i, 0)),
        scratch_shapes=[pltpu.VMEM((g, rows, 2 * LANES), BF16), pltpu.VMEM((g, rows, LANES), F32),
                        pltpu.VMEM((g, rows, LANES), F32), pltpu.VMEM((2, g, rows, SEL_CHUNK), F32)],
        compiler_params=_cparams(("parallel", "arbitrary")),
        name="nsa",
    )(qrope, bias, oc, kst, vs, kwt, vw, gates, p)


def _outproj_kernel(final, ya_ref, yb_ref, ga_ref, gb_ref, x_ref, wa_ref, wb_ref, wo_ref, fg_ref, o_ref):
    a = jnp.dot(ya_ref[...], wa_ref[...], preferred_element_type=F32)
    bq = jnp.dot(yb_ref[...], wb_ref[...], preferred_element_type=F32)
    merged = _sigmoid(ga_ref[...].astype(F32)) * a + _sigmoid(gb_ref[...].astype(F32)) * bq
    y = x_ref[...] + jnp.dot(merged.astype(BF16), wo_ref[...], preferred_element_type=F32)
    if final:
        y = y * lax.rsqrt(jnp.mean(y * y, axis=-1, keepdims=True) + NORM_EPS) * fg_ref[...]
    o_ref[...] = y


def _outproj(ya, yb, p, x2, wa, wb, wo, fg, final):
    m = x2.shape[0]
    tm = OUTPROJ_TM

    def full(a):
        return pl.BlockSpec(a.shape, lambda i: (0, 0))

    return pl.pallas_call(
        functools.partial(_outproj_kernel, final),
        out_shape=jax.ShapeDtypeStruct((m, D_MODEL), F32),
        grid=(m // tm,),
        in_specs=[
            pl.BlockSpec((tm, M_WIDTH), lambda i: (i, 0)),
            pl.BlockSpec((tm, N_WIDTH), lambda i: (i, 0)),
            pl.BlockSpec((tm, D_MODEL), lambda i: (i, C_GA // D_MODEL)),
            pl.BlockSpec((tm, D_MODEL), lambda i: (i, C_GB // D_MODEL)),
            pl.BlockSpec((tm, D_MODEL), lambda i: (i, 0)),
            full(wa), full(wb), full(wo), full(fg),
        ],
        out_specs=pl.BlockSpec((tm, D_MODEL), lambda i: (i, 0)),
        compiler_params=_cparams(("parallel",)),
        name="outproj",
    )(ya, yb, p, p, x2, wa, wb, wo, fg)


def _permute_w_tail(w):
    pad = jnp.zeros((w.shape[0], P_W - _O_END), w.dtype)
    return jnp.concatenate([
        w[:, N_HEAD_TILES * PROJ_TN:_O_MI],
        w[:, _O_GA:_O_END],
        w[:, _O_NQ:_O_KC],
        w[:, _O_NZ:_O_GA],
        w[:, _O_KS:_O_NG],
        w[:, _O_KC:_O_KS],
        w[:, _O_MI:_O_NQ],
        w[:, _O_NG:_O_NZ],
        pad], axis=1)


def _expand_cmp_w1(w1k, w1v):
    half = CMP_BLOCK // 2
    k3 = w1k.astype(BF16).reshape(CMP_BLOCK, 1, N_HEAD_DIM, 1, CMP_HIDDEN)
    v3 = w1v.astype(BF16).reshape(CMP_BLOCK, 1, N_HEAD_DIM, 1, CMP_HIDDEN)
    nc = 2 * N_KV_GROUPS
    eye = jnp.eye(nc, dtype=BF16)
    is_k = (jnp.arange(nc) < N_KV_GROUPS).astype(BF16)
    mk = (eye * is_k[:, None]).reshape(1, nc, 1, nc, 1)
    mv = (eye * (1 - is_k)[:, None]).reshape(1, nc, 1, nc, 1)
    z = k3 * mk + v3 * mv
    z = z.reshape(CMP_BLOCK * nc * N_HEAD_DIM, nc * CMP_HIDDEN)
    return z[:half * nc * N_HEAD_DIM], z[half * nc * N_HEAD_DIM:]


def kernel(x, positions, norm_g, w_in, conv_q_w, conv_q_b, conv_k_w, conv_k_b, b_igate, b_fgate, mh_norm_g,
           cmp_pos_k, cmp_w1_k, cmp_w2_k, cmp_pos_v, cmp_w1_v, cmp_w2_v, b_nsa_gate, w_branch_a, w_branch_b,
           w_out, final_norm_g):
    b, t, _ = x.shape
    assert t % (2 * SEL_CHUNK) == 0 and t >= WINDOW + Q_BLOCK and t // SEL_BLOCK <= LANES and t % MLSTM_CHUNK == 0
    x2 = x.reshape(b * t, D_MODEL)
    for l in range(norm_g.shape[0]):
        w_bf = w_in[l].astype(BF16)
        p, aux = _inproj(x2, norm_g[l][None, :], w_bf, _permute_w_tail(w_bf))

        sm_bias = jnp.zeros((1, LANES), F32).at[0, 0:M_HEADS].set(b_igate[l]).at[0, M_HEADS:2 * M_HEADS].set(b_fgate[l])
        ya = _mlstm(p, aux, b, t, conv_q_w[l], conv_q_b[l][None, :], conv_k_w[l], conv_k_b[l][None, :], sm_bias,
                    mh_norm_g[l][None, :])

        nr = t // CMP_STRIDE
        wt, wb = _expand_cmp_w1(cmp_w1_k[l], cmp_w1_v[l])
        w2pad = lambda w: jnp.pad(w, ((0, 0), (0, LANES - N_HEAD_DIM))).astype(BF16)
        kc, vct = _compress(aux, b, t, wt, wb, cmp_pos_k[l].reshape(1, -1), cmp_pos_v[l].reshape(1, -1),
                            cmp_w1_k[l].astype(BF16), cmp_w1_v[l].astype(BF16), w2pad(cmp_w2_k[l]), w2pad(cmp_w2_v[l]))

        half = ROPE_DIM // 2
        inv = jnp.power(jnp.float32(ROPE_THETA), -jnp.arange(half, dtype=F32) * (2.0 / ROPE_DIM))
        lih = jnp.arange(LANES) % N_HEAD_DIM
        inv_row = jnp.where(lih < ROPE_DIM, inv[lih % half], 0.0)[None, :].astype(F32)
        pos_col = positions.astype(F32)[:, None, :]
        gate_bias = jnp.zeros((1, LANES), F32).at[0, SM_G_OFF:SM_G_OFF + 3 * N_HEADS].set(b_nsa_gate[l])
        qrawt, qrope, kst, vs, kwt, vw, gates, gates_t = _kvprep(p, aux, b, t, pos_col, inv_row, gate_bias)

        ci = np.arange(nr)[None, :] * CMP_STRIDE
        sj = np.arange(LANES)[:, None] * SEL_BLOCK
        overlap_t = jnp.asarray((ci < sj + SEL_BLOCK) & (ci + CMP_BLOCK > sj) & (np.arange(nr)[None, :] < nr - 1)
                                & (np.arange(LANES)[:, None] < t // SEL_BLOCK), dtype=BF16)
        oc, bias = _select(qrawt, kc, vct, overlap_t, gates_t)
        yb = _nsa(p, b, t, qrope, bias, oc, kst, vs, kwt, vw, gates)

        x2 = _outproj(ya, yb, p, x2, w_branch_a[l].astype(BF16), w_branch_b[l].astype(BF16),
                      w_out[l].astype(BF16), final_norm_g[None, :], l == norm_g.shape[0] - 1)
    return x2.reshape(b, t, D_MODEL)
```

```python
import functools
import math

import jax
import jax.numpy as jnp
import numpy as np
from jax import lax
from jax.experimental import pallas as pl
from jax.experimental.pallas import tpu as pltpu

F32 = jnp.float32
BF16 = jnp.bfloat16

D_MODEL = 1024
M_HEADS = 4
M_HEAD_DIM = 256
M_WIDTH = M_HEADS * M_HEAD_DIM
CONV_WIDTH = 4
N_HEADS = 8
N_KV_GROUPS = 2
N_HPG = N_HEADS // N_KV_GROUPS
N_HEAD_DIM = 64
N_WIDTH = N_HEADS * N_HEAD_DIM
CMP_BLOCK = 32
CMP_STRIDE = 16
CMP_HIDDEN = 128
SEL_BLOCK = 64
SEL_TOPK = 16
WINDOW = 512
Q_BLOCK = 256
ROPE_THETA = 500000.0
ROPE_DIM = N_HEAD_DIM // 4
NORM_EPS = 1e-6
NEG = -1e30
FORCE = 1e9

LANES = 128
SUBLANES = 8
MLSTM_CHUNK = 256
SEL_CHUNK = 512
SEL_UNROLL = 4
V7X_VMEM_BYTES = 64 * 1024 * 1024
VMEM_LIMIT = V7X_VMEM_BYTES - 8 * 1024 * 1024
INPROJ_VMEM_LIMIT = V7X_VMEM_BYTES - 4 * 1024 * 1024
INPROJ_TM, OUTPROJ_TM, KVPREP_TK, SELECT_QS = 2048, 1024, 1024, 2048
SM_F_OFF = M_HEADS
SM_G_OFF = 2 * M_HEADS
GATE_ROWS = -(-3 * N_HPG // SUBLANES) * SUBLANES

C_MQ, C_MK, C_MV, C_MO, C_MZ, C_GA, C_GB = 0, 1024, 2048, 3072, 4096, 5120, 6144
C_NQ, C_NZ = 7168, 7680
C_KV = 8192
C_CMP = 8704
C_SM = 8960
P_W = 9216
AUX_W = 3 * LANES
X_KC, X_VC, X_SM = 0, 1, 2
_O_MI, _O_NQ, _O_KC, _O_KS, _O_NG, _O_NZ, _O_GA, _O_END = 5120, 5128, 5640, 5896, 6408, 6432, 6944, 8992


def _cparams(sem, vmem_limit=VMEM_LIMIT):
    return pltpu.CompilerParams(dimension_semantics=sem, vmem_limit_bytes=vmem_limit)


def _sigmoid(x):
    return 0.5 * jnp.tanh(0.5 * x) + 0.5


def _silu(x):
    h = 0.5 * x
    return h + h * jnp.tanh(h)


def _log_sigmoid(x):
    return jnp.minimum(x, 0.0) - jnp.log1p(jnp.exp(-jnp.abs(x)))


PROJ_TN = 1536
AUX_TILE = C_CMP // PROJ_TN
AUX_OFF = C_CMP % PROJ_TN


N_HEAD_TILES = _O_MI // PROJ_TN


def _inproj_kernel(x_ref, g_ref, wh_ref, wt_ref, o_ref, aux_ref, hn_ref):
    j = pl.program_id(1)

    @pl.when(j == 0)
    def _():
        x = x_ref[...]
        r = lax.rsqrt(jnp.mean(x * x, axis=-1, keepdims=True) + NORM_EPS)
        hn_ref[...] = (x * r * g_ref[...]).astype(BF16)

    @pl.when(j < N_HEAD_TILES)
    def _():
        o_ref[...] = jnp.dot(hn_ref[...], wh_ref[...], preferred_element_type=F32).astype(BF16)

    @pl.when((j >= N_HEAD_TILES) & (j != AUX_TILE))
    def _():
        o_ref[...] = jnp.dot(hn_ref[...], wt_ref[...], preferred_element_type=F32).astype(BF16)

    @pl.when(j == AUX_TILE)
    def _():
        acc = jnp.dot(hn_ref[...], wt_ref[...], preferred_element_type=F32)
        o_ref[...] = acc.astype(BF16)
        aux_ref[...] = acc[:, AUX_OFF:AUX_OFF + AUX_W]


def _inproj(x2, norm_g, w_head, w_tail):
    m = x2.shape[0]
    tm, tn = INPROJ_TM, PROJ_TN
    return pl.pallas_call(
        _inproj_kernel,
        out_shape=(jax.ShapeDtypeStruct((m, P_W), BF16), jax.ShapeDtypeStruct((m, AUX_W), F32)),
        grid=(m // tm, P_W // tn),
        in_specs=[
            pl.BlockSpec((tm, D_MODEL), lambda i, j: (i, 0)),
            pl.BlockSpec((1, D_MODEL), lambda i, j: (0, 0)),
            pl.BlockSpec((D_MODEL, tn), lambda i, j: (0, jnp.minimum(j, N_HEAD_TILES - 1))),
            pl.BlockSpec((D_MODEL, tn), lambda i, j: (0, jnp.maximum(j - N_HEAD_TILES, 0))),
        ],
        out_specs=(pl.BlockSpec((tm, tn), lambda i, j: (i, j)), pl.BlockSpec((tm, AUX_W), lambda i, j: (i, 0))),
        scratch_shapes=[pltpu.VMEM((tm, D_MODEL), BF16)],
        compiler_params=_cparams(("parallel", "arbitrary"), INPROJ_VMEM_LIMIT),
        name="inproj",
    )(x2, norm_g, w_head, w_tail)


def _causal_conv_silu(u16, prev_ref, w_ref, b_ref):
    n, w = u16.shape
    u = u16.astype(F32)
    row = lax.broadcasted_iota(jnp.int32, (n, n), 0)
    col = lax.broadcasted_iota(jnp.int32, (n, n), 1)
    sub = lax.broadcasted_iota(jnp.int32, (SUBLANES, w), 0)
    prev = prev_ref[...]
    y = b_ref[...] + u * w_ref[CONV_WIDTH - 1:CONV_WIDTH, :]
    for d in range(1, CONV_WIDTH):
        shift = jnp.where(row - col == d, 1.0, 0.0).astype(BF16)
        sh = jnp.dot(shift, u16, preferred_element_type=F32)
        head = jnp.where(sub < d, pltpu.roll(prev, d, axis=0), sh[0:SUBLANES, :])
        sh = jnp.concatenate([head, sh[SUBLANES:, :]], axis=0)
        y = y + sh * w_ref[CONV_WIDTH - 1 - d:CONV_WIDTH - d, :]
    prev_ref[...] = u[n - SUBLANES:n, :]
    return _silu(y)


def _mlstm_kernel(q_ref, k_ref, v_ref, og_ref, z_ref, sm_ref, cqw_ref, cqb_ref, ckw_ref, ckb_ref,
                  sb_ref, ng_ref, ya_ref, qprev, kprev, c_state, n_state, m_state):
    L = q_ref.shape[0]
    d = M_HEAD_DIM

    @pl.when(pl.program_id(1) == 0)
    def _():
        qprev[...] = jnp.zeros_like(qprev)
        kprev[...] = jnp.zeros_like(kprev)
        c_state[...] = jnp.zeros_like(c_state)
        n_state[...] = jnp.zeros_like(n_state)
        m_state[...] = jnp.zeros_like(m_state)

    mq = _causal_conv_silu(q_ref[...], qprev, cqw_ref, cqb_ref)
    mk = _causal_conv_silu(k_ref[...], kprev, ckw_ref, ckb_ref) * (d ** -0.5)

    sm = sm_ref[...] + sb_ref[...]
    lf = _log_sigmoid(sm)
    row = lax.broadcasted_iota(jnp.int32, (L, L), 0)
    col = lax.broadcasted_iota(jnp.int32, (L, L), 1)
    causal = row >= col
    tril = causal.astype(F32)
    triu = (row <= col).astype(F32)
    a_col = jnp.dot(tril, lf, precision=lax.Precision.HIGHEST, preferred_element_type=F32)
    sm_t = sm.T[0:SUBLANES, :]
    a_row = jnp.dot(_log_sigmoid(sm_t), triu, precision=lax.Precision.HIGHEST,
                    preferred_element_type=F32)

    for h in range(M_HEADS):
        hs = slice(h * d, (h + 1) * d)
        a_c = a_col[:, SM_F_OFF + h:SM_F_OFF + h + 1]
        li_c = sm[:, h:h + 1]
        a_r = a_row[SM_F_OFF + h:SM_F_OFF + h + 1, :]
        li_r = sm_t[h:h + 1, :]
        m_prev = m_state[h:h + 1, 0:1]
        a_end = a_c[L - 1:L, :]

        log_d = jnp.where(causal, a_c - (a_r - li_r), -jnp.inf)
        inter = a_c + m_prev
        m_t = jnp.maximum(inter, jnp.max(log_d, axis=1, keepdims=True))
        w_inter = jnp.exp(inter - m_t)
        dm = jnp.exp(log_d - m_t)

        qh = mq[:, hs]
        kh = mk[:, hs]
        qb = qh.astype(BF16)
        kb = kh.astype(BF16)
        vb = v_ref[:, hs]
        s = lax.dot_general(qb, kb, (((1,), (1,)), ((), ())), preferred_element_type=F32) * dm
        c_old = c_state[h]
        n_old = n_state[h:h + 1, :]
        num = (w_inter * jnp.dot(qb, c_old.astype(BF16), preferred_element_type=F32)
               + jnp.dot(s.astype(BF16), vb, preferred_element_type=F32))
        den = (w_inter * jnp.sum(qh * n_old, axis=1, keepdims=True)
               + jnp.sum(s, axis=1, keepdims=True))
        hh = num * (1.0 / jnp.maximum(jnp.abs(den), jnp.exp(-m_t)))

        logw = a_end - a_c + li_c
        m_new = jnp.maximum(a_end + m_prev, jnp.max(logw, axis=0, keepdims=True))
        wk = jnp.exp(logw - m_new)
        decay = jnp.exp(a_end + m_prev - m_new)
        kw = kh * wk
        c_state[h] = decay * c_old + lax.dot_general(kw.astype(BF16), vb, (((0,), (0,)), ((), ())),
                                                     preferred_element_type=F32)
        n_state[h:h + 1, :] = decay * n_old + jnp.sum(kw, axis=0, keepdims=True)
        m_state[h:h + 1, :] = jnp.broadcast_to(m_new, (1, LANES))

        mu = jnp.mean(hh, axis=1, keepdims=True)
        dc = hh - mu
        var = jnp.mean(dc * dc, axis=1, keepdims=True)
        hn = dc * lax.rsqrt(var + NORM_EPS) * ng_ref[:, hs]
        ya = _sigmoid(og_ref[:, hs].astype(F32)) * hn * _silu(z_ref[:, hs].astype(F32))
        ya_ref[:, hs] = ya.astype(BF16)


def _mlstm(p, aux, b, t, conv_q_w, conv_q_b, conv_k_w, conv_k_b, sm_bias, mh_norm_g):
    L = MLSTM_CHUNK
    nc = t // L

    def colspec(cstart):
        return pl.BlockSpec((L, M_WIDTH), lambda bi, c: (bi * nc + c, cstart // M_WIDTH))

    def full(shape):
        return pl.BlockSpec(shape, lambda bi, c: (0,) * len(shape))

    return pl.pallas_call(
        _mlstm_kernel,
        out_shape=jax.ShapeDtypeStruct((b * t, M_WIDTH), BF16),
        grid=(b, nc),
        in_specs=[
            colspec(C_MQ), colspec(C_MK), colspec(C_MV), colspec(C_MO), colspec(C_MZ),
            pl.BlockSpec((L, LANES), lambda bi, c: (bi * nc + c, X_SM)),
            full((CONV_WIDTH, M_WIDTH)), full((1, M_WIDTH)), full((CONV_WIDTH, M_WIDTH)), full((1, M_WIDTH)),
            full((1, LANES)), full((1, M_WIDTH)),
        ],
        out_specs=pl.BlockSpec((L, M_WIDTH), lambda bi, c: (bi * nc + c, 0)),
        scratch_shapes=[
            pltpu.VMEM((SUBLANES, M_WIDTH), F32), pltpu.VMEM((SUBLANES, M_WIDTH), F32),
            pltpu.VMEM((M_HEADS, M_HEAD_DIM, M_HEAD_DIM), F32),
            pltpu.VMEM((SUBLANES, M_HEAD_DIM), F32), pltpu.VMEM((SUBLANES, LANES), F32),
        ],
        compiler_params=_cparams(("parallel", "arbitrary")),
        name="mlstm",
    )(p, p, p, p, p, aux, conv_q_w, conv_q_b, conv_k_w, conv_k_b, sm_bias, mh_norm_g)


def _compress_kernel(pk_ref, pv_ref, wt_ref, wb_ref, posk_ref, posv_ref, w1k_ref, w1v_ref, w2k_ref, w2v_ref,
                     kc_ref, vct_ref):
    nr = pk_ref.shape[0] // CMP_STRIDE
    cw = 2 * LANES
    u = jnp.zeros((nr, wt_ref.shape[1]), F32)
    v = jnp.zeros((nr, wb_ref.shape[1]), F32)
    for l in range(CMP_STRIDE):
        x = jnp.concatenate([pk_ref[pl.ds(l, nr, stride=CMP_STRIDE), :],
                             pv_ref[pl.ds(l, nr, stride=CMP_STRIDE), :]], axis=1).astype(BF16)
        u = u + jnp.dot(x, wt_ref[l * cw:(l + 1) * cw, :], preferred_element_type=F32)
        v = v + jnp.dot(x, wb_ref[l * cw:(l + 1) * cw, :], preferred_element_type=F32)
    v = pltpu.roll(v, nr - 1, axis=0)
    rowi = lax.broadcasted_iota(jnp.int32, v.shape, 0)
    v = jnp.where(rowi < nr - 1, v, 0.0)

    def pos_term(pos_ref, w_ref):
        pb = jnp.broadcast_to(pos_ref[...].astype(BF16), (SUBLANES, CMP_BLOCK * N_HEAD_DIM))
        return jnp.dot(pb, w_ref[...], preferred_element_type=F32)[0:1, :]

    pk = pos_term(posk_ref, w1k_ref)
    pv = pos_term(posv_ref, w1v_ref)
    hid = u + v + jnp.concatenate([pk, pk, pv, pv], axis=1)
    act = 0.5 * hid * (1.0 + jnp.tanh(math.sqrt(2.0 / math.pi) * (hid + 0.044715 * (hid * hid * hid))))
    for c in range(2 * N_KV_GROUPS):
        w2 = w2k_ref[...] if c < N_KV_GROUPS else w2v_ref[...]
        out = jnp.dot(act[:, c * CMP_HIDDEN:(c + 1) * CMP_HIDDEN].astype(BF16), w2,
                      preferred_element_type=F32)
        if c < N_KV_GROUPS:
            kc_ref[c] = out.astype(BF16)
        else:
            out_t = out.T
            ones_row = lax.broadcasted_iota(jnp.int32, out_t.shape, 0) == N_HEAD_DIM
            vct_ref[c - N_KV_GROUPS] = jnp.where(ones_row, 1.0, out_t).astype(BF16)


def _compress(aux, b, t, wt, wb, posk, posv, w1k, w1v, w2k, w2v):
    nr = t // CMP_STRIDE
    g = N_KV_GROUPS

    def full(a):
        return pl.BlockSpec(a.shape, lambda bi: (0,) * a.ndim)

    return pl.pallas_call(
        _compress_kernel,
        out_shape=(jax.ShapeDtypeStruct((b, g, nr, LANES), BF16), jax.ShapeDtypeStruct((b, g, LANES, nr), BF16)),
        grid=(b,),
        in_specs=[pl.BlockSpec((t, LANES), lambda bi: (bi, X_KC)),
                  pl.BlockSpec((t, LANES), lambda bi: (bi, X_VC)),
                  full(wt), full(wb), full(posk), full(posv), full(w1k), full(w1v), full(w2k), full(w2v)],
        out_specs=(pl.BlockSpec((None, g, nr, LANES), lambda bi: (bi, 0, 0, 0)),
                   pl.BlockSpec((None, g, LANES, nr), lambda bi: (bi, 0, 0, 0))),
        compiler_params=_cparams(("parallel",)),
        name="compress",
    )(aux, aux, wt, wb, posk, posv, w1k, w1v, w2k, w2v)


def _kvprep_kernel(q_ref, kv_ref, sm_ref, pos_ref, inv_ref, gb_ref,
                   qrawt_ref, qrope_ref, kst_ref, vs_ref, kwt_ref, vw_ref, gates_ref, gatest_ref):
    tk = q_ref.shape[0]
    lane = lax.broadcasted_iota(jnp.int32, (tk, LANES), 1)
    half = ROPE_DIM // 2
    ang = inv_ref[...] * pos_ref[...]
    cos8 = jnp.cos(ang)
    sin8 = jnp.sin(ang)
    rest = N_HEAD_DIM - ROPE_DIM
    one_r, zero_r, zero_h = jnp.ones((rest, tk), F32), jnp.zeros((rest, tk), F32), jnp.zeros((half, tk), F32)
    heads = LANES // N_HEAD_DIM
    cos = jnp.concatenate([cos8, cos8, one_r] * heads, axis=0).T
    s_up = jnp.concatenate([-sin8, zero_h, zero_r] * heads, axis=0).T
    s_dn = jnp.concatenate([zero_h, sin8, zero_r] * heads, axis=0).T

    def rope(u):
        return u * cos + pltpu.roll(u, LANES - half, axis=1) * s_up + pltpu.roll(u, half, axis=1) * s_dn

    def head_pad(u, odd):
        if odd:
            u = pltpu.roll(u, N_HEAD_DIM, axis=1)
        return jnp.where(lane < N_HEAD_DIM, u, 0.0)

    scale = N_HEAD_DIM ** -0.5 * math.log2(math.e)
    zeros_t = jnp.zeros((N_HEAD_DIM, tk), BF16)
    for g in range(N_KV_GROUPS):
        for hp in range(N_HPG // 2):
            c0 = g * (N_HPG * N_HEAD_DIM) + hp * LANES
            u = q_ref[:, c0:c0 + LANES].astype(F32) * scale
            ur = rope(u)
            u_t = u.T.astype(BF16)
            for odd in range(2):
                h = 2 * hp + odd
                qrawt_ref[g, h * LANES:h * LANES + N_HEAD_DIM, :] = u_t[odd * N_HEAD_DIM:(odd + 1) * N_HEAD_DIM, :]
                qrawt_ref[g, h * LANES + N_HEAD_DIM:(h + 1) * LANES, :] = zeros_t
                qrope_ref[g, :, h * LANES:(h + 1) * LANES] = head_pad(ur, odd).astype(BF16)

    ks_t = rope(kv_ref[:, 0:LANES].astype(F32)).T
    kw_t = rope(kv_ref[:, 2 * LANES:3 * LANES].astype(F32)).T
    vs = kv_ref[:, LANES:2 * LANES].astype(F32)
    vw = kv_ref[:, 3 * LANES:4 * LANES].astype(F32)
    t_glob = pl.program_id(1) * tk + lax.broadcasted_iota(jnp.int32, (LANES, tk), 1)
    blk_row = lax.broadcasted_iota(jnp.int32, (LANES, tk), 0)
    onehot = jnp.where(blk_row == t_glob // SEL_BLOCK, 1.0, 0.0).astype(BF16)
    ones_lane = jnp.where(lane == N_HEAD_DIM, 1.0, 0.0)
    gates = _sigmoid(sm_ref[...] + gb_ref[...])
    for g in range(N_KV_GROUPS):
        rs = slice(g * N_HEAD_DIM, (g + 1) * N_HEAD_DIM)
        kst_ref[g, 0:LANES, :] = onehot
        kst_ref[g, LANES:LANES + N_HEAD_DIM, :] = ks_t[rs, :].astype(BF16)
        kst_ref[g, LANES + N_HEAD_DIM:2 * LANES, :] = zeros_t
        kwt_ref[g, 0:N_HEAD_DIM, :] = kw_t[rs, :].astype(BF16)
        kwt_ref[g, N_HEAD_DIM:LANES, :] = zeros_t
        vs_ref[g] = (head_pad(vs, g) + ones_lane).astype(BF16)
        vw_ref[g] = (head_pad(vw, g) + ones_lane).astype(BF16)
        goff = SM_G_OFF + g * (3 * N_HPG)
        gg = pltpu.roll(gates, LANES - goff, axis=1)
        gates_ref[g] = gg
        gatest_ref[g] = gg.T[0:GATE_ROWS, :]


def _kvprep(p, aux, b, t, pos_col, inv_row, gate_bias):
    tk = KVPREP_TK
    nt = t // tk
    g = N_KV_GROUPS
    qw = N_HPG * LANES

    def tok(shape):
        return pl.BlockSpec((None, g) + shape, lambda bi, i: (bi, 0, i, 0))

    def tok_t(rows):
        return pl.BlockSpec((None, g, rows, tk), lambda bi, i: (bi, 0, 0, i))

    return pl.pallas_call(
        _kvprep_kernel,
        out_shape=(
            jax.ShapeDtypeStruct((b, g, qw, t), BF16), jax.ShapeDtypeStruct((b, g, t, qw), BF16),
            jax.ShapeDtypeStruct((b, g, 2 * LANES, t), BF16), jax.ShapeDtypeStruct((b, g, t, LANES), BF16),
            jax.ShapeDtypeStruct((b, g, LANES, t), BF16), jax.ShapeDtypeStruct((b, g, t, LANES), BF16),
            jax.ShapeDtypeStruct((b, g, t, LANES), F32), jax.ShapeDtypeStruct((b, g, GATE_ROWS, t), F32),
        ),
        grid=(b, nt),
        in_specs=[
            pl.BlockSpec((tk, N_WIDTH), lambda bi, i: (bi * nt + i, C_NQ // N_WIDTH)),
            pl.BlockSpec((tk, 4 * LANES), lambda bi, i: (bi * nt + i, C_KV // (4 * LANES))),
            pl.BlockSpec((tk, LANES), lambda bi, i: (bi * nt + i, X_SM)),
            pl.BlockSpec((None, 1, tk), lambda bi, i: (bi, 0, i)),
            pl.BlockSpec(inv_row.shape, lambda bi, i: (0, 0)),
            pl.BlockSpec((1, LANES), lambda bi, i: (0, 0)),
        ],
        out_specs=(tok_t(qw), tok((tk, qw)), tok_t(2 * LANES), tok((tk, LANES)),
                   tok_t(LANES), tok((tk, LANES)), tok((tk, LANES)), tok_t(GATE_ROWS)),
        compiler_params=_cparams(("parallel", "arbitrary")),
        name="kvprep",
    )(p, p, aux, pos_col, inv_row, gate_bias)


CMP_ROW_CHUNK = 128


def _select_kernel(qt_ref, kc_ref, vct_ref, ovt_ref, gt_ref, oc_ref, bias_ref, imp_ref):
    qs = qt_ref.shape[1]
    nr = kc_ref.shape[0]
    t0 = pl.program_id(2) * qs
    sees_any = t0 + lax.broadcasted_iota(jnp.int32, (1, qs), 1) >= CMP_BLOCK - 1

    def cmp_branch(rows):
        m_c = (lax.broadcasted_iota(jnp.int32, (rows, qs), 0) * CMP_STRIDE + (CMP_BLOCK - 1)
               <= t0 + lax.broadcasted_iota(jnp.int32, (rows, qs), 1))
        kc = kc_ref[0:rows, :]
        vct = vct_ref[:, 0:rows]
        ovt = ovt_ref[:, 0:rows]
        imp = jnp.zeros((LANES, qs), F32)
        for h in range(N_HPG):
            s = jnp.dot(kc, qt_ref[h * LANES:(h + 1) * LANES, :], preferred_element_type=F32)
            s = jnp.where(m_c, s, NEG)
            e = jnp.exp2(s - jnp.max(s, axis=0, keepdims=True)).astype(BF16)
            r = jnp.dot(vct, e, preferred_element_type=F32)
            inv = jnp.where(sees_any, 1.0 / r[N_HEAD_DIM:N_HEAD_DIM + 1, :], 0.0)
            imp = imp + jnp.dot(ovt, e, preferred_element_type=F32) * inv
            oc_ref[:, h * LANES:(h + 1) * LANES] = (r * (inv * gt_ref[3 * h:3 * h + 1, :])).T.astype(BF16)
        imp_ref[...] = imp

    n_var = nr // CMP_ROW_CHUNK
    visible = (t0 + qs - CMP_BLOCK) // CMP_STRIDE + 1
    need = jnp.clip((visible + CMP_ROW_CHUNK - 1) // CMP_ROW_CHUNK, 1, n_var)
    for v in range(1, n_var + 1):
        pl.when(need == v)(functools.partial(cmp_branch, v * CMP_ROW_CHUNK))
    imp = imp_ref[...]

    blk = lax.broadcasted_iota(jnp.int32, (LANES, qs), 0)
    tq = t0 + lax.broadcasted_iota(jnp.int32, (LANES, qs), 1)
    cur = tq // SEL_BLOCK
    forced = (blk == 0) | (blk == cur) | (blk == cur - 1)
    n_sel = (nr * CMP_STRIDE) // SEL_BLOCK
    free_score = jnp.where(blk * SEL_BLOCK <= tq, imp, -1.0)
    blk_f = blk.astype(F32)
    n_top = min(SEL_TOPK, n_sel)
    n_forced = 3

    def pick(score, n_iter):
        score = jnp.where(blk < n_sel, score, -jnp.inf)
        for _ in range(n_iter):
            mx = jnp.max(score, axis=0, keepdims=True)
            idx = jnp.min(jnp.where(score == mx, blk_f, float(LANES)), axis=0, keepdims=True)
            score = jnp.where(blk_f == idx, -jnp.inf, score)
        bias_t = jnp.where((score == -jnp.inf) & (blk < n_sel), 0.0, NEG)
        bias_ref[...] = bias_t.T.astype(BF16)

    all_three_forced = (t0 >= 2 * SEL_BLOCK) & (n_top >= n_forced)

    @pl.when(all_three_forced)
    def _():
        pick(jnp.where(forced, -jnp.inf, free_score), n_top - n_forced)

    @pl.when(jnp.logical_not(all_three_forced))
    def _():
        pick(jnp.where(forced, FORCE, free_score), n_top)


def _select(qrawt, kc, vct, ovt, gates_t):
    b, g, qw, t = qrawt.shape
    nr = kc.shape[2]
    qs = SELECT_QS

    def whole(rows, cols):
        return pl.BlockSpec((None, None, rows, cols), lambda bi, gi, i: (bi, gi, 0, 0))

    return pl.pallas_call(
        _select_kernel,
        out_shape=(jax.ShapeDtypeStruct((b, g, t, qw), BF16), jax.ShapeDtypeStruct((b, g, t, LANES), BF16)),
        grid=(b, g, t // qs),
        in_specs=[
            pl.BlockSpec((None, None, qw, qs), lambda bi, gi, i: (bi, gi, 0, i)),
            whole(nr, LANES), whole(LANES, nr),
            pl.BlockSpec(ovt.shape, lambda bi, gi, i: (0, 0)),
            pl.BlockSpec((None, None, GATE_ROWS, qs), lambda bi, gi, i: (bi, gi, 0, i)),
        ],
        out_specs=(pl.BlockSpec((None, None, qs, qw), lambda bi, gi, i: (bi, gi, i, 0)),
                   pl.BlockSpec((None, None, qs, LANES), lambda bi, gi, i: (bi, gi, i, 0))),
        scratch_shapes=[pltpu.VMEM((LANES, qs), F32)],
        compiler_params=_cparams(("parallel", "parallel", "parallel")),
        name="select",
    )(qrawt, kc, vct, ovt, gates_t)


def _nsa_kernel(qrope_ref, bias_ref, oc_ref, kst_ref, vs_ref, kwt_ref, vw_ref, gates_ref, nz_ref,
                yb_ref, qaug_ref, acc_ref, m_ref, s_ref):
    i = pl.program_id(1)
    t0 = i * Q_BLOCK
    Q = Q_BLOCK
    nh = N_HPG
    ng = N_KV_GROUPS
    rows = nh * Q
    t0a = pl.multiple_of(t0, Q)
    causal = (lax.broadcasted_iota(jnp.int32, (Q, Q), 1) <= lax.broadcasted_iota(jnp.int32, (Q, Q), 0))
    lane = lax.broadcasted_iota(jnp.int32, (Q, LANES), 1)

    def q_rope(g):
        return jnp.concatenate([qrope_ref[g, :, h * LANES:(h + 1) * LANES] for h in range(nh)], axis=0)

    def mask_heads(s, m):
        return jnp.concatenate([jnp.where(m, s[h * Q:(h + 1) * Q, :], NEG) for h in range(nh)], axis=0)

    first_blk = t0 // SEL_BLOCK
    for g in range(ng):
        qr = q_rope(g)
        bias = bias_ref[g]
        bias_early = jnp.where(lane < first_blk, bias, NEG).astype(BF16)
        qaug_ref[g, :, 0:LANES] = jnp.concatenate([bias_early] * nh, axis=0)
        qaug_ref[g, :, LANES:2 * LANES] = qr
        q_diag = jnp.concatenate([jnp.concatenate([bias] * nh, axis=0), qr], axis=1)
        s_d = jnp.dot(q_diag, kst_ref[g, :, pl.ds(t0a, Q)], preferred_element_type=F32)
        s_d = mask_heads(s_d, causal)
        m0 = jnp.max(s_d, axis=1, keepdims=True)
        p_d = jnp.exp2(s_d - m0)
        m_ref[g] = jnp.broadcast_to(m0, (rows, LANES))
        acc_ref[g] = jnp.dot(p_d.astype(BF16), vs_ref[g, pl.ds(t0a, Q), :], preferred_element_type=F32)

    ncb = SEL_CHUNK // LANES
    n_chunks = (t0 + SEL_CHUNK - 1) // SEL_CHUNK
    last = jnp.maximum(n_chunks - 1, 0)

    def scores(g, c, slot):
        k0 = pl.multiple_of(c * SEL_CHUNK, SEL_CHUNK)
        s_ref[slot, g] = jnp.dot(qaug_ref[g], kst_ref[g, :, pl.ds(k0, SEL_CHUNK)], preferred_element_type=F32)

    def consume(g, c, slot):
        k0 = pl.multiple_of(c * SEL_CHUNK, SEL_CHUNK)
        s = s_ref[slot, g]
        m_old = m_ref[g]
        m_new = jnp.maximum(m_old, jnp.max(s, axis=1, keepdims=True))
        p = jnp.concatenate([jnp.exp2(s[:, cb * LANES:(cb + 1) * LANES] - m_new) for cb in range(ncb)], axis=1)
        acc_ref[g] = (jnp.exp2(m_old - m_new) * acc_ref[g]
                      + jnp.dot(p.astype(BF16), vs_ref[g, pl.ds(k0, SEL_CHUNK), :], preferred_element_type=F32))
        m_ref[g] = m_new

    for g in range(ng):
        scores(g, 0, 0)

    def steps(c, n):
        for u in range(n):
            for g in range(ng):
                scores(g, jnp.minimum(c + u + 1, last), (u + 1) % 2)
                consume(g, c + u, u % 2)

    def sel_quad(c4, carry):
        steps(SEL_UNROLL * c4, SEL_UNROLL)
        return carry

    n_quads = n_chunks // SEL_UNROLL
    lax.fori_loop(0, n_quads, sel_quad, 0)
    rem = n_chunks - SEL_UNROLL * n_quads

    @pl.when(rem >= 2)
    def _():
        steps(SEL_UNROLL * n_quads, 2)

    @pl.when(rem % 2 == 1)
    def _():
        for g in range(ng):
            consume(g, n_chunks - 1, 0)

    wlen = WINDOW + Q
    ws = pl.multiple_of(jnp.maximum(t0 - WINDOW, 0), Q)
    diff = (t0 + lax.broadcasted_iota(jnp.int32, (Q, wlen), 0)) - (ws + lax.broadcasted_iota(jnp.int32, (Q, wlen), 1))
    m_w = (diff >= 0) & (diff < WINDOW)
    for g in range(ng):
        gates = gates_ref[g]

        def gate_col(br):
            return jnp.concatenate([gates[:, 3 * h + br:3 * h + br + 1] for h in range(nh)], axis=0)

        acc = acc_ref[g]
        o_s = acc * (gate_col(1) / acc[:, N_HEAD_DIM:N_HEAD_DIM + 1])
        s_w = jnp.dot(q_rope(g), kwt_ref[g, :, pl.ds(ws, wlen)], preferred_element_type=F32)
        s_w = mask_heads(s_w, m_w)
        p_w = jnp.exp2(s_w - jnp.max(s_w, axis=1, keepdims=True))
        acc_w = jnp.dot(p_w.astype(BF16), vw_ref[g, pl.ds(ws, wlen), :], preferred_element_type=F32)
        o_w = acc_w * (gate_col(2) / acc_w[:, N_HEAD_DIM:N_HEAD_DIM + 1])
        outs = []
        for h in range(nh):
            rs = slice(h * Q, (h + 1) * Q)
            outs.append(oc_ref[g, :, h * LANES:(h + 1) * LANES].astype(F32) + o_s[rs, :] + o_w[rs, :])
        for hp in range(nh // 2):
            yn = jnp.where(lane < N_HEAD_DIM, outs[2 * hp], pltpu.roll(outs[2 * hp + 1], N_HEAD_DIM, axis=1))
            ls = slice((g * (nh // 2) + hp) * LANES, (g * (nh // 2) + hp + 1) * LANES)
            yb_ref[:, ls] = (yn * _silu(nz_ref[:, ls].astype(F32))).astype(BF16)


def _nsa(p, b, t, qrope, bias, oc, kst, vs, kwt, vw, gates):
    g = N_KV_GROUPS
    nqb = t // Q_BLOCK
    qw = N_HPG * LANES
    rows = N_HPG * Q_BLOCK

    def qblk(w):
        return pl.BlockSpec((None, g, Q_BLOCK, w), lambda bi, i: (bi, 0, i, 0))

    def whole(r, c):
        return pl.BlockSpec((None, g, r, c), lambda bi, i: (bi, 0, 0, 0), pipeline_mode=pl.Buffered(1))

    return pl.pallas_call(
        _nsa_kernel,
        out_shape=jax.ShapeDtypeStruct((b * t, N_WIDTH), BF16),
        grid=(b, nqb),
        in_specs=[
            qblk(qw), qblk(LANES), qblk(qw),
            whole(2 * LANES, t), whole(t, LANES), whole(LANES, t), whole(t, LANES),
            qblk(LANES),
            pl.BlockSpec((Q_BLOCK, N_WIDTH), lambda bi, i: (bi * nqb + i, C_NZ // N_WIDTH)),
        ],
        out_specs=pl.BlockSpec((Q_BLOCK, N_WIDTH), lambda bi, i: (bi * nqb + i, 0)),
        scratch_shapes=[pltpu.VMEM((g, rows, 2 * LANES), BF16), pltpu.VMEM((g, rows, LANES), F32),
                        pltpu.VMEM((g, rows, LANES), F32), pltpu.VMEM((2, g, rows, SEL_CHUNK), F32)],
        compiler_params=_cparams(("parallel", "arbitrary")),
        name="nsa",
    )(qrope, bias, oc, kst, vs, kwt, vw, gates, p)


def _outproj_kernel(final, ya_ref, yb_ref, ga_ref, gb_ref, x_ref, wa_ref, wb_ref, wo_ref, fg_ref, o_ref):
    a = jnp.dot(ya_ref[...], wa_ref[...], preferred_element_type=F32)
    bq = jnp.dot(yb_ref[...], wb_ref[...], preferred_element_type=F32)
    merged = _sigmoid(ga_ref[...].astype(F32)) * a + _sigmoid(gb_ref[...].astype(F32)) * bq
    y = x_ref[...] + jnp.dot(merged.astype(BF16), wo_ref[...], preferred_element_type=F32)
    if final:
        y = y * lax.rsqrt(jnp.mean(y * y, axis=-1, keepdims=True) + NORM_EPS) * fg_ref[...]
    o_ref[...] = y


def _outproj(ya, yb, p, x2, wa, wb, wo, fg, final):
    m = x2.shape[0]
    tm = OUTPROJ_TM

    def full(a):
        return pl.BlockSpec(a.shape, lambda i: (0, 0))

    return pl.pallas_call(
        functools.partial(_outproj_kernel, final),
        out_shape=jax.ShapeDtypeStruct((m, D_MODEL), F32),
        grid=(m // tm,),
        in_specs=[
            pl.BlockSpec((tm, M_WIDTH), lambda i: (i, 0)),
            pl.BlockSpec((tm, N_WIDTH), lambda i: (i, 0)),
            pl.BlockSpec((tm, D_MODEL), lambda i: (i, C_GA // D_MODEL)),
            pl.BlockSpec((tm, D_MODEL), lambda i: (i, C_GB // D_MODEL)),
            pl.BlockSpec((tm, D_MODEL), lambda i: (i, 0)),
            full(wa), full(wb), full(wo), full(fg),
        ],
        out_specs=pl.BlockSpec((tm, D_MODEL), lambda i: (i, 0)),
        compiler_params=_cparams(("parallel",)),
        name="outproj",
    )(ya, yb, p, p, x2, wa, wb, wo, fg)


def _permute_w_tail(w):
    pad = jnp.zeros((w.shape[0], P_W - _O_END), w.dtype)
    return jnp.concatenate([
        w[:, N_HEAD_TILES * PROJ_TN:_O_MI],
        w[:, _O_GA:_O_END],
        w[:, _O_NQ:_O_KC],
        w[:, _O_NZ:_O_GA],
        w[:, _O_KS:_O_NG],
        w[:, _O_KC:_O_KS],
        w[:, _O_MI:_O_NQ],
        w[:, _O_NG:_O_NZ],
        pad], axis=1)


def _expand_cmp_w1(w1k, w1v):
    half = CMP_BLOCK // 2
    k3 = w1k.astype(BF16).reshape(CMP_BLOCK, 1, N_HEAD_DIM, 1, CMP_HIDDEN)
    v3 = w1v.astype(BF16).reshape(CMP_BLOCK, 1, N_HEAD_DIM, 1, CMP_HIDDEN)
    nc = 2 * N_KV_GROUPS
    eye = jnp.eye(nc, dtype=BF16)
    is_k = (jnp.arange(nc) < N_KV_GROUPS).astype(BF16)
    mk = (eye * is_k[:, None]).reshape(1, nc, 1, nc, 1)
    mv = (eye * (1 - is_k)[:, None]).reshape(1, nc, 1, nc, 1)
    z = k3 * mk + v3 * mv
    z = z.reshape(CMP_BLOCK * nc * N_HEAD_DIM, nc * CMP_HIDDEN)
    return z[:half * nc * N_HEAD_DIM], z[half * nc * N_HEAD_DIM:]


def kernel(x, positions, norm_g, w_in, conv_q_w, conv_q_b, conv_k_w, conv_k_b, b_igate, b_fgate, mh_norm_g,
           cmp_pos_k, cmp_w1_k, cmp_w2_k, cmp_pos_v, cmp_w1_v, cmp_w2_v, b_nsa_gate, w_branch_a, w_branch_b,
           w_out, final_norm_g):
    b, t, _ = x.shape
    assert t % (2 * SEL_CHUNK) == 0 and t >= WINDOW + Q_BLOCK and t // SEL_BLOCK <= LANES and t % MLSTM_CHUNK == 0
    x2 = x.reshape(b * t, D_MODEL)
    for l in range(norm_g.shape[0]):
        w_bf = w_in[l].astype(BF16)
        p, aux = _inproj(x2, norm_g[l][None, :], w_bf, _permute_w_tail(w_bf))

        sm_bias = jnp.zeros((1, LANES), F32).at[0, 0:M_HEADS].set(b_igate[l]).at[0, M_HEADS:2 * M_HEADS].set(b_fgate[l])
        ya = _mlstm(p, aux, b, t, conv_q_w[l], conv_q_b[l][None, :], conv_k_w[l], conv_k_b[l][None, :], sm_bias,
                    mh_norm_g[l][None, :])

        nr = t // CMP_STRIDE
        wt, wb = _expand_cmp_w1(cmp_w1_k[l], cmp_w1_v[l])
        w2pad = lambda w: jnp.pad(w, ((0, 0), (0, LANES - N_HEAD_DIM))).astype(BF16)
        kc, vct = _compress(aux, b, t, wt, wb, cmp_pos_k[l].reshape(1, -1), cmp_pos_v[l].reshape(1, -1),
                            cmp_w1_k[l].astype(BF16), cmp_w1_v[l].astype(BF16), w2pad(cmp_w2_k[l]), w2pad(cmp_w2_v[l]))

        half = ROPE_DIM // 2
        inv = jnp.power(jnp.float32(ROPE_THETA), -jnp.arange(half, dtype=F32) * (2.0 / ROPE_DIM))
        inv_row = inv[:, None]
        pos_col = positions.astype(F32)[:, None, :]
        gate_bias = jnp.zeros((1, LANES), F32).at[0, SM_G_OFF:SM_G_OFF + 3 * N_HEADS].set(b_nsa_gate[l])
        qrawt, qrope, kst, vs, kwt, vw, gates, gates_t = _kvprep(p, aux, b, t, pos_col, inv_row, gate_bias)

        ci = np.arange(nr)[None, :] * CMP_STRIDE
        sj = np.arange(LANES)[:, None] * SEL_BLOCK
        overlap_t = jnp.asarray((ci < sj + SEL_BLOCK) & (ci + CMP_BLOCK > sj) & (np.arange(nr)[None, :] < nr - 1)
                                & (np.arange(LANES)[:, None] < t // SEL_BLOCK), dtype=BF16)
        oc, bias = _select(qrawt, kc, vct, overlap_t, gates_t)
        yb = _nsa(p, b, t, qrope, bias, oc, kst, vs, kwt, vw, gates)

        x2 = _outproj(ya, yb, p, x2, w_branch_a[l].astype(BF16), w_branch_b[l].astype(BF16),
                      w_out[l].astype(BF16), final_norm_g[None, :], l == norm_g.shape[0] - 1)
    return x2.reshape(b, t, D_MODEL)
```

```python
import functools
import math

import jax
import jax.numpy as jnp
import numpy as np
from jax import lax
from jax.experimental import pallas as pl
from jax.experimental.pallas import tpu as pltpu

F32 = jnp.float32
BF16 = jnp.bfloat16

D_MODEL = 1024
M_HEADS = 4
M_HEAD_DIM = 256
M_WIDTH = M_HEADS * M_HEAD_DIM
CONV_WIDTH = 4
N_HEADS = 8
N_KV_GROUPS = 2
N_HPG = N_HEADS // N_KV_GROUPS
N_HEAD_DIM = 64
N_WIDTH = N_HEADS * N_HEAD_DIM
CMP_BLOCK = 32
CMP_STRIDE = 16
CMP_HIDDEN = 128
SEL_BLOCK = 64
SEL_TOPK = 16
WINDOW = 512
Q_BLOCK = 256
ROPE_THETA = 500000.0
ROPE_DIM = N_HEAD_DIM // 4
NORM_EPS = 1e-6
NEG = -1e30
FORCE = 1e9

LANES = 128
SUBLANES = 8
MLSTM_CHUNK = 256
SEL_CHUNK = 512
SEL_UNROLL = 4
V7X_VMEM_BYTES = 64 * 1024 * 1024
VMEM_LIMIT = V7X_VMEM_BYTES - 8 * 1024 * 1024
INPROJ_VMEM_LIMIT = V7X_VMEM_BYTES - 4 * 1024 * 1024
INPROJ_TM, OUTPROJ_TM, KVPREP_TK, SELECT_QS = 2048, 1024, 1024, 2048
SM_F_OFF = M_HEADS
SM_G_OFF = 2 * M_HEADS
GATE_ROWS = -(-3 * N_HPG // SUBLANES) * SUBLANES

C_MQ, C_MK, C_MV, C_MO, C_MZ, C_GA, C_GB = 0, 1024, 2048, 3072, 4096, 5120, 6144
C_NQ, C_NZ = 7168, 7680
C_KV = 8192
C_CMP = 8704
C_SM = 8960
P_W = 9216
AUX_W = 3 * LANES
X_KC, X_VC, X_SM = 0, 1, 2
_O_MI, _O_NQ, _O_KC, _O_KS, _O_NG, _O_NZ, _O_GA, _O_END = 5120, 5128, 5640, 5896, 6408, 6432, 6944, 8992


def _cparams(sem, vmem_limit=VMEM_LIMIT):
    return pltpu.CompilerParams(dimension_semantics=sem, vmem_limit_bytes=vmem_limit)


def _sigmoid(x):
    return 0.5 * jnp.tanh(0.5 * x) + 0.5


def _silu(x):
    h = 0.5 * x
    return h + h * jnp.tanh(h)


def _log_sigmoid(x):
    return jnp.minimum(x, 0.0) - jnp.log1p(jnp.exp(-jnp.abs(x)))


PROJ_TN = 1536
AUX_TILE = C_CMP // PROJ_TN
AUX_OFF = C_CMP % PROJ_TN


N_HEAD_TILES = _O_MI // PROJ_TN


WPREP_ROWS, WPREP_SUB, WPREP_PIECE = 128, 64, 512
_W_SLABS = ((0, 0, _O_MI), (C_GA, _O_GA, _O_END - _O_GA), (C_NQ, _O_NQ, _O_KC - _O_NQ), (C_NZ, _O_NZ, _O_GA - _O_NZ),
            (C_KV, _O_KS, _O_NG - _O_KS), (C_CMP, _O_KC, _O_KS - _O_KC))


def _wprep_kernel(w_ref, o_ref):
    for r0 in range(0, WPREP_ROWS, WPREP_SUB):
        rows = pl.ds(r0, WPREP_SUB)
        for dst, src, width in _W_SLABS:
            for c in range(0, width, WPREP_PIECE):
                n = min(WPREP_PIECE, width - c)
                a0 = (src + c) // LANES * LANES
                off = src + c - a0
                if off == 0:
                    piece = w_ref[rows, a0:a0 + n]
                else:
                    piece = w_ref[rows, a0:min(a0 + n + LANES, _O_END)][:, off:off + n]
                o_ref[rows, dst + c:dst + c + n] = piece.astype(BF16)
        lane = lax.broadcasted_iota(jnp.int32, (WPREP_SUB, LANES), 1)
        ng0 = _O_NG - SM_G_OFF
        sm = jnp.where(lane < SM_G_OFF, w_ref[rows, _O_MI:_O_MI + LANES],
                       jnp.where(lane < SM_G_OFF + _O_NZ - _O_NG, w_ref[rows, ng0:ng0 + LANES], 0.0))
        o_ref[rows, C_SM:C_SM + LANES] = sm.astype(BF16)
        o_ref[rows, C_SM + LANES:P_W] = jnp.zeros((WPREP_SUB, P_W - C_SM - LANES), BF16)


def _wprep(w):
    assert _O_MI % LANES == 0 and (_O_NG - SM_G_OFF) % LANES == 0 and _O_NQ - _O_MI == SM_G_OFF
    return pl.pallas_call(
        _wprep_kernel,
        out_shape=jax.ShapeDtypeStruct((w.shape[0], P_W), BF16),
        grid=(w.shape[0] // WPREP_ROWS,),
        in_specs=[pl.BlockSpec((WPREP_ROWS, w.shape[1]), lambda i: (i, 0))],
        out_specs=pl.BlockSpec((WPREP_ROWS, P_W), lambda i: (i, 0)),
        compiler_params=_cparams(("parallel",)),
        name="wprep",
    )(w)


def _inproj_kernel(x_ref, g_ref, wh_ref, wt_ref, o_ref, aux_ref, hn_ref):
    j = pl.program_id(1)

    @pl.when(j == 0)
    def _():
        x = x_ref[...]
        r = lax.rsqrt(jnp.mean(x * x, axis=-1, keepdims=True) + NORM_EPS)
        hn_ref[...] = (x * r * g_ref[...]).astype(BF16)

    @pl.when(j < N_HEAD_TILES)
    def _():
        o_ref[...] = jnp.dot(hn_ref[...], wh_ref[...], preferred_element_type=F32).astype(BF16)

    @pl.when((j >= N_HEAD_TILES) & (j != AUX_TILE))
    def _():
        o_ref[...] = jnp.dot(hn_ref[...], wt_ref[...], preferred_element_type=F32).astype(BF16)

    @pl.when(j == AUX_TILE)
    def _():
        acc = jnp.dot(hn_ref[...], wt_ref[...], preferred_element_type=F32)
        o_ref[...] = acc.astype(BF16)
        aux_ref[...] = acc[:, AUX_OFF:AUX_OFF + AUX_W]


def _inproj(x2, norm_g, w_head, w_tail):
    m = x2.shape[0]
    tm, tn = INPROJ_TM, PROJ_TN
    return pl.pallas_call(
        _inproj_kernel,
        out_shape=(jax.ShapeDtypeStruct((m, P_W), BF16), jax.ShapeDtypeStruct((m, AUX_W), F32)),
        grid=(m // tm, P_W // tn),
        in_specs=[
            pl.BlockSpec((tm, D_MODEL), lambda i, j: (i, 0)),
            pl.BlockSpec((1, D_MODEL), lambda i, j: (0, 0)),
            pl.BlockSpec((D_MODEL, tn), lambda i, j: (0, jnp.minimum(j, N_HEAD_TILES - 1))),
            pl.BlockSpec((D_MODEL, tn), lambda i, j: (0, jnp.maximum(j, N_HEAD_TILES))),
        ],
        out_specs=(pl.BlockSpec((tm, tn), lambda i, j: (i, j)), pl.BlockSpec((tm, AUX_W), lambda i, j: (i, 0))),
        scratch_shapes=[pltpu.VMEM((tm, D_MODEL), BF16)],
        compiler_params=_cparams(("parallel", "arbitrary"), INPROJ_VMEM_LIMIT),
        name="inproj",
    )(x2, norm_g, w_head, w_tail)


def _causal_conv_silu(u16, prev_ref, w_ref, b_ref):
    n, w = u16.shape
    u = u16.astype(F32)
    row = lax.broadcasted_iota(jnp.int32, (n, n), 0)
    col = lax.broadcasted_iota(jnp.int32, (n, n), 1)
    sub = lax.broadcasted_iota(jnp.int32, (SUBLANES, w), 0)
    prev = prev_ref[...]
    y = b_ref[...] + u * w_ref[CONV_WIDTH - 1:CONV_WIDTH, :]
    for d in range(1, CONV_WIDTH):
        shift = jnp.where(row - col == d, 1.0, 0.0).astype(BF16)
        sh = jnp.dot(shift, u16, preferred_element_type=F32)
        head = jnp.where(sub < d, pltpu.roll(prev, d, axis=0), sh[0:SUBLANES, :])
        sh = jnp.concatenate([head, sh[SUBLANES:, :]], axis=0)
        y = y + sh * w_ref[CONV_WIDTH - 1 - d:CONV_WIDTH - d, :]
    prev_ref[...] = u[n - SUBLANES:n, :]
    return _silu(y)


def _mlstm_kernel(q_ref, k_ref, v_ref, og_ref, z_ref, sm_ref, cqw_ref, cqb_ref, ckw_ref, ckb_ref,
                  sb_ref, ng_ref, ya_ref, qprev, kprev, c_state, n_state, m_state):
    L = q_ref.shape[0]
    d = M_HEAD_DIM

    @pl.when(pl.program_id(1) == 0)
    def _():
        qprev[...] = jnp.zeros_like(qprev)
        kprev[...] = jnp.zeros_like(kprev)
        c_state[...] = jnp.zeros_like(c_state)
        n_state[...] = jnp.zeros_like(n_state)
        m_state[...] = jnp.zeros_like(m_state)

    mq = _causal_conv_silu(q_ref[...], qprev, cqw_ref, cqb_ref)
    mk = _causal_conv_silu(k_ref[...], kprev, ckw_ref, ckb_ref) * (d ** -0.5)

    sm = sm_ref[...] + sb_ref[...]
    lf = _log_sigmoid(sm)
    row = lax.broadcasted_iota(jnp.int32, (L, L), 0)
    col = lax.broadcasted_iota(jnp.int32, (L, L), 1)
    causal = row >= col
    tril = causal.astype(F32)
    triu = (row <= col).astype(F32)
    a_col = jnp.dot(tril, lf, precision=lax.Precision.HIGHEST, preferred_element_type=F32)
    sm_t = sm.T[0:SUBLANES, :]
    a_row = jnp.dot(_log_sigmoid(sm_t), triu, precision=lax.Precision.HIGHEST,
                    preferred_element_type=F32)

    for h in range(M_HEADS):
        hs = slice(h * d, (h + 1) * d)
        a_c = a_col[:, SM_F_OFF + h:SM_F_OFF + h + 1]
        li_c = sm[:, h:h + 1]
        a_r = a_row[SM_F_OFF + h:SM_F_OFF + h + 1, :]
        li_r = sm_t[h:h + 1, :]
        m_prev = m_state[h:h + 1, 0:1]
        a_end = a_c[L - 1:L, :]

        log_d = jnp.where(causal, a_c - (a_r - li_r), -jnp.inf)
        inter = a_c + m_prev
        m_t = jnp.maximum(inter, jnp.max(log_d, axis=1, keepdims=True))
        w_inter = jnp.exp(inter - m_t)
        dm = jnp.exp(log_d - m_t)

        qh = mq[:, hs]
        kh = mk[:, hs]
        qb = qh.astype(BF16)
        kb = kh.astype(BF16)
        vb = v_ref[:, hs]
        s = lax.dot_general(qb, kb, (((1,), (1,)), ((), ())), preferred_element_type=F32) * dm
        c_old = c_state[h]
        n_old = n_state[h:h + 1, :]
        num = (w_inter * jnp.dot(qb, c_old.astype(BF16), preferred_element_type=F32)
               + jnp.dot(s.astype(BF16), vb, preferred_element_type=F32))
        den = (w_inter * jnp.sum(qh * n_old, axis=1, keepdims=True)
               + jnp.sum(s, axis=1, keepdims=True))
        hh = num * (1.0 / jnp.maximum(jnp.abs(den), jnp.exp(-m_t)))

        logw = a_end - a_c + li_c
        m_new = jnp.maximum(a_end + m_prev, jnp.max(logw, axis=0, keepdims=True))
        wk = jnp.exp(logw - m_new)
        decay = jnp.exp(a_end + m_prev - m_new)
        kw = kh * wk
        c_state[h] = decay * c_old + lax.dot_general(kw.astype(BF16), vb, (((0,), (0,)), ((), ())),
                                                     preferred_element_type=F32)
        n_state[h:h + 1, :] = decay * n_old + jnp.sum(kw, axis=0, keepdims=True)
        m_state[h:h + 1, :] = jnp.broadcast_to(m_new, (1, LANES))

        mu = jnp.mean(hh, axis=1, keepdims=True)
        dc = hh - mu
        var = jnp.mean(dc * dc, axis=1, keepdims=True)
        hn = dc * lax.rsqrt(var + NORM_EPS) * ng_ref[:, hs]
        ya = _sigmoid(og_ref[:, hs].astype(F32)) * hn * _silu(z_ref[:, hs].astype(F32))
        ya_ref[:, hs] = ya.astype(BF16)


def _mlstm(p, aux, b, t, conv_q_w, conv_q_b, conv_k_w, conv_k_b, sm_bias, mh_norm_g):
    L = MLSTM_CHUNK
    nc = t // L

    def colspec(cstart):
        return pl.BlockSpec((L, M_WIDTH), lambda bi, c: (bi * nc + c, cstart // M_WIDTH))

    def full(shape):
        return pl.BlockSpec(shape, lambda bi, c: (0,) * len(shape))

    return pl.pallas_call(
        _mlstm_kernel,
        out_shape=jax.ShapeDtypeStruct((b * t, M_WIDTH), BF16),
        grid=(b, nc),
        in_specs=[
            colspec(C_MQ), colspec(C_MK), colspec(C_MV), colspec(C_MO), colspec(C_MZ),
            pl.BlockSpec((L, LANES), lambda bi, c: (bi * nc + c, X_SM)),
            full((CONV_WIDTH, M_WIDTH)), full((1, M_WIDTH)), full((CONV_WIDTH, M_WIDTH)), full((1, M_WIDTH)),
            full((1, LANES)), full((1, M_WIDTH)),
        ],
        out_specs=pl.BlockSpec((L, M_WIDTH), lambda bi, c: (bi * nc + c, 0)),
        scratch_shapes=[
            pltpu.VMEM((SUBLANES, M_WIDTH), F32), pltpu.VMEM((SUBLANES, M_WIDTH), F32),
            pltpu.VMEM((M_HEADS, M_HEAD_DIM, M_HEAD_DIM), F32),
            pltpu.VMEM((SUBLANES, M_HEAD_DIM), F32), pltpu.VMEM((SUBLANES, LANES), F32),
        ],
        compiler_params=_cparams(("parallel", "arbitrary")),
        name="mlstm",
    )(p, p, p, p, p, aux, conv_q_w, conv_q_b, conv_k_w, conv_k_b, sm_bias, mh_norm_g)


def _compress_kernel(pk_ref, pv_ref, wt_ref, wb_ref, posk_ref, posv_ref, w1k_ref, w1v_ref, w2k_ref, w2v_ref,
                     kc_ref, vct_ref):
    nr = pk_ref.shape[0] // CMP_STRIDE
    cw = 2 * LANES
    u = jnp.zeros((nr, wt_ref.shape[1]), F32)
    v = jnp.zeros((nr, wb_ref.shape[1]), F32)
    for l in range(CMP_STRIDE):
        x = jnp.concatenate([pk_ref[pl.ds(l, nr, stride=CMP_STRIDE), :],
                             pv_ref[pl.ds(l, nr, stride=CMP_STRIDE), :]], axis=1).astype(BF16)
        u = u + jnp.dot(x, wt_ref[l * cw:(l + 1) * cw, :], preferred_element_type=F32)
        v = v + jnp.dot(x, wb_ref[l * cw:(l + 1) * cw, :], preferred_element_type=F32)
    v = pltpu.roll(v, nr - 1, axis=0)
    rowi = lax.broadcasted_iota(jnp.int32, v.shape, 0)
    v = jnp.where(rowi < nr - 1, v, 0.0)

    def pos_term(pos_ref, w_ref):
        pb = jnp.broadcast_to(pos_ref[...].astype(BF16), (SUBLANES, CMP_BLOCK * N_HEAD_DIM))
        return jnp.dot(pb, w_ref[...], preferred_element_type=F32)[0:1, :]

    pk = pos_term(posk_ref, w1k_ref)
    pv = pos_term(posv_ref, w1v_ref)
    hid = u + v + jnp.concatenate([pk, pk, pv, pv], axis=1)
    act = 0.5 * hid * (1.0 + jnp.tanh(math.sqrt(2.0 / math.pi) * (hid + 0.044715 * (hid * hid * hid))))
    for c in range(2 * N_KV_GROUPS):
        w2 = w2k_ref[...] if c < N_KV_GROUPS else w2v_ref[...]
        out = jnp.dot(act[:, c * CMP_HIDDEN:(c + 1) * CMP_HIDDEN].astype(BF16), w2,
                      preferred_element_type=F32)
        if c < N_KV_GROUPS:
            kc_ref[c] = out.astype(BF16)
        else:
            out_t = out.T
            ones_row = lax.broadcasted_iota(jnp.int32, out_t.shape, 0) == N_HEAD_DIM
            vct_ref[c - N_KV_GROUPS] = jnp.where(ones_row, 1.0, out_t).astype(BF16)


def _compress(aux, b, t, wt, wb, posk, posv, w1k, w1v, w2k, w2v):
    nr = t // CMP_STRIDE
    g = N_KV_GROUPS

    def full(a):
        return pl.BlockSpec(a.shape, lambda bi: (0,) * a.ndim)

    return pl.pallas_call(
        _compress_kernel,
        out_shape=(jax.ShapeDtypeStruct((b, g, nr, LANES), BF16), jax.ShapeDtypeStruct((b, g, LANES, nr), BF16)),
        grid=(b,),
        in_specs=[pl.BlockSpec((t, LANES), lambda bi: (bi, X_KC)),
                  pl.BlockSpec((t, LANES), lambda bi: (bi, X_VC)),
                  full(wt), full(wb), full(posk), full(posv), full(w1k), full(w1v), full(w2k), full(w2v)],
        out_specs=(pl.BlockSpec((None, g, nr, LANES), lambda bi: (bi, 0, 0, 0)),
                   pl.BlockSpec((None, g, LANES, nr), lambda bi: (bi, 0, 0, 0))),
        compiler_params=_cparams(("parallel",)),
        name="compress",
    )(aux, aux, wt, wb, posk, posv, w1k, w1v, w2k, w2v)


def _kvprep_kernel(q_ref, kv_ref, sm_ref, pos_ref, inv_ref, gb_ref,
                   qrawt_ref, qrope_ref, kst_ref, vs_ref, kwt_ref, vw_ref, gates_ref, gatest_ref):
    tk = q_ref.shape[0]
    lane = lax.broadcasted_iota(jnp.int32, (tk, LANES), 1)
    half = ROPE_DIM // 2
    ang = inv_ref[...] * pos_ref[...]
    cos8 = jnp.cos(ang)
    sin8 = jnp.sin(ang)
    rest = N_HEAD_DIM - ROPE_DIM
    one_r, zero_r, zero_h = jnp.ones((rest, tk), F32), jnp.zeros((rest, tk), F32), jnp.zeros((half, tk), F32)
    heads = LANES // N_HEAD_DIM
    cos = jnp.concatenate([cos8, cos8, one_r] * heads, axis=0).T
    s_up = jnp.concatenate([-sin8, zero_h, zero_r] * heads, axis=0).T
    s_dn = jnp.concatenate([zero_h, sin8, zero_r] * heads, axis=0).T

    def rope(u):
        return u * cos + pltpu.roll(u, LANES - half, axis=1) * s_up + pltpu.roll(u, half, axis=1) * s_dn

    def head_pad(u, odd):
        if odd:
            u = pltpu.roll(u, N_HEAD_DIM, axis=1)
        return jnp.where(lane < N_HEAD_DIM, u, 0.0)

    scale = N_HEAD_DIM ** -0.5 * math.log2(math.e)
    zeros_t = jnp.zeros((N_HEAD_DIM, tk), BF16)
    for g in range(N_KV_GROUPS):
        for hp in range(N_HPG // 2):
            c0 = g * (N_HPG * N_HEAD_DIM) + hp * LANES
            u = q_ref[:, c0:c0 + LANES].astype(F32) * scale
            ur = rope(u)
            u_t = u.T.astype(BF16)
            for odd in range(2):
                h = 2 * hp + odd
                qrawt_ref[g, h * LANES:h * LANES + N_HEAD_DIM, :] = u_t[odd * N_HEAD_DIM:(odd + 1) * N_HEAD_DIM, :]
                qrawt_ref[g, h * LANES + N_HEAD_DIM:(h + 1) * LANES, :] = zeros_t
                qrope_ref[g, :, h * LANES:(h + 1) * LANES] = head_pad(ur, odd).astype(BF16)

    ks_t = rope(kv_ref[:, 0:LANES].astype(F32)).T
    kw_t = rope(kv_ref[:, 2 * LANES:3 * LANES].astype(F32)).T
    vs = kv_ref[:, LANES:2 * LANES].astype(F32)
    vw = kv_ref[:, 3 * LANES:4 * LANES].astype(F32)
    t_glob = pl.program_id(1) * tk + lax.broadcasted_iota(jnp.int32, (LANES, tk), 1)
    blk_row = lax.broadcasted_iota(jnp.int32, (LANES, tk), 0)
    onehot = jnp.where(blk_row == t_glob // SEL_BLOCK, 1.0, 0.0).astype(BF16)
    ones_lane = jnp.where(lane == N_HEAD_DIM, 1.0, 0.0)
    gates = _sigmoid(sm_ref[...] + gb_ref[...])
    for g in range(N_KV_GROUPS):
        rs = slice(g * N_HEAD_DIM, (g + 1) * N_HEAD_DIM)
        kst_ref[g, 0:LANES, :] = onehot
        kst_ref[g, LANES:LANES + N_HEAD_DIM, :] = ks_t[rs, :].astype(BF16)
        kst_ref[g, LANES + N_HEAD_DIM:2 * LANES, :] = zeros_t
        kwt_ref[g, 0:N_HEAD_DIM, :] = kw_t[rs, :].astype(BF16)
        kwt_ref[g, N_HEAD_DIM:LANES, :] = zeros_t
        vs_ref[g] = (head_pad(vs, g) + ones_lane).astype(BF16)
        vw_ref[g] = (head_pad(vw, g) + ones_lane).astype(BF16)
        goff = SM_G_OFF + g * (3 * N_HPG)
        gg = pltpu.roll(gates, LANES - goff, axis=1)
        gates_ref[g] = gg
        gatest_ref[g] = gg.T[0:GATE_ROWS, :]


def _kvprep(p, aux, b, t, pos_col, inv_row, gate_bias):
    tk = KVPREP_TK
    nt = t // tk
    g = N_KV_GROUPS
    qw = N_HPG * LANES

    def tok(shape):
        return pl.BlockSpec((None, g) + shape, lambda bi, i: (bi, 0, i, 0))

    def tok_t(rows):
        return pl.BlockSpec((None, g, rows, tk), lambda bi, i: (bi, 0, 0, i))

    return pl.pallas_call(
        _kvprep_kernel,
        out_shape=(
            jax.ShapeDtypeStruct((b, g, qw, t), BF16), jax.ShapeDtypeStruct((b, g, t, qw), BF16),
            jax.ShapeDtypeStruct((b, g, 2 * LANES, t), BF16), jax.ShapeDtypeStruct((b, g, t, LANES), BF16),
            jax.ShapeDtypeStruct((b, g, LANES, t), BF16), jax.ShapeDtypeStruct((b, g, t, LANES), BF16),
            jax.ShapeDtypeStruct((b, g, t, LANES), F32), jax.ShapeDtypeStruct((b, g, GATE_ROWS, t), F32),
        ),
        grid=(b, nt),
        in_specs=[
            pl.BlockSpec((tk, N_WIDTH), lambda bi, i: (bi * nt + i, C_NQ // N_WIDTH)),
            pl.BlockSpec((tk, 4 * LANES), lambda bi, i: (bi * nt + i, C_KV // (4 * LANES))),
            pl.BlockSpec((tk, LANES), lambda bi, i: (bi * nt + i, X_SM)),
            pl.BlockSpec((None, 1, tk), lambda bi, i: (bi, 0, i)),
            pl.BlockSpec(inv_row.shape, lambda bi, i: (0, 0)),
            pl.BlockSpec((1, LANES), lambda bi, i: (0, 0)),
        ],
        out_specs=(tok_t(qw), tok((tk, qw)), tok_t(2 * LANES), tok((tk, LANES)),
                   tok_t(LANES), tok((tk, LANES)), tok((tk, LANES)), tok_t(GATE_ROWS)),
        compiler_params=_cparams(("parallel", "arbitrary")),
        name="kvprep",
    )(p, p, aux, pos_col, inv_row, gate_bias)


CMP_ROW_CHUNK = 128


def _select_kernel(qt_ref, kc_ref, vct_ref, ovt_ref, gt_ref, oc_ref, bias_ref, imp_ref):
    qs = qt_ref.shape[1]
    nr = kc_ref.shape[0]
    t0 = pl.program_id(2) * qs
    sees_any = t0 + lax.broadcasted_iota(jnp.int32, (1, qs), 1) >= CMP_BLOCK - 1

    def cmp_branch(rows):
        m_c = (lax.broadcasted_iota(jnp.int32, (rows, qs), 0) * CMP_STRIDE + (CMP_BLOCK - 1)
               <= t0 + lax.broadcasted_iota(jnp.int32, (rows, qs), 1))
        kc = kc_ref[0:rows, :]
        vct = vct_ref[:, 0:rows]
        ovt = ovt_ref[:, 0:rows]
        imp = jnp.zeros((LANES, qs), F32)
        for h in range(N_HPG):
            s = jnp.dot(kc, qt_ref[h * LANES:(h + 1) * LANES, :], preferred_element_type=F32)
            s = jnp.where(m_c, s, NEG)
            e = jnp.exp2(s - jnp.max(s, axis=0, keepdims=True)).astype(BF16)
            r = jnp.dot(vct, e, preferred_element_type=F32)
            inv = jnp.where(sees_any, 1.0 / r[N_HEAD_DIM:N_HEAD_DIM + 1, :], 0.0)
            imp = imp + jnp.dot(ovt, e, preferred_element_type=F32) * inv
            oc_ref[:, h * LANES:(h + 1) * LANES] = (r * (inv * gt_ref[3 * h:3 * h + 1, :])).T.astype(BF16)
        imp_ref[...] = imp

    n_var = nr // CMP_ROW_CHUNK
    visible = (t0 + qs - CMP_BLOCK) // CMP_STRIDE + 1
    need = jnp.clip((visible + CMP_ROW_CHUNK - 1) // CMP_ROW_CHUNK, 1, n_var)
    for v in range(1, n_var + 1):
        pl.when(need == v)(functools.partial(cmp_branch, v * CMP_ROW_CHUNK))
    imp = imp_ref[...]

    blk = lax.broadcasted_iota(jnp.int32, (LANES, qs), 0)
    tq = t0 + lax.broadcasted_iota(jnp.int32, (LANES, qs), 1)
    cur = tq // SEL_BLOCK
    forced = (blk == 0) | (blk == cur) | (blk == cur - 1)
    n_sel = (nr * CMP_STRIDE) // SEL_BLOCK
    free_score = jnp.where(blk * SEL_BLOCK <= tq, imp, -1.0)
    blk_f = blk.astype(F32)
    n_top = min(SEL_TOPK, n_sel)
    n_forced = 3

    def pick(score, n_iter):
        score = jnp.where(blk < n_sel, score, -jnp.inf)
        for _ in range(n_iter):
            mx = jnp.max(score, axis=0, keepdims=True)
            idx = jnp.min(jnp.where(score == mx, blk_f, float(LANES)), axis=0, keepdims=True)
            score = jnp.where(blk_f == idx, -jnp.inf, score)
        bias_t = jnp.where((score == -jnp.inf) & (blk < n_sel), 0.0, NEG)
        bias_ref[...] = bias_t.T.astype(BF16)

    all_three_forced = (t0 >= 2 * SEL_BLOCK) & (n_top >= n_forced)

    @pl.when(all_three_forced)
    def _():
        pick(jnp.where(forced, -jnp.inf, free_score), n_top - n_forced)

    @pl.when(jnp.logical_not(all_three_forced))
    def _():
        pick(jnp.where(forced, FORCE, free_score), n_top)


def _select(qrawt, kc, vct, ovt, gates_t):
    b, g, qw, t = qrawt.shape
    nr = kc.shape[2]
    qs = SELECT_QS

    def whole(rows, cols):
        return pl.BlockSpec((None, None, rows, cols), lambda bi, gi, i: (bi, gi, 0, 0))

    return pl.pallas_call(
        _select_kernel,
        out_shape=(jax.ShapeDtypeStruct((b, g, t, qw), BF16), jax.ShapeDtypeStruct((b, g, t, LANES), BF16)),
        grid=(b, g, t // qs),
        in_specs=[
            pl.BlockSpec((None, None, qw, qs), lambda bi, gi, i: (bi, gi, 0, i)),
            whole(nr, LANES), whole(LANES, nr),
            pl.BlockSpec(ovt.shape, lambda bi, gi, i: (0, 0)),
            pl.BlockSpec((None, None, GATE_ROWS, qs), lambda bi, gi, i: (bi, gi, 0, i)),
        ],
        out_specs=(pl.BlockSpec((None, None, qs, qw), lambda bi, gi, i: (bi, gi, i, 0)),
                   pl.BlockSpec((None, None, qs, LANES), lambda bi, gi, i: (bi, gi, i, 0))),
        scratch_shapes=[pltpu.VMEM((LANES, qs), F32)],
        compiler_params=_cparams(("parallel", "parallel", "parallel")),
        name="select",
    )(qrawt, kc, vct, ovt, gates_t)


def _nsa_kernel(qrope_ref, bias_ref, oc_ref, kst_ref, vs_ref, kwt_ref, vw_ref, gates_ref, nz_ref,
                yb_ref, qaug_ref, acc_ref, m_ref, s_ref):
    i = pl.program_id(1)
    t0 = i * Q_BLOCK
    Q = Q_BLOCK
    nh = N_HPG
    ng = N_KV_GROUPS
    rows = nh * Q
    t0a = pl.multiple_of(t0, Q)
    causal = (lax.broadcasted_iota(jnp.int32, (Q, Q), 1) <= lax.broadcasted_iota(jnp.int32, (Q, Q), 0))
    lane = lax.broadcasted_iota(jnp.int32, (Q, LANES), 1)

    def q_rope(g):
        return jnp.concatenate([qrope_ref[g, :, h * LANES:(h + 1) * LANES] for h in range(nh)], axis=0)

    def mask_heads(s, m):
        return jnp.concatenate([jnp.where(m, s[h * Q:(h + 1) * Q, :], NEG) for h in range(nh)], axis=0)

    first_blk = t0 // SEL_BLOCK
    for g in range(ng):
        qr = q_rope(g)
        bias = bias_ref[g]
        bias_early = jnp.where(lane < first_blk, bias, NEG).astype(BF16)
        qaug_ref[g, :, 0:LANES] = jnp.concatenate([bias_early] * nh, axis=0)
        qaug_ref[g, :, LANES:2 * LANES] = qr
        q_diag = jnp.concatenate([jnp.concatenate([bias] * nh, axis=0), qr], axis=1)
        s_d = jnp.dot(q_diag, kst_ref[g, :, pl.ds(t0a, Q)], preferred_element_type=F32)
        s_d = mask_heads(s_d, causal)
        m0 = jnp.max(s_d, axis=1, keepdims=True)
        p_d = jnp.exp2(s_d - m0)
        m_ref[g] = jnp.broadcast_to(m0, (rows, LANES))
        acc_ref[g] = jnp.dot(p_d.astype(BF16), vs_ref[g, pl.ds(t0a, Q), :], preferred_element_type=F32)

    ncb = SEL_CHUNK // LANES
    n_chunks = (t0 + SEL_CHUNK - 1) // SEL_CHUNK
    last = jnp.maximum(n_chunks - 1, 0)

    def scores(g, c, slot):
        k0 = pl.multiple_of(c * SEL_CHUNK, SEL_CHUNK)
        s_ref[slot, g] = jnp.dot(qaug_ref[g], kst_ref[g, :, pl.ds(k0, SEL_CHUNK)], preferred_element_type=F32)

    def consume(g, c, slot):
        k0 = pl.multiple_of(c * SEL_CHUNK, SEL_CHUNK)
        s = s_ref[slot, g]
        m_old = m_ref[g]
        m_new = jnp.maximum(m_old, jnp.max(s, axis=1, keepdims=True))
        p = jnp.concatenate([jnp.exp2(s[:, cb * LANES:(cb + 1) * LANES] - m_new) for cb in range(ncb)], axis=1)
        acc_ref[g] = (jnp.exp2(m_old - m_new) * acc_ref[g]
                      + jnp.dot(p.astype(BF16), vs_ref[g, pl.ds(k0, SEL_CHUNK), :], preferred_element_type=F32))
        m_ref[g] = m_new

    for g in range(ng):
        scores(g, 0, 0)

    def steps(c, n):
        for u in range(n):
            for g in range(ng):
                scores(g, jnp.minimum(c + u + 1, last), (u + 1) % 2)
                consume(g, c + u, u % 2)

    def sel_quad(c4, carry):
        steps(SEL_UNROLL * c4, SEL_UNROLL)
        return carry

    n_quads = n_chunks // SEL_UNROLL
    lax.fori_loop(0, n_quads, sel_quad, 0)
    rem = n_chunks - SEL_UNROLL * n_quads

    @pl.when(rem >= 2)
    def _():
        steps(SEL_UNROLL * n_quads, 2)

    @pl.when(rem % 2 == 1)
    def _():
        for g in range(ng):
            consume(g, n_chunks - 1, 0)

    wlen = WINDOW + Q
    ws = pl.multiple_of(jnp.maximum(t0 - WINDOW, 0), Q)
    diff = (t0 + lax.broadcasted_iota(jnp.int32, (Q, wlen), 0)) - (ws + lax.broadcasted_iota(jnp.int32, (Q, wlen), 1))
    m_w = (diff >= 0) & (diff < WINDOW)
    for g in range(ng):
        gates = gates_ref[g]

        def gate_col(br):
            return jnp.concatenate([gates[:, 3 * h + br:3 * h + br + 1] for h in range(nh)], axis=0)

        acc = acc_ref[g]
        o_s = acc * (gate_col(1) / acc[:, N_HEAD_DIM:N_HEAD_DIM + 1])
        s_w = jnp.dot(q_rope(g), kwt_ref[g, :, pl.ds(ws, wlen)], preferred_element_type=F32)
        s_w = mask_heads(s_w, m_w)
        p_w = jnp.exp2(s_w - jnp.max(s_w, axis=1, keepdims=True))
        acc_w = jnp.dot(p_w.astype(BF16), vw_ref[g, pl.ds(ws, wlen), :], preferred_element_type=F32)
        o_w = acc_w * (gate_col(2) / acc_w[:, N_HEAD_DIM:N_HEAD_DIM + 1])
        outs = []
        for h in range(nh):
            rs = slice(h * Q, (h + 1) * Q)
            outs.append(oc_ref[g, :, h * LANES:(h + 1) * LANES].astype(F32) + o_s[rs, :] + o_w[rs, :])
        for hp in range(nh // 2):
            yn = jnp.where(lane < N_HEAD_DIM, outs[2 * hp], pltpu.roll(outs[2 * hp + 1], N_HEAD_DIM, axis=1))
            ls = slice((g * (nh // 2) + hp) * LANES, (g * (nh // 2) + hp + 1) * LANES)
            yb_ref[:, ls] = (yn * _silu(nz_ref[:, ls].astype(F32))).astype(BF16)


def _nsa(p, b, t, qrope, bias, oc, kst, vs, kwt, vw, gates):
    g = N_KV_GROUPS
    nqb = t // Q_BLOCK
    qw = N_HPG * LANES
    rows = N_HPG * Q_BLOCK

    def qblk(w):
        return pl.BlockSpec((None, g, Q_BLOCK, w), lambda bi, i: (bi, 0, i, 0))

    def whole(r, c):
        return pl.BlockSpec((None, g, r, c), lambda bi, i: (bi, 0, 0, 0), pipeline_mode=pl.Buffered(1))

    return pl.pallas_call(
        _nsa_kernel,
        out_shape=jax.ShapeDtypeStruct((b * t, N_WIDTH), BF16),
        grid=(b, nqb),
        in_specs=[
            qblk(qw), qblk(LANES), qblk(qw),
            whole(2 * LANES, t), whole(t, LANES), whole(LANES, t), whole(t, LANES),
            qblk(LANES),
            pl.BlockSpec((Q_BLOCK, N_WIDTH), lambda bi, i: (bi * nqb + i, C_NZ // N_WIDTH)),
        ],
        out_specs=pl.BlockSpec((Q_BLOCK, N_WIDTH), lambda bi, i: (bi * nqb + i, 0)),
        scratch_shapes=[pltpu.VMEM((g, rows, 2 * LANES), BF16), pltpu.VMEM((g, rows, LANES), F32),
                        pltpu.VMEM((g, rows, LANES), F32), pltpu.VMEM((2, g, rows, SEL_CHUNK), F32)],
        compiler_params=_cparams(("parallel", "arbitrary")),
        name="nsa",
    )(qrope, bias, oc, kst, vs, kwt, vw, gates, p)


def _outproj_kernel(final, ya_ref, yb_ref, ga_ref, gb_ref, x_ref, wa_ref, wb_ref, wo_ref, fg_ref, o_ref):
    a = jnp.dot(ya_ref[...], wa_ref[...], preferred_element_type=F32)
    bq = jnp.dot(yb_ref[...], wb_ref[...], preferred_element_type=F32)
    merged = _sigmoid(ga_ref[...].astype(F32)) * a + _sigmoid(gb_ref[...].astype(F32)) * bq
    y = x_ref[...] + jnp.dot(merged.astype(BF16), wo_ref[...], preferred_element_type=F32)
    if final:
        y = y * lax.rsqrt(jnp.mean(y * y, axis=-1, keepdims=True) + NORM_EPS) * fg_ref[...]
    o_ref[...] = y


def _outproj(ya, yb, p, x2, wa, wb, wo, fg, final):
    m = x2.shape[0]
    tm = OUTPROJ_TM

    def full(a):
        return pl.BlockSpec(a.shape, lambda i: (0, 0))

    return pl.pallas_call(
        functools.partial(_outproj_kernel, final),
        out_shape=jax.ShapeDtypeStruct((m, D_MODEL), F32),
        grid=(m // tm,),
        in_specs=[
            pl.BlockSpec((tm, M_WIDTH), lambda i: (i, 0)),
            pl.BlockSpec((tm, N_WIDTH), lambda i: (i, 0)),
            pl.BlockSpec((tm, D_MODEL), lambda i: (i, C_GA // D_MODEL)),
            pl.BlockSpec((tm, D_MODEL), lambda i: (i, C_GB // D_MODEL)),
            pl.BlockSpec((tm, D_MODEL), lambda i: (i, 0)),
            full(wa), full(wb), full(wo), full(fg),
        ],
        out_specs=pl.BlockSpec((tm, D_MODEL), lambda i: (i, 0)),
        compiler_params=_cparams(("parallel",)),
        name="outproj",
    )(ya, yb, p, p, x2, wa, wb, wo, fg)


def _expand_cmp_w1(w1k, w1v):
    half = CMP_BLOCK // 2
    k3 = w1k.astype(BF16).reshape(CMP_BLOCK, 1, N_HEAD_DIM, 1, CMP_HIDDEN)
    v3 = w1v.astype(BF16).reshape(CMP_BLOCK, 1, N_HEAD_DIM, 1, CMP_HIDDEN)
    nc = 2 * N_KV_GROUPS
    eye = jnp.eye(nc, dtype=BF16)
    is_k = (jnp.arange(nc) < N_KV_GROUPS).astype(BF16)
    mk = (eye * is_k[:, None]).reshape(1, nc, 1, nc, 1)
    mv = (eye * (1 - is_k)[:, None]).reshape(1, nc, 1, nc, 1)
    z = k3 * mk + v3 * mv
    z = z.reshape(CMP_BLOCK * nc * N_HEAD_DIM, nc * CMP_HIDDEN)
    return z[:half * nc * N_HEAD_DIM], z[half * nc * N_HEAD_DIM:]


def kernel(x, positions, norm_g, w_in, conv_q_w, conv_q_b, conv_k_w, conv_k_b, b_igate, b_fgate, mh_norm_g,
           cmp_pos_k, cmp_w1_k, cmp_w2_k, cmp_pos_v, cmp_w1_v, cmp_w2_v, b_nsa_gate, w_branch_a, w_branch_b,
           w_out, final_norm_g):
    b, t, _ = x.shape
    assert t % (2 * SEL_CHUNK) == 0 and t >= WINDOW + Q_BLOCK and t // SEL_BLOCK <= LANES and t % MLSTM_CHUNK == 0
    x2 = x.reshape(b * t, D_MODEL)
    for l in range(norm_g.shape[0]):
        w_p = _wprep(w_in[l])
        p, aux = _inproj(x2, norm_g[l][None, :], w_p, w_p)

        sm_bias = jnp.zeros((1, LANES), F32).at[0, 0:M_HEADS].set(b_igate[l]).at[0, M_HEADS:2 * M_HEADS].set(b_fgate[l])
        ya = _mlstm(p, aux, b, t, conv_q_w[l], conv_q_b[l][None, :], conv_k_w[l], conv_k_b[l][None, :], sm_bias,
                    mh_norm_g[l][None, :])

        nr = t // CMP_STRIDE
        wt, wb = _expand_cmp_w1(cmp_w1_k[l], cmp_w1_v[l])
        w2pad = lambda w: jnp.pad(w, ((0, 0), (0, LANES - N_HEAD_DIM))).astype(BF16)
        kc, vct = _compress(aux, b, t, wt, wb, cmp_pos_k[l].reshape(1, -1), cmp_pos_v[l].reshape(1, -1),
                            cmp_w1_k[l].astype(BF16), cmp_w1_v[l].astype(BF16), w2pad(cmp_w2_k[l]), w2pad(cmp_w2_v[l]))

        half = ROPE_DIM // 2
        inv = jnp.power(jnp.float32(ROPE_THETA), -jnp.arange(half, dtype=F32) * (2.0 / ROPE_DIM))
        inv_row = inv[:, None]
        pos_col = positions.astype(F32)[:, None, :]
        gate_bias = jnp.zeros((1, LANES), F32).at[0, SM_G_OFF:SM_G_OFF + 3 * N_HEADS].set(b_nsa_gate[l])
        qrawt, qrope, kst, vs, kwt, vw, gates, gates_t = _kvprep(p, aux, b, t, pos_col, inv_row, gate_bias)

        ci = np.arange(nr)[None, :] * CMP_STRIDE
        sj = np.arange(LANES)[:, None] * SEL_BLOCK
        overlap_t = jnp.asarray((ci < sj + SEL_BLOCK) & (ci + CMP_BLOCK > sj) & (np.arange(nr)[None, :] < nr - 1)
                                & (np.arange(LANES)[:, None] < t // SEL_BLOCK), dtype=BF16)
        oc, bias = _select(qrawt, kc, vct, overlap_t, gates_t)
        yb = _nsa(p, b, t, qrope, bias, oc, kst, vs, kwt, vw, gates)

        x2 = _outproj(ya, yb, p, x2, w_branch_a[l].astype(BF16), w_branch_b[l].astype(BF16),
                      w_out[l].astype(BF16), final_norm_g[None, :], l == norm_g.shape[0] - 1)
    return x2.reshape(b, t, D_MODEL)
```

```python
import functools
import math

import jax
import jax.numpy as jnp
import numpy as np
from jax import lax
from jax.experimental import pallas as pl
from jax.experimental.pallas import tpu as pltpu

F32 = jnp.float32
BF16 = jnp.bfloat16

D_MODEL = 1024
M_HEADS = 4
M_HEAD_DIM = 256
M_WIDTH = M_HEADS * M_HEAD_DIM
CONV_WIDTH = 4
N_HEADS = 8
N_KV_GROUPS = 2
N_HPG = N_HEADS // N_KV_GROUPS
N_HEAD_DIM = 64
N_WIDTH = N_HEADS * N_HEAD_DIM
CMP_BLOCK = 32
CMP_STRIDE = 16
CMP_HIDDEN = 128
SEL_BLOCK = 64
SEL_TOPK = 16
WINDOW = 512
Q_BLOCK = 256
ROPE_THETA = 500000.0
ROPE_DIM = N_HEAD_DIM // 4
NORM_EPS = 1e-6
NEG = -1e30
FORCE = 1e9

LANES = 128
SUBLANES = 8
MLSTM_CHUNK = 256
SEL_CHUNK = 512
SEL_UNROLL = 4
V7X_VMEM_BYTES = 64 * 1024 * 1024
VMEM_LIMIT = V7X_VMEM_BYTES - 8 * 1024 * 1024
INPROJ_VMEM_LIMIT = V7X_VMEM_BYTES - 4 * 1024 * 1024
INPROJ_TM, OUTPROJ_TM, KVPREP_TK, SELECT_QS = 2048, 1024, 1024, 2048
SM_F_OFF = M_HEADS
SM_G_OFF = 2 * M_HEADS
GATE_ROWS = -(-3 * N_HPG // SUBLANES) * SUBLANES

C_MQ, C_MK, C_MV, C_MO, C_MZ, C_GA, C_GB = 0, 1024, 2048, 3072, 4096, 5120, 6144
C_NQ, C_NZ = 7168, 7680
C_KV = 8192
C_CMP = 8704
C_SM = 8960
P_W = 9216
AUX_W = 3 * LANES
X_KC, X_VC, X_SM = 0, 1, 2
_O_MI, _O_NQ, _O_KC, _O_KS, _O_NG, _O_NZ, _O_GA, _O_END = 5120, 5128, 5640, 5896, 6408, 6432, 6944, 8992


def _cparams(sem, vmem_limit=VMEM_LIMIT):
    return pltpu.CompilerParams(dimension_semantics=sem, vmem_limit_bytes=vmem_limit)


def _sigmoid(x):
    return 0.5 * jnp.tanh(0.5 * x) + 0.5


def _silu(x):
    h = 0.5 * x
    return h + h * jnp.tanh(h)


def _log_sigmoid(x):
    return jnp.minimum(x, 0.0) - jnp.log1p(jnp.exp(-jnp.abs(x)))


PROJ_TN = 1536
AUX_TILE = C_CMP // PROJ_TN
AUX_OFF = C_CMP % PROJ_TN


N_HEAD_TILES = _O_MI // PROJ_TN


WPREP_ROWS, WPREP_SUB, WPREP_PIECE = 128, 64, 512
_W_SLABS = ((0, 0, _O_MI), (C_GA, _O_GA, _O_END - _O_GA), (C_NQ, _O_NQ, _O_KC - _O_NQ), (C_NZ, _O_NZ, _O_GA - _O_NZ),
            (C_KV, _O_KS, _O_NG - _O_KS), (C_CMP, _O_KC, _O_KS - _O_KC))


def _wprep_kernel(w_ref, o_ref):
    for r0 in range(0, WPREP_ROWS, WPREP_SUB):
        rows = pl.ds(r0, WPREP_SUB)
        for dst, src, width in _W_SLABS:
            for c in range(0, width, WPREP_PIECE):
                n = min(WPREP_PIECE, width - c)
                a0 = (src + c) // LANES * LANES
                off = src + c - a0
                if off == 0:
                    piece = w_ref[rows, a0:a0 + n]
                else:
                    piece = w_ref[rows, a0:min(a0 + n + LANES, _O_END)][:, off:off + n]
                o_ref[rows, dst + c:dst + c + n] = piece.astype(BF16)
        lane = lax.broadcasted_iota(jnp.int32, (WPREP_SUB, LANES), 1)
        ng0 = _O_NG - SM_G_OFF
        sm = jnp.where(lane < SM_G_OFF, w_ref[rows, _O_MI:_O_MI + LANES],
                       jnp.where(lane < SM_G_OFF + _O_NZ - _O_NG, w_ref[rows, ng0:ng0 + LANES], 0.0))
        o_ref[rows, C_SM:C_SM + LANES] = sm.astype(BF16)
        o_ref[rows, C_SM + LANES:P_W] = jnp.zeros((WPREP_SUB, P_W - C_SM - LANES), BF16)


def _wprep(w, l):
    assert _O_MI % LANES == 0 and (_O_NG - SM_G_OFF) % LANES == 0 and _O_NQ - _O_MI == SM_G_OFF
    return pl.pallas_call(
        _wprep_kernel,
        out_shape=jax.ShapeDtypeStruct((w.shape[1], P_W), BF16),
        grid=(w.shape[1] // WPREP_ROWS,),
        in_specs=[pl.BlockSpec((None, WPREP_ROWS, w.shape[2]), lambda i: (l, i, 0))],
        out_specs=pl.BlockSpec((WPREP_ROWS, P_W), lambda i: (i, 0)),
        compiler_params=_cparams(("parallel",)),
        name="wprep",
    )(w)


def _inproj_kernel(x_ref, g_ref, wh_ref, wt_ref, o_ref, aux_ref, hn_ref):
    j = pl.program_id(1)

    @pl.when(j == 0)
    def _():
        x = x_ref[...]
        r = lax.rsqrt(jnp.mean(x * x, axis=-1, keepdims=True) + NORM_EPS)
        hn_ref[...] = (x * r * g_ref[...]).astype(BF16)

    @pl.when(j < N_HEAD_TILES)
    def _():
        o_ref[...] = jnp.dot(hn_ref[...], wh_ref[...], preferred_element_type=F32).astype(BF16)

    @pl.when((j >= N_HEAD_TILES) & (j != AUX_TILE))
    def _():
        o_ref[...] = jnp.dot(hn_ref[...], wt_ref[...], preferred_element_type=F32).astype(BF16)

    @pl.when(j == AUX_TILE)
    def _():
        acc = jnp.dot(hn_ref[...], wt_ref[...], preferred_element_type=F32)
        o_ref[...] = acc.astype(BF16)
        aux_ref[...] = acc[:, AUX_OFF:AUX_OFF + AUX_W]


def _inproj(x2, norm_g, w_head, w_tail):
    m = x2.shape[0]
    tm, tn = INPROJ_TM, PROJ_TN
    return pl.pallas_call(
        _inproj_kernel,
        out_shape=(jax.ShapeDtypeStruct((m, P_W), BF16), jax.ShapeDtypeStruct((m, AUX_W), F32)),
        grid=(m // tm, P_W // tn),
        in_specs=[
            pl.BlockSpec((tm, D_MODEL), lambda i, j: (i, 0)),
            pl.BlockSpec((1, D_MODEL), lambda i, j: (0, 0)),
            pl.BlockSpec((D_MODEL, tn), lambda i, j: (0, jnp.minimum(j, N_HEAD_TILES - 1))),
            pl.BlockSpec((D_MODEL, tn), lambda i, j: (0, jnp.maximum(j, N_HEAD_TILES))),
        ],
        out_specs=(pl.BlockSpec((tm, tn), lambda i, j: (i, j)), pl.BlockSpec((tm, AUX_W), lambda i, j: (i, 0))),
        scratch_shapes=[pltpu.VMEM((tm, D_MODEL), BF16)],
        compiler_params=_cparams(("parallel", "arbitrary"), INPROJ_VMEM_LIMIT),
        name="inproj",
    )(x2, norm_g, w_head, w_tail)


def _causal_conv_silu(u16, prev_ref, w_ref, b_ref):
    n, w = u16.shape
    u = u16.astype(F32)
    row = lax.broadcasted_iota(jnp.int32, (n, n), 0)
    col = lax.broadcasted_iota(jnp.int32, (n, n), 1)
    sub = lax.broadcasted_iota(jnp.int32, (SUBLANES, w), 0)
    prev = prev_ref[...]
    y = b_ref[...] + u * w_ref[CONV_WIDTH - 1:CONV_WIDTH, :]
    for d in range(1, CONV_WIDTH):
        shift = jnp.where(row - col == d, 1.0, 0.0).astype(BF16)
        sh = jnp.dot(shift, u16, preferred_element_type=F32)
        head = jnp.where(sub < d, pltpu.roll(prev, d, axis=0), sh[0:SUBLANES, :])
        sh = jnp.concatenate([head, sh[SUBLANES:, :]], axis=0)
        y = y + sh * w_ref[CONV_WIDTH - 1 - d:CONV_WIDTH - d, :]
    prev_ref[...] = u[n - SUBLANES:n, :]
    return _silu(y)


def _mlstm_kernel(q_ref, k_ref, v_ref, og_ref, z_ref, sm_ref, cqw_ref, cqb_ref, ckw_ref, ckb_ref,
                  sb_ref, ng_ref, ya_ref, qprev, kprev, c_state, n_state, m_state):
    L = q_ref.shape[0]
    d = M_HEAD_DIM

    @pl.when(pl.program_id(1) == 0)
    def _():
        qprev[...] = jnp.zeros_like(qprev)
        kprev[...] = jnp.zeros_like(kprev)
        c_state[...] = jnp.zeros_like(c_state)
        n_state[...] = jnp.zeros_like(n_state)
        m_state[...] = jnp.zeros_like(m_state)

    mq = _causal_conv_silu(q_ref[...], qprev, cqw_ref, cqb_ref)
    mk = _causal_conv_silu(k_ref[...], kprev, ckw_ref, ckb_ref) * (d ** -0.5)

    sm = sm_ref[...] + sb_ref[...]
    lf = _log_sigmoid(sm)
    row = lax.broadcasted_iota(jnp.int32, (L, L), 0)
    col = lax.broadcasted_iota(jnp.int32, (L, L), 1)
    causal = row >= col
    tril = causal.astype(F32)
    triu = (row <= col).astype(F32)
    a_col = jnp.dot(tril, lf, precision=lax.Precision.HIGHEST, preferred_element_type=F32)
    sm_t = sm.T[0:SUBLANES, :]
    a_row = jnp.dot(_log_sigmoid(sm_t), triu, precision=lax.Precision.HIGHEST,
                    preferred_element_type=F32)

    for h in range(M_HEADS):
        hs = slice(h * d, (h + 1) * d)
        a_c = a_col[:, SM_F_OFF + h:SM_F_OFF + h + 1]
        li_c = sm[:, h:h + 1]
        a_r = a_row[SM_F_OFF + h:SM_F_OFF + h + 1, :]
        li_r = sm_t[h:h + 1, :]
        m_prev = m_state[h:h + 1, 0:1]
        a_end = a_c[L - 1:L, :]

        log_d = jnp.where(causal, a_c - (a_r - li_r), -jnp.inf)
        inter = a_c + m_prev
        m_t = jnp.maximum(inter, jnp.max(log_d, axis=1, keepdims=True))
        w_inter = jnp.exp(inter - m_t)
        dm = jnp.exp(log_d - m_t)

        qh = mq[:, hs]
        kh = mk[:, hs]
        qb = qh.astype(BF16)
        kb = kh.astype(BF16)
        vb = v_ref[:, hs]
        s = lax.dot_general(qb, kb, (((1,), (1,)), ((), ())), preferred_element_type=F32) * dm
        c_old = c_state[h]
        n_old = n_state[h:h + 1, :]
        num = (w_inter * jnp.dot(qb, c_old.astype(BF16), preferred_element_type=F32)
               + jnp.dot(s.astype(BF16), vb, preferred_element_type=F32))
        den = (w_inter * jnp.sum(qh * n_old, axis=1, keepdims=True)
               + jnp.sum(s, axis=1, keepdims=True))
        hh = num * (1.0 / jnp.maximum(jnp.abs(den), jnp.exp(-m_t)))

        logw = a_end - a_c + li_c
        m_new = jnp.maximum(a_end + m_prev, jnp.max(logw, axis=0, keepdims=True))
        wk = jnp.exp(logw - m_new)
        decay = jnp.exp(a_end + m_prev - m_new)
        kw = kh * wk
        c_state[h] = decay * c_old + lax.dot_general(kw.astype(BF16), vb, (((0,), (0,)), ((), ())),
                                                     preferred_element_type=F32)
        n_state[h:h + 1, :] = decay * n_old + jnp.sum(kw, axis=0, keepdims=True)
        m_state[h:h + 1, :] = jnp.broadcast_to(m_new, (1, LANES))

        mu = jnp.mean(hh, axis=1, keepdims=True)
        dc = hh - mu
        var = jnp.mean(dc * dc, axis=1, keepdims=True)
        hn = dc * lax.rsqrt(var + NORM_EPS) * ng_ref[:, hs]
        ya = _sigmoid(og_ref[:, hs].astype(F32)) * hn * _silu(z_ref[:, hs].astype(F32))
        ya_ref[:, hs] = ya.astype(BF16)


def _mlstm(p, aux, b, t, conv_q_w, conv_q_b, conv_k_w, conv_k_b, sm_bias, mh_norm_g):
    L = MLSTM_CHUNK
    nc = t // L

    def colspec(cstart):
        return pl.BlockSpec((L, M_WIDTH), lambda bi, c: (bi * nc + c, cstart // M_WIDTH))

    def full(shape):
        return pl.BlockSpec(shape, lambda bi, c: (0,) * len(shape))

    return pl.pallas_call(
        _mlstm_kernel,
        out_shape=jax.ShapeDtypeStruct((b * t, M_WIDTH), BF16),
        grid=(b, nc),
        in_specs=[
            colspec(C_MQ), colspec(C_MK), colspec(C_MV), colspec(C_MO), colspec(C_MZ),
            pl.BlockSpec((L, LANES), lambda bi, c: (bi * nc + c, X_SM)),
            full((CONV_WIDTH, M_WIDTH)), full((1, M_WIDTH)), full((CONV_WIDTH, M_WIDTH)), full((1, M_WIDTH)),
            full((1, LANES)), full((1, M_WIDTH)),
        ],
        out_specs=pl.BlockSpec((L, M_WIDTH), lambda bi, c: (bi * nc + c, 0)),
        scratch_shapes=[
            pltpu.VMEM((SUBLANES, M_WIDTH), F32), pltpu.VMEM((SUBLANES, M_WIDTH), F32),
            pltpu.VMEM((M_HEADS, M_HEAD_DIM, M_HEAD_DIM), F32),
            pltpu.VMEM((SUBLANES, M_HEAD_DIM), F32), pltpu.VMEM((SUBLANES, LANES), F32),
        ],
        compiler_params=_cparams(("parallel", "arbitrary")),
        name="mlstm",
    )(p, p, p, p, p, aux, conv_q_w, conv_q_b, conv_k_w, conv_k_b, sm_bias, mh_norm_g)


def _compress_kernel(pk_ref, pv_ref, wt_ref, wb_ref, posk_ref, posv_ref, w1k_ref, w1v_ref, w2k_ref, w2v_ref,
                     kc_ref, vct_ref):
    nr = pk_ref.shape[0] // CMP_STRIDE
    cw = 2 * LANES
    u = jnp.zeros((nr, wt_ref.shape[1]), F32)
    v = jnp.zeros((nr, wb_ref.shape[1]), F32)
    for l in range(CMP_STRIDE):
        x = jnp.concatenate([pk_ref[pl.ds(l, nr, stride=CMP_STRIDE), :],
                             pv_ref[pl.ds(l, nr, stride=CMP_STRIDE), :]], axis=1).astype(BF16)
        u = u + jnp.dot(x, wt_ref[l * cw:(l + 1) * cw, :], preferred_element_type=F32)
        v = v + jnp.dot(x, wb_ref[l * cw:(l + 1) * cw, :], preferred_element_type=F32)
    v = pltpu.roll(v, nr - 1, axis=0)
    rowi = lax.broadcasted_iota(jnp.int32, v.shape, 0)
    v = jnp.where(rowi < nr - 1, v, 0.0)

    def pos_term(pos_ref, w_ref):
        pb = jnp.broadcast_to(pos_ref[...].astype(BF16), (SUBLANES, CMP_BLOCK * N_HEAD_DIM))
        return jnp.dot(pb, w_ref[...], preferred_element_type=F32)[0:1, :]

    pk = pos_term(posk_ref, w1k_ref)
    pv = pos_term(posv_ref, w1v_ref)
    hid = u + v + jnp.concatenate([pk, pk, pv, pv], axis=1)
    act = 0.5 * hid * (1.0 + jnp.tanh(math.sqrt(2.0 / math.pi) * (hid + 0.044715 * (hid * hid * hid))))
    for c in range(2 * N_KV_GROUPS):
        w2 = w2k_ref[...] if c < N_KV_GROUPS else w2v_ref[...]
        out = jnp.dot(act[:, c * CMP_HIDDEN:(c + 1) * CMP_HIDDEN].astype(BF16), w2,
                      preferred_element_type=F32)
        if c < N_KV_GROUPS:
            kc_ref[c] = out.astype(BF16)
        else:
            out_t = out.T
            ones_row = lax.broadcasted_iota(jnp.int32, out_t.shape, 0) == N_HEAD_DIM
            vct_ref[c - N_KV_GROUPS] = jnp.where(ones_row, 1.0, out_t).astype(BF16)


def _compress(aux, b, t, wt, wb, posk, posv, w1k, w1v, w2k, w2v):
    nr = t // CMP_STRIDE
    g = N_KV_GROUPS

    def full(a):
        return pl.BlockSpec(a.shape, lambda bi: (0,) * a.ndim)

    return pl.pallas_call(
        _compress_kernel,
        out_shape=(jax.ShapeDtypeStruct((b, g, nr, LANES), BF16), jax.ShapeDtypeStruct((b, g, LANES, nr), BF16)),
        grid=(b,),
        in_specs=[pl.BlockSpec((t, LANES), lambda bi: (bi, X_KC)),
                  pl.BlockSpec((t, LANES), lambda bi: (bi, X_VC)),
                  full(wt), full(wb), full(posk), full(posv), full(w1k), full(w1v), full(w2k), full(w2v)],
        out_specs=(pl.BlockSpec((None, g, nr, LANES), lambda bi: (bi, 0, 0, 0)),
                   pl.BlockSpec((None, g, LANES, nr), lambda bi: (bi, 0, 0, 0))),
        compiler_params=_cparams(("parallel",)),
        name="compress",
    )(aux, aux, wt, wb, posk, posv, w1k, w1v, w2k, w2v)


def _kvprep_kernel(q_ref, kv_ref, sm_ref, pos_ref, inv_ref, gb_ref,
                   qrawt_ref, qrope_ref, kst_ref, vs_ref, kwt_ref, vw_ref, gates_ref, gatest_ref):
    tk = q_ref.shape[0]
    lane = lax.broadcasted_iota(jnp.int32, (tk, LANES), 1)
    half = ROPE_DIM // 2
    ang = inv_ref[...] * pos_ref[...]
    cos8 = jnp.cos(ang)
    sin8 = jnp.sin(ang)
    rest = N_HEAD_DIM - ROPE_DIM
    one_r, zero_r, zero_h = jnp.ones((rest, tk), F32), jnp.zeros((rest, tk), F32), jnp.zeros((half, tk), F32)
    heads = LANES // N_HEAD_DIM
    cos = jnp.concatenate([cos8, cos8, one_r] * heads, axis=0).T
    s_up = jnp.concatenate([-sin8, zero_h, zero_r] * heads, axis=0).T
    s_dn = jnp.concatenate([zero_h, sin8, zero_r] * heads, axis=0).T

    def rope(u):
        return u * cos + pltpu.roll(u, LANES - half, axis=1) * s_up + pltpu.roll(u, half, axis=1) * s_dn

    def head_pad(u, odd):
        if odd:
            u = pltpu.roll(u, N_HEAD_DIM, axis=1)
        return jnp.where(lane < N_HEAD_DIM, u, 0.0)

    scale = N_HEAD_DIM ** -0.5 * math.log2(math.e)
    zeros_t = jnp.zeros((N_HEAD_DIM, tk), BF16)
    for g in range(N_KV_GROUPS):
        for hp in range(N_HPG // 2):
            c0 = g * (N_HPG * N_HEAD_DIM) + hp * LANES
            u = q_ref[:, c0:c0 + LANES].astype(F32) * scale
            ur = rope(u)
            u_t = u.T.astype(BF16)
            for odd in range(2):
                h = 2 * hp + odd
                qrawt_ref[g, h * LANES:h * LANES + N_HEAD_DIM, :] = u_t[odd * N_HEAD_DIM:(odd + 1) * N_HEAD_DIM, :]
                qrawt_ref[g, h * LANES + N_HEAD_DIM:(h + 1) * LANES, :] = zeros_t
                qrope_ref[g, :, h * LANES:(h + 1) * LANES] = head_pad(ur, odd).astype(BF16)

    ks_t = rope(kv_ref[:, 0:LANES].astype(F32)).T
    kw_t = rope(kv_ref[:, 2 * LANES:3 * LANES].astype(F32)).T
    vs = kv_ref[:, LANES:2 * LANES].astype(F32)
    vw = kv_ref[:, 3 * LANES:4 * LANES].astype(F32)
    t_glob = pl.program_id(1) * tk + lax.broadcasted_iota(jnp.int32, (LANES, tk), 1)
    blk_row = lax.broadcasted_iota(jnp.int32, (LANES, tk), 0)
    onehot = jnp.where(blk_row == t_glob // SEL_BLOCK, 1.0, 0.0).astype(BF16)
    ones_lane = jnp.where(lane == N_HEAD_DIM, 1.0, 0.0)
    gates = _sigmoid(sm_ref[...] + gb_ref[...])
    for g in range(N_KV_GROUPS):
        rs = slice(g * N_HEAD_DIM, (g + 1) * N_HEAD_DIM)
        kst_ref[g, 0:LANES, :] = onehot
        kst_ref[g, LANES:LANES + N_HEAD_DIM, :] = ks_t[rs, :].astype(BF16)
        kst_ref[g, LANES + N_HEAD_DIM:2 * LANES, :] = zeros_t
        kwt_ref[g, 0:N_HEAD_DIM, :] = kw_t[rs, :].astype(BF16)
        kwt_ref[g, N_HEAD_DIM:LANES, :] = zeros_t
        vs_ref[g] = (head_pad(vs, g) + ones_lane).astype(BF16)
        vw_ref[g] = (head_pad(vw, g) + ones_lane).astype(BF16)
        goff = SM_G_OFF + g * (3 * N_HPG)
        gg = pltpu.roll(gates, LANES - goff, axis=1)
        gates_ref[g] = gg
        gatest_ref[g] = gg.T[0:GATE_ROWS, :]


def _kvprep(p, aux, b, t, pos_col, inv_row, gate_bias):
    tk = KVPREP_TK
    nt = t // tk
    g = N_KV_GROUPS
    qw = N_HPG * LANES

    def tok(shape):
        return pl.BlockSpec((None, g) + shape, lambda bi, i: (bi, 0, i, 0))

    def tok_t(rows):
        return pl.BlockSpec((None, g, rows, tk), lambda bi, i: (bi, 0, 0, i))

    return pl.pallas_call(
        _kvprep_kernel,
        out_shape=(
            jax.ShapeDtypeStruct((b, g, qw, t), BF16), jax.ShapeDtypeStruct((b, g, t, qw), BF16),
            jax.ShapeDtypeStruct((b, g, 2 * LANES, t), BF16), jax.ShapeDtypeStruct((b, g, t, LANES), BF16),
            jax.ShapeDtypeStruct((b, g, LANES, t), BF16), jax.ShapeDtypeStruct((b, g, t, LANES), BF16),
            jax.ShapeDtypeStruct((b, g, t, LANES), F32), jax.ShapeDtypeStruct((b, g, GATE_ROWS, t), F32),
        ),
        grid=(b, nt),
        in_specs=[
            pl.BlockSpec((tk, N_WIDTH), lambda bi, i: (bi * nt + i, C_NQ // N_WIDTH)),
            pl.BlockSpec((tk, 4 * LANES), lambda bi, i: (bi * nt + i, C_KV // (4 * LANES))),
            pl.BlockSpec((tk, LANES), lambda bi, i: (bi * nt + i, X_SM)),
            pl.BlockSpec((None, 1, tk), lambda bi, i: (bi, 0, i)),
            pl.BlockSpec(inv_row.shape, lambda bi, i: (0, 0)),
            pl.BlockSpec((1, LANES), lambda bi, i: (0, 0)),
        ],
        out_specs=(tok_t(qw), tok((tk, qw)), tok_t(2 * LANES), tok((tk, LANES)),
                   tok_t(LANES), tok((tk, LANES)), tok((tk, LANES)), tok_t(GATE_ROWS)),
        compiler_params=_cparams(("parallel", "arbitrary")),
        name="kvprep",
    )(p, p, aux, pos_col, inv_row, gate_bias)


CMP_ROW_CHUNK = 128


def _select_kernel(qt_ref, kc_ref, vct_ref, ovt_ref, gt_ref, oc_ref, bias_ref, imp_ref):
    qs = qt_ref.shape[1]
    nr = kc_ref.shape[0]
    t0 = pl.program_id(2) * qs
    sees_any = t0 + lax.broadcasted_iota(jnp.int32, (1, qs), 1) >= CMP_BLOCK - 1

    def cmp_branch(rows):
        m_c = (lax.broadcasted_iota(jnp.int32, (rows, qs), 0) * CMP_STRIDE + (CMP_BLOCK - 1)
               <= t0 + lax.broadcasted_iota(jnp.int32, (rows, qs), 1))
        kc = kc_ref[0:rows, :]
        vct = vct_ref[:, 0:rows]
        ovt = ovt_ref[:, 0:rows]
        imp = jnp.zeros((LANES, qs), F32)
        for h in range(N_HPG):
            s = jnp.dot(kc, qt_ref[h * LANES:(h + 1) * LANES, :], preferred_element_type=F32)
            s = jnp.where(m_c, s, NEG)
            e = jnp.exp2(s - jnp.max(s, axis=0, keepdims=True)).astype(BF16)
            r = jnp.dot(vct, e, preferred_element_type=F32)
            inv = jnp.where(sees_any, 1.0 / r[N_HEAD_DIM:N_HEAD_DIM + 1, :], 0.0)
            imp = imp + jnp.dot(ovt, e, preferred_element_type=F32) * inv
            oc_ref[:, h * LANES:(h + 1) * LANES] = (r * (inv * gt_ref[3 * h:3 * h + 1, :])).T.astype(BF16)
        imp_ref[...] = imp

    n_var = nr // CMP_ROW_CHUNK
    visible = (t0 + qs - CMP_BLOCK) // CMP_STRIDE + 1
    need = jnp.clip((visible + CMP_ROW_CHUNK - 1) // CMP_ROW_CHUNK, 1, n_var)
    for v in range(1, n_var + 1):
        pl.when(need == v)(functools.partial(cmp_branch, v * CMP_ROW_CHUNK))
    imp = imp_ref[...]

    blk = lax.broadcasted_iota(jnp.int32, (LANES, qs), 0)
    tq = t0 + lax.broadcasted_iota(jnp.int32, (LANES, qs), 1)
    cur = tq // SEL_BLOCK
    forced = (blk == 0) | (blk == cur) | (blk == cur - 1)
    n_sel = (nr * CMP_STRIDE) // SEL_BLOCK
    free_score = jnp.where(blk * SEL_BLOCK <= tq, imp, -1.0)
    blk_f = blk.astype(F32)
    n_top = min(SEL_TOPK, n_sel)
    n_forced = 3

    def pick(score, n_iter):
        score = jnp.where(blk < n_sel, score, -jnp.inf)
        for _ in range(n_iter):
            mx = jnp.max(score, axis=0, keepdims=True)
            idx = jnp.min(jnp.where(score == mx, blk_f, float(LANES)), axis=0, keepdims=True)
            score = jnp.where(blk_f == idx, -jnp.inf, score)
        bias_t = jnp.where((score == -jnp.inf) & (blk < n_sel), 0.0, NEG)
        bias_ref[...] = bias_t.T.astype(BF16)

    all_three_forced = (t0 >= 2 * SEL_BLOCK) & (n_top >= n_forced)

    @pl.when(all_three_forced)
    def _():
        pick(jnp.where(forced, -jnp.inf, free_score), n_top - n_forced)

    @pl.when(jnp.logical_not(all_three_forced))
    def _():
        pick(jnp.where(forced, FORCE, free_score), n_top)


def _select(qrawt, kc, vct, ovt, gates_t):
    b, g, qw, t = qrawt.shape
    nr = kc.shape[2]
    qs = SELECT_QS

    def whole(rows, cols):
        return pl.BlockSpec((None, None, rows, cols), lambda bi, gi, i: (bi, gi, 0, 0))

    return pl.pallas_call(
        _select_kernel,
        out_shape=(jax.ShapeDtypeStruct((b, g, t, qw), BF16), jax.ShapeDtypeStruct((b, g, t, LANES), BF16)),
        grid=(b, g, t // qs),
        in_specs=[
            pl.BlockSpec((None, None, qw, qs), lambda bi, gi, i: (bi, gi, 0, i)),
            whole(nr, LANES), whole(LANES, nr),
            pl.BlockSpec(ovt.shape, lambda bi, gi, i: (0, 0)),
            pl.BlockSpec((None, None, GATE_ROWS, qs), lambda bi, gi, i: (bi, gi, 0, i)),
        ],
        out_specs=(pl.BlockSpec((None, None, qs, qw), lambda bi, gi, i: (bi, gi, i, 0)),
                   pl.BlockSpec((None, None, qs, LANES), lambda bi, gi, i: (bi, gi, i, 0))),
        scratch_shapes=[pltpu.VMEM((LANES, qs), F32)],
        compiler_params=_cparams(("parallel", "parallel", "parallel")),
        name="select",
    )(qrawt, kc, vct, ovt, gates_t)


def _nsa_kernel(qrope_ref, bias_ref, oc_ref, kst_ref, vs_ref, kwt_ref, vw_ref, gates_ref, nz_ref,
                yb_ref, qaug_ref, acc_ref, m_ref, s_ref):
    i = pl.program_id(1)
    t0 = i * Q_BLOCK
    Q = Q_BLOCK
    nh = N_HPG
    ng = N_KV_GROUPS
    rows = nh * Q
    t0a = pl.multiple_of(t0, Q)
    causal = (lax.broadcasted_iota(jnp.int32, (Q, Q), 1) <= lax.broadcasted_iota(jnp.int32, (Q, Q), 0))
    lane = lax.broadcasted_iota(jnp.int32, (Q, LANES), 1)

    def q_rope(g):
        return jnp.concatenate([qrope_ref[g, :, h * LANES:(h + 1) * LANES] for h in range(nh)], axis=0)

    def mask_heads(s, m):
        return jnp.concatenate([jnp.where(m, s[h * Q:(h + 1) * Q, :], NEG) for h in range(nh)], axis=0)

    first_blk = t0 // SEL_BLOCK
    for g in range(ng):
        qr = q_rope(g)
        bias = bias_ref[g]
        bias_early = jnp.where(lane < first_blk, bias, NEG).astype(BF16)
        qaug_ref[g, :, 0:LANES] = jnp.concatenate([bias_early] * nh, axis=0)
        qaug_ref[g, :, LANES:2 * LANES] = qr
        q_diag = jnp.concatenate([jnp.concatenate([bias] * nh, axis=0), qr], axis=1)
        s_d = jnp.dot(q_diag, kst_ref[g, :, pl.ds(t0a, Q)], preferred_element_type=F32)
        s_d = mask_heads(s_d, causal)
        m0 = jnp.max(s_d, axis=1, keepdims=True)
        p_d = jnp.exp2(s_d - m0)
        m_ref[g] = jnp.broadcast_to(m0, (rows, LANES))
        acc_ref[g] = jnp.dot(p_d.astype(BF16), vs_ref[g, pl.ds(t0a, Q), :], preferred_element_type=F32)

    ncb = SEL_CHUNK // LANES
    n_chunks = (t0 + SEL_CHUNK - 1) // SEL_CHUNK
    last = jnp.maximum(n_chunks - 1, 0)

    def scores(g, c, slot):
        k0 = pl.multiple_of(c * SEL_CHUNK, SEL_CHUNK)
        s_ref[slot, g] = jnp.dot(qaug_ref[g], kst_ref[g, :, pl.ds(k0, SEL_CHUNK)], preferred_element_type=F32)

    def consume(g, c, slot):
        k0 = pl.multiple_of(c * SEL_CHUNK, SEL_CHUNK)
        s = s_ref[slot, g]
        m_old = m_ref[g]
        m_new = jnp.maximum(m_old, jnp.max(s, axis=1, keepdims=True))
        p = jnp.concatenate([jnp.exp2(s[:, cb * LANES:(cb + 1) * LANES] - m_new) for cb in range(ncb)], axis=1)
        acc_ref[g] = (jnp.exp2(m_old - m_new) * acc_ref[g]
                      + jnp.dot(p.astype(BF16), vs_ref[g, pl.ds(k0, SEL_CHUNK), :], preferred_element_type=F32))
        m_ref[g] = m_new

    for g in range(ng):
        scores(g, 0, 0)

    def steps(c, n):
        for u in range(n):
            for g in range(ng):
                scores(g, jnp.minimum(c + u + 1, last), (u + 1) % 2)
                consume(g, c + u, u % 2)

    def sel_quad(c4, carry):
        steps(SEL_UNROLL * c4, SEL_UNROLL)
        return carry

    n_quads = n_chunks // SEL_UNROLL
    lax.fori_loop(0, n_quads, sel_quad, 0)
    rem = n_chunks - SEL_UNROLL * n_quads

    @pl.when(rem >= 2)
    def _():
        steps(SEL_UNROLL * n_quads, 2)

    @pl.when(rem % 2 == 1)
    def _():
        for g in range(ng):
            consume(g, n_chunks - 1, 0)

    wlen = WINDOW + Q
    ws = pl.multiple_of(jnp.maximum(t0 - WINDOW, 0), Q)
    diff = (t0 + lax.broadcasted_iota(jnp.int32, (Q, wlen), 0)) - (ws + lax.broadcasted_iota(jnp.int32, (Q, wlen), 1))
    m_w = (diff >= 0) & (diff < WINDOW)
    for g in range(ng):
        gates = gates_ref[g]

        def gate_col(br):
            return jnp.concatenate([gates[:, 3 * h + br:3 * h + br + 1] for h in range(nh)], axis=0)

        acc = acc_ref[g]
        o_s = acc * (gate_col(1) / acc[:, N_HEAD_DIM:N_HEAD_DIM + 1])
        s_w = jnp.dot(q_rope(g), kwt_ref[g, :, pl.ds(ws, wlen)], preferred_element_type=F32)
        s_w = mask_heads(s_w, m_w)
        p_w = jnp.exp2(s_w - jnp.max(s_w, axis=1, keepdims=True))
        acc_w = jnp.dot(p_w.astype(BF16), vw_ref[g, pl.ds(ws, wlen), :], preferred_element_type=F32)
        o_w = acc_w * (gate_col(2) / acc_w[:, N_HEAD_DIM:N_HEAD_DIM + 1])
        outs = []
        for h in range(nh):
            rs = slice(h * Q, (h + 1) * Q)
            outs.append(oc_ref[g, :, h * LANES:(h + 1) * LANES].astype(F32) + o_s[rs, :] + o_w[rs, :])
        for hp in range(nh // 2):
            yn = jnp.where(lane < N_HEAD_DIM, outs[2 * hp], pltpu.roll(outs[2 * hp + 1], N_HEAD_DIM, axis=1))
            ls = slice((g * (nh // 2) + hp) * LANES, (g * (nh // 2) + hp + 1) * LANES)
            yb_ref[:, ls] = (yn * _silu(nz_ref[:, ls].astype(F32))).astype(BF16)


def _nsa(p, b, t, qrope, bias, oc, kst, vs, kwt, vw, gates):
    g = N_KV_GROUPS
    nqb = t // Q_BLOCK
    qw = N_HPG * LANES
    rows = N_HPG * Q_BLOCK

    def qblk(w):
        return pl.BlockSpec((None, g, Q_BLOCK, w), lambda bi, i: (bi, 0, i, 0))

    def whole(r, c):
        return pl.BlockSpec((None, g, r, c), lambda bi, i: (bi, 0, 0, 0), pipeline_mode=pl.Buffered(1))

    return pl.pallas_call(
        _nsa_kernel,
        out_shape=jax.ShapeDtypeStruct((b * t, N_WIDTH), BF16),
        grid=(b, nqb),
        in_specs=[
            qblk(qw), qblk(LANES), qblk(qw),
            whole(2 * LANES, t), whole(t, LANES), whole(LANES, t), whole(t, LANES),
            qblk(LANES),
            pl.BlockSpec((Q_BLOCK, N_WIDTH), lambda bi, i: (bi * nqb + i, C_NZ // N_WIDTH)),
        ],
        out_specs=pl.BlockSpec((Q_BLOCK, N_WIDTH), lambda bi, i: (bi * nqb + i, 0)),
        scratch_shapes=[pltpu.VMEM((g, rows, 2 * LANES), BF16), pltpu.VMEM((g, rows, LANES), F32),
                        pltpu.VMEM((g, rows, LANES), F32), pltpu.VMEM((2, g, rows, SEL_CHUNK), F32)],
        compiler_params=_cparams(("parallel", "arbitrary")),
        name="nsa",
    )(qrope, bias, oc, kst, vs, kwt, vw, gates, p)


def _outproj_kernel(final, ya_ref, yb_ref, ga_ref, gb_ref, x_ref, wa_ref, wb_ref, wo_ref, fg_ref, o_ref):
    a = jnp.dot(ya_ref[...], wa_ref[...], preferred_element_type=F32)
    bq = jnp.dot(yb_ref[...], wb_ref[...], preferred_element_type=F32)
    merged = _sigmoid(ga_ref[...].astype(F32)) * a + _sigmoid(gb_ref[...].astype(F32)) * bq
    y = x_ref[...] + jnp.dot(merged.astype(BF16), wo_ref[...], preferred_element_type=F32)
    if final:
        y = y * lax.rsqrt(jnp.mean(y * y, axis=-1, keepdims=True) + NORM_EPS) * fg_ref[...]
    o_ref[...] = y


def _outproj(ya, yb, p, x2, wa, wb, wo, fg, final):
    m = x2.shape[0]
    tm = OUTPROJ_TM

    def full(a):
        return pl.BlockSpec(a.shape, lambda i: (0, 0))

    return pl.pallas_call(
        functools.partial(_outproj_kernel, final),
        out_shape=jax.ShapeDtypeStruct((m, D_MODEL), F32),
        grid=(m // tm,),
        in_specs=[
            pl.BlockSpec((tm, M_WIDTH), lambda i: (i, 0)),
            pl.BlockSpec((tm, N_WIDTH), lambda i: (i, 0)),
            pl.BlockSpec((tm, D_MODEL), lambda i: (i, C_GA // D_MODEL)),
            pl.BlockSpec((tm, D_MODEL), lambda i: (i, C_GB // D_MODEL)),
            pl.BlockSpec((tm, D_MODEL), lambda i: (i, 0)),
            full(wa), full(wb), full(wo), full(fg),
        ],
        out_specs=pl.BlockSpec((tm, D_MODEL), lambda i: (i, 0)),
        compiler_params=_cparams(("parallel",)),
        name="outproj",
    )(ya, yb, p, p, x2, wa, wb, wo, fg)


def _expand_cmp_w1(w1k, w1v):
    half = CMP_BLOCK // 2
    k3 = w1k.astype(BF16).reshape(CMP_BLOCK, 1, N_HEAD_DIM, 1, CMP_HIDDEN)
    v3 = w1v.astype(BF16).reshape(CMP_BLOCK, 1, N_HEAD_DIM, 1, CMP_HIDDEN)
    nc = 2 * N_KV_GROUPS
    eye = jnp.eye(nc, dtype=BF16)
    is_k = (jnp.arange(nc) < N_KV_GROUPS).astype(BF16)
    mk = (eye * is_k[:, None]).reshape(1, nc, 1, nc, 1)
    mv = (eye * (1 - is_k)[:, None]).reshape(1, nc, 1, nc, 1)
    z = k3 * mk + v3 * mv
    z = z.reshape(CMP_BLOCK * nc * N_HEAD_DIM, nc * CMP_HIDDEN)
    return z[:half * nc * N_HEAD_DIM], z[half * nc * N_HEAD_DIM:]


def kernel(x, positions, norm_g, w_in, conv_q_w, conv_q_b, conv_k_w, conv_k_b, b_igate, b_fgate, mh_norm_g,
           cmp_pos_k, cmp_w1_k, cmp_w2_k, cmp_pos_v, cmp_w1_v, cmp_w2_v, b_nsa_gate, w_branch_a, w_branch_b,
           w_out, final_norm_g):
    b, t, _ = x.shape
    assert t % (2 * SEL_CHUNK) == 0 and t >= WINDOW + Q_BLOCK and t // SEL_BLOCK <= LANES and t % MLSTM_CHUNK == 0
    x2 = x.reshape(b * t, D_MODEL)
    for l in range(norm_g.shape[0]):
        w_p = _wprep(w_in, l)
        p, aux = _inproj(x2, norm_g[l][None, :], w_p, w_p)

        sm_bias = jnp.zeros((1, LANES), F32).at[0, 0:M_HEADS].set(b_igate[l]).at[0, M_HEADS:2 * M_HEADS].set(b_fgate[l])
        ya = _mlstm(p, aux, b, t, conv_q_w[l], conv_q_b[l][None, :], conv_k_w[l], conv_k_b[l][None, :], sm_bias,
                    mh_norm_g[l][None, :])

        nr = t // CMP_STRIDE
        wt, wb = _expand_cmp_w1(cmp_w1_k[l], cmp_w1_v[l])
        w2pad = lambda w: jnp.pad(w, ((0, 0), (0, LANES - N_HEAD_DIM))).astype(BF16)
        kc, vct = _compress(aux, b, t, wt, wb, cmp_pos_k[l].reshape(1, -1), cmp_pos_v[l].reshape(1, -1),
                            cmp_w1_k[l].astype(BF16), cmp_w1_v[l].astype(BF16), w2pad(cmp_w2_k[l]), w2pad(cmp_w2_v[l]))

        half = ROPE_DIM // 2
        inv = jnp.power(jnp.float32(ROPE_THETA), -jnp.arange(half, dtype=F32) * (2.0 / ROPE_DIM))
        inv_row = inv[:, None]
        pos_col = positions.astype(F32)[:, None, :]
        gate_bias = jnp.zeros((1, LANES), F32).at[0, SM_G_OFF:SM_G_OFF + 3 * N_HEADS].set(b_nsa_gate[l])
        qrawt, qrope, kst, vs, kwt, vw, gates, gates_t = _kvprep(p, aux, b, t, pos_col, inv_row, gate_bias)

        ci = np.arange(nr)[None, :] * CMP_STRIDE
        sj = np.arange(LANES)[:, None] * SEL_BLOCK
        overlap_t = jnp.asarray((ci < sj + SEL_BLOCK) & (ci + CMP_BLOCK > sj) & (np.arange(nr)[None, :] < nr - 1)
                                & (np.arange(LANES)[:, None] < t // SEL_BLOCK), dtype=BF16)
        oc, bias = _select(qrawt, kc, vct, overlap_t, gates_t)
        yb = _nsa(p, b, t, qrope, bias, oc, kst, vs, kwt, vw, gates)

        x2 = _outproj(ya, yb, p, x2, w_branch_a[l].astype(BF16), w_branch_b[l].astype(BF16),
                      w_out[l].astype(BF16), final_norm_g[None, :], l == norm_g.shape[0] - 1)
    return x2.reshape(b, t, D_MODEL)
```

```python
import functools
import math

import jax
import jax.numpy as jnp
import numpy as np
from jax import lax
from jax.experimental import pallas as pl
from jax.experimental.pallas import tpu as pltpu

F32 = jnp.float32
BF16 = jnp.bfloat16

D_MODEL = 1024
M_HEADS = 4
M_HEAD_DIM = 256
M_WIDTH = M_HEADS * M_HEAD_DIM
CONV_WIDTH = 4
N_HEADS = 8
N_KV_GROUPS = 2
N_HPG = N_HEADS // N_KV_GROUPS
N_HEAD_DIM = 64
N_WIDTH = N_HEADS * N_HEAD_DIM
CMP_BLOCK = 32
CMP_STRIDE = 16
CMP_HIDDEN = 128
SEL_BLOCK = 64
SEL_TOPK = 16
WINDOW = 512
Q_BLOCK = 256
ROPE_THETA = 500000.0
ROPE_DIM = N_HEAD_DIM // 4
NORM_EPS = 1e-6
NEG = -1e30
FORCE = 1e9

LANES = 128
SUBLANES = 8
MLSTM_CHUNK = 256
SEL_CHUNK = 512
SEL_UNROLL = 4
V7X_VMEM_BYTES = 64 * 1024 * 1024
VMEM_LIMIT = V7X_VMEM_BYTES - 8 * 1024 * 1024
INPROJ_VMEM_LIMIT = V7X_VMEM_BYTES - 4 * 1024 * 1024
INPROJ_TM, OUTPROJ_TM, KVPREP_TK, SELECT_QS = 2048, 1024, 1024, 2048
SM_F_OFF = M_HEADS
SM_G_OFF = 2 * M_HEADS
GATE_ROWS = -(-3 * N_HPG // SUBLANES) * SUBLANES

C_MQ, C_MK, C_MV, C_MO, C_MZ, C_GA, C_GB = 0, 1024, 2048, 3072, 4096, 5120, 6144
C_NQ, C_NZ = 7168, 7680
C_KV = 8192
C_CMP = 8704
C_SM = 8960
P_W = 9216
AUX_W = 3 * LANES
X_KC, X_VC, X_SM = 0, 1, 2
_O_MI, _O_NQ, _O_KC, _O_KS, _O_NG, _O_NZ, _O_GA, _O_END = 5120, 5128, 5640, 5896, 6408, 6432, 6944, 8992


def _cparams(sem, vmem_limit=VMEM_LIMIT):
    return pltpu.CompilerParams(dimension_semantics=sem, vmem_limit_bytes=vmem_limit)


def _sigmoid(x):
    return 0.5 * jnp.tanh(0.5 * x) + 0.5


def _silu(x):
    h = 0.5 * x
    return h + h * jnp.tanh(h)


def _log_sigmoid(x):
    return jnp.minimum(x, 0.0) - jnp.log1p(jnp.exp(-jnp.abs(x)))


PROJ_TN = 1536
AUX_TILE = C_CMP // PROJ_TN
AUX_OFF = C_CMP % PROJ_TN


N_HEAD_TILES = _O_MI // PROJ_TN


WPREP_COLS, WPREP_PIECE = 128, 512
_W_SLABS = ((0, 0, _O_MI), (C_GA, _O_GA, _O_END - _O_GA), (C_NQ, _O_NQ, _O_KC - _O_NQ), (C_NZ, _O_NZ, _O_GA - _O_NZ),
            (C_KV, _O_KS, _O_NG - _O_KS), (C_CMP, _O_KC, _O_KS - _O_KC))


def _wprep_kernel(w_ref, o_ref):
    for dst, src, width in _W_SLABS:
        for c in range(0, width, WPREP_PIECE):
            n = min(WPREP_PIECE, width - c)
            o_ref[dst + c:dst + c + n, :] = w_ref[src + c:src + c + n, :].astype(BF16)
    pad = jnp.zeros((P_W - C_SM - (_O_NQ - _O_MI) - (_O_NZ - _O_NG), o_ref.shape[1]), F32)
    sm = jnp.concatenate([w_ref[_O_MI:_O_NQ, :], w_ref[_O_NG:_O_NZ, :], pad], axis=0)
    o_ref[C_SM:P_W, :] = sm.astype(BF16)


def _wprep(w_t, l):
    assert all(v % (2 * SUBLANES) == 0 for s in _W_SLABS for v in (s[0], s[2])) and _O_NQ - _O_MI == SM_G_OFF
    assert all(s[1] % SUBLANES == 0 for s in _W_SLABS) and _O_NG % SUBLANES == 0 and _O_NZ % SUBLANES == 0
    return pl.pallas_call(
        _wprep_kernel,
        out_shape=jax.ShapeDtypeStruct((P_W, w_t.shape[2]), BF16),
        grid=(w_t.shape[2] // WPREP_COLS,),
        in_specs=[pl.BlockSpec((None, w_t.shape[1], WPREP_COLS), lambda i: (l, 0, i))],
        out_specs=pl.BlockSpec((P_W, WPREP_COLS), lambda i: (0, i)),
        compiler_params=_cparams(("parallel",)),
        name="wprep",
    )(w_t)


def _dot_nt(a, b):
    return lax.dot_general(a, b, (((1,), (1,)), ((), ())), preferred_element_type=F32)


def _inproj_kernel(x_ref, g_ref, wh_ref, wt_ref, o_ref, aux_ref, hn_ref):
    j = pl.program_id(1)

    @pl.when(j == 0)
    def _():
        x = x_ref[...]
        r = lax.rsqrt(jnp.mean(x * x, axis=-1, keepdims=True) + NORM_EPS)
        hn_ref[...] = (x * r * g_ref[...]).astype(BF16)

    @pl.when(j < N_HEAD_TILES)
    def _():
        o_ref[...] = _dot_nt(hn_ref[...], wh_ref[...]).astype(BF16)

    @pl.when((j >= N_HEAD_TILES) & (j != AUX_TILE))
    def _():
        o_ref[...] = _dot_nt(hn_ref[...], wt_ref[...]).astype(BF16)

    @pl.when(j == AUX_TILE)
    def _():
        acc = _dot_nt(hn_ref[...], wt_ref[...])
        o_ref[...] = acc.astype(BF16)
        aux_ref[...] = acc[:, AUX_OFF:AUX_OFF + AUX_W]


def _inproj(x2, norm_g, w_head, w_tail):
    m = x2.shape[0]
    tm, tn = INPROJ_TM, PROJ_TN
    return pl.pallas_call(
        _inproj_kernel,
        out_shape=(jax.ShapeDtypeStruct((m, P_W), BF16), jax.ShapeDtypeStruct((m, AUX_W), F32)),
        grid=(m // tm, P_W // tn),
        in_specs=[
            pl.BlockSpec((tm, D_MODEL), lambda i, j: (i, 0)),
            pl.BlockSpec((1, D_MODEL), lambda i, j: (0, 0)),
            pl.BlockSpec((tn, D_MODEL), lambda i, j: (jnp.minimum(j, N_HEAD_TILES - 1), 0)),
            pl.BlockSpec((tn, D_MODEL), lambda i, j: (jnp.maximum(j, N_HEAD_TILES), 0)),
        ],
        out_specs=(pl.BlockSpec((tm, tn), lambda i, j: (i, j)), pl.BlockSpec((tm, AUX_W), lambda i, j: (i, 0))),
        scratch_shapes=[pltpu.VMEM((tm, D_MODEL), BF16)],
        compiler_params=_cparams(("parallel", "arbitrary"), INPROJ_VMEM_LIMIT),
        name="inproj",
    )(x2, norm_g, w_head, w_tail)


def _causal_conv_silu(u16, prev_ref, w_ref, b_ref):
    n, w = u16.shape
    u = u16.astype(F32)
    row = lax.broadcasted_iota(jnp.int32, (n, n), 0)
    col = lax.broadcasted_iota(jnp.int32, (n, n), 1)
    sub = lax.broadcasted_iota(jnp.int32, (SUBLANES, w), 0)
    prev = prev_ref[...]
    y = b_ref[...] + u * w_ref[CONV_WIDTH - 1:CONV_WIDTH, :]
    for d in range(1, CONV_WIDTH):
        shift = jnp.where(row - col == d, 1.0, 0.0).astype(BF16)
        sh = jnp.dot(shift, u16, preferred_element_type=F32)
        head = jnp.where(sub < d, pltpu.roll(prev, d, axis=0), sh[0:SUBLANES, :])
        sh = jnp.concatenate([head, sh[SUBLANES:, :]], axis=0)
        y = y + sh * w_ref[CONV_WIDTH - 1 - d:CONV_WIDTH - d, :]
    prev_ref[...] = u[n - SUBLANES:n, :]
    return _silu(y)


def _mlstm_kernel(q_ref, k_ref, v_ref, og_ref, z_ref, sm_ref, cqw_ref, cqb_ref, ckw_ref, ckb_ref,
                  sb_ref, ng_ref, ya_ref, qprev, kprev, c_state, n_state, m_state):
    L = q_ref.shape[0]
    d = M_HEAD_DIM

    @pl.when(pl.program_id(1) == 0)
    def _():
        qprev[...] = jnp.zeros_like(qprev)
        kprev[...] = jnp.zeros_like(kprev)
        c_state[...] = jnp.zeros_like(c_state)
        n_state[...] = jnp.zeros_like(n_state)
        m_state[...] = jnp.zeros_like(m_state)

    mq = _causal_conv_silu(q_ref[...], qprev, cqw_ref, cqb_ref)
    mk = _causal_conv_silu(k_ref[...], kprev, ckw_ref, ckb_ref) * (d ** -0.5)

    sm = sm_ref[...] + sb_ref[...]
    lf = _log_sigmoid(sm)
    row = lax.broadcasted_iota(jnp.int32, (L, L), 0)
    col = lax.broadcasted_iota(jnp.int32, (L, L), 1)
    causal = row >= col
    tril = causal.astype(F32)
    triu = (row <= col).astype(F32)
    a_col = jnp.dot(tril, lf, precision=lax.Precision.HIGHEST, preferred_element_type=F32)
    sm_t = sm.T[0:SUBLANES, :]
    a_row = jnp.dot(_log_sigmoid(sm_t), triu, precision=lax.Precision.HIGHEST,
                    preferred_element_type=F32)

    for h in range(M_HEADS):
        hs = slice(h * d, (h + 1) * d)
        a_c = a_col[:, SM_F_OFF + h:SM_F_OFF + h + 1]
        li_c = sm[:, h:h + 1]
        a_r = a_row[SM_F_OFF + h:SM_F_OFF + h + 1, :]
        li_r = sm_t[h:h + 1, :]
        m_prev = m_state[h:h + 1, 0:1]
        a_end = a_c[L - 1:L, :]

        log_d = jnp.where(causal, a_c - (a_r - li_r), -jnp.inf)
        inter = a_c + m_prev
        m_t = jnp.maximum(inter, jnp.max(log_d, axis=1, keepdims=True))
        w_inter = jnp.exp(inter - m_t)
        dm = jnp.exp(log_d - m_t)

        qh = mq[:, hs]
        kh = mk[:, hs]
        qb = qh.astype(BF16)
        kb = kh.astype(BF16)
        vb = v_ref[:, hs]
        s = lax.dot_general(qb, kb, (((1,), (1,)), ((), ())), preferred_element_type=F32) * dm
        c_old = c_state[h]
        n_old = n_state[h:h + 1, :]
        num = (w_inter * jnp.dot(qb, c_old.astype(BF16), preferred_element_type=F32)
               + jnp.dot(s.astype(BF16), vb, preferred_element_type=F32))
        den = (w_inter * jnp.sum(qh * n_old, axis=1, keepdims=True)
               + jnp.sum(s, axis=1, keepdims=True))
        hh = num * (1.0 / jnp.maximum(jnp.abs(den), jnp.exp(-m_t)))

        logw = a_end - a_c + li_c
        m_new = jnp.maximum(a_end + m_prev, jnp.max(logw, axis=0, keepdims=True))
        wk = jnp.exp(logw - m_new)
        decay = jnp.exp(a_end + m_prev - m_new)
        kw = kh * wk
        c_state[h] = decay * c_old + lax.dot_general(kw.astype(BF16), vb, (((0,), (0,)), ((), ())),
                                                     preferred_element_type=F32)
        n_state[h:h + 1, :] = decay * n_old + jnp.sum(kw, axis=0, keepdims=True)
        m_state[h:h + 1, :] = jnp.broadcast_to(m_new, (1, LANES))

        mu = jnp.mean(hh, axis=1, keepdims=True)
        dc = hh - mu
        var = jnp.mean(dc * dc, axis=1, keepdims=True)
        hn = dc * lax.rsqrt(var + NORM_EPS) * ng_ref[:, hs]
        ya = _sigmoid(og_ref[:, hs].astype(F32)) * hn * _silu(z_ref[:, hs].astype(F32))
        ya_ref[:, hs] = ya.astype(BF16)


def _mlstm(p, aux, b, t, conv_q_w, conv_q_b, conv_k_w, conv_k_b, sm_bias, mh_norm_g):
    L = MLSTM_CHUNK
    nc = t // L

    def colspec(cstart):
        return pl.BlockSpec((L, M_WIDTH), lambda bi, c: (bi * nc + c, cstart // M_WIDTH))

    def full(shape):
        return pl.BlockSpec(shape, lambda bi, c: (0,) * len(shape))

    return pl.pallas_call(
        _mlstm_kernel,
        out_shape=jax.ShapeDtypeStruct((b * t, M_WIDTH), BF16),
        grid=(b, nc),
        in_specs=[
            colspec(C_MQ), colspec(C_MK), colspec(C_MV), colspec(C_MO), colspec(C_MZ),
            pl.BlockSpec((L, LANES), lambda bi, c: (bi * nc + c, X_SM)),
            full((CONV_WIDTH, M_WIDTH)), full((1, M_WIDTH)), full((CONV_WIDTH, M_WIDTH)), full((1, M_WIDTH)),
            full((1, LANES)), full((1, M_WIDTH)),
        ],
        out_specs=pl.BlockSpec((L, M_WIDTH), lambda bi, c: (bi * nc + c, 0)),
        scratch_shapes=[
            pltpu.VMEM((SUBLANES, M_WIDTH), F32), pltpu.VMEM((SUBLANES, M_WIDTH), F32),
            pltpu.VMEM((M_HEADS, M_HEAD_DIM, M_HEAD_DIM), F32),
            pltpu.VMEM((SUBLANES, M_HEAD_DIM), F32), pltpu.VMEM((SUBLANES, LANES), F32),
        ],
        compiler_params=_cparams(("parallel", "arbitrary")),
        name="mlstm",
    )(p, p, p, p, p, aux, conv_q_w, conv_q_b, conv_k_w, conv_k_b, sm_bias, mh_norm_g)


def _compress_kernel(pk_ref, pv_ref, wt_ref, wb_ref, posk_ref, posv_ref, w1k_ref, w1v_ref, w2k_ref, w2v_ref,
                     kc_ref, vct_ref):
    nr = pk_ref.shape[0] // CMP_STRIDE
    cw = 2 * LANES
    u = jnp.zeros((nr, wt_ref.shape[1]), F32)
    v = jnp.zeros((nr, wb_ref.shape[1]), F32)
    for l in range(CMP_STRIDE):
        x = jnp.concatenate([pk_ref[pl.ds(l, nr, stride=CMP_STRIDE), :],
                             pv_ref[pl.ds(l, nr, stride=CMP_STRIDE), :]], axis=1).astype(BF16)
        u = u + jnp.dot(x, wt_ref[l * cw:(l + 1) * cw, :], preferred_element_type=F32)
        v = v + jnp.dot(x, wb_ref[l * cw:(l + 1) * cw, :], preferred_element_type=F32)
    v = pltpu.roll(v, nr - 1, axis=0)
    rowi = lax.broadcasted_iota(jnp.int32, v.shape, 0)
    v = jnp.where(rowi < nr - 1, v, 0.0)

    def pos_term(pos_ref, w_ref):
        pb = jnp.broadcast_to(pos_ref[...].astype(BF16), (SUBLANES, CMP_BLOCK * N_HEAD_DIM))
        return jnp.dot(pb, w_ref[...], preferred_element_type=F32)[0:1, :]

    pk = pos_term(posk_ref, w1k_ref)
    pv = pos_term(posv_ref, w1v_ref)
    hid = u + v + jnp.concatenate([pk, pk, pv, pv], axis=1)
    act = 0.5 * hid * (1.0 + jnp.tanh(math.sqrt(2.0 / math.pi) * (hid + 0.044715 * (hid * hid * hid))))
    for c in range(2 * N_KV_GROUPS):
        w2 = w2k_ref[...] if c < N_KV_GROUPS else w2v_ref[...]
        out = jnp.dot(act[:, c * CMP_HIDDEN:(c + 1) * CMP_HIDDEN].astype(BF16), w2,
                      preferred_element_type=F32)
        if c < N_KV_GROUPS:
            kc_ref[c] = out.astype(BF16)
        else:
            out_t = out.T
            ones_row = lax.broadcasted_iota(jnp.int32, out_t.shape, 0) == N_HEAD_DIM
            vct_ref[c - N_KV_GROUPS] = jnp.where(ones_row, 1.0, out_t).astype(BF16)


def _compress(aux, b, t, wt, wb, posk, posv, w1k, w1v, w2k, w2v):
    nr = t // CMP_STRIDE
    g = N_KV_GROUPS

    def full(a):
        return pl.BlockSpec(a.shape, lambda bi: (0,) * a.ndim)

    return pl.pallas_call(
        _compress_kernel,
        out_shape=(jax.ShapeDtypeStruct((b, g, nr, LANES), BF16), jax.ShapeDtypeStruct((b, g, LANES, nr), BF16)),
        grid=(b,),
        in_specs=[pl.BlockSpec((t, LANES), lambda bi: (bi, X_KC)),
                  pl.BlockSpec((t, LANES), lambda bi: (bi, X_VC)),
                  full(wt), full(wb), full(posk), full(posv), full(w1k), full(w1v), full(w2k), full(w2v)],
        out_specs=(pl.BlockSpec((None, g, nr, LANES), lambda bi: (bi, 0, 0, 0)),
                   pl.BlockSpec((None, g, LANES, nr), lambda bi: (bi, 0, 0, 0))),
        compiler_params=_cparams(("parallel",)),
        name="compress",
    )(aux, aux, wt, wb, posk, posv, w1k, w1v, w2k, w2v)


def _kvprep_kernel(q_ref, kv_ref, sm_ref, pos_ref, inv_ref, gb_ref,
                   qrawt_ref, qrope_ref, kst_ref, vs_ref, kwt_ref, vw_ref, gates_ref, gatest_ref):
    tk = q_ref.shape[0]
    lane = lax.broadcasted_iota(jnp.int32, (tk, LANES), 1)
    half = ROPE_DIM // 2
    ang = inv_ref[...] * pos_ref[...]
    cos8 = jnp.cos(ang)
    sin8 = jnp.sin(ang)
    rest = N_HEAD_DIM - ROPE_DIM
    one_r, zero_r, zero_h = jnp.ones((rest, tk), F32), jnp.zeros((rest, tk), F32), jnp.zeros((half, tk), F32)
    heads = LANES // N_HEAD_DIM
    cos = jnp.concatenate([cos8, cos8, one_r] * heads, axis=0).T
    s_up = jnp.concatenate([-sin8, zero_h, zero_r] * heads, axis=0).T
    s_dn = jnp.concatenate([zero_h, sin8, zero_r] * heads, axis=0).T

    def rope(u):
        return u * cos + pltpu.roll(u, LANES - half, axis=1) * s_up + pltpu.roll(u, half, axis=1) * s_dn

    def head_pad(u, odd):
        if odd:
            u = pltpu.roll(u, N_HEAD_DIM, axis=1)
        return jnp.where(lane < N_HEAD_DIM, u, 0.0)

    scale = N_HEAD_DIM ** -0.5 * math.log2(math.e)
    zeros_t = jnp.zeros((N_HEAD_DIM, tk), BF16)
    for g in range(N_KV_GROUPS):
        for hp in range(N_HPG // 2):
            c0 = g * (N_HPG * N_HEAD_DIM) + hp * LANES
            u = q_ref[:, c0:c0 + LANES].astype(F32) * scale
            ur = rope(u)
            u_t = u.T.astype(BF16)
            for odd in range(2):
                h = 2 * hp + odd
                qrawt_ref[g, h * LANES:h * LANES + N_HEAD_DIM, :] = u_t[odd * N_HEAD_DIM:(odd + 1) * N_HEAD_DIM, :]
                qrawt_ref[g, h * LANES + N_HEAD_DIM:(h + 1) * LANES, :] = zeros_t
                qrope_ref[g, :, h * LANES:(h + 1) * LANES] = head_pad(ur, odd).astype(BF16)

    ks_t = rope(kv_ref[:, 0:LANES].astype(F32)).T
    kw_t = rope(kv_ref[:, 2 * LANES:3 * LANES].astype(F32)).T
    vs = kv_ref[:, LANES:2 * LANES].astype(F32)
    vw = kv_ref[:, 3 * LANES:4 * LANES].astype(F32)
    t_glob = pl.program_id(1) * tk + lax.broadcasted_iota(jnp.int32, (LANES, tk), 1)
    blk_row = lax.broadcasted_iota(jnp.int32, (LANES, tk), 0)
    onehot = jnp.where(blk_row == t_glob // SEL_BLOCK, 1.0, 0.0).astype(BF16)
    ones_lane = jnp.where(lane == N_HEAD_DIM, 1.0, 0.0)
    gates = _sigmoid(sm_ref[...] + gb_ref[...])
    for g in range(N_KV_GROUPS):
        rs = slice(g * N_HEAD_DIM, (g + 1) * N_HEAD_DIM)
        kst_ref[g, 0:LANES, :] = onehot
        kst_ref[g, LANES:LANES + N_HEAD_DIM, :] = ks_t[rs, :].astype(BF16)
        kst_ref[g, LANES + N_HEAD_DIM:2 * LANES, :] = zeros_t
        kwt_ref[g, 0:N_HEAD_DIM, :] = kw_t[rs, :].astype(BF16)
        kwt_ref[g, N_HEAD_DIM:LANES, :] = zeros_t
        vs_ref[g] = (head_pad(vs, g) + ones_lane).astype(BF16)
        vw_ref[g] = (head_pad(vw, g) + ones_lane).astype(BF16)
        goff = SM_G_OFF + g * (3 * N_HPG)
        gg = pltpu.roll(gates, LANES - goff, axis=1)
        gates_ref[g] = gg
        gatest_ref[g] = gg.T[0:GATE_ROWS, :]


def _kvprep(p, aux, b, t, pos_col, inv_row, gate_bias):
    tk = KVPREP_TK
    nt = t // tk
    g = N_KV_GROUPS
    qw = N_HPG * LANES

    def tok(shape):
        return pl.BlockSpec((None, g) + shape, lambda bi, i: (bi, 0, i, 0))

    def tok_t(rows):
        return pl.BlockSpec((None, g, rows, tk), lambda bi, i: (bi, 0, 0, i))

    return pl.pallas_call(
        _kvprep_kernel,
        out_shape=(
            jax.ShapeDtypeStruct((b, g, qw, t), BF16), jax.ShapeDtypeStruct((b, g, t, qw), BF16),
            jax.ShapeDtypeStruct((b, g, 2 * LANES, t), BF16), jax.ShapeDtypeStruct((b, g, t, LANES), BF16),
            jax.ShapeDtypeStruct((b, g, LANES, t), BF16), jax.ShapeDtypeStruct((b, g, t, LANES), BF16),
            jax.ShapeDtypeStruct((b, g, t, LANES), F32), jax.ShapeDtypeStruct((b, g, GATE_ROWS, t), F32),
        ),
        grid=(b, nt),
        in_specs=[
            pl.BlockSpec((tk, N_WIDTH), lambda bi, i: (bi * nt + i, C_NQ // N_WIDTH)),
            pl.BlockSpec((tk, 4 * LANES), lambda bi, i: (bi * nt + i, C_KV // (4 * LANES))),
            pl.BlockSpec((tk, LANES), lambda bi, i: (bi * nt + i, X_SM)),
            pl.BlockSpec((None, 1, tk), lambda bi, i: (bi, 0, i)),
            pl.BlockSpec(inv_row.shape, lambda bi, i: (0, 0)),
            pl.BlockSpec((1, LANES), lambda bi, i: (0, 0)),
        ],
        out_specs=(tok_t(qw), tok((tk, qw)), tok_t(2 * LANES), tok((tk, LANES)),
                   tok_t(LANES), tok((tk, LANES)), tok((tk, LANES)), tok_t(GATE_ROWS)),
        compiler_params=_cparams(("parallel", "arbitrary")),
        name="kvprep",
    )(p, p, aux, pos_col, inv_row, gate_bias)


CMP_ROW_CHUNK = 128


def _select_kernel(qt_ref, kc_ref, vct_ref, ovt_ref, gt_ref, oc_ref, bias_ref, imp_ref):
    qs = qt_ref.shape[1]
    nr = kc_ref.shape[0]
    t0 = pl.program_id(2) * qs
    sees_any = t0 + lax.broadcasted_iota(jnp.int32, (1, qs), 1) >= CMP_BLOCK - 1

    def cmp_branch(rows):
        m_c = (lax.broadcasted_iota(jnp.int32, (rows, qs), 0) * CMP_STRIDE + (CMP_BLOCK - 1)
               <= t0 + lax.broadcasted_iota(jnp.int32, (rows, qs), 1))
        kc = kc_ref[0:rows, :]
        vct = vct_ref[:, 0:rows]
        ovt = ovt_ref[:, 0:rows]
        imp = jnp.zeros((LANES, qs), F32)
        for h in range(N_HPG):
            s = jnp.dot(kc, qt_ref[h * LANES:(h + 1) * LANES, :], preferred_element_type=F32)
            s = jnp.where(m_c, s, NEG)
            e = jnp.exp2(s - jnp.max(s, axis=0, keepdims=True)).astype(BF16)
            r = jnp.dot(vct, e, preferred_element_type=F32)
            inv = jnp.where(sees_any, 1.0 / r[N_HEAD_DIM:N_HEAD_DIM + 1, :], 0.0)
            imp = imp + jnp.dot(ovt, e, preferred_element_type=F32) * inv
            oc_ref[:, h * LANES:(h + 1) * LANES] = (r * (inv * gt_ref[3 * h:3 * h + 1, :])).T.astype(BF16)
        imp_ref[...] = imp

    n_var = nr // CMP_ROW_CHUNK
    visible = (t0 + qs - CMP_BLOCK) // CMP_STRIDE + 1
    need = jnp.clip((visible + CMP_ROW_CHUNK - 1) // CMP_ROW_CHUNK, 1, n_var)
    for v in range(1, n_var + 1):
        pl.when(need == v)(functools.partial(cmp_branch, v * CMP_ROW_CHUNK))
    imp = imp_ref[...]

    blk = lax.broadcasted_iota(jnp.int32, (LANES, qs), 0)
    tq = t0 + lax.broadcasted_iota(jnp.int32, (LANES, qs), 1)
    cur = tq // SEL_BLOCK
    forced = (blk == 0) | (blk == cur) | (blk == cur - 1)
    n_sel = (nr * CMP_STRIDE) // SEL_BLOCK
    free_score = jnp.where(blk * SEL_BLOCK <= tq, imp, -1.0)
    blk_f = blk.astype(F32)
    n_top = min(SEL_TOPK, n_sel)
    n_forced = 3

    def pick(score, n_iter):
        score = jnp.where(blk < n_sel, score, -jnp.inf)
        for _ in range(n_iter):
            mx = jnp.max(score, axis=0, keepdims=True)
            idx = jnp.min(jnp.where(score == mx, blk_f, float(LANES)), axis=0, keepdims=True)
            score = jnp.where(blk_f == idx, -jnp.inf, score)
        bias_t = jnp.where((score == -jnp.inf) & (blk < n_sel), 0.0, NEG)
        bias_ref[...] = bias_t.T.astype(BF16)

    all_three_forced = (t0 >= 2 * SEL_BLOCK) & (n_top >= n_forced)

    @pl.when(all_three_forced)
    def _():
        pick(jnp.where(forced, -jnp.inf, free_score), n_top - n_forced)

    @pl.when(jnp.logical_not(all_three_forced))
    def _():
        pick(jnp.where(forced, FORCE, free_score), n_top)


def _select(qrawt, kc, vct, ovt, gates_t):
    b, g, qw, t = qrawt.shape
    nr = kc.shape[2]
    qs = SELECT_QS

    def whole(rows, cols):
        return pl.BlockSpec((None, None, rows, cols), lambda bi, gi, i: (bi, gi, 0, 0))

    return pl.pallas_call(
        _select_kernel,
        out_shape=(jax.ShapeDtypeStruct((b, g, t, qw), BF16), jax.ShapeDtypeStruct((b, g, t, LANES), BF16)),
        grid=(b, g, t // qs),
        in_specs=[
            pl.BlockSpec((None, None, qw, qs), lambda bi, gi, i: (bi, gi, 0, i)),
            whole(nr, LANES), whole(LANES, nr),
            pl.BlockSpec(ovt.shape, lambda bi, gi, i: (0, 0)),
            pl.BlockSpec((None, None, GATE_ROWS, qs), lambda bi, gi, i: (bi, gi, 0, i)),
        ],
        out_specs=(pl.BlockSpec((None, None, qs, qw), lambda bi, gi, i: (bi, gi, i, 0)),
                   pl.BlockSpec((None, None, qs, LANES), lambda bi, gi, i: (bi, gi, i, 0))),
        scratch_shapes=[pltpu.VMEM((LANES, qs), F32)],
        compiler_params=_cparams(("parallel", "parallel", "parallel")),
        name="select",
    )(qrawt, kc, vct, ovt, gates_t)


def _nsa_kernel(qrope_ref, bias_ref, oc_ref, kst_ref, vs_ref, kwt_ref, vw_ref, gates_ref, nz_ref,
                yb_ref, qaug_ref, acc_ref, m_ref, s_ref):
    i = pl.program_id(1)
    t0 = i * Q_BLOCK
    Q = Q_BLOCK
    nh = N_HPG
    ng = N_KV_GROUPS
    rows = nh * Q
    t0a = pl.multiple_of(t0, Q)
    causal = (lax.broadcasted_iota(jnp.int32, (Q, Q), 1) <= lax.broadcasted_iota(jnp.int32, (Q, Q), 0))
    lane = lax.broadcasted_iota(jnp.int32, (Q, LANES), 1)

    def q_rope(g):
        return jnp.concatenate([qrope_ref[g, :, h * LANES:(h + 1) * LANES] for h in range(nh)], axis=0)

    def mask_heads(s, m):
        return jnp.concatenate([jnp.where(m, s[h * Q:(h + 1) * Q, :], NEG) for h in range(nh)], axis=0)

    first_blk = t0 // SEL_BLOCK
    for g in range(ng):
        qr = q_rope(g)
        bias = bias_ref[g]
        bias_early = jnp.where(lane < first_blk, bias, NEG).astype(BF16)
        qaug_ref[g, :, 0:LANES] = jnp.concatenate([bias_early] * nh, axis=0)
        qaug_ref[g, :, LANES:2 * LANES] = qr
        q_diag = jnp.concatenate([jnp.concatenate([bias] * nh, axis=0), qr], axis=1)
        s_d = jnp.dot(q_diag, kst_ref[g, :, pl.ds(t0a, Q)], preferred_element_type=F32)
        s_d = mask_heads(s_d, causal)
        m0 = jnp.max(s_d, axis=1, keepdims=True)
        p_d = jnp.exp2(s_d - m0)
        m_ref[g] = jnp.broadcast_to(m0, (rows, LANES))
        acc_ref[g] = jnp.dot(p_d.astype(BF16), vs_ref[g, pl.ds(t0a, Q), :], preferred_element_type=F32)

    ncb = SEL_CHUNK // LANES
    n_chunks = (t0 + SEL_CHUNK - 1) // SEL_CHUNK
    last = jnp.maximum(n_chunks - 1, 0)

    def scores(g, c, slot):
        k0 = pl.multiple_of(c * SEL_CHUNK, SEL_CHUNK)
        s_ref[slot, g] = jnp.dot(qaug_ref[g], kst_ref[g, :, pl.ds(k0, SEL_CHUNK)], preferred_element_type=F32)

    def consume(g, c, slot):
        k0 = pl.multiple_of(c * SEL_CHUNK, SEL_CHUNK)
        s = s_ref[slot, g]
        m_old = m_ref[g]
        m_new = jnp.maximum(m_old, jnp.max(s, axis=1, keepdims=True))
        p = jnp.concatenate([jnp.exp2(s[:, cb * LANES:(cb + 1) * LANES] - m_new) for cb in range(ncb)], axis=1)
        acc_ref[g] = (jnp.exp2(m_old - m_new) * acc_ref[g]
                      + jnp.dot(p.astype(BF16), vs_ref[g, pl.ds(k0, SEL_CHUNK), :], preferred_element_type=F32))
        m_ref[g] = m_new

    for g in range(ng):
        scores(g, 0, 0)

    def steps(c, n):
        for u in range(n):
            for g in range(ng):
                scores(g, jnp.minimum(c + u + 1, last), (u + 1) % 2)
                consume(g, c + u, u % 2)

    def sel_quad(c4, carry):
        steps(SEL_UNROLL * c4, SEL_UNROLL)
        return carry

    n_quads = n_chunks // SEL_UNROLL
    lax.fori_loop(0, n_quads, sel_quad, 0)
    rem = n_chunks - SEL_UNROLL * n_quads

    @pl.when(rem >= 2)
    def _():
        steps(SEL_UNROLL * n_quads, 2)

    @pl.when(rem % 2 == 1)
    def _():
        for g in range(ng):
            consume(g, n_chunks - 1, 0)

    wlen = WINDOW + Q
    ws = pl.multiple_of(jnp.maximum(t0 - WINDOW, 0), Q)
    diff = (t0 + lax.broadcasted_iota(jnp.int32, (Q, wlen), 0)) - (ws + lax.broadcasted_iota(jnp.int32, (Q, wlen), 1))
    m_w = (diff >= 0) & (diff < WINDOW)
    for g in range(ng):
        gates = gates_ref[g]

        def gate_col(br):
            return jnp.concatenate([gates[:, 3 * h + br:3 * h + br + 1] for h in range(nh)], axis=0)

        acc = acc_ref[g]
        o_s = acc * (gate_col(1) / acc[:, N_HEAD_DIM:N_HEAD_DIM + 1])
        s_w = jnp.dot(q_rope(g), kwt_ref[g, :, pl.ds(ws, wlen)], preferred_element_type=F32)
        s_w = mask_heads(s_w, m_w)
        p_w = jnp.exp2(s_w - jnp.max(s_w, axis=1, keepdims=True))
        acc_w = jnp.dot(p_w.astype(BF16), vw_ref[g, pl.ds(ws, wlen), :], preferred_element_type=F32)
        o_w = acc_w * (gate_col(2) / acc_w[:, N_HEAD_DIM:N_HEAD_DIM + 1])
        outs = []
        for h in range(nh):
            rs = slice(h * Q, (h + 1) * Q)
            outs.append(oc_ref[g, :, h * LANES:(h + 1) * LANES].astype(F32) + o_s[rs, :] + o_w[rs, :])
        for hp in range(nh // 2):
            yn = jnp.where(lane < N_HEAD_DIM, outs[2 * hp], pltpu.roll(outs[2 * hp + 1], N_HEAD_DIM, axis=1))
            ls = slice((g * (nh // 2) + hp) * LANES, (g * (nh // 2) + hp + 1) * LANES)
            yb_ref[:, ls] = (yn * _silu(nz_ref[:, ls].astype(F32))).astype(BF16)


def _nsa(p, b, t, qrope, bias, oc, kst, vs, kwt, vw, gates):
    g = N_KV_GROUPS
    nqb = t // Q_BLOCK
    qw = N_HPG * LANES
    rows = N_HPG * Q_BLOCK

    def qblk(w):
        return pl.BlockSpec((None, g, Q_BLOCK, w), lambda bi, i: (bi, 0, i, 0))

    def whole(r, c):
        return pl.BlockSpec((None, g, r, c), lambda bi, i: (bi, 0, 0, 0), pipeline_mode=pl.Buffered(1))

    return pl.pallas_call(
        _nsa_kernel,
        out_shape=jax.ShapeDtypeStruct((b * t, N_WIDTH), BF16),
        grid=(b, nqb),
        in_specs=[
            qblk(qw), qblk(LANES), qblk(qw),
            whole(2 * LANES, t), whole(t, LANES), whole(LANES, t), whole(t, LANES),
            qblk(LANES),
            pl.BlockSpec((Q_BLOCK, N_WIDTH), lambda bi, i: (bi * nqb + i, C_NZ // N_WIDTH)),
        ],
        out_specs=pl.BlockSpec((Q_BLOCK, N_WIDTH), lambda bi, i: (bi * nqb + i, 0)),
        scratch_shapes=[pltpu.VMEM((g, rows, 2 * LANES), BF16), pltpu.VMEM((g, rows, LANES), F32),
                        pltpu.VMEM((g, rows, LANES), F32), pltpu.VMEM((2, g, rows, SEL_CHUNK), F32)],
        compiler_params=_cparams(("parallel", "arbitrary")),
        name="nsa",
    )(qrope, bias, oc, kst, vs, kwt, vw, gates, p)


def _outproj_kernel(final, ya_ref, yb_ref, ga_ref, gb_ref, x_ref, wa_ref, wb_ref, wo_ref, fg_ref, o_ref):
    a = jnp.dot(ya_ref[...], wa_ref[...], preferred_element_type=F32)
    bq = jnp.dot(yb_ref[...], wb_ref[...], preferred_element_type=F32)
    merged = _sigmoid(ga_ref[...].astype(F32)) * a + _sigmoid(gb_ref[...].astype(F32)) * bq
    y = x_ref[...] + jnp.dot(merged.astype(BF16), wo_ref[...], preferred_element_type=F32)
    if final:
        y = y * lax.rsqrt(jnp.mean(y * y, axis=-1, keepdims=True) + NORM_EPS) * fg_ref[...]
    o_ref[...] = y


def _outproj(ya, yb, p, x2, wa, wb, wo, fg, final):
    m = x2.shape[0]
    tm = OUTPROJ_TM

    def full(a):
        return pl.BlockSpec(a.shape, lambda i: (0, 0))

    return pl.pallas_call(
        functools.partial(_outproj_kernel, final),
        out_shape=jax.ShapeDtypeStruct((m, D_MODEL), F32),
        grid=(m // tm,),
        in_specs=[
            pl.BlockSpec((tm, M_WIDTH), lambda i: (i, 0)),
            pl.BlockSpec((tm, N_WIDTH), lambda i: (i, 0)),
            pl.BlockSpec((tm, D_MODEL), lambda i: (i, C_GA // D_MODEL)),
            pl.BlockSpec((tm, D_MODEL), lambda i: (i, C_GB // D_MODEL)),
            pl.BlockSpec((tm, D_MODEL), lambda i: (i, 0)),
            full(wa), full(wb), full(wo), full(fg),
        ],
        out_specs=pl.BlockSpec((tm, D_MODEL), lambda i: (i, 0)),
        compiler_params=_cparams(("parallel",)),
        name="outproj",
    )(ya, yb, p, p, x2, wa, wb, wo, fg)


def _expand_cmp_w1(w1k, w1v):
    half = CMP_BLOCK // 2
    k3 = w1k.astype(BF16).reshape(CMP_BLOCK, 1, N_HEAD_DIM, 1, CMP_HIDDEN)
    v3 = w1v.astype(BF16).reshape(CMP_BLOCK, 1, N_HEAD_DIM, 1, CMP_HIDDEN)
    nc = 2 * N_KV_GROUPS
    eye = jnp.eye(nc, dtype=BF16)
    is_k = (jnp.arange(nc) < N_KV_GROUPS).astype(BF16)
    mk = (eye * is_k[:, None]).reshape(1, nc, 1, nc, 1)
    mv = (eye * (1 - is_k)[:, None]).reshape(1, nc, 1, nc, 1)
    z = k3 * mk + v3 * mv
    z = z.reshape(CMP_BLOCK * nc * N_HEAD_DIM, nc * CMP_HIDDEN)
    return z[:half * nc * N_HEAD_DIM], z[half * nc * N_HEAD_DIM:]


def kernel(x, positions, norm_g, w_in, conv_q_w, conv_q_b, conv_k_w, conv_k_b, b_igate, b_fgate, mh_norm_g,
           cmp_pos_k, cmp_w1_k, cmp_w2_k, cmp_pos_v, cmp_w1_v, cmp_w2_v, b_nsa_gate, w_branch_a, w_branch_b,
           w_out, final_norm_g):
    b, t, _ = x.shape
    assert t % (2 * SEL_CHUNK) == 0 and t >= WINDOW + Q_BLOCK and t // SEL_BLOCK <= LANES and t % MLSTM_CHUNK == 0
    x2 = x.reshape(b * t, D_MODEL)
    for l in range(norm_g.shape[0]):
        w_p = _wprep(jnp.swapaxes(w_in, 1, 2), l)
        p, aux = _inproj(x2, norm_g[l][None, :], w_p, w_p)

        sm_bias = jnp.zeros((1, LANES), F32).at[0, 0:M_HEADS].set(b_igate[l]).at[0, M_HEADS:2 * M_HEADS].set(b_fgate[l])
        ya = _mlstm(p, aux, b, t, conv_q_w[l], conv_q_b[l][None, :], conv_k_w[l], conv_k_b[l][None, :], sm_bias,
                    mh_norm_g[l][None, :])

        nr = t // CMP_STRIDE
        wt, wb = _expand_cmp_w1(cmp_w1_k[l], cmp_w1_v[l])
        w2pad = lambda w: jnp.pad(w, ((0, 0), (0, LANES - N_HEAD_DIM))).astype(BF16)
        kc, vct = _compress(aux, b, t, wt, wb, cmp_pos_k[l].reshape(1, -1), cmp_pos_v[l].reshape(1, -1),
                            cmp_w1_k[l].astype(BF16), cmp_w1_v[l].astype(BF16), w2pad(cmp_w2_k[l]), w2pad(cmp_w2_v[l]))

        half = ROPE_DIM // 2
        inv = jnp.power(jnp.float32(ROPE_THETA), -jnp.arange(half, dtype=F32) * (2.0 / ROPE_DIM))
        inv_row = inv[:, None]
        pos_col = positions.astype(F32)[:, None, :]
        gate_bias = jnp.zeros((1, LANES), F32).at[0, SM_G_OFF:SM_G_OFF + 3 * N_HEADS].set(b_nsa_gate[l])
        qrawt, qrope, kst, vs, kwt, vw, gates, gates_t = _kvprep(p, aux, b, t, pos_col, inv_row, gate_bias)

        ci = np.arange(nr)[None, :] * CMP_STRIDE
        sj = np.arange(LANES)[:, None] * SEL_BLOCK
        overlap_t = jnp.asarray((ci < sj + SEL_BLOCK) & (ci + CMP_BLOCK > sj) & (np.arange(nr)[None, :] < nr - 1)
                                & (np.arange(LANES)[:, None] < t // SEL_BLOCK), dtype=BF16)
        oc, bias = _select(qrawt, kc, vct, overlap_t, gates_t)
        yb = _nsa(p, b, t, qrope, bias, oc, kst, vs, kwt, vw, gates)

        x2 = _outproj(ya, yb, p, x2, w_branch_a[l].astype(BF16), w_branch_b[l].astype(BF16),
                      w_out[l].astype(BF16), final_norm_g[None, :], l == norm_g.shape[0] - 1)
    return x2.reshape(b, t, D_MODEL)
```

```python
import functools
import math

import jax
import jax.numpy as jnp
import numpy as np
from jax import lax
from jax.experimental import pallas as pl
from jax.experimental.pallas import tpu as pltpu

F32 = jnp.float32
BF16 = jnp.bfloat16

D_MODEL = 1024
M_HEADS = 4
M_HEAD_DIM = 256
M_WIDTH = M_HEADS * M_HEAD_DIM
CONV_WIDTH = 4
N_HEADS = 8
N_KV_GROUPS = 2
N_HPG = N_HEADS // N_KV_GROUPS
N_HEAD_DIM = 64
N_WIDTH = N_HEADS * N_HEAD_DIM
CMP_BLOCK = 32
CMP_STRIDE = 16
CMP_HIDDEN = 128
SEL_BLOCK = 64
SEL_TOPK = 16
WINDOW = 512
Q_BLOCK = 256
ROPE_THETA = 500000.0
ROPE_DIM = N_HEAD_DIM // 4
NORM_EPS = 1e-6
NEG = -1e30
FORCE = 1e9

LANES = 128
SUBLANES = 8
MLSTM_CHUNK = 256
SEL_CHUNK = 512
SEL_UNROLL = 4
V7X_VMEM_BYTES = 64 * 1024 * 1024
VMEM_LIMIT = V7X_VMEM_BYTES - 8 * 1024 * 1024
INPROJ_VMEM_LIMIT = V7X_VMEM_BYTES - 4 * 1024 * 1024
INPROJ_TM, OUTPROJ_TM, KVPREP_TK, SELECT_QS = 2048, 1024, 1024, 2048
SM_F_OFF = M_HEADS
SM_G_OFF = 2 * M_HEADS
GATE_ROWS = -(-3 * N_HPG // SUBLANES) * SUBLANES

C_MQ, C_MK, C_MV, C_MO, C_MZ, C_GA, C_GB = 0, 1024, 2048, 3072, 4096, 5120, 6144
C_NQ, C_NZ = 7168, 7680
C_KV = 8192
C_CMP = 8704
C_SM = 8960
P_W = 9216
AUX_W = 3 * LANES
X_KC, X_VC, X_SM = 0, 1, 2
_O_MI, _O_NQ, _O_KC, _O_KS, _O_NG, _O_NZ, _O_GA, _O_END = 5120, 5128, 5640, 5896, 6408, 6432, 6944, 8992


def _cparams(sem, vmem_limit=VMEM_LIMIT):
    return pltpu.CompilerParams(dimension_semantics=sem, vmem_limit_bytes=vmem_limit)


def _sigmoid(x):
    return 0.5 * jnp.tanh(0.5 * x) + 0.5


def _silu(x):
    h = 0.5 * x
    return h + h * jnp.tanh(h)


def _log_sigmoid(x):
    return jnp.minimum(x, 0.0) - jnp.log1p(jnp.exp(-jnp.abs(x)))


PROJ_TN = 1536
AUX_TILE = C_CMP // PROJ_TN
AUX_OFF = C_CMP % PROJ_TN


N_HEAD_TILES = _O_MI // PROJ_TN


WPREP_COLS, WPREP_PIECE = 128, 512
_W_SLABS = ((0, 0, _O_MI), (C_GA, _O_GA, _O_END - _O_GA), (C_NQ, _O_NQ, _O_KC - _O_NQ), (C_NZ, _O_NZ, _O_GA - _O_NZ),
            (C_KV, _O_KS, _O_NG - _O_KS), (C_CMP, _O_KC, _O_KS - _O_KC))


def _wprep_kernel(w_ref, o_ref):
    for dst, src, width in _W_SLABS:
        for c in range(0, width, WPREP_PIECE):
            n = min(WPREP_PIECE, width - c)
            o_ref[dst + c:dst + c + n, :] = w_ref[src + c:src + c + n, :].astype(BF16)
    pad = jnp.zeros((P_W - C_SM - (_O_NQ - _O_MI) - (_O_NZ - _O_NG), o_ref.shape[1]), F32)
    sm = jnp.concatenate([w_ref[_O_MI:_O_NQ, :], w_ref[_O_NG:_O_NZ, :], pad], axis=0)
    o_ref[C_SM:P_W, :] = sm.astype(BF16)


def _wprep(w_t, l):
    assert all(v % (2 * SUBLANES) == 0 for s in _W_SLABS for v in (s[0], s[2])) and _O_NQ - _O_MI == SM_G_OFF
    assert all(s[1] % SUBLANES == 0 for s in _W_SLABS) and _O_NG % SUBLANES == 0 and _O_NZ % SUBLANES == 0
    return pl.pallas_call(
        _wprep_kernel,
        out_shape=jax.ShapeDtypeStruct((P_W, w_t.shape[2]), BF16),
        grid=(w_t.shape[2] // WPREP_COLS,),
        in_specs=[pl.BlockSpec((None, w_t.shape[1], WPREP_COLS), lambda i: (l, 0, i))],
        out_specs=pl.BlockSpec((P_W, WPREP_COLS), lambda i: (0, i)),
        compiler_params=_cparams(("parallel",)),
        name="wprep",
    )(w_t)


def _dot_nt(a, b):
    return lax.dot_general(a, b, (((1,), (1,)), ((), ())), preferred_element_type=F32)


def _inproj_kernel(x_ref, g_ref, wh_ref, wt_ref, o_ref, aux_ref, hn_ref):
    j = pl.program_id(1)

    @pl.when(j == 0)
    def _():
        x = x_ref[...]
        r = lax.rsqrt(jnp.mean(x * x, axis=-1, keepdims=True) + NORM_EPS)
        hn_ref[...] = (x * r * g_ref[...]).astype(BF16)

    @pl.when(j < N_HEAD_TILES)
    def _():
        o_ref[...] = _dot_nt(hn_ref[...], wh_ref[...]).astype(BF16)

    @pl.when((j >= N_HEAD_TILES) & (j != AUX_TILE))
    def _():
        o_ref[...] = _dot_nt(hn_ref[...], wt_ref[...]).astype(BF16)

    @pl.when(j == AUX_TILE)
    def _():
        acc = _dot_nt(hn_ref[...], wt_ref[...])
        o_ref[...] = acc.astype(BF16)
        aux_ref[...] = acc[:, AUX_OFF:AUX_OFF + AUX_W]


def _inproj(x2, norm_g, w_head, w_tail):
    m = x2.shape[0]
    tm, tn = INPROJ_TM, PROJ_TN
    return pl.pallas_call(
        _inproj_kernel,
        out_shape=(jax.ShapeDtypeStruct((m, P_W), BF16), jax.ShapeDtypeStruct((m, AUX_W), F32)),
        grid=(m // tm, P_W // tn),
        in_specs=[
            pl.BlockSpec((tm, D_MODEL), lambda i, j: (i, 0)),
            pl.BlockSpec((1, D_MODEL), lambda i, j: (0, 0)),
            pl.BlockSpec((tn, D_MODEL), lambda i, j: (jnp.minimum(j, N_HEAD_TILES - 1), 0)),
            pl.BlockSpec((tn, D_MODEL), lambda i, j: (jnp.maximum(j, N_HEAD_TILES), 0)),
        ],
        out_specs=(pl.BlockSpec((tm, tn), lambda i, j: (i, j)), pl.BlockSpec((tm, AUX_W), lambda i, j: (i, 0))),
        scratch_shapes=[pltpu.VMEM((tm, D_MODEL), BF16)],
        compiler_params=_cparams(("parallel", "arbitrary"), INPROJ_VMEM_LIMIT),
        name="inproj",
    )(x2, norm_g, w_head, w_tail)


def _causal_conv_silu(u16, prev_ref, w_ref, b_ref):
    n, w = u16.shape
    u = u16.astype(F32)
    row = lax.broadcasted_iota(jnp.int32, (n, n), 0)
    col = lax.broadcasted_iota(jnp.int32, (n, n), 1)
    sub = lax.broadcasted_iota(jnp.int32, (SUBLANES, w), 0)
    prev = prev_ref[...]
    y = b_ref[...] + u * w_ref[CONV_WIDTH - 1:CONV_WIDTH, :]
    for d in range(1, CONV_WIDTH):
        shift = jnp.where(row - col == d, 1.0, 0.0).astype(BF16)
        sh = jnp.dot(shift, u16, preferred_element_type=F32)
        head = jnp.where(sub < d, pltpu.roll(prev, d, axis=0), sh[0:SUBLANES, :])
        sh = jnp.concatenate([head, sh[SUBLANES:, :]], axis=0)
        y = y + sh * w_ref[CONV_WIDTH - 1 - d:CONV_WIDTH - d, :]
    prev_ref[...] = u[n - SUBLANES:n, :]
    return _silu(y)


def _mlstm_kernel(q_ref, k_ref, v_ref, og_ref, z_ref, sm_ref, cqw_ref, cqb_ref, ckw_ref, ckb_ref,
                  sb_ref, ng_ref, ya_ref, qprev, kprev, c_state, n_state, m_state):
    L = q_ref.shape[0]
    d = M_HEAD_DIM

    @pl.when(pl.program_id(1) == 0)
    def _():
        qprev[...] = jnp.zeros_like(qprev)
        kprev[...] = jnp.zeros_like(kprev)
        c_state[...] = jnp.zeros_like(c_state)
        n_state[...] = jnp.zeros_like(n_state)
        m_state[...] = jnp.zeros_like(m_state)

    mq = _causal_conv_silu(q_ref[...], qprev, cqw_ref, cqb_ref)
    mk = _causal_conv_silu(k_ref[...], kprev, ckw_ref, ckb_ref) * (d ** -0.5)

    sm = sm_ref[...] + sb_ref[...]
    lf = _log_sigmoid(sm)
    row = lax.broadcasted_iota(jnp.int32, (L, L), 0)
    col = lax.broadcasted_iota(jnp.int32, (L, L), 1)
    causal = row >= col
    tril = causal.astype(F32)
    triu = (row <= col).astype(F32)
    a_col = jnp.dot(tril, lf, precision=lax.Precision.HIGHEST, preferred_element_type=F32)
    sm_t = sm.T[0:SUBLANES, :]
    a_row = jnp.dot(_log_sigmoid(sm_t), triu, precision=lax.Precision.HIGHEST,
                    preferred_element_type=F32)

    for h in range(M_HEADS):
        hs = slice(h * d, (h + 1) * d)
        a_c = a_col[:, SM_F_OFF + h:SM_F_OFF + h + 1]
        li_c = sm[:, h:h + 1]
        a_r = a_row[SM_F_OFF + h:SM_F_OFF + h + 1, :]
        li_r = sm_t[h:h + 1, :]
        m_prev = m_state[h:h + 1, 0:1]
        a_end = a_c[L - 1:L, :]

        log_d = jnp.where(causal, a_c - (a_r - li_r), -jnp.inf)
        inter = a_c + m_prev
        m_t = jnp.maximum(inter, jnp.max(log_d, axis=1, keepdims=True))
        w_inter = jnp.exp(inter - m_t)
        dm = jnp.exp(log_d - m_t)

        qh = mq[:, hs]
        kh = mk[:, hs]
        qb = qh.astype(BF16)
        kb = kh.astype(BF16)
        vb = v_ref[:, hs]
        s = lax.dot_general(qb, kb, (((1,), (1,)), ((), ())), preferred_element_type=F32) * dm
        c_old = c_state[h]
        n_old = n_state[h:h + 1, :]
        num = (w_inter * jnp.dot(qb, c_old.astype(BF16), preferred_element_type=F32)
               + jnp.dot(s.astype(BF16), vb, preferred_element_type=F32))
        den = (w_inter * jnp.sum(qh * n_old, axis=1, keepdims=True)
               + jnp.sum(s, axis=1, keepdims=True))
        hh = num * (1.0 / jnp.maximum(jnp.abs(den), jnp.exp(-m_t)))

        logw = a_end - a_c + li_c
        m_new = jnp.maximum(a_end + m_prev, jnp.max(logw, axis=0, keepdims=True))
        wk = jnp.exp(logw - m_new)
        decay = jnp.exp(a_end + m_prev - m_new)
        kw = kh * wk
        c_state[h] = decay * c_old + lax.dot_general(kw.astype(BF16), vb, (((0,), (0,)), ((), ())),
                                                     preferred_element_type=F32)
        n_state[h:h + 1, :] = decay * n_old + jnp.sum(kw, axis=0, keepdims=True)
        m_state[h:h + 1, :] = jnp.broadcast_to(m_new, (1, LANES))

        mu = jnp.mean(hh, axis=1, keepdims=True)
        dc = hh - mu
        var = jnp.mean(dc * dc, axis=1, keepdims=True)
        hn = dc * lax.rsqrt(var + NORM_EPS) * ng_ref[:, hs]
        ya = _sigmoid(og_ref[:, hs].astype(F32)) * hn * _silu(z_ref[:, hs].astype(F32))
        ya_ref[:, hs] = ya.astype(BF16)


def _mlstm(p, aux, b, t, conv_q_w, conv_q_b, conv_k_w, conv_k_b, sm_bias, mh_norm_g):
    L = MLSTM_CHUNK
    nc = t // L

    def colspec(cstart):
        return pl.BlockSpec((L, M_WIDTH), lambda bi, c: (bi * nc + c, cstart // M_WIDTH))

    def full(shape):
        return pl.BlockSpec(shape, lambda bi, c: (0,) * len(shape))

    return pl.pallas_call(
        _mlstm_kernel,
        out_shape=jax.ShapeDtypeStruct((b * t, M_WIDTH), BF16),
        grid=(b, nc),
        in_specs=[
            colspec(C_MQ), colspec(C_MK), colspec(C_MV), colspec(C_MO), colspec(C_MZ),
            pl.BlockSpec((L, LANES), lambda bi, c: (bi * nc + c, X_SM)),
            full((CONV_WIDTH, M_WIDTH)), full((1, M_WIDTH)), full((CONV_WIDTH, M_WIDTH)), full((1, M_WIDTH)),
            full((1, LANES)), full((1, M_WIDTH)),
        ],
        out_specs=pl.BlockSpec((L, M_WIDTH), lambda bi, c: (bi * nc + c, 0)),
        scratch_shapes=[
            pltpu.VMEM((SUBLANES, M_WIDTH), F32), pltpu.VMEM((SUBLANES, M_WIDTH), F32),
            pltpu.VMEM((M_HEADS, M_HEAD_DIM, M_HEAD_DIM), F32),
            pltpu.VMEM((SUBLANES, M_HEAD_DIM), F32), pltpu.VMEM((SUBLANES, LANES), F32),
        ],
        compiler_params=_cparams(("parallel", "arbitrary")),
        name="mlstm",
    )(p, p, p, p, p, aux, conv_q_w, conv_q_b, conv_k_w, conv_k_b, sm_bias, mh_norm_g)


def _compress_kernel(pk_ref, pv_ref, wt_ref, wb_ref, posk_ref, posv_ref, w1k_ref, w1v_ref, w2k_ref, w2v_ref,
                     kc_ref, vct_ref):
    nr = pk_ref.shape[0] // CMP_STRIDE
    cw = 2 * LANES
    u = jnp.zeros((nr, wt_ref.shape[1]), F32)
    v = jnp.zeros((nr, wb_ref.shape[1]), F32)
    for l in range(CMP_STRIDE):
        x = jnp.concatenate([pk_ref[pl.ds(l, nr, stride=CMP_STRIDE), :],
                             pv_ref[pl.ds(l, nr, stride=CMP_STRIDE), :]], axis=1).astype(BF16)
        u = u + jnp.dot(x, wt_ref[l * cw:(l + 1) * cw, :], preferred_element_type=F32)
        v = v + jnp.dot(x, wb_ref[l * cw:(l + 1) * cw, :], preferred_element_type=F32)
    v = pltpu.roll(v, nr - 1, axis=0)
    rowi = lax.broadcasted_iota(jnp.int32, v.shape, 0)
    v = jnp.where(rowi < nr - 1, v, 0.0)

    def pos_term(pos_ref, w_ref):
        pb = jnp.broadcast_to(pos_ref[...].astype(BF16), (SUBLANES, CMP_BLOCK * N_HEAD_DIM))
        return jnp.dot(pb, w_ref[...], preferred_element_type=F32)[0:1, :]

    pk = pos_term(posk_ref, w1k_ref)
    pv = pos_term(posv_ref, w1v_ref)
    hid = u + v + jnp.concatenate([pk, pk, pv, pv], axis=1)
    act = 0.5 * hid * (1.0 + jnp.tanh(math.sqrt(2.0 / math.pi) * (hid + 0.044715 * (hid * hid * hid))))
    for c in range(2 * N_KV_GROUPS):
        w2 = w2k_ref[...] if c < N_KV_GROUPS else w2v_ref[...]
        out = jnp.dot(act[:, c * CMP_HIDDEN:(c + 1) * CMP_HIDDEN].astype(BF16), w2,
                      preferred_element_type=F32)
        if c < N_KV_GROUPS:
            kc_ref[c] = out.astype(BF16)
        else:
            out_t = out.T
            ones_row = lax.broadcasted_iota(jnp.int32, out_t.shape, 0) == N_HEAD_DIM
            vct_ref[c - N_KV_GROUPS] = jnp.where(ones_row, 1.0, out_t).astype(BF16)


def _compress(aux, b, t, wz, posk, posv, w1k, w1v, w2k, w2v):
    nr = t // CMP_STRIDE
    g = N_KV_GROUPS
    hz = wz.shape[0] // 2

    def full(a):
        return pl.BlockSpec(a.shape, lambda bi: (0,) * a.ndim)

    return pl.pallas_call(
        _compress_kernel,
        out_shape=(jax.ShapeDtypeStruct((b, g, nr, LANES), BF16), jax.ShapeDtypeStruct((b, g, LANES, nr), BF16)),
        grid=(b,),
        in_specs=[pl.BlockSpec((t, LANES), lambda bi: (bi, X_KC)),
                  pl.BlockSpec((t, LANES), lambda bi: (bi, X_VC)),
                  pl.BlockSpec((hz, wz.shape[1]), lambda bi: (0, 0)), pl.BlockSpec((hz, wz.shape[1]), lambda bi: (1, 0)),
                  full(posk), full(posv), full(w1k), full(w1v), full(w2k), full(w2v)],
        out_specs=(pl.BlockSpec((None, g, nr, LANES), lambda bi: (bi, 0, 0, 0)),
                   pl.BlockSpec((None, g, LANES, nr), lambda bi: (bi, 0, 0, 0))),
        compiler_params=_cparams(("parallel",)),
        name="compress",
    )(aux, aux, wz, wz, posk, posv, w1k, w1v, w2k, w2v)


def _kvprep_kernel(q_ref, kv_ref, sm_ref, pos_ref, inv_ref, gb_ref,
                   qrawt_ref, qrope_ref, kst_ref, vs_ref, kwt_ref, vw_ref, gates_ref, gatest_ref):
    tk = q_ref.shape[0]
    lane = lax.broadcasted_iota(jnp.int32, (tk, LANES), 1)
    half = ROPE_DIM // 2
    ang = inv_ref[...] * pos_ref[...]
    cos8 = jnp.cos(ang)
    sin8 = jnp.sin(ang)
    rest = N_HEAD_DIM - ROPE_DIM
    one_r, zero_r, zero_h = jnp.ones((rest, tk), F32), jnp.zeros((rest, tk), F32), jnp.zeros((half, tk), F32)
    heads = LANES // N_HEAD_DIM
    cos = jnp.concatenate([cos8, cos8, one_r] * heads, axis=0).T
    s_up = jnp.concatenate([-sin8, zero_h, zero_r] * heads, axis=0).T
    s_dn = jnp.concatenate([zero_h, sin8, zero_r] * heads, axis=0).T

    def rope(u):
        return u * cos + pltpu.roll(u, LANES - half, axis=1) * s_up + pltpu.roll(u, half, axis=1) * s_dn

    def head_pad(u, odd):
        if odd:
            u = pltpu.roll(u, N_HEAD_DIM, axis=1)
        return jnp.where(lane < N_HEAD_DIM, u, 0.0)

    scale = N_HEAD_DIM ** -0.5 * math.log2(math.e)
    zeros_t = jnp.zeros((N_HEAD_DIM, tk), BF16)
    for g in range(N_KV_GROUPS):
        for hp in range(N_HPG // 2):
            c0 = g * (N_HPG * N_HEAD_DIM) + hp * LANES
            u = q_ref[:, c0:c0 + LANES].astype(F32) * scale
            ur = rope(u)
            u_t = u.T.astype(BF16)
            for odd in range(2):
                h = 2 * hp + odd
                qrawt_ref[g, h * LANES:h * LANES + N_HEAD_DIM, :] = u_t[odd * N_HEAD_DIM:(odd + 1) * N_HEAD_DIM, :]
                qrawt_ref[g, h * LANES + N_HEAD_DIM:(h + 1) * LANES, :] = zeros_t
                qrope_ref[g, :, h * LANES:(h + 1) * LANES] = head_pad(ur, odd).astype(BF16)

    ks_t = rope(kv_ref[:, 0:LANES].astype(F32)).T
    kw_t = rope(kv_ref[:, 2 * LANES:3 * LANES].astype(F32)).T
    vs = kv_ref[:, LANES:2 * LANES].astype(F32)
    vw = kv_ref[:, 3 * LANES:4 * LANES].astype(F32)
    t_glob = pl.program_id(1) * tk + lax.broadcasted_iota(jnp.int32, (LANES, tk), 1)
    blk_row = lax.broadcasted_iota(jnp.int32, (LANES, tk), 0)
    onehot = jnp.where(blk_row == t_glob // SEL_BLOCK, 1.0, 0.0).astype(BF16)
    ones_lane = jnp.where(lane == N_HEAD_DIM, 1.0, 0.0)
    gates = _sigmoid(sm_ref[...] + gb_ref[...])
    for g in range(N_KV_GROUPS):
        rs = slice(g * N_HEAD_DIM, (g + 1) * N_HEAD_DIM)
        kst_ref[g, 0:LANES, :] = onehot
        kst_ref[g, LANES:LANES + N_HEAD_DIM, :] = ks_t[rs, :].astype(BF16)
        kst_ref[g, LANES + N_HEAD_DIM:2 * LANES, :] = zeros_t
        kwt_ref[g, 0:N_HEAD_DIM, :] = kw_t[rs, :].astype(BF16)
        kwt_ref[g, N_HEAD_DIM:LANES, :] = zeros_t
        vs_ref[g] = (head_pad(vs, g) + ones_lane).astype(BF16)
        vw_ref[g] = (head_pad(vw, g) + ones_lane).astype(BF16)
        goff = SM_G_OFF + g * (3 * N_HPG)
        gg = pltpu.roll(gates, LANES - goff, axis=1)
        gates_ref[g] = gg
        gatest_ref[g] = gg.T[0:GATE_ROWS, :]


def _kvprep(p, aux, b, t, pos_col, inv_row, gate_bias):
    tk = KVPREP_TK
    nt = t // tk
    g = N_KV_GROUPS
    qw = N_HPG * LANES

    def tok(shape):
        return pl.BlockSpec((None, g) + shape, lambda bi, i: (bi, 0, i, 0))

    def tok_t(rows):
        return pl.BlockSpec((None, g, rows, tk), lambda bi, i: (bi, 0, 0, i))

    return pl.pallas_call(
        _kvprep_kernel,
        out_shape=(
            jax.ShapeDtypeStruct((b, g, qw, t), BF16), jax.ShapeDtypeStruct((b, g, t, qw), BF16),
            jax.ShapeDtypeStruct((b, g, 2 * LANES, t), BF16), jax.ShapeDtypeStruct((b, g, t, LANES), BF16),
            jax.ShapeDtypeStruct((b, g, LANES, t), BF16), jax.ShapeDtypeStruct((b, g, t, LANES), BF16),
            jax.ShapeDtypeStruct((b, g, t, LANES), F32), jax.ShapeDtypeStruct((b, g, GATE_ROWS, t), F32),
        ),
        grid=(b, nt),
        in_specs=[
            pl.BlockSpec((tk, N_WIDTH), lambda bi, i: (bi * nt + i, C_NQ // N_WIDTH)),
            pl.BlockSpec((tk, 4 * LANES), lambda bi, i: (bi * nt + i, C_KV // (4 * LANES))),
            pl.BlockSpec((tk, LANES), lambda bi, i: (bi * nt + i, X_SM)),
            pl.BlockSpec((None, 1, tk), lambda bi, i: (bi, 0, i)),
            pl.BlockSpec(inv_row.shape, lambda bi, i: (0, 0)),
            pl.BlockSpec((1, LANES), lambda bi, i: (0, 0)),
        ],
        out_specs=(tok_t(qw), tok((tk, qw)), tok_t(2 * LANES), tok((tk, LANES)),
                   tok_t(LANES), tok((tk, LANES)), tok((tk, LANES)), tok_t(GATE_ROWS)),
        compiler_params=_cparams(("parallel", "arbitrary")),
        name="kvprep",
    )(p, p, aux, pos_col, inv_row, gate_bias)


CMP_ROW_CHUNK = 128


def _select_kernel(qt_ref, kc_ref, vct_ref, ovt_ref, gt_ref, oc_ref, bias_ref, imp_ref):
    qs = qt_ref.shape[1]
    nr = kc_ref.shape[0]
    t0 = pl.program_id(2) * qs
    sees_any = t0 + lax.broadcasted_iota(jnp.int32, (1, qs), 1) >= CMP_BLOCK - 1

    def cmp_branch(rows):
        m_c = (lax.broadcasted_iota(jnp.int32, (rows, qs), 0) * CMP_STRIDE + (CMP_BLOCK - 1)
               <= t0 + lax.broadcasted_iota(jnp.int32, (rows, qs), 1))
        kc = kc_ref[0:rows, :]
        vct = vct_ref[:, 0:rows]
        ovt = ovt_ref[:, 0:rows]
        imp = jnp.zeros((LANES, qs), F32)
        for h in range(N_HPG):
            s = jnp.dot(kc, qt_ref[h * LANES:(h + 1) * LANES, :], preferred_element_type=F32)
            s = jnp.where(m_c, s, NEG)
            e = jnp.exp2(s - jnp.max(s, axis=0, keepdims=True)).astype(BF16)
            r = jnp.dot(vct, e, preferred_element_type=F32)
            inv = jnp.where(sees_any, 1.0 / r[N_HEAD_DIM:N_HEAD_DIM + 1, :], 0.0)
            imp = imp + jnp.dot(ovt, e, preferred_element_type=F32) * inv
            oc_ref[:, h * LANES:(h + 1) * LANES] = (r * (inv * gt_ref[3 * h:3 * h + 1, :])).T.astype(BF16)
        imp_ref[...] = imp

    n_var = nr // CMP_ROW_CHUNK
    visible = (t0 + qs - CMP_BLOCK) // CMP_STRIDE + 1
    need = jnp.clip((visible + CMP_ROW_CHUNK - 1) // CMP_ROW_CHUNK, 1, n_var)
    for v in range(1, n_var + 1):
        pl.when(need == v)(functools.partial(cmp_branch, v * CMP_ROW_CHUNK))
    imp = imp_ref[...]

    blk = lax.broadcasted_iota(jnp.int32, (LANES, qs), 0)
    tq = t0 + lax.broadcasted_iota(jnp.int32, (LANES, qs), 1)
    cur = tq // SEL_BLOCK
    forced = (blk == 0) | (blk == cur) | (blk == cur - 1)
    n_sel = (nr * CMP_STRIDE) // SEL_BLOCK
    free_score = jnp.where(blk * SEL_BLOCK <= tq, imp, -1.0)
    blk_f = blk.astype(F32)
    n_top = min(SEL_TOPK, n_sel)
    n_forced = 3

    def pick(score, n_iter):
        score = jnp.where(blk < n_sel, score, -jnp.inf)
        for _ in range(n_iter):
            mx = jnp.max(score, axis=0, keepdims=True)
            idx = jnp.min(jnp.where(score == mx, blk_f, float(LANES)), axis=0, keepdims=True)
            score = jnp.where(blk_f == idx, -jnp.inf, score)
        bias_t = jnp.where((score == -jnp.inf) & (blk < n_sel), 0.0, NEG)
        bias_ref[...] = bias_t.T.astype(BF16)

    all_three_forced = (t0 >= 2 * SEL_BLOCK) & (n_top >= n_forced)

    @pl.when(all_three_forced)
    def _():
        pick(jnp.where(forced, -jnp.inf, free_score), n_top - n_forced)

    @pl.when(jnp.logical_not(all_three_forced))
    def _():
        pick(jnp.where(forced, FORCE, free_score), n_top)


def _select(qrawt, kc, vct, ovt, gates_t):
    b, g, qw, t = qrawt.shape
    nr = kc.shape[2]
    qs = SELECT_QS

    def whole(rows, cols):
        return pl.BlockSpec((None, None, rows, cols), lambda bi, gi, i: (bi, gi, 0, 0))

    return pl.pallas_call(
        _select_kernel,
        out_shape=(jax.ShapeDtypeStruct((b, g, t, qw), BF16), jax.ShapeDtypeStruct((b, g, t, LANES), BF16)),
        grid=(b, g, t // qs),
        in_specs=[
            pl.BlockSpec((None, None, qw, qs), lambda bi, gi, i: (bi, gi, 0, i)),
            whole(nr, LANES), whole(LANES, nr),
            pl.BlockSpec(ovt.shape, lambda bi, gi, i: (0, 0)),
            pl.BlockSpec((None, None, GATE_ROWS, qs), lambda bi, gi, i: (bi, gi, 0, i)),
        ],
        out_specs=(pl.BlockSpec((None, None, qs, qw), lambda bi, gi, i: (bi, gi, i, 0)),
                   pl.BlockSpec((None, None, qs, LANES), lambda bi, gi, i: (bi, gi, i, 0))),
        scratch_shapes=[pltpu.VMEM((LANES, qs), F32)],
        compiler_params=_cparams(("parallel", "parallel", "parallel")),
        name="select",
    )(qrawt, kc, vct, ovt, gates_t)


def _nsa_kernel(qrope_ref, bias_ref, oc_ref, kst_ref, vs_ref, kwt_ref, vw_ref, gates_ref, nz_ref,
                yb_ref, qaug_ref, acc_ref, m_ref, s_ref):
    i = pl.program_id(1)
    t0 = i * Q_BLOCK
    Q = Q_BLOCK
    nh = N_HPG
    ng = N_KV_GROUPS
    rows = nh * Q
    t0a = pl.multiple_of(t0, Q)
    causal = (lax.broadcasted_iota(jnp.int32, (Q, Q), 1) <= lax.broadcasted_iota(jnp.int32, (Q, Q), 0))
    lane = lax.broadcasted_iota(jnp.int32, (Q, LANES), 1)

    def q_rope(g):
        return jnp.concatenate([qrope_ref[g, :, h * LANES:(h + 1) * LANES] for h in range(nh)], axis=0)

    def mask_heads(s, m):
        return jnp.concatenate([jnp.where(m, s[h * Q:(h + 1) * Q, :], NEG) for h in range(nh)], axis=0)

    first_blk = t0 // SEL_BLOCK
    for g in range(ng):
        qr = q_rope(g)
        bias = bias_ref[g]
        bias_early = jnp.where(lane < first_blk, bias, NEG).astype(BF16)
        qaug_ref[g, :, 0:LANES] = jnp.concatenate([bias_early] * nh, axis=0)
        qaug_ref[g, :, LANES:2 * LANES] = qr
        q_diag = jnp.concatenate([jnp.concatenate([bias] * nh, axis=0), qr], axis=1)
        s_d = jnp.dot(q_diag, kst_ref[g, :, pl.ds(t0a, Q)], preferred_element_type=F32)
        s_d = mask_heads(s_d, causal)
        m0 = jnp.max(s_d, axis=1, keepdims=True)
        p_d = jnp.exp2(s_d - m0)
        m_ref[g] = jnp.broadcast_to(m0, (rows, LANES))
        acc_ref[g] = jnp.dot(p_d.astype(BF16), vs_ref[g, pl.ds(t0a, Q), :], preferred_element_type=F32)

    ncb = SEL_CHUNK // LANES
    n_chunks = (t0 + SEL_CHUNK - 1) // SEL_CHUNK
    last = jnp.maximum(n_chunks - 1, 0)

    def scores(g, c, slot):
        k0 = pl.multiple_of(c * SEL_CHUNK, SEL_CHUNK)
        s_ref[slot, g] = jnp.dot(qaug_ref[g], kst_ref[g, :, pl.ds(k0, SEL_CHUNK)], preferred_element_type=F32)

    def consume(g, c, slot):
        k0 = pl.multiple_of(c * SEL_CHUNK, SEL_CHUNK)
        s = s_ref[slot, g]
        m_old = m_ref[g]
        m_new = jnp.maximum(m_old, jnp.max(s, axis=1, keepdims=True))
        p = jnp.concatenate([jnp.exp2(s[:, cb * LANES:(cb + 1) * LANES] - m_new) for cb in range(ncb)], axis=1)
        acc_ref[g] = (jnp.exp2(m_old - m_new) * acc_ref[g]
                      + jnp.dot(p.astype(BF16), vs_ref[g, pl.ds(k0, SEL_CHUNK), :], preferred_element_type=F32))
        m_ref[g] = m_new

    for g in range(ng):
        scores(g, 0, 0)

    def steps(c, n):
        for u in range(n):
            for g in range(ng):
                scores(g, jnp.minimum(c + u + 1, last), (u + 1) % 2)
                consume(g, c + u, u % 2)

    def sel_quad(c4, carry):
        steps(SEL_UNROLL * c4, SEL_UNROLL)
        return carry

    n_quads = n_chunks // SEL_UNROLL
    lax.fori_loop(0, n_quads, sel_quad, 0)
    rem = n_chunks - SEL_UNROLL * n_quads

    @pl.when(rem >= 2)
    def _():
        steps(SEL_UNROLL * n_quads, 2)

    @pl.when(rem % 2 == 1)
    def _():
        for g in range(ng):
            consume(g, n_chunks - 1, 0)

    wlen = WINDOW + Q
    ws = pl.multiple_of(jnp.maximum(t0 - WINDOW, 0), Q)
    diff = (t0 + lax.broadcasted_iota(jnp.int32, (Q, wlen), 0)) - (ws + lax.broadcasted_iota(jnp.int32, (Q, wlen), 1))
    m_w = (diff >= 0) & (diff < WINDOW)
    for g in range(ng):
        gates = gates_ref[g]

        def gate_col(br):
            return jnp.concatenate([gates[:, 3 * h + br:3 * h + br + 1] for h in range(nh)], axis=0)

        acc = acc_ref[g]
        o_s = acc * (gate_col(1) / acc[:, N_HEAD_DIM:N_HEAD_DIM + 1])
        s_w = jnp.dot(q_rope(g), kwt_ref[g, :, pl.ds(ws, wlen)], preferred_element_type=F32)
        s_w = mask_heads(s_w, m_w)
        p_w = jnp.exp2(s_w - jnp.max(s_w, axis=1, keepdims=True))
        acc_w = jnp.dot(p_w.astype(BF16), vw_ref[g, pl.ds(ws, wlen), :], preferred_element_type=F32)
        o_w = acc_w * (gate_col(2) / acc_w[:, N_HEAD_DIM:N_HEAD_DIM + 1])
        outs = []
        for h in range(nh):
            rs = slice(h * Q, (h + 1) * Q)
            outs.append(oc_ref[g, :, h * LANES:(h + 1) * LANES].astype(F32) + o_s[rs, :] + o_w[rs, :])
        for hp in range(nh // 2):
            yn = jnp.where(lane < N_HEAD_DIM, outs[2 * hp], pltpu.roll(outs[2 * hp + 1], N_HEAD_DIM, axis=1))
            ls = slice((g * (nh // 2) + hp) * LANES, (g * (nh // 2) + hp + 1) * LANES)
            yb_ref[:, ls] = (yn * _silu(nz_ref[:, ls].astype(F32))).astype(BF16)


def _nsa(p, b, t, qrope, bias, oc, kst, vs, kwt, vw, gates):
    g = N_KV_GROUPS
    nqb = t // Q_BLOCK
    qw = N_HPG * LANES
    rows = N_HPG * Q_BLOCK

    def qblk(w):
        return pl.BlockSpec((None, g, Q_BLOCK, w), lambda bi, i: (bi, 0, i, 0))

    def whole(r, c):
        return pl.BlockSpec((None, g, r, c), lambda bi, i: (bi, 0, 0, 0), pipeline_mode=pl.Buffered(1))

    return pl.pallas_call(
        _nsa_kernel,
        out_shape=jax.ShapeDtypeStruct((b * t, N_WIDTH), BF16),
        grid=(b, nqb),
        in_specs=[
            qblk(qw), qblk(LANES), qblk(qw),
            whole(2 * LANES, t), whole(t, LANES), whole(LANES, t), whole(t, LANES),
            qblk(LANES),
            pl.BlockSpec((Q_BLOCK, N_WIDTH), lambda bi, i: (bi * nqb + i, C_NZ // N_WIDTH)),
        ],
        out_specs=pl.BlockSpec((Q_BLOCK, N_WIDTH), lambda bi, i: (bi * nqb + i, 0)),
        scratch_shapes=[pltpu.VMEM((g, rows, 2 * LANES), BF16), pltpu.VMEM((g, rows, LANES), F32),
                        pltpu.VMEM((g, rows, LANES), F32), pltpu.VMEM((2, g, rows, SEL_CHUNK), F32)],
        compiler_params=_cparams(("parallel", "arbitrary")),
        name="nsa",
    )(qrope, bias, oc, kst, vs, kwt, vw, gates, p)


def _outproj_kernel(final, ya_ref, yb_ref, ga_ref, gb_ref, x_ref, wa_ref, wb_ref, wo_ref, fg_ref, o_ref):
    a = jnp.dot(ya_ref[...], wa_ref[...], preferred_element_type=F32)
    bq = jnp.dot(yb_ref[...], wb_ref[...], preferred_element_type=F32)
    merged = _sigmoid(ga_ref[...].astype(F32)) * a + _sigmoid(gb_ref[...].astype(F32)) * bq
    y = x_ref[...] + jnp.dot(merged.astype(BF16), wo_ref[...], preferred_element_type=F32)
    if final:
        y = y * lax.rsqrt(jnp.mean(y * y, axis=-1, keepdims=True) + NORM_EPS) * fg_ref[...]
    o_ref[...] = y


def _outproj(ya, yb, p, x2, wa, wb, wo, fg, final):
    m = x2.shape[0]
    tm = OUTPROJ_TM

    def full(a):
        return pl.BlockSpec(a.shape, lambda i: (0, 0))

    return pl.pallas_call(
        functools.partial(_outproj_kernel, final),
        out_shape=jax.ShapeDtypeStruct((m, D_MODEL), F32),
        grid=(m // tm,),
        in_specs=[
            pl.BlockSpec((tm, M_WIDTH), lambda i: (i, 0)),
            pl.BlockSpec((tm, N_WIDTH), lambda i: (i, 0)),
            pl.BlockSpec((tm, D_MODEL), lambda i: (i, C_GA // D_MODEL)),
            pl.BlockSpec((tm, D_MODEL), lambda i: (i, C_GB // D_MODEL)),
            pl.BlockSpec((tm, D_MODEL), lambda i: (i, 0)),
            full(wa), full(wb), full(wo), full(fg),
        ],
        out_specs=pl.BlockSpec((tm, D_MODEL), lambda i: (i, 0)),
        compiler_params=_cparams(("parallel",)),
        name="outproj",
    )(ya, yb, p, p, x2, wa, wb, wo, fg)


def _expand_cmp_w1(w1k, w1v):
    k3 = w1k.astype(BF16).reshape(CMP_BLOCK, 1, N_HEAD_DIM, 1, CMP_HIDDEN)
    v3 = w1v.astype(BF16).reshape(CMP_BLOCK, 1, N_HEAD_DIM, 1, CMP_HIDDEN)
    nc = 2 * N_KV_GROUPS
    eye = jnp.eye(nc, dtype=BF16)
    is_k = (jnp.arange(nc) < N_KV_GROUPS).astype(BF16)
    mk = (eye * is_k[:, None]).reshape(1, nc, 1, nc, 1)
    mv = (eye * (1 - is_k)[:, None]).reshape(1, nc, 1, nc, 1)
    z = k3 * mk + v3 * mv
    return z.reshape(CMP_BLOCK * nc * N_HEAD_DIM, nc * CMP_HIDDEN)


def kernel(x, positions, norm_g, w_in, conv_q_w, conv_q_b, conv_k_w, conv_k_b, b_igate, b_fgate, mh_norm_g,
           cmp_pos_k, cmp_w1_k, cmp_w2_k, cmp_pos_v, cmp_w1_v, cmp_w2_v, b_nsa_gate, w_branch_a, w_branch_b,
           w_out, final_norm_g):
    b, t, _ = x.shape
    assert t % (2 * SEL_CHUNK) == 0 and t >= WINDOW + Q_BLOCK and t // SEL_BLOCK <= LANES and t % MLSTM_CHUNK == 0
    x2 = x.reshape(b * t, D_MODEL)
    for l in range(norm_g.shape[0]):
        w_p = _wprep(jnp.swapaxes(w_in, 1, 2), l)
        p, aux = _inproj(x2, norm_g[l][None, :], w_p, w_p)

        sm_bias = jnp.zeros((1, LANES), F32).at[0, 0:M_HEADS].set(b_igate[l]).at[0, M_HEADS:2 * M_HEADS].set(b_fgate[l])
        ya = _mlstm(p, aux, b, t, conv_q_w[l], conv_q_b[l][None, :], conv_k_w[l], conv_k_b[l][None, :], sm_bias,
                    mh_norm_g[l][None, :])

        nr = t // CMP_STRIDE
        wz = _expand_cmp_w1(cmp_w1_k[l], cmp_w1_v[l])
        w2pad = lambda w: jnp.pad(w, ((0, 0), (0, LANES - N_HEAD_DIM))).astype(BF16)
        kc, vct = _compress(aux, b, t, wz, cmp_pos_k[l].reshape(1, -1), cmp_pos_v[l].reshape(1, -1),
                            cmp_w1_k[l].astype(BF16), cmp_w1_v[l].astype(BF16), w2pad(cmp_w2_k[l]), w2pad(cmp_w2_v[l]))

        half = ROPE_DIM // 2
        inv = jnp.power(jnp.float32(ROPE_THETA), -jnp.arange(half, dtype=F32) * (2.0 / ROPE_DIM))
        inv_row = inv[:, None]
        pos_col = positions.astype(F32)[:, None, :]
        gate_bias = jnp.zeros((1, LANES), F32).at[0, SM_G_OFF:SM_G_OFF + 3 * N_HEADS].set(b_nsa_gate[l])
        qrawt, qrope, kst, vs, kwt, vw, gates, gates_t = _kvprep(p, aux, b, t, pos_col, inv_row, gate_bias)

        ci = np.arange(nr)[None, :] * CMP_STRIDE
        sj = np.arange(LANES)[:, None] * SEL_BLOCK
        overlap_t = jnp.asarray((ci < sj + SEL_BLOCK) & (ci + CMP_BLOCK > sj) & (np.arange(nr)[None, :] < nr - 1)
                                & (np.arange(LANES)[:, None] < t // SEL_BLOCK), dtype=BF16)
        oc, bias = _select(qrawt, kc, vct, overlap_t, gates_t)
        yb = _nsa(p, b, t, qrope, bias, oc, kst, vs, kwt, vw, gates)

        x2 = _outproj(ya, yb, p, x2, w_branch_a[l].astype(BF16), w_branch_b[l].astype(BF16),
                      w_out[l].astype(BF16), final_norm_g[None, :], l == norm_g.shape[0] - 1)
    return x2.reshape(b, t, D_MODEL)
```

```python
import functools
import math

import jax
import jax.numpy as jnp
import numpy as np
from jax import lax
from jax.experimental import pallas as pl
from jax.experimental.pallas import tpu as pltpu

F32 = jnp.float32
BF16 = jnp.bfloat16

D_MODEL = 1024
M_HEADS = 4
M_HEAD_DIM = 256
M_WIDTH = M_HEADS * M_HEAD_DIM
CONV_WIDTH = 4
N_HEADS = 8
N_KV_GROUPS = 2
N_HPG = N_HEADS // N_KV_GROUPS
N_HEAD_DIM = 64
N_WIDTH = N_HEADS * N_HEAD_DIM
CMP_BLOCK = 32
CMP_STRIDE = 16
CMP_HIDDEN = 128
SEL_BLOCK = 64
SEL_TOPK = 16
WINDOW = 512
Q_BLOCK = 256
ROPE_THETA = 500000.0
ROPE_DIM = N_HEAD_DIM // 4
NORM_EPS = 1e-6
NEG = -1e30
FORCE = 1e9

LANES = 128
SUBLANES = 8
MLSTM_CHUNK = 256
SEL_CHUNK = 512
SEL_UNROLL = 4
V7X_VMEM_BYTES = 64 * 1024 * 1024
VMEM_LIMIT = V7X_VMEM_BYTES - 8 * 1024 * 1024
INPROJ_VMEM_LIMIT = V7X_VMEM_BYTES - 4 * 1024 * 1024
INPROJ_TM, OUTPROJ_TM, KVPREP_TK, SELECT_QS = 2048, 1024, 1024, 2048
SM_F_OFF = M_HEADS
SM_G_OFF = 2 * M_HEADS
GATE_ROWS = -(-3 * N_HPG // SUBLANES) * SUBLANES

C_MQ, C_MK, C_MV, C_MO, C_MZ, C_GA, C_GB = 0, 1024, 2048, 3072, 4096, 5120, 6144
C_NQ, C_NZ = 7168, 7680
C_KV = 8192
C_CMP = 8704
C_SM = 8960
P_W = 9216
AUX_W = 3 * LANES
X_KC, X_VC, X_SM = 0, 1, 2
_O_MI, _O_NQ, _O_KC, _O_KS, _O_NG, _O_NZ, _O_GA, _O_END = 5120, 5128, 5640, 5896, 6408, 6432, 6944, 8992


def _cparams(sem, vmem_limit=VMEM_LIMIT):
    return pltpu.CompilerParams(dimension_semantics=sem, vmem_limit_bytes=vmem_limit)


def _sigmoid(x):
    return 0.5 * jnp.tanh(0.5 * x) + 0.5


def _silu(x):
    h = 0.5 * x
    return h + h * jnp.tanh(h)


def _log_sigmoid(x):
    return jnp.minimum(x, 0.0) - jnp.log1p(jnp.exp(-jnp.abs(x)))


PROJ_TN = 1536
AUX_TILE = C_CMP // PROJ_TN
AUX_OFF = C_CMP % PROJ_TN


N_HEAD_TILES = _O_MI // PROJ_TN


WPREP_COLS, WPREP_PIECE = 128, 512
_W_SLABS = ((0, 0, _O_MI), (C_GA, _O_GA, _O_END - _O_GA), (C_NQ, _O_NQ, _O_KC - _O_NQ), (C_NZ, _O_NZ, _O_GA - _O_NZ),
            (C_KV, _O_KS, _O_NG - _O_KS), (C_CMP, _O_KC, _O_KS - _O_KC))


def _wprep_kernel(w_ref, o_ref):
    for dst, src, width in _W_SLABS:
        for c in range(0, width, WPREP_PIECE):
            n = min(WPREP_PIECE, width - c)
            o_ref[dst + c:dst + c + n, :] = w_ref[src + c:src + c + n, :].astype(BF16)
    pad = jnp.zeros((P_W - C_SM - (_O_NQ - _O_MI) - (_O_NZ - _O_NG), o_ref.shape[1]), F32)
    sm = jnp.concatenate([w_ref[_O_MI:_O_NQ, :], w_ref[_O_NG:_O_NZ, :], pad], axis=0)
    o_ref[C_SM:P_W, :] = sm.astype(BF16)


def _wprep(w_t, l):
    assert all(v % (2 * SUBLANES) == 0 for s in _W_SLABS for v in (s[0], s[2])) and _O_NQ - _O_MI == SM_G_OFF
    assert all(s[1] % SUBLANES == 0 for s in _W_SLABS) and _O_NG % SUBLANES == 0 and _O_NZ % SUBLANES == 0
    return pl.pallas_call(
        _wprep_kernel,
        out_shape=jax.ShapeDtypeStruct((P_W, w_t.shape[2]), BF16),
        grid=(w_t.shape[2] // WPREP_COLS,),
        in_specs=[pl.BlockSpec((None, w_t.shape[1], WPREP_COLS), lambda i: (l, 0, i))],
        out_specs=pl.BlockSpec((P_W, WPREP_COLS), lambda i: (0, i)),
        compiler_params=_cparams(("parallel",)),
        name="wprep",
    )(w_t)


def _dot_nt(a, b):
    return lax.dot_general(a, b, (((1,), (1,)), ((), ())), preferred_element_type=F32)


def _inproj_kernel(x_ref, g_ref, wh_ref, wt_ref, o_ref, aux_ref, hn_ref):
    j = pl.program_id(1)

    @pl.when(j == 0)
    def _():
        x = x_ref[...]
        r = lax.rsqrt(jnp.mean(x * x, axis=-1, keepdims=True) + NORM_EPS)
        hn_ref[...] = (x * r * g_ref[...]).astype(BF16)

    @pl.when(j < N_HEAD_TILES)
    def _():
        o_ref[...] = _dot_nt(hn_ref[...], wh_ref[...]).astype(BF16)

    @pl.when((j >= N_HEAD_TILES) & (j != AUX_TILE))
    def _():
        o_ref[...] = _dot_nt(hn_ref[...], wt_ref[...]).astype(BF16)

    @pl.when(j == AUX_TILE)
    def _():
        acc = _dot_nt(hn_ref[...], wt_ref[...])
        o_ref[...] = acc.astype(BF16)
        aux_ref[...] = acc[:, AUX_OFF:AUX_OFF + AUX_W]


def _inproj(x2, norm_g, w_head, w_tail):
    m = x2.shape[0]
    tm, tn = INPROJ_TM, PROJ_TN
    return pl.pallas_call(
        _inproj_kernel,
        out_shape=(jax.ShapeDtypeStruct((m, P_W), BF16), jax.ShapeDtypeStruct((m, AUX_W), F32)),
        grid=(m // tm, P_W // tn),
        in_specs=[
            pl.BlockSpec((tm, D_MODEL), lambda i, j: (i, 0)),
            pl.BlockSpec((1, D_MODEL), lambda i, j: (0, 0)),
            pl.BlockSpec((tn, D_MODEL), lambda i, j: (jnp.minimum(j, N_HEAD_TILES - 1), 0)),
            pl.BlockSpec((tn, D_MODEL), lambda i, j: (jnp.maximum(j, N_HEAD_TILES), 0)),
        ],
        out_specs=(pl.BlockSpec((tm, tn), lambda i, j: (i, j)), pl.BlockSpec((tm, AUX_W), lambda i, j: (i, 0))),
        scratch_shapes=[pltpu.VMEM((tm, D_MODEL), BF16)],
        compiler_params=_cparams(("parallel", "arbitrary"), INPROJ_VMEM_LIMIT),
        name="inproj",
    )(x2, norm_g, w_head, w_tail)


def _causal_conv_silu(u16, prev_ref, w_ref, b_ref):
    n, w = u16.shape
    u = u16.astype(F32)
    row = lax.broadcasted_iota(jnp.int32, (n, n), 0)
    col = lax.broadcasted_iota(jnp.int32, (n, n), 1)
    sub = lax.broadcasted_iota(jnp.int32, (SUBLANES, w), 0)
    prev = prev_ref[...]
    y = b_ref[...] + u * w_ref[CONV_WIDTH - 1:CONV_WIDTH, :]
    for d in range(1, CONV_WIDTH):
        shift = jnp.where(row - col == d, 1.0, 0.0).astype(BF16)
        sh = jnp.dot(shift, u16, preferred_element_type=F32)
        head = jnp.where(sub < d, pltpu.roll(prev, d, axis=0), sh[0:SUBLANES, :])
        sh = jnp.concatenate([head, sh[SUBLANES:, :]], axis=0)
        y = y + sh * w_ref[CONV_WIDTH - 1 - d:CONV_WIDTH - d, :]
    prev_ref[...] = u[n - SUBLANES:n, :]
    return _silu(y)


def _mlstm_kernel(q_ref, k_ref, v_ref, og_ref, z_ref, sm_ref, cqw_ref, cqb_ref, ckw_ref, ckb_ref,
                  sb_ref, ng_ref, ya_ref, qprev, kprev, c_state, n_state, m_state):
    for bb in range(q_ref.shape[0]):
        _mlstm_chunk(q_ref.at[bb], k_ref.at[bb], v_ref.at[bb], og_ref.at[bb], z_ref.at[bb], sm_ref.at[bb],
                     cqw_ref, cqb_ref, ckw_ref, ckb_ref, sb_ref, ng_ref, ya_ref.at[bb],
                     qprev.at[bb], kprev.at[bb], c_state.at[bb], n_state.at[bb], m_state.at[bb])


def _mlstm_chunk(q_ref, k_ref, v_ref, og_ref, z_ref, sm_ref, cqw_ref, cqb_ref, ckw_ref, ckb_ref,
                 sb_ref, ng_ref, ya_ref, qprev, kprev, c_state, n_state, m_state):
    L = q_ref.shape[0]
    d = M_HEAD_DIM

    @pl.when(pl.program_id(0) == 0)
    def _():
        qprev[...] = jnp.zeros_like(qprev)
        kprev[...] = jnp.zeros_like(kprev)
        c_state[...] = jnp.zeros_like(c_state)
        n_state[...] = jnp.zeros_like(n_state)
        m_state[...] = jnp.zeros_like(m_state)

    mq = _causal_conv_silu(q_ref[...], qprev, cqw_ref, cqb_ref)
    mk = _causal_conv_silu(k_ref[...], kprev, ckw_ref, ckb_ref) * (d ** -0.5)

    sm = sm_ref[...] + sb_ref[...]
    lf = _log_sigmoid(sm)
    row = lax.broadcasted_iota(jnp.int32, (L, L), 0)
    col = lax.broadcasted_iota(jnp.int32, (L, L), 1)
    causal = row >= col
    tril = causal.astype(F32)
    triu = (row <= col).astype(F32)
    a_col = jnp.dot(tril, lf, precision=lax.Precision.HIGHEST, preferred_element_type=F32)
    sm_t = sm.T[0:SUBLANES, :]
    a_row = jnp.dot(_log_sigmoid(sm_t), triu, precision=lax.Precision.HIGHEST,
                    preferred_element_type=F32)

    for h in range(M_HEADS):
        hs = slice(h * d, (h + 1) * d)
        a_c = a_col[:, SM_F_OFF + h:SM_F_OFF + h + 1]
        li_c = sm[:, h:h + 1]
        a_r = a_row[SM_F_OFF + h:SM_F_OFF + h + 1, :]
        li_r = sm_t[h:h + 1, :]
        m_prev = m_state[h:h + 1, 0:1]
        a_end = a_c[L - 1:L, :]

        log_d = jnp.where(causal, a_c - (a_r - li_r), -jnp.inf)
        inter = a_c + m_prev
        m_t = jnp.maximum(inter, jnp.max(log_d, axis=1, keepdims=True))
        w_inter = jnp.exp(inter - m_t)
        dm = jnp.exp(log_d - m_t)

        qh = mq[:, hs]
        kh = mk[:, hs]
        qb = qh.astype(BF16)
        kb = kh.astype(BF16)
        vb = v_ref[:, hs]
        s = lax.dot_general(qb, kb, (((1,), (1,)), ((), ())), preferred_element_type=F32) * dm
        c_old = c_state[h]
        n_old = n_state[h:h + 1, :]
        num = (w_inter * jnp.dot(qb, c_old.astype(BF16), preferred_element_type=F32)
               + jnp.dot(s.astype(BF16), vb, preferred_element_type=F32))
        den = (w_inter * jnp.sum(qh * n_old, axis=1, keepdims=True)
               + jnp.sum(s, axis=1, keepdims=True))
        hh = num * (1.0 / jnp.maximum(jnp.abs(den), jnp.exp(-m_t)))

        logw = a_end - a_c + li_c
        m_new = jnp.maximum(a_end + m_prev, jnp.max(logw, axis=0, keepdims=True))
        wk = jnp.exp(logw - m_new)
        decay = jnp.exp(a_end + m_prev - m_new)
        kw = kh * wk
        c_state[h] = decay * c_old + lax.dot_general(kw.astype(BF16), vb, (((0,), (0,)), ((), ())),
                                                     preferred_element_type=F32)
        n_state[h:h + 1, :] = decay * n_old + jnp.sum(kw, axis=0, keepdims=True)
        m_state[h:h + 1, :] = jnp.broadcast_to(m_new, (1, LANES))

        mu = jnp.mean(hh, axis=1, keepdims=True)
        dc = hh - mu
        var = jnp.mean(dc * dc, axis=1, keepdims=True)
        hn = dc * lax.rsqrt(var + NORM_EPS) * ng_ref[:, hs]
        ya = _sigmoid(og_ref[:, hs].astype(F32)) * hn * _silu(z_ref[:, hs].astype(F32))
        ya_ref[:, hs] = ya.astype(BF16)


def _mlstm(p, aux, b, t, conv_q_w, conv_q_b, conv_k_w, conv_k_b, sm_bias, mh_norm_g):
    L = MLSTM_CHUNK
    nc = t // L

    def colspec(cstart):
        return pl.BlockSpec((b, L, M_WIDTH), lambda c: (0, c, cstart // M_WIDTH))

    def full(shape):
        return pl.BlockSpec(shape, lambda c: (0,) * len(shape))

    p3 = p.reshape(b, t, P_W)
    ya = pl.pallas_call(
        _mlstm_kernel,
        out_shape=jax.ShapeDtypeStruct((b, t, M_WIDTH), BF16),
        grid=(nc,),
        in_specs=[
            colspec(C_MQ), colspec(C_MK), colspec(C_MV), colspec(C_MO), colspec(C_MZ),
            pl.BlockSpec((b, L, LANES), lambda c: (0, c, X_SM)),
            full((CONV_WIDTH, M_WIDTH)), full((1, M_WIDTH)), full((CONV_WIDTH, M_WIDTH)), full((1, M_WIDTH)),
            full((1, LANES)), full((1, M_WIDTH)),
        ],
        out_specs=pl.BlockSpec((b, L, M_WIDTH), lambda c: (0, c, 0)),
        scratch_shapes=[
            pltpu.VMEM((b, SUBLANES, M_WIDTH), F32), pltpu.VMEM((b, SUBLANES, M_WIDTH), F32),
            pltpu.VMEM((b, M_HEADS, M_HEAD_DIM, M_HEAD_DIM), F32),
            pltpu.VMEM((b, SUBLANES, M_HEAD_DIM), F32), pltpu.VMEM((b, SUBLANES, LANES), F32),
        ],
        compiler_params=_cparams(("arbitrary",)),
        name="mlstm",
    )(p3, p3, p3, p3, p3, aux.reshape(b, t, AUX_W), conv_q_w, conv_q_b, conv_k_w, conv_k_b, sm_bias, mh_norm_g)
    return ya.reshape(b * t, M_WIDTH)


def _compress_kernel(pk_ref, pv_ref, wt_ref, wb_ref, posk_ref, posv_ref, w1k_ref, w1v_ref, w2k_ref, w2v_ref,
                     kc_ref, vct_ref):
    nr = pk_ref.shape[0] // CMP_STRIDE
    cw = 2 * LANES
    u = jnp.zeros((nr, wt_ref.shape[1]), F32)
    v = jnp.zeros((nr, wb_ref.shape[1]), F32)
    for l in range(CMP_STRIDE):
        x = jnp.concatenate([pk_ref[pl.ds(l, nr, stride=CMP_STRIDE), :],
                             pv_ref[pl.ds(l, nr, stride=CMP_STRIDE), :]], axis=1).astype(BF16)
        u = u + jnp.dot(x, wt_ref[l * cw:(l + 1) * cw, :], preferred_element_type=F32)
        v = v + jnp.dot(x, wb_ref[l * cw:(l + 1) * cw, :], preferred_element_type=F32)
    v = pltpu.roll(v, nr - 1, axis=0)
    rowi = lax.broadcasted_iota(jnp.int32, v.shape, 0)
    v = jnp.where(rowi < nr - 1, v, 0.0)

    def pos_term(pos_ref, w_ref):
        pb = jnp.broadcast_to(pos_ref[...].astype(BF16), (SUBLANES, CMP_BLOCK * N_HEAD_DIM))
        return jnp.dot(pb, w_ref[...], preferred_element_type=F32)[0:1, :]

    pk = pos_term(posk_ref, w1k_ref)
    pv = pos_term(posv_ref, w1v_ref)
    hid = u + v + jnp.concatenate([pk, pk, pv, pv], axis=1)
    act = 0.5 * hid * (1.0 + jnp.tanh(math.sqrt(2.0 / math.pi) * (hid + 0.044715 * (hid * hid * hid))))
    for c in range(2 * N_KV_GROUPS):
        w2 = w2k_ref[...] if c < N_KV_GROUPS else w2v_ref[...]
        out = jnp.dot(act[:, c * CMP_HIDDEN:(c + 1) * CMP_HIDDEN].astype(BF16), w2,
                      preferred_element_type=F32)
        if c < N_KV_GROUPS:
            kc_ref[c] = out.astype(BF16)
        else:
            out_t = out.T
            ones_row = lax.broadcasted_iota(jnp.int32, out_t.shape, 0) == N_HEAD_DIM
            vct_ref[c - N_KV_GROUPS] = jnp.where(ones_row, 1.0, out_t).astype(BF16)


def _compress(aux, b, t, wz, posk, posv, w1k, w1v, w2k, w2v):
    nr = t // CMP_STRIDE
    g = N_KV_GROUPS
    hz = wz.shape[0] // 2

    def full(a):
        return pl.BlockSpec(a.shape, lambda bi: (0,) * a.ndim)

    return pl.pallas_call(
        _compress_kernel,
        out_shape=(jax.ShapeDtypeStruct((b, g, nr, LANES), BF16), jax.ShapeDtypeStruct((b, g, LANES, nr), BF16)),
        grid=(b,),
        in_specs=[pl.BlockSpec((t, LANES), lambda bi: (bi, X_KC)),
                  pl.BlockSpec((t, LANES), lambda bi: (bi, X_VC)),
                  pl.BlockSpec((hz, wz.shape[1]), lambda bi: (0, 0)), pl.BlockSpec((hz, wz.shape[1]), lambda bi: (1, 0)),
                  full(posk), full(posv), full(w1k), full(w1v), full(w2k), full(w2v)],
        out_specs=(pl.BlockSpec((None, g, nr, LANES), lambda bi: (bi, 0, 0, 0)),
                   pl.BlockSpec((None, g, LANES, nr), lambda bi: (bi, 0, 0, 0))),
        compiler_params=_cparams(("parallel",)),
        name="compress",
    )(aux, aux, wz, wz, posk, posv, w1k, w1v, w2k, w2v)


def _kvprep_kernel(q_ref, kv_ref, sm_ref, pos_ref, inv_ref, gb_ref,
                   qrawt_ref, qrope_ref, kst_ref, vs_ref, kwt_ref, vw_ref, gates_ref, gatest_ref):
    tk = q_ref.shape[0]
    lane = lax.broadcasted_iota(jnp.int32, (tk, LANES), 1)
    half = ROPE_DIM // 2
    ang = inv_ref[...] * pos_ref[...]
    cos8 = jnp.cos(ang)
    sin8 = jnp.sin(ang)
    rest = N_HEAD_DIM - ROPE_DIM
    one_r, zero_r, zero_h = jnp.ones((rest, tk), F32), jnp.zeros((rest, tk), F32), jnp.zeros((half, tk), F32)
    heads = LANES // N_HEAD_DIM
    cos = jnp.concatenate([cos8, cos8, one_r] * heads, axis=0).T
    s_up = jnp.concatenate([-sin8, zero_h, zero_r] * heads, axis=0).T
    s_dn = jnp.concatenate([zero_h, sin8, zero_r] * heads, axis=0).T

    def rope(u):
        return u * cos + pltpu.roll(u, LANES - half, axis=1) * s_up + pltpu.roll(u, half, axis=1) * s_dn

    def head_pad(u, odd):
        if odd:
            u = pltpu.roll(u, N_HEAD_DIM, axis=1)
        return jnp.where(lane < N_HEAD_DIM, u, 0.0)

    scale = N_HEAD_DIM ** -0.5 * math.log2(math.e)
    zeros_t = jnp.zeros((N_HEAD_DIM, tk), BF16)
    for g in range(N_KV_GROUPS):
        for hp in range(N_HPG // 2):
            c0 = g * (N_HPG * N_HEAD_DIM) + hp * LANES
            u = q_ref[:, c0:c0 + LANES].astype(F32) * scale
            ur = rope(u)
            u_t = u.T.astype(BF16)
            for odd in range(2):
                h = 2 * hp + odd
                qrawt_ref[g, h * LANES:h * LANES + N_HEAD_DIM, :] = u_t[odd * N_HEAD_DIM:(odd + 1) * N_HEAD_DIM, :]
                qrawt_ref[g, h * LANES + N_HEAD_DIM:(h + 1) * LANES, :] = zeros_t
                qrope_ref[g, :, h * LANES:(h + 1) * LANES] = head_pad(ur, odd).astype(BF16)

    ks_t = rope(kv_ref[:, 0:LANES].astype(F32)).T
    kw_t = rope(kv_ref[:, 2 * LANES:3 * LANES].astype(F32)).T
    vs = kv_ref[:, LANES:2 * LANES].astype(F32)
    vw = kv_ref[:, 3 * LANES:4 * LANES].astype(F32)
    t_glob = pl.program_id(1) * tk + lax.broadcasted_iota(jnp.int32, (LANES, tk), 1)
    blk_row = lax.broadcasted_iota(jnp.int32, (LANES, tk), 0)
    onehot = jnp.where(blk_row == t_glob // SEL_BLOCK, 1.0, 0.0).astype(BF16)
    ones_lane = jnp.where(lane == N_HEAD_DIM, 1.0, 0.0)
    gates = _sigmoid(sm_ref[...] + gb_ref[...])
    for g in range(N_KV_GROUPS):
        rs = slice(g * N_HEAD_DIM, (g + 1) * N_HEAD_DIM)
        kst_ref[g, 0:LANES, :] = onehot
        kst_ref[g, LANES:LANES + N_HEAD_DIM, :] = ks_t[rs, :].astype(BF16)
        kst_ref[g, LANES + N_HEAD_DIM:2 * LANES, :] = zeros_t
        kwt_ref[g, 0:N_HEAD_DIM, :] = kw_t[rs, :].astype(BF16)
        kwt_ref[g, N_HEAD_DIM:LANES, :] = zeros_t
        vs_ref[g] = (head_pad(vs, g) + ones_lane).astype(BF16)
        vw_ref[g] = (head_pad(vw, g) + ones_lane).astype(BF16)
        goff = SM_G_OFF + g * (3 * N_HPG)
        gg = pltpu.roll(gates, LANES - goff, axis=1)
        gates_ref[g] = gg
        gatest_ref[g] = gg.T[0:GATE_ROWS, :]


def _kvprep(p, aux, b, t, pos_col, inv_row, gate_bias):
    tk = KVPREP_TK
    nt = t // tk
    g = N_KV_GROUPS
    qw = N_HPG * LANES

    def tok(shape):
        return pl.BlockSpec((None, g) + shape, lambda bi, i: (bi, 0, i, 0))

    def tok_t(rows):
        return pl.BlockSpec((None, g, rows, tk), lambda bi, i: (bi, 0, 0, i))

    return pl.pallas_call(
        _kvprep_kernel,
        out_shape=(
            jax.ShapeDtypeStruct((b, g, qw, t), BF16), jax.ShapeDtypeStruct((b, g, t, qw), BF16),
            jax.ShapeDtypeStruct((b, g, 2 * LANES, t), BF16), jax.ShapeDtypeStruct((b, g, t, LANES), BF16),
            jax.ShapeDtypeStruct((b, g, LANES, t), BF16), jax.ShapeDtypeStruct((b, g, t, LANES), BF16),
            jax.ShapeDtypeStruct((b, g, t, LANES), F32), jax.ShapeDtypeStruct((b, g, GATE_ROWS, t), F32),
        ),
        grid=(b, nt),
        in_specs=[
            pl.BlockSpec((tk, N_WIDTH), lambda bi, i: (bi * nt + i, C_NQ // N_WIDTH)),
            pl.BlockSpec((tk, 4 * LANES), lambda bi, i: (bi * nt + i, C_KV // (4 * LANES))),
            pl.BlockSpec((tk, LANES), lambda bi, i: (bi * nt + i, X_SM)),
            pl.BlockSpec((None, 1, tk), lambda bi, i: (bi, 0, i)),
            pl.BlockSpec(inv_row.shape, lambda bi, i: (0, 0)),
            pl.BlockSpec((1, LANES), lambda bi, i: (0, 0)),
        ],
        out_specs=(tok_t(qw), tok((tk, qw)), tok_t(2 * LANES), tok((tk, LANES)),
                   tok_t(LANES), tok((tk, LANES)), tok((tk, LANES)), tok_t(GATE_ROWS)),
        compiler_params=_cparams(("parallel", "arbitrary")),
        name="kvprep",
    )(p, p, aux, pos_col, inv_row, gate_bias)


CMP_ROW_CHUNK = 128


def _select_kernel(qt_ref, kc_ref, vct_ref, ovt_ref, gt_ref, oc_ref, bias_ref, imp_ref):
    qs = qt_ref.shape[1]
    nr = kc_ref.shape[0]
    t0 = pl.program_id(2) * qs
    sees_any = t0 + lax.broadcasted_iota(jnp.int32, (1, qs), 1) >= CMP_BLOCK - 1

    def cmp_branch(rows):
        m_c = (lax.broadcasted_iota(jnp.int32, (rows, qs), 0) * CMP_STRIDE + (CMP_BLOCK - 1)
               <= t0 + lax.broadcasted_iota(jnp.int32, (rows, qs), 1))
        kc = kc_ref[0:rows, :]
        vct = vct_ref[:, 0:rows]
        ovt = ovt_ref[:, 0:rows]
        imp = jnp.zeros((LANES, qs), F32)
        for h in range(N_HPG):
            s = jnp.dot(kc, qt_ref[h * LANES:(h + 1) * LANES, :], preferred_element_type=F32)
            s = jnp.where(m_c, s, NEG)
            e = jnp.exp2(s - jnp.max(s, axis=0, keepdims=True)).astype(BF16)
            r = jnp.dot(vct, e, preferred_element_type=F32)
            inv = jnp.where(sees_any, 1.0 / r[N_HEAD_DIM:N_HEAD_DIM + 1, :], 0.0)
            imp = imp + jnp.dot(ovt, e, preferred_element_type=F32) * inv
            oc_ref[:, h * LANES:(h + 1) * LANES] = (r * (inv * gt_ref[3 * h:3 * h + 1, :])).T.astype(BF16)
        imp_ref[...] = imp

    n_var = nr // CMP_ROW_CHUNK
    visible = (t0 + qs - CMP_BLOCK) // CMP_STRIDE + 1
    need = jnp.clip((visible + CMP_ROW_CHUNK - 1) // CMP_ROW_CHUNK, 1, n_var)
    for v in range(1, n_var + 1):
        pl.when(need == v)(functools.partial(cmp_branch, v * CMP_ROW_CHUNK))
    imp = imp_ref[...]

    blk = lax.broadcasted_iota(jnp.int32, (LANES, qs), 0)
    tq = t0 + lax.broadcasted_iota(jnp.int32, (LANES, qs), 1)
    cur = tq // SEL_BLOCK
    forced = (blk == 0) | (blk == cur) | (blk == cur - 1)
    n_sel = (nr * CMP_STRIDE) // SEL_BLOCK
    free_score = jnp.where(blk * SEL_BLOCK <= tq, imp, -1.0)
    blk_f = blk.astype(F32)
    n_top = min(SEL_TOPK, n_sel)
    n_forced = 3

    def pick(score, n_iter):
        score = jnp.where(blk < n_sel, score, -jnp.inf)
        for _ in range(n_iter):
            mx = jnp.max(score, axis=0, keepdims=True)
            idx = jnp.min(jnp.where(score == mx, blk_f, float(LANES)), axis=0, keepdims=True)
            score = jnp.where(blk_f == idx, -jnp.inf, score)
        bias_t = jnp.where((score == -jnp.inf) & (blk < n_sel), 0.0, NEG)
        bias_ref[...] = bias_t.T.astype(BF16)

    all_three_forced = (t0 >= 2 * SEL_BLOCK) & (n_top >= n_forced)

    @pl.when(all_three_forced)
    def _():
        pick(jnp.where(forced, -jnp.inf, free_score), n_top - n_forced)

    @pl.when(jnp.logical_not(all_three_forced))
    def _():
        pick(jnp.where(forced, FORCE, free_score), n_top)


def _select(qrawt, kc, vct, ovt, gates_t):
    b, g, qw, t = qrawt.shape
    nr = kc.shape[2]
    qs = SELECT_QS

    def whole(rows, cols):
        return pl.BlockSpec((None, None, rows, cols), lambda bi, gi, i: (bi, gi, 0, 0))

    return pl.pallas_call(
        _select_kernel,
        out_shape=(jax.ShapeDtypeStruct((b, g, t, qw), BF16), jax.ShapeDtypeStruct((b, g, t, LANES), BF16)),
        grid=(b, g, t // qs),
        in_specs=[
            pl.BlockSpec((None, None, qw, qs), lambda bi, gi, i: (bi, gi, 0, i)),
            whole(nr, LANES), whole(LANES, nr),
            pl.BlockSpec(ovt.shape, lambda bi, gi, i: (0, 0)),
            pl.BlockSpec((None, None, GATE_ROWS, qs), lambda bi, gi, i: (bi, gi, 0, i)),
        ],
        out_specs=(pl.BlockSpec((None, None, qs, qw), lambda bi, gi, i: (bi, gi, i, 0)),
                   pl.BlockSpec((None, None, qs, LANES), lambda bi, gi, i: (bi, gi, i, 0))),
        scratch_shapes=[pltpu.VMEM((LANES, qs), F32)],
        compiler_params=_cparams(("parallel", "parallel", "parallel")),
        name="select",
    )(qrawt, kc, vct, ovt, gates_t)


def _nsa_kernel(qrope_ref, bias_ref, oc_ref, kst_ref, vs_ref, kwt_ref, vw_ref, gates_ref, nz_ref,
                yb_ref, qaug_ref, acc_ref, m_ref, s_ref):
    i = pl.program_id(1)
    t0 = i * Q_BLOCK
    Q = Q_BLOCK
    nh = N_HPG
    ng = N_KV_GROUPS
    rows = nh * Q
    t0a = pl.multiple_of(t0, Q)
    causal = (lax.broadcasted_iota(jnp.int32, (Q, Q), 1) <= lax.broadcasted_iota(jnp.int32, (Q, Q), 0))
    lane = lax.broadcasted_iota(jnp.int32, (Q, LANES), 1)

    def q_rope(g):
        return jnp.concatenate([qrope_ref[g, :, h * LANES:(h + 1) * LANES] for h in range(nh)], axis=0)

    def mask_heads(s, m):
        return jnp.concatenate([jnp.where(m, s[h * Q:(h + 1) * Q, :], NEG) for h in range(nh)], axis=0)

    first_blk = t0 // SEL_BLOCK
    for g in range(ng):
        qr = q_rope(g)
        bias = bias_ref[g]
        bias_early = jnp.where(lane < first_blk, bias, NEG).astype(BF16)
        qaug_ref[g, :, 0:LANES] = jnp.concatenate([bias_early] * nh, axis=0)
        qaug_ref[g, :, LANES:2 * LANES] = qr
        q_diag = jnp.concatenate([jnp.concatenate([bias] * nh, axis=0), qr], axis=1)
        s_d = jnp.dot(q_diag, kst_ref[g, :, pl.ds(t0a, Q)], preferred_element_type=F32)
        s_d = mask_heads(s_d, causal)
        m0 = jnp.max(s_d, axis=1, keepdims=True)
        p_d = jnp.exp2(s_d - m0)
        m_ref[g] = jnp.broadcast_to(m0, (rows, LANES))
        acc_ref[g] = jnp.dot(p_d.astype(BF16), vs_ref[g, pl.ds(t0a, Q), :], preferred_element_type=F32)

    ncb = SEL_CHUNK // LANES
    n_chunks = (t0 + SEL_CHUNK - 1) // SEL_CHUNK
    last = jnp.maximum(n_chunks - 1, 0)

    def scores(g, c, slot):
        k0 = pl.multiple_of(c * SEL_CHUNK, SEL_CHUNK)
        s_ref[slot, g] = jnp.dot(qaug_ref[g], kst_ref[g, :, pl.ds(k0, SEL_CHUNK)], preferred_element_type=F32)

    def consume(g, c, slot):
        k0 = pl.multiple_of(c * SEL_CHUNK, SEL_CHUNK)
        s = s_ref[slot, g]
        m_old = m_ref[g]
        m_new = jnp.maximum(m_old, jnp.max(s, axis=1, keepdims=True))
        p = jnp.concatenate([jnp.exp2(s[:, cb * LANES:(cb + 1) * LANES] - m_new) for cb in range(ncb)], axis=1)
        acc_ref[g] = (jnp.exp2(m_old - m_new) * acc_ref[g]
                      + jnp.dot(p.astype(BF16), vs_ref[g, pl.ds(k0, SEL_CHUNK), :], preferred_element_type=F32))
        m_ref[g] = m_new

    for g in range(ng):
        scores(g, 0, 0)

    def steps(c, n):
        for u in range(n):
            for g in range(ng):
                scores(g, jnp.minimum(c + u + 1, last), (u + 1) % 2)
                consume(g, c + u, u % 2)

    def sel_quad(c4, carry):
        steps(SEL_UNROLL * c4, SEL_UNROLL)
        return carry

    n_quads = n_chunks // SEL_UNROLL
    lax.fori_loop(0, n_quads, sel_quad, 0)
    rem = n_chunks - SEL_UNROLL * n_quads

    @pl.when(rem >= 2)
    def _():
        steps(SEL_UNROLL * n_quads, 2)

    @pl.when(rem % 2 == 1)
    def _():
        for g in range(ng):
            consume(g, n_chunks - 1, 0)

    wlen = WINDOW + Q
    ws = pl.multiple_of(jnp.maximum(t0 - WINDOW, 0), Q)
    diff = (t0 + lax.broadcasted_iota(jnp.int32, (Q, wlen), 0)) - (ws + lax.broadcasted_iota(jnp.int32, (Q, wlen), 1))
    m_w = (diff >= 0) & (diff < WINDOW)
    for g in range(ng):
        gates = gates_ref[g]

        def gate_col(br):
            return jnp.concatenate([gates[:, 3 * h + br:3 * h + br + 1] for h in range(nh)], axis=0)

        acc = acc_ref[g]
        o_s = acc * (gate_col(1) / acc[:, N_HEAD_DIM:N_HEAD_DIM + 1])
        s_w = jnp.dot(q_rope(g), kwt_ref[g, :, pl.ds(ws, wlen)], preferred_element_type=F32)
        s_w = mask_heads(s_w, m_w)
        p_w = jnp.exp2(s_w - jnp.max(s_w, axis=1, keepdims=True))
        acc_w = jnp.dot(p_w.astype(BF16), vw_ref[g, pl.ds(ws, wlen), :], preferred_element_type=F32)
        o_w = acc_w * (gate_col(2) / acc_w[:, N_HEAD_DIM:N_HEAD_DIM + 1])
        outs = []
        for h in range(nh):
            rs = slice(h * Q, (h + 1) * Q)
            outs.append(oc_ref[g, :, h * LANES:(h + 1) * LANES].astype(F32) + o_s[rs, :] + o_w[rs, :])
        for hp in range(nh // 2):
            yn = jnp.where(lane < N_HEAD_DIM, outs[2 * hp], pltpu.roll(outs[2 * hp + 1], N_HEAD_DIM, axis=1))
            ls = slice((g * (nh // 2) + hp) * LANES, (g * (nh // 2) + hp + 1) * LANES)
            yb_ref[:, ls] = (yn * _silu(nz_ref[:, ls].astype(F32))).astype(BF16)


def _nsa(p, b, t, qrope, bias, oc, kst, vs, kwt, vw, gates):
    g = N_KV_GROUPS
    nqb = t // Q_BLOCK
    qw = N_HPG * LANES
    rows = N_HPG * Q_BLOCK

    def qblk(w):
        return pl.BlockSpec((None, g, Q_BLOCK, w), lambda bi, i: (bi, 0, i, 0))

    def whole(r, c):
        return pl.BlockSpec((None, g, r, c), lambda bi, i: (bi, 0, 0, 0), pipeline_mode=pl.Buffered(1))

    return pl.pallas_call(
        _nsa_kernel,
        out_shape=jax.ShapeDtypeStruct((b * t, N_WIDTH), BF16),
        grid=(b, nqb),
        in_specs=[
            qblk(qw), qblk(LANES), qblk(qw),
            whole(2 * LANES, t), whole(t, LANES), whole(LANES, t), whole(t, LANES),
            qblk(LANES),
            pl.BlockSpec((Q_BLOCK, N_WIDTH), lambda bi, i: (bi * nqb + i, C_NZ // N_WIDTH)),
        ],
        out_specs=pl.BlockSpec((Q_BLOCK, N_WIDTH), lambda bi, i: (bi * nqb + i, 0)),
        scratch_shapes=[pltpu.VMEM((g, rows, 2 * LANES), BF16), pltpu.VMEM((g, rows, LANES), F32),
                        pltpu.VMEM((g, rows, LANES), F32), pltpu.VMEM((2, g, rows, SEL_CHUNK), F32)],
        compiler_params=_cparams(("parallel", "arbitrary")),
        name="nsa",
    )(qrope, bias, oc, kst, vs, kwt, vw, gates, p)


def _outproj_kernel(final, ya_ref, yb_ref, ga_ref, gb_ref, x_ref, wa_ref, wb_ref, wo_ref, fg_ref, o_ref):
    a = jnp.dot(ya_ref[...], wa_ref[...], preferred_element_type=F32)
    bq = jnp.dot(yb_ref[...], wb_ref[...], preferred_element_type=F32)
    merged = _sigmoid(ga_ref[...].astype(F32)) * a + _sigmoid(gb_ref[...].astype(F32)) * bq
    y = x_ref[...] + jnp.dot(merged.astype(BF16), wo_ref[...], preferred_element_type=F32)
    if final:
        y = y * lax.rsqrt(jnp.mean(y * y, axis=-1, keepdims=True) + NORM_EPS) * fg_ref[...]
    o_ref[...] = y


def _outproj(ya, yb, p, x2, wa, wb, wo, fg, final):
    m = x2.shape[0]
    tm = OUTPROJ_TM

    def full(a):
        return pl.BlockSpec(a.shape, lambda i: (0, 0))

    return pl.pallas_call(
        functools.partial(_outproj_kernel, final),
        out_shape=jax.ShapeDtypeStruct((m, D_MODEL), F32),
        grid=(m // tm,),
        in_specs=[
            pl.BlockSpec((tm, M_WIDTH), lambda i: (i, 0)),
            pl.BlockSpec((tm, N_WIDTH), lambda i: (i, 0)),
            pl.BlockSpec((tm, D_MODEL), lambda i: (i, C_GA // D_MODEL)),
            pl.BlockSpec((tm, D_MODEL), lambda i: (i, C_GB // D_MODEL)),
            pl.BlockSpec((tm, D_MODEL), lambda i: (i, 0)),
            full(wa), full(wb), full(wo), full(fg),
        ],
        out_specs=pl.BlockSpec((tm, D_MODEL), lambda i: (i, 0)),
        compiler_params=_cparams(("parallel",)),
        name="outproj",
    )(ya, yb, p, p, x2, wa, wb, wo, fg)


def _expand_cmp_w1(w1k, w1v):
    k3 = w1k.astype(BF16).reshape(CMP_BLOCK, 1, N_HEAD_DIM, 1, CMP_HIDDEN)
    v3 = w1v.astype(BF16).reshape(CMP_BLOCK, 1, N_HEAD_DIM, 1, CMP_HIDDEN)
    nc = 2 * N_KV_GROUPS
    eye = jnp.eye(nc, dtype=BF16)
    is_k = (jnp.arange(nc) < N_KV_GROUPS).astype(BF16)
    mk = (eye * is_k[:, None]).reshape(1, nc, 1, nc, 1)
    mv = (eye * (1 - is_k)[:, None]).reshape(1, nc, 1, nc, 1)
    z = k3 * mk + v3 * mv
    return z.reshape(CMP_BLOCK * nc * N_HEAD_DIM, nc * CMP_HIDDEN)


def kernel(x, positions, norm_g, w_in, conv_q_w, conv_q_b, conv_k_w, conv_k_b, b_igate, b_fgate, mh_norm_g,
           cmp_pos_k, cmp_w1_k, cmp_w2_k, cmp_pos_v, cmp_w1_v, cmp_w2_v, b_nsa_gate, w_branch_a, w_branch_b,
           w_out, final_norm_g):
    b, t, _ = x.shape
    assert t % (2 * SEL_CHUNK) == 0 and t >= WINDOW + Q_BLOCK and t // SEL_BLOCK <= LANES and t % MLSTM_CHUNK == 0
    x2 = x.reshape(b * t, D_MODEL)
    for l in range(norm_g.shape[0]):
        w_p = _wprep(jnp.swapaxes(w_in, 1, 2), l)
        p, aux = _inproj(x2, norm_g[l][None, :], w_p, w_p)

        sm_bias = jnp.zeros((1, LANES), F32).at[0, 0:M_HEADS].set(b_igate[l]).at[0, M_HEADS:2 * M_HEADS].set(b_fgate[l])
        ya = _mlstm(p, aux, b, t, conv_q_w[l], conv_q_b[l][None, :], conv_k_w[l], conv_k_b[l][None, :], sm_bias,
                    mh_norm_g[l][None, :])

        nr = t // CMP_STRIDE
        wz = _expand_cmp_w1(cmp_w1_k[l], cmp_w1_v[l])
        w2pad = lambda w: jnp.pad(w, ((0, 0), (0, LANES - N_HEAD_DIM))).astype(BF16)
        kc, vct = _compress(aux, b, t, wz, cmp_pos_k[l].reshape(1, -1), cmp_pos_v[l].reshape(1, -1),
                            cmp_w1_k[l].astype(BF16), cmp_w1_v[l].astype(BF16), w2pad(cmp_w2_k[l]), w2pad(cmp_w2_v[l]))

        half = ROPE_DIM // 2
        inv = jnp.power(jnp.float32(ROPE_THETA), -jnp.arange(half, dtype=F32) * (2.0 / ROPE_DIM))
        inv_row = inv[:, None]
        pos_col = positions.astype(F32)[:, None, :]
        gate_bias = jnp.zeros((1, LANES), F32).at[0, SM_G_OFF:SM_G_OFF + 3 * N_HEADS].set(b_nsa_gate[l])
        qrawt, qrope, kst, vs, kwt, vw, gates, gates_t = _kvprep(p, aux, b, t, pos_col, inv_row, gate_bias)

        ci = np.arange(nr)[None, :] * CMP_STRIDE
        sj = np.arange(LANES)[:, None] * SEL_BLOCK
        overlap_t = jnp.asarray((ci < sj + SEL_BLOCK) & (ci + CMP_BLOCK > sj) & (np.arange(nr)[None, :] < nr - 1)
                                & (np.arange(LANES)[:, None] < t // SEL_BLOCK), dtype=BF16)
        oc, bias = _select(qrawt, kc, vct, overlap_t, gates_t)
        yb = _nsa(p, b, t, qrope, bias, oc, kst, vs, kwt, vw, gates)

        x2 = _outproj(ya, yb, p, x2, w_branch_a[l].astype(BF16), w_branch_b[l].astype(BF16),
                      w_out[l].astype(BF16), final_norm_g[None, :], l == norm_g.shape[0] - 1)
    return x2.reshape(b, t, D_MODEL)
```
